```python
import math
import jax
import jax.numpy as jnp
from jax import lax
import numpy as np

D_MODEL = 1024
BATCH = 16
SEQ = 4096
DEPTH = 2

EPS = 1e-6
N_BRANCH = 3
GDN_HEADS = 4
GDN_DK = 128
GDN_DV = 128
GDN_CONV = 4
GDN_CHUNK = 64
GDN_KEY_W = GDN_HEADS * GDN_DK
GDN_VAL_W = GDN_HEADS * GDN_DV
HGRN_HEADS = 4
HGRN_DK = 128
HGRN_DV = 128
HGRN_CHUNK = 16
HGRN_KEY_W = HGRN_HEADS * HGRN_DK
HGRN_VAL_W = HGRN_HEADS * HGRN_DV
SSD_HEADS = 8
SSD_HEAD_DIM = 64
SSD_GROUPS = 2
SSD_HEADS_PER_GROUP = SSD_HEADS // SSD_GROUPS
SSD_STATE = 128
SSD_CONV = 4
SSD_CHUNK = 64
SSD_INNER = SSD_HEADS * SSD_HEAD_DIM
SSD_XBC_W = SSD_INNER + 2 * SSD_GROUPS * SSD_STATE
FFN_HIDDEN = 2816
FFN_CONV = 3

SPLIT_SIZES = (
    2 * GDN_KEY_W + GDN_VAL_W,
    GDN_HEADS,
    GDN_HEADS,
    GDN_VAL_W,
    HGRN_KEY_W,
    HGRN_KEY_W,
    HGRN_VAL_W,
    HGRN_VAL_W,
    SSD_INNER,
    SSD_XBC_W,
    SSD_HEADS,
    N_BRANCH * D_MODEL,
)
N_IN = sum(SPLIT_SIZES)

kernel_name = "hybrid_gdn_hgrn2_ssd_adaln_block"


def _f32(t):
    return t.astype(jnp.float32)


def rms_norm(x, w):
    xf = x.astype(jnp.float32)
    y = xf * lax.rsqrt(jnp.mean(xf * xf, axis=-1, keepdims=True) + EPS)
    return (y * w.astype(jnp.float32)).astype(x.dtype)


def l2_normalize(x):
    return x * lax.rsqrt(jnp.sum(x * x, axis=-1, keepdims=True) + EPS)


def modulate(h, shift, scale):
    return h * (1.0 + scale[:, None, :]) + shift[:, None, :]


def causal_dwconv(x, w, b=None):
    k_width = w.shape[0]
    s = x.shape[1]
    xp = jnp.pad(x, ((0, 0), (k_width - 1, 0), (0, 0)))
    y = w[k_width - 1] * x
    for k in range(k_width - 1):
        y = y + w[k] * xp[:, k:k + s]
    if b is not None:
        y = y + b
    return y


def to_chunks(t, c):
    b, s = t.shape[:2]
    t = t.reshape((b, s // c, c) + t.shape[2:])
    t = jnp.swapaxes(t, 0, 1)
    return jnp.swapaxes(t, 2, 3)


def from_chunks(t):
    t = jnp.swapaxes(jnp.swapaxes(t, 2, 3), 0, 1)
    return t.reshape((t.shape[0], t.shape[1] * t.shape[2]) + t.shape[3:])


def gated_delta_chunked(q, k, v, g, beta):
    b, s, h, dk = q.shape
    dv = v.shape[-1]
    c = GDN_CHUNK
    qc = to_chunks(q * (dk ** -0.5), c)
    kc = to_chunks(k, c)
    vc = to_chunks(v, c)
    bc = to_chunks(beta, c)
    big_g = jnp.cumsum(to_chunks(g, c), axis=-1)
    incl = jnp.tril(jnp.ones((c, c), bool))
    strict = jnp.tril(jnp.ones((c, c), bool), -1)
    diff = big_g[..., :, None] - big_g[..., None, :]
    decay = jnp.where(incl, jnp.exp(jnp.where(incl, diff, 0.0)), 0.0)
    kb = kc * bc[..., None]
    m = jnp.where(strict, jnp.einsum('nbhlk,nbhsk->nbhls', kb, kc) * decay, 0.0)
    a = m + jnp.eye(c, dtype=m.dtype)
    rhs = jnp.concatenate([vc * bc[..., None], kb * jnp.exp(big_g)[..., None]], axis=-1)
    sol = lax.linalg.triangular_solve(a, rhs, left_side=True, lower=True, unit_diagonal=True)
    u, w = sol[..., :dv], sol[..., dv:]
    attn = jnp.einsum('nbhlk,nbhsk->nbhls', qc, kc) * decay
    qg = qc * jnp.exp(big_g)[..., None]
    k_end = kc * jnp.exp(big_g[..., -1:] - big_g)[..., None]
    g_end = jnp.exp(big_g[..., -1])

    def step(state, inp):
        qg_i, k_end_i, u_i, w_i, attn_i, g_end_i = inp
        v_new = u_i - jnp.einsum('bhlk,bhkv->bhlv', w_i, state)
        o = (jnp.einsum('bhlk,bhkv->bhlv', qg_i, state)
             + jnp.einsum('bhls,bhsv->bhlv', attn_i, v_new))
        state = state * g_end_i[..., None, None] + jnp.einsum('bhsk,bhsv->bhkv', k_end_i, v_new)
        return state, o

    state0 = jnp.zeros((b, h, dk, dv), q.dtype)
    _, o = lax.scan(step, state0, (qg, k_end, u, w, attn, g_end))
    return from_chunks(o)


def hgrn2_chunked(q, k, v, logf):
    b, s, h, dk = q.shape
    dv = v.shape[-1]
    c = HGRN_CHUNK
    qc, kc, vc = to_chunks(q, c), to_chunks(k, c), to_chunks(v, c)
    big_g = jnp.cumsum(to_chunks(logf, c), axis=-2)
    g_ref = big_g[..., c // 2 - 1:c // 2, :]
    incl = jnp.tril(jnp.ones((c, c), bool))
    scores = jnp.einsum('nbhlk,nbhsk->nbhls', qc * jnp.exp(big_g - g_ref), kc * jnp.exp(g_ref - big_g))
    attn = jnp.where(incl, scores, 0.0)
    o_intra = jnp.einsum('nbhls,nbhsv->nbhlv', attn, vc)
    qg = qc * jnp.exp(big_g)
    k_end = kc * jnp.exp(big_g[..., -1:, :] - big_g)
    g_end = jnp.exp(big_g[..., -1, :])

    def step(state, inp):
        qg_i, k_end_i, v_i, g_end_i = inp
        o = jnp.einsum('bhlk,bhkv->bhlv', qg_i, state)
        state = state * g_end_i[..., None] + jnp.einsum('bhsk,bhsv->bhkv', k_end_i, v_i)
        return state, o

    state0 = jnp.zeros((b, h, dk, dv), q.dtype)
    _, o_inter = lax.scan(step, state0, (qg, k_end, vc, g_end))
    return from_chunks(o_intra + o_inter)


def ssd_chunked(xs, da, bm, cm):
    b, s, g, hg, p = xs.shape
    n_state = bm.shape[-1]
    c = SSD_CHUNK
    n = s // c
    xc = jnp.swapaxes(xs.reshape(b, n, c, g, hg, p), 0, 1)
    bc = jnp.swapaxes(bm.reshape(b, n, c, g, n_state), 0, 1)
    cc = jnp.swapaxes(cm.reshape(b, n, c, g, n_state), 0, 1)
    acs = jnp.cumsum(jnp.swapaxes(da.reshape(b, n, c, g, hg), 0, 1), axis=2)
    incl = jnp.tril(jnp.ones((c, c), bool))[:, :, None, None]
    diff = acs[:, :, :, None] - acs[:, :, None, :]
    seg = jnp.where(incl, jnp.exp(jnp.where(incl, diff, 0.0)), 0.0)
    cb = jnp.einsum('nblgd,nbsgd->nblsg', cc, bc)
    y_diag = jnp.einsum('nblsg,nblsgh,nbsghp->nblghp', cb, seg, xc)

    def step(state, inp):
        x_i, b_i, c_i, acs_i = inp
        y_off = jnp.einsum('blgd,bghpd,blgh->blghp', c_i, state, jnp.exp(acs_i))
        last = acs_i[:, -1]
        state = (state * jnp.exp(last)[..., None, None]
                 + jnp.einsum('bsgd,bsgh,bsghp->bghpd', b_i, jnp.exp(last[:, None] - acs_i), x_i))
        return state, y_off

    state0 = jnp.zeros((b, g, hg, p, n_state), xs.dtype)
    _, y_off = lax.scan(step, state0, (xc, bc, cc, acs))
    return jnp.swapaxes(y_diag + y_off, 0, 1).reshape(b, s, g, hg, p)


def gdn_branch(qkv_raw, a_raw, b_raw, z_raw, conv_w, a_log, dt_bias, norm_w):
    bsz, s, _ = qkv_raw.shape
    qkv = jax.nn.silu(causal_dwconv(qkv_raw, conv_w))
    q, k, v = jnp.split(qkv, [GDN_KEY_W, 2 * GDN_KEY_W], axis=-1)
    q = l2_normalize(q.reshape(bsz, s, GDN_HEADS, GDN_DK))
    k = l2_normalize(k.reshape(bsz, s, GDN_HEADS, GDN_DK))
    v = v.reshape(bsz, s, GDN_HEADS, GDN_DV)
    beta = jax.nn.sigmoid(b_raw)
    g = -jnp.exp(a_log) * jax.nn.softplus(a_raw + dt_bias)
    o = gated_delta_chunked(q, k, v, g, beta)
    o = rms_norm(o, norm_w) * jax.nn.silu(z_raw.reshape(bsz, s, GDN_HEADS, GDN_DV))
    return o.reshape(bsz, s, GDN_VAL_W)


def hgrn2_branch(q_raw, f_raw, i_raw, g_raw, lb, norm_w):
    bsz, s, _ = q_raw.shape
    shp = (bsz, s, HGRN_HEADS, HGRN_DK)
    q = jax.nn.silu(q_raw).reshape(shp)
    logf = jnp.log(lb + (1.0 - lb) * jax.nn.sigmoid(f_raw)).reshape(shp)
    k = ((1.0 - lb) * jax.nn.sigmoid(-f_raw)).reshape(shp)
    v = i_raw.reshape(bsz, s, HGRN_HEADS, HGRN_DV)
    o = hgrn2_chunked(q, k, v, logf)
    o = rms_norm(o, norm_w) * jax.nn.silu(g_raw.reshape(bsz, s, HGRN_HEADS, HGRN_DV))
    return o.reshape(bsz, s, HGRN_VAL_W)


def ssd_branch(z_raw, xbc_raw, dt_raw, conv_w, conv_b, a_log, dt_bias, d_skip, norm_w):
    bsz, s, _ = xbc_raw.shape
    xbc = jax.nn.silu(causal_dwconv(xbc_raw, conv_w, conv_b))
    xs, bm, cm = jnp.split(xbc, [SSD_INNER, SSD_INNER + SSD_GROUPS * SSD_STATE], axis=-1)
    xs = xs.reshape(bsz, s, SSD_GROUPS, SSD_HEADS_PER_GROUP, SSD_HEAD_DIM)
    bm = bm.reshape(bsz, s, SSD_GROUPS, SSD_STATE)
    cm = cm.reshape(bsz, s, SSD_GROUPS, SSD_STATE)
    dt = jax.nn.softplus(dt_raw + dt_bias).reshape(bsz, s, SSD_GROUPS, SSD_HEADS_PER_GROUP)
    a = -jnp.exp(a_log).reshape(SSD_GROUPS, SSD_HEADS_PER_GROUP)
    y = ssd_chunked(xs * dt[..., None], dt * a, bm, cm)
    y = y + d_skip.reshape(SSD_GROUPS, SSD_HEADS_PER_GROUP)[..., None] * xs
    group_w = SSD_HEADS_PER_GROUP * SSD_HEAD_DIM
    y = y.reshape(bsz, s, SSD_GROUPS, group_w)
    z = z_raw.reshape(bsz, s, SSD_GROUPS, group_w)
    y = rms_norm(y * jax.nn.silu(z), norm_w.reshape(SSD_GROUPS, group_w))
    return y.reshape(bsz, s, SSD_INNER)


def token_mixing(h, lb, w_in, gdn_conv_w, gdn_a_log, gdn_dt_bias, gdn_norm_w, hgrn_norm_w,
                 ssd_conv_w, ssd_conv_b, ssd_a_log, ssd_dt_bias, ssd_d, ssd_norm_w,
                 w_br_a, w_br_b, w_br_c, w_out):
    bsz, s, _ = h.shape
    dtype = h.dtype
    split_at = [int(i) for i in np.cumsum(SPLIT_SIZES)[:-1]]
    parts = jnp.split(h @ w_in, split_at, axis=-1)
    (gdn_qkv, gdn_a, gdn_b, gdn_z, hg_q, hg_f, hg_i, hg_g,
     ssd_z, ssd_xbc, ssd_dt, gate_raw) = parts
    o_a = gdn_branch(_f32(gdn_qkv), _f32(gdn_a), _f32(gdn_b), _f32(gdn_z),
                     _f32(gdn_conv_w), _f32(gdn_a_log), _f32(gdn_dt_bias), _f32(gdn_norm_w))
    o_b = hgrn2_branch(_f32(hg_q), _f32(hg_f), _f32(hg_i), _f32(hg_g), lb, _f32(hgrn_norm_w))
    o_c = ssd_branch(_f32(ssd_z), _f32(ssd_xbc), _f32(ssd_dt), _f32(ssd_conv_w), _f32(ssd_conv_b),
                     _f32(ssd_a_log), _f32(ssd_dt_bias), _f32(ssd_d), _f32(ssd_norm_w))
    gates = jax.nn.sigmoid(gate_raw).reshape(bsz, s, N_BRANCH, D_MODEL)
    merged = (gates[:, :, 0] * (o_a.astype(dtype) @ w_br_a)
              + gates[:, :, 1] * (o_b.astype(dtype) @ w_br_b)
              + gates[:, :, 2] * (o_c.astype(dtype) @ w_br_c))
    return merged @ w_out


def conv_glu_ffn(h, w_up, conv_w, conv_b, w_down):
    u = causal_dwconv(h @ w_up, conv_w, conv_b)
    gate, val = jnp.split(u, 2, axis=-1)
    return (jax.nn.silu(gate) * val) @ w_down


def _log_uniform_dt_bias(key, shape):
    lo, hi = math.log(1e-3), math.log(1e-1)
    dt = jnp.exp(jax.random.uniform(key, shape) * (hi - lo) + lo)
    return dt + jnp.log(-jnp.expm1(-dt))


def _fwd_setup_inputs(seed: int = 0) -> dict:
    key = jax.random.key(seed)
    ks = jax.random.split(key, 32)
    nrm = jax.random.normal
    d = D_MODEL
    f2 = 2 * FFN_HIDDEN
    gdn_qkv_w = 2 * GDN_KEY_W + GDN_VAL_W
    return {
        "x": nrm(ks[0], (BATCH, SEQ, d), jnp.float32),
        "c": nrm(ks[1], (BATCH, d), jnp.float32),
        "w_ada": nrm(ks[2], (DEPTH, d, 6 * d)) * (0.5 * d ** -0.5),
        "b_ada": 0.02 * nrm(ks[3], (DEPTH, 6 * d)),
        "norm1_w": 1.0 + 0.05 * nrm(ks[4], (DEPTH, d)),
        "w_in": nrm(ks[5], (DEPTH, d, N_IN)) * d ** -0.5,
        "gdn_conv_w": nrm(ks[6], (DEPTH, GDN_CONV, gdn_qkv_w)) * GDN_CONV ** -0.5,
        "gdn_a_log": jnp.log(jax.random.uniform(ks[7], (DEPTH, GDN_HEADS), minval=1.0, maxval=16.0)),
        "gdn_dt_bias": _log_uniform_dt_bias(ks[8], (DEPTH, GDN_HEADS)),
        "gdn_norm_w": 1.0 + 0.05 * nrm(ks[9], (DEPTH, GDN_DV)),
        "hgrn_lb_param": nrm(ks[10], (DEPTH, HGRN_KEY_W)),
        "hgrn_norm_w": 1.0 + 0.05 * nrm(ks[11], (DEPTH, HGRN_DV)),
        "ssd_conv_w": nrm(ks[12], (DEPTH, SSD_CONV, SSD_XBC_W)) * SSD_CONV ** -0.5,
        "ssd_conv_b": 0.02 * nrm(ks[13], (DEPTH, SSD_XBC_W)),
        "ssd_a_log": jnp.log(jax.random.uniform(ks[14], (DEPTH, SSD_HEADS), minval=1.0, maxval=16.0)),
        "ssd_dt_bias": _log_uniform_dt_bias(ks[15], (DEPTH, SSD_HEADS)),
        "ssd_d": 1.0 + 0.05 * nrm(ks[16], (DEPTH, SSD_HEADS)),
        "ssd_norm_w": 1.0 + 0.05 * nrm(ks[17], (DEPTH, SSD_INNER)),
        "w_br_a": nrm(ks[18], (DEPTH, GDN_VAL_W, d)) * GDN_VAL_W ** -0.5,
        "w_br_b": nrm(ks[19], (DEPTH, HGRN_VAL_W, d)) * HGRN_VAL_W ** -0.5,
        "w_br_c": nrm(ks[20], (DEPTH, SSD_INNER, d)) * SSD_INNER ** -0.5,
        "w_out": nrm(ks[21], (DEPTH, d, d)) * d ** -0.5,
        "norm2_w": 1.0 + 0.05 * nrm(ks[22], (DEPTH, d)),
        "ffn_w_up": nrm(ks[23], (DEPTH, d, f2)) * d ** -0.5,
        "ffn_conv_w": nrm(ks[24], (DEPTH, FFN_CONV, f2)) * FFN_CONV ** -0.5,
        "ffn_conv_b": 0.02 * nrm(ks[25], (DEPTH, f2)),
        "ffn_w_down": nrm(ks[26], (DEPTH, FFN_HIDDEN, d)) * FFN_HIDDEN ** -0.5,
        "final_norm_w": 1.0 + 0.05 * nrm(ks[27], (d,)),
    }


def _fwd_reference(x, c, w_ada, b_ada, norm1_w, w_in, gdn_conv_w, gdn_a_log, gdn_dt_bias, gdn_norm_w,
              hgrn_lb_param, hgrn_norm_w, ssd_conv_w, ssd_conv_b, ssd_a_log, ssd_dt_bias, ssd_d,
              ssd_norm_w, w_br_a, w_br_b, w_br_c, w_out, norm2_w, ffn_w_up, ffn_conv_w, ffn_conv_b,
              ffn_w_down, final_norm_w):
    c_act = jax.nn.silu(c)
    lb_soft = jax.nn.softmax(hgrn_lb_param.astype(jnp.float32), axis=0)
    lower_bounds = jnp.cumsum(lb_soft, axis=0) - lb_soft[0]
    for l in range(DEPTH):
        mod = c_act @ w_ada[l] + b_ada[l]
        shift1, scale1, gate1, shift2, scale2, gate2 = jnp.split(mod, 6, axis=-1)
        h = modulate(rms_norm(x, norm1_w[l]), shift1, scale1)
        mix = token_mixing(h, lower_bounds[l], w_in[l], gdn_conv_w[l], gdn_a_log[l], gdn_dt_bias[l],
                           gdn_norm_w[l], hgrn_norm_w[l], ssd_conv_w[l], ssd_conv_b[l], ssd_a_log[l],
                           ssd_dt_bias[l], ssd_d[l], ssd_norm_w[l], w_br_a[l], w_br_b[l], w_br_c[l],
                           w_out[l])
        x = x + gate1[:, None, :] * mix
        h = modulate(rms_norm(x, norm2_w[l]), shift2, scale2)
        x = x + gate2[:, None, :] * conv_glu_ffn(h, ffn_w_up[l], ffn_conv_w[l], ffn_conv_b[l], ffn_w_down[l])
    return rms_norm(x, final_norm_w)


import jax as _jax
import jax.numpy as _jnp

TWIN_FORMAT = 'train_step'
FWD_PARAMS = ['x', 'c', 'w_ada', 'b_ada', 'norm1_w', 'w_in', 'gdn_conv_w', 'gdn_a_log', 'gdn_dt_bias', 'gdn_norm_w', 'hgrn_lb_param', 'hgrn_norm_w', 'ssd_conv_w', 'ssd_conv_b', 'ssd_a_log', 'ssd_dt_bias', 'ssd_d', 'ssd_norm_w', 'w_br_a', 'w_br_b', 'w_br_c', 'w_out', 'norm2_w', 'ffn_w_up', 'ffn_conv_w', 'ffn_conv_b', 'ffn_w_down', 'final_norm_w']
TWIN_WEIGHTS = ['w_ada', 'b_ada', 'norm1_w', 'w_in', 'gdn_conv_w', 'gdn_a_log', 'gdn_dt_bias', 'gdn_norm_w', 'hgrn_lb_param', 'hgrn_norm_w', 'ssd_conv_w', 'ssd_conv_b', 'ssd_a_log', 'ssd_dt_bias', 'ssd_d', 'ssd_norm_w', 'w_br_a', 'w_br_b', 'w_br_c', 'w_out', 'norm2_w', 'ffn_w_up', 'ffn_conv_w', 'ffn_conv_b', 'ffn_w_down', 'final_norm_w']
TWIN_DIFF_INPUT = 'x'
TWIN_INPUTS = ['x', 'c', 'w_ada', 'b_ada', 'norm1_w', 'w_in', 'gdn_conv_w', 'gdn_a_log', 'gdn_dt_bias', 'gdn_norm_w', 'hgrn_lb_param', 'hgrn_norm_w', 'ssd_conv_w', 'ssd_conv_b', 'ssd_a_log', 'ssd_dt_bias', 'ssd_d', 'ssd_norm_w', 'w_br_a', 'w_br_b', 'w_br_c', 'w_out', 'norm2_w', 'ffn_w_up', 'ffn_conv_w', 'ffn_conv_b', 'ffn_w_down', 'final_norm_w', 'loss_target', 'm_w_ada', 'm_b_ada', 'm_norm1_w', 'm_w_in', 'm_gdn_conv_w', 'm_gdn_a_log', 'm_gdn_dt_bias', 'm_gdn_norm_w', 'm_hgrn_lb_param', 'm_hgrn_norm_w', 'm_ssd_conv_w', 'm_ssd_conv_b', 'm_ssd_a_log', 'm_ssd_dt_bias', 'm_ssd_d', 'm_ssd_norm_w', 'm_w_br_a', 'm_w_br_b', 'm_w_br_c', 'm_w_out', 'm_norm2_w', 'm_ffn_w_up', 'm_ffn_conv_w', 'm_ffn_conv_b', 'm_ffn_w_down', 'm_final_norm_w', 'v_w_ada', 'v_b_ada', 'v_norm1_w', 'v_w_in', 'v_gdn_conv_w', 'v_gdn_a_log', 'v_gdn_dt_bias', 'v_gdn_norm_w', 'v_hgrn_lb_param', 'v_hgrn_norm_w', 'v_ssd_conv_w', 'v_ssd_conv_b', 'v_ssd_a_log', 'v_ssd_dt_bias', 'v_ssd_d', 'v_ssd_norm_w', 'v_w_br_a', 'v_w_br_b', 'v_w_br_c', 'v_w_out', 'v_norm2_w', 'v_ffn_w_up', 'v_ffn_conv_w', 'v_ffn_conv_b', 'v_ffn_w_down', 'v_final_norm_w']
TWIN_OUTPUTS = ['loss', 'grad_x', 'grad_w_ada', 'grad_b_ada', 'grad_norm1_w', 'grad_w_in', 'grad_gdn_conv_w', 'grad_gdn_a_log', 'grad_gdn_dt_bias', 'grad_gdn_norm_w', 'grad_hgrn_lb_param', 'grad_hgrn_norm_w', 'grad_ssd_conv_w', 'grad_ssd_conv_b', 'grad_ssd_a_log', 'grad_ssd_dt_bias', 'grad_ssd_d', 'grad_ssd_norm_w', 'grad_w_br_a', 'grad_w_br_b', 'grad_w_br_c', 'grad_w_out', 'grad_norm2_w', 'grad_ffn_w_up', 'grad_ffn_conv_w', 'grad_ffn_conv_b', 'grad_ffn_w_down', 'grad_final_norm_w', 'delta_w_ada', 'delta_b_ada', 'delta_norm1_w', 'delta_w_in', 'delta_gdn_conv_w', 'delta_gdn_a_log', 'delta_gdn_dt_bias', 'delta_gdn_norm_w', 'delta_hgrn_lb_param', 'delta_hgrn_norm_w', 'delta_ssd_conv_w', 'delta_ssd_conv_b', 'delta_ssd_a_log', 'delta_ssd_dt_bias', 'delta_ssd_d', 'delta_ssd_norm_w', 'delta_w_br_a', 'delta_w_br_b', 'delta_w_br_c', 'delta_w_out', 'delta_norm2_w', 'delta_ffn_w_up', 'delta_ffn_conv_w', 'delta_ffn_conv_b', 'delta_ffn_w_down', 'delta_final_norm_w', 'new_m_w_ada', 'new_m_b_ada', 'new_m_norm1_w', 'new_m_w_in', 'new_m_gdn_conv_w', 'new_m_gdn_a_log', 'new_m_gdn_dt_bias', 'new_m_gdn_norm_w', 'new_m_hgrn_lb_param', 'new_m_hgrn_norm_w', 'new_m_ssd_conv_w', 'new_m_ssd_conv_b', 'new_m_ssd_a_log', 'new_m_ssd_dt_bias', 'new_m_ssd_d', 'new_m_ssd_norm_w', 'new_m_w_br_a', 'new_m_w_br_b', 'new_m_w_br_c', 'new_m_w_out', 'new_m_norm2_w', 'new_m_ffn_w_up', 'new_m_ffn_conv_w', 'new_m_ffn_conv_b', 'new_m_ffn_w_down', 'new_m_final_norm_w', 'new_v_w_ada', 'new_v_b_ada', 'new_v_norm1_w', 'new_v_w_in', 'new_v_gdn_conv_w', 'new_v_gdn_a_log', 'new_v_gdn_dt_bias', 'new_v_gdn_norm_w', 'new_v_hgrn_lb_param', 'new_v_hgrn_norm_w', 'new_v_ssd_conv_w', 'new_v_ssd_conv_b', 'new_v_ssd_a_log', 'new_v_ssd_dt_bias', 'new_v_ssd_d', 'new_v_ssd_norm_w', 'new_v_w_br_a', 'new_v_w_br_b', 'new_v_w_br_c', 'new_v_w_out', 'new_v_norm2_w', 'new_v_ffn_w_up', 'new_v_ffn_conv_w', 'new_v_ffn_conv_b', 'new_v_ffn_w_down', 'new_v_final_norm_w']
TWIN_LEAF_KINDS = {'loss': 'loss', 'grad_x': 'grad_x', 'grad_w_ada': 'grad_w', 'grad_b_ada': 'grad_w', 'grad_norm1_w': 'grad_w', 'grad_w_in': 'grad_w', 'grad_gdn_conv_w': 'grad_w', 'grad_gdn_a_log': 'grad_w', 'grad_gdn_dt_bias': 'grad_w', 'grad_gdn_norm_w': 'grad_w', 'grad_hgrn_lb_param': 'grad_w', 'grad_hgrn_norm_w': 'grad_w', 'grad_ssd_conv_w': 'grad_w', 'grad_ssd_conv_b': 'grad_w', 'grad_ssd_a_log': 'grad_w', 'grad_ssd_dt_bias': 'grad_w', 'grad_ssd_d': 'grad_w', 'grad_ssd_norm_w': 'grad_w', 'grad_w_br_a': 'grad_w', 'grad_w_br_b': 'grad_w', 'grad_w_br_c': 'grad_w', 'grad_w_out': 'grad_w', 'grad_norm2_w': 'grad_w', 'grad_ffn_w_up': 'grad_w', 'grad_ffn_conv_w': 'grad_w', 'grad_ffn_conv_b': 'grad_w', 'grad_ffn_w_down': 'grad_w', 'grad_final_norm_w': 'grad_w', 'delta_w_ada': 'delta_w', 'delta_b_ada': 'delta_w', 'delta_norm1_w': 'delta_w', 'delta_w_in': 'delta_w', 'delta_gdn_conv_w': 'delta_w', 'delta_gdn_a_log': 'delta_w', 'delta_gdn_dt_bias': 'delta_w', 'delta_gdn_norm_w': 'delta_w', 'delta_hgrn_lb_param': 'delta_w', 'delta_hgrn_norm_w': 'delta_w', 'delta_ssd_conv_w': 'delta_w', 'delta_ssd_conv_b': 'delta_w', 'delta_ssd_a_log': 'delta_w', 'delta_ssd_dt_bias': 'delta_w', 'delta_ssd_d': 'delta_w', 'delta_ssd_norm_w': 'delta_w', 'delta_w_br_a': 'delta_w', 'delta_w_br_b': 'delta_w', 'delta_w_br_c': 'delta_w', 'delta_w_out': 'delta_w', 'delta_norm2_w': 'delta_w', 'delta_ffn_w_up': 'delta_w', 'delta_ffn_conv_w': 'delta_w', 'delta_ffn_conv_b': 'delta_w', 'delta_ffn_w_down': 'delta_w', 'delta_final_norm_w': 'delta_w', 'new_m_w_ada': 'new_m', 'new_m_b_ada': 'new_m', 'new_m_norm1_w': 'new_m', 'new_m_w_in': 'new_m', 'new_m_gdn_conv_w': 'new_m', 'new_m_gdn_a_log': 'new_m', 'new_m_gdn_dt_bias': 'new_m', 'new_m_gdn_norm_w': 'new_m', 'new_m_hgrn_lb_param': 'new_m', 'new_m_hgrn_norm_w': 'new_m', 'new_m_ssd_conv_w': 'new_m', 'new_m_ssd_conv_b': 'new_m', 'new_m_ssd_a_log': 'new_m', 'new_m_ssd_dt_bias': 'new_m', 'new_m_ssd_d': 'new_m', 'new_m_ssd_norm_w': 'new_m', 'new_m_w_br_a': 'new_m', 'new_m_w_br_b': 'new_m', 'new_m_w_br_c': 'new_m', 'new_m_w_out': 'new_m', 'new_m_norm2_w': 'new_m', 'new_m_ffn_w_up': 'new_m', 'new_m_ffn_conv_w': 'new_m', 'new_m_ffn_conv_b': 'new_m', 'new_m_ffn_w_down': 'new_m', 'new_m_final_norm_w': 'new_m', 'new_v_w_ada': 'new_v', 'new_v_b_ada': 'new_v', 'new_v_norm1_w': 'new_v', 'new_v_w_in': 'new_v', 'new_v_gdn_conv_w': 'new_v', 'new_v_gdn_a_log': 'new_v', 'new_v_gdn_dt_bias': 'new_v', 'new_v_gdn_norm_w': 'new_v', 'new_v_hgrn_lb_param': 'new_v', 'new_v_hgrn_norm_w': 'new_v', 'new_v_ssd_conv_w': 'new_v', 'new_v_ssd_conv_b': 'new_v', 'new_v_ssd_a_log': 'new_v', 'new_v_ssd_dt_bias': 'new_v', 'new_v_ssd_d': 'new_v', 'new_v_ssd_norm_w': 'new_v', 'new_v_w_br_a': 'new_v', 'new_v_w_br_b': 'new_v', 'new_v_w_br_c': 'new_v', 'new_v_w_out': 'new_v', 'new_v_norm2_w': 'new_v', 'new_v_ffn_w_up': 'new_v', 'new_v_ffn_conv_w': 'new_v', 'new_v_ffn_conv_b': 'new_v', 'new_v_ffn_w_down': 'new_v', 'new_v_final_norm_w': 'new_v'}


def _forward(args):
    return _fwd_reference(*[args[k] for k in FWD_PARAMS])


def _output_shape():
    out = _jax.eval_shape(lambda: _forward(_fwd_setup_inputs(0)))
    return out.shape, out.dtype

N_MICROBATCH = 1
ADAM_LR = 0.001
ADAM_B1 = 0.9
ADAM_B2 = 0.999
ADAM_EPS = 1e-08
ADAM_WD = 0.01
ADAM_STEP = 10
PER_EXAMPLE_BATCH_AXIS = {'x': 0, 'c': 0, 'loss_target': 0}
SHARED_INPUTS = []
_WEIGHT_DTYPES = {'w_ada': _jnp.float32, 'b_ada': _jnp.float32, 'norm1_w': _jnp.float32, 'w_in': _jnp.float32, 'gdn_conv_w': _jnp.float32, 'gdn_a_log': _jnp.float32, 'gdn_dt_bias': _jnp.float32, 'gdn_norm_w': _jnp.float32, 'hgrn_lb_param': _jnp.float32, 'hgrn_norm_w': _jnp.float32, 'ssd_conv_w': _jnp.float32, 'ssd_conv_b': _jnp.float32, 'ssd_a_log': _jnp.float32, 'ssd_dt_bias': _jnp.float32, 'ssd_d': _jnp.float32, 'ssd_norm_w': _jnp.float32, 'w_br_a': _jnp.float32, 'w_br_b': _jnp.float32, 'w_br_c': _jnp.float32, 'w_out': _jnp.float32, 'norm2_w': _jnp.float32, 'ffn_w_up': _jnp.float32, 'ffn_conv_w': _jnp.float32, 'ffn_conv_b': _jnp.float32, 'ffn_w_down': _jnp.float32, 'final_norm_w': _jnp.float32}
MOMENT_SCALE = {'w_ada': 1.045307e-01, 'b_ada': 1.833872e-01, 'norm1_w': 8.600961e-02, 'w_in': 3.175894e-02, 'gdn_conv_w': 3.078100e-02, 'gdn_a_log': 2.538571e-01, 'gdn_dt_bias': 2.612085e-01, 'gdn_norm_w': 1.116074e-01, 'hgrn_lb_param': 2.712372e-03, 'hgrn_norm_w': 7.889728e-02, 'ssd_conv_w': 4.434964e-02, 'ssd_conv_b': 6.261457e-02, 'ssd_a_log': 1.703488e-01, 'ssd_dt_bias': 1.496816e-01, 'ssd_d': 3.166379e-01, 'ssd_norm_w': 6.385626e-02, 'w_br_a': 2.863717e-02, 'w_br_b': 2.846999e-02, 'w_br_c': 4.329609e-02, 'w_out': 5.903486e-02, 'norm2_w': 8.178922e-02, 'ffn_w_up': 3.368059e-02, 'ffn_conv_w': 3.374820e-02, 'ffn_conv_b': 3.121380e-02, 'ffn_w_down': 5.499465e-02, 'final_norm_w': 6.402499e+01}


def _to_microbatches(a, axis):
    t = _jnp.moveaxis(a, axis, 0)
    t = t.reshape((N_MICROBATCH, t.shape[0] // N_MICROBATCH) + t.shape[1:])
    return _jnp.moveaxis(t, 1, axis + 1)


def setup_inputs(seed: int = 0) -> dict:
    inp = _fwd_setup_inputs(seed)
    key = _jax.random.fold_in(_jax.random.key(seed), 7919)
    shape, _ = _output_shape()
    out = dict(inp)
    out["loss_target"] = _jax.random.normal(_jax.random.fold_in(key, 0), shape, _jnp.float32)
    for i, name in enumerate(TWIN_WEIGHTS):
        w = inp[name].astype(_jnp.float32)
        if MOMENT_SCALE is None:
            s = _jnp.sqrt(_jnp.mean(_jnp.square(w)) + 1e-30)
        else:
            s = MOMENT_SCALE[name]
        km, kv = _jax.random.split(_jax.random.fold_in(key, i + 1))
        out[name] = w
        out["m_" + name] = s * _jax.random.normal(km, w.shape, _jnp.float32)
        out["v_" + name] = (s * s) * _jax.random.uniform(kv, w.shape, _jnp.float32, 0.5, 1.5)
    if N_MICROBATCH > 1:
        for name, axis in PER_EXAMPLE_BATCH_AXIS.items():
            out[name] = _to_microbatches(out[name], axis)
    return {'x': out['x'], 'c': out['c'], 'w_ada': out['w_ada'], 'b_ada': out['b_ada'], 'norm1_w': out['norm1_w'], 'w_in': out['w_in'], 'gdn_conv_w': out['gdn_conv_w'], 'gdn_a_log': out['gdn_a_log'], 'gdn_dt_bias': out['gdn_dt_bias'], 'gdn_norm_w': out['gdn_norm_w'], 'hgrn_lb_param': out['hgrn_lb_param'], 'hgrn_norm_w': out['hgrn_norm_w'], 'ssd_conv_w': out['ssd_conv_w'], 'ssd_conv_b': out['ssd_conv_b'], 'ssd_a_log': out['ssd_a_log'], 'ssd_dt_bias': out['ssd_dt_bias'], 'ssd_d': out['ssd_d'], 'ssd_norm_w': out['ssd_norm_w'], 'w_br_a': out['w_br_a'], 'w_br_b': out['w_br_b'], 'w_br_c': out['w_br_c'], 'w_out': out['w_out'], 'norm2_w': out['norm2_w'], 'ffn_w_up': out['ffn_w_up'], 'ffn_conv_w': out['ffn_conv_w'], 'ffn_conv_b': out['ffn_conv_b'], 'ffn_w_down': out['ffn_w_down'], 'final_norm_w': out['final_norm_w'], 'loss_target': out['loss_target'], 'm_w_ada': out['m_w_ada'], 'm_b_ada': out['m_b_ada'], 'm_norm1_w': out['m_norm1_w'], 'm_w_in': out['m_w_in'], 'm_gdn_conv_w': out['m_gdn_conv_w'], 'm_gdn_a_log': out['m_gdn_a_log'], 'm_gdn_dt_bias': out['m_gdn_dt_bias'], 'm_gdn_norm_w': out['m_gdn_norm_w'], 'm_hgrn_lb_param': out['m_hgrn_lb_param'], 'm_hgrn_norm_w': out['m_hgrn_norm_w'], 'm_ssd_conv_w': out['m_ssd_conv_w'], 'm_ssd_conv_b': out['m_ssd_conv_b'], 'm_ssd_a_log': out['m_ssd_a_log'], 'm_ssd_dt_bias': out['m_ssd_dt_bias'], 'm_ssd_d': out['m_ssd_d'], 'm_ssd_norm_w': out['m_ssd_norm_w'], 'm_w_br_a': out['m_w_br_a'], 'm_w_br_b': out['m_w_br_b'], 'm_w_br_c': out['m_w_br_c'], 'm_w_out': out['m_w_out'], 'm_norm2_w': out['m_norm2_w'], 'm_ffn_w_up': out['m_ffn_w_up'], 'm_ffn_conv_w': out['m_ffn_conv_w'], 'm_ffn_conv_b': out['m_ffn_conv_b'], 'm_ffn_w_down': out['m_ffn_w_down'], 'm_final_norm_w': out['m_final_norm_w'], 'v_w_ada': out['v_w_ada'], 'v_b_ada': out['v_b_ada'], 'v_norm1_w': out['v_norm1_w'], 'v_w_in': out['v_w_in'], 'v_gdn_conv_w': out['v_gdn_conv_w'], 'v_gdn_a_log': out['v_gdn_a_log'], 'v_gdn_dt_bias': out['v_gdn_dt_bias'], 'v_gdn_norm_w': out['v_gdn_norm_w'], 'v_hgrn_lb_param': out['v_hgrn_lb_param'], 'v_hgrn_norm_w': out['v_hgrn_norm_w'], 'v_ssd_conv_w': out['v_ssd_conv_w'], 'v_ssd_conv_b': out['v_ssd_conv_b'], 'v_ssd_a_log': out['v_ssd_a_log'], 'v_ssd_dt_bias': out['v_ssd_dt_bias'], 'v_ssd_d': out['v_ssd_d'], 'v_ssd_norm_w': out['v_ssd_norm_w'], 'v_w_br_a': out['v_w_br_a'], 'v_w_br_b': out['v_w_br_b'], 'v_w_br_c': out['v_w_br_c'], 'v_w_out': out['v_w_out'], 'v_norm2_w': out['v_norm2_w'], 'v_ffn_w_up': out['v_ffn_w_up'], 'v_ffn_conv_w': out['v_ffn_conv_w'], 'v_ffn_conv_b': out['v_ffn_conv_b'], 'v_ffn_w_down': out['v_ffn_w_down'], 'v_final_norm_w': out['v_final_norm_w']}


def _loss(weights, diff, rest, loss_target):
    with _jax.named_scope("forward"):
        args = {**rest, TWIN_DIFF_INPUT: diff, **{k: w.astype(_WEIGHT_DTYPES[k]) for k, w in weights.items()}}
        y = _forward(args)
    with _jax.named_scope("loss_head"):
        err = _jnp.square(y.astype(_jnp.float32) - loss_target)
        return 0.5 * _jnp.sum(_jnp.mean(err, axis=-1)) if err.ndim else 0.5 * err


def _adamw(w, g, m, v):
    m = ADAM_B1 * m + (1.0 - ADAM_B1) * g
    v = ADAM_B2 * v + (1.0 - ADAM_B2) * _jnp.square(g)
    m_hat = m / (1.0 - ADAM_B1 ** ADAM_STEP)
    v_hat = v / (1.0 - ADAM_B2 ** ADAM_STEP)
    delta = -ADAM_LR * (m_hat / (_jnp.sqrt(v_hat) + ADAM_EPS) + ADAM_WD * w)
    return delta, m, v


def reference(x, c, w_ada, b_ada, norm1_w, w_in, gdn_conv_w, gdn_a_log, gdn_dt_bias, gdn_norm_w, hgrn_lb_param, hgrn_norm_w, ssd_conv_w, ssd_conv_b, ssd_a_log, ssd_dt_bias, ssd_d, ssd_norm_w, w_br_a, w_br_b, w_br_c, w_out, norm2_w, ffn_w_up, ffn_conv_w, ffn_conv_b, ffn_w_down, final_norm_w, loss_target, m_w_ada, m_b_ada, m_norm1_w, m_w_in, m_gdn_conv_w, m_gdn_a_log, m_gdn_dt_bias, m_gdn_norm_w, m_hgrn_lb_param, m_hgrn_norm_w, m_ssd_conv_w, m_ssd_conv_b, m_ssd_a_log, m_ssd_dt_bias, m_ssd_d, m_ssd_norm_w, m_w_br_a, m_w_br_b, m_w_br_c, m_w_out, m_norm2_w, m_ffn_w_up, m_ffn_conv_w, m_ffn_conv_b, m_ffn_w_down, m_final_norm_w, v_w_ada, v_b_ada, v_norm1_w, v_w_in, v_gdn_conv_w, v_gdn_a_log, v_gdn_dt_bias, v_gdn_norm_w, v_hgrn_lb_param, v_hgrn_norm_w, v_ssd_conv_w, v_ssd_conv_b, v_ssd_a_log, v_ssd_dt_bias, v_ssd_d, v_ssd_norm_w, v_w_br_a, v_w_br_b, v_w_br_c, v_w_out, v_norm2_w, v_ffn_w_up, v_ffn_conv_w, v_ffn_conv_b, v_ffn_w_down, v_final_norm_w):
    given = dict(x=x, c=c, w_ada=w_ada, b_ada=b_ada, norm1_w=norm1_w, w_in=w_in, gdn_conv_w=gdn_conv_w, gdn_a_log=gdn_a_log, gdn_dt_bias=gdn_dt_bias, gdn_norm_w=gdn_norm_w, hgrn_lb_param=hgrn_lb_param, hgrn_norm_w=hgrn_norm_w, ssd_conv_w=ssd_conv_w, ssd_conv_b=ssd_conv_b, ssd_a_log=ssd_a_log, ssd_dt_bias=ssd_dt_bias, ssd_d=ssd_d, ssd_norm_w=ssd_norm_w, w_br_a=w_br_a, w_br_b=w_br_b, w_br_c=w_br_c, w_out=w_out, norm2_w=norm2_w, ffn_w_up=ffn_w_up, ffn_conv_w=ffn_conv_w, ffn_conv_b=ffn_conv_b, ffn_w_down=ffn_w_down, final_norm_w=final_norm_w, loss_target=loss_target, m_w_ada=m_w_ada, m_b_ada=m_b_ada, m_norm1_w=m_norm1_w, m_w_in=m_w_in, m_gdn_conv_w=m_gdn_conv_w, m_gdn_a_log=m_gdn_a_log, m_gdn_dt_bias=m_gdn_dt_bias, m_gdn_norm_w=m_gdn_norm_w, m_hgrn_lb_param=m_hgrn_lb_param, m_hgrn_norm_w=m_hgrn_norm_w, m_ssd_conv_w=m_ssd_conv_w, m_ssd_conv_b=m_ssd_conv_b, m_ssd_a_log=m_ssd_a_log, m_ssd_dt_bias=m_ssd_dt_bias, m_ssd_d=m_ssd_d, m_ssd_norm_w=m_ssd_norm_w, m_w_br_a=m_w_br_a, m_w_br_b=m_w_br_b, m_w_br_c=m_w_br_c, m_w_out=m_w_out, m_norm2_w=m_norm2_w, m_ffn_w_up=m_ffn_w_up, m_ffn_conv_w=m_ffn_conv_w, m_ffn_conv_b=m_ffn_conv_b, m_ffn_w_down=m_ffn_w_down, m_final_norm_w=m_final_norm_w, v_w_ada=v_w_ada, v_b_ada=v_b_ada, v_norm1_w=v_norm1_w, v_w_in=v_w_in, v_gdn_conv_w=v_gdn_conv_w, v_gdn_a_log=v_gdn_a_log, v_gdn_dt_bias=v_gdn_dt_bias, v_gdn_norm_w=v_gdn_norm_w, v_hgrn_lb_param=v_hgrn_lb_param, v_hgrn_norm_w=v_hgrn_norm_w, v_ssd_conv_w=v_ssd_conv_w, v_ssd_conv_b=v_ssd_conv_b, v_ssd_a_log=v_ssd_a_log, v_ssd_dt_bias=v_ssd_dt_bias, v_ssd_d=v_ssd_d, v_ssd_norm_w=v_ssd_norm_w, v_w_br_a=v_w_br_a, v_w_br_b=v_w_br_b, v_w_br_c=v_w_br_c, v_w_out=v_w_out, v_norm2_w=v_norm2_w, v_ffn_w_up=v_ffn_w_up, v_ffn_conv_w=v_ffn_conv_w, v_ffn_conv_b=v_ffn_conv_b, v_ffn_w_down=v_ffn_w_down, v_final_norm_w=v_final_norm_w)
    weights = {n: given[n] for n in TWIN_WEIGHTS}
    shared = {n: given[n] for n in SHARED_INPUTS}
    per_example = {n: given[n] for n in ['x', 'c']}
    grad_fn = _jax.value_and_grad(_loss, argnums=(0, 1))

    def one_microbatch(ex, loss_target):
        ex = dict(ex)
        diff = ex.pop(TWIN_DIFF_INPUT)
        return grad_fn(weights, diff, {**shared, **ex}, loss_target)

    if N_MICROBATCH == 1:
        loss, (grad_w, grad_x) = one_microbatch(per_example, given["loss_target"])
    else:
        def body(carry, xs):
            loss_sum, grad_sum = carry
            l_k, (gw_k, gx_k) = one_microbatch(xs[0], xs[1])
            with _jax.named_scope("update"):
                return (loss_sum + l_k, _jax.tree.map(_jnp.add, grad_sum, gw_k)), gx_k

        init = (_jnp.zeros((), _jnp.float32), _jax.tree.map(_jnp.zeros_like, weights))
        (loss, grad_w), grad_x = _jax.lax.scan(body, init, (per_example, given["loss_target"]))
    with _jax.named_scope("update"):
        delta_w, new_m, new_v = {}, {}, {}
        for n in TWIN_WEIGHTS:
            delta_w[n], new_m[n], new_v[n] = _adamw(weights[n], grad_w[n], given["m_" + n], given["v_" + n])
    return (loss, grad_x, *[grad_w[n] for n in TWIN_WEIGHTS], *[delta_w[n] for n in TWIN_WEIGHTS],
            *[new_m[n] for n in TWIN_WEIGHTS], *[new_v[n] for n in TWIN_WEIGHTS])
```

```python
import functools
import math

import jax
import jax.numpy as jnp
from jax import lax
from jax.experimental import pallas as pl
from jax.experimental.pallas import tpu as pltpu

F32, BF16 = jnp.float32, jnp.bfloat16
HI = lax.Precision.HIGHEST
MESH_ID = pl.DeviceIdType.MESH

N_DEV = 8
EPS = 1e-6
D_MODEL = 1024
DEPTH = 2
GDN_HEADS, GDN_DK, GDN_CHUNK = 4, 128, 64
HGRN_HEADS, HGRN_DK, HGRN_CHUNK, HGRN_BLOCK = 4, 128, 16, 128
SSD_HEADS, SSD_HEAD_DIM, SSD_GROUPS, SSD_STATE, SSD_CHUNK = 8, 64, 2, 128, 64
SSD_INNER = SSD_HEADS * SSD_HEAD_DIM
FFN_HIDDEN = 2816
LANES = 128
P_QKV, P_GZ, P_XBC, P_GATE, P_HQ, P_HF, P_HI, P_HG, P_SZ, P_SMALL, P_WIDTH = (
    0, 1536, 2048, 3072, 6144, 6656, 7168, 7680, 8192, 8704, 8832)
SM_A, SM_B, SM_DT = 0, 4, 8
W_IN_SPLITS = (1536, 4, 4, 512, 512, 512, 512, 512, 512, 1024, 8, 3072)

ADAM_LR, ADAM_B1, ADAM_B2, ADAM_EPS, ADAM_WD, ADAM_STEP = 0.001, 0.9, 0.999, 1e-08, 0.01, 10

V7X_VMEM_LIMIT = 56 * 1024 * 1024
PACK_W = 1024
PACK_ROWS = 128


def _call(body, name, grid, in_specs, out_specs, out_shape, scratch=()):
    return pl.pallas_call(
        body, name=name, grid=grid, in_specs=in_specs, out_specs=out_specs, out_shape=out_shape,
        scratch_shapes=list(scratch),
        compiler_params=pltpu.CompilerParams(
            dimension_semantics=("arbitrary",) * len(grid), vmem_limit_bytes=V7X_VMEM_LIMIT),
    )


def _pick(n, cands):
    for c in cands:
        if n % c == 0:
            return c
    raise ValueError(f"no tile for {n} among {cands}")


def _sds(shape, dtype):
    return jax.ShapeDtypeStruct(shape, dtype)


def _dot(a, b):
    return lax.dot_general(a, b, (((1,), (0,)), ((), ())), precision=HI, preferred_element_type=F32)


def _dot_nt(a, b):
    return lax.dot_general(a, b, (((1,), (1,)), ((), ())), precision=HI, preferred_element_type=F32)


def _dot_tn(a, b):
    return lax.dot_general(a, b, (((0,), (0,)), ((), ())), precision=HI, preferred_element_type=F32)


def _iota(shape, axis):
    return lax.broadcasted_iota(jnp.int32, shape, axis)


def _silu(x):
    return x * jax.nn.sigmoid(x)


def _softplus(x):
    return jnp.maximum(x, 0.0) + jnp.log1p(jnp.exp(-jnp.abs(x)))


def _rms(x, w):
    return x * lax.rsqrt(jnp.mean(x * x, axis=-1, keepdims=True) + EPS) * w


def _lane_col(x, lane):
    m = (_iota(x.shape, 1) == lane).astype(F32)
    return jnp.sum(x * m, axis=1, keepdims=True)


def _col_to_row(c):
    n = c.shape[0]
    eye = (_iota((n, n), 0) == _iota((n, n), 1)).astype(F32)
    return jnp.sum(c * eye, axis=0, keepdims=True)


def _tril(n, strict=False):
    r, c = _iota((n, n), 0), _iota((n, n), 1)
    return (r > c) if strict else (r >= c)


def _mm(a, b, out_dtype, name):
    m, k = a.shape
    n = b.shape[1]
    tm = _pick(m, (512, 256, 128, 64, 32, 16, 8))
    tn = _pick(n, (1024, 768, 512, 384, 256, 128))
    tk = k if k <= 3072 else _pick(k, (1024, 768, 512, 384, 256, 128))
    nk = k // tk

    def body(a_ref, b_ref, o_ref, acc_ref):
        kk = pl.program_id(2)

        @pl.when(kk == 0)
        def _():
            acc_ref[...] = jnp.zeros_like(acc_ref)

        acc_ref[...] += jnp.dot(a_ref[...], b_ref[...], preferred_element_type=F32)

        @pl.when(kk == nk - 1)
        def _():
            o_ref[...] = acc_ref[...].astype(out_dtype)

    return _call(
        body, name, (m // tm, n // tn, nk),
        [pl.BlockSpec((tm, tk), lambda i, j, kk: (i, kk)), pl.BlockSpec((tk, tn), lambda i, j, kk: (kk, j))],
        pl.BlockSpec((tm, tn), lambda i, j, kk: (i, j)), _sds((m, n), out_dtype),
        scratch=[pltpu.VMEM((tm, tn), F32)],
    )(a, b)


def _ada_fwd(c8, w, b, name):
    n = w.shape[1]
    tn = 768

    def body(c_ref, w_ref, b_ref, o_ref):
        o_ref[...] = _dot(_silu(c_ref[...]), w_ref[...]) + b_ref[...]

    return _call(
        body, name, (n // tn,),
        [pl.BlockSpec((8, D_MODEL), lambda j: (0, 0)), pl.BlockSpec((D_MODEL, tn), lambda j: (0, j)),
         pl.BlockSpec((1, tn), lambda j: (0, j))],
        pl.BlockSpec((8, tn), lambda j: (0, j)), _sds((8, n), F32),
    )(c8, w, b)


def _ada_bwd(ct8, dmod8, name):
    n = dmod8.shape[1]
    tn = 768

    def body(ct_ref, dm_ref, dw_ref, db_ref):
        dm = dm_ref[...]
        dw_ref[...] = _dot(_silu(ct_ref[...]), dm)
        db_ref[...] = jnp.sum(dm, axis=0, keepdims=True)

    return _call(
        body, name, (n // tn,),
        [pl.BlockSpec((D_MODEL, 8), lambda j: (0, 0)), pl.BlockSpec((8, tn), lambda j: (0, j))],
        [pl.BlockSpec((D_MODEL, tn), lambda j: (0, j)), pl.BlockSpec((1, tn), lambda j: (0, j))],
        [_sds((D_MODEL, n), F32), _sds((1, n), F32)],
    )(ct8, dmod8)


def _lb_fn(p):
    rows = [p[l:l + 1] for l in range(DEPTH)]
    mx = functools.reduce(jnp.maximum, rows)
    es = [jnp.exp(r - mx) for r in rows]
    tot = functools.reduce(lambda a, b: a + b, es)
    sm = [e / tot for e in es]
    out, run = [], None
    for l in range(DEPTH):
        run = sm[l] if run is None else run + sm[l]
        out.append(run - sm[0])
    return jnp.concatenate(out, axis=0)


def _lb_fwd(p):
    def body(p_ref, o_ref):
        o_ref[...] = _lb_fn(p_ref[...])

    full = pl.BlockSpec(p.shape, lambda i: (0, 0))
    return _call(body, "lb_fwd", (1,), [full], full, _sds(p.shape, F32))(p)


def _lb_bwd(p, d_lower):
    def body(p_ref, d_ref, o_ref):
        _, vjp = jax.vjp(_lb_fn, p_ref[...])
        o_ref[...] = vjp(d_ref[...])[0]

    full = pl.BlockSpec(p.shape, lambda i: (0, 0))
    return _call(body, "lb_bwd", (1,), [full, full], full, _sds(p.shape, F32))(p, d_lower)


def _norm_mod_fn(x, w, shift, scale):
    return _rms(x, w) * (1.0 + scale) + shift


def _norm_mod_fwd(x, w, shift, scale, name):
    bsz, s, d = x.shape
    ts = _pick(s, (256, 128, 64, 32, 16, 8))

    def body(x_ref, w_ref, sh_ref, sc_ref, o_ref):
        o_ref[...] = _norm_mod_fn(x_ref[...], w_ref[...], sh_ref[...], sc_ref[...]).astype(BF16)

    row = pl.BlockSpec((None, ts, d), lambda b, i: (b, i, 0))
    per_b = pl.BlockSpec((None, 1, d), lambda b, i: (b, 0, 0))
    return _call(body, name, (bsz, s // ts), [row, pl.BlockSpec((1, d), lambda b, i: (0, 0)), per_b, per_b],
                 row, _sds(x.shape, BF16))(x, w, shift, scale)


def _norm_mod_bwd(x, w, shift, scale, dh, carry, name):
    bsz, s, d = x.shape
    ts = _pick(s, (256, 128, 64, 32, 16, 8))

    def body(x_ref, w_ref, sh_ref, sc_ref, dh_ref, c_ref, dx_ref, dw_ref, dsh_ref, dsc_ref):
        b, i = pl.program_id(0), pl.program_id(1)
        _, vjp = jax.vjp(_norm_mod_fn, x_ref[...], w_ref[...], sh_ref[...], sc_ref[...])
        dx, dw, dsh, dsc = vjp(dh_ref[...])
        dx_ref[...] = dx + c_ref[...]

        @pl.when((b == 0) & (i == 0))
        def _():
            dw_ref[...] = jnp.zeros_like(dw_ref)

        @pl.when(i == 0)
        def _():
            dsh_ref[...] = jnp.zeros_like(dsh_ref)
            dsc_ref[...] = jnp.zeros_like(dsc_ref)

        dw_ref[...] += dw
        dsh_ref[...] += dsh
        dsc_ref[...] += dsc

    row = pl.BlockSpec((None, ts, d), lambda b, i: (b, i, 0))
    per_b = pl.BlockSpec((None, 1, d), lambda b, i: (b, 0, 0))
    wspec = pl.BlockSpec((1, d), lambda b, i: (0, 0))
    return _call(body, name, (bsz, s // ts), [row, wspec, per_b, per_b, row, row],
                 [row, wspec, per_b, per_b],
                 [_sds(x.shape, F32), _sds((1, d), F32), _sds((bsz, 1, d), F32), _sds((bsz, 1, d), F32)],
                 )(x, w, shift, scale, dh, carry)


def _resid_fwd(x, y, gate, name):
    bsz, s, d = x.shape
    ts = _pick(s, (512, 256, 128, 64, 32, 16, 8))

    def body(x_ref, y_ref, g_ref, o_ref):
        o_ref[...] = x_ref[...] + g_ref[...] * y_ref[...]

    row = pl.BlockSpec((None, ts, d), lambda b, i: (b, i, 0))
    per_b = pl.BlockSpec((None, 1, d), lambda b, i: (b, 0, 0))
    return _call(body, name, (bsz, s // ts), [row, row, per_b], row, _sds(x.shape, F32))(x, y, gate)


def _gate_bwd(dx, y, gate, name):
    bsz, s, d = dx.shape
    ts = _pick(s, (512, 256, 128, 64, 32, 16, 8))

    def body(dx_ref, y_ref, g_ref, dy_ref, dg_ref):
        dxv = dx_ref[...]
        dy_ref[...] = (dxv * g_ref[...]).astype(BF16)

        @pl.when(pl.program_id(1) == 0)
        def _():
            dg_ref[...] = jnp.zeros_like(dg_ref)

        dg_ref[...] += jnp.sum(dxv * y_ref[...], axis=0, keepdims=True)

    row = pl.BlockSpec((None, ts, d), lambda b, i: (b, i, 0))
    per_b = pl.BlockSpec((None, 1, d), lambda b, i: (b, 0, 0))
    return _call(body, name, (bsz, s // ts), [row, row, per_b], [row, per_b],
                 [_sds(dx.shape, BF16), _sds((bsz, 1, d), F32)])(dx, y, gate)


HALO = 8


def _conv_pre(xx, w_ref, b_ref, kw, rows):
    acc = w_ref[kw - 1:kw, :] * xx[HALO:HALO + rows]
    for k in range(kw - 1):
        acc = acc + w_ref[k:k + 1, :] * pltpu.roll(xx, kw - 1 - k, 0)[HALO:HALO + rows]
    return acc + b_ref[...]


def _conv_fwd(x, col0, width, w, b, act, name):
    bsz, s, _ = x.shape
    kw = w.shape[0]
    ts = _pick(s, (512, 256, 128, 64, 32, 16, 8))
    tc = _pick(width, (512, 256, 128))
    assert col0 % tc == 0
    c0 = col0 // tc
    hb = ts // HALO

    def body(x_ref, xp_ref, w_ref, b_ref, o_ref):
        i = pl.program_id(1)
        xp = jnp.where(i > 0, xp_ref[...], 0.0)
        xx = jnp.concatenate([xp, x_ref[...]], axis=0)
        pre = _conv_pre(xx, w_ref, b_ref, kw, ts)
        o_ref[...] = _silu(pre) if act else pre

    return _call(
        body, name, (bsz, s // ts, width // tc),
        [pl.BlockSpec((None, ts, tc), lambda bb, i, j: (bb, i, c0 + j)),
         pl.BlockSpec((None, HALO, tc), lambda bb, i, j: (bb, jnp.maximum(i * hb - 1, 0), c0 + j)),
         pl.BlockSpec((kw, tc), lambda bb, i, j: (0, j)), pl.BlockSpec((1, tc), lambda bb, i, j: (0, j))],
        pl.BlockSpec((None, ts, tc), lambda bb, i, j: (bb, i, j)), _sds((bsz, s, width), F32),
    )(x, x, w, b)


def _conv_bwd(dy, x, col0, width, w, b, act, name):
    bsz, s, _ = x.shape
    kw = w.shape[0]
    ts = _pick(s, (512, 256, 128, 64, 32, 16, 8))
    tc = _pick(width, (512, 256, 128))
    c0 = col0 // tc
    hb = ts // HALO
    nt = s // ts
    last_h = s // HALO - 1

    def body(x_ref, xp_ref, xn_ref, dy_ref, dyn_ref, w_ref, b_ref, dx_ref, dw_ref, db_ref):
        bb, i = pl.program_id(1), pl.program_id(2)
        xp = jnp.where(i > 0, xp_ref[...], 0.0)
        xx = jnp.concatenate([xp, x_ref[...], xn_ref[...]], axis=0)
        dyy = jnp.concatenate([dy_ref[...], jnp.where(i < nt - 1, dyn_ref[...], 0.0)], axis=0)
        n = ts + HALO
        if act:
            pre = _conv_pre(xx, w_ref, b_ref, kw, n)
            sg = jax.nn.sigmoid(pre)
            dpre = dyy * (sg * (1.0 + pre * (1.0 - sg)))
        else:
            dpre = dyy
        dx = w_ref[kw - 1:kw, :] * dpre[:ts]
        for k in range(kw - 1):
            dx = dx + w_ref[k:k + 1, :] * pltpu.roll(dpre, n - (kw - 1 - k), 0)[:ts]
        dx_ref[...] = dx.astype(BF16)

        @pl.when((bb == 0) & (i == 0))
        def _():
            dw_ref[...] = jnp.zeros_like(dw_ref)
            db_ref[...] = jnp.zeros_like(db_ref)

        dt = dpre[:ts]
        db_ref[...] += jnp.sum(dt, axis=0, keepdims=True)
        dw_ref[kw - 1:kw, :] += jnp.sum(dt * xx[HALO:HALO + ts], axis=0, keepdims=True)
        for k in range(kw - 1):
            xs = pltpu.roll(xx, kw - 1 - k, 0)[HALO:HALO + ts]
            dw_ref[k:k + 1, :] += jnp.sum(dt * xs, axis=0, keepdims=True)

    xspec = lambda f: pl.BlockSpec((None, HALO, tc), f)
    return _call(
        body, name, (width // tc, bsz, nt),
        [pl.BlockSpec((None, ts, tc), lambda j, bb, i: (bb, i, c0 + j)),
         xspec(lambda j, bb, i: (bb, jnp.maximum(i * hb - 1, 0), c0 + j)),
         xspec(lambda j, bb, i: (bb, jnp.minimum((i + 1) * hb, last_h), c0 + j)),
         pl.BlockSpec((None, ts, tc), lambda j, bb, i: (bb, i, j)),
         xspec(lambda j, bb, i: (bb, jnp.minimum((i + 1) * hb, last_h), j)),
         pl.BlockSpec((kw, tc), lambda j, bb, i: (0, j)), pl.BlockSpec((1, tc), lambda j, bb, i: (0, j))],
        [pl.BlockSpec((None, ts, tc), lambda j, bb, i: (bb, i, j)),
         pl.BlockSpec((kw, tc), lambda j, bb, i: (0, j)), pl.BlockSpec((1, tc), lambda j, bb, i: (0, j))],
        [_sds((bsz, s, width), BF16), _sds((kw, width), F32), _sds((1, width), F32)],
    )(x, x, x, dy, dy, w, b)


def _glu_fwd(u, name):
    bsz, s, f2 = u.shape
    f = f2 // 2
    ts = _pick(s, (512, 256, 128, 64, 32, 16, 8))
    tc = _pick(f, (256, 128))
    nf = f // tc

    def body(g_ref, v_ref, o_ref):
        o_ref[...] = (_silu(g_ref[...]) * v_ref[...]).astype(BF16)

    return _call(
        body, name, (bsz, s // ts, nf),
        [pl.BlockSpec((None, ts, tc), lambda b, i, j: (b, i, j)),
         pl.BlockSpec((None, ts, tc), lambda b, i, j: (b, i, nf + j))],
        pl.BlockSpec((None, ts, tc), lambda b, i, j: (b, i, j)), _sds((bsz, s, f), BF16),
    )(u, u)


def _glu_bwd(da, u, name):
    bsz, s, f2 = u.shape
    f = f2 // 2
    ts = _pick(s, (512, 256, 128, 64, 32, 16, 8))
    tc = _pick(f, (256, 128))
    nf = f // tc

    def body(da_ref, g_ref, v_ref, o_ref):
        j = pl.program_id(2)
        g, v, d = g_ref[...], v_ref[...], da_ref[...]
        sg = jax.nn.sigmoid(g)
        dg = d * v * (sg * (1.0 + g * (1.0 - sg)))
        dv = d * (g * sg)
        o_ref[...] = jnp.where(j < nf, dg, dv)

    return _call(
        body, name, (bsz, s // ts, 2 * nf),
        [pl.BlockSpec((None, ts, tc), lambda b, i, j: (b, i, j % nf)),
         pl.BlockSpec((None, ts, tc), lambda b, i, j: (b, i, j % nf)),
         pl.BlockSpec((None, ts, tc), lambda b, i, j: (b, i, nf + j % nf))],
        pl.BlockSpec((None, ts, tc), lambda b, i, j: (b, i, j)), _sds((bsz, s, f2), F32),
    )(da, u, u)


def _merge_fwd(p, oa, ob, oc, wa, wb, wc, name):
    bsz, s, _ = p.shape
    tm = _pick(s, (256, 128, 64, 32, 16, 8))
    gblk = P_GATE // (3 * D_MODEL)

    def body(g_ref, oa_ref, ob_ref, oc_ref, wa_ref, wb_ref, wc_ref, o_ref):
        acc = None
        for i, (o_r, w_r) in enumerate(((oa_ref, wa_ref), (ob_ref, wb_ref), (oc_ref, wc_ref))):
            y = jnp.dot(o_r[...], w_r[...], preferred_element_type=F32)
            t = jax.nn.sigmoid(g_ref[:, i * D_MODEL:(i + 1) * D_MODEL]) * y
            acc = t if acc is None else acc + t
        o_ref[...] = acc.astype(BF16)

    orow = pl.BlockSpec((None, tm, 512), lambda b, i: (b, i, 0))
    wfull = pl.BlockSpec((512, D_MODEL), lambda b, i: (0, 0))
    return _call(
        body, name, (bsz, s // tm),
        [pl.BlockSpec((None, tm, 3 * D_MODEL), lambda b, i: (b, i, gblk)), orow, orow, orow, wfull, wfull, wfull],
        pl.BlockSpec((None, tm, D_MODEL), lambda b, i: (b, i, 0)), _sds((bsz, s, D_MODEL), BF16),
    )(p, oa, ob, oc, wa, wb, wc)


def _merge_bwd(dm, p, oa, ob, oc, wa, wb, wc, wat, wbt, wct, name):
    bsz, s, _ = p.shape
    tm = _pick(s, (256, 128, 64, 32, 16, 8))
    gblk = P_GATE // (3 * D_MODEL)

    def body(dm_ref, g_ref, oa_ref, ob_ref, oc_ref, wa_ref, wb_ref, wc_ref, wat_ref, wbt_ref, wct_ref,
             dg_ref, doa_ref, dob_ref, doc_ref, dya_ref, dyb_ref, dyc_ref):
        dmv = dm_ref[...]
        trip = ((oa_ref, wa_ref, wat_ref, doa_ref, dya_ref), (ob_ref, wb_ref, wbt_ref, dob_ref, dyb_ref),
                (oc_ref, wc_ref, wct_ref, doc_ref, dyc_ref))
        for i, (o_r, w_r, wt_r, do_r, dy_r) in enumerate(trip):
            y = jnp.dot(o_r[...], w_r[...], preferred_element_type=F32)
            sg = jax.nn.sigmoid(g_ref[:, i * D_MODEL:(i + 1) * D_MODEL])
            dg_ref[:, i * D_MODEL:(i + 1) * D_MODEL] = (dmv * y * sg * (1.0 - sg)).astype(BF16)
            dy = (dmv * sg).astype(BF16)
            dy_r[...] = dy
            do_r[...] = jnp.dot(dy, wt_r[...], preferred_element_type=F32)

    orow = pl.BlockSpec((None, tm, 512), lambda b, i: (b, i, 0))
    drow = pl.BlockSpec((None, tm, D_MODEL), lambda b, i: (b, i, 0))
    grow = pl.BlockSpec((None, tm, 3 * D_MODEL), lambda b, i: (b, i, 0))
    wfull = pl.BlockSpec((512, D_MODEL), lambda b, i: (0, 0))
    wtfull = pl.BlockSpec((D_MODEL, 512), lambda b, i: (0, 0))
    return _call(
        body, name, (bsz, s // tm),
        [drow, pl.BlockSpec((None, tm, 3 * D_MODEL), lambda b, i: (b, i, gblk)), orow, orow, orow,
         wfull, wfull, wfull, wtfull, wtfull, wtfull],
        [grow, orow, orow, orow, drow, drow, drow],
        [_sds((bsz, s, 3 * D_MODEL), BF16)] + [_sds((bsz, s, 512), F32)] * 3 + [_sds((bsz, s, D_MODEL), BF16)] * 3,
    )(dm, p, oa, ob, oc, wa, wb, wc, wat, wbt, wct)


def _final_loss(x, w, target):
    bsz, s, d = x.shape
    ts = _pick(s, (256, 128, 64, 32, 16, 8))

    def body(x_ref, w_ref, t_ref, loss_ref, dx_ref, dw_ref):
        first = (pl.program_id(0) == 0) & (pl.program_id(1) == 0)
        y, vjp = jax.vjp(_rms, x_ref[...], w_ref[...])
        err = y - t_ref[...]
        dx, dw = vjp(err * (1.0 / d))
        dx_ref[...] = dx

        @pl.when(first)
        def _():
            loss_ref[...] = jnp.zeros_like(loss_ref)
            dw_ref[...] = jnp.zeros_like(dw_ref)

        loss_ref[...] += 0.5 * jnp.sum(jnp.sum(err * err, axis=1, keepdims=True), axis=0, keepdims=True) * (1.0 / d)
        dw_ref[...] += dw

    row = pl.BlockSpec((None, ts, d), lambda b, i: (b, i, 0))
    wspec = pl.BlockSpec((1, d), lambda b, i: (0, 0))
    return _call(body, "final_loss", (bsz, s // ts), [row, wspec, row],
                 [pl.BlockSpec((8, LANES), lambda b, i: (0, 0)), row, wspec],
                 [_sds((8, LANES), F32), _sds(x.shape, F32), _sds((1, d), F32)])(x, w, target)


def _unit_lower_inverse(m):
    n = m.shape[0]
    eye = (_iota((n, n), 0) == _iota((n, n), 1)).astype(F32)
    p = -m
    x = eye + p
    for _ in range(int(math.log2(n)) - 1):
        p = _dot(p, p)
        x = x + _dot(x, p)
    return x


def _gdn_chunk(states, qkv, small, z, a_row, dt_row, nw):
    c = qkv.shape[0]
    kw = GDN_HEADS * GDN_DK
    g_all = -jnp.exp(a_row) * _softplus(small + dt_row)
    beta_all = jax.nn.sigmoid(small)
    incl, strict = _tril(c), _tril(c, True)
    big_g_all = _dot(incl.astype(F32), g_all)
    outs, new_states = [], []
    for h in range(GDN_HEADS):
        sl = slice(h * GDN_DK, (h + 1) * GDN_DK)
        q = qkv[:, sl]
        k = qkv[:, kw + h * GDN_DK:kw + (h + 1) * GDN_DK]
        v = qkv[:, 2 * kw + h * GDN_DK:2 * kw + (h + 1) * GDN_DK]
        q = q * lax.rsqrt(jnp.sum(q * q, axis=-1, keepdims=True) + EPS) * (GDN_DK ** -0.5)
        k = k * lax.rsqrt(jnp.sum(k * k, axis=-1, keepdims=True) + EPS)
        gc = _lane_col(big_g_all, SM_A + h)
        bc = _lane_col(beta_all, SM_B + h)
        g_last = jnp.sum(_lane_col(g_all, SM_A + h), axis=0, keepdims=True)
        diff = gc - _col_to_row(gc)
        decay = jnp.where(incl, jnp.exp(jnp.where(incl, diff, 0.0)), 0.0)
        kb = k * bc
        m = jnp.where(strict, _dot_nt(kb, k) * decay, 0.0)
        tinv = _unit_lower_inverse(m)
        eg = jnp.exp(gc)
        u = _dot(tinv, v * bc)
        w = _dot(tinv, kb * eg)
        attn = _dot_nt(q, k) * decay
        st = states[h]
        v_new = u - _dot(w, st)
        o = _dot(q * eg, st) + _dot(attn, v_new)
        new_states.append(st * jnp.exp(g_last) + _dot_tn(k * jnp.exp(g_last - gc), v_new))
        outs.append(_rms(o, nw) * _silu(z[:, sl]))
    return new_states, jnp.concatenate(outs, axis=1)


def _gdn_specs(c):
    row = lambda w, blk: pl.BlockSpec((None, c, w), lambda b, n, blk=blk: (b, n, blk))
    prm = pl.BlockSpec((1, LANES), lambda b, n: (0, 0))
    return [row(1536, 0), row(LANES, P_SMALL // LANES), row(512, P_GZ // 512), prm, prm, prm]


def _state_spec():
    return pl.BlockSpec((None, None, 4, LANES, LANES), lambda b, n: (b, n, 0, 0, 0))


def _gdn_fwd(qkv_act, p, a_row, dt_row, nw, name):
    bsz, s, _ = qkv_act.shape
    c = GDN_CHUNK
    nc = s // c

    def body(qkv_ref, sm_ref, z_ref, a_ref, dt_ref, nw_ref, o_ref, st_ref, st_scr):
        @pl.when(pl.program_id(1) == 0)
        def _():
            st_scr[...] = jnp.zeros_like(st_scr)

        st_ref[...] = st_scr[...]
        states = [st_scr[h] for h in range(GDN_HEADS)]
        new_states, o = _gdn_chunk(states, qkv_ref[...], sm_ref[...], z_ref[...], a_ref[...], dt_ref[...], nw_ref[...])
        for h in range(GDN_HEADS):
            st_scr[h] = new_states[h]
        o_ref[...] = o.astype(BF16)

    return _call(
        body, name, (bsz, nc), _gdn_specs(c),
        [pl.BlockSpec((None, c, 512), lambda b, n: (b, n, 0)), _state_spec()],
        [_sds((bsz, s, 512), BF16), _sds((bsz, nc, 4, LANES, LANES), F32)],
        scratch=[pltpu.VMEM((4, LANES, LANES), F32)],
    )(qkv_act, p, p, a_row, dt_row, nw)


def _gdn_bwd(do, qkv_act, p, a_row, dt_row, nw, st_all, name):
    bsz, s, _ = qkv_act.shape
    c = GDN_CHUNK
    nc = s // c

    def body(qkv_ref, sm_ref, z_ref, a_ref, dt_ref, nw_ref, do_ref, st_ref,
             dqkv_ref, dsm_ref, dz_ref, da_ref, ddt_ref, dnw_ref, ds_scr):
        first = (pl.program_id(0) == 0) & (pl.program_id(1) == 0)

        @pl.when(pl.program_id(1) == 0)
        def _():
            ds_scr[...] = jnp.zeros_like(ds_scr)

        @pl.when(first)
        def _():
            da_ref[...] = jnp.zeros_like(da_ref)
            ddt_ref[...] = jnp.zeros_like(ddt_ref)
            dnw_ref[...] = jnp.zeros_like(dnw_ref)

        states = [st_ref[h] for h in range(GDN_HEADS)]
        _, vjp = jax.vjp(_gdn_chunk, states, qkv_ref[...], sm_ref[...], z_ref[...], a_ref[...], dt_ref[...], nw_ref[...])
        d_states, dqkv, dsm, dz, da, ddt, dnw = vjp(([ds_scr[h] for h in range(GDN_HEADS)], do_ref[...]))
        for h in range(GDN_HEADS):
            ds_scr[h] = d_states[h]
        dqkv_ref[...] = dqkv
        dsm_ref[...] = dsm
        dz_ref[...] = dz.astype(BF16)
        da_ref[...] += da
        ddt_ref[...] += ddt
        dnw_ref[...] += dnw

    rrow = lambda w, blk: pl.BlockSpec((None, c, w), lambda b, n, blk=blk: (b, nc - 1 - n, blk))
    prm = pl.BlockSpec((1, LANES), lambda b, n: (0, 0))
    return _call(
        body, name, (bsz, nc),
        [rrow(1536, 0), rrow(LANES, P_SMALL // LANES), rrow(512, P_GZ // 512), prm, prm, prm, rrow(512, 0),
         pl.BlockSpec((None, None, 4, LANES, LANES), lambda b, n: (b, nc - 1 - n, 0, 0, 0))],
        [rrow(1536, 0), rrow(LANES, 0), rrow(512, 0), prm, prm, prm],
        [_sds((bsz, s, 1536), F32), _sds((bsz, s, LANES), F32), _sds((bsz, s, 512), BF16)] + [_sds((1, LANES), F32)] * 3,
        scratch=[pltpu.VMEM((4, LANES, LANES), F32)],
    )(qkv_act, p, p, a_row, dt_row, nw, do, st_all)


def _hgrn_block(states, q_raw, f_raw, i_raw, g_raw, lb, nw):
    n = q_raw.shape[0]
    c = HGRN_CHUNK
    r, cc = _iota((n, n), 0), _iota((n, n), 1)
    same = (r // c) == (cc // c)
    run_sum = (same & (r >= cc)).astype(F32)
    ref_pick = (cc == (r // c) * c + (c // 2 - 1)).astype(F32)
    tot_sum = same.astype(F32)
    causal = same & (r >= cc)
    outs, new_states = [], []
    for h in range(HGRN_HEADS):
        sl = slice(h * HGRN_DK, (h + 1) * HGRN_DK)
        fr, lbh = f_raw[:, sl], lb[:, sl]
        q = _silu(q_raw[:, sl])
        logf = jnp.log(lbh + (1.0 - lbh) * jax.nn.sigmoid(fr))
        k = (1.0 - lbh) * jax.nn.sigmoid(-fr)
        v = i_raw[:, sl]
        big_g = _dot(run_sum, logf)
        g_rel = big_g - _dot(ref_pick, big_g)
        g_tot = _dot(tot_sum, logf)
        scores = _dot_nt(q * jnp.exp(g_rel), k * jnp.exp(-g_rel))
        o_intra = _dot(jnp.where(causal, scores, 0.0), v)
        qg = q * jnp.exp(big_g)
        k_end = k * jnp.exp(g_tot - big_g)
        st = states[h]
        parts = []
        for j in range(n // c):
            rows = slice(j * c, (j + 1) * c)
            parts.append(_dot_nt(qg[rows], st))
            st = st * jnp.exp(g_tot[j * c:j * c + 1]) + _dot_tn(v[rows], k_end[rows])
        new_states.append(st)
        o = o_intra + jnp.concatenate(parts, axis=0)
        outs.append(_rms(o, nw) * _silu(g_raw[:, sl]))
    return new_states, jnp.concatenate(outs, axis=1)


def _hgrn_fwd(p, lb, nw, name):
    bsz, s, _ = p.shape
    n = HGRN_BLOCK
    nb = s // n

    def body(q_ref, f_ref, i_ref, g_ref, lb_ref, nw_ref, o_ref, st_ref, st_scr):
        @pl.when(pl.program_id(1) == 0)
        def _():
            st_scr[...] = jnp.zeros_like(st_scr)

        st_ref[...] = st_scr[...]
        states = [st_scr[h] for h in range(HGRN_HEADS)]
        new_states, o = _hgrn_block(states, q_ref[...], f_ref[...], i_ref[...], g_ref[...], lb_ref[...], nw_ref[...])
        for h in range(HGRN_HEADS):
            st_scr[h] = new_states[h]
        o_ref[...] = o.astype(BF16)

    row = lambda blk: pl.BlockSpec((None, n, 512), lambda b, i, blk=blk: (b, i, blk))
    return _call(
        body, name, (bsz, nb),
        [row(P_HQ // 512), row(P_HF // 512), row(P_HI // 512), row(P_HG // 512),
         pl.BlockSpec((1, 512), lambda b, i: (0, 0)), pl.BlockSpec((1, LANES), lambda b, i: (0, 0))],
        [row(0), _state_spec()],
        [_sds((bsz, s, 512), BF16), _sds((bsz, nb, 4, LANES, LANES), F32)],
        scratch=[pltpu.VMEM((4, LANES, LANES), F32)],
    )(p, p, p, p, lb, nw)


def _hgrn_bwd(do, p, lb, nw, st_all, name):
    bsz, s, _ = p.shape
    n = HGRN_BLOCK
    nb = s // n

    def body(q_ref, f_ref, i_ref, g_ref, lb_ref, nw_ref, do_ref, st_ref, dp_ref, dlb_ref, dnw_ref, ds_scr):
        first = (pl.program_id(0) == 0) & (pl.program_id(1) == 0)

        @pl.when(pl.program_id(1) == 0)
        def _():
            ds_scr[...] = jnp.zeros_like(ds_scr)

        @pl.when(first)
        def _():
            dlb_ref[...] = jnp.zeros_like(dlb_ref)
            dnw_ref[...] = jnp.zeros_like(dnw_ref)

        states = [st_ref[h] for h in range(HGRN_HEADS)]
        _, vjp = jax.vjp(_hgrn_block, states, q_ref[...], f_ref[...], i_ref[...], g_ref[...], lb_ref[...], nw_ref[...])
        d_states, dq, df, di, dg, dlb, dnw = vjp(([ds_scr[h] for h in range(HGRN_HEADS)], do_ref[...]))
        for h in range(HGRN_HEADS):
            ds_scr[h] = d_states[h]
        for j, t in enumerate((dq, df, di, dg)):
            dp_ref[:, j * 512:(j + 1) * 512] = t.astype(BF16)
        dlb_ref[...] += dlb
        dnw_ref[...] += dnw

    row = lambda blk: pl.BlockSpec((None, n, 512), lambda b, i, blk=blk: (b, nb - 1 - i, blk))
    return _call(
        body, name, (bsz, nb),
        [row(P_HQ // 512), row(P_HF // 512), row(P_HI // 512), row(P_HG // 512),
         pl.BlockSpec((1, 512), lambda b, i: (0, 0)), pl.BlockSpec((1, LANES), lambda b, i: (0, 0)), row(0),
         pl.BlockSpec((None, None, 4, LANES, LANES), lambda b, i: (b, nb - 1 - i, 0, 0, 0))],
        [pl.BlockSpec((None, n, 2048), lambda b, i: (b, nb - 1 - i, 0)),
         pl.BlockSpec((1, 512), lambda b, i: (0, 0)), pl.BlockSpec((1, LANES), lambda b, i: (0, 0))],
        [_sds((bsz, s, 2048), BF16), _sds((1, 512), F32), _sds((1, LANES), F32)],
        scratch=[pltpu.VMEM((4, LANES, LANES), F32)],
    )(p, p, p, p, lb, nw, do, st_all)


def _ssd_chunk(states, xbc, small, z, a_row, dt_row, d_row, nw):
    c = xbc.shape[0]
    incl = _tril(c)
    dt_all = _softplus(small + dt_row)
    da_all = dt_all * (-jnp.exp(a_row))
    spread = (_iota((LANES, SSD_INNER), 0) == SM_DT + _iota((LANES, SSD_INNER), 1) // SSD_HEAD_DIM).astype(F32)
    dt_e = _dot(dt_all, spread)
    da_e = _dot(da_all, spread)
    acs_e = _dot(incl.astype(F32), da_e)
    last_e = jnp.sum(da_e, axis=0, keepdims=True)
    xs = xbc[:, :SSD_INNER]
    xdt = xs * dt_e
    gw = SSD_GROUPS * SSD_STATE
    lane = _iota((1, LANES), 1)
    ys, new_states = [], []
    for j in range(4):
        g = j // 2
        bg = xbc[:, SSD_INNER + g * SSD_STATE:SSD_INNER + (g + 1) * SSD_STATE]
        cg = xbc[:, SSD_INNER + gw + g * SSD_STATE:SSD_INNER + gw + (g + 1) * SSD_STATE]
        cb = _dot_nt(cg, bg)
        sl = slice(j * LANES, (j + 1) * LANES)
        xblk, acs, last = xdt[:, sl], acs_e[:, sl], last_e[:, sl]
        y = None
        for sub in range(2):
            ac = acs[:, sub * SSD_HEAD_DIM:sub * SSD_HEAD_DIM + 1]
            seg = jnp.where(incl, jnp.exp(jnp.where(incl, ac - _col_to_row(ac), 0.0)), 0.0)
            mine = ((lane // SSD_HEAD_DIM) == sub).astype(F32)
            t = _dot(cb * seg, xblk * mine)
            y = t if y is None else y + t
        st = states[j]
        y = y + _dot(cg, st) * jnp.exp(acs)
        new_states.append(st * jnp.exp(last) + _dot_tn(bg, xblk * jnp.exp(last - acs)))
        ys.append(y + d_row[:, sl] * xs[:, sl])
    yz = jnp.concatenate(ys, axis=1) * _silu(z)
    gwid = SSD_INNER // SSD_GROUPS
    outs = [_rms(yz[:, g * gwid:(g + 1) * gwid], nw[:, g * gwid:(g + 1) * gwid]) for g in range(SSD_GROUPS)]
    return new_states, jnp.concatenate(outs, axis=1)


def _ssd_fwd(xbc_act, p, a_row, dt_row, d_row, nw, name):
    bsz, s, _ = xbc_act.shape
    c = SSD_CHUNK
    nc = s // c

    def body(x_ref, sm_ref, z_ref, a_ref, dt_ref, d_ref, nw_ref, o_ref, st_ref, st_scr):
        @pl.when(pl.program_id(1) == 0)
        def _():
            st_scr[...] = jnp.zeros_like(st_scr)

        st_ref[...] = st_scr[...]
        states = [st_scr[h] for h in range(4)]
        new_states, o = _ssd_chunk(states, x_ref[...], sm_ref[...], z_ref[...], a_ref[...], dt_ref[...], d_ref[...], nw_ref[...])
        for h in range(4):
            st_scr[h] = new_states[h]
        o_ref[...] = o.astype(BF16)

    row = lambda w, blk: pl.BlockSpec((None, c, w), lambda b, n, blk=blk: (b, n, blk))
    prm = pl.BlockSpec((1, LANES), lambda b, n: (0, 0))
    prm5 = pl.BlockSpec((1, 512), lambda b, n: (0, 0))
    return _call(
        body, name, (bsz, nc),
        [row(1024, 0), row(LANES, P_SMALL // LANES), row(512, P_SZ // 512), prm, prm, prm5, prm5],
        [row(512, 0), _state_spec()],
        [_sds((bsz, s, 512), BF16), _sds((bsz, nc, 4, LANES, LANES), F32)],
        scratch=[pltpu.VMEM((4, LANES, LANES), F32)],
    )(xbc_act, p, p, a_row, dt_row, d_row, nw)


def _ssd_bwd(do, xbc_act, p, a_row, dt_row, d_row, nw, st_all, name):
    bsz, s, _ = xbc_act.shape
    c = SSD_CHUNK
    nc = s // c

    def body(x_ref, sm_ref, z_ref, a_ref, dt_ref, d_ref, nw_ref, do_ref, st_ref,
             dx_ref, dsm_ref, dz_ref, da_ref, ddt_ref, dd_ref, dnw_ref, ds_scr):
        first = (pl.program_id(0) == 0) & (pl.program_id(1) == 0)

        @pl.when(pl.program_id(1) == 0)
        def _():
            ds_scr[...] = jnp.zeros_like(ds_scr)

        @pl.when(first)
        def _():
            da_ref[...] = jnp.zeros_like(da_ref)
            ddt_ref[...] = jnp.zeros_like(ddt_ref)
            dd_ref[...] = jnp.zeros_like(dd_ref)
            dnw_ref[...] = jnp.zeros_like(dnw_ref)

        states = [st_ref[h] for h in range(4)]
        _, vjp = jax.vjp(_ssd_chunk, states, x_ref[...], sm_ref[...], z_ref[...], a_ref[...], dt_ref[...], d_ref[...], nw_ref[...])
        d_states, dx, dsm, dz, da, ddt, dd, dnw = vjp(([ds_scr[h] for h in range(4)], do_ref[...]))
        for h in range(4):
            ds_scr[h] = d_states[h]
        dx_ref[...] = dx
        dsm_ref[...] = dsm
        dz_ref[...] = dz.astype(BF16)
        da_ref[...] += da
        ddt_ref[...] += ddt
        dd_ref[...] += dd
        dnw_ref[...] += dnw

    row = lambda w, blk: pl.BlockSpec((None, c, w), lambda b, n, blk=blk: (b, nc - 1 - n, blk))
    prm = pl.BlockSpec((1, LANES), lambda b, n: (0, 0))
    prm5 = pl.BlockSpec((1, 512), lambda b, n: (0, 0))
    return _call(
        body, name, (bsz, nc),
        [row(1024, 0), row(LANES, P_SMALL // LANES), row(512, P_SZ // 512), prm, prm, prm5, prm5, row(512, 0),
         pl.BlockSpec((None, None, 4, LANES, LANES), lambda b, n: (b, nc - 1 - n, 0, 0, 0))],
        [row(1024, 0), row(LANES, 0), row(512, 0), prm, prm, prm5, prm5],
        [_sds((bsz, s, 1024), F32), _sds((bsz, s, LANES), F32), _sds((bsz, s, 512), BF16),
         _sds((1, LANES), F32), _sds((1, LANES), F32), _sds((1, 512), F32), _sds((1, 512), F32)],
        scratch=[pltpu.VMEM((4, LANES, LANES), F32)],
    )(xbc_act, p, p, a_row, dt_row, d_row, nw, do, st_all)


def _peer(k):
    x, y, c = lax.axis_index("x"), lax.axis_index("y"), lax.axis_index("c")
    px = 1 - x if k & 4 else x
    py = 1 - y if k & 2 else y
    pc = 1 - c if k & 1 else c
    return (px, py, pc), 4 * px + 2 * py + pc


def _my_index():
    return 4 * lax.axis_index("x") + 2 * lax.axis_index("y") + lax.axis_index("c")


def _exchange(body, name, arg, out_shape):
    return pl.pallas_call(
        body, name=name, out_shape=out_shape,
        in_specs=[pl.BlockSpec(memory_space=pl.ANY)], out_specs=pl.BlockSpec(memory_space=pl.ANY),
        scratch_shapes=[pltpu.SemaphoreType.DMA((N_DEV - 1,)), pltpu.SemaphoreType.DMA((N_DEV - 1,)),
                        pltpu.SemaphoreType.DMA(())],
    )(arg)


def _all_gather(shard, name):
    def body(x_ref, out_ref, send_sems, recv_sems, local_sem):
        me = _my_index()
        mine = pltpu.make_async_copy(x_ref, out_ref.at[me], local_sem)
        mine.start()
        sends = []
        for k in range(1, N_DEV):
            peer, _ = _peer(k)
            cp = pltpu.make_async_remote_copy(src_ref=x_ref, dst_ref=out_ref.at[me], send_sem=send_sems.at[k - 1],
                                              recv_sem=recv_sems.at[k - 1], device_id=peer, device_id_type=MESH_ID)
            cp.start()
            sends.append(cp)
        for k in range(1, N_DEV):
            peer, slot = _peer(k)
            pltpu.make_async_remote_copy(src_ref=x_ref, dst_ref=out_ref.at[slot], send_sem=send_sems.at[k - 1],
                                         recv_sem=recv_sems.at[k - 1], device_id=peer, device_id_type=MESH_ID).wait_recv()
        for cp in sends:
            cp.wait_send()
        mine.wait()

    return _exchange(body, name, shard, _sds((N_DEV,) + shard.shape, shard.dtype))


def _scatter_parts(parts, name):
    def body(p_ref, out_ref, send_sems, recv_sems, local_sem):
        me = _my_index()
        mine = pltpu.make_async_copy(p_ref.at[me], out_ref.at[me], local_sem)
        mine.start()
        sends = []
        for k in range(1, N_DEV):
            peer, slot = _peer(k)
            cp = pltpu.make_async_remote_copy(src_ref=p_ref.at[slot], dst_ref=out_ref.at[me], send_sem=send_sems.at[k - 1],
                                              recv_sem=recv_sems.at[k - 1], device_id=peer, device_id_type=MESH_ID)
            cp.start()
            sends.append(cp)
        for k in range(1, N_DEV):
            peer, slot = _peer(k)
            pltpu.make_async_remote_copy(src_ref=p_ref.at[slot], dst_ref=out_ref.at[slot], send_sem=send_sems.at[k - 1],
                                         recv_sem=recv_sems.at[k - 1], device_id=peer, device_id_type=MESH_ID).wait_recv()
        for cp in sends:
            cp.wait_send()
        mine.wait()

    return _exchange(body, name, parts, _sds(parts.shape, parts.dtype))


def _sum_adamw(g8, w, m, v, name):
    rows, width = w.shape
    tr = PACK_ROWS if rows % PACK_ROWS == 0 else rows

    def body(g_ref, w_ref, m_ref, v_ref, go_ref, d_ref, mo_ref, vo_ref):
        g = g_ref[0]
        for i in range(1, N_DEV):
            g = g + g_ref[i]
        m2 = ADAM_B1 * m_ref[...] + (1.0 - ADAM_B1) * g
        v2 = ADAM_B2 * v_ref[...] + (1.0 - ADAM_B2) * (g * g)
        m_hat = m2 / (1.0 - ADAM_B1 ** ADAM_STEP)
        v_hat = v2 / (1.0 - ADAM_B2 ** ADAM_STEP)
        go_ref[...] = g
        d_ref[...] = -ADAM_LR * (m_hat / (jnp.sqrt(v_hat) + ADAM_EPS) + ADAM_WD * w_ref[...])
        mo_ref[...] = m2
        vo_ref[...] = v2

    flat = pl.BlockSpec((tr, width), lambda i: (i, 0))
    return _call(body, name, (rows // tr,), [pl.BlockSpec((N_DEV, tr, width), lambda i: (0, i, 0)), flat, flat, flat],
                 [flat] * 4, [_sds(w.shape, F32)] * 4)(g8, w, m, v)


SPLIT = (
    ("w_ada", (DEPTH, D_MODEL, 6 * D_MODEL), 2), ("w_in", (DEPTH, D_MODEL, 8720), 2),
    ("gdn_conv_w", (DEPTH, 4, 1536), 2), ("ssd_conv_w", (DEPTH, 4, 1024), 2),
    ("w_br_a", (DEPTH, 512, D_MODEL), 2), ("w_br_b", (DEPTH, 512, D_MODEL), 2), ("w_br_c", (DEPTH, 512, D_MODEL), 2),
    ("w_out", (DEPTH, D_MODEL, D_MODEL), 1), ("ffn_w_up", (DEPTH, D_MODEL, 2 * FFN_HIDDEN), 2),
    ("ffn_conv_w", (DEPTH, 3, 2 * FFN_HIDDEN), 2), ("ffn_w_down", (DEPTH, FFN_HIDDEN, D_MODEL), 1),
)
REPL = (
    ("b_ada", (DEPTH, 6 * D_MODEL)), ("norm1_w", (DEPTH, D_MODEL)), ("gdn_a_log", (DEPTH, 4)),
    ("gdn_dt_bias", (DEPTH, 4)), ("gdn_norm_w", (DEPTH, 128)), ("hgrn_lb_param", (DEPTH, 512)),
    ("hgrn_norm_w", (DEPTH, 128)), ("ssd_conv_b", (DEPTH, 1024)), ("ssd_a_log", (DEPTH, 8)),
    ("ssd_dt_bias", (DEPTH, 8)), ("ssd_d", (DEPTH, 8)), ("ssd_norm_w", (DEPTH, 512)), ("norm2_w", (DEPTH, D_MODEL)),
    ("ffn_conv_b", (DEPTH, 2 * FFN_HIDDEN)), ("final_norm_w", (D_MODEL,)),
)
WEIGHTS = ("w_ada", "b_ada", "norm1_w", "w_in", "gdn_conv_w", "gdn_a_log", "gdn_dt_bias", "gdn_norm_w",
           "hgrn_lb_param", "hgrn_norm_w", "ssd_conv_w", "ssd_conv_b", "ssd_a_log", "ssd_dt_bias", "ssd_d",
           "ssd_norm_w", "w_br_a", "w_br_b", "w_br_c", "w_out", "norm2_w", "ffn_w_up", "ffn_conv_w", "ffn_conv_b",
           "ffn_w_down", "final_norm_w")


def _block_shape(shape, axis):
    return tuple(d // N_DEV if i == axis else d for i, d in enumerate(shape))


def _pack(flat_parts, lead, row_multiple):
    cat = jnp.concatenate(flat_parts, axis=-1)
    n = cat.shape[-1]
    per = PACK_W * row_multiple
    total = -(-n // per) * per
    cat = jnp.pad(cat, [(0, 0)] * len(lead) + [(0, total - n)])
    return cat.reshape(lead + (total // PACK_W, PACK_W))


def _pack_blocks(blocks):
    return _pack([blocks[n].reshape(-1) for n, _, _ in SPLIT], (), PACK_ROWS)


def _unpack_blocks(packed):
    flat, out, off = packed.reshape(-1), {}, 0
    for n, shape, axis in SPLIT:
        bs = _block_shape(shape, axis)
        size = math.prod(bs)
        out[n] = flat[off:off + size].reshape(bs)
        off += size
    return out


def _unpack_gathered(gathered):
    flat, out, off = gathered.reshape(N_DEV, -1), {}, 0
    for n, shape, axis in SPLIT:
        bs = _block_shape(shape, axis)
        size = math.prod(bs)
        seg = flat[:, off:off + size].reshape((N_DEV,) + bs)
        out[n] = jnp.moveaxis(seg, 0, axis).reshape(shape)
        off += size
    return out


def _pack_full(full):
    parts = []
    for n, shape, axis in SPLIT:
        bs = _block_shape(shape, axis)
        t = full[n].reshape(shape[:axis] + (N_DEV, bs[axis]) + shape[axis + 1:])
        parts.append(jnp.moveaxis(t, axis, 0).reshape(N_DEV, -1))
    return _pack(parts, (N_DEV,), PACK_ROWS)


def _pack_repl(vals):
    parts = []
    for n, shape in REPL:
        size = math.prod(shape)
        parts.append(jnp.pad(vals[n].reshape(-1), (0, -(-size // PACK_W) * PACK_W - size)))
    return _pack(parts, (), 8)


def _unpack_repl(packed):
    flat, out, off = packed.reshape(-1), {}, 0
    for n, shape in REPL:
        size = math.prod(shape)
        out[n] = flat[off:off + size].reshape(shape)
        off += -(-size // PACK_W) * PACK_W
    return out


def _lane_row(vec, lane0):
    return jnp.pad(vec, (lane0, LANES - lane0 - vec.shape[0]))[None]


def _arrange_w_in(w):
    offs = [0]
    for sz in W_IN_SPLITS:
        offs.append(offs[-1] + sz)
    qkv, a, b, gz, hq, hf, hi, hg, sz_, xbc, dt, gate = [w[:, offs[i]:offs[i + 1]] for i in range(12)]
    pad = jnp.zeros((w.shape[0], LANES - 16), w.dtype)
    return jnp.concatenate([qkv, gz, xbc, gate, hq, hf, hi, hg, sz_, a, b, dt, pad], axis=1)


def _restore_w_in(wp):
    cut = lambda o, n: wp[:, o:o + n]
    return jnp.concatenate([
        cut(P_QKV, 1536), cut(P_SMALL + SM_A, 4), cut(P_SMALL + SM_B, 4), cut(P_GZ, 512), cut(P_HQ, 512),
        cut(P_HF, 512), cut(P_HI, 512), cut(P_HG, 512), cut(P_SZ, 512), cut(P_XBC, 1024), cut(P_SMALL + SM_DT, 8),
        cut(P_GATE, 3072)], axis=1)


def _layer_consts(l, wf, wr, lower):
    t = lambda a: a.T
    k = {}
    k["w_ada"], k["b_ada"] = wf["w_ada"][l], wr["b_ada"][l][None]
    k["n1w"], k["n2w"] = wr["norm1_w"][l][None], wr["norm2_w"][l][None]
    win = _arrange_w_in(wf["w_in"][l]).astype(BF16)
    k["win"], k["win_t"] = win, t(win)
    for n in ("w_br_a", "w_br_b", "w_br_c", "w_out", "ffn_w_up", "ffn_w_down"):
        wb = wf[n][l].astype(BF16)
        k[n], k[n + "_t"] = wb, t(wb)
    k["gdn_conv_w"], k["gdn_conv_b"] = wf["gdn_conv_w"][l], jnp.zeros((1, 1536), F32)
    k["ssd_conv_w"], k["ssd_conv_b"] = wf["ssd_conv_w"][l], wr["ssd_conv_b"][l][None]
    k["ffn_conv_w"], k["ffn_conv_b"] = wf["ffn_conv_w"][l], wr["ffn_conv_b"][l][None]
    k["gdn_a"], k["gdn_dt"] = _lane_row(wr["gdn_a_log"][l], SM_A), _lane_row(wr["gdn_dt_bias"][l], SM_A)
    k["gdn_nw"], k["hgrn_nw"] = wr["gdn_norm_w"][l][None], wr["hgrn_norm_w"][l][None]
    k["ssd_a"], k["ssd_dt"] = _lane_row(wr["ssd_a_log"][l], SM_DT), _lane_row(wr["ssd_dt_bias"][l], SM_DT)
    k["ssd_d"] = jnp.repeat(wr["ssd_d"][l], SSD_HEAD_DIM)[None]
    k["ssd_nw"] = wr["ssd_norm_w"][l][None]
    k["lb"] = lower[l:l + 1]
    return k


def _layer_fwd(l, x, c8, k):
    bsz, s, d = x.shape
    t = bsz * s
    sv = {"x": x}
    mod = _ada_fwd(c8, k["w_ada"], k["b_ada"], f"ada_fwd{l}")[:bsz]
    sv["mod"] = [mod[:, None, i * d:(i + 1) * d] for i in range(6)]
    sh1, sc1, g1, sh2, sc2, g2 = sv["mod"]
    h1 = _norm_mod_fwd(x, k["n1w"], sh1, sc1, f"norm1_fwd{l}")
    p = _mm(h1.reshape(t, d), k["win"], F32, f"mm_in{l}").reshape(bsz, s, P_WIDTH)
    qkv_act = _conv_fwd(p, P_QKV, 1536, k["gdn_conv_w"], k["gdn_conv_b"], True, f"gdn_conv_fwd{l}")
    oa, st_a = _gdn_fwd(qkv_act, p, k["gdn_a"], k["gdn_dt"], k["gdn_nw"], f"gdn_fwd{l}")
    ob, st_b = _hgrn_fwd(p, k["lb"], k["hgrn_nw"], f"hgrn_fwd{l}")
    xbc_act = _conv_fwd(p, P_XBC, 1024, k["ssd_conv_w"], k["ssd_conv_b"], True, f"ssd_conv_fwd{l}")
    oc, st_c = _ssd_fwd(xbc_act, p, k["ssd_a"], k["ssd_dt"], k["ssd_d"], k["ssd_nw"], f"ssd_fwd{l}")
    merged = _merge_fwd(p, oa, ob, oc, k["w_br_a"], k["w_br_b"], k["w_br_c"], f"merge_fwd{l}")
    mix = _mm(merged.reshape(t, d), k["w_out"], F32, f"mm_out{l}").reshape(bsz, s, d)
    x1 = _resid_fwd(x, mix, g1, f"resid1_fwd{l}")
    h2 = _norm_mod_fwd(x1, k["n2w"], sh2, sc2, f"norm2_fwd{l}")
    u_pre = _mm(h2.reshape(t, d), k["ffn_w_up"], F32, f"mm_up{l}").reshape(bsz, s, 2 * FFN_HIDDEN)
    u = _conv_fwd(u_pre, 0, 2 * FFN_HIDDEN, k["ffn_conv_w"], k["ffn_conv_b"], False, f"ffn_conv_fwd{l}")
    a = _glu_fwd(u, f"glu_fwd{l}")
    ffn = _mm(a.reshape(t, FFN_HIDDEN), k["ffn_w_down"], F32, f"mm_down{l}").reshape(bsz, s, d)
    x2 = _resid_fwd(x1, ffn, g2, f"resid2_fwd{l}")
    sv.update(h1=h1, p=p, qkv_act=qkv_act, oa=oa, st_a=st_a, ob=ob, st_b=st_b, xbc_act=xbc_act, oc=oc, st_c=st_c,
              merged=merged, mix=mix, x1=x1, h2=h2, u_pre=u_pre, u=u, a=a, ffn=ffn)
    return x2, sv


def _layer_bwd(l, dx2, ct8, k, sv):
    bsz, s, d = dx2.shape
    t = bsz * s
    f2 = 2 * FFN_HIDDEN
    sh1, sc1, g1, sh2, sc2, g2 = sv["mod"]
    tr = lambda a: a.reshape(t, -1).T
    g = {}
    dffn, dg2 = _gate_bwd(dx2, sv["ffn"], g2, f"gate2_bwd{l}")
    dffn2 = dffn.reshape(t, d)
    da = _mm(dffn2, k["ffn_w_down_t"], F32, f"mm_down_dx{l}").reshape(bsz, s, FFN_HIDDEN)
    g["ffn_w_down"] = _mm(tr(sv["a"]), dffn2, F32, f"mm_down_dw{l}")
    du = _glu_bwd(da, sv["u"], f"glu_bwd{l}")
    du_pre, g["ffn_conv_w"], dfcb = _conv_bwd(du, sv["u_pre"], 0, f2, k["ffn_conv_w"], k["ffn_conv_b"], False, f"ffn_conv_bwd{l}")
    g["ffn_conv_b"] = dfcb[0]
    du2 = du_pre.reshape(t, f2)
    dh2 = _mm(du2, k["ffn_w_up_t"], F32, f"mm_up_dx{l}").reshape(bsz, s, d)
    g["ffn_w_up"] = _mm(tr(sv["h2"]), du2, F32, f"mm_up_dw{l}")
    dx1, dn2w, dsh2, dsc2 = _norm_mod_bwd(sv["x1"], k["n2w"], sh2, sc2, dh2, dx2, f"norm2_bwd{l}")
    g["norm2_w"] = dn2w[0]
    dmix, dg1 = _gate_bwd(dx1, sv["mix"], g1, f"gate1_bwd{l}")
    dmix2 = dmix.reshape(t, d)
    dmerged = _mm(dmix2, k["w_out_t"], F32, f"mm_out_dx{l}").reshape(bsz, s, d)
    g["w_out"] = _mm(tr(sv["merged"]), dmix2, F32, f"mm_out_dw{l}")
    p = sv["p"]
    dgate, doa, dob, doc, dya, dyb, dyc = _merge_bwd(
        dmerged, p, sv["oa"], sv["ob"], sv["oc"], k["w_br_a"], k["w_br_b"], k["w_br_c"],
        k["w_br_a_t"], k["w_br_b_t"], k["w_br_c_t"], f"merge_bwd{l}")
    g["w_br_a"] = _mm(tr(sv["oa"]), dya.reshape(t, d), F32, f"mm_bra_dw{l}")
    g["w_br_b"] = _mm(tr(sv["ob"]), dyb.reshape(t, d), F32, f"mm_brb_dw{l}")
    g["w_br_c"] = _mm(tr(sv["oc"]), dyc.reshape(t, d), F32, f"mm_brc_dw{l}")
    dxbc_act, dsm_c, dsz, da_c, ddt_c, dd_c, dnw_c = _ssd_bwd(
        doc, sv["xbc_act"], p, k["ssd_a"], k["ssd_dt"], k["ssd_d"], k["ssd_nw"], sv["st_c"], f"ssd_bwd{l}")
    dxbc_raw, g["ssd_conv_w"], dscb = _conv_bwd(dxbc_act, p, P_XBC, 1024, k["ssd_conv_w"], k["ssd_conv_b"], True, f"ssd_conv_bwd{l}")
    g["ssd_conv_b"] = dscb[0]
    g["ssd_a_log"], g["ssd_dt_bias"] = da_c[0, SM_DT:SM_DT + 8], ddt_c[0, SM_DT:SM_DT + 8]
    g["ssd_d"] = dd_c.reshape(SSD_HEADS, SSD_HEAD_DIM).sum(axis=1)
    g["ssd_norm_w"] = dnw_c[0]
    dhg, dlb, dnw_b = _hgrn_bwd(dob, p, k["lb"], k["hgrn_nw"], sv["st_b"], f"hgrn_bwd{l}")
    g["hgrn_norm_w"] = dnw_b[0]
    dqkv_act, dsm_a, dgz, da_a, ddt_a, dnw_a = _gdn_bwd(
        doa, sv["qkv_act"], p, k["gdn_a"], k["gdn_dt"], k["gdn_nw"], sv["st_a"], f"gdn_bwd{l}")
    dqkv_raw, g["gdn_conv_w"], _ = _conv_bwd(dqkv_act, p, P_QKV, 1536, k["gdn_conv_w"], k["gdn_conv_b"], True, f"gdn_conv_bwd{l}")
    g["gdn_a_log"], g["gdn_dt_bias"], g["gdn_norm_w"] = da_a[0, :4], ddt_a[0, :4], dnw_a[0]
    dsmall = (dsm_a + dsm_c).astype(BF16)
    dp = jnp.concatenate([dqkv_raw, dgz, dxbc_raw, dgate, dhg, dsz, dsmall], axis=-1).reshape(t, P_WIDTH)
    dh1 = _mm(dp, k["win_t"], F32, f"mm_in_dx{l}").reshape(bsz, s, d)
    g["w_in"] = _restore_w_in(_mm(tr(sv["h1"]), dp, F32, f"mm_in_dw{l}"))
    dx, dn1w, dsh1, dsc1 = _norm_mod_bwd(sv["x"], k["n1w"], sh1, sc1, dh1, dx1, f"norm1_bwd{l}")
    g["norm1_w"] = dn1w[0]
    dmod = jnp.concatenate([dsh1, dsc1, dg1, dsh2, dsc2, dg2], axis=-1)[:, 0]
    dmod8 = jnp.pad(dmod, ((0, 8 - bsz), (0, 0)))
    g["w_ada"], dba = _ada_bwd(ct8, dmod8, f"ada_bwd{l}")
    g["b_ada"] = dba[0]
    return dx, g, dlb


def _local_step(x, c, wf, wr, target):
    bsz = x.shape[0]
    c8 = jnp.pad(c, ((0, 8 - bsz), (0, 0)))
    ct8 = c8.T
    lower = _lb_fwd(wr["hgrn_lb_param"])
    ks = [_layer_consts(l, wf, wr, lower) for l in range(DEPTH)]
    saved = []
    h = x
    for l in range(DEPTH):
        h, sv = _layer_fwd(l, h, c8, ks[l])
        saved.append(sv)
    loss8, dh, dfnw = _final_loss(h, wr["final_norm_w"][None], target)
    per_layer, dlbs = [None] * DEPTH, [None] * DEPTH
    for l in reversed(range(DEPTH)):
        dh, per_layer[l], dlbs[l] = _layer_bwd(l, dh, ct8, ks[l], saved[l])
    grads = {n: jnp.stack([per_layer[l][n] for l in range(DEPTH)]) for n in per_layer[0]}
    grads["hgrn_lb_param"] = _lb_bwd(wr["hgrn_lb_param"], jnp.concatenate(dlbs, axis=0))
    grads["final_norm_w"] = dfnw[0]
    return loss8[0, 0], dh, grads


def kernel(x, c, w_ada, b_ada, norm1_w, w_in, gdn_conv_w, gdn_a_log, gdn_dt_bias, gdn_norm_w, hgrn_lb_param, hgrn_norm_w, ssd_conv_w, ssd_conv_b, ssd_a_log, ssd_dt_bias, ssd_d, ssd_norm_w, w_br_a, w_br_b, w_br_c, w_out, norm2_w, ffn_w_up, ffn_conv_w, ffn_conv_b, ffn_w_down, final_norm_w, loss_target, m_w_ada, m_b_ada, m_norm1_w, m_w_in, m_gdn_conv_w, m_gdn_a_log, m_gdn_dt_bias, m_gdn_norm_w, m_hgrn_lb_param, m_hgrn_norm_w, m_ssd_conv_w, m_ssd_conv_b, m_ssd_a_log, m_ssd_dt_bias, m_ssd_d, m_ssd_norm_w, m_w_br_a, m_w_br_b, m_w_br_c, m_w_out, m_norm2_w, m_ffn_w_up, m_ffn_conv_w, m_ffn_conv_b, m_ffn_w_down, m_final_norm_w, v_w_ada, v_b_ada, v_norm1_w, v_w_in, v_gdn_conv_w, v_gdn_a_log, v_gdn_dt_bias, v_gdn_norm_w, v_hgrn_lb_param, v_hgrn_norm_w, v_ssd_conv_w, v_ssd_conv_b, v_ssd_a_log, v_ssd_dt_bias, v_ssd_d, v_ssd_norm_w, v_w_br_a, v_w_br_b, v_w_br_c, v_w_out, v_norm2_w, v_ffn_w_up, v_ffn_conv_w, v_ffn_conv_b, v_ffn_w_down, v_final_norm_w):
    given = dict(locals())
    w = {n: given[n] for n in WEIGHTS}
    m = {n: given["m_" + n] for n in WEIGHTS}
    v = {n: given["v_" + n] for n in WEIGHTS}
    w_packed = _pack_blocks(w)
    wf = _unpack_gathered(_all_gather(w_packed, "gather_weights"))
    loss, dx, grads = _local_step(x, c, wf, w, loss_target)
    g8 = _scatter_parts(_pack_full(grads), "scatter_grads")
    outs = _sum_adamw(g8, w_packed, _pack_blocks(m), _pack_blocks(v), "adamw_split")
    split_out = [_unpack_blocks(o) for o in outs]
    r8 = _all_gather(_pack_repl(grads), "gather_small_grads")
    outs_r = _sum_adamw(r8, _pack_repl(w), _pack_repl(m), _pack_repl(v), "adamw_repl")
    repl_out = [_unpack_repl(o) for o in outs_r]
    split_names = {n for n, _, _ in SPLIT}
    pick = lambda i, n: split_out[i][n] if n in split_names else repl_out[i][n]
    loss = lax.psum(loss, ("x", "y", "c"))
    return (loss, dx, *[pick(i, n) for i in range(4) for n in WEIGHTS])
```

```python
import functools
import math

import jax
import jax.numpy as jnp
from jax import lax
from jax.experimental import pallas as pl
from jax.experimental.pallas import tpu as pltpu

F32, BF16 = jnp.float32, jnp.bfloat16
HI = lax.Precision.HIGHEST
MESH_ID = pl.DeviceIdType.MESH

N_DEV = 8
EPS = 1e-6
D_MODEL = 1024
DEPTH = 2
GDN_HEADS, GDN_DK, GDN_CHUNK = 4, 128, 64
HGRN_HEADS, HGRN_DK, HGRN_CHUNK, HGRN_BLOCK = 4, 128, 16, 128
SSD_HEADS, SSD_HEAD_DIM, SSD_GROUPS, SSD_STATE, SSD_CHUNK = 8, 64, 2, 128, 64
SSD_INNER = SSD_HEADS * SSD_HEAD_DIM
FFN_HIDDEN = 2816
LANES = 128
P_QKV, P_GZ, P_XBC, P_GATE, P_HQ, P_HF, P_HI, P_HG, P_SZ, P_SMALL, P_WIDTH = (
    0, 1536, 2048, 3072, 6144, 6656, 7168, 7680, 8192, 8704, 9216)
SM_A, SM_B, SM_DT = 0, 4, 8
W_IN_SPLITS = (1536, 4, 4, 512, 512, 512, 512, 512, 512, 1024, 8, 3072)

ADAM_LR, ADAM_B1, ADAM_B2, ADAM_EPS, ADAM_WD, ADAM_STEP = 0.001, 0.9, 0.999, 1e-08, 0.01, 10

V7X_VMEM_LIMIT = 56 * 1024 * 1024
PACK_W = 1024
PACK_ROWS = 128


def _call(body, name, grid, in_specs, out_specs, out_shape, scratch=()):
    return pl.pallas_call(
        body, name=name, grid=grid, in_specs=in_specs, out_specs=out_specs, out_shape=out_shape,
        scratch_shapes=list(scratch),
        compiler_params=pltpu.CompilerParams(
            dimension_semantics=("arbitrary",) * len(grid), vmem_limit_bytes=V7X_VMEM_LIMIT),
    )


def _pick(n, cands):
    for c in cands:
        if n % c == 0:
            return c
    raise ValueError(f"no tile for {n} among {cands}")


def _sds(shape, dtype):
    return jax.ShapeDtypeStruct(shape, dtype)


def _dot(a, b):
    return lax.dot_general(a, b, (((1,), (0,)), ((), ())), precision=HI, preferred_element_type=F32)


def _dot_nt(a, b):
    return lax.dot_general(a, b, (((1,), (1,)), ((), ())), precision=HI, preferred_element_type=F32)


def _dot_tn(a, b):
    return lax.dot_general(a, b, (((0,), (0,)), ((), ())), precision=HI, preferred_element_type=F32)


def _mxu(a, b, dims):
    return lax.dot_general(a.astype(BF16), b.astype(BF16), dims, preferred_element_type=F32)


def _bdot(a, b):
    return _mxu(a, b, (((1,), (0,)), ((), ())))


def _bdot_nt(a, b):
    return _mxu(a, b, (((1,), (1,)), ((), ())))


def _bdot_tn(a, b):
    return _mxu(a, b, (((0,), (0,)), ((), ())))


def _iota(shape, axis):
    return lax.broadcasted_iota(jnp.int32, shape, axis)


def _silu(x):
    return x * jax.nn.sigmoid(x)


def _softplus(x):
    return jnp.maximum(x, 0.0) + jnp.log1p(jnp.exp(-jnp.abs(x)))


def _rms(x, w):
    return x * lax.rsqrt(jnp.mean(x * x, axis=-1, keepdims=True) + EPS) * w


def _lane_col(x, lane):
    m = (_iota(x.shape, 1) == lane).astype(F32)
    return jnp.sum(x * m, axis=1, keepdims=True)


def _col_to_row(c):
    n = c.shape[0]
    eye = (_iota((n, n), 0) == _iota((n, n), 1)).astype(F32)
    return jnp.sum(c * eye, axis=0, keepdims=True)


def _tril(n, strict=False):
    r, c = _iota((n, n), 0), _iota((n, n), 1)
    return (r > c) if strict else (r >= c)


def _mm(a, b, out_dtype, name):
    m, k = a.shape
    n = b.shape[1]
    tm = _pick(m, (512, 256, 128, 64, 32, 16, 8))
    tn = _pick(n, (1024, 768, 512, 384, 256, 128))
    tk = k if k <= 3072 else _pick(k, (1024, 768, 512, 384, 256, 128))
    nk = k // tk

    def body(a_ref, b_ref, o_ref, acc_ref):
        kk = pl.program_id(2)

        @pl.when(kk == 0)
        def _():
            acc_ref[...] = jnp.zeros_like(acc_ref)

        acc_ref[...] += _bdot(a_ref[...], b_ref[...])

        @pl.when(kk == nk - 1)
        def _():
            o_ref[...] = acc_ref[...].astype(out_dtype)

    return _call(
        body, name, (m // tm, n // tn, nk),
        [pl.BlockSpec((tm, tk), lambda i, j, kk: (i, kk)), pl.BlockSpec((tk, tn), lambda i, j, kk: (kk, j))],
        pl.BlockSpec((tm, tn), lambda i, j, kk: (i, j)), _sds((m, n), out_dtype),
        scratch=[pltpu.VMEM((tm, tn), F32)],
    )(a, b)


def _ada_fwd(c_all, w, b):
    depth, _, n = w.shape
    rows = c_all.shape[0]

    def body(c_ref, w_ref, b_ref, o_ref):
        o_ref[...] = _dot(_silu(c_ref[...]), w_ref[...]) + b_ref[...]

    return _call(
        body, "ada_fwd", (depth,),
        [pl.BlockSpec((rows, D_MODEL), lambda l: (0, 0)), pl.BlockSpec((None, D_MODEL, n), lambda l: (l, 0, 0)),
         pl.BlockSpec((None, 1, n), lambda l: (l, 0, 0))],
        pl.BlockSpec((None, rows, n), lambda l: (l, 0, 0)), _sds((depth, rows, n), F32),
    )(c_all, w, b)


def _ada_bwd(c_all_t, dmod):
    depth, rows, n = dmod.shape

    def body(ct_ref, dm_ref, dw_ref, db_ref):
        dm = dm_ref[...]
        dw_ref[...] = _dot(_silu(ct_ref[...]), dm)
        db_ref[...] = jnp.sum(dm, axis=0, keepdims=True)

    return _call(
        body, "ada_bwd", (depth,),
        [pl.BlockSpec((D_MODEL, rows), lambda l: (0, 0)), pl.BlockSpec((None, rows, n), lambda l: (l, 0, 0))],
        [pl.BlockSpec((None, D_MODEL, n), lambda l: (l, 0, 0)), pl.BlockSpec((None, 1, n), lambda l: (l, 0, 0))],
        [_sds((depth, D_MODEL, n), F32), _sds((depth, 1, n), F32)],
    )(c_all_t, dmod)


def _lb_fn(p):
    rows = [p[l:l + 1] for l in range(DEPTH)]
    mx = functools.reduce(jnp.maximum, rows)
    es = [jnp.exp(r - mx) for r in rows]
    tot = functools.reduce(lambda a, b: a + b, es)
    sm = [e / tot for e in es]
    out, run = [], None
    for l in range(DEPTH):
        run = sm[l] if run is None else run + sm[l]
        out.append(run - sm[0])
    return jnp.concatenate(out, axis=0)


def _lb_fwd(p):
    def body(p_ref, o_ref):
        o_ref[...] = _lb_fn(p_ref[...])

    full = pl.BlockSpec(p.shape, lambda i: (0, 0))
    return _call(body, "lb_fwd", (1,), [full], full, _sds(p.shape, F32))(p)


def _lb_bwd(p, d_lower):
    def body(p_ref, d_ref, o_ref):
        _, vjp = jax.vjp(_lb_fn, p_ref[...])
        o_ref[...] = vjp(d_ref[...])[0]

    full = pl.BlockSpec(p.shape, lambda i: (0, 0))
    return _call(body, "lb_bwd", (1,), [full, full], full, _sds(p.shape, F32))(p, d_lower)


def _norm_mod_fn(x, w, shift, scale):
    return _rms(x, w) * (1.0 + scale) + shift


def _norm_mod_fwd(x, w, shift, scale, name):
    bsz, s, d = x.shape
    ts = _pick(s, (256, 128, 64, 32, 16, 8))

    def body(x_ref, w_ref, sh_ref, sc_ref, o_ref):
        o_ref[...] = _norm_mod_fn(x_ref[...], w_ref[...], sh_ref[...], sc_ref[...]).astype(BF16)

    row = pl.BlockSpec((None, ts, d), lambda b, i: (b, i, 0))
    per_b = pl.BlockSpec((None, 1, d), lambda b, i: (b, 0, 0))
    return _call(body, name, (bsz, s // ts), [row, pl.BlockSpec((1, d), lambda b, i: (0, 0)), per_b, per_b],
                 row, _sds(x.shape, BF16))(x, w, shift, scale)


def _norm_mod_bwd(x, w, shift, scale, dh, carry, name):
    bsz, s, d = x.shape
    ts = _pick(s, (256, 128, 64, 32, 16, 8))

    def body(x_ref, w_ref, sh_ref, sc_ref, dh_ref, c_ref, dx_ref, dw_ref, dsh_ref, dsc_ref):
        b, i = pl.program_id(0), pl.program_id(1)
        _, vjp = jax.vjp(_norm_mod_fn, x_ref[...], w_ref[...], sh_ref[...], sc_ref[...])
        dx, dw, dsh, dsc = vjp(dh_ref[...])
        dx_ref[...] = dx + c_ref[...]

        @pl.when((b == 0) & (i == 0))
        def _():
            dw_ref[...] = jnp.zeros_like(dw_ref)

        @pl.when(i == 0)
        def _():
            dsh_ref[...] = jnp.zeros_like(dsh_ref)
            dsc_ref[...] = jnp.zeros_like(dsc_ref)

        dw_ref[...] += dw
        dsh_ref[...] += dsh
        dsc_ref[...] += dsc

    row = pl.BlockSpec((None, ts, d), lambda b, i: (b, i, 0))
    per_b = pl.BlockSpec((None, 1, d), lambda b, i: (b, 0, 0))
    wspec = pl.BlockSpec((1, d), lambda b, i: (0, 0))
    return _call(body, name, (bsz, s // ts), [row, wspec, per_b, per_b, row, row],
                 [row, wspec, per_b, per_b],
                 [_sds(x.shape, F32), _sds((1, d), F32), _sds((bsz, 1, d), F32), _sds((bsz, 1, d), F32)],
                 )(x, w, shift, scale, dh, carry)


def _resid_fwd(x, y, gate, name):
    bsz, s, d = x.shape
    ts = _pick(s, (512, 256, 128, 64, 32, 16, 8))

    def body(x_ref, y_ref, g_ref, o_ref):
        o_ref[...] = x_ref[...] + g_ref[...] * y_ref[...]

    row = pl.BlockSpec((None, ts, d), lambda b, i: (b, i, 0))
    per_b = pl.BlockSpec((None, 1, d), lambda b, i: (b, 0, 0))
    return _call(body, name, (bsz, s // ts), [row, row, per_b], row, _sds(x.shape, F32))(x, y, gate)


def _gate_bwd(dx, y, gate, name):
    bsz, s, d = dx.shape
    ts = _pick(s, (512, 256, 128, 64, 32, 16, 8))

    def body(dx_ref, y_ref, g_ref, dy_ref, dg_ref):
        dxv = dx_ref[...]
        dy_ref[...] = (dxv * g_ref[...]).astype(BF16)

        @pl.when(pl.program_id(1) == 0)
        def _():
            dg_ref[...] = jnp.zeros_like(dg_ref)

        dg_ref[...] += jnp.sum(dxv * y_ref[...], axis=0, keepdims=True)

    row = pl.BlockSpec((None, ts, d), lambda b, i: (b, i, 0))
    per_b = pl.BlockSpec((None, 1, d), lambda b, i: (b, 0, 0))
    return _call(body, name, (bsz, s // ts), [row, row, per_b], [row, per_b],
                 [_sds(dx.shape, BF16), _sds((bsz, 1, d), F32)])(dx, y, gate)


HALO = 8


def _conv_pre(xx, w_ref, b_ref, kw, rows):
    acc = w_ref[kw - 1:kw, :] * xx[HALO:HALO + rows]
    for k in range(kw - 1):
        acc = acc + w_ref[k:k + 1, :] * pltpu.roll(xx, kw - 1 - k, 0)[HALO:HALO + rows]
    return acc + b_ref[...]


def _conv_fwd(x, col0, width, w, b, act, name):
    bsz, s, _ = x.shape
    kw = w.shape[0]
    ts = _pick(s, (512, 256, 128, 64, 32, 16, 8))
    tc = _pick(width, (512, 256, 128))
    assert col0 % tc == 0
    c0 = col0 // tc
    hb = ts // HALO

    def body(x_ref, xp_ref, w_ref, b_ref, o_ref):
        i = pl.program_id(1)
        xp = jnp.where(i > 0, xp_ref[...], 0.0)
        xx = jnp.concatenate([xp, x_ref[...]], axis=0)
        pre = _conv_pre(xx, w_ref, b_ref, kw, ts)
        o_ref[...] = _silu(pre) if act else pre

    return _call(
        body, name, (bsz, s // ts, width // tc),
        [pl.BlockSpec((None, ts, tc), lambda bb, i, j: (bb, i, c0 + j)),
         pl.BlockSpec((None, HALO, tc), lambda bb, i, j: (bb, jnp.maximum(i * hb - 1, 0), c0 + j)),
         pl.BlockSpec((kw, tc), lambda bb, i, j: (0, j)), pl.BlockSpec((1, tc), lambda bb, i, j: (0, j))],
        pl.BlockSpec((None, ts, tc), lambda bb, i, j: (bb, i, j)), _sds((bsz, s, width), F32),
    )(x, x, w, b)


def _conv_bwd(dy, x, col0, width, w, b, act, name):
    bsz, s, _ = x.shape
    kw = w.shape[0]
    ts = _pick(s, (512, 256, 128, 64, 32, 16, 8))
    tc = _pick(width, (512, 256, 128))
    c0 = col0 // tc
    hb = ts // HALO
    nt = s // ts
    last_h = s // HALO - 1

    def body(x_ref, xp_ref, xn_ref, dy_ref, dyn_ref, w_ref, b_ref, dx_ref, dw_ref, db_ref):
        bb, i = pl.program_id(1), pl.program_id(2)
        xp = jnp.where(i > 0, xp_ref[...], 0.0)
        xx = jnp.concatenate([xp, x_ref[...], xn_ref[...]], axis=0)
        dyy = jnp.concatenate([dy_ref[...], jnp.where(i < nt - 1, dyn_ref[...], 0.0)], axis=0)
        n = ts + HALO
        if act:
            pre = _conv_pre(xx, w_ref, b_ref, kw, n)
            sg = jax.nn.sigmoid(pre)
            dpre = dyy * (sg * (1.0 + pre * (1.0 - sg)))
        else:
            dpre = dyy
        dx = w_ref[kw - 1:kw, :] * dpre[:ts]
        for k in range(kw - 1):
            dx = dx + w_ref[k:k + 1, :] * pltpu.roll(dpre, n - (kw - 1 - k), 0)[:ts]
        dx_ref[...] = dx.astype(BF16)

        @pl.when((bb == 0) & (i == 0))
        def _():
            dw_ref[...] = jnp.zeros_like(dw_ref)
            db_ref[...] = jnp.zeros_like(db_ref)

        dt = dpre[:ts]
        db_ref[...] += jnp.sum(dt, axis=0, keepdims=True)
        dw_ref[kw - 1:kw, :] += jnp.sum(dt * xx[HALO:HALO + ts], axis=0, keepdims=True)
        for k in range(kw - 1):
            xs = pltpu.roll(xx, kw - 1 - k, 0)[HALO:HALO + ts]
            dw_ref[k:k + 1, :] += jnp.sum(dt * xs, axis=0, keepdims=True)

    xspec = lambda f: pl.BlockSpec((None, HALO, tc), f)
    return _call(
        body, name, (width // tc, bsz, nt),
        [pl.BlockSpec((None, ts, tc), lambda j, bb, i: (bb, i, c0 + j)),
         xspec(lambda j, bb, i: (bb, jnp.maximum(i * hb - 1, 0), c0 + j)),
         xspec(lambda j, bb, i: (bb, jnp.minimum((i + 1) * hb, last_h), c0 + j)),
         pl.BlockSpec((None, ts, tc), lambda j, bb, i: (bb, i, j)),
         xspec(lambda j, bb, i: (bb, jnp.minimum((i + 1) * hb, last_h), j)),
         pl.BlockSpec((kw, tc), lambda j, bb, i: (0, j)), pl.BlockSpec((1, tc), lambda j, bb, i: (0, j))],
        [pl.BlockSpec((None, ts, tc), lambda j, bb, i: (bb, i, j)),
         pl.BlockSpec((kw, tc), lambda j, bb, i: (0, j)), pl.BlockSpec((1, tc), lambda j, bb, i: (0, j))],
        [_sds((bsz, s, width), BF16), _sds((kw, width), F32), _sds((1, width), F32)],
    )(x, x, x, dy, dy, w, b)


def _glu_fwd(u, name):
    bsz, s, f2 = u.shape
    f = f2 // 2
    ts = _pick(s, (512, 256, 128, 64, 32, 16, 8))
    tc = _pick(f, (256, 128))
    nf = f // tc

    def body(g_ref, v_ref, o_ref):
        o_ref[...] = (_silu(g_ref[...]) * v_ref[...]).astype(BF16)

    return _call(
        body, name, (bsz, s // ts, nf),
        [pl.BlockSpec((None, ts, tc), lambda b, i, j: (b, i, j)),
         pl.BlockSpec((None, ts, tc), lambda b, i, j: (b, i, nf + j))],
        pl.BlockSpec((None, ts, tc), lambda b, i, j: (b, i, j)), _sds((bsz, s, f), BF16),
    )(u, u)


def _glu_bwd(da, u, name):
    bsz, s, f2 = u.shape
    f = f2 // 2
    ts = _pick(s, (512, 256, 128, 64, 32, 16, 8))
    tc = _pick(f, (256, 128))
    nf = f // tc

    def body(da_ref, g_ref, v_ref, o_ref):
        j = pl.program_id(2)
        g, v, d = g_ref[...], v_ref[...], da_ref[...]
        sg = jax.nn.sigmoid(g)
        dg = d * v * (sg * (1.0 + g * (1.0 - sg)))
        dv = d * (g * sg)
        o_ref[...] = jnp.where(j < nf, dg, dv)

    return _call(
        body, name, (bsz, s // ts, 2 * nf),
        [pl.BlockSpec((None, ts, tc), lambda b, i, j: (b, i, j % nf)),
         pl.BlockSpec((None, ts, tc), lambda b, i, j: (b, i, j % nf)),
         pl.BlockSpec((None, ts, tc), lambda b, i, j: (b, i, nf + j % nf))],
        pl.BlockSpec((None, ts, tc), lambda b, i, j: (b, i, j)), _sds((bsz, s, f2), F32),
    )(da, u, u)


def _merge_fwd(p, oa, ob, oc, wa, wb, wc, name):
    bsz, s, _ = p.shape
    tm = _pick(s, (256, 128, 64, 32, 16, 8))
    gblk = P_GATE // (3 * D_MODEL)

    def body(g_ref, oa_ref, ob_ref, oc_ref, wa_ref, wb_ref, wc_ref, o_ref):
        acc = None
        for i, (o_r, w_r) in enumerate(((oa_ref, wa_ref), (ob_ref, wb_ref), (oc_ref, wc_ref))):
            y = _bdot(o_r[...], w_r[...])
            t = jax.nn.sigmoid(g_ref[:, i * D_MODEL:(i + 1) * D_MODEL]) * y
            acc = t if acc is None else acc + t
        o_ref[...] = acc.astype(BF16)

    orow = pl.BlockSpec((None, tm, 512), lambda b, i: (b, i, 0))
    wfull = pl.BlockSpec((512, D_MODEL), lambda b, i: (0, 0))
    return _call(
        body, name, (bsz, s // tm),
        [pl.BlockSpec((None, tm, 3 * D_MODEL), lambda b, i: (b, i, gblk)), orow, orow, orow, wfull, wfull, wfull],
        pl.BlockSpec((None, tm, D_MODEL), lambda b, i: (b, i, 0)), _sds((bsz, s, D_MODEL), BF16),
    )(p, oa, ob, oc, wa, wb, wc)


def _merge_bwd(dm, p, oa, ob, oc, wa, wb, wc, wat, wbt, wct, name):
    bsz, s, _ = p.shape
    tm = _pick(s, (256, 128, 64, 32, 16, 8))
    gblk = P_GATE // (3 * D_MODEL)

    def body(dm_ref, g_ref, oa_ref, ob_ref, oc_ref, wa_ref, wb_ref, wc_ref, wat_ref, wbt_ref, wct_ref,
             dg_ref, doa_ref, dob_ref, doc_ref, dya_ref, dyb_ref, dyc_ref):
        dmv = dm_ref[...]
        trip = ((oa_ref, wa_ref, wat_ref, doa_ref, dya_ref), (ob_ref, wb_ref, wbt_ref, dob_ref, dyb_ref),
                (oc_ref, wc_ref, wct_ref, doc_ref, dyc_ref))
        for i, (o_r, w_r, wt_r, do_r, dy_r) in enumerate(trip):
            y = _bdot(o_r[...], w_r[...])
            sg = jax.nn.sigmoid(g_ref[:, i * D_MODEL:(i + 1) * D_MODEL])
            dg_ref[:, i * D_MODEL:(i + 1) * D_MODEL] = (dmv * y * sg * (1.0 - sg)).astype(BF16)
            dy = (dmv * sg).astype(BF16)
            dy_r[...] = dy
            do_r[...] = _bdot(dy, wt_r[...])

    orow = pl.BlockSpec((None, tm, 512), lambda b, i: (b, i, 0))
    drow = pl.BlockSpec((None, tm, D_MODEL), lambda b, i: (b, i, 0))
    grow = pl.BlockSpec((None, tm, 3 * D_MODEL), lambda b, i: (b, i, 0))
    wfull = pl.BlockSpec((512, D_MODEL), lambda b, i: (0, 0))
    wtfull = pl.BlockSpec((D_MODEL, 512), lambda b, i: (0, 0))
    return _call(
        body, name, (bsz, s // tm),
        [drow, pl.BlockSpec((None, tm, 3 * D_MODEL), lambda b, i: (b, i, gblk)), orow, orow, orow,
         wfull, wfull, wfull, wtfull, wtfull, wtfull],
        [grow, orow, orow, orow, drow, drow, drow],
        [_sds((bsz, s, 3 * D_MODEL), BF16)] + [_sds((bsz, s, 512), F32)] * 3 + [_sds((bsz, s, D_MODEL), BF16)] * 3,
    )(dm, p, oa, ob, oc, wa, wb, wc, wat, wbt, wct)


def _final_loss(x, w, target):
    bsz, s, d = x.shape
    ts = _pick(s, (256, 128, 64, 32, 16, 8))

    def body(x_ref, w_ref, t_ref, loss_ref, dx_ref, dw_ref):
        first = (pl.program_id(0) == 0) & (pl.program_id(1) == 0)
        y, vjp = jax.vjp(_rms, x_ref[...], w_ref[...])
        err = y - t_ref[...]
        dx, dw = vjp(err * (1.0 / d))
        dx_ref[...] = dx

        @pl.when(first)
        def _():
            loss_ref[...] = jnp.zeros_like(loss_ref)
            dw_ref[...] = jnp.zeros_like(dw_ref)

        loss_ref[...] += 0.5 * jnp.sum(jnp.sum(err * err, axis=1, keepdims=True), axis=0, keepdims=True) * (1.0 / d)
        dw_ref[...] += dw

    row = pl.BlockSpec((None, ts, d), lambda b, i: (b, i, 0))
    wspec = pl.BlockSpec((1, d), lambda b, i: (0, 0))
    return _call(body, "final_loss", (bsz, s // ts), [row, wspec, row],
                 [pl.BlockSpec((8, LANES), lambda b, i: (0, 0)), row, wspec],
                 [_sds((8, LANES), F32), _sds(x.shape, F32), _sds((1, d), F32)])(x, w, target)


def _unit_lower_inverse(m):
    n = m.shape[0]
    eye = (_iota((n, n), 0) == _iota((n, n), 1)).astype(F32)
    p = -m
    x = eye + p
    for _ in range(int(math.log2(n)) - 1):
        p = _dot(p, p)
        x = x + _dot(x, p)
    return x


def _gdn_chunk(states, qkv, small, z, a_row, dt_row, nw):
    c = qkv.shape[0]
    kw = GDN_HEADS * GDN_DK
    g_all = -jnp.exp(a_row) * _softplus(small + dt_row)
    beta_all = jax.nn.sigmoid(small)
    incl, strict = _tril(c), _tril(c, True)
    big_g_all = _dot(incl.astype(F32), g_all)
    outs, new_states = [], []
    for h in range(GDN_HEADS):
        sl = slice(h * GDN_DK, (h + 1) * GDN_DK)
        q = qkv[:, sl]
        k = qkv[:, kw + h * GDN_DK:kw + (h + 1) * GDN_DK]
        v = qkv[:, 2 * kw + h * GDN_DK:2 * kw + (h + 1) * GDN_DK]
        q = q * lax.rsqrt(jnp.sum(q * q, axis=-1, keepdims=True) + EPS) * (GDN_DK ** -0.5)
        k = k * lax.rsqrt(jnp.sum(k * k, axis=-1, keepdims=True) + EPS)
        gc = _lane_col(big_g_all, SM_A + h)
        bc = _lane_col(beta_all, SM_B + h)
        g_last = jnp.sum(_lane_col(g_all, SM_A + h), axis=0, keepdims=True)
        diff = gc - _col_to_row(gc)
        decay = jnp.where(incl, jnp.exp(jnp.where(incl, diff, 0.0)), 0.0)
        kb = k * bc
        m = jnp.where(strict, _bdot_nt(kb, k) * decay, 0.0)
        tinv = _unit_lower_inverse(m)
        eg = jnp.exp(gc)
        u = _dot(tinv, v * bc)
        w = _dot(tinv, kb * eg)
        attn = _bdot_nt(q, k) * decay
        st = states[h]
        v_new = u - _bdot(w, st)
        o = _bdot(q * eg, st) + _bdot(attn, v_new)
        new_states.append(st * jnp.exp(g_last) + _bdot_tn(k * jnp.exp(g_last - gc), v_new))
        outs.append(_rms(o, nw) * _silu(z[:, sl]))
    return new_states, jnp.concatenate(outs, axis=1)


def _gdn_specs(c):
    row = lambda w, blk: pl.BlockSpec((None, c, w), lambda b, n, blk=blk: (b, n, blk))
    prm = pl.BlockSpec((1, LANES), lambda b, n: (0, 0))
    return [row(1536, 0), row(LANES, P_SMALL // LANES), row(512, P_GZ // 512), prm, prm, prm]


def _state_spec():
    return pl.BlockSpec((None, None, 4, LANES, LANES), lambda b, n: (b, n, 0, 0, 0))


def _gdn_fwd(qkv_act, p, a_row, dt_row, nw, name):
    bsz, s, _ = qkv_act.shape
    c = GDN_CHUNK
    nc = s // c

    def body(qkv_ref, sm_ref, z_ref, a_ref, dt_ref, nw_ref, o_ref, st_ref, st_scr):
        @pl.when(pl.program_id(1) == 0)
        def _():
            st_scr[...] = jnp.zeros_like(st_scr)

        st_ref[...] = st_scr[...]
        states = [st_scr[h] for h in range(GDN_HEADS)]
        new_states, o = _gdn_chunk(states, qkv_ref[...], sm_ref[...], z_ref[...], a_ref[...], dt_ref[...], nw_ref[...])
        for h in range(GDN_HEADS):
            st_scr[h] = new_states[h]
        o_ref[...] = o.astype(BF16)

    return _call(
        body, name, (bsz, nc), _gdn_specs(c),
        [pl.BlockSpec((None, c, 512), lambda b, n: (b, n, 0)), _state_spec()],
        [_sds((bsz, s, 512), BF16), _sds((bsz, nc, 4, LANES, LANES), F32)],
        scratch=[pltpu.VMEM((4, LANES, LANES), F32)],
    )(qkv_act, p, p, a_row, dt_row, nw)


def _gdn_bwd(do, qkv_act, p, a_row, dt_row, nw, st_all, name):
    bsz, s, _ = qkv_act.shape
    c = GDN_CHUNK
    nc = s // c

    def body(qkv_ref, sm_ref, z_ref, a_ref, dt_ref, nw_ref, do_ref, st_ref,
             dqkv_ref, dsm_ref, dz_ref, da_ref, ddt_ref, dnw_ref, ds_scr):
        first = (pl.program_id(0) == 0) & (pl.program_id(1) == 0)

        @pl.when(pl.program_id(1) == 0)
        def _():
            ds_scr[...] = jnp.zeros_like(ds_scr)

        @pl.when(first)
        def _():
            da_ref[...] = jnp.zeros_like(da_ref)
            ddt_ref[...] = jnp.zeros_like(ddt_ref)
            dnw_ref[...] = jnp.zeros_like(dnw_ref)

        states = [st_ref[h] for h in range(GDN_HEADS)]
        _, vjp = jax.vjp(_gdn_chunk, states, qkv_ref[...], sm_ref[...], z_ref[...], a_ref[...], dt_ref[...], nw_ref[...])
        d_states, dqkv, dsm, dz, da, ddt, dnw = vjp(([ds_scr[h] for h in range(GDN_HEADS)], do_ref[...]))
        for h in range(GDN_HEADS):
            ds_scr[h] = d_states[h]
        dqkv_ref[...] = dqkv
        dsm_ref[...] = dsm
        dz_ref[...] = dz.astype(BF16)
        da_ref[...] += da
        ddt_ref[...] += ddt
        dnw_ref[...] += dnw

    rrow = lambda w, blk: pl.BlockSpec((None, c, w), lambda b, n, blk=blk: (b, nc - 1 - n, blk))
    prm = pl.BlockSpec((1, LANES), lambda b, n: (0, 0))
    return _call(
        body, name, (bsz, nc),
        [rrow(1536, 0), rrow(LANES, P_SMALL // LANES), rrow(512, P_GZ // 512), prm, prm, prm, rrow(512, 0),
         pl.BlockSpec((None, None, 4, LANES, LANES), lambda b, n: (b, nc - 1 - n, 0, 0, 0))],
        [rrow(1536, 0), rrow(LANES, 0), rrow(512, 0), prm, prm, prm],
        [_sds((bsz, s, 1536), F32), _sds((bsz, s, LANES), F32), _sds((bsz, s, 512), BF16)] + [_sds((1, LANES), F32)] * 3,
        scratch=[pltpu.VMEM((4, LANES, LANES), F32)],
    )(qkv_act, p, p, a_row, dt_row, nw, do, st_all)


def _hgrn_block(states, q_raw, f_raw, i_raw, g_raw, lb, nw):
    n = q_raw.shape[0]
    c = HGRN_CHUNK
    r, cc = _iota((n, n), 0), _iota((n, n), 1)
    same = (r // c) == (cc // c)
    causal = same & (r >= cc)
    ref_row = (r // c) * c + (c // 2 - 1)
    run_sum = causal.astype(F32)
    rel_sum = run_sum - (same & (ref_row >= cc)).astype(F32)
    sums = jnp.concatenate([run_sum, rel_sum, same.astype(F32)], axis=0)
    outs, new_states = [], []
    for h in range(HGRN_HEADS):
        sl = slice(h * HGRN_DK, (h + 1) * HGRN_DK)
        fr, lbh = f_raw[:, sl], lb[:, sl]
        q = _silu(q_raw[:, sl])
        logf = jnp.log(lbh + (1.0 - lbh) * jax.nn.sigmoid(fr))
        k = (1.0 - lbh) * jax.nn.sigmoid(-fr)
        v = i_raw[:, sl]
        all_sums = _dot(sums, logf)
        big_g, g_rel, g_tot = all_sums[:n], all_sums[n:2 * n], all_sums[2 * n:]
        scores = _bdot_nt(q * jnp.exp(g_rel), k * jnp.exp(-g_rel))
        o_intra = _bdot(jnp.where(causal, scores, 0.0), v)
        qg = q * jnp.exp(big_g)
        k_end = k * jnp.exp(g_tot - big_g)
        st = states[h]
        parts = []
        for j in range(n // c):
            rows = slice(j * c, (j + 1) * c)
            parts.append(_bdot_nt(qg[rows], st))
            st = st * jnp.exp(g_tot[j * c:j * c + 1]) + _bdot_tn(v[rows], k_end[rows])
        new_states.append(st)
        o = o_intra + jnp.concatenate(parts, axis=0)
        outs.append(_rms(o, nw) * _silu(g_raw[:, sl]))
    return new_states, jnp.concatenate(outs, axis=1)


def _hgrn_fwd(p, lb, nw, name):
    bsz, s, _ = p.shape
    n = HGRN_BLOCK
    nb = s // n

    def body(q_ref, f_ref, i_ref, g_ref, lb_ref, nw_ref, o_ref, st_ref, st_scr):
        @pl.when(pl.program_id(1) == 0)
        def _():
            st_scr[...] = jnp.zeros_like(st_scr)

        st_ref[...] = st_scr[...]
        states = [st_scr[h] for h in range(HGRN_HEADS)]
        new_states, o = _hgrn_block(states, q_ref[...], f_ref[...], i_ref[...], g_ref[...], lb_ref[...], nw_ref[...])
        for h in range(HGRN_HEADS):
            st_scr[h] = new_states[h]
        o_ref[...] = o.astype(BF16)

    row = lambda blk: pl.BlockSpec((None, n, 512), lambda b, i, blk=blk: (b, i, blk))
    return _call(
        body, name, (bsz, nb),
        [row(P_HQ // 512), row(P_HF // 512), row(P_HI // 512), row(P_HG // 512),
         pl.BlockSpec((1, 512), lambda b, i: (0, 0)), pl.BlockSpec((1, LANES), lambda b, i: (0, 0))],
        [row(0), _state_spec()],
        [_sds((bsz, s, 512), BF16), _sds((bsz, nb, 4, LANES, LANES), F32)],
        scratch=[pltpu.VMEM((4, LANES, LANES), F32)],
    )(p, p, p, p, lb, nw)


def _hgrn_bwd(do, p, lb, nw, st_all, name):
    bsz, s, _ = p.shape
    n = HGRN_BLOCK
    nb = s // n

    def body(q_ref, f_ref, i_ref, g_ref, lb_ref, nw_ref, do_ref, st_ref, dp_ref, dlb_ref, dnw_ref, ds_scr):
        first = (pl.program_id(0) == 0) & (pl.program_id(1) == 0)

        @pl.when(pl.program_id(1) == 0)
        def _():
            ds_scr[...] = jnp.zeros_like(ds_scr)

        @pl.when(first)
        def _():
            dlb_ref[...] = jnp.zeros_like(dlb_ref)
            dnw_ref[...] = jnp.zeros_like(dnw_ref)

        states = [st_ref[h] for h in range(HGRN_HEADS)]
        _, vjp = jax.vjp(_hgrn_block, states, q_ref[...], f_ref[...], i_ref[...], g_ref[...], lb_ref[...], nw_ref[...])
        d_states, dq, df, di, dg, dlb, dnw = vjp(([ds_scr[h] for h in range(HGRN_HEADS)], do_ref[...]))
        for h in range(HGRN_HEADS):
            ds_scr[h] = d_states[h]
        for j, t in enumerate((dq, df, di, dg)):
            dp_ref[:, j * 512:(j + 1) * 512] = t.astype(BF16)
        dlb_ref[...] += dlb
        dnw_ref[...] += dnw

    row = lambda blk: pl.BlockSpec((None, n, 512), lambda b, i, blk=blk: (b, nb - 1 - i, blk))
    return _call(
        body, name, (bsz, nb),
        [row(P_HQ // 512), row(P_HF // 512), row(P_HI // 512), row(P_HG // 512),
         pl.BlockSpec((1, 512), lambda b, i: (0, 0)), pl.BlockSpec((1, LANES), lambda b, i: (0, 0)), row(0),
         pl.BlockSpec((None, None, 4, LANES, LANES), lambda b, i: (b, nb - 1 - i, 0, 0, 0))],
        [pl.BlockSpec((None, n, 2048), lambda b, i: (b, nb - 1 - i, 0)),
         pl.BlockSpec((1, 512), lambda b, i: (0, 0)), pl.BlockSpec((1, LANES), lambda b, i: (0, 0))],
        [_sds((bsz, s, 2048), BF16), _sds((1, 512), F32), _sds((1, LANES), F32)],
        scratch=[pltpu.VMEM((4, LANES, LANES), F32)],
    )(p, p, p, p, lb, nw, do, st_all)


def _ssd_chunk(states, xbc, small, z, a_row, dt_row, d_row, nw):
    c = xbc.shape[0]
    incl = _tril(c)
    dt_all = _softplus(small + dt_row)
    da_all = dt_all * (-jnp.exp(a_row))
    spread = (_iota((LANES, SSD_INNER), 0) == SM_DT + _iota((LANES, SSD_INNER), 1) // SSD_HEAD_DIM).astype(F32)
    both = _dot(jnp.concatenate([dt_all, da_all], axis=0), spread)
    dt_e, da_e = both[:c], both[c:]
    acs_e = _dot(incl.astype(F32), da_e)
    last_e = jnp.sum(da_e, axis=0, keepdims=True)
    xs = xbc[:, :SSD_INNER]
    xdt = xs * dt_e
    gw = SSD_GROUPS * SSD_STATE
    lane = _iota((1, LANES), 1)
    ys, new_states = [], []
    for j in range(4):
        g = j // 2
        bg = xbc[:, SSD_INNER + g * SSD_STATE:SSD_INNER + (g + 1) * SSD_STATE]
        cg = xbc[:, SSD_INNER + gw + g * SSD_STATE:SSD_INNER + gw + (g + 1) * SSD_STATE]
        cb = _bdot_nt(cg, bg)
        sl = slice(j * LANES, (j + 1) * LANES)
        xblk, acs, last = xdt[:, sl], acs_e[:, sl], last_e[:, sl]
        y = None
        for sub in range(2):
            ac = acs[:, sub * SSD_HEAD_DIM:sub * SSD_HEAD_DIM + 1]
            seg = jnp.where(incl, jnp.exp(jnp.where(incl, ac - _col_to_row(ac), 0.0)), 0.0)
            mine = ((lane // SSD_HEAD_DIM) == sub).astype(F32)
            t = _bdot(cb * seg, xblk * mine)
            y = t if y is None else y + t
        st = states[j]
        y = y + _bdot(cg, st) * jnp.exp(acs)
        new_states.append(st * jnp.exp(last) + _bdot_tn(bg, xblk * jnp.exp(last - acs)))
        ys.append(y + d_row[:, sl] * xs[:, sl])
    yz = jnp.concatenate(ys, axis=1) * _silu(z)
    gwid = SSD_INNER // SSD_GROUPS
    outs = [_rms(yz[:, g * gwid:(g + 1) * gwid], nw[:, g * gwid:(g + 1) * gwid]) for g in range(SSD_GROUPS)]
    return new_states, jnp.concatenate(outs, axis=1)


def _ssd_fwd(xbc_act, p, a_row, dt_row, d_row, nw, name):
    bsz, s, _ = xbc_act.shape
    c = SSD_CHUNK
    nc = s // c

    def body(x_ref, sm_ref, z_ref, a_ref, dt_ref, d_ref, nw_ref, o_ref, st_ref, st_scr):
        @pl.when(pl.program_id(1) == 0)
        def _():
            st_scr[...] = jnp.zeros_like(st_scr)

        st_ref[...] = st_scr[...]
        states = [st_scr[h] for h in range(4)]
        new_states, o = _ssd_chunk(states, x_ref[...], sm_ref[...], z_ref[...], a_ref[...], dt_ref[...], d_ref[...], nw_ref[...])
        for h in range(4):
            st_scr[h] = new_states[h]
        o_ref[...] = o.astype(BF16)

    row = lambda w, blk: pl.BlockSpec((None, c, w), lambda b, n, blk=blk: (b, n, blk))
    prm = pl.BlockSpec((1, LANES), lambda b, n: (0, 0))
    prm5 = pl.BlockSpec((1, 512), lambda b, n: (0, 0))
    return _call(
        body, name, (bsz, nc),
        [row(1024, 0), row(LANES, P_SMALL // LANES), row(512, P_SZ // 512), prm, prm, prm5, prm5],
        [row(512, 0), _state_spec()],
        [_sds((bsz, s, 512), BF16), _sds((bsz, nc, 4, LANES, LANES), F32)],
        scratch=[pltpu.VMEM((4, LANES, LANES), F32)],
    )(xbc_act, p, p, a_row, dt_row, d_row, nw)


def _ssd_bwd(do, xbc_act, p, a_row, dt_row, d_row, nw, st_all, name):
    bsz, s, _ = xbc_act.shape
    c = SSD_CHUNK
    nc = s // c

    def body(x_ref, sm_ref, z_ref, a_ref, dt_ref, d_ref, nw_ref, do_ref, st_ref,
             dx_ref, dsm_ref, dz_ref, da_ref, ddt_ref, dd_ref, dnw_ref, ds_scr):
        first = (pl.program_id(0) == 0) & (pl.program_id(1) == 0)

        @pl.when(pl.program_id(1) == 0)
        def _():
            ds_scr[...] = jnp.zeros_like(ds_scr)

        @pl.when(first)
        def _():
            da_ref[...] = jnp.zeros_like(da_ref)
            ddt_ref[...] = jnp.zeros_like(ddt_ref)
            dd_ref[...] = jnp.zeros_like(dd_ref)
            dnw_ref[...] = jnp.zeros_like(dnw_ref)

        states = [st_ref[h] for h in range(4)]
        _, vjp = jax.vjp(_ssd_chunk, states, x_ref[...], sm_ref[...], z_ref[...], a_ref[...], dt_ref[...], d_ref[...], nw_ref[...])
        d_states, dx, dsm, dz, da, ddt, dd, dnw = vjp(([ds_scr[h] for h in range(4)], do_ref[...]))
        for h in range(4):
            ds_scr[h] = d_states[h]
        dx_ref[...] = dx
        dsm_ref[...] = dsm
        dz_ref[...] = dz.astype(BF16)
        da_ref[...] += da
        ddt_ref[...] += ddt
        dd_ref[...] += dd
        dnw_ref[...] += dnw

    row = lambda w, blk: pl.BlockSpec((None, c, w), lambda b, n, blk=blk: (b, nc - 1 - n, blk))
    prm = pl.BlockSpec((1, LANES), lambda b, n: (0, 0))
    prm5 = pl.BlockSpec((1, 512), lambda b, n: (0, 0))
    return _call(
        body, name, (bsz, nc),
        [row(1024, 0), row(LANES, P_SMALL // LANES), row(512, P_SZ // 512), prm, prm, prm5, prm5, row(512, 0),
         pl.BlockSpec((None, None, 4, LANES, LANES), lambda b, n: (b, nc - 1 - n, 0, 0, 0))],
        [row(1024, 0), row(LANES, 0), row(512, 0), prm, prm, prm5, prm5],
        [_sds((bsz, s, 1024), F32), _sds((bsz, s, LANES), F32), _sds((bsz, s, 512), BF16),
         _sds((1, LANES), F32), _sds((1, LANES), F32), _sds((1, 512), F32), _sds((1, 512), F32)],
        scratch=[pltpu.VMEM((4, LANES, LANES), F32)],
    )(xbc_act, p, p, a_row, dt_row, d_row, nw, do, st_all)


def _peer(k):
    x, y, c = lax.axis_index("x"), lax.axis_index("y"), lax.axis_index("c")
    px = 1 - x if k & 4 else x
    py = 1 - y if k & 2 else y
    pc = 1 - c if k & 1 else c
    return (px, py, pc), 4 * px + 2 * py + pc


def _my_index():
    return 4 * lax.axis_index("x") + 2 * lax.axis_index("y") + lax.axis_index("c")


def _exchange(arrays, name, gather):
    n = len(arrays)

    def body(*refs):
        ins, outs = refs[:n], refs[n:2 * n]
        send_sems, recv_sems, local_sems = refs[2 * n:]
        me = _my_index()

        def copy(i, k):
            peer, slot = _peer(k)
            src = ins[i] if gather else ins[i].at[slot]
            return pltpu.make_async_remote_copy(src_ref=src, dst_ref=outs[i].at[me], send_sem=send_sems.at[k - 1, i],
                                                recv_sem=recv_sems.at[k - 1, i], device_id=peer, device_id_type=MESH_ID)

        def arrival(i, k):
            peer, slot = _peer(k)
            src = ins[i] if gather else ins[i].at[slot]
            return pltpu.make_async_remote_copy(src_ref=src, dst_ref=outs[i].at[slot], send_sem=send_sems.at[k - 1, i],
                                                recv_sem=recv_sems.at[k - 1, i], device_id=peer, device_id_type=MESH_ID)

        mine = [pltpu.make_async_copy(ins[i] if gather else ins[i].at[me], outs[i].at[me], local_sems.at[i])
                for i in range(n)]
        sends = [copy(i, k) for k in range(1, N_DEV) for i in range(n)]
        for cp in mine + sends:
            cp.start()
        for k in range(1, N_DEV):
            for i in range(n):
                arrival(i, k).wait_recv()
        for cp in sends:
            cp.wait_send()
        for cp in mine:
            cp.wait()

    out_shape = [_sds(((N_DEV,) + a.shape) if gather else a.shape, a.dtype) for a in arrays]
    any_spec = pl.BlockSpec(memory_space=pl.ANY)
    return pl.pallas_call(
        body, name=name, out_shape=out_shape, in_specs=[any_spec] * n, out_specs=[any_spec] * n,
        scratch_shapes=[pltpu.SemaphoreType.DMA((N_DEV - 1, n)), pltpu.SemaphoreType.DMA((N_DEV - 1, n)),
                        pltpu.SemaphoreType.DMA((n,))],
    )(*arrays)


def _sum_adamw(gs, w, m, v, name):
    rows, width = w.shape
    slots = gs.shape[0]
    tr = _pick(rows, (128, 64, 32, 16, 8)) if rows % 8 == 0 else rows

    def body(g_ref, w_ref, m_ref, v_ref, go_ref, d_ref, mo_ref, vo_ref):
        g = g_ref[0]
        for i in range(1, slots):
            g = g + g_ref[i]
        m2 = ADAM_B1 * m_ref[...] + (1.0 - ADAM_B1) * g
        v2 = ADAM_B2 * v_ref[...] + (1.0 - ADAM_B2) * (g * g)
        m_hat = m2 / (1.0 - ADAM_B1 ** ADAM_STEP)
        v_hat = v2 / (1.0 - ADAM_B2 ** ADAM_STEP)
        go_ref[...] = g
        d_ref[...] = -ADAM_LR * (m_hat / (jnp.sqrt(v_hat) + ADAM_EPS) + ADAM_WD * w_ref[...])
        mo_ref[...] = m2
        vo_ref[...] = v2

    flat = pl.BlockSpec((tr, width), lambda i: (i, 0))
    return _call(body, name, (rows // tr,), [pl.BlockSpec((slots, tr, width), lambda i: (0, i, 0)), flat, flat, flat],
                 [flat] * 4, [_sds(w.shape, F32)] * 4)(gs, w, m, v)


MATMUL_WEIGHTS = ("w_in", "w_br_a", "w_br_b", "w_br_c", "w_out", "ffn_w_up", "ffn_w_down")
SPLIT = (
    ("w_in", (DEPTH, D_MODEL, 8720), 2),
    ("gdn_conv_w", (DEPTH, 4, 1536), 2), ("ssd_conv_w", (DEPTH, 4, 1024), 2),
    ("w_br_a", (DEPTH, 512, D_MODEL), 2), ("w_br_b", (DEPTH, 512, D_MODEL), 2), ("w_br_c", (DEPTH, 512, D_MODEL), 2),
    ("w_out", (DEPTH, D_MODEL, D_MODEL), 1), ("ffn_w_up", (DEPTH, D_MODEL, 2 * FFN_HIDDEN), 2),
    ("ffn_conv_w", (DEPTH, 3, 2 * FFN_HIDDEN), 2), ("ffn_w_down", (DEPTH, FFN_HIDDEN, D_MODEL), 1),
)
REPL = (
    ("b_ada", (DEPTH, 6 * D_MODEL)), ("norm1_w", (DEPTH, D_MODEL)), ("gdn_a_log", (DEPTH, 4)),
    ("gdn_dt_bias", (DEPTH, 4)), ("gdn_norm_w", (DEPTH, 128)), ("hgrn_lb_param", (DEPTH, 512)),
    ("hgrn_norm_w", (DEPTH, 128)), ("ssd_conv_b", (DEPTH, 1024)), ("ssd_a_log", (DEPTH, 8)),
    ("ssd_dt_bias", (DEPTH, 8)), ("ssd_d", (DEPTH, 8)), ("ssd_norm_w", (DEPTH, 512)), ("norm2_w", (DEPTH, D_MODEL)),
    ("ffn_conv_b", (DEPTH, 2 * FFN_HIDDEN)), ("final_norm_w", (D_MODEL,)),
)
WEIGHTS = ("w_ada", "b_ada", "norm1_w", "w_in", "gdn_conv_w", "gdn_a_log", "gdn_dt_bias", "gdn_norm_w",
           "hgrn_lb_param", "hgrn_norm_w", "ssd_conv_w", "ssd_conv_b", "ssd_a_log", "ssd_dt_bias", "ssd_d",
           "ssd_norm_w", "w_br_a", "w_br_b", "w_br_c", "w_out", "norm2_w", "ffn_w_up", "ffn_conv_w", "ffn_conv_b",
           "ffn_w_down", "final_norm_w")


def _block_shape(shape, axis):
    return tuple(d // N_DEV if i == axis else d for i, d in enumerate(shape))


def _join_blocks(gathered, shape, axis):
    return jnp.moveaxis(gathered, 0, axis).reshape(shape)


def _split_blocks(full, shape, axis):
    bs = _block_shape(shape, axis)
    t = full.reshape(shape[:axis] + (N_DEV, bs[axis]) + shape[axis + 1:])
    return jnp.moveaxis(t, axis, 0)


def _pack_repl(vals):
    parts = []
    for n, shape in REPL:
        size = math.prod(shape)
        parts.append(jnp.pad(vals[n].reshape(-1), (0, -(-size // PACK_W) * PACK_W - size)))
    cat = jnp.concatenate(parts)
    rows = -(-cat.shape[0] // (8 * PACK_W)) * 8
    return jnp.pad(cat, (0, rows * PACK_W - cat.shape[0])).reshape(rows, PACK_W)


def _unpack_repl(packed):
    flat, out, off = packed.reshape(-1), {}, 0
    for n, shape in REPL:
        size = math.prod(shape)
        out[n] = flat[off:off + size].reshape(shape)
        off += -(-size // PACK_W) * PACK_W
    return out


def _lane_row(vec, lane0):
    return jnp.pad(vec, (lane0, LANES - lane0 - vec.shape[0]))[None]


def _arrange_w_in(w):
    offs = [0]
    for sz in W_IN_SPLITS:
        offs.append(offs[-1] + sz)
    qkv, a, b, gz, hq, hf, hi, hg, sz_, xbc, dt, gate = [w[:, offs[i]:offs[i + 1]] for i in range(12)]
    pad = jnp.zeros((w.shape[0], P_WIDTH - P_SMALL - 16), w.dtype)
    return jnp.concatenate([qkv, gz, xbc, gate, hq, hf, hi, hg, sz_, a, b, dt, pad], axis=1)


def _restore_w_in(wp):
    cut = lambda o, n: wp[:, o:o + n]
    return jnp.concatenate([
        cut(P_QKV, 1536), cut(P_SMALL + SM_A, 4), cut(P_SMALL + SM_B, 4), cut(P_GZ, 512), cut(P_HQ, 512),
        cut(P_HF, 512), cut(P_HI, 512), cut(P_HG, 512), cut(P_SZ, 512), cut(P_XBC, 1024), cut(P_SMALL + SM_DT, 8),
        cut(P_GATE, 3072)], axis=1)


def _layer_consts(l, wf, wr, lower):
    t = lambda a: a.T
    k = {}
    k["n1w"], k["n2w"] = wr["norm1_w"][l][None], wr["norm2_w"][l][None]
    win = _arrange_w_in(wf["w_in"][l])
    k["win"], k["win_t"] = win, t(win)
    for n in ("w_br_a", "w_br_b", "w_br_c", "w_out", "ffn_w_up", "ffn_w_down"):
        k[n], k[n + "_t"] = wf[n][l], t(wf[n][l])
    k["gdn_conv_w"], k["gdn_conv_b"] = wf["gdn_conv_w"][l], jnp.zeros((1, 1536), F32)
    k["ssd_conv_w"], k["ssd_conv_b"] = wf["ssd_conv_w"][l], wr["ssd_conv_b"][l][None]
    k["ffn_conv_w"], k["ffn_conv_b"] = wf["ffn_conv_w"][l], wr["ffn_conv_b"][l][None]
    k["gdn_a"], k["gdn_dt"] = _lane_row(wr["gdn_a_log"][l], SM_A), _lane_row(wr["gdn_dt_bias"][l], SM_A)
    k["gdn_nw"], k["hgrn_nw"] = wr["gdn_norm_w"][l][None], wr["hgrn_norm_w"][l][None]
    k["ssd_a"], k["ssd_dt"] = _lane_row(wr["ssd_a_log"][l], SM_DT), _lane_row(wr["ssd_dt_bias"][l], SM_DT)
    k["ssd_d"] = jnp.repeat(wr["ssd_d"][l], SSD_HEAD_DIM)[None]
    k["ssd_nw"] = wr["ssd_norm_w"][l][None]
    k["lb"] = lower[l:l + 1]
    return k


def _layer_fwd(l, x, mod, k):
    bsz, s, d = x.shape
    t = bsz * s
    sv = {"x": x}
    sv["mod"] = [mod[:, None, i * d:(i + 1) * d] for i in range(6)]
    sh1, sc1, g1, sh2, sc2, g2 = sv["mod"]
    h1 = _norm_mod_fwd(x, k["n1w"], sh1, sc1, f"norm1_fwd{l}")
    p = _mm(h1.reshape(t, d), k["win"], F32, f"mm_in{l}").reshape(bsz, s, P_WIDTH)
    qkv_act = _conv_fwd(p, P_QKV, 1536, k["gdn_conv_w"], k["gdn_conv_b"], True, f"gdn_conv_fwd{l}")
    oa, st_a = _gdn_fwd(qkv_act, p, k["gdn_a"], k["gdn_dt"], k["gdn_nw"], f"gdn_fwd{l}")
    ob, st_b = _hgrn_fwd(p, k["lb"], k["hgrn_nw"], f"hgrn_fwd{l}")
    xbc_act = _conv_fwd(p, P_XBC, 1024, k["ssd_conv_w"], k["ssd_conv_b"], True, f"ssd_conv_fwd{l}")
    oc, st_c = _ssd_fwd(xbc_act, p, k["ssd_a"], k["ssd_dt"], k["ssd_d"], k["ssd_nw"], f"ssd_fwd{l}")
    merged = _merge_fwd(p, oa, ob, oc, k["w_br_a"], k["w_br_b"], k["w_br_c"], f"merge_fwd{l}")
    mix = _mm(merged.reshape(t, d), k["w_out"], F32, f"mm_out{l}").reshape(bsz, s, d)
    x1 = _resid_fwd(x, mix, g1, f"resid1_fwd{l}")
    h2 = _norm_mod_fwd(x1, k["n2w"], sh2, sc2, f"norm2_fwd{l}")
    u_pre = _mm(h2.reshape(t, d), k["ffn_w_up"], F32, f"mm_up{l}").reshape(bsz, s, 2 * FFN_HIDDEN)
    u = _conv_fwd(u_pre, 0, 2 * FFN_HIDDEN, k["ffn_conv_w"], k["ffn_conv_b"], False, f"ffn_conv_fwd{l}")
    a = _glu_fwd(u, f"glu_fwd{l}")
    ffn = _mm(a.reshape(t, FFN_HIDDEN), k["ffn_w_down"], F32, f"mm_down{l}").reshape(bsz, s, d)
    x2 = _resid_fwd(x1, ffn, g2, f"resid2_fwd{l}")
    sv.update(h1=h1, p=p, qkv_act=qkv_act, oa=oa, st_a=st_a, ob=ob, st_b=st_b, xbc_act=xbc_act, oc=oc, st_c=st_c,
              merged=merged, mix=mix, x1=x1, h2=h2, u_pre=u_pre, u=u, a=a, ffn=ffn)
    return x2, sv


def _layer_bwd(l, dx2, k, sv):
    bsz, s, d = dx2.shape
    t = bsz * s
    f2 = 2 * FFN_HIDDEN
    sh1, sc1, g1, sh2, sc2, g2 = sv["mod"]
    tr = lambda a: a.reshape(t, -1).T
    g = {}
    dffn, dg2 = _gate_bwd(dx2, sv["ffn"], g2, f"gate2_bwd{l}")
    dffn2 = dffn.reshape(t, d)
    da = _mm(dffn2, k["ffn_w_down_t"], F32, f"mm_down_dx{l}").reshape(bsz, s, FFN_HIDDEN)
    g["ffn_w_down"] = _mm(tr(sv["a"]), dffn2, F32, f"mm_down_dw{l}")
    du = _glu_bwd(da, sv["u"], f"glu_bwd{l}")
    du_pre, g["ffn_conv_w"], dfcb = _conv_bwd(du, sv["u_pre"], 0, f2, k["ffn_conv_w"], k["ffn_conv_b"], False, f"ffn_conv_bwd{l}")
    g["ffn_conv_b"] = dfcb[0]
    du2 = du_pre.reshape(t, f2)
    dh2 = _mm(du2, k["ffn_w_up_t"], F32, f"mm_up_dx{l}").reshape(bsz, s, d)
    g["ffn_w_up"] = _mm(tr(sv["h2"]), du2, F32, f"mm_up_dw{l}")
    dx1, dn2w, dsh2, dsc2 = _norm_mod_bwd(sv["x1"], k["n2w"], sh2, sc2, dh2, dx2, f"norm2_bwd{l}")
    g["norm2_w"] = dn2w[0]
    dmix, dg1 = _gate_bwd(dx1, sv["mix"], g1, f"gate1_bwd{l}")
    dmix2 = dmix.reshape(t, d)
    dmerged = _mm(dmix2, k["w_out_t"], F32, f"mm_out_dx{l}").reshape(bsz, s, d)
    g["w_out"] = _mm(tr(sv["merged"]), dmix2, F32, f"mm_out_dw{l}")
    p = sv["p"]
    dgate, doa, dob, doc, dya, dyb, dyc = _merge_bwd(
        dmerged, p, sv["oa"], sv["ob"], sv["oc"], k["w_br_a"], k["w_br_b"], k["w_br_c"],
        k["w_br_a_t"], k["w_br_b_t"], k["w_br_c_t"], f"merge_bwd{l}")
    g["w_br_a"] = _mm(tr(sv["oa"]), dya.reshape(t, d), F32, f"mm_bra_dw{l}")
    g["w_br_b"] = _mm(tr(sv["ob"]), dyb.reshape(t, d), F32, f"mm_brb_dw{l}")
    g["w_br_c"] = _mm(tr(sv["oc"]), dyc.reshape(t, d), F32, f"mm_brc_dw{l}")
    dxbc_act, dsm_c, dsz, da_c, ddt_c, dd_c, dnw_c = _ssd_bwd(
        doc, sv["xbc_act"], p, k["ssd_a"], k["ssd_dt"], k["ssd_d"], k["ssd_nw"], sv["st_c"], f"ssd_bwd{l}")
    dxbc_raw, g["ssd_conv_w"], dscb = _conv_bwd(dxbc_act, p, P_XBC, 1024, k["ssd_conv_w"], k["ssd_conv_b"], True, f"ssd_conv_bwd{l}")
    g["ssd_conv_b"] = dscb[0]
    g["ssd_a_log"], g["ssd_dt_bias"] = da_c[0, SM_DT:SM_DT + 8], ddt_c[0, SM_DT:SM_DT + 8]
    g["ssd_d"] = dd_c.reshape(SSD_HEADS, SSD_HEAD_DIM).sum(axis=1)
    g["ssd_norm_w"] = dnw_c[0]
    dhg, dlb, dnw_b = _hgrn_bwd(dob, p, k["lb"], k["hgrn_nw"], sv["st_b"], f"hgrn_bwd{l}")
    g["hgrn_norm_w"] = dnw_b[0]
    dqkv_act, dsm_a, dgz, da_a, ddt_a, dnw_a = _gdn_bwd(
        doa, sv["qkv_act"], p, k["gdn_a"], k["gdn_dt"], k["gdn_nw"], sv["st_a"], f"gdn_bwd{l}")
    dqkv_raw, g["gdn_conv_w"], _ = _conv_bwd(dqkv_act, p, P_QKV, 1536, k["gdn_conv_w"], k["gdn_conv_b"], True, f"gdn_conv_bwd{l}")
    g["gdn_a_log"], g["gdn_dt_bias"], g["gdn_norm_w"] = da_a[0, :4], ddt_a[0, :4], dnw_a[0]
    dsmall = jnp.pad((dsm_a + dsm_c).astype(BF16), ((0, 0), (0, 0), (0, P_WIDTH - P_SMALL - LANES)))
    dp = jnp.concatenate([dqkv_raw, dgz, dxbc_raw, dgate, dhg, dsz, dsmall], axis=-1).reshape(t, P_WIDTH)
    dh1 = _mm(dp, k["win_t"], F32, f"mm_in_dx{l}").reshape(bsz, s, d)
    g["w_in"] = _restore_w_in(_mm(tr(sv["h1"]), dp, F32, f"mm_in_dw{l}"))
    dx, dn1w, dsh1, dsc1 = _norm_mod_bwd(sv["x"], k["n1w"], sh1, sc1, dh1, dx1, f"norm1_bwd{l}")
    g["norm1_w"] = dn1w[0]
    dmod = jnp.concatenate([dsh1, dsc1, dg1, dsh2, dsc2, dg2], axis=-1)[:, 0]
    return dx, g, dlb, dmod


def _local_step(x, mod, wf, wr, target):
    lower = _lb_fwd(wr["hgrn_lb_param"])
    ks = [_layer_consts(l, wf, wr, lower) for l in range(DEPTH)]
    saved = []
    h = x
    for l in range(DEPTH):
        h, sv = _layer_fwd(l, h, mod[l], ks[l])
        saved.append(sv)
    loss8, dh, dfnw = _final_loss(h, wr["final_norm_w"][None], target)
    per_layer, dlbs, dmods = [None] * DEPTH, [None] * DEPTH, [None] * DEPTH
    for l in reversed(range(DEPTH)):
        dh, per_layer[l], dlbs[l], dmods[l] = _layer_bwd(l, dh, ks[l], saved[l])
    grads = {n: jnp.stack([per_layer[l][n] for l in range(DEPTH)]) for n in per_layer[0]}
    grads["hgrn_lb_param"] = _lb_bwd(wr["hgrn_lb_param"], jnp.concatenate(dlbs, axis=0))
    grads["final_norm_w"] = dfnw[0]
    return loss8[0, 0], dh, grads, jnp.stack(dmods)


def kernel(x, c, w_ada, b_ada, norm1_w, w_in, gdn_conv_w, gdn_a_log, gdn_dt_bias, gdn_norm_w, hgrn_lb_param, hgrn_norm_w, ssd_conv_w, ssd_conv_b, ssd_a_log, ssd_dt_bias, ssd_d, ssd_norm_w, w_br_a, w_br_b, w_br_c, w_out, norm2_w, ffn_w_up, ffn_conv_w, ffn_conv_b, ffn_w_down, final_norm_w, loss_target, m_w_ada, m_b_ada, m_norm1_w, m_w_in, m_gdn_conv_w, m_gdn_a_log, m_gdn_dt_bias, m_gdn_norm_w, m_hgrn_lb_param, m_hgrn_norm_w, m_ssd_conv_w, m_ssd_conv_b, m_ssd_a_log, m_ssd_dt_bias, m_ssd_d, m_ssd_norm_w, m_w_br_a, m_w_br_b, m_w_br_c, m_w_out, m_norm2_w, m_ffn_w_up, m_ffn_conv_w, m_ffn_conv_b, m_ffn_w_down, m_final_norm_w, v_w_ada, v_b_ada, v_norm1_w, v_w_in, v_gdn_conv_w, v_gdn_a_log, v_gdn_dt_bias, v_gdn_norm_w, v_hgrn_lb_param, v_hgrn_norm_w, v_ssd_conv_w, v_ssd_conv_b, v_ssd_a_log, v_ssd_dt_bias, v_ssd_d, v_ssd_norm_w, v_w_br_a, v_w_br_b, v_w_br_c, v_w_out, v_norm2_w, v_ffn_w_up, v_ffn_conv_w, v_ffn_conv_b, v_ffn_w_down, v_final_norm_w):
    given = dict(locals())
    w = {n: given[n] for n in WEIGHTS}
    m = {n: given["m_" + n] for n in WEIGHTS}
    v = {n: given["v_" + n] for n in WEIGHTS}
    me = _my_index()
    bsz = c.shape[0]
    ncol = 6 * D_MODEL // N_DEV

    shards = [w[n].astype(BF16) if n in MATMUL_WEIGHTS else w[n] for n, _, _ in SPLIT] + [c]
    gathered = _exchange(shards, "gather_weights", True)
    wf = {n: _join_blocks(g, shape, axis) for (n, shape, axis), g in zip(SPLIT, gathered)}
    c_all = gathered[-1].reshape(N_DEV * bsz, D_MODEL)

    b_cols = lax.dynamic_slice_in_dim(b_ada, me * ncol, ncol, axis=1)[:, None]
    mod_cols = _ada_fwd(c_all, w_ada, b_cols)
    send = mod_cols.reshape(DEPTH, N_DEV, bsz, ncol).transpose(1, 0, 2, 3)
    got = _exchange([send], "scatter_mod", False)[0]
    mod = got.transpose(1, 2, 0, 3).reshape(DEPTH, bsz, 6 * D_MODEL)

    loss, dx, grads, dmod = _local_step(x, mod, wf, w, loss_target)

    send = dmod.reshape(DEPTH, bsz, N_DEV, ncol).transpose(2, 0, 1, 3)
    parts = [_split_blocks(grads[n], shape, axis) for n, shape, axis in SPLIT] + [send]
    got = _exchange(parts, "scatter_grads", False)
    dmod_all = got[-1].transpose(1, 0, 2, 3).reshape(DEPTH, N_DEV * bsz, ncol)
    g_w_ada, g_b_cols = _ada_bwd(c_all.T, dmod_all)
    grads["b_ada"] = lax.dynamic_update_slice_in_dim(jnp.zeros_like(b_ada), g_b_cols[:, 0], me * ncol, axis=1)

    out = {}
    slots = [(n, g8) for (n, _, _), g8 in zip(SPLIT, got)] + [("w_ada", g_w_ada[None])]
    for n, gs in slots:
        bs = w[n].shape
        two = lambda a: a.reshape(-1, bs[-1])
        res = _sum_adamw(gs.reshape(gs.shape[0], -1, bs[-1]), two(w[n]), two(m[n]), two(v[n]), f"adamw_{n}")
        out[n] = [r.reshape(bs) for r in res]
    r8 = _exchange([_pack_repl(grads)], "gather_small_grads", True)[0]
    res = _sum_adamw(r8, _pack_repl(w), _pack_repl(m), _pack_repl(v), "adamw_repl")
    repl_out = [_unpack_repl(o) for o in res]
    pick = lambda i, n: out[n][i] if n in out else repl_out[i][n]
    loss = lax.psum(loss, ("x", "y", "c"))
    return (loss, dx, *[pick(i, n) for i in range(4) for n in WEIGHTS])
```

```python
import functools
import math

import jax
import jax.numpy as jnp
from jax import lax
from jax.experimental import pallas as pl
from jax.experimental.pallas import tpu as pltpu

F32, BF16 = jnp.float32, jnp.bfloat16
HI = lax.Precision.HIGHEST
MESH_ID = pl.DeviceIdType.MESH

N_DEV = 8
EPS = 1e-6
D_MODEL = 1024
DEPTH = 2
GDN_HEADS, GDN_DK, GDN_CHUNK = 4, 128, 64
HGRN_HEADS, HGRN_DK, HGRN_CHUNK, HGRN_BLOCK = 4, 128, 16, 128
SSD_HEADS, SSD_HEAD_DIM, SSD_GROUPS, SSD_STATE, SSD_CHUNK = 8, 64, 2, 128, 64
SSD_INNER = SSD_HEADS * SSD_HEAD_DIM
FFN_HIDDEN = 2816
LANES = 128
P_QKV, P_GZ, P_XBC, P_GATE, P_HQ, P_HF, P_HI, P_HG, P_SZ, P_SMALL, P_WIDTH = (
    0, 1536, 2048, 3072, 6144, 6656, 7168, 7680, 8192, 8704, 9216)
SM_A, SM_B, SM_DT = 0, 4, 8
W_IN_SPLITS = (1536, 4, 4, 512, 512, 512, 512, 512, 512, 1024, 8, 3072)

ADAM_LR, ADAM_B1, ADAM_B2, ADAM_EPS, ADAM_WD, ADAM_STEP = 0.001, 0.9, 0.999, 1e-08, 0.01, 10

V7X_VMEM_LIMIT = 56 * 1024 * 1024
PACK_W = 1024
PACK_ROWS = 128


def _call(body, name, grid, in_specs, out_specs, out_shape, scratch=()):
    return pl.pallas_call(
        body, name=name, grid=grid, in_specs=in_specs, out_specs=out_specs, out_shape=out_shape,
        scratch_shapes=list(scratch),
        compiler_params=pltpu.CompilerParams(
            dimension_semantics=("arbitrary",) * len(grid), vmem_limit_bytes=V7X_VMEM_LIMIT),
    )


def _pick(n, cands):
    for c in cands:
        if n % c == 0:
            return c
    raise ValueError(f"no tile for {n} among {cands}")


def _sds(shape, dtype):
    return jax.ShapeDtypeStruct(shape, dtype)


def _dot(a, b):
    return lax.dot_general(a, b, (((1,), (0,)), ((), ())), precision=HI, preferred_element_type=F32)


NN, NT, TN = (((1,), (0,)), ((), ())), (((1,), (1,)), ((), ())), (((0,), (0,)), ((), ()))


def _mxu(a, b, dims):
    return lax.dot_general(a.astype(BF16), b.astype(BF16), dims, preferred_element_type=F32)


@jax.custom_vjp
def _bdot(a, b):
    return _mxu(a, b, NN)


@jax.custom_vjp
def _bdot_nt(a, b):
    return _mxu(a, b, NT)


@jax.custom_vjp
def _bdot_tn(a, b):
    return _mxu(a, b, TN)


_bdot.defvjp(lambda a, b: (_mxu(a, b, NN), (a, b)), lambda r, d: (_mxu(d, r[1], NT), _mxu(r[0], d, TN)))
_bdot_nt.defvjp(lambda a, b: (_mxu(a, b, NT), (a, b)), lambda r, d: (_mxu(d, r[1], NN), _mxu(d, r[0], TN)))
_bdot_tn.defvjp(lambda a, b: (_mxu(a, b, TN), (a, b)), lambda r, d: (_mxu(r[1], d, NT), _mxu(r[0], d, NN)))


def _split(x):
    hi = x.astype(BF16)
    return hi, (x - hi.astype(F32)).astype(BF16)


def _mxu3(a, b, dims):
    ah, al = _split(a)
    bh, bl = _split(b)
    return _mxu(ah, bh, dims) + (_mxu(ah, bl, dims) + _mxu(al, bh, dims))


@jax.custom_vjp
def _dot3(a, b):
    return _mxu3(a, b, NN)


_dot3.defvjp(lambda a, b: (_mxu3(a, b, NN), (a, b)), lambda r, d: (_mxu3(d, r[1], NT), _mxu3(r[0], d, TN)))


def _pieces(x):
    x1 = x.astype(BF16)
    r1 = x - x1.astype(F32)
    x2 = r1.astype(BF16)
    return x1, x2, (r1 - x2.astype(F32)).astype(BF16)


def _mask_mxu(mask, x, dims):
    x1, x2, x3 = _pieces(x)
    return _mxu(mask, x1, dims) + (_mxu(mask, x2, dims) + _mxu(mask, x3, dims))


def _spread_mxu(x, mask, dims):
    x1, x2, x3 = _pieces(x)
    return _mxu(x1, mask, dims) + (_mxu(x2, mask, dims) + _mxu(x3, mask, dims))


@jax.custom_vjp
def _mask_dot(mask, x):
    return _mask_mxu(mask, x, NN)


@jax.custom_vjp
def _spread_dot(x, mask):
    return _spread_mxu(x, mask, NN)


_mask_dot.defvjp(lambda m, x: (_mask_mxu(m, x, NN), m), lambda m, d: (jnp.zeros_like(m), _mask_mxu(m, d, TN)))
_spread_dot.defvjp(lambda x, m: (_spread_mxu(x, m, NN), m), lambda m, d: (_spread_mxu(d, m, NT), jnp.zeros_like(m)))


def _iota(shape, axis):
    return lax.broadcasted_iota(jnp.int32, shape, axis)


def _silu(x):
    return x * jax.nn.sigmoid(x)


def _softplus(x):
    return jnp.maximum(x, 0.0) + jnp.log1p(jnp.exp(-jnp.abs(x)))


def _rms(x, w):
    return x * lax.rsqrt(jnp.mean(x * x, axis=-1, keepdims=True) + EPS) * w


def _lane_col(x, lane):
    m = (_iota(x.shape, 1) == lane).astype(F32)
    return jnp.sum(x * m, axis=1, keepdims=True)


def _col_to_row(c):
    n = c.shape[0]
    eye = (_iota((n, n), 0) == _iota((n, n), 1)).astype(F32)
    return jnp.sum(c * eye, axis=0, keepdims=True)


def _tril(n, strict=False):
    r, c = _iota((n, n), 0), _iota((n, n), 1)
    return (r > c) if strict else (r >= c)


def _mm(a, b, out_dtype, name):
    m, k = a.shape
    n = b.shape[1]
    tm = _pick(m, (512, 256, 128, 64, 32, 16, 8))
    tn = _pick(n, (1024, 768, 512, 384, 256, 128))
    tk = k if k <= 3072 else _pick(k, (1024, 768, 512, 384, 256, 128))
    nk = k // tk

    def body(a_ref, b_ref, o_ref, acc_ref):
        kk = pl.program_id(2)

        @pl.when(kk == 0)
        def _():
            acc_ref[...] = jnp.zeros_like(acc_ref)

        acc_ref[...] += _bdot(a_ref[...], b_ref[...])

        @pl.when(kk == nk - 1)
        def _():
            o_ref[...] = acc_ref[...].astype(out_dtype)

    return _call(
        body, name, (m // tm, n // tn, nk),
        [pl.BlockSpec((tm, tk), lambda i, j, kk: (i, kk)), pl.BlockSpec((tk, tn), lambda i, j, kk: (kk, j))],
        pl.BlockSpec((tm, tn), lambda i, j, kk: (i, j)), _sds((m, n), out_dtype),
        scratch=[pltpu.VMEM((tm, tn), F32)],
    )(a, b)


def _ada_fwd(c_all, w, b):
    depth, _, n = w.shape
    rows = c_all.shape[0]

    def body(c_ref, w_ref, b_ref, o_ref):
        o_ref[...] = _dot(_silu(c_ref[...]), w_ref[...]) + b_ref[...]

    return _call(
        body, "ada_fwd", (depth,),
        [pl.BlockSpec((rows, D_MODEL), lambda l: (0, 0)), pl.BlockSpec((None, D_MODEL, n), lambda l: (l, 0, 0)),
         pl.BlockSpec((None, 1, n), lambda l: (l, 0, 0))],
        pl.BlockSpec((None, rows, n), lambda l: (l, 0, 0)), _sds((depth, rows, n), F32),
    )(c_all, w, b)


def _ada_bwd(c_all_t, dmod):
    depth, rows, n = dmod.shape

    def body(ct_ref, dm_ref, dw_ref, db_ref):
        dm = dm_ref[...]
        dw_ref[...] = _dot(_silu(ct_ref[...]), dm)
        db_ref[...] = jnp.sum(dm, axis=0, keepdims=True)

    return _call(
        body, "ada_bwd", (depth,),
        [pl.BlockSpec((D_MODEL, rows), lambda l: (0, 0)), pl.BlockSpec((None, rows, n), lambda l: (l, 0, 0))],
        [pl.BlockSpec((None, D_MODEL, n), lambda l: (l, 0, 0)), pl.BlockSpec((None, 1, n), lambda l: (l, 0, 0))],
        [_sds((depth, D_MODEL, n), F32), _sds((depth, 1, n), F32)],
    )(c_all_t, dmod)


def _lb_fn(p):
    rows = [p[l:l + 1] for l in range(DEPTH)]
    mx = functools.reduce(jnp.maximum, rows)
    es = [jnp.exp(r - mx) for r in rows]
    tot = functools.reduce(lambda a, b: a + b, es)
    sm = [e / tot for e in es]
    out, run = [], None
    for l in range(DEPTH):
        run = sm[l] if run is None else run + sm[l]
        out.append(run - sm[0])
    return jnp.concatenate(out, axis=0)


def _lb_fwd(p):
    def body(p_ref, o_ref):
        o_ref[...] = _lb_fn(p_ref[...])

    full = pl.BlockSpec(p.shape, lambda i: (0, 0))
    return _call(body, "lb_fwd", (1,), [full], full, _sds(p.shape, F32))(p)


def _lb_bwd(p, d_lower):
    def body(p_ref, d_ref, o_ref):
        _, vjp = jax.vjp(_lb_fn, p_ref[...])
        o_ref[...] = vjp(d_ref[...])[0]

    full = pl.BlockSpec(p.shape, lambda i: (0, 0))
    return _call(body, "lb_bwd", (1,), [full, full], full, _sds(p.shape, F32))(p, d_lower)


def _norm_mod_fn(x, w, shift, scale):
    return _rms(x, w) * (1.0 + scale) + shift


def _norm_mod_fwd(x, w, shift, scale, name):
    bsz, s, d = x.shape
    ts = _pick(s, (256, 128, 64, 32, 16, 8))

    def body(x_ref, w_ref, sh_ref, sc_ref, o_ref):
        o_ref[...] = _norm_mod_fn(x_ref[...], w_ref[...], sh_ref[...], sc_ref[...]).astype(BF16)

    row = pl.BlockSpec((None, ts, d), lambda b, i: (b, i, 0))
    per_b = pl.BlockSpec((None, 1, d), lambda b, i: (b, 0, 0))
    return _call(body, name, (bsz, s // ts), [row, pl.BlockSpec((1, d), lambda b, i: (0, 0)), per_b, per_b],
                 row, _sds(x.shape, BF16))(x, w, shift, scale)


def _norm_mod_bwd(x, w, shift, scale, dh, carry, name):
    bsz, s, d = x.shape
    ts = _pick(s, (256, 128, 64, 32, 16, 8))

    def body(x_ref, w_ref, sh_ref, sc_ref, dh_ref, c_ref, dx_ref, dw_ref, dsh_ref, dsc_ref):
        b, i = pl.program_id(0), pl.program_id(1)
        _, vjp = jax.vjp(_norm_mod_fn, x_ref[...], w_ref[...], sh_ref[...], sc_ref[...])
        dx, dw, dsh, dsc = vjp(dh_ref[...])
        dx_ref[...] = dx + c_ref[...]

        @pl.when((b == 0) & (i == 0))
        def _():
            dw_ref[...] = jnp.zeros_like(dw_ref)

        @pl.when(i == 0)
        def _():
            dsh_ref[...] = jnp.zeros_like(dsh_ref)
            dsc_ref[...] = jnp.zeros_like(dsc_ref)

        dw_ref[...] += dw
        dsh_ref[...] += dsh
        dsc_ref[...] += dsc

    row = pl.BlockSpec((None, ts, d), lambda b, i: (b, i, 0))
    per_b = pl.BlockSpec((None, 1, d), lambda b, i: (b, 0, 0))
    wspec = pl.BlockSpec((1, d), lambda b, i: (0, 0))
    return _call(body, name, (bsz, s // ts), [row, wspec, per_b, per_b, row, row],
                 [row, wspec, per_b, per_b],
                 [_sds(x.shape, F32), _sds((1, d), F32), _sds((bsz, 1, d), F32), _sds((bsz, 1, d), F32)],
                 )(x, w, shift, scale, dh, carry)


def _resid_fwd(x, y, gate, name):
    bsz, s, d = x.shape
    ts = _pick(s, (512, 256, 128, 64, 32, 16, 8))

    def body(x_ref, y_ref, g_ref, o_ref):
        o_ref[...] = x_ref[...] + g_ref[...] * y_ref[...]

    row = pl.BlockSpec((None, ts, d), lambda b, i: (b, i, 0))
    per_b = pl.BlockSpec((None, 1, d), lambda b, i: (b, 0, 0))
    return _call(body, name, (bsz, s // ts), [row, row, per_b], row, _sds(x.shape, F32))(x, y, gate)


def _gate_bwd(dx, y, gate, name):
    bsz, s, d = dx.shape
    ts = _pick(s, (512, 256, 128, 64, 32, 16, 8))

    def body(dx_ref, y_ref, g_ref, dy_ref, dg_ref):
        dxv = dx_ref[...]
        dy_ref[...] = (dxv * g_ref[...]).astype(BF16)

        @pl.when(pl.program_id(1) == 0)
        def _():
            dg_ref[...] = jnp.zeros_like(dg_ref)

        dg_ref[...] += jnp.sum(dxv * y_ref[...], axis=0, keepdims=True)

    row = pl.BlockSpec((None, ts, d), lambda b, i: (b, i, 0))
    per_b = pl.BlockSpec((None, 1, d), lambda b, i: (b, 0, 0))
    return _call(body, name, (bsz, s // ts), [row, row, per_b], [row, per_b],
                 [_sds(dx.shape, BF16), _sds((bsz, 1, d), F32)])(dx, y, gate)


HALO = 8


def _conv_pre(xx, w_ref, b_ref, kw, rows):
    acc = w_ref[kw - 1:kw, :] * xx[HALO:HALO + rows]
    for k in range(kw - 1):
        acc = acc + w_ref[k:k + 1, :] * pltpu.roll(xx, kw - 1 - k, 0)[HALO:HALO + rows]
    return acc + b_ref[...]


def _conv_fwd(x, col0, width, w, b, act, name):
    bsz, s, _ = x.shape
    kw = w.shape[0]
    ts = _pick(s, (512, 256, 128, 64, 32, 16, 8))
    tc = _pick(width, (512, 256, 128))
    assert col0 % tc == 0
    c0 = col0 // tc
    hb = ts // HALO

    def body(x_ref, xp_ref, w_ref, b_ref, o_ref):
        i = pl.program_id(1)
        xp = jnp.where(i > 0, xp_ref[...], 0.0)
        xx = jnp.concatenate([xp, x_ref[...]], axis=0)
        pre = _conv_pre(xx, w_ref, b_ref, kw, ts)
        o_ref[...] = _silu(pre) if act else pre

    return _call(
        body, name, (bsz, s // ts, width // tc),
        [pl.BlockSpec((None, ts, tc), lambda bb, i, j: (bb, i, c0 + j)),
         pl.BlockSpec((None, HALO, tc), lambda bb, i, j: (bb, jnp.maximum(i * hb - 1, 0), c0 + j)),
         pl.BlockSpec((kw, tc), lambda bb, i, j: (0, j)), pl.BlockSpec((1, tc), lambda bb, i, j: (0, j))],
        pl.BlockSpec((None, ts, tc), lambda bb, i, j: (bb, i, j)), _sds((bsz, s, width), F32),
    )(x, x, w, b)


def _conv_bwd(dy, x, col0, width, w, b, act, name):
    bsz, s, _ = x.shape
    kw = w.shape[0]
    ts = _pick(s, (512, 256, 128, 64, 32, 16, 8))
    tc = _pick(width, (512, 256, 128))
    c0 = col0 // tc
    hb = ts // HALO
    nt = s // ts
    last_h = s // HALO - 1

    def body(x_ref, xp_ref, xn_ref, dy_ref, dyn_ref, w_ref, b_ref, dx_ref, dw_ref, db_ref):
        bb, i = pl.program_id(1), pl.program_id(2)
        xp = jnp.where(i > 0, xp_ref[...], 0.0)
        xx = jnp.concatenate([xp, x_ref[...], xn_ref[...]], axis=0)
        dyy = jnp.concatenate([dy_ref[...], jnp.where(i < nt - 1, dyn_ref[...], 0.0)], axis=0)
        n = ts + HALO
        if act:
            pre = _conv_pre(xx, w_ref, b_ref, kw, n)
            sg = jax.nn.sigmoid(pre)
            dpre = dyy * (sg * (1.0 + pre * (1.0 - sg)))
        else:
            dpre = dyy
        dx = w_ref[kw - 1:kw, :] * dpre[:ts]
        for k in range(kw - 1):
            dx = dx + w_ref[k:k + 1, :] * pltpu.roll(dpre, n - (kw - 1 - k), 0)[:ts]
        dx_ref[...] = dx.astype(BF16)

        @pl.when((bb == 0) & (i == 0))
        def _():
            dw_ref[...] = jnp.zeros_like(dw_ref)
            db_ref[...] = jnp.zeros_like(db_ref)

        dt = dpre[:ts]
        db_ref[...] += jnp.sum(dt, axis=0, keepdims=True)
        dw_ref[kw - 1:kw, :] += jnp.sum(dt * xx[HALO:HALO + ts], axis=0, keepdims=True)
        for k in range(kw - 1):
            xs = pltpu.roll(xx, kw - 1 - k, 0)[HALO:HALO + ts]
            dw_ref[k:k + 1, :] += jnp.sum(dt * xs, axis=0, keepdims=True)

    xspec = lambda f: pl.BlockSpec((None, HALO, tc), f)
    return _call(
        body, name, (width // tc, bsz, nt),
        [pl.BlockSpec((None, ts, tc), lambda j, bb, i: (bb, i, c0 + j)),
         xspec(lambda j, bb, i: (bb, jnp.maximum(i * hb - 1, 0), c0 + j)),
         xspec(lambda j, bb, i: (bb, jnp.minimum((i + 1) * hb, last_h), c0 + j)),
         pl.BlockSpec((None, ts, tc), lambda j, bb, i: (bb, i, j)),
         xspec(lambda j, bb, i: (bb, jnp.minimum((i + 1) * hb, last_h), j)),
         pl.BlockSpec((kw, tc), lambda j, bb, i: (0, j)), pl.BlockSpec((1, tc), lambda j, bb, i: (0, j))],
        [pl.BlockSpec((None, ts, tc), lambda j, bb, i: (bb, i, j)),
         pl.BlockSpec((kw, tc), lambda j, bb, i: (0, j)), pl.BlockSpec((1, tc), lambda j, bb, i: (0, j))],
        [_sds((bsz, s, width), BF16), _sds((kw, width), F32), _sds((1, width), F32)],
    )(x, x, x, dy, dy, w, b)


def _glu_fwd(u, name):
    bsz, s, f2 = u.shape
    f = f2 // 2
    ts = _pick(s, (512, 256, 128, 64, 32, 16, 8))
    tc = _pick(f, (256, 128))
    nf = f // tc

    def body(g_ref, v_ref, o_ref):
        o_ref[...] = (_silu(g_ref[...]) * v_ref[...]).astype(BF16)

    return _call(
        body, name, (bsz, s // ts, nf),
        [pl.BlockSpec((None, ts, tc), lambda b, i, j: (b, i, j)),
         pl.BlockSpec((None, ts, tc), lambda b, i, j: (b, i, nf + j))],
        pl.BlockSpec((None, ts, tc), lambda b, i, j: (b, i, j)), _sds((bsz, s, f), BF16),
    )(u, u)


def _glu_bwd(da, u, name):
    bsz, s, f2 = u.shape
    f = f2 // 2
    ts = _pick(s, (512, 256, 128, 64, 32, 16, 8))
    tc = _pick(f, (256, 128))
    nf = f // tc

    def body(da_ref, g_ref, v_ref, o_ref):
        j = pl.program_id(2)
        g, v, d = g_ref[...], v_ref[...], da_ref[...]
        sg = jax.nn.sigmoid(g)
        dg = d * v * (sg * (1.0 + g * (1.0 - sg)))
        dv = d * (g * sg)
        o_ref[...] = jnp.where(j < nf, dg, dv)

    return _call(
        body, name, (bsz, s // ts, 2 * nf),
        [pl.BlockSpec((None, ts, tc), lambda b, i, j: (b, i, j % nf)),
         pl.BlockSpec((None, ts, tc), lambda b, i, j: (b, i, j % nf)),
         pl.BlockSpec((None, ts, tc), lambda b, i, j: (b, i, nf + j % nf))],
        pl.BlockSpec((None, ts, tc), lambda b, i, j: (b, i, j)), _sds((bsz, s, f2), F32),
    )(da, u, u)


def _merge_fwd(p, oa, ob, oc, wa, wb, wc, name):
    bsz, s, _ = p.shape
    tm = _pick(s, (256, 128, 64, 32, 16, 8))
    gblk = P_GATE // (3 * D_MODEL)

    def body(g_ref, oa_ref, ob_ref, oc_ref, wa_ref, wb_ref, wc_ref, o_ref):
        acc = None
        for i, (o_r, w_r) in enumerate(((oa_ref, wa_ref), (ob_ref, wb_ref), (oc_ref, wc_ref))):
            y = _bdot(o_r[...], w_r[...])
            t = jax.nn.sigmoid(g_ref[:, i * D_MODEL:(i + 1) * D_MODEL]) * y
            acc = t if acc is None else acc + t
        o_ref[...] = acc.astype(BF16)

    orow = pl.BlockSpec((None, tm, 512), lambda b, i: (b, i, 0))
    wfull = pl.BlockSpec((512, D_MODEL), lambda b, i: (0, 0))
    return _call(
        body, name, (bsz, s // tm),
        [pl.BlockSpec((None, tm, 3 * D_MODEL), lambda b, i: (b, i, gblk)), orow, orow, orow, wfull, wfull, wfull],
        pl.BlockSpec((None, tm, D_MODEL), lambda b, i: (b, i, 0)), _sds((bsz, s, D_MODEL), BF16),
    )(p, oa, ob, oc, wa, wb, wc)


def _merge_bwd(dm, p, oa, ob, oc, wa, wb, wc, wat, wbt, wct, name):
    bsz, s, _ = p.shape
    tm = _pick(s, (256, 128, 64, 32, 16, 8))
    gblk = P_GATE // (3 * D_MODEL)

    def body(dm_ref, g_ref, oa_ref, ob_ref, oc_ref, wa_ref, wb_ref, wc_ref, wat_ref, wbt_ref, wct_ref,
             dg_ref, doa_ref, dob_ref, doc_ref, dya_ref, dyb_ref, dyc_ref):
        dmv = dm_ref[...]
        trip = ((oa_ref, wa_ref, wat_ref, doa_ref, dya_ref), (ob_ref, wb_ref, wbt_ref, dob_ref, dyb_ref),
                (oc_ref, wc_ref, wct_ref, doc_ref, dyc_ref))
        for i, (o_r, w_r, wt_r, do_r, dy_r) in enumerate(trip):
            y = _bdot(o_r[...], w_r[...])
            sg = jax.nn.sigmoid(g_ref[:, i * D_MODEL:(i + 1) * D_MODEL])
            dg_ref[:, i * D_MODEL:(i + 1) * D_MODEL] = (dmv * y * sg * (1.0 - sg)).astype(BF16)
            dy = (dmv * sg).astype(BF16)
            dy_r[...] = dy
            do_r[...] = _bdot(dy, wt_r[...])

    orow = pl.BlockSpec((None, tm, 512), lambda b, i: (b, i, 0))
    drow = pl.BlockSpec((None, tm, D_MODEL), lambda b, i: (b, i, 0))
    grow = pl.BlockSpec((None, tm, 3 * D_MODEL), lambda b, i: (b, i, 0))
    wfull = pl.BlockSpec((512, D_MODEL), lambda b, i: (0, 0))
    wtfull = pl.BlockSpec((D_MODEL, 512), lambda b, i: (0, 0))
    return _call(
        body, name, (bsz, s // tm),
        [drow, pl.BlockSpec((None, tm, 3 * D_MODEL), lambda b, i: (b, i, gblk)), orow, orow, orow,
         wfull, wfull, wfull, wtfull, wtfull, wtfull],
        [grow, orow, orow, orow, drow, drow, drow],
        [_sds((bsz, s, 3 * D_MODEL), BF16)] + [_sds((bsz, s, 512), F32)] * 3 + [_sds((bsz, s, D_MODEL), BF16)] * 3,
    )(dm, p, oa, ob, oc, wa, wb, wc, wat, wbt, wct)


def _final_loss(x, w, target):
    bsz, s, d = x.shape
    ts = _pick(s, (256, 128, 64, 32, 16, 8))

    def body(x_ref, w_ref, t_ref, loss_ref, dx_ref, dw_ref):
        first = (pl.program_id(0) == 0) & (pl.program_id(1) == 0)
        y, vjp = jax.vjp(_rms, x_ref[...], w_ref[...])
        err = y - t_ref[...]
        dx, dw = vjp(err * (1.0 / d))
        dx_ref[...] = dx

        @pl.when(first)
        def _():
            loss_ref[...] = jnp.zeros_like(loss_ref)
            dw_ref[...] = jnp.zeros_like(dw_ref)

        loss_ref[...] += 0.5 * jnp.sum(jnp.sum(err * err, axis=1, keepdims=True), axis=0, keepdims=True) * (1.0 / d)
        dw_ref[...] += dw

    row = pl.BlockSpec((None, ts, d), lambda b, i: (b, i, 0))
    wspec = pl.BlockSpec((1, d), lambda b, i: (0, 0))
    return _call(body, "final_loss", (bsz, s // ts), [row, wspec, row],
                 [pl.BlockSpec((8, LANES), lambda b, i: (0, 0)), row, wspec],
                 [_sds((8, LANES), F32), _sds(x.shape, F32), _sds((1, d), F32)])(x, w, target)


def _neumann_inverse(m):
    n = m.shape[0]
    eye = (_iota((n, n), 0) == _iota((n, n), 1)).astype(F32)
    p = -m
    x = eye + p
    for _ in range(int(math.log2(n)) - 1):
        p = _mxu3(p, p, NN)
        x = x + _mxu3(x, p, NN)
    return x


@jax.custom_vjp
def _unit_lower_inverse(m):
    return _neumann_inverse(m)


def _unit_lower_inverse_fwd(m):
    t = _neumann_inverse(m)
    return t, t


def _unit_lower_inverse_bwd(t, dt):
    return (-_mxu3(t, _mxu3(dt, t, NT), TN),)


_unit_lower_inverse.defvjp(_unit_lower_inverse_fwd, _unit_lower_inverse_bwd)


def _gdn_chunk(states, qkv, small, z, a_row, dt_row, nw):
    c = qkv.shape[0]
    kw = GDN_HEADS * GDN_DK
    g_all = -jnp.exp(a_row) * _softplus(small + dt_row)
    beta_all = jax.nn.sigmoid(small)
    incl, strict = _tril(c), _tril(c, True)
    big_g_all = _mask_dot(incl.astype(BF16), g_all)
    outs, new_states = [], []
    for h in range(GDN_HEADS):
        sl = slice(h * GDN_DK, (h + 1) * GDN_DK)
        q = qkv[:, sl]
        k = qkv[:, kw + h * GDN_DK:kw + (h + 1) * GDN_DK]
        v = qkv[:, 2 * kw + h * GDN_DK:2 * kw + (h + 1) * GDN_DK]
        q = q * lax.rsqrt(jnp.sum(q * q, axis=-1, keepdims=True) + EPS) * (GDN_DK ** -0.5)
        k = k * lax.rsqrt(jnp.sum(k * k, axis=-1, keepdims=True) + EPS)
        gc = _lane_col(big_g_all, SM_A + h)
        bc = _lane_col(beta_all, SM_B + h)
        g_last = jnp.sum(_lane_col(g_all, SM_A + h), axis=0, keepdims=True)
        diff = gc - _col_to_row(gc)
        decay = jnp.where(incl, jnp.exp(jnp.where(incl, diff, 0.0)), 0.0)
        kb = k * bc
        m = jnp.where(strict, _bdot_nt(kb, k) * decay, 0.0)
        tinv = _unit_lower_inverse(m)
        eg = jnp.exp(gc)
        u = _dot3(tinv, v * bc)
        w = _dot3(tinv, kb * eg)
        attn = _bdot_nt(q, k) * decay
        st = states[h]
        v_new = u - _bdot(w, st)
        o = _bdot(q * eg, st) + _bdot(attn, v_new)
        new_states.append(st * jnp.exp(g_last) + _bdot_tn(k * jnp.exp(g_last - gc), v_new))
        outs.append(_rms(o, nw) * _silu(z[:, sl]))
    return new_states, jnp.concatenate(outs, axis=1)


def _gdn_specs(c):
    row = lambda w, blk: pl.BlockSpec((None, c, w), lambda b, n, blk=blk: (b, n, blk))
    prm = pl.BlockSpec((1, LANES), lambda b, n: (0, 0))
    return [row(1536, 0), row(LANES, P_SMALL // LANES), row(512, P_GZ // 512), prm, prm, prm]


def _state_spec():
    return pl.BlockSpec((None, None, 4, LANES, LANES), lambda b, n: (b, n, 0, 0, 0))


def _gdn_fwd(qkv_act, p, a_row, dt_row, nw, name):
    bsz, s, _ = qkv_act.shape
    c = GDN_CHUNK
    nc = s // c

    def body(qkv_ref, sm_ref, z_ref, a_ref, dt_ref, nw_ref, o_ref, st_ref, st_scr):
        @pl.when(pl.program_id(1) == 0)
        def _():
            st_scr[...] = jnp.zeros_like(st_scr)

        st_ref[...] = st_scr[...]
        states = [st_scr[h] for h in range(GDN_HEADS)]
        new_states, o = _gdn_chunk(states, qkv_ref[...], sm_ref[...], z_ref[...], a_ref[...], dt_ref[...], nw_ref[...])
        for h in range(GDN_HEADS):
            st_scr[h] = new_states[h]
        o_ref[...] = o.astype(BF16)

    return _call(
        body, name, (bsz, nc), _gdn_specs(c),
        [pl.BlockSpec((None, c, 512), lambda b, n: (b, n, 0)), _state_spec()],
        [_sds((bsz, s, 512), BF16), _sds((bsz, nc, 4, LANES, LANES), F32)],
        scratch=[pltpu.VMEM((4, LANES, LANES), F32)],
    )(qkv_act, p, p, a_row, dt_row, nw)


def _gdn_bwd(do, qkv_act, p, a_row, dt_row, nw, st_all, name):
    bsz, s, _ = qkv_act.shape
    c = GDN_CHUNK
    nc = s // c

    def body(qkv_ref, sm_ref, z_ref, a_ref, dt_ref, nw_ref, do_ref, st_ref,
             dqkv_ref, dsm_ref, dz_ref, da_ref, ddt_ref, dnw_ref, ds_scr):
        first = (pl.program_id(0) == 0) & (pl.program_id(1) == 0)

        @pl.when(pl.program_id(1) == 0)
        def _():
            ds_scr[...] = jnp.zeros_like(ds_scr)

        @pl.when(first)
        def _():
            da_ref[...] = jnp.zeros_like(da_ref)
            ddt_ref[...] = jnp.zeros_like(ddt_ref)
            dnw_ref[...] = jnp.zeros_like(dnw_ref)

        states = [st_ref[h] for h in range(GDN_HEADS)]
        _, vjp = jax.vjp(_gdn_chunk, states, qkv_ref[...], sm_ref[...], z_ref[...], a_ref[...], dt_ref[...], nw_ref[...])
        d_states, dqkv, dsm, dz, da, ddt, dnw = vjp(([ds_scr[h] for h in range(GDN_HEADS)], do_ref[...]))
        for h in range(GDN_HEADS):
            ds_scr[h] = d_states[h]
        dqkv_ref[...] = dqkv
        dsm_ref[...] = dsm
        dz_ref[...] = dz.astype(BF16)
        da_ref[...] += da
        ddt_ref[...] += ddt
        dnw_ref[...] += dnw

    rrow = lambda w, blk: pl.BlockSpec((None, c, w), lambda b, n, blk=blk: (b, nc - 1 - n, blk))
    prm = pl.BlockSpec((1, LANES), lambda b, n: (0, 0))
    return _call(
        body, name, (bsz, nc),
        [rrow(1536, 0), rrow(LANES, P_SMALL // LANES), rrow(512, P_GZ // 512), prm, prm, prm, rrow(512, 0),
         pl.BlockSpec((None, None, 4, LANES, LANES), lambda b, n: (b, nc - 1 - n, 0, 0, 0))],
        [rrow(1536, 0), rrow(LANES, 0), rrow(512, 0), prm, prm, prm],
        [_sds((bsz, s, 1536), F32), _sds((bsz, s, LANES), F32), _sds((bsz, s, 512), BF16)] + [_sds((1, LANES), F32)] * 3,
        scratch=[pltpu.VMEM((4, LANES, LANES), F32)],
    )(qkv_act, p, p, a_row, dt_row, nw, do, st_all)


def _hgrn_block(states, q_raw, f_raw, i_raw, g_raw, lb, nw):
    n = q_raw.shape[0]
    c = HGRN_CHUNK
    r, cc = _iota((n, n), 0), _iota((n, n), 1)
    same = (r // c) == (cc // c)
    causal = same & (r >= cc)
    ref_row = (r // c) * c + (c // 2 - 1)
    run_sum = causal.astype(F32)
    rel_sum = run_sum - (same & (ref_row >= cc)).astype(F32)
    sums = jnp.concatenate([run_sum, rel_sum, same.astype(F32)], axis=0).astype(BF16)
    outs, new_states = [], []
    for h in range(HGRN_HEADS):
        sl = slice(h * HGRN_DK, (h + 1) * HGRN_DK)
        fr, lbh = f_raw[:, sl], lb[:, sl]
        q = _silu(q_raw[:, sl])
        logf = jnp.log(lbh + (1.0 - lbh) * jax.nn.sigmoid(fr))
        k = (1.0 - lbh) * jax.nn.sigmoid(-fr)
        v = i_raw[:, sl]
        all_sums = _mask_dot(sums, logf)
        big_g, g_rel, g_tot = all_sums[:n], all_sums[n:2 * n], all_sums[2 * n:]
        scores = _bdot_nt(q * jnp.exp(g_rel), k * jnp.exp(-g_rel))
        o_intra = _bdot(jnp.where(causal, scores, 0.0), v)
        qg = q * jnp.exp(big_g)
        k_end = k * jnp.exp(g_tot - big_g)
        st = states[h]
        parts = []
        for j in range(n // c):
            rows = slice(j * c, (j + 1) * c)
            parts.append(_bdot_nt(qg[rows], st))
            st = st * jnp.exp(g_tot[j * c:j * c + 1]) + _bdot_tn(v[rows], k_end[rows])
        new_states.append(st)
        o = o_intra + jnp.concatenate(parts, axis=0)
        outs.append(_rms(o, nw) * _silu(g_raw[:, sl]))
    return new_states, jnp.concatenate(outs, axis=1)


def _hgrn_fwd(p, lb, nw, name):
    bsz, s, _ = p.shape
    n = HGRN_BLOCK
    nb = s // n

    def body(q_ref, f_ref, i_ref, g_ref, lb_ref, nw_ref, o_ref, st_ref, st_scr):
        @pl.when(pl.program_id(1) == 0)
        def _():
            st_scr[...] = jnp.zeros_like(st_scr)

        st_ref[...] = st_scr[...]
        states = [st_scr[h] for h in range(HGRN_HEADS)]
        new_states, o = _hgrn_block(states, q_ref[...], f_ref[...], i_ref[...], g_ref[...], lb_ref[...], nw_ref[...])
        for h in range(HGRN_HEADS):
            st_scr[h] = new_states[h]
        o_ref[...] = o.astype(BF16)

    row = lambda blk: pl.BlockSpec((None, n, 512), lambda b, i, blk=blk: (b, i, blk))
    return _call(
        body, name, (bsz, nb),
        [row(P_HQ // 512), row(P_HF // 512), row(P_HI // 512), row(P_HG // 512),
         pl.BlockSpec((1, 512), lambda b, i: (0, 0)), pl.BlockSpec((1, LANES), lambda b, i: (0, 0))],
        [row(0), _state_spec()],
        [_sds((bsz, s, 512), BF16), _sds((bsz, nb, 4, LANES, LANES), F32)],
        scratch=[pltpu.VMEM((4, LANES, LANES), F32)],
    )(p, p, p, p, lb, nw)


def _hgrn_bwd(do, p, lb, nw, st_all, name):
    bsz, s, _ = p.shape
    n = HGRN_BLOCK
    nb = s // n

    def body(q_ref, f_ref, i_ref, g_ref, lb_ref, nw_ref, do_ref, st_ref, dp_ref, dlb_ref, dnw_ref, ds_scr):
        first = (pl.program_id(0) == 0) & (pl.program_id(1) == 0)

        @pl.when(pl.program_id(1) == 0)
        def _():
            ds_scr[...] = jnp.zeros_like(ds_scr)

        @pl.when(first)
        def _():
            dlb_ref[...] = jnp.zeros_like(dlb_ref)
            dnw_ref[...] = jnp.zeros_like(dnw_ref)

        states = [st_ref[h] for h in range(HGRN_HEADS)]
        _, vjp = jax.vjp(_hgrn_block, states, q_ref[...], f_ref[...], i_ref[...], g_ref[...], lb_ref[...], nw_ref[...])
        d_states, dq, df, di, dg, dlb, dnw = vjp(([ds_scr[h] for h in range(HGRN_HEADS)], do_ref[...]))
        for h in range(HGRN_HEADS):
            ds_scr[h] = d_states[h]
        for j, t in enumerate((dq, df, di, dg)):
            dp_ref[:, j * 512:(j + 1) * 512] = t.astype(BF16)
        dlb_ref[...] += dlb
        dnw_ref[...] += dnw

    row = lambda blk: pl.BlockSpec((None, n, 512), lambda b, i, blk=blk: (b, nb - 1 - i, blk))
    return _call(
        body, name, (bsz, nb),
        [row(P_HQ // 512), row(P_HF // 512), row(P_HI // 512), row(P_HG // 512),
         pl.BlockSpec((1, 512), lambda b, i: (0, 0)), pl.BlockSpec((1, LANES), lambda b, i: (0, 0)), row(0),
         pl.BlockSpec((None, None, 4, LANES, LANES), lambda b, i: (b, nb - 1 - i, 0, 0, 0))],
        [pl.BlockSpec((None, n, 2048), lambda b, i: (b, nb - 1 - i, 0)),
         pl.BlockSpec((1, 512), lambda b, i: (0, 0)), pl.BlockSpec((1, LANES), lambda b, i: (0, 0))],
        [_sds((bsz, s, 2048), BF16), _sds((1, 512), F32), _sds((1, LANES), F32)],
        scratch=[pltpu.VMEM((4, LANES, LANES), F32)],
    )(p, p, p, p, lb, nw, do, st_all)


def _ssd_chunk(states, xbc, small, z, a_row, dt_row, d_row, nw):
    c = xbc.shape[0]
    incl = _tril(c)
    dt_all = _softplus(small + dt_row)
    da_all = dt_all * (-jnp.exp(a_row))
    spread = (_iota((LANES, SSD_INNER), 0) == SM_DT + _iota((LANES, SSD_INNER), 1) // SSD_HEAD_DIM).astype(BF16)
    both = _spread_dot(jnp.concatenate([dt_all, da_all], axis=0), spread)
    dt_e, da_e = both[:c], both[c:]
    acs_e = _mask_dot(incl.astype(BF16), da_e)
    last_e = jnp.sum(da_e, axis=0, keepdims=True)
    xs = xbc[:, :SSD_INNER]
    xdt = xs * dt_e
    gw = SSD_GROUPS * SSD_STATE
    lane = _iota((1, LANES), 1)
    ys, new_states = [], []
    for j in range(4):
        g = j // 2
        bg = xbc[:, SSD_INNER + g * SSD_STATE:SSD_INNER + (g + 1) * SSD_STATE]
        cg = xbc[:, SSD_INNER + gw + g * SSD_STATE:SSD_INNER + gw + (g + 1) * SSD_STATE]
        cb = _bdot_nt(cg, bg)
        sl = slice(j * LANES, (j + 1) * LANES)
        xblk, acs, last = xdt[:, sl], acs_e[:, sl], last_e[:, sl]
        y = None
        for sub in range(2):
            ac = acs[:, sub * SSD_HEAD_DIM:sub * SSD_HEAD_DIM + 1]
            seg = jnp.where(incl, jnp.exp(jnp.where(incl, ac - _col_to_row(ac), 0.0)), 0.0)
            mine = ((lane // SSD_HEAD_DIM) == sub).astype(F32)
            t = _bdot(cb * seg, xblk * mine)
            y = t if y is None else y + t
        st = states[j]
        y = y + _bdot(cg, st) * jnp.exp(acs)
        new_states.append(st * jnp.exp(last) + _bdot_tn(bg, xblk * jnp.exp(last - acs)))
        ys.append(y + d_row[:, sl] * xs[:, sl])
    yz = jnp.concatenate(ys, axis=1) * _silu(z)
    gwid = SSD_INNER // SSD_GROUPS
    outs = [_rms(yz[:, g * gwid:(g + 1) * gwid], nw[:, g * gwid:(g + 1) * gwid]) for g in range(SSD_GROUPS)]
    return new_states, jnp.concatenate(outs, axis=1)


def _ssd_fwd(xbc_act, p, a_row, dt_row, d_row, nw, name):
    bsz, s, _ = xbc_act.shape
    c = SSD_CHUNK
    nc = s // c

    def body(x_ref, sm_ref, z_ref, a_ref, dt_ref, d_ref, nw_ref, o_ref, st_ref, st_scr):
        @pl.when(pl.program_id(1) == 0)
        def _():
            st_scr[...] = jnp.zeros_like(st_scr)

        st_ref[...] = st_scr[...]
        states = [st_scr[h] for h in range(4)]
        new_states, o = _ssd_chunk(states, x_ref[...], sm_ref[...], z_ref[...], a_ref[...], dt_ref[...], d_ref[...], nw_ref[...])
        for h in range(4):
            st_scr[h] = new_states[h]
        o_ref[...] = o.astype(BF16)

    row = lambda w, blk: pl.BlockSpec((None, c, w), lambda b, n, blk=blk: (b, n, blk))
    prm = pl.BlockSpec((1, LANES), lambda b, n: (0, 0))
    prm5 = pl.BlockSpec((1, 512), lambda b, n: (0, 0))
    return _call(
        body, name, (bsz, nc),
        [row(1024, 0), row(LANES, P_SMALL // LANES), row(512, P_SZ // 512), prm, prm, prm5, prm5],
        [row(512, 0), _state_spec()],
        [_sds((bsz, s, 512), BF16), _sds((bsz, nc, 4, LANES, LANES), F32)],
        scratch=[pltpu.VMEM((4, LANES, LANES), F32)],
    )(xbc_act, p, p, a_row, dt_row, d_row, nw)


def _ssd_bwd(do, xbc_act, p, a_row, dt_row, d_row, nw, st_all, name):
    bsz, s, _ = xbc_act.shape
    c = SSD_CHUNK
    nc = s // c

    def body(x_ref, sm_ref, z_ref, a_ref, dt_ref, d_ref, nw_ref, do_ref, st_ref,
             dx_ref, dsm_ref, dz_ref, da_ref, ddt_ref, dd_ref, dnw_ref, ds_scr):
        first = (pl.program_id(0) == 0) & (pl.program_id(1) == 0)

        @pl.when(pl.program_id(1) == 0)
        def _():
            ds_scr[...] = jnp.zeros_like(ds_scr)

        @pl.when(first)
        def _():
            da_ref[...] = jnp.zeros_like(da_ref)
            ddt_ref[...] = jnp.zeros_like(ddt_ref)
            dd_ref[...] = jnp.zeros_like(dd_ref)
            dnw_ref[...] = jnp.zeros_like(dnw_ref)

        states = [st_ref[h] for h in range(4)]
        _, vjp = jax.vjp(_ssd_chunk, states, x_ref[...], sm_ref[...], z_ref[...], a_ref[...], dt_ref[...], d_ref[...], nw_ref[...])
        d_states, dx, dsm, dz, da, ddt, dd, dnw = vjp(([ds_scr[h] for h in range(4)], do_ref[...]))
        for h in range(4):
            ds_scr[h] = d_states[h]
        dx_ref[...] = dx
        dsm_ref[...] = dsm
        dz_ref[...] = dz.astype(BF16)
        da_ref[...] += da
        ddt_ref[...] += ddt
        dd_ref[...] += dd
        dnw_ref[...] += dnw

    row = lambda w, blk: pl.BlockSpec((None, c, w), lambda b, n, blk=blk: (b, nc - 1 - n, blk))
    prm = pl.BlockSpec((1, LANES), lambda b, n: (0, 0))
    prm5 = pl.BlockSpec((1, 512), lambda b, n: (0, 0))
    return _call(
        body, name, (bsz, nc),
        [row(1024, 0), row(LANES, P_SMALL // LANES), row(512, P_SZ // 512), prm, prm, prm5, prm5, row(512, 0),
         pl.BlockSpec((None, None, 4, LANES, LANES), lambda b, n: (b, nc - 1 - n, 0, 0, 0))],
        [row(1024, 0), row(LANES, 0), row(512, 0), prm, prm, prm5, prm5],
        [_sds((bsz, s, 1024), F32), _sds((bsz, s, LANES), F32), _sds((bsz, s, 512), BF16),
         _sds((1, LANES), F32), _sds((1, LANES), F32), _sds((1, 512), F32), _sds((1, 512), F32)],
        scratch=[pltpu.VMEM((4, LANES, LANES), F32)],
    )(xbc_act, p, p, a_row, dt_row, d_row, nw, do, st_all)


def _peer(k):
    x, y, c = lax.axis_index("x"), lax.axis_index("y"), lax.axis_index("c")
    px = 1 - x if k & 4 else x
    py = 1 - y if k & 2 else y
    pc = 1 - c if k & 1 else c
    return (px, py, pc), 4 * px + 2 * py + pc


def _my_index():
    return 4 * lax.axis_index("x") + 2 * lax.axis_index("y") + lax.axis_index("c")


def _exchange(arrays, name, gather):
    n = len(arrays)

    def body(*refs):
        ins, outs = refs[:n], refs[n:2 * n]
        send_sems, recv_sems, local_sems = refs[2 * n:]
        me = _my_index()

        def copy(i, k):
            peer, slot = _peer(k)
            src = ins[i] if gather else ins[i].at[slot]
            return pltpu.make_async_remote_copy(src_ref=src, dst_ref=outs[i].at[me], send_sem=send_sems.at[k - 1, i],
                                                recv_sem=recv_sems.at[k - 1, i], device_id=peer, device_id_type=MESH_ID)

        def arrival(i, k):
            peer, slot = _peer(k)
            src = ins[i] if gather else ins[i].at[slot]
            return pltpu.make_async_remote_copy(src_ref=src, dst_ref=outs[i].at[slot], send_sem=send_sems.at[k - 1, i],
                                                recv_sem=recv_sems.at[k - 1, i], device_id=peer, device_id_type=MESH_ID)

        mine = [pltpu.make_async_copy(ins[i] if gather else ins[i].at[me], outs[i].at[me], local_sems.at[i])
                for i in range(n)]
        sends = [copy(i, k) for k in range(1, N_DEV) for i in range(n)]
        for cp in mine + sends:
            cp.start()
        for k in range(1, N_DEV):
            for i in range(n):
                arrival(i, k).wait_recv()
        for cp in sends:
            cp.wait_send()
        for cp in mine:
            cp.wait()

    out_shape = [_sds(((N_DEV,) + a.shape) if gather else a.shape, a.dtype) for a in arrays]
    any_spec = pl.BlockSpec(memory_space=pl.ANY)
    return pl.pallas_call(
        body, name=name, out_shape=out_shape, in_specs=[any_spec] * n, out_specs=[any_spec] * n,
        scratch_shapes=[pltpu.SemaphoreType.DMA((N_DEV - 1, n)), pltpu.SemaphoreType.DMA((N_DEV - 1, n)),
                        pltpu.SemaphoreType.DMA((n,))],
    )(*arrays)


def _sum_adamw(gs, w, m, v, name):
    rows, width = w.shape
    slots = gs.shape[0]
    tr = _pick(rows, (128, 64, 32, 16, 8)) if rows % 8 == 0 else rows

    def body(g_ref, w_ref, m_ref, v_ref, go_ref, d_ref, mo_ref, vo_ref):
        g = g_ref[0].astype(F32)
        for i in range(1, slots):
            g = g + g_ref[i].astype(F32)
        m2 = ADAM_B1 * m_ref[...] + (1.0 - ADAM_B1) * g
        v2 = ADAM_B2 * v_ref[...] + (1.0 - ADAM_B2) * (g * g)
        m_hat = m2 / (1.0 - ADAM_B1 ** ADAM_STEP)
        v_hat = v2 / (1.0 - ADAM_B2 ** ADAM_STEP)
        go_ref[...] = g
        d_ref[...] = -ADAM_LR * (m_hat / (jnp.sqrt(v_hat) + ADAM_EPS) + ADAM_WD * w_ref[...])
        mo_ref[...] = m2
        vo_ref[...] = v2

    flat = pl.BlockSpec((tr, width), lambda i: (i, 0))
    return _call(body, name, (rows // tr,), [pl.BlockSpec((slots, tr, width), lambda i: (0, i, 0)), flat, flat, flat],
                 [flat] * 4, [_sds(w.shape, F32)] * 4)(gs, w, m, v)


MATMUL_WEIGHTS = ("w_in", "w_br_a", "w_br_b", "w_br_c", "w_out", "ffn_w_up", "ffn_w_down")
SPLIT = (
    ("w_in", (DEPTH, D_MODEL, 8720), 2),
    ("gdn_conv_w", (DEPTH, 4, 1536), 2), ("ssd_conv_w", (DEPTH, 4, 1024), 2),
    ("w_br_a", (DEPTH, 512, D_MODEL), 2), ("w_br_b", (DEPTH, 512, D_MODEL), 2), ("w_br_c", (DEPTH, 512, D_MODEL), 2),
    ("w_out", (DEPTH, D_MODEL, D_MODEL), 1), ("ffn_w_up", (DEPTH, D_MODEL, 2 * FFN_HIDDEN), 2),
    ("ffn_conv_w", (DEPTH, 3, 2 * FFN_HIDDEN), 2), ("ffn_w_down", (DEPTH, FFN_HIDDEN, D_MODEL), 1),
)
REPL = (
    ("b_ada", (DEPTH, 6 * D_MODEL)), ("norm1_w", (DEPTH, D_MODEL)), ("gdn_a_log", (DEPTH, 4)),
    ("gdn_dt_bias", (DEPTH, 4)), ("gdn_norm_w", (DEPTH, 128)), ("hgrn_lb_param", (DEPTH, 512)),
    ("hgrn_norm_w", (DEPTH, 128)), ("ssd_conv_b", (DEPTH, 1024)), ("ssd_a_log", (DEPTH, 8)),
    ("ssd_dt_bias", (DEPTH, 8)), ("ssd_d", (DEPTH, 8)), ("ssd_norm_w", (DEPTH, 512)), ("norm2_w", (DEPTH, D_MODEL)),
    ("ffn_conv_b", (DEPTH, 2 * FFN_HIDDEN)), ("final_norm_w", (D_MODEL,)),
)
WEIGHTS = ("w_ada", "b_ada", "norm1_w", "w_in", "gdn_conv_w", "gdn_a_log", "gdn_dt_bias", "gdn_norm_w",
           "hgrn_lb_param", "hgrn_norm_w", "ssd_conv_w", "ssd_conv_b", "ssd_a_log", "ssd_dt_bias", "ssd_d",
           "ssd_norm_w", "w_br_a", "w_br_b", "w_br_c", "w_out", "norm2_w", "ffn_w_up", "ffn_conv_w", "ffn_conv_b",
           "ffn_w_down", "final_norm_w")


def _block_shape(shape, axis):
    return tuple(d // N_DEV if i == axis else d for i, d in enumerate(shape))


def _join_blocks(gathered, shape, axis):
    return jnp.moveaxis(gathered, 0, axis).reshape(shape)


def _split_blocks(full, shape, axis):
    bs = _block_shape(shape, axis)
    t = full.reshape(shape[:axis] + (N_DEV, bs[axis]) + shape[axis + 1:])
    return jnp.moveaxis(t, axis, 0)


def _pack_repl(vals):
    parts = []
    for n, shape in REPL:
        size = math.prod(shape)
        parts.append(jnp.pad(vals[n].reshape(-1), (0, -(-size // PACK_W) * PACK_W - size)))
    cat = jnp.concatenate(parts)
    rows = -(-cat.shape[0] // (8 * PACK_W)) * 8
    return jnp.pad(cat, (0, rows * PACK_W - cat.shape[0])).reshape(rows, PACK_W)


def _unpack_repl(packed):
    flat, out, off = packed.reshape(-1), {}, 0
    for n, shape in REPL:
        size = math.prod(shape)
        out[n] = flat[off:off + size].reshape(shape)
        off += -(-size // PACK_W) * PACK_W
    return out


def _lane_row(vec, lane0):
    return jnp.pad(vec, (lane0, LANES - lane0 - vec.shape[0]))[None]


def _arrange_w_in(w):
    offs = [0]
    for sz in W_IN_SPLITS:
        offs.append(offs[-1] + sz)
    qkv, a, b, gz, hq, hf, hi, hg, sz_, xbc, dt, gate = [w[:, offs[i]:offs[i + 1]] for i in range(12)]
    pad = jnp.zeros((w.shape[0], P_WIDTH - P_SMALL - 16), w.dtype)
    return jnp.concatenate([qkv, gz, xbc, gate, hq, hf, hi, hg, sz_, a, b, dt, pad], axis=1)


def _restore_w_in(wp):
    cut = lambda o, n: wp[:, o:o + n]
    return jnp.concatenate([
        cut(P_QKV, 1536), cut(P_SMALL + SM_A, 4), cut(P_SMALL + SM_B, 4), cut(P_GZ, 512), cut(P_HQ, 512),
        cut(P_HF, 512), cut(P_HI, 512), cut(P_HG, 512), cut(P_SZ, 512), cut(P_XBC, 1024), cut(P_SMALL + SM_DT, 8),
        cut(P_GATE, 3072)], axis=1)


def _layer_consts(l, wf, wr, lower):
    t = lambda a: a.T
    k = {}
    k["n1w"], k["n2w"] = wr["norm1_w"][l][None], wr["norm2_w"][l][None]
    win = _arrange_w_in(wf["w_in"][l])
    k["win"], k["win_t"] = win, t(win)
    for n in ("w_br_a", "w_br_b", "w_br_c", "w_out", "ffn_w_up", "ffn_w_down"):
        k[n], k[n + "_t"] = wf[n][l], t(wf[n][l])
    k["gdn_conv_w"], k["gdn_conv_b"] = wf["gdn_conv_w"][l], jnp.zeros((1, 1536), F32)
    k["ssd_conv_w"], k["ssd_conv_b"] = wf["ssd_conv_w"][l], wr["ssd_conv_b"][l][None]
    k["ffn_conv_w"], k["ffn_conv_b"] = wf["ffn_conv_w"][l], wr["ffn_conv_b"][l][None]
    k["gdn_a"], k["gdn_dt"] = _lane_row(wr["gdn_a_log"][l], SM_A), _lane_row(wr["gdn_dt_bias"][l], SM_A)
    k["gdn_nw"], k["hgrn_nw"] = wr["gdn_norm_w"][l][None], wr["hgrn_norm_w"][l][None]
    k["ssd_a"], k["ssd_dt"] = _lane_row(wr["ssd_a_log"][l], SM_DT), _lane_row(wr["ssd_dt_bias"][l], SM_DT)
    k["ssd_d"] = jnp.repeat(wr["ssd_d"][l], SSD_HEAD_DIM)[None]
    k["ssd_nw"] = wr["ssd_norm_w"][l][None]
    k["lb"] = lower[l:l + 1]
    return k


def _layer_fwd(l, x, mod, k):
    bsz, s, d = x.shape
    t = bsz * s
    sv = {"x": x}
    sv["mod"] = [mod[:, None, i * d:(i + 1) * d] for i in range(6)]
    sh1, sc1, g1, sh2, sc2, g2 = sv["mod"]
    h1 = _norm_mod_fwd(x, k["n1w"], sh1, sc1, f"norm1_fwd{l}")
    p = _mm(h1.reshape(t, d), k["win"], F32, f"mm_in{l}").reshape(bsz, s, P_WIDTH)
    qkv_act = _conv_fwd(p, P_QKV, 1536, k["gdn_conv_w"], k["gdn_conv_b"], True, f"gdn_conv_fwd{l}")
    oa, st_a = _gdn_fwd(qkv_act, p, k["gdn_a"], k["gdn_dt"], k["gdn_nw"], f"gdn_fwd{l}")
    ob, st_b = _hgrn_fwd(p, k["lb"], k["hgrn_nw"], f"hgrn_fwd{l}")
    xbc_act = _conv_fwd(p, P_XBC, 1024, k["ssd_conv_w"], k["ssd_conv_b"], True, f"ssd_conv_fwd{l}")
    oc, st_c = _ssd_fwd(xbc_act, p, k["ssd_a"], k["ssd_dt"], k["ssd_d"], k["ssd_nw"], f"ssd_fwd{l}")
    merged = _merge_fwd(p, oa, ob, oc, k["w_br_a"], k["w_br_b"], k["w_br_c"], f"merge_fwd{l}")
    mix = _mm(merged.reshape(t, d), k["w_out"], F32, f"mm_out{l}").reshape(bsz, s, d)
    x1 = _resid_fwd(x, mix, g1, f"resid1_fwd{l}")
    h2 = _norm_mod_fwd(x1, k["n2w"], sh2, sc2, f"norm2_fwd{l}")
    u_pre = _mm(h2.reshape(t, d), k["ffn_w_up"], F32, f"mm_up{l}").reshape(bsz, s, 2 * FFN_HIDDEN)
    u = _conv_fwd(u_pre, 0, 2 * FFN_HIDDEN, k["ffn_conv_w"], k["ffn_conv_b"], False, f"ffn_conv_fwd{l}")
    a = _glu_fwd(u, f"glu_fwd{l}")
    ffn = _mm(a.reshape(t, FFN_HIDDEN), k["ffn_w_down"], F32, f"mm_down{l}").reshape(bsz, s, d)
    x2 = _resid_fwd(x1, ffn, g2, f"resid2_fwd{l}")
    sv.update(h1=h1, p=p, qkv_act=qkv_act, oa=oa, st_a=st_a, ob=ob, st_b=st_b, xbc_act=xbc_act, oc=oc, st_c=st_c,
              merged=merged, mix=mix, x1=x1, h2=h2, u_pre=u_pre, u=u, a=a, ffn=ffn)
    return x2, sv


def _layer_bwd(l, dx2, k, sv):
    bsz, s, d = dx2.shape
    t = bsz * s
    f2 = 2 * FFN_HIDDEN
    sh1, sc1, g1, sh2, sc2, g2 = sv["mod"]
    tr = lambda a: a.reshape(t, -1).T
    g = {}
    dffn, dg2 = _gate_bwd(dx2, sv["ffn"], g2, f"gate2_bwd{l}")
    dffn2 = dffn.reshape(t, d)
    da = _mm(dffn2, k["ffn_w_down_t"], F32, f"mm_down_dx{l}").reshape(bsz, s, FFN_HIDDEN)
    g["ffn_w_down"] = _mm(tr(sv["a"]), dffn2, BF16, f"mm_down_dw{l}")
    du = _glu_bwd(da, sv["u"], f"glu_bwd{l}")
    du_pre, g["ffn_conv_w"], dfcb = _conv_bwd(du, sv["u_pre"], 0, f2, k["ffn_conv_w"], k["ffn_conv_b"], False, f"ffn_conv_bwd{l}")
    g["ffn_conv_b"] = dfcb[0]
    du2 = du_pre.reshape(t, f2)
    dh2 = _mm(du2, k["ffn_w_up_t"], F32, f"mm_up_dx{l}").reshape(bsz, s, d)
    g["ffn_w_up"] = _mm(tr(sv["h2"]), du2, BF16, f"mm_up_dw{l}")
    dx1, dn2w, dsh2, dsc2 = _norm_mod_bwd(sv["x1"], k["n2w"], sh2, sc2, dh2, dx2, f"norm2_bwd{l}")
    g["norm2_w"] = dn2w[0]
    dmix, dg1 = _gate_bwd(dx1, sv["mix"], g1, f"gate1_bwd{l}")
    dmix2 = dmix.reshape(t, d)
    dmerged = _mm(dmix2, k["w_out_t"], F32, f"mm_out_dx{l}").reshape(bsz, s, d)
    g["w_out"] = _mm(tr(sv["merged"]), dmix2, BF16, f"mm_out_dw{l}")
    p = sv["p"]
    dgate, doa, dob, doc, dya, dyb, dyc = _merge_bwd(
        dmerged, p, sv["oa"], sv["ob"], sv["oc"], k["w_br_a"], k["w_br_b"], k["w_br_c"],
        k["w_br_a_t"], k["w_br_b_t"], k["w_br_c_t"], f"merge_bwd{l}")
    g["w_br_a"] = _mm(tr(sv["oa"]), dya.reshape(t, d), BF16, f"mm_bra_dw{l}")
    g["w_br_b"] = _mm(tr(sv["ob"]), dyb.reshape(t, d), BF16, f"mm_brb_dw{l}")
    g["w_br_c"] = _mm(tr(sv["oc"]), dyc.reshape(t, d), BF16, f"mm_brc_dw{l}")
    dxbc_act, dsm_c, dsz, da_c, ddt_c, dd_c, dnw_c = _ssd_bwd(
        doc, sv["xbc_act"], p, k["ssd_a"], k["ssd_dt"], k["ssd_d"], k["ssd_nw"], sv["st_c"], f"ssd_bwd{l}")
    dxbc_raw, g["ssd_conv_w"], dscb = _conv_bwd(dxbc_act, p, P_XBC, 1024, k["ssd_conv_w"], k["ssd_conv_b"], True, f"ssd_conv_bwd{l}")
    g["ssd_conv_b"] = dscb[0]
    g["ssd_a_log"], g["ssd_dt_bias"] = da_c[0, SM_DT:SM_DT + 8], ddt_c[0, SM_DT:SM_DT + 8]
    g["ssd_d"] = dd_c.reshape(SSD_HEADS, SSD_HEAD_DIM).sum(axis=1)
    g["ssd_norm_w"] = dnw_c[0]
    dhg, dlb, dnw_b = _hgrn_bwd(dob, p, k["lb"], k["hgrn_nw"], sv["st_b"], f"hgrn_bwd{l}")
    g["hgrn_norm_w"] = dnw_b[0]
    dqkv_act, dsm_a, dgz, da_a, ddt_a, dnw_a = _gdn_bwd(
        doa, sv["qkv_act"], p, k["gdn_a"], k["gdn_dt"], k["gdn_nw"], sv["st_a"], f"gdn_bwd{l}")
    dqkv_raw, g["gdn_conv_w"], _ = _conv_bwd(dqkv_act, p, P_QKV, 1536, k["gdn_conv_w"], k["gdn_conv_b"], True, f"gdn_conv_bwd{l}")
    g["gdn_a_log"], g["gdn_dt_bias"], g["gdn_norm_w"] = da_a[0, :4], ddt_a[0, :4], dnw_a[0]
    dsmall = jnp.pad((dsm_a + dsm_c).astype(BF16), ((0, 0), (0, 0), (0, P_WIDTH - P_SMALL - LANES)))
    dp = jnp.concatenate([dqkv_raw, dgz, dxbc_raw, dgate, dhg, dsz, dsmall], axis=-1).reshape(t, P_WIDTH)
    dh1 = _mm(dp, k["win_t"], F32, f"mm_in_dx{l}").reshape(bsz, s, d)
    g["w_in"] = _restore_w_in(_mm(tr(sv["h1"]), dp, BF16, f"mm_in_dw{l}"))
    dx, dn1w, dsh1, dsc1 = _norm_mod_bwd(sv["x"], k["n1w"], sh1, sc1, dh1, dx1, f"norm1_bwd{l}")
    g["norm1_w"] = dn1w[0]
    dmod = jnp.concatenate([dsh1, dsc1, dg1, dsh2, dsc2, dg2], axis=-1)[:, 0]
    return dx, g, dlb, dmod


def _local_step(x, mod, wf, wr, target):
    lower = _lb_fwd(wr["hgrn_lb_param"])
    ks = [_layer_consts(l, wf, wr, lower) for l in range(DEPTH)]
    saved = []
    h = x
    for l in range(DEPTH):
        h, sv = _layer_fwd(l, h, mod[l], ks[l])
        saved.append(sv)
    loss8, dh, dfnw = _final_loss(h, wr["final_norm_w"][None], target)
    per_layer, dlbs, dmods = [None] * DEPTH, [None] * DEPTH, [None] * DEPTH
    for l in reversed(range(DEPTH)):
        dh, per_layer[l], dlbs[l], dmods[l] = _layer_bwd(l, dh, ks[l], saved[l])
    grads = {n: jnp.stack([per_layer[l][n] for l in range(DEPTH)]) for n in per_layer[0]}
    grads["hgrn_lb_param"] = _lb_bwd(wr["hgrn_lb_param"], jnp.concatenate(dlbs, axis=0))
    grads["final_norm_w"] = dfnw[0]
    return loss8[0, 0], dh, grads, jnp.stack(dmods)


def kernel(x, c, w_ada, b_ada, norm1_w, w_in, gdn_conv_w, gdn_a_log, gdn_dt_bias, gdn_norm_w, hgrn_lb_param, hgrn_norm_w, ssd_conv_w, ssd_conv_b, ssd_a_log, ssd_dt_bias, ssd_d, ssd_norm_w, w_br_a, w_br_b, w_br_c, w_out, norm2_w, ffn_w_up, ffn_conv_w, ffn_conv_b, ffn_w_down, final_norm_w, loss_target, m_w_ada, m_b_ada, m_norm1_w, m_w_in, m_gdn_conv_w, m_gdn_a_log, m_gdn_dt_bias, m_gdn_norm_w, m_hgrn_lb_param, m_hgrn_norm_w, m_ssd_conv_w, m_ssd_conv_b, m_ssd_a_log, m_ssd_dt_bias, m_ssd_d, m_ssd_norm_w, m_w_br_a, m_w_br_b, m_w_br_c, m_w_out, m_norm2_w, m_ffn_w_up, m_ffn_conv_w, m_ffn_conv_b, m_ffn_w_down, m_final_norm_w, v_w_ada, v_b_ada, v_norm1_w, v_w_in, v_gdn_conv_w, v_gdn_a_log, v_gdn_dt_bias, v_gdn_norm_w, v_hgrn_lb_param, v_hgrn_norm_w, v_ssd_conv_w, v_ssd_conv_b, v_ssd_a_log, v_ssd_dt_bias, v_ssd_d, v_ssd_norm_w, v_w_br_a, v_w_br_b, v_w_br_c, v_w_out, v_norm2_w, v_ffn_w_up, v_ffn_conv_w, v_ffn_conv_b, v_ffn_w_down, v_final_norm_w):
    given = dict(locals())
    w = {n: given[n] for n in WEIGHTS}
    m = {n: given["m_" + n] for n in WEIGHTS}
    v = {n: given["v_" + n] for n in WEIGHTS}
    me = _my_index()
    bsz = c.shape[0]
    ncol = 6 * D_MODEL // N_DEV

    shards = [w[n].astype(BF16) if n in MATMUL_WEIGHTS else w[n] for n, _, _ in SPLIT] + [c]
    gathered = _exchange(shards, "gather_weights", True)
    wf = {n: _join_blocks(g, shape, axis) for (n, shape, axis), g in zip(SPLIT, gathered)}
    c_all = gathered[-1].reshape(N_DEV * bsz, D_MODEL)

    b_cols = lax.dynamic_slice_in_dim(b_ada, me * ncol, ncol, axis=1)[:, None]
    mod_cols = _ada_fwd(c_all, w_ada, b_cols)
    send = mod_cols.reshape(DEPTH, N_DEV, bsz, ncol).transpose(1, 0, 2, 3)
    got = _exchange([send], "scatter_mod", False)[0]
    mod = got.transpose(1, 2, 0, 3).reshape(DEPTH, bsz, 6 * D_MODEL)

    loss, dx, grads, dmod = _local_step(x, mod, wf, w, loss_target)

    send = dmod.reshape(DEPTH, bsz, N_DEV, ncol).transpose(2, 0, 1, 3)
    parts = [_split_blocks(grads[n], shape, axis).astype(BF16) for n, shape, axis in SPLIT] + [send]
    got = _exchange(parts, "scatter_grads", False)
    dmod_all = got[-1].transpose(1, 0, 2, 3).reshape(DEPTH, N_DEV * bsz, ncol)
    g_w_ada, g_b_cols = _ada_bwd(c_all.T, dmod_all)
    grads["b_ada"] = lax.dynamic_update_slice_in_dim(jnp.zeros_like(b_ada), g_b_cols[:, 0], me * ncol, axis=1)

    out = {}
    slots = [(n, g8) for (n, _, _), g8 in zip(SPLIT, got)] + [("w_ada", g_w_ada[None])]
    for n, gs in slots:
        bs = w[n].shape
        two = lambda a: a.reshape(-1, bs[-1])
        res = _sum_adamw(gs.reshape(gs.shape[0], -1, bs[-1]), two(w[n]), two(m[n]), two(v[n]), f"adamw_{n}")
        out[n] = [r.reshape(bs) for r in res]
    r8 = _exchange([_pack_repl(grads)], "gather_small_grads", True)[0]
    res = _sum_adamw(r8, _pack_repl(w), _pack_repl(m), _pack_repl(v), "adamw_repl")
    repl_out = [_unpack_repl(o) for o in res]
    pick = lambda i, n: out[n][i] if n in out else repl_out[i][n]
    loss = lax.psum(loss, ("x", "y", "c"))
    return (loss, dx, *[pick(i, n) for i in range(4) for n in WEIGHTS])
```

```python
import functools
import math

import jax
import jax.numpy as jnp
from jax import lax
from jax.experimental import pallas as pl
from jax.experimental.pallas import tpu as pltpu

F32, BF16 = jnp.float32, jnp.bfloat16
HI = lax.Precision.HIGHEST
MESH_ID = pl.DeviceIdType.MESH

N_DEV = 8
EPS = 1e-6
D_MODEL = 1024
DEPTH = 2
GDN_HEADS, GDN_DK, GDN_CHUNK = 4, 128, 64
HGRN_HEADS, HGRN_DK, HGRN_CHUNK, HGRN_BLOCK = 4, 128, 16, 128
SSD_HEADS, SSD_HEAD_DIM, SSD_GROUPS, SSD_STATE, SSD_CHUNK = 8, 64, 2, 128, 64
SSD_INNER = SSD_HEADS * SSD_HEAD_DIM
FFN_HIDDEN = 2816
LANES = 128
P_QKV, P_GZ, P_XBC, P_GATE, P_HQ, P_HF, P_HI, P_HG, P_SZ, P_SMALL, P_WIDTH = (
    0, 1536, 2048, 3072, 6144, 6656, 7168, 7680, 8192, 8704, 9216)
SM_A, SM_B, SM_DT = 0, 4, 8
W_IN_SPLITS = (1536, 4, 4, 512, 512, 512, 512, 512, 512, 1024, 8, 3072)

ADAM_LR, ADAM_B1, ADAM_B2, ADAM_EPS, ADAM_WD, ADAM_STEP = 0.001, 0.9, 0.999, 1e-08, 0.01, 10

V7X_VMEM_LIMIT = 56 * 1024 * 1024
PACK_W = 1024
PACK_ROWS = 128


def _call(body, name, grid, in_specs, out_specs, out_shape, scratch=()):
    return pl.pallas_call(
        body, name=name, grid=grid, in_specs=in_specs, out_specs=out_specs, out_shape=out_shape,
        scratch_shapes=list(scratch),
        compiler_params=pltpu.CompilerParams(
            dimension_semantics=("arbitrary",) * len(grid), vmem_limit_bytes=V7X_VMEM_LIMIT),
    )


def _pick(n, cands):
    for c in cands:
        if n % c == 0:
            return c
    raise ValueError(f"no tile for {n} among {cands}")


def _sds(shape, dtype):
    return jax.ShapeDtypeStruct(shape, dtype)


def _dot(a, b):
    return lax.dot_general(a, b, (((1,), (0,)), ((), ())), precision=HI, preferred_element_type=F32)


NN, NT, TN = (((1,), (0,)), ((), ())), (((1,), (1,)), ((), ())), (((0,), (0,)), ((), ()))


def _mxu(a, b, dims):
    return lax.dot_general(a.astype(BF16), b.astype(BF16), dims, preferred_element_type=F32)


@jax.custom_vjp
def _bdot(a, b):
    return _mxu(a, b, NN)


@jax.custom_vjp
def _bdot_nt(a, b):
    return _mxu(a, b, NT)


@jax.custom_vjp
def _bdot_tn(a, b):
    return _mxu(a, b, TN)


_bdot.defvjp(lambda a, b: (_mxu(a, b, NN), (a, b)), lambda r, d: (_mxu(d, r[1], NT), _mxu(r[0], d, TN)))
_bdot_nt.defvjp(lambda a, b: (_mxu(a, b, NT), (a, b)), lambda r, d: (_mxu(d, r[1], NN), _mxu(d, r[0], TN)))
_bdot_tn.defvjp(lambda a, b: (_mxu(a, b, TN), (a, b)), lambda r, d: (_mxu(r[1], d, NT), _mxu(r[0], d, NN)))


def _split(x):
    hi = x.astype(BF16)
    return hi, (x - hi.astype(F32)).astype(BF16)


def _mxu3(a, b, dims):
    ah, al = _split(a)
    bh, bl = _split(b)
    return _mxu(ah, bh, dims) + (_mxu(ah, bl, dims) + _mxu(al, bh, dims))


@jax.custom_vjp
def _dot3(a, b):
    return _mxu3(a, b, NN)


_dot3.defvjp(lambda a, b: (_mxu3(a, b, NN), (a, b)), lambda r, d: (_mxu3(d, r[1], NT), _mxu3(r[0], d, TN)))


def _pieces(x):
    x1 = x.astype(BF16)
    r1 = x - x1.astype(F32)
    x2 = r1.astype(BF16)
    return x1, x2, (r1 - x2.astype(F32)).astype(BF16)


def _mask_mxu(mask, x, dims):
    x1, x2, x3 = _pieces(x)
    return _mxu(mask, x1, dims) + (_mxu(mask, x2, dims) + _mxu(mask, x3, dims))


def _spread_mxu(x, mask, dims):
    x1, x2, x3 = _pieces(x)
    return _mxu(x1, mask, dims) + (_mxu(x2, mask, dims) + _mxu(x3, mask, dims))


@jax.custom_vjp
def _mask_dot(mask, x):
    return _mask_mxu(mask, x, NN)


@jax.custom_vjp
def _spread_dot(x, mask):
    return _spread_mxu(x, mask, NN)


_mask_dot.defvjp(lambda m, x: (_mask_mxu(m, x, NN), m), lambda m, d: (jnp.zeros_like(m), _mask_mxu(m, d, TN)))
_spread_dot.defvjp(lambda x, m: (_spread_mxu(x, m, NN), m), lambda m, d: (_spread_mxu(d, m, NT), jnp.zeros_like(m)))


def _iota(shape, axis):
    return lax.broadcasted_iota(jnp.int32, shape, axis)


def _silu(x):
    return x * jax.nn.sigmoid(x)


def _softplus(x):
    return jnp.maximum(x, 0.0) + jnp.log1p(jnp.exp(-jnp.abs(x)))


def _rms(x, w):
    return x * lax.rsqrt(jnp.mean(x * x, axis=-1, keepdims=True) + EPS) * w


def _lane_col(x, lane):
    m = (_iota(x.shape, 1) == lane).astype(F32)
    return jnp.sum(x * m, axis=1, keepdims=True)


def _col_to_row(c):
    n = c.shape[0]
    eye = (_iota((n, n), 0) == _iota((n, n), 1)).astype(F32)
    return jnp.sum(c * eye, axis=0, keepdims=True)


def _tril(n, strict=False):
    r, c = _iota((n, n), 0), _iota((n, n), 1)
    return (r > c) if strict else (r >= c)


def _mm(a, b, out_dtype, name):
    m, k = a.shape
    n = b.shape[1]
    tm = _pick(m, (512, 256, 128, 64, 32, 16, 8))
    tn = _pick(n, (1024, 768, 512, 384, 256, 128))
    tk = k if k <= 3072 else _pick(k, (1024, 768, 512, 384, 256, 128))
    nk = k // tk

    def body(a_ref, b_ref, o_ref, acc_ref):
        kk = pl.program_id(2)

        @pl.when(kk == 0)
        def _():
            acc_ref[...] = jnp.zeros_like(acc_ref)

        acc_ref[...] += _bdot(a_ref[...], b_ref[...])

        @pl.when(kk == nk - 1)
        def _():
            o_ref[...] = acc_ref[...].astype(out_dtype)

    return _call(
        body, name, (m // tm, n // tn, nk),
        [pl.BlockSpec((tm, tk), lambda i, j, kk: (i, kk)), pl.BlockSpec((tk, tn), lambda i, j, kk: (kk, j))],
        pl.BlockSpec((tm, tn), lambda i, j, kk: (i, j)), _sds((m, n), out_dtype),
        scratch=[pltpu.VMEM((tm, tn), F32)],
    )(a, b)


def _ada_fwd(c_all, w, b):
    depth, _, n = w.shape
    rows = c_all.shape[0]

    def body(c_ref, w_ref, b_ref, o_ref):
        o_ref[...] = _dot(_silu(c_ref[...]), w_ref[...]) + b_ref[...]

    return _call(
        body, "ada_fwd", (depth,),
        [pl.BlockSpec((rows, D_MODEL), lambda l: (0, 0)), pl.BlockSpec((None, D_MODEL, n), lambda l: (l, 0, 0)),
         pl.BlockSpec((None, 1, n), lambda l: (l, 0, 0))],
        pl.BlockSpec((None, rows, n), lambda l: (l, 0, 0)), _sds((depth, rows, n), F32),
    )(c_all, w, b)


def _ada_bwd(c_all_t, dmod):
    depth, rows, n = dmod.shape

    def body(ct_ref, dm_ref, dw_ref, db_ref):
        dm = dm_ref[...]
        dw_ref[...] = _dot(_silu(ct_ref[...]), dm)
        db_ref[...] = jnp.sum(dm, axis=0, keepdims=True)

    return _call(
        body, "ada_bwd", (depth,),
        [pl.BlockSpec((D_MODEL, rows), lambda l: (0, 0)), pl.BlockSpec((None, rows, n), lambda l: (l, 0, 0))],
        [pl.BlockSpec((None, D_MODEL, n), lambda l: (l, 0, 0)), pl.BlockSpec((None, 1, n), lambda l: (l, 0, 0))],
        [_sds((depth, D_MODEL, n), F32), _sds((depth, 1, n), F32)],
    )(c_all_t, dmod)


def _lb_fn(p):
    rows = [p[l:l + 1] for l in range(DEPTH)]
    mx = functools.reduce(jnp.maximum, rows)
    es = [jnp.exp(r - mx) for r in rows]
    tot = functools.reduce(lambda a, b: a + b, es)
    sm = [e / tot for e in es]
    out, run = [], None
    for l in range(DEPTH):
        run = sm[l] if run is None else run + sm[l]
        out.append(run - sm[0])
    return jnp.concatenate(out, axis=0)


def _lb_fwd(p):
    def body(p_ref, o_ref):
        o_ref[...] = _lb_fn(p_ref[...])

    full = pl.BlockSpec(p.shape, lambda i: (0, 0))
    return _call(body, "lb_fwd", (1,), [full], full, _sds(p.shape, F32))(p)


def _lb_bwd(p, d_lower):
    def body(p_ref, d_ref, o_ref):
        _, vjp = jax.vjp(_lb_fn, p_ref[...])
        o_ref[...] = vjp(d_ref[...])[0]

    full = pl.BlockSpec(p.shape, lambda i: (0, 0))
    return _call(body, "lb_bwd", (1,), [full, full], full, _sds(p.shape, F32))(p, d_lower)


def _norm_mod_fn(x, w, shift, scale):
    return _rms(x, w) * (1.0 + scale) + shift


def _norm_mod_fwd(x, w, shift, scale, name):
    bsz, s, d = x.shape
    ts = _pick(s, (256, 128, 64, 32, 16, 8))

    def body(x_ref, w_ref, sh_ref, sc_ref, o_ref):
        o_ref[...] = _norm_mod_fn(x_ref[...], w_ref[...], sh_ref[...], sc_ref[...]).astype(BF16)

    row = pl.BlockSpec((None, ts, d), lambda b, i: (b, i, 0))
    per_b = pl.BlockSpec((None, 1, d), lambda b, i: (b, 0, 0))
    return _call(body, name, (bsz, s // ts), [row, pl.BlockSpec((1, d), lambda b, i: (0, 0)), per_b, per_b],
                 row, _sds(x.shape, BF16))(x, w, shift, scale)


def _norm_mod_bwd(x, w, shift, scale, dh, carry, name):
    bsz, s, d = x.shape
    ts = _pick(s, (256, 128, 64, 32, 16, 8))

    def body(x_ref, w_ref, sh_ref, sc_ref, dh_ref, c_ref, dx_ref, dw_ref, dsh_ref, dsc_ref):
        b, i = pl.program_id(0), pl.program_id(1)
        _, vjp = jax.vjp(_norm_mod_fn, x_ref[...], w_ref[...], sh_ref[...], sc_ref[...])
        dx, dw, dsh, dsc = vjp(dh_ref[...])
        dx_ref[...] = dx + c_ref[...]

        @pl.when((b == 0) & (i == 0))
        def _():
            dw_ref[...] = jnp.zeros_like(dw_ref)

        @pl.when(i == 0)
        def _():
            dsh_ref[...] = jnp.zeros_like(dsh_ref)
            dsc_ref[...] = jnp.zeros_like(dsc_ref)

        dw_ref[...] += dw
        dsh_ref[...] += dsh
        dsc_ref[...] += dsc

    row = pl.BlockSpec((None, ts, d), lambda b, i: (b, i, 0))
    per_b = pl.BlockSpec((None, 1, d), lambda b, i: (b, 0, 0))
    wspec = pl.BlockSpec((1, d), lambda b, i: (0, 0))
    return _call(body, name, (bsz, s // ts), [row, wspec, per_b, per_b, row, row],
                 [row, wspec, per_b, per_b],
                 [_sds(x.shape, F32), _sds((1, d), F32), _sds((bsz, 1, d), F32), _sds((bsz, 1, d), F32)],
                 )(x, w, shift, scale, dh, carry)


def _resid_fwd(x, y, gate, name):
    bsz, s, d = x.shape
    ts = _pick(s, (512, 256, 128, 64, 32, 16, 8))

    def body(x_ref, y_ref, g_ref, o_ref):
        o_ref[...] = x_ref[...] + g_ref[...] * y_ref[...]

    row = pl.BlockSpec((None, ts, d), lambda b, i: (b, i, 0))
    per_b = pl.BlockSpec((None, 1, d), lambda b, i: (b, 0, 0))
    return _call(body, name, (bsz, s // ts), [row, row, per_b], row, _sds(x.shape, F32))(x, y, gate)


def _gate_bwd(dx, y, gate, name):
    bsz, s, d = dx.shape
    ts = _pick(s, (512, 256, 128, 64, 32, 16, 8))

    def body(dx_ref, y_ref, g_ref, dy_ref, dg_ref):
        dxv = dx_ref[...]
        dy_ref[...] = (dxv * g_ref[...]).astype(BF16)

        @pl.when(pl.program_id(1) == 0)
        def _():
            dg_ref[...] = jnp.zeros_like(dg_ref)

        dg_ref[...] += jnp.sum(dxv * y_ref[...], axis=0, keepdims=True)

    row = pl.BlockSpec((None, ts, d), lambda b, i: (b, i, 0))
    per_b = pl.BlockSpec((None, 1, d), lambda b, i: (b, 0, 0))
    return _call(body, name, (bsz, s // ts), [row, row, per_b], [row, per_b],
                 [_sds(dx.shape, BF16), _sds((bsz, 1, d), F32)])(dx, y, gate)


HALO = 8


def _conv_pre(xx, w_ref, b_ref, kw, rows):
    acc = w_ref[kw - 1:kw, :] * xx[HALO:HALO + rows]
    for k in range(kw - 1):
        acc = acc + w_ref[k:k + 1, :] * pltpu.roll(xx, kw - 1 - k, 0)[HALO:HALO + rows]
    return acc + b_ref[...]


def _conv_fwd(x, col0, width, w, b, act, name):
    bsz, s, _ = x.shape
    kw = w.shape[0]
    ts = _pick(s, (512, 256, 128, 64, 32, 16, 8))
    tc = _pick(width, (512, 256, 128))
    assert col0 % tc == 0
    c0 = col0 // tc
    hb = ts // HALO

    def body(x_ref, xp_ref, w_ref, b_ref, o_ref):
        i = pl.program_id(1)
        xp = jnp.where(i > 0, xp_ref[...], 0.0)
        xx = jnp.concatenate([xp, x_ref[...]], axis=0)
        pre = _conv_pre(xx, w_ref, b_ref, kw, ts)
        o_ref[...] = _silu(pre) if act else pre

    return _call(
        body, name, (bsz, s // ts, width // tc),
        [pl.BlockSpec((None, ts, tc), lambda bb, i, j: (bb, i, c0 + j)),
         pl.BlockSpec((None, HALO, tc), lambda bb, i, j: (bb, jnp.maximum(i * hb - 1, 0), c0 + j)),
         pl.BlockSpec((kw, tc), lambda bb, i, j: (0, j)), pl.BlockSpec((1, tc), lambda bb, i, j: (0, j))],
        pl.BlockSpec((None, ts, tc), lambda bb, i, j: (bb, i, j)), _sds((bsz, s, width), F32),
    )(x, x, w, b)


def _conv_bwd(dy, x, col0, width, w, b, act, name):
    bsz, s, _ = x.shape
    kw = w.shape[0]
    ts = _pick(s, (512, 256, 128, 64, 32, 16, 8))
    tc = _pick(width, (512, 256, 128))
    c0 = col0 // tc
    hb = ts // HALO
    nt = s // ts
    last_h = s // HALO - 1

    def body(x_ref, xp_ref, xn_ref, dy_ref, dyn_ref, w_ref, b_ref, dx_ref, dw_ref, db_ref):
        bb, i = pl.program_id(1), pl.program_id(2)
        xp = jnp.where(i > 0, xp_ref[...], 0.0)
        xx = jnp.concatenate([xp, x_ref[...], xn_ref[...]], axis=0)
        dyy = jnp.concatenate([dy_ref[...], jnp.where(i < nt - 1, dyn_ref[...], 0.0)], axis=0)
        n = ts + HALO
        if act:
            pre = _conv_pre(xx, w_ref, b_ref, kw, n)
            sg = jax.nn.sigmoid(pre)
            dpre = dyy * (sg * (1.0 + pre * (1.0 - sg)))
        else:
            dpre = dyy
        dx = w_ref[kw - 1:kw, :] * dpre[:ts]
        for k in range(kw - 1):
            dx = dx + w_ref[k:k + 1, :] * pltpu.roll(dpre, n - (kw - 1 - k), 0)[:ts]
        dx_ref[...] = dx.astype(BF16)

        @pl.when((bb == 0) & (i == 0))
        def _():
            dw_ref[...] = jnp.zeros_like(dw_ref)
            db_ref[...] = jnp.zeros_like(db_ref)

        dt = dpre[:ts]
        db_ref[...] += jnp.sum(dt, axis=0, keepdims=True)
        dw_ref[kw - 1:kw, :] += jnp.sum(dt * xx[HALO:HALO + ts], axis=0, keepdims=True)
        for k in range(kw - 1):
            xs = pltpu.roll(xx, kw - 1 - k, 0)[HALO:HALO + ts]
            dw_ref[k:k + 1, :] += jnp.sum(dt * xs, axis=0, keepdims=True)

    xspec = lambda f: pl.BlockSpec((None, HALO, tc), f)
    return _call(
        body, name, (width // tc, bsz, nt),
        [pl.BlockSpec((None, ts, tc), lambda j, bb, i: (bb, i, c0 + j)),
         xspec(lambda j, bb, i: (bb, jnp.maximum(i * hb - 1, 0), c0 + j)),
         xspec(lambda j, bb, i: (bb, jnp.minimum((i + 1) * hb, last_h), c0 + j)),
         pl.BlockSpec((None, ts, tc), lambda j, bb, i: (bb, i, j)),
         xspec(lambda j, bb, i: (bb, jnp.minimum((i + 1) * hb, last_h), j)),
         pl.BlockSpec((kw, tc), lambda j, bb, i: (0, j)), pl.BlockSpec((1, tc), lambda j, bb, i: (0, j))],
        [pl.BlockSpec((None, ts, tc), lambda j, bb, i: (bb, i, j)),
         pl.BlockSpec((kw, tc), lambda j, bb, i: (0, j)), pl.BlockSpec((1, tc), lambda j, bb, i: (0, j))],
        [_sds((bsz, s, width), BF16), _sds((kw, width), F32), _sds((1, width), F32)],
    )(x, x, x, dy, dy, w, b)


def _conv_glu_fwd(x, w, b, name):
    bsz, s, f2 = x.shape
    f = f2 // 2
    kw = w.shape[0]
    ts = _pick(s, (512, 256, 128, 64, 32, 16, 8))
    tc = _pick(f, (256, 128))
    nf = f // tc
    hb = ts // HALO

    def body(xg_ref, xgp_ref, xv_ref, xvp_ref, wg_ref, wv_ref, bg_ref, bv_ref, o_ref):
        i = pl.program_id(1)
        halves = []
        for x_ref, xp_ref, w_ref, b_ref in ((xg_ref, xgp_ref, wg_ref, bg_ref), (xv_ref, xvp_ref, wv_ref, bv_ref)):
            xx = jnp.concatenate([jnp.where(i > 0, xp_ref[...], 0.0), x_ref[...]], axis=0)
            halves.append(_conv_pre(xx, w_ref, b_ref, kw, ts))
        o_ref[...] = (_silu(halves[0]) * halves[1]).astype(BF16)

    tile = lambda off: pl.BlockSpec((None, ts, tc), lambda bb, i, j: (bb, i, off + j))
    prev = lambda off: pl.BlockSpec((None, HALO, tc), lambda bb, i, j: (bb, jnp.maximum(i * hb - 1, 0), off + j))
    wsp = lambda rows, off: pl.BlockSpec((rows, tc), lambda bb, i, j: (0, off + j))
    return _call(
        body, name, (bsz, s // ts, nf),
        [tile(0), prev(0), tile(nf), prev(nf), wsp(kw, 0), wsp(kw, nf), wsp(1, 0), wsp(1, nf)],
        pl.BlockSpec((None, ts, tc), lambda bb, i, j: (bb, i, j)), _sds((bsz, s, f), BF16),
    )(x, x, x, x, w, w, b, b)


def _conv_glu_bwd(da, x, w, b, name):
    bsz, s, f2 = x.shape
    f = f2 // 2
    kw = w.shape[0]
    ts = _pick(s, (512, 256, 128, 64, 32, 16, 8))
    tc = _pick(f, (256, 128))
    nf = f // tc
    hb = ts // HALO
    nt = s // ts
    last_h = s // HALO - 1
    n = ts + HALO

    def body(xg_ref, xgp_ref, xgn_ref, xv_ref, xvp_ref, xvn_ref, da_ref, dan_ref,
             wg_ref, wv_ref, bg_ref, bv_ref, wx_ref, dx_ref, dw_ref, db_ref):
        j, bb, i = pl.program_id(0), pl.program_id(1), pl.program_id(2)
        day = jnp.concatenate([da_ref[...], jnp.where(i < nt - 1, dan_ref[...], 0.0)], axis=0)
        xg = jnp.concatenate([jnp.where(i > 0, xgp_ref[...], 0.0), xg_ref[...], xgn_ref[...]], axis=0)
        pre_g = _conv_pre(xg, wg_ref, bg_ref, kw, n)
        sg = jax.nn.sigmoid(pre_g)

        @pl.when((bb == 0) & (i == 0))
        def _():
            dw_ref[...] = jnp.zeros_like(dw_ref)
            db_ref[...] = jnp.zeros_like(db_ref)

        def finish(dpre, xx):
            dx = wx_ref[kw - 1:kw, :] * dpre[:ts]
            for k in range(kw - 1):
                dx = dx + wx_ref[k:k + 1, :] * pltpu.roll(dpre, n - (kw - 1 - k), 0)[:ts]
            dx_ref[...] = dx.astype(BF16)
            dt = dpre[:ts]
            db_ref[...] += jnp.sum(dt, axis=0, keepdims=True)
            dw_ref[kw - 1:kw, :] += jnp.sum(dt * xx[HALO:HALO + ts], axis=0, keepdims=True)
            for k in range(kw - 1):
                dw_ref[k:k + 1, :] += jnp.sum(dt * pltpu.roll(xx, kw - 1 - k, 0)[HALO:HALO + ts], axis=0, keepdims=True)

        @pl.when(j < nf)
        def _():
            xv = jnp.concatenate([jnp.where(i > 0, xvp_ref[...], 0.0), xv_ref[...], xvn_ref[...]], axis=0)
            pre_v = _conv_pre(xv, wv_ref, bv_ref, kw, n)
            finish(day * pre_v * (sg * (1.0 + pre_g * (1.0 - sg))), xg)

        @pl.when(j >= nf)
        def _():
            xv = jnp.concatenate([jnp.where(i > 0, xvp_ref[...], 0.0), xv_ref[...], xvn_ref[...]], axis=0)
            finish(day * (pre_g * sg), xv)

    tile = lambda off: pl.BlockSpec((None, ts, tc), lambda j, bb, i: (bb, i, off + j % nf))
    prev = lambda off: pl.BlockSpec((None, HALO, tc), lambda j, bb, i: (bb, jnp.maximum(i * hb - 1, 0), off + j % nf))
    nxt = lambda off: pl.BlockSpec((None, HALO, tc), lambda j, bb, i: (bb, jnp.minimum((i + 1) * hb, last_h), off + j % nf))
    wsp = lambda rows, off: pl.BlockSpec((rows, tc), lambda j, bb, i: (0, off + j % nf))
    own = lambda rows: pl.BlockSpec((rows, tc), lambda j, bb, i: (0, j))
    return _call(
        body, name, (2 * nf, bsz, nt),
        [tile(0), prev(0), nxt(0), tile(nf), prev(nf), nxt(nf), tile(0), nxt(0),
         wsp(kw, 0), wsp(kw, nf), wsp(1, 0), wsp(1, nf), own(kw)],
        [pl.BlockSpec((None, ts, tc), lambda j, bb, i: (bb, i, j)), own(kw), own(1)],
        [_sds((bsz, s, f2), BF16), _sds((kw, f2), F32), _sds((1, f2), F32)],
    )(x, x, x, x, x, x, da, da, w, w, b, b, w)


def _merge_fwd(p, oa, ob, oc, wa, wb, wc, name):
    bsz, s, _ = p.shape
    tm = _pick(s, (256, 128, 64, 32, 16, 8))
    gblk = P_GATE // (3 * D_MODEL)

    def body(g_ref, oa_ref, ob_ref, oc_ref, wa_ref, wb_ref, wc_ref, o_ref):
        acc = None
        for i, (o_r, w_r) in enumerate(((oa_ref, wa_ref), (ob_ref, wb_ref), (oc_ref, wc_ref))):
            y = _bdot(o_r[...], w_r[...])
            t = jax.nn.sigmoid(g_ref[:, i * D_MODEL:(i + 1) * D_MODEL]) * y
            acc = t if acc is None else acc + t
        o_ref[...] = acc.astype(BF16)

    orow = pl.BlockSpec((None, tm, 512), lambda b, i: (b, i, 0))
    wfull = pl.BlockSpec((512, D_MODEL), lambda b, i: (0, 0))
    return _call(
        body, name, (bsz, s // tm),
        [pl.BlockSpec((None, tm, 3 * D_MODEL), lambda b, i: (b, i, gblk)), orow, orow, orow, wfull, wfull, wfull],
        pl.BlockSpec((None, tm, D_MODEL), lambda b, i: (b, i, 0)), _sds((bsz, s, D_MODEL), BF16),
    )(p, oa, ob, oc, wa, wb, wc)


def _merge_bwd(dm, p, oa, ob, oc, wa, wb, wc, wat, wbt, wct, name):
    bsz, s, _ = p.shape
    tm = _pick(s, (256, 128, 64, 32, 16, 8))
    gblk = P_GATE // (3 * D_MODEL)

    def body(dm_ref, g_ref, oa_ref, ob_ref, oc_ref, wa_ref, wb_ref, wc_ref, wat_ref, wbt_ref, wct_ref,
             dg_ref, doa_ref, dob_ref, doc_ref, dya_ref, dyb_ref, dyc_ref):
        dmv = dm_ref[...]
        trip = ((oa_ref, wa_ref, wat_ref, doa_ref, dya_ref), (ob_ref, wb_ref, wbt_ref, dob_ref, dyb_ref),
                (oc_ref, wc_ref, wct_ref, doc_ref, dyc_ref))
        for i, (o_r, w_r, wt_r, do_r, dy_r) in enumerate(trip):
            y = _bdot(o_r[...], w_r[...])
            sg = jax.nn.sigmoid(g_ref[:, i * D_MODEL:(i + 1) * D_MODEL])
            dg_ref[:, i * D_MODEL:(i + 1) * D_MODEL] = (dmv * y * sg * (1.0 - sg)).astype(BF16)
            dy = (dmv * sg).astype(BF16)
            dy_r[...] = dy
            do_r[...] = _bdot(dy, wt_r[...])

    orow = pl.BlockSpec((None, tm, 512), lambda b, i: (b, i, 0))
    drow = pl.BlockSpec((None, tm, D_MODEL), lambda b, i: (b, i, 0))
    grow = pl.BlockSpec((None, tm, 3 * D_MODEL), lambda b, i: (b, i, 0))
    wfull = pl.BlockSpec((512, D_MODEL), lambda b, i: (0, 0))
    wtfull = pl.BlockSpec((D_MODEL, 512), lambda b, i: (0, 0))
    return _call(
        body, name, (bsz, s // tm),
        [drow, pl.BlockSpec((None, tm, 3 * D_MODEL), lambda b, i: (b, i, gblk)), orow, orow, orow,
         wfull, wfull, wfull, wtfull, wtfull, wtfull],
        [grow, orow, orow, orow, drow, drow, drow],
        [_sds((bsz, s, 3 * D_MODEL), BF16)] + [_sds((bsz, s, 512), F32)] * 3 + [_sds((bsz, s, D_MODEL), BF16)] * 3,
    )(dm, p, oa, ob, oc, wa, wb, wc, wat, wbt, wct)


def _final_loss(x, w, target):
    bsz, s, d = x.shape
    ts = _pick(s, (256, 128, 64, 32, 16, 8))

    def body(x_ref, w_ref, t_ref, loss_ref, dx_ref, dw_ref):
        first = (pl.program_id(0) == 0) & (pl.program_id(1) == 0)
        y, vjp = jax.vjp(_rms, x_ref[...], w_ref[...])
        err = y - t_ref[...]
        dx, dw = vjp(err * (1.0 / d))
        dx_ref[...] = dx

        @pl.when(first)
        def _():
            loss_ref[...] = jnp.zeros_like(loss_ref)
            dw_ref[...] = jnp.zeros_like(dw_ref)

        loss_ref[...] += 0.5 * jnp.sum(jnp.sum(err * err, axis=1, keepdims=True), axis=0, keepdims=True) * (1.0 / d)
        dw_ref[...] += dw

    row = pl.BlockSpec((None, ts, d), lambda b, i: (b, i, 0))
    wspec = pl.BlockSpec((1, d), lambda b, i: (0, 0))
    return _call(body, "final_loss", (bsz, s // ts), [row, wspec, row],
                 [pl.BlockSpec((8, LANES), lambda b, i: (0, 0)), row, wspec],
                 [_sds((8, LANES), F32), _sds(x.shape, F32), _sds((1, d), F32)])(x, w, target)


def _unit_lower_inverse(m):
    n = m.shape[0]
    eye = (_iota((n, n), 0) == _iota((n, n), 1)).astype(F32)
    p = -m
    x = eye + p
    for _ in range(int(math.log2(n)) - 1):
        p = _mxu3(p, p, NN)
        x = x + _mxu3(x, p, NN)
    return x


@jax.custom_vjp
def _known_inverse(m, t):
    return t


_known_inverse.defvjp(lambda m, t: (t, t), lambda t, dt: (-_mxu3(t, _mxu3(dt, t, NT), TN), jnp.zeros_like(t)))


def _gdn_chunk(states, qkv, small, z, a_row, dt_row, nw, tinvs=None):
    c = qkv.shape[0]
    kw = GDN_HEADS * GDN_DK
    g_all = -jnp.exp(a_row) * _softplus(small + dt_row)
    beta_all = jax.nn.sigmoid(small)
    incl, strict = _tril(c), _tril(c, True)
    big_g_all = _mask_dot(incl.astype(BF16), g_all)
    outs, new_states, kept = [], [], []
    for h in range(GDN_HEADS):
        sl = slice(h * GDN_DK, (h + 1) * GDN_DK)
        q = qkv[:, sl]
        k = qkv[:, kw + h * GDN_DK:kw + (h + 1) * GDN_DK]
        v = qkv[:, 2 * kw + h * GDN_DK:2 * kw + (h + 1) * GDN_DK]
        q = q * lax.rsqrt(jnp.sum(q * q, axis=-1, keepdims=True) + EPS) * (GDN_DK ** -0.5)
        k = k * lax.rsqrt(jnp.sum(k * k, axis=-1, keepdims=True) + EPS)
        gc = _lane_col(big_g_all, SM_A + h)
        bc = _lane_col(beta_all, SM_B + h)
        g_last = jnp.sum(_lane_col(g_all, SM_A + h), axis=0, keepdims=True)
        diff = gc - _col_to_row(gc)
        decay = jnp.where(incl, jnp.exp(jnp.where(incl, diff, 0.0)), 0.0)
        kb = k * bc
        m = jnp.where(strict, _bdot_nt(kb, k) * decay, 0.0)
        tinv = _unit_lower_inverse(m) if tinvs is None else _known_inverse(m, tinvs[h])
        kept.append(tinv)
        eg = jnp.exp(gc)
        u = _dot3(tinv, v * bc)
        w = _dot3(tinv, kb * eg)
        attn = _bdot_nt(q, k) * decay
        st = states[h]
        v_new = u - _bdot(w, st)
        o = _bdot(q * eg, st) + _bdot(attn, v_new)
        new_states.append(st * jnp.exp(g_last) + _bdot_tn(k * jnp.exp(g_last - gc), v_new))
        outs.append(_rms(o, nw) * _silu(z[:, sl]))
    return new_states, jnp.concatenate(outs, axis=1), kept


def _gdn_specs(c):
    row = lambda w, blk: pl.BlockSpec((None, c, w), lambda b, n, blk=blk: (b, n, blk))
    prm = pl.BlockSpec((1, LANES), lambda b, n: (0, 0))
    return [row(1536, 0), row(LANES, P_SMALL // LANES), row(512, P_GZ // 512), prm, prm, prm]


def _state_spec():
    return pl.BlockSpec((None, None, 4, LANES, LANES), lambda b, n: (b, n, 0, 0, 0))


def _gdn_fwd(qkv_act, p, a_row, dt_row, nw, name):
    bsz, s, _ = qkv_act.shape
    c = GDN_CHUNK
    nc = s // c

    def body(qkv_ref, sm_ref, z_ref, a_ref, dt_ref, nw_ref, o_ref, st_ref, ti_ref, st_scr):
        @pl.when(pl.program_id(1) == 0)
        def _():
            st_scr[...] = jnp.zeros_like(st_scr)

        st_ref[...] = st_scr[...]
        states = [st_scr[h] for h in range(GDN_HEADS)]
        new_states, o, tinvs = _gdn_chunk(states, qkv_ref[...], sm_ref[...], z_ref[...], a_ref[...], dt_ref[...], nw_ref[...])
        for h in range(GDN_HEADS):
            st_scr[h] = new_states[h]
            ti_ref[h] = tinvs[h]
        o_ref[...] = o.astype(BF16)

    return _call(
        body, name, (bsz, nc), _gdn_specs(c),
        [pl.BlockSpec((None, c, 512), lambda b, n: (b, n, 0)), _state_spec(),
         pl.BlockSpec((None, None, 4, c, c), lambda b, n: (b, n, 0, 0, 0))],
        [_sds((bsz, s, 512), BF16), _sds((bsz, nc, 4, LANES, LANES), F32), _sds((bsz, nc, 4, c, c), F32)],
        scratch=[pltpu.VMEM((4, LANES, LANES), F32)],
    )(qkv_act, p, p, a_row, dt_row, nw)


def _gdn_bwd(do, qkv_act, p, a_row, dt_row, nw, st_all, ti_all, name):
    bsz, s, _ = qkv_act.shape
    c = GDN_CHUNK
    nc = s // c

    def body(qkv_ref, sm_ref, z_ref, a_ref, dt_ref, nw_ref, do_ref, st_ref, ti_ref,
             dqkv_ref, dsm_ref, dz_ref, da_ref, ddt_ref, dnw_ref, ds_scr):
        first = (pl.program_id(0) == 0) & (pl.program_id(1) == 0)

        @pl.when(pl.program_id(1) == 0)
        def _():
            ds_scr[...] = jnp.zeros_like(ds_scr)

        @pl.when(first)
        def _():
            da_ref[...] = jnp.zeros_like(da_ref)
            ddt_ref[...] = jnp.zeros_like(ddt_ref)
            dnw_ref[...] = jnp.zeros_like(dnw_ref)

        states = [st_ref[h] for h in range(GDN_HEADS)]
        tinvs = [ti_ref[h] for h in range(GDN_HEADS)]
        chunk = lambda *a: _gdn_chunk(*a, tinvs=tinvs)[:2]
        _, vjp = jax.vjp(chunk, states, qkv_ref[...], sm_ref[...], z_ref[...], a_ref[...], dt_ref[...], nw_ref[...])
        d_states, dqkv, dsm, dz, da, ddt, dnw = vjp(([ds_scr[h] for h in range(GDN_HEADS)], do_ref[...]))
        for h in range(GDN_HEADS):
            ds_scr[h] = d_states[h]
        dqkv_ref[...] = dqkv
        dsm_ref[...] = dsm
        dz_ref[...] = dz.astype(BF16)
        da_ref[...] += da
        ddt_ref[...] += ddt
        dnw_ref[...] += dnw

    rrow = lambda w, blk: pl.BlockSpec((None, c, w), lambda b, n, blk=blk: (b, nc - 1 - n, blk))
    prm = pl.BlockSpec((1, LANES), lambda b, n: (0, 0))
    return _call(
        body, name, (bsz, nc),
        [rrow(1536, 0), rrow(LANES, P_SMALL // LANES), rrow(512, P_GZ // 512), prm, prm, prm, rrow(512, 0),
         pl.BlockSpec((None, None, 4, LANES, LANES), lambda b, n: (b, nc - 1 - n, 0, 0, 0)),
         pl.BlockSpec((None, None, 4, c, c), lambda b, n: (b, nc - 1 - n, 0, 0, 0))],
        [rrow(1536, 0), rrow(LANES, 0), rrow(512, 0), prm, prm, prm],
        [_sds((bsz, s, 1536), F32), _sds((bsz, s, LANES), F32), _sds((bsz, s, 512), BF16)] + [_sds((1, LANES), F32)] * 3,
        scratch=[pltpu.VMEM((4, LANES, LANES), F32)],
    )(qkv_act, p, p, a_row, dt_row, nw, do, st_all, ti_all)


def _hgrn_block(states, q_raw, f_raw, i_raw, g_raw, lb, nw):
    n = q_raw.shape[0]
    c = HGRN_CHUNK
    r, cc = _iota((n, n), 0), _iota((n, n), 1)
    same = (r // c) == (cc // c)
    causal = same & (r >= cc)
    ref_row = (r // c) * c + (c // 2 - 1)
    run_sum = causal.astype(F32)
    rel_sum = run_sum - (same & (ref_row >= cc)).astype(F32)
    sums = jnp.concatenate([run_sum, rel_sum, same.astype(F32)], axis=0).astype(BF16)
    outs, new_states = [], []
    for h in range(HGRN_HEADS):
        sl = slice(h * HGRN_DK, (h + 1) * HGRN_DK)
        fr, lbh = f_raw[:, sl], lb[:, sl]
        q = _silu(q_raw[:, sl])
        logf = jnp.log(lbh + (1.0 - lbh) * jax.nn.sigmoid(fr))
        k = (1.0 - lbh) * jax.nn.sigmoid(-fr)
        v = i_raw[:, sl]
        all_sums = _mask_dot(sums, logf)
        big_g, g_rel, g_tot = all_sums[:n], all_sums[n:2 * n], all_sums[2 * n:]
        scores = _bdot_nt(q * jnp.exp(g_rel), k * jnp.exp(-g_rel))
        o_intra = _bdot(jnp.where(causal, scores, 0.0), v)
        qg = q * jnp.exp(big_g)
        k_end = k * jnp.exp(g_tot - big_g)
        st = states[h]
        parts = []
        for j in range(n // c):
            rows = slice(j * c, (j + 1) * c)
            parts.append(_bdot_nt(qg[rows], st))
            st = st * jnp.exp(g_tot[j * c:j * c + 1]) + _bdot_tn(v[rows], k_end[rows])
        new_states.append(st)
        o = o_intra + jnp.concatenate(parts, axis=0)
        outs.append(_rms(o, nw) * _silu(g_raw[:, sl]))
    return new_states, jnp.concatenate(outs, axis=1)


def _hgrn_fwd(p, lb, nw, name):
    bsz, s, _ = p.shape
    n = HGRN_BLOCK
    nb = s // n

    def body(q_ref, f_ref, i_ref, g_ref, lb_ref, nw_ref, o_ref, st_ref, st_scr):
        @pl.when(pl.program_id(1) == 0)
        def _():
            st_scr[...] = jnp.zeros_like(st_scr)

        st_ref[...] = st_scr[...]
        states = [st_scr[h] for h in range(HGRN_HEADS)]
        new_states, o = _hgrn_block(states, q_ref[...], f_ref[...], i_ref[...], g_ref[...], lb_ref[...], nw_ref[...])
        for h in range(HGRN_HEADS):
            st_scr[h] = new_states[h]
        o_ref[...] = o.astype(BF16)

    row = lambda blk: pl.BlockSpec((None, n, 512), lambda b, i, blk=blk: (b, i, blk))
    return _call(
        body, name, (bsz, nb),
        [row(P_HQ // 512), row(P_HF // 512), row(P_HI // 512), row(P_HG // 512),
         pl.BlockSpec((1, 512), lambda b, i: (0, 0)), pl.BlockSpec((1, LANES), lambda b, i: (0, 0))],
        [row(0), _state_spec()],
        [_sds((bsz, s, 512), BF16), _sds((bsz, nb, 4, LANES, LANES), F32)],
        scratch=[pltpu.VMEM((4, LANES, LANES), F32)],
    )(p, p, p, p, lb, nw)


def _hgrn_bwd(do, p, lb, nw, st_all, name):
    bsz, s, _ = p.shape
    n = HGRN_BLOCK
    nb = s // n

    def body(q_ref, f_ref, i_ref, g_ref, lb_ref, nw_ref, do_ref, st_ref, dp_ref, dlb_ref, dnw_ref, ds_scr):
        first = (pl.program_id(0) == 0) & (pl.program_id(1) == 0)

        @pl.when(pl.program_id(1) == 0)
        def _():
            ds_scr[...] = jnp.zeros_like(ds_scr)

        @pl.when(first)
        def _():
            dlb_ref[...] = jnp.zeros_like(dlb_ref)
            dnw_ref[...] = jnp.zeros_like(dnw_ref)

        states = [st_ref[h] for h in range(HGRN_HEADS)]
        _, vjp = jax.vjp(_hgrn_block, states, q_ref[...], f_ref[...], i_ref[...], g_ref[...], lb_ref[...], nw_ref[...])
        d_states, dq, df, di, dg, dlb, dnw = vjp(([ds_scr[h] for h in range(HGRN_HEADS)], do_ref[...]))
        for h in range(HGRN_HEADS):
            ds_scr[h] = d_states[h]
        for j, t in enumerate((dq, df, di, dg)):
            dp_ref[:, j * 512:(j + 1) * 512] = t.astype(BF16)
        dlb_ref[...] += dlb
        dnw_ref[...] += dnw

    row = lambda blk: pl.BlockSpec((None, n, 512), lambda b, i, blk=blk: (b, nb - 1 - i, blk))
    return _call(
        body, name, (bsz, nb),
        [row(P_HQ // 512), row(P_HF // 512), row(P_HI // 512), row(P_HG // 512),
         pl.BlockSpec((1, 512), lambda b, i: (0, 0)), pl.BlockSpec((1, LANES), lambda b, i: (0, 0)), row(0),
         pl.BlockSpec((None, None, 4, LANES, LANES), lambda b, i: (b, nb - 1 - i, 0, 0, 0))],
        [pl.BlockSpec((None, n, 2048), lambda b, i: (b, nb - 1 - i, 0)),
         pl.BlockSpec((1, 512), lambda b, i: (0, 0)), pl.BlockSpec((1, LANES), lambda b, i: (0, 0))],
        [_sds((bsz, s, 2048), BF16), _sds((1, 512), F32), _sds((1, LANES), F32)],
        scratch=[pltpu.VMEM((4, LANES, LANES), F32)],
    )(p, p, p, p, lb, nw, do, st_all)


def _ssd_chunk(states, xbc, small, z, a_row, dt_row, d_row, nw):
    c = xbc.shape[0]
    incl = _tril(c)
    dt_all = _softplus(small + dt_row)
    da_all = dt_all * (-jnp.exp(a_row))
    spread = (_iota((LANES, SSD_INNER), 0) == SM_DT + _iota((LANES, SSD_INNER), 1) // SSD_HEAD_DIM).astype(BF16)
    both = _spread_dot(jnp.concatenate([dt_all, da_all], axis=0), spread)
    dt_e, da_e = both[:c], both[c:]
    acs_e = _mask_dot(incl.astype(BF16), da_e)
    last_e = jnp.sum(da_e, axis=0, keepdims=True)
    xs = xbc[:, :SSD_INNER]
    xdt = xs * dt_e
    gw = SSD_GROUPS * SSD_STATE
    lane = _iota((1, LANES), 1)
    ys, new_states = [], []
    for j in range(4):
        g = j // 2
        bg = xbc[:, SSD_INNER + g * SSD_STATE:SSD_INNER + (g + 1) * SSD_STATE]
        cg = xbc[:, SSD_INNER + gw + g * SSD_STATE:SSD_INNER + gw + (g + 1) * SSD_STATE]
        cb = _bdot_nt(cg, bg)
        sl = slice(j * LANES, (j + 1) * LANES)
        xblk, acs, last = xdt[:, sl], acs_e[:, sl], last_e[:, sl]
        y = None
        for sub in range(2):
            ac = acs[:, sub * SSD_HEAD_DIM:sub * SSD_HEAD_DIM + 1]
            seg = jnp.where(incl, jnp.exp(jnp.where(incl, ac - _col_to_row(ac), 0.0)), 0.0)
            mine = ((lane // SSD_HEAD_DIM) == sub).astype(F32)
            t = _bdot(cb * seg, xblk * mine)
            y = t if y is None else y + t
        st = states[j]
        y = y + _bdot(cg, st) * jnp.exp(acs)
        new_states.append(st * jnp.exp(last) + _bdot_tn(bg, xblk * jnp.exp(last - acs)))
        ys.append(y + d_row[:, sl] * xs[:, sl])
    yz = jnp.concatenate(ys, axis=1) * _silu(z)
    gwid = SSD_INNER // SSD_GROUPS
    outs = [_rms(yz[:, g * gwid:(g + 1) * gwid], nw[:, g * gwid:(g + 1) * gwid]) for g in range(SSD_GROUPS)]
    return new_states, jnp.concatenate(outs, axis=1)


def _ssd_fwd(xbc_act, p, a_row, dt_row, d_row, nw, name):
    bsz, s, _ = xbc_act.shape
    c = SSD_CHUNK
    nc = s // c

    def body(x_ref, sm_ref, z_ref, a_ref, dt_ref, d_ref, nw_ref, o_ref, st_ref, st_scr):
        @pl.when(pl.program_id(1) == 0)
        def _():
            st_scr[...] = jnp.zeros_like(st_scr)

        st_ref[...] = st_scr[...]
        states = [st_scr[h] for h in range(4)]
        new_states, o = _ssd_chunk(states, x_ref[...], sm_ref[...], z_ref[...], a_ref[...], dt_ref[...], d_ref[...], nw_ref[...])
        for h in range(4):
            st_scr[h] = new_states[h]
        o_ref[...] = o.astype(BF16)

    row = lambda w, blk: pl.BlockSpec((None, c, w), lambda b, n, blk=blk: (b, n, blk))
    prm = pl.BlockSpec((1, LANES), lambda b, n: (0, 0))
    prm5 = pl.BlockSpec((1, 512), lambda b, n: (0, 0))
    return _call(
        body, name, (bsz, nc),
        [row(1024, 0), row(LANES, P_SMALL // LANES), row(512, P_SZ // 512), prm, prm, prm5, prm5],
        [row(512, 0), _state_spec()],
        [_sds((bsz, s, 512), BF16), _sds((bsz, nc, 4, LANES, LANES), F32)],
        scratch=[pltpu.VMEM((4, LANES, LANES), F32)],
    )(xbc_act, p, p, a_row, dt_row, d_row, nw)


def _ssd_bwd(do, xbc_act, p, a_row, dt_row, d_row, nw, st_all, name):
    bsz, s, _ = xbc_act.shape
    c = SSD_CHUNK
    nc = s // c

    def body(x_ref, sm_ref, z_ref, a_ref, dt_ref, d_ref, nw_ref, do_ref, st_ref,
             dx_ref, dsm_ref, dz_ref, da_ref, ddt_ref, dd_ref, dnw_ref, ds_scr):
        first = (pl.program_id(0) == 0) & (pl.program_id(1) == 0)

        @pl.when(pl.program_id(1) == 0)
        def _():
            ds_scr[...] = jnp.zeros_like(ds_scr)

        @pl.when(first)
        def _():
            da_ref[...] = jnp.zeros_like(da_ref)
            ddt_ref[...] = jnp.zeros_like(ddt_ref)
            dd_ref[...] = jnp.zeros_like(dd_ref)
            dnw_ref[...] = jnp.zeros_like(dnw_ref)

        states = [st_ref[h] for h in range(4)]
        _, vjp = jax.vjp(_ssd_chunk, states, x_ref[...], sm_ref[...], z_ref[...], a_ref[...], dt_ref[...], d_ref[...], nw_ref[...])
        d_states, dx, dsm, dz, da, ddt, dd, dnw = vjp(([ds_scr[h] for h in range(4)], do_ref[...]))
        for h in range(4):
            ds_scr[h] = d_states[h]
        dx_ref[...] = dx
        dsm_ref[...] = dsm
        dz_ref[...] = dz.astype(BF16)
        da_ref[...] += da
        ddt_ref[...] += ddt
        dd_ref[...] += dd
        dnw_ref[...] += dnw

    row = lambda w, blk: pl.BlockSpec((None, c, w), lambda b, n, blk=blk: (b, nc - 1 - n, blk))
    prm = pl.BlockSpec((1, LANES), lambda b, n: (0, 0))
    prm5 = pl.BlockSpec((1, 512), lambda b, n: (0, 0))
    return _call(
        body, name, (bsz, nc),
        [row(1024, 0), row(LANES, P_SMALL // LANES), row(512, P_SZ // 512), prm, prm, prm5, prm5, row(512, 0),
         pl.BlockSpec((None, None, 4, LANES, LANES), lambda b, n: (b, nc - 1 - n, 0, 0, 0))],
        [row(1024, 0), row(LANES, 0), row(512, 0), prm, prm, prm5, prm5],
        [_sds((bsz, s, 1024), F32), _sds((bsz, s, LANES), F32), _sds((bsz, s, 512), BF16),
         _sds((1, LANES), F32), _sds((1, LANES), F32), _sds((1, 512), F32), _sds((1, 512), F32)],
        scratch=[pltpu.VMEM((4, LANES, LANES), F32)],
    )(xbc_act, p, p, a_row, dt_row, d_row, nw, do, st_all)


def _peer(k):
    x, y, c = lax.axis_index("x"), lax.axis_index("y"), lax.axis_index("c")
    px = 1 - x if k & 4 else x
    py = 1 - y if k & 2 else y
    pc = 1 - c if k & 1 else c
    return (px, py, pc), 4 * px + 2 * py + pc


def _my_index():
    return 4 * lax.axis_index("x") + 2 * lax.axis_index("y") + lax.axis_index("c")


def _exchange(arrays, name, gather):
    n = len(arrays)

    def body(*refs):
        ins, outs = refs[:n], refs[n:2 * n]
        send_sems, recv_sems, local_sems = refs[2 * n:]
        me = _my_index()

        def copy(i, k):
            peer, slot = _peer(k)
            src = ins[i] if gather else ins[i].at[slot]
            return pltpu.make_async_remote_copy(src_ref=src, dst_ref=outs[i].at[me], send_sem=send_sems.at[k - 1, i],
                                                recv_sem=recv_sems.at[k - 1, i], device_id=peer, device_id_type=MESH_ID)

        def arrival(i, k):
            peer, slot = _peer(k)
            src = ins[i] if gather else ins[i].at[slot]
            return pltpu.make_async_remote_copy(src_ref=src, dst_ref=outs[i].at[slot], send_sem=send_sems.at[k - 1, i],
                                                recv_sem=recv_sems.at[k - 1, i], device_id=peer, device_id_type=MESH_ID)

        mine = [pltpu.make_async_copy(ins[i] if gather else ins[i].at[me], outs[i].at[me], local_sems.at[i])
                for i in range(n)]
        sends = [copy(i, k) for k in range(1, N_DEV) for i in range(n)]
        for cp in mine + sends:
            cp.start()
        for k in range(1, N_DEV):
            for i in range(n):
                arrival(i, k).wait_recv()
        for cp in sends:
            cp.wait_send()
        for cp in mine:
            cp.wait()

    out_shape = [_sds(((N_DEV,) + a.shape) if gather else a.shape, a.dtype) for a in arrays]
    any_spec = pl.BlockSpec(memory_space=pl.ANY)
    return pl.pallas_call(
        body, name=name, out_shape=out_shape, in_specs=[any_spec] * n, out_specs=[any_spec] * n,
        scratch_shapes=[pltpu.SemaphoreType.DMA((N_DEV - 1, n)), pltpu.SemaphoreType.DMA((N_DEV - 1, n)),
                        pltpu.SemaphoreType.DMA((n,))],
    )(*arrays)


def _mesh_place():
    x, y, c = lax.axis_index("x"), lax.axis_index("y"), lax.axis_index("c")
    return (x, y, c), (x, y, 1 - c), [(1 - x, y), (x, 1 - y), (1 - x, 1 - y)]


def _run_exchange(body, name, arrays, out_shape, n_sems):
    n = len(arrays)
    any_spec = pl.BlockSpec(memory_space=pl.ANY)
    return pl.pallas_call(
        body, name=name, out_shape=out_shape, in_specs=[any_spec] * n, out_specs=[any_spec] * n,
        scratch_shapes=[pltpu.SemaphoreType.DMA((n_sems, n)), pltpu.SemaphoreType.DMA((n_sems, n)),
                        pltpu.SemaphoreType.DMA((n,))],
    )(*arrays)


def _gather_two_level(arrays, name):
    n = len(arrays)

    def body(*refs):
        ins, outs = refs[:n], refs[n:2 * n]
        send_sems, recv_sems, local_sems = refs[2 * n:]
        (x, y, c), sibling, chips = _mesh_place()
        slot = lambda px, py, pc: 4 * px + 2 * py + pc

        def copy(i, k, block, to, src=None):
            return pltpu.make_async_remote_copy(
                src_ref=outs[i].at[block] if src is None else src, dst_ref=outs[i].at[block],
                send_sem=send_sems.at[k, i], recv_sem=recv_sems.at[k, i], device_id=to, device_id_type=MESH_ID)

        me = slot(x, y, c)
        mine = [pltpu.make_async_copy(ins[i], outs[i].at[me], local_sems.at[i]) for i in range(n)]
        first = [copy(i, 0, me, sibling, src=ins[i]) for i in range(n)]
        first += [copy(i, 1 + j, me, (*chip, c), src=ins[i]) for j, chip in enumerate(chips) for i in range(n)]
        for cp in mine + first:
            cp.start()
        passed = []
        for j, chip in enumerate(chips):
            for i in range(n):
                copy(i, 1 + j, slot(*chip, c), (x, y, c)).wait_recv()
                cp = copy(i, 4 + j, slot(*chip, c), sibling)
                cp.start()
                passed.append(cp)
        for i in range(n):
            copy(i, 0, slot(x, y, 1 - c), (x, y, c)).wait_recv()
        for j, chip in enumerate(chips):
            for i in range(n):
                copy(i, 4 + j, slot(*chip, 1 - c), (x, y, c)).wait_recv()
        for cp in first + passed:
            cp.wait_send()
        for cp in mine:
            cp.wait()

    out_shape = [_sds((N_DEV,) + a.shape, a.dtype) for a in arrays]
    return _run_exchange(body, name, arrays, out_shape, 7)


def _sibling_swap(arrays, name):
    n = len(arrays)

    def body(*refs):
        ins, outs = refs[:n], refs[n:2 * n]
        send_sems, recv_sems, _ = refs[2 * n:]
        (x, y, c), sibling, _ = _mesh_place()
        copies = [pltpu.make_async_remote_copy(
            src_ref=ins[i].at[1 - c], dst_ref=outs[i], send_sem=send_sems.at[0, i], recv_sem=recv_sems.at[0, i],
            device_id=sibling, device_id_type=MESH_ID) for i in range(n)]
        for cp in copies:
            cp.start()
        for cp in copies:
            cp.wait()

    out_shape = [_sds(a.shape[1:], a.dtype) for a in arrays]
    return _run_exchange(body, name, arrays, out_shape, 1)


def _chip_scatter(arrays, name):
    n = len(arrays)

    def body(*refs):
        ins, outs = refs[:n], refs[n:2 * n]
        send_sems, recv_sems, local_sems = refs[2 * n:]
        (x, y, c), _, chips = _mesh_place()
        me = 2 * x + y
        mine = [pltpu.make_async_copy(ins[i].at[me], outs[i].at[me], local_sems.at[i]) for i in range(n)]
        sends = [pltpu.make_async_remote_copy(
            src_ref=ins[i].at[2 * chip[0] + chip[1]], dst_ref=outs[i].at[me], send_sem=send_sems.at[j, i],
            recv_sem=recv_sems.at[j, i], device_id=(*chip, c), device_id_type=MESH_ID)
            for j, chip in enumerate(chips) for i in range(n)]
        for cp in mine + sends:
            cp.start()
        for j, chip in enumerate(chips):
            for i in range(n):
                pltpu.make_async_remote_copy(
                    src_ref=ins[i].at[me], dst_ref=outs[i].at[2 * chip[0] + chip[1]], send_sem=send_sems.at[j, i],
                    recv_sem=recv_sems.at[j, i], device_id=(*chip, c), device_id_type=MESH_ID).wait_recv()
        for cp in sends:
            cp.wait_send()
        for cp in mine:
            cp.wait()

    out_shape = [_sds(a.shape, a.dtype) for a in arrays]
    return _run_exchange(body, name, arrays, out_shape, 3)


def _pair_sum(a, b, name):
    lead, rows, width = a.shape
    tr = _pick(rows, (256, 128, 64, 32, 16, 8)) if rows % 8 == 0 else rows

    def body(a_ref, b_ref, o_ref):
        o_ref[...] = (a_ref[...].astype(F32) + b_ref[...].astype(F32)).astype(o_ref.dtype)

    blk = pl.BlockSpec((None, tr, width), lambda l, i: (l, i, 0))
    return _call(body, name, (lead, rows // tr), [blk, blk], blk, _sds(a.shape, a.dtype))(a, b)


def _sum_adamw(gs, w, m, v, name):
    rows, width = w.shape
    slots = gs.shape[0]
    tr = _pick(rows, (128, 64, 32, 16, 8)) if rows % 8 == 0 else rows

    def body(g_ref, w_ref, m_ref, v_ref, go_ref, d_ref, mo_ref, vo_ref):
        g = g_ref[0].astype(F32)
        for i in range(1, slots):
            g = g + g_ref[i].astype(F32)
        m2 = ADAM_B1 * m_ref[...] + (1.0 - ADAM_B1) * g
        v2 = ADAM_B2 * v_ref[...] + (1.0 - ADAM_B2) * (g * g)
        m_hat = m2 / (1.0 - ADAM_B1 ** ADAM_STEP)
        v_hat = v2 / (1.0 - ADAM_B2 ** ADAM_STEP)
        go_ref[...] = g
        d_ref[...] = -ADAM_LR * (m_hat / (jnp.sqrt(v_hat) + ADAM_EPS) + ADAM_WD * w_ref[...])
        mo_ref[...] = m2
        vo_ref[...] = v2

    flat = pl.BlockSpec((tr, width), lambda i: (i, 0))
    return _call(body, name, (rows // tr,), [pl.BlockSpec((slots, tr, width), lambda i: (0, i, 0)), flat, flat, flat],
                 [flat] * 4, [_sds(w.shape, F32)] * 4)(gs, w, m, v)


MATMUL_WEIGHTS = ("w_in", "w_br_a", "w_br_b", "w_br_c", "w_out", "ffn_w_up", "ffn_w_down")
SPLIT = (
    ("w_in", (DEPTH, D_MODEL, 8720), 2),
    ("gdn_conv_w", (DEPTH, 4, 1536), 2), ("ssd_conv_w", (DEPTH, 4, 1024), 2),
    ("w_br_a", (DEPTH, 512, D_MODEL), 2), ("w_br_b", (DEPTH, 512, D_MODEL), 2), ("w_br_c", (DEPTH, 512, D_MODEL), 2),
    ("w_out", (DEPTH, D_MODEL, D_MODEL), 1), ("ffn_w_up", (DEPTH, D_MODEL, 2 * FFN_HIDDEN), 2),
    ("ffn_conv_w", (DEPTH, 3, 2 * FFN_HIDDEN), 2), ("ffn_w_down", (DEPTH, FFN_HIDDEN, D_MODEL), 1),
)
REPL = (
    ("b_ada", (DEPTH, 6 * D_MODEL)), ("norm1_w", (DEPTH, D_MODEL)), ("gdn_a_log", (DEPTH, 4)),
    ("gdn_dt_bias", (DEPTH, 4)), ("gdn_norm_w", (DEPTH, 128)), ("hgrn_lb_param", (DEPTH, 512)),
    ("hgrn_norm_w", (DEPTH, 128)), ("ssd_conv_b", (DEPTH, 1024)), ("ssd_a_log", (DEPTH, 8)),
    ("ssd_dt_bias", (DEPTH, 8)), ("ssd_d", (DEPTH, 8)), ("ssd_norm_w", (DEPTH, 512)), ("norm2_w", (DEPTH, D_MODEL)),
    ("ffn_conv_b", (DEPTH, 2 * FFN_HIDDEN)), ("final_norm_w", (D_MODEL,)),
)
WEIGHTS = ("w_ada", "b_ada", "norm1_w", "w_in", "gdn_conv_w", "gdn_a_log", "gdn_dt_bias", "gdn_norm_w",
           "hgrn_lb_param", "hgrn_norm_w", "ssd_conv_w", "ssd_conv_b", "ssd_a_log", "ssd_dt_bias", "ssd_d",
           "ssd_norm_w", "w_br_a", "w_br_b", "w_br_c", "w_out", "norm2_w", "ffn_w_up", "ffn_conv_w", "ffn_conv_b",
           "ffn_w_down", "final_norm_w")


def _block_shape(shape, axis):
    return tuple(d // N_DEV if i == axis else d for i, d in enumerate(shape))


def _join_blocks(gathered, shape, axis):
    return jnp.moveaxis(gathered, 0, axis).reshape(shape)


def _split_blocks(full, shape, axis):
    bs = _block_shape(shape, axis)
    t = full.reshape(shape[:axis] + (N_DEV, bs[axis]) + shape[axis + 1:])
    return jnp.moveaxis(t, axis, 0)


def _pack_repl(vals):
    parts = []
    for n, shape in REPL:
        size = math.prod(shape)
        parts.append(jnp.pad(vals[n].reshape(-1), (0, -(-size // PACK_W) * PACK_W - size)))
    cat = jnp.concatenate(parts)
    rows = -(-cat.shape[0] // (8 * PACK_W)) * 8
    return jnp.pad(cat, (0, rows * PACK_W - cat.shape[0])).reshape(rows, PACK_W)


def _unpack_repl(packed):
    flat, out, off = packed.reshape(-1), {}, 0
    for n, shape in REPL:
        size = math.prod(shape)
        out[n] = flat[off:off + size].reshape(shape)
        off += -(-size // PACK_W) * PACK_W
    return out


def _lane_row(vec, lane0):
    return jnp.pad(vec, (lane0, LANES - lane0 - vec.shape[0]))[None]


def _arrange_w_in(w):
    offs = [0]
    for sz in W_IN_SPLITS:
        offs.append(offs[-1] + sz)
    qkv, a, b, gz, hq, hf, hi, hg, sz_, xbc, dt, gate = [w[:, offs[i]:offs[i + 1]] for i in range(12)]
    pad = jnp.zeros((w.shape[0], P_WIDTH - P_SMALL - 16), w.dtype)
    return jnp.concatenate([qkv, gz, xbc, gate, hq, hf, hi, hg, sz_, a, b, dt, pad], axis=1)


def _restore_w_in(wp):
    cut = lambda o, n: wp[:, o:o + n]
    return jnp.concatenate([
        cut(P_QKV, 1536), cut(P_SMALL + SM_A, 4), cut(P_SMALL + SM_B, 4), cut(P_GZ, 512), cut(P_HQ, 512),
        cut(P_HF, 512), cut(P_HI, 512), cut(P_HG, 512), cut(P_SZ, 512), cut(P_XBC, 1024), cut(P_SMALL + SM_DT, 8),
        cut(P_GATE, 3072)], axis=1)


def _layer_consts(l, wf, wr, lower):
    t = lambda a: a.T
    k = {}
    k["n1w"], k["n2w"] = wr["norm1_w"][l][None], wr["norm2_w"][l][None]
    win = _arrange_w_in(wf["w_in"][l])
    k["win"], k["win_t"] = win, t(win)
    for n in ("w_br_a", "w_br_b", "w_br_c", "w_out", "ffn_w_up", "ffn_w_down"):
        k[n], k[n + "_t"] = wf[n][l], t(wf[n][l])
    k["gdn_conv_w"], k["gdn_conv_b"] = wf["gdn_conv_w"][l], jnp.zeros((1, 1536), F32)
    k["ssd_conv_w"], k["ssd_conv_b"] = wf["ssd_conv_w"][l], wr["ssd_conv_b"][l][None]
    k["ffn_conv_w"], k["ffn_conv_b"] = wf["ffn_conv_w"][l], wr["ffn_conv_b"][l][None]
    k["gdn_a"], k["gdn_dt"] = _lane_row(wr["gdn_a_log"][l], SM_A), _lane_row(wr["gdn_dt_bias"][l], SM_A)
    k["gdn_nw"], k["hgrn_nw"] = wr["gdn_norm_w"][l][None], wr["hgrn_norm_w"][l][None]
    k["ssd_a"], k["ssd_dt"] = _lane_row(wr["ssd_a_log"][l], SM_DT), _lane_row(wr["ssd_dt_bias"][l], SM_DT)
    k["ssd_d"] = jnp.repeat(wr["ssd_d"][l], SSD_HEAD_DIM)[None]
    k["ssd_nw"] = wr["ssd_norm_w"][l][None]
    k["lb"] = lower[l:l + 1]
    return k


def _layer_fwd(l, x, mod, k):
    bsz, s, d = x.shape
    t = bsz * s
    sv = {"x": x}
    sv["mod"] = [mod[:, None, i * d:(i + 1) * d] for i in range(6)]
    sh1, sc1, g1, sh2, sc2, g2 = sv["mod"]
    h1 = _norm_mod_fwd(x, k["n1w"], sh1, sc1, f"norm1_fwd{l}")
    p = _mm(h1.reshape(t, d), k["win"], F32, f"mm_in{l}").reshape(bsz, s, P_WIDTH)
    qkv_act = _conv_fwd(p, P_QKV, 1536, k["gdn_conv_w"], k["gdn_conv_b"], True, f"gdn_conv_fwd{l}")
    oa, st_a, ti_a = _gdn_fwd(qkv_act, p, k["gdn_a"], k["gdn_dt"], k["gdn_nw"], f"gdn_fwd{l}")
    ob, st_b = _hgrn_fwd(p, k["lb"], k["hgrn_nw"], f"hgrn_fwd{l}")
    xbc_act = _conv_fwd(p, P_XBC, 1024, k["ssd_conv_w"], k["ssd_conv_b"], True, f"ssd_conv_fwd{l}")
    oc, st_c = _ssd_fwd(xbc_act, p, k["ssd_a"], k["ssd_dt"], k["ssd_d"], k["ssd_nw"], f"ssd_fwd{l}")
    merged = _merge_fwd(p, oa, ob, oc, k["w_br_a"], k["w_br_b"], k["w_br_c"], f"merge_fwd{l}")
    mix = _mm(merged.reshape(t, d), k["w_out"], F32, f"mm_out{l}").reshape(bsz, s, d)
    x1 = _resid_fwd(x, mix, g1, f"resid1_fwd{l}")
    h2 = _norm_mod_fwd(x1, k["n2w"], sh2, sc2, f"norm2_fwd{l}")
    u_pre = _mm(h2.reshape(t, d), k["ffn_w_up"], F32, f"mm_up{l}").reshape(bsz, s, 2 * FFN_HIDDEN)
    a = _conv_glu_fwd(u_pre, k["ffn_conv_w"], k["ffn_conv_b"], f"ffn_conv_glu_fwd{l}")
    ffn = _mm(a.reshape(t, FFN_HIDDEN), k["ffn_w_down"], F32, f"mm_down{l}").reshape(bsz, s, d)
    x2 = _resid_fwd(x1, ffn, g2, f"resid2_fwd{l}")
    sv.update(h1=h1, p=p, qkv_act=qkv_act, oa=oa, st_a=st_a, ti_a=ti_a, ob=ob, st_b=st_b, xbc_act=xbc_act, oc=oc, st_c=st_c,
              merged=merged, mix=mix, x1=x1, h2=h2, u_pre=u_pre, a=a, ffn=ffn)
    return x2, sv


def _layer_bwd(l, dx2, k, sv):
    bsz, s, d = dx2.shape
    t = bsz * s
    f2 = 2 * FFN_HIDDEN
    sh1, sc1, g1, sh2, sc2, g2 = sv["mod"]
    tr = lambda a: a.reshape(t, -1).T
    g = {}
    dffn, dg2 = _gate_bwd(dx2, sv["ffn"], g2, f"gate2_bwd{l}")
    dffn2 = dffn.reshape(t, d)
    da = _mm(dffn2, k["ffn_w_down_t"], F32, f"mm_down_dx{l}").reshape(bsz, s, FFN_HIDDEN)
    g["ffn_w_down"] = _mm(tr(sv["a"]), dffn2, BF16, f"mm_down_dw{l}")
    du_pre, g["ffn_conv_w"], dfcb = _conv_glu_bwd(da, sv["u_pre"], k["ffn_conv_w"], k["ffn_conv_b"], f"ffn_conv_glu_bwd{l}")
    g["ffn_conv_b"] = dfcb[0]
    du2 = du_pre.reshape(t, f2)
    dh2 = _mm(du2, k["ffn_w_up_t"], F32, f"mm_up_dx{l}").reshape(bsz, s, d)
    g["ffn_w_up"] = _mm(tr(sv["h2"]), du2, BF16, f"mm_up_dw{l}")
    dx1, dn2w, dsh2, dsc2 = _norm_mod_bwd(sv["x1"], k["n2w"], sh2, sc2, dh2, dx2, f"norm2_bwd{l}")
    g["norm2_w"] = dn2w[0]
    dmix, dg1 = _gate_bwd(dx1, sv["mix"], g1, f"gate1_bwd{l}")
    dmix2 = dmix.reshape(t, d)
    dmerged = _mm(dmix2, k["w_out_t"], F32, f"mm_out_dx{l}").reshape(bsz, s, d)
    g["w_out"] = _mm(tr(sv["merged"]), dmix2, BF16, f"mm_out_dw{l}")
    p = sv["p"]
    dgate, doa, dob, doc, dya, dyb, dyc = _merge_bwd(
        dmerged, p, sv["oa"], sv["ob"], sv["oc"], k["w_br_a"], k["w_br_b"], k["w_br_c"],
        k["w_br_a_t"], k["w_br_b_t"], k["w_br_c_t"], f"merge_bwd{l}")
    g["w_br_a"] = _mm(tr(sv["oa"]), dya.reshape(t, d), BF16, f"mm_bra_dw{l}")
    g["w_br_b"] = _mm(tr(sv["ob"]), dyb.reshape(t, d), BF16, f"mm_brb_dw{l}")
    g["w_br_c"] = _mm(tr(sv["oc"]), dyc.reshape(t, d), BF16, f"mm_brc_dw{l}")
    dxbc_act, dsm_c, dsz, da_c, ddt_c, dd_c, dnw_c = _ssd_bwd(
        doc, sv["xbc_act"], p, k["ssd_a"], k["ssd_dt"], k["ssd_d"], k["ssd_nw"], sv["st_c"], f"ssd_bwd{l}")
    dxbc_raw, g["ssd_conv_w"], dscb = _conv_bwd(dxbc_act, p, P_XBC, 1024, k["ssd_conv_w"], k["ssd_conv_b"], True, f"ssd_conv_bwd{l}")
    g["ssd_conv_b"] = dscb[0]
    g["ssd_a_log"], g["ssd_dt_bias"] = da_c[0, SM_DT:SM_DT + 8], ddt_c[0, SM_DT:SM_DT + 8]
    g["ssd_d"] = dd_c.reshape(SSD_HEADS, SSD_HEAD_DIM).sum(axis=1)
    g["ssd_norm_w"] = dnw_c[0]
    dhg, dlb, dnw_b = _hgrn_bwd(dob, p, k["lb"], k["hgrn_nw"], sv["st_b"], f"hgrn_bwd{l}")
    g["hgrn_norm_w"] = dnw_b[0]
    dqkv_act, dsm_a, dgz, da_a, ddt_a, dnw_a = _gdn_bwd(
        doa, sv["qkv_act"], p, k["gdn_a"], k["gdn_dt"], k["gdn_nw"], sv["st_a"], sv["ti_a"], f"gdn_bwd{l}")
    dqkv_raw, g["gdn_conv_w"], _ = _conv_bwd(dqkv_act, p, P_QKV, 1536, k["gdn_conv_w"], k["gdn_conv_b"], True, f"gdn_conv_bwd{l}")
    g["gdn_a_log"], g["gdn_dt_bias"], g["gdn_norm_w"] = da_a[0, :4], ddt_a[0, :4], dnw_a[0]
    dsmall = jnp.pad((dsm_a + dsm_c).astype(BF16), ((0, 0), (0, 0), (0, P_WIDTH - P_SMALL - LANES)))
    dp = jnp.concatenate([dqkv_raw, dgz, dxbc_raw, dgate, dhg, dsz, dsmall], axis=-1).reshape(t, P_WIDTH)
    dh1 = _mm(dp, k["win_t"], F32, f"mm_in_dx{l}").reshape(bsz, s, d)
    g["w_in"] = _restore_w_in(_mm(tr(sv["h1"]), dp, BF16, f"mm_in_dw{l}"))
    dx, dn1w, dsh1, dsc1 = _norm_mod_bwd(sv["x"], k["n1w"], sh1, sc1, dh1, dx1, f"norm1_bwd{l}")
    g["norm1_w"] = dn1w[0]
    dmod = jnp.concatenate([dsh1, dsc1, dg1, dsh2, dsc2, dg2], axis=-1)[:, 0]
    return dx, g, dlb, dmod


def _local_step(x, mod, wf, wr, target):
    lower = _lb_fwd(wr["hgrn_lb_param"])
    ks = [_layer_consts(l, wf, wr, lower) for l in range(DEPTH)]
    saved = []
    h = x
    for l in range(DEPTH):
        h, sv = _layer_fwd(l, h, mod[l], ks[l])
        saved.append(sv)
    loss8, dh, dfnw = _final_loss(h, wr["final_norm_w"][None], target)
    per_layer, dlbs, dmods = [None] * DEPTH, [None] * DEPTH, [None] * DEPTH
    for l in reversed(range(DEPTH)):
        dh, per_layer[l], dlbs[l], dmods[l] = _layer_bwd(l, dh, ks[l], saved[l])
    grads = {n: jnp.stack([per_layer[l][n] for l in range(DEPTH)]) for n in per_layer[0]}
    grads["hgrn_lb_param"] = _lb_bwd(wr["hgrn_lb_param"], jnp.concatenate(dlbs, axis=0))
    grads["final_norm_w"] = dfnw[0]
    return loss8[0, 0], dh, grads, jnp.stack(dmods)


def kernel(x, c, w_ada, b_ada, norm1_w, w_in, gdn_conv_w, gdn_a_log, gdn_dt_bias, gdn_norm_w, hgrn_lb_param, hgrn_norm_w, ssd_conv_w, ssd_conv_b, ssd_a_log, ssd_dt_bias, ssd_d, ssd_norm_w, w_br_a, w_br_b, w_br_c, w_out, norm2_w, ffn_w_up, ffn_conv_w, ffn_conv_b, ffn_w_down, final_norm_w, loss_target, m_w_ada, m_b_ada, m_norm1_w, m_w_in, m_gdn_conv_w, m_gdn_a_log, m_gdn_dt_bias, m_gdn_norm_w, m_hgrn_lb_param, m_hgrn_norm_w, m_ssd_conv_w, m_ssd_conv_b, m_ssd_a_log, m_ssd_dt_bias, m_ssd_d, m_ssd_norm_w, m_w_br_a, m_w_br_b, m_w_br_c, m_w_out, m_norm2_w, m_ffn_w_up, m_ffn_conv_w, m_ffn_conv_b, m_ffn_w_down, m_final_norm_w, v_w_ada, v_b_ada, v_norm1_w, v_w_in, v_gdn_conv_w, v_gdn_a_log, v_gdn_dt_bias, v_gdn_norm_w, v_hgrn_lb_param, v_hgrn_norm_w, v_ssd_conv_w, v_ssd_conv_b, v_ssd_a_log, v_ssd_dt_bias, v_ssd_d, v_ssd_norm_w, v_w_br_a, v_w_br_b, v_w_br_c, v_w_out, v_norm2_w, v_ffn_w_up, v_ffn_conv_w, v_ffn_conv_b, v_ffn_w_down, v_final_norm_w):
    given = dict(locals())
    w = {n: given[n] for n in WEIGHTS}
    m = {n: given["m_" + n] for n in WEIGHTS}
    v = {n: given["v_" + n] for n in WEIGHTS}
    me = _my_index()
    bsz = c.shape[0]
    ncol = 6 * D_MODEL // N_DEV

    shards = [w[n].astype(BF16) if n in MATMUL_WEIGHTS else w[n] for n, _, _ in SPLIT] + [c]
    gathered = _gather_two_level(shards, "gather_weights")
    wf = {n: _join_blocks(g, shape, axis) for (n, shape, axis), g in zip(SPLIT, gathered)}
    c_all = gathered[-1].reshape(N_DEV * bsz, D_MODEL)

    b_cols = lax.dynamic_slice_in_dim(b_ada, me * ncol, ncol, axis=1)[:, None]
    mod_cols = _ada_fwd(c_all, w_ada, b_cols)
    send = mod_cols.reshape(DEPTH, N_DEV, bsz, ncol).transpose(1, 0, 2, 3)
    got = _exchange([send], "scatter_mod", False)[0]
    mod = got.transpose(1, 2, 0, 3).reshape(DEPTH, bsz, 6 * D_MODEL)

    loss, dx, grads, dmod = _local_step(x, mod, wf, w, loss_target)

    send = dmod.reshape(DEPTH, bsz, N_DEV, ncol).transpose(2, 0, 1, 3)
    got_dmod = _exchange([send], "scatter_dmod", False)[0]
    dmod_all = got_dmod.transpose(1, 0, 2, 3).reshape(DEPTH, N_DEV * bsz, ncol)
    g_w_ada, g_b_cols = _ada_bwd(c_all.T, dmod_all)

    core = lax.axis_index("c")
    by_core = []
    for n, shape, axis in SPLIT:
        parts = _split_blocks(grads[n], shape, axis).astype(BF16)
        parts = parts.reshape((N_DEV // 2, 2, -1, parts.shape[-1]))
        by_core.append(jnp.swapaxes(parts, 0, 1))
    from_sibling = _sibling_swap(by_core, "swap_grads")
    sums = [_pair_sum(lax.dynamic_index_in_dim(mine, core, 0, keepdims=False), theirs, f"pair_sum_{n}")
            for (n, _, _), mine, theirs in zip(SPLIT, by_core, from_sibling)]
    got = _chip_scatter(sums, "scatter_grads")
    grads["b_ada"] = lax.dynamic_update_slice_in_dim(jnp.zeros_like(b_ada), g_b_cols[:, 0], me * ncol, axis=1)

    out = {}
    slots = [(n, g8) for (n, _, _), g8 in zip(SPLIT, got)] + [("w_ada", g_w_ada[None])]
    for n, gs in slots:
        bs = w[n].shape
        two = lambda a: a.reshape(-1, bs[-1])
        res = _sum_adamw(gs.reshape(gs.shape[0], -1, bs[-1]), two(w[n]), two(m[n]), two(v[n]), f"adamw_{n}")
        out[n] = [r.reshape(bs) for r in res]
    r8 = _gather_two_level([_pack_repl(grads)], "gather_small_grads")[0]
    res = _sum_adamw(r8, _pack_repl(w), _pack_repl(m), _pack_repl(v), "adamw_repl")
    repl_out = [_unpack_repl(o) for o in res]
    pick = lambda i, n: out[n][i] if n in out else repl_out[i][n]
    loss = lax.psum(loss, ("x", "y", "c"))
    return (loss, dx, *[pick(i, n) for i in range(4) for n in WEIGHTS])
```

```python
import functools
import math

import jax
import jax.numpy as jnp
from jax import lax
from jax.experimental import pallas as pl
from jax.experimental.pallas import tpu as pltpu

F32, BF16 = jnp.float32, jnp.bfloat16
HI = lax.Precision.HIGHEST
MESH_ID = pl.DeviceIdType.MESH

N_DEV = 8
EPS = 1e-6
D_MODEL = 1024
DEPTH = 2
GDN_HEADS, GDN_DK, GDN_CHUNK = 4, 128, 64
HGRN_HEADS, HGRN_DK, HGRN_CHUNK, HGRN_BLOCK = 4, 128, 16, 128
SSD_HEADS, SSD_HEAD_DIM, SSD_GROUPS, SSD_STATE, SSD_CHUNK = 8, 64, 2, 128, 64
SSD_INNER = SSD_HEADS * SSD_HEAD_DIM
FFN_HIDDEN = 2816
LANES = 128
P_QKV, P_GZ, P_XBC, P_GATE, P_HQ, P_HF, P_HI, P_HG, P_SZ, P_SMALL, P_WIDTH = (
    0, 1536, 2048, 3072, 6144, 6656, 7168, 7680, 8192, 8704, 9216)
SM_A, SM_B, SM_DT = 0, 4, 8
W_IN_SPLITS = (1536, 4, 4, 512, 512, 512, 512, 512, 512, 1024, 8, 3072)

ADAM_LR, ADAM_B1, ADAM_B2, ADAM_EPS, ADAM_WD, ADAM_STEP = 0.001, 0.9, 0.999, 1e-08, 0.01, 10

V7X_VMEM_LIMIT = 56 * 1024 * 1024
PACK_W = 1024
PACK_ROWS = 128


def _call(body, name, grid, in_specs, out_specs, out_shape, scratch=()):
    return pl.pallas_call(
        body, name=name, grid=grid, in_specs=in_specs, out_specs=out_specs, out_shape=out_shape,
        scratch_shapes=list(scratch),
        compiler_params=pltpu.CompilerParams(
            dimension_semantics=("arbitrary",) * len(grid), vmem_limit_bytes=V7X_VMEM_LIMIT),
    )


def _pick(n, cands):
    for c in cands:
        if n % c == 0:
            return c
    raise ValueError(f"no tile for {n} among {cands}")


def _sds(shape, dtype):
    return jax.ShapeDtypeStruct(shape, dtype)


def _dot(a, b):
    return lax.dot_general(a, b, (((1,), (0,)), ((), ())), precision=HI, preferred_element_type=F32)


NN, NT, TN = (((1,), (0,)), ((), ())), (((1,), (1,)), ((), ())), (((0,), (0,)), ((), ()))


def _mxu(a, b, dims):
    return lax.dot_general(a.astype(BF16), b.astype(BF16), dims, preferred_element_type=F32)


@jax.custom_vjp
def _bdot(a, b):
    return _mxu(a, b, NN)


@jax.custom_vjp
def _bdot_nt(a, b):
    return _mxu(a, b, NT)


@jax.custom_vjp
def _bdot_tn(a, b):
    return _mxu(a, b, TN)


_bdot.defvjp(lambda a, b: (_mxu(a, b, NN), (a, b)), lambda r, d: (_mxu(d, r[1], NT), _mxu(r[0], d, TN)))
_bdot_nt.defvjp(lambda a, b: (_mxu(a, b, NT), (a, b)), lambda r, d: (_mxu(d, r[1], NN), _mxu(d, r[0], TN)))
_bdot_tn.defvjp(lambda a, b: (_mxu(a, b, TN), (a, b)), lambda r, d: (_mxu(r[1], d, NT), _mxu(r[0], d, NN)))


def _split(x):
    hi = x.astype(BF16)
    return hi, (x - hi.astype(F32)).astype(BF16)


def _mxu3(a, b, dims):
    ah, al = _split(a)
    bh, bl = _split(b)
    return _mxu(ah, bh, dims) + (_mxu(ah, bl, dims) + _mxu(al, bh, dims))


@jax.custom_vjp
def _dot3(a, b):
    return _mxu3(a, b, NN)


_dot3.defvjp(lambda a, b: (_mxu3(a, b, NN), (a, b)), lambda r, d: (_mxu3(d, r[1], NT), _mxu3(r[0], d, TN)))


def _pieces(x):
    x1 = x.astype(BF16)
    r1 = x - x1.astype(F32)
    x2 = r1.astype(BF16)
    return x1, x2, (r1 - x2.astype(F32)).astype(BF16)


def _mask_mxu(mask, x, dims):
    x1, x2, x3 = _pieces(x)
    return _mxu(mask, x1, dims) + (_mxu(mask, x2, dims) + _mxu(mask, x3, dims))


def _spread_mxu(x, mask, dims):
    x1, x2, x3 = _pieces(x)
    return _mxu(x1, mask, dims) + (_mxu(x2, mask, dims) + _mxu(x3, mask, dims))


@jax.custom_vjp
def _mask_dot(mask, x):
    return _mask_mxu(mask, x, NN)


@jax.custom_vjp
def _spread_dot(x, mask):
    return _spread_mxu(x, mask, NN)


_mask_dot.defvjp(lambda m, x: (_mask_mxu(m, x, NN), m), lambda m, d: (jnp.zeros_like(m), _mask_mxu(m, d, TN)))
_spread_dot.defvjp(lambda x, m: (_spread_mxu(x, m, NN), m), lambda m, d: (_spread_mxu(d, m, NT), jnp.zeros_like(m)))


def _iota(shape, axis):
    return lax.broadcasted_iota(jnp.int32, shape, axis)


def _silu(x):
    return x * jax.nn.sigmoid(x)


def _softplus(x):
    return jnp.maximum(x, 0.0) + jnp.log1p(jnp.exp(-jnp.abs(x)))


def _rms(x, w):
    return x * lax.rsqrt(jnp.mean(x * x, axis=-1, keepdims=True) + EPS) * w


def _lane_col(x, lane):
    m = (_iota(x.shape, 1) == lane).astype(F32)
    return jnp.sum(x * m, axis=1, keepdims=True)


def _col_to_row(c):
    n = c.shape[0]
    eye = (_iota((n, n), 0) == _iota((n, n), 1)).astype(F32)
    return jnp.sum(c * eye, axis=0, keepdims=True)


def _tril(n, strict=False):
    r, c = _iota((n, n), 0), _iota((n, n), 1)
    return (r > c) if strict else (r >= c)


def _mm(a, b, out_dtype, name):
    m, k = a.shape
    n = b.shape[1]
    tm = _pick(m, (1024, 1408, 512, 256, 128, 64, 32, 16, 8))
    tn = _pick(n, (1024, 768, 512, 384, 256, 128))
    tk = k if k <= 3072 else _pick(k, (1024, 768, 512, 384, 256, 128))
    nk = k // tk

    def body(a_ref, b_ref, o_ref, acc_ref):
        kk = pl.program_id(2)

        @pl.when(kk == 0)
        def _():
            acc_ref[...] = jnp.zeros_like(acc_ref)

        acc_ref[...] += _bdot(a_ref[...], b_ref[...])

        @pl.when(kk == nk - 1)
        def _():
            o_ref[...] = acc_ref[...].astype(out_dtype)

    return _call(
        body, name, (m // tm, n // tn, nk),
        [pl.BlockSpec((tm, tk), lambda i, j, kk: (i, kk)), pl.BlockSpec((tk, tn), lambda i, j, kk: (kk, j))],
        pl.BlockSpec((tm, tn), lambda i, j, kk: (i, j)), _sds((m, n), out_dtype),
        scratch=[pltpu.VMEM((tm, tn), F32)],
    )(a, b)


def _ada_fwd(c_all, w, b):
    depth, _, n = w.shape
    rows = c_all.shape[0]

    def body(c_ref, w_ref, b_ref, o_ref):
        o_ref[...] = _dot(_silu(c_ref[...]), w_ref[...]) + b_ref[...]

    return _call(
        body, "ada_fwd", (depth,),
        [pl.BlockSpec((rows, D_MODEL), lambda l: (0, 0)), pl.BlockSpec((None, D_MODEL, n), lambda l: (l, 0, 0)),
         pl.BlockSpec((None, 1, n), lambda l: (l, 0, 0))],
        pl.BlockSpec((None, rows, n), lambda l: (l, 0, 0)), _sds((depth, rows, n), F32),
    )(c_all, w, b)


def _ada_bwd(c_all_t, dmod):
    depth, rows, n = dmod.shape

    def body(ct_ref, dm_ref, dw_ref, db_ref):
        dm = dm_ref[...]
        dw_ref[...] = _dot(_silu(ct_ref[...]), dm)
        db_ref[...] = jnp.sum(dm, axis=0, keepdims=True)

    return _call(
        body, "ada_bwd", (depth,),
        [pl.BlockSpec((D_MODEL, rows), lambda l: (0, 0)), pl.BlockSpec((None, rows, n), lambda l: (l, 0, 0))],
        [pl.BlockSpec((None, D_MODEL, n), lambda l: (l, 0, 0)), pl.BlockSpec((None, 1, n), lambda l: (l, 0, 0))],
        [_sds((depth, D_MODEL, n), F32), _sds((depth, 1, n), F32)],
    )(c_all_t, dmod)


def _lb_fn(p):
    rows = [p[l:l + 1] for l in range(DEPTH)]
    mx = functools.reduce(jnp.maximum, rows)
    es = [jnp.exp(r - mx) for r in rows]
    tot = functools.reduce(lambda a, b: a + b, es)
    sm = [e / tot for e in es]
    out, run = [], None
    for l in range(DEPTH):
        run = sm[l] if run is None else run + sm[l]
        out.append(run - sm[0])
    return jnp.concatenate(out, axis=0)


def _lb_fwd(p):
    def body(p_ref, o_ref):
        o_ref[...] = _lb_fn(p_ref[...])

    full = pl.BlockSpec(p.shape, lambda i: (0, 0))
    return _call(body, "lb_fwd", (1,), [full], full, _sds(p.shape, F32))(p)


def _lb_bwd(p, d_lower):
    def body(p_ref, d_ref, o_ref):
        _, vjp = jax.vjp(_lb_fn, p_ref[...])
        o_ref[...] = vjp(d_ref[...])[0]

    full = pl.BlockSpec(p.shape, lambda i: (0, 0))
    return _call(body, "lb_bwd", (1,), [full, full], full, _sds(p.shape, F32))(p, d_lower)


def _norm_mod_fn(x, w, shift, scale):
    return _rms(x, w) * (1.0 + scale) + shift


def _norm_mod_fwd(x, w, shift, scale, name):
    bsz, s, d = x.shape
    ts = _pick(s, (256, 128, 64, 32, 16, 8))

    def body(x_ref, w_ref, sh_ref, sc_ref, o_ref):
        o_ref[...] = _norm_mod_fn(x_ref[...], w_ref[...], sh_ref[...], sc_ref[...]).astype(BF16)

    row = pl.BlockSpec((None, ts, d), lambda b, i: (b, i, 0))
    per_b = pl.BlockSpec((None, 1, d), lambda b, i: (b, 0, 0))
    return _call(body, name, (bsz, s // ts), [row, pl.BlockSpec((1, d), lambda b, i: (0, 0)), per_b, per_b],
                 row, _sds(x.shape, BF16))(x, w, shift, scale)


def _norm_mod_bwd(x, w, shift, scale, dh, carry, name):
    bsz, s, d = x.shape
    ts = _pick(s, (256, 128, 64, 32, 16, 8))

    def body(x_ref, w_ref, sh_ref, sc_ref, dh_ref, c_ref, dx_ref, dw_ref, dsh_ref, dsc_ref):
        b, i = pl.program_id(0), pl.program_id(1)
        _, vjp = jax.vjp(_norm_mod_fn, x_ref[...], w_ref[...], sh_ref[...], sc_ref[...])
        dx, dw, dsh, dsc = vjp(dh_ref[...])
        dx_ref[...] = dx + c_ref[...]

        @pl.when((b == 0) & (i == 0))
        def _():
            dw_ref[...] = jnp.zeros_like(dw_ref)

        @pl.when(i == 0)
        def _():
            dsh_ref[...] = jnp.zeros_like(dsh_ref)
            dsc_ref[...] = jnp.zeros_like(dsc_ref)

        dw_ref[...] += dw
        dsh_ref[...] += dsh
        dsc_ref[...] += dsc

    row = pl.BlockSpec((None, ts, d), lambda b, i: (b, i, 0))
    per_b = pl.BlockSpec((None, 1, d), lambda b, i: (b, 0, 0))
    wspec = pl.BlockSpec((1, d), lambda b, i: (0, 0))
    return _call(body, name, (bsz, s // ts), [row, wspec, per_b, per_b, row, row],
                 [row, wspec, per_b, per_b],
                 [_sds(x.shape, F32), _sds((1, d), F32), _sds((bsz, 1, d), F32), _sds((bsz, 1, d), F32)],
                 )(x, w, shift, scale, dh, carry)


def _resid_fwd(x, y, gate, name):
    bsz, s, d = x.shape
    ts = _pick(s, (512, 256, 128, 64, 32, 16, 8))

    def body(x_ref, y_ref, g_ref, o_ref):
        o_ref[...] = x_ref[...] + g_ref[...] * y_ref[...]

    row = pl.BlockSpec((None, ts, d), lambda b, i: (b, i, 0))
    per_b = pl.BlockSpec((None, 1, d), lambda b, i: (b, 0, 0))
    return _call(body, name, (bsz, s // ts), [row, row, per_b], row, _sds(x.shape, F32))(x, y, gate)


def _gate_bwd(dx, y, gate, name):
    bsz, s, d = dx.shape
    ts = _pick(s, (512, 256, 128, 64, 32, 16, 8))

    def body(dx_ref, y_ref, g_ref, dy_ref, dg_ref):
        dxv = dx_ref[...]
        dy_ref[...] = (dxv * g_ref[...]).astype(BF16)

        @pl.when(pl.program_id(1) == 0)
        def _():
            dg_ref[...] = jnp.zeros_like(dg_ref)

        dg_ref[...] += jnp.sum(dxv * y_ref[...], axis=0, keepdims=True)

    row = pl.BlockSpec((None, ts, d), lambda b, i: (b, i, 0))
    per_b = pl.BlockSpec((None, 1, d), lambda b, i: (b, 0, 0))
    return _call(body, name, (bsz, s // ts), [row, row, per_b], [row, per_b],
                 [_sds(dx.shape, BF16), _sds((bsz, 1, d), F32)])(dx, y, gate)


HALO = 8


def _conv_pre(xx, w_ref, b_ref, kw, rows):
    acc = w_ref[kw - 1:kw, :] * xx[HALO:HALO + rows]
    for k in range(kw - 1):
        acc = acc + w_ref[k:k + 1, :] * pltpu.roll(xx, kw - 1 - k, 0)[HALO:HALO + rows]
    return acc + b_ref[...]


def _conv_fwd(x, col0, width, w, b, act, name):
    bsz, s, _ = x.shape
    kw = w.shape[0]
    ts = _pick(s, (512, 256, 128, 64, 32, 16, 8))
    tc = _pick(width, (512, 256, 128))
    assert col0 % tc == 0
    c0 = col0 // tc
    hb = ts // HALO

    def body(x_ref, xp_ref, w_ref, b_ref, o_ref):
        i = pl.program_id(1)
        xp = jnp.where(i > 0, xp_ref[...], 0.0)
        xx = jnp.concatenate([xp, x_ref[...]], axis=0)
        pre = _conv_pre(xx, w_ref, b_ref, kw, ts)
        o_ref[...] = _silu(pre) if act else pre

    return _call(
        body, name, (bsz, s // ts, width // tc),
        [pl.BlockSpec((None, ts, tc), lambda bb, i, j: (bb, i, c0 + j)),
         pl.BlockSpec((None, HALO, tc), lambda bb, i, j: (bb, jnp.maximum(i * hb - 1, 0), c0 + j)),
         pl.BlockSpec((kw, tc), lambda bb, i, j: (0, j)), pl.BlockSpec((1, tc), lambda bb, i, j: (0, j))],
        pl.BlockSpec((None, ts, tc), lambda bb, i, j: (bb, i, j)), _sds((bsz, s, width), F32),
    )(x, x, w, b)


def _conv_bwd(dy, x, col0, width, w, b, act, name):
    bsz, s, _ = x.shape
    kw = w.shape[0]
    ts = _pick(s, (512, 256, 128, 64, 32, 16, 8))
    tc = _pick(width, (512, 256, 128))
    c0 = col0 // tc
    hb = ts // HALO
    nt = s // ts
    last_h = s // HALO - 1

    def body(x_ref, xp_ref, xn_ref, dy_ref, dyn_ref, w_ref, b_ref, dx_ref, dw_ref, db_ref):
        bb, i = pl.program_id(1), pl.program_id(2)
        xp = jnp.where(i > 0, xp_ref[...], 0.0)
        xx = jnp.concatenate([xp, x_ref[...], xn_ref[...]], axis=0)
        dyy = jnp.concatenate([dy_ref[...], jnp.where(i < nt - 1, dyn_ref[...], 0.0)], axis=0)
        n = ts + HALO
        if act:
            pre = _conv_pre(xx, w_ref, b_ref, kw, n)
            sg = jax.nn.sigmoid(pre)
            dpre = dyy * (sg * (1.0 + pre * (1.0 - sg)))
        else:
            dpre = dyy
        dx = w_ref[kw - 1:kw, :] * dpre[:ts]
        for k in range(kw - 1):
            dx = dx + w_ref[k:k + 1, :] * pltpu.roll(dpre, n - (kw - 1 - k), 0)[:ts]
        dx_ref[...] = dx.astype(BF16)

        @pl.when((bb == 0) & (i == 0))
        def _():
            dw_ref[...] = jnp.zeros_like(dw_ref)
            db_ref[...] = jnp.zeros_like(db_ref)

        dt = dpre[:ts]
        db_ref[...] += jnp.sum(dt, axis=0, keepdims=True)
        dw_ref[kw - 1:kw, :] += jnp.sum(dt * xx[HALO:HALO + ts], axis=0, keepdims=True)
        for k in range(kw - 1):
            xs = pltpu.roll(xx, kw - 1 - k, 0)[HALO:HALO + ts]
            dw_ref[k:k + 1, :] += jnp.sum(dt * xs, axis=0, keepdims=True)

    xspec = lambda f: pl.BlockSpec((None, HALO, tc), f)
    return _call(
        body, name, (width // tc, bsz, nt),
        [pl.BlockSpec((None, ts, tc), lambda j, bb, i: (bb, i, c0 + j)),
         xspec(lambda j, bb, i: (bb, jnp.maximum(i * hb - 1, 0), c0 + j)),
         xspec(lambda j, bb, i: (bb, jnp.minimum((i + 1) * hb, last_h), c0 + j)),
         pl.BlockSpec((None, ts, tc), lambda j, bb, i: (bb, i, j)),
         xspec(lambda j, bb, i: (bb, jnp.minimum((i + 1) * hb, last_h), j)),
         pl.BlockSpec((kw, tc), lambda j, bb, i: (0, j)), pl.BlockSpec((1, tc), lambda j, bb, i: (0, j))],
        [pl.BlockSpec((None, ts, tc), lambda j, bb, i: (bb, i, j)),
         pl.BlockSpec((kw, tc), lambda j, bb, i: (0, j)), pl.BlockSpec((1, tc), lambda j, bb, i: (0, j))],
        [_sds((bsz, s, width), BF16), _sds((kw, width), F32), _sds((1, width), F32)],
    )(x, x, x, dy, dy, w, b)


def _conv_glu_fwd(x, w, b, name):
    bsz, s, f2 = x.shape
    f = f2 // 2
    kw = w.shape[0]
    ts = _pick(s, (512, 256, 128, 64, 32, 16, 8))
    tc = _pick(f, (256, 128))
    nf = f // tc
    hb = ts // HALO

    def body(xg_ref, xgp_ref, xv_ref, xvp_ref, wg_ref, wv_ref, bg_ref, bv_ref, o_ref):
        i = pl.program_id(1)
        halves = []
        for x_ref, xp_ref, w_ref, b_ref in ((xg_ref, xgp_ref, wg_ref, bg_ref), (xv_ref, xvp_ref, wv_ref, bv_ref)):
            xx = jnp.concatenate([jnp.where(i > 0, xp_ref[...], 0.0), x_ref[...]], axis=0)
            halves.append(_conv_pre(xx, w_ref, b_ref, kw, ts))
        o_ref[...] = (_silu(halves[0]) * halves[1]).astype(BF16)

    tile = lambda off: pl.BlockSpec((None, ts, tc), lambda bb, i, j: (bb, i, off + j))
    prev = lambda off: pl.BlockSpec((None, HALO, tc), lambda bb, i, j: (bb, jnp.maximum(i * hb - 1, 0), off + j))
    wsp = lambda rows, off: pl.BlockSpec((rows, tc), lambda bb, i, j: (0, off + j))
    return _call(
        body, name, (bsz, s // ts, nf),
        [tile(0), prev(0), tile(nf), prev(nf), wsp(kw, 0), wsp(kw, nf), wsp(1, 0), wsp(1, nf)],
        pl.BlockSpec((None, ts, tc), lambda bb, i, j: (bb, i, j)), _sds((bsz, s, f), BF16),
    )(x, x, x, x, w, w, b, b)


def _conv_glu_bwd(da, x, w, b, name):
    bsz, s, f2 = x.shape
    f = f2 // 2
    kw = w.shape[0]
    ts = _pick(s, (512, 256, 128, 64, 32, 16, 8))
    tc = _pick(f, (256, 128))
    nf = f // tc
    hb = ts // HALO
    nt = s // ts
    last_h = s // HALO - 1
    n = ts + HALO

    def body(xg_ref, xgp_ref, xgn_ref, xv_ref, xvp_ref, xvn_ref, da_ref, dan_ref,
             wg_ref, wv_ref, bg_ref, bv_ref, wx_ref, dx_ref, dw_ref, db_ref):
        j, bb, i = pl.program_id(0), pl.program_id(1), pl.program_id(2)
        day = jnp.concatenate([da_ref[...], jnp.where(i < nt - 1, dan_ref[...], 0.0)], axis=0)
        xg = jnp.concatenate([jnp.where(i > 0, xgp_ref[...], 0.0), xg_ref[...], xgn_ref[...]], axis=0)
        pre_g = _conv_pre(xg, wg_ref, bg_ref, kw, n)
        sg = jax.nn.sigmoid(pre_g)

        @pl.when((bb == 0) & (i == 0))
        def _():
            dw_ref[...] = jnp.zeros_like(dw_ref)
            db_ref[...] = jnp.zeros_like(db_ref)

        def finish(dpre, xx):
            dx = wx_ref[kw - 1:kw, :] * dpre[:ts]
            for k in range(kw - 1):
                dx = dx + wx_ref[k:k + 1, :] * pltpu.roll(dpre, n - (kw - 1 - k), 0)[:ts]
            dx_ref[...] = dx.astype(BF16)
            dt = dpre[:ts]
            db_ref[...] += jnp.sum(dt, axis=0, keepdims=True)
            dw_ref[kw - 1:kw, :] += jnp.sum(dt * xx[HALO:HALO + ts], axis=0, keepdims=True)
            for k in range(kw - 1):
                dw_ref[k:k + 1, :] += jnp.sum(dt * pltpu.roll(xx, kw - 1 - k, 0)[HALO:HALO + ts], axis=0, keepdims=True)

        @pl.when(j < nf)
        def _():
            xv = jnp.concatenate([jnp.where(i > 0, xvp_ref[...], 0.0), xv_ref[...], xvn_ref[...]], axis=0)
            pre_v = _conv_pre(xv, wv_ref, bv_ref, kw, n)
            finish(day * pre_v * (sg * (1.0 + pre_g * (1.0 - sg))), xg)

        @pl.when(j >= nf)
        def _():
            xv = jnp.concatenate([jnp.where(i > 0, xvp_ref[...], 0.0), xv_ref[...], xvn_ref[...]], axis=0)
            finish(day * (pre_g * sg), xv)

    tile = lambda off: pl.BlockSpec((None, ts, tc), lambda j, bb, i: (bb, i, off + j % nf))
    prev = lambda off: pl.BlockSpec((None, HALO, tc), lambda j, bb, i: (bb, jnp.maximum(i * hb - 1, 0), off + j % nf))
    nxt = lambda off: pl.BlockSpec((None, HALO, tc), lambda j, bb, i: (bb, jnp.minimum((i + 1) * hb, last_h), off + j % nf))
    wsp = lambda rows, off: pl.BlockSpec((rows, tc), lambda j, bb, i: (0, off + j % nf))
    own = lambda rows: pl.BlockSpec((rows, tc), lambda j, bb, i: (0, j))
    return _call(
        body, name, (2 * nf, bsz, nt),
        [tile(0), prev(0), nxt(0), tile(nf), prev(nf), nxt(nf), tile(0), nxt(0),
         wsp(kw, 0), wsp(kw, nf), wsp(1, 0), wsp(1, nf), own(kw)],
        [pl.BlockSpec((None, ts, tc), lambda j, bb, i: (bb, i, j)), own(kw), own(1)],
        [_sds((bsz, s, f2), BF16), _sds((kw, f2), F32), _sds((1, f2), F32)],
    )(x, x, x, x, x, x, da, da, w, w, b, b, w)


def _merge_fwd(p, oa, ob, oc, wa, wb, wc, name):
    bsz, s, _ = p.shape
    tm = _pick(s, (256, 128, 64, 32, 16, 8))
    gblk = P_GATE // (3 * D_MODEL)

    def body(g_ref, oa_ref, ob_ref, oc_ref, wa_ref, wb_ref, wc_ref, o_ref):
        acc = None
        for i, (o_r, w_r) in enumerate(((oa_ref, wa_ref), (ob_ref, wb_ref), (oc_ref, wc_ref))):
            y = _bdot(o_r[...], w_r[...])
            t = jax.nn.sigmoid(g_ref[:, i * D_MODEL:(i + 1) * D_MODEL]) * y
            acc = t if acc is None else acc + t
        o_ref[...] = acc.astype(BF16)

    orow = pl.BlockSpec((None, tm, 512), lambda b, i: (b, i, 0))
    wfull = pl.BlockSpec((512, D_MODEL), lambda b, i: (0, 0))
    return _call(
        body, name, (bsz, s // tm),
        [pl.BlockSpec((None, tm, 3 * D_MODEL), lambda b, i: (b, i, gblk)), orow, orow, orow, wfull, wfull, wfull],
        pl.BlockSpec((None, tm, D_MODEL), lambda b, i: (b, i, 0)), _sds((bsz, s, D_MODEL), BF16),
    )(p, oa, ob, oc, wa, wb, wc)


def _merge_bwd(dm, p, oa, ob, oc, wa, wb, wc, wat, wbt, wct, name):
    bsz, s, _ = p.shape
    tm = _pick(s, (256, 128, 64, 32, 16, 8))
    gblk = P_GATE // (3 * D_MODEL)

    def body(dm_ref, g_ref, oa_ref, ob_ref, oc_ref, wa_ref, wb_ref, wc_ref, wat_ref, wbt_ref, wct_ref,
             dg_ref, doa_ref, dob_ref, doc_ref, dya_ref, dyb_ref, dyc_ref):
        dmv = dm_ref[...]
        trip = ((oa_ref, wa_ref, wat_ref, doa_ref, dya_ref), (ob_ref, wb_ref, wbt_ref, dob_ref, dyb_ref),
                (oc_ref, wc_ref, wct_ref, doc_ref, dyc_ref))
        for i, (o_r, w_r, wt_r, do_r, dy_r) in enumerate(trip):
            y = _bdot(o_r[...], w_r[...])
            sg = jax.nn.sigmoid(g_ref[:, i * D_MODEL:(i + 1) * D_MODEL])
            dg_ref[:, i * D_MODEL:(i + 1) * D_MODEL] = (dmv * y * sg * (1.0 - sg)).astype(BF16)
            dy = (dmv * sg).astype(BF16)
            dy_r[...] = dy
            do_r[...] = _bdot(dy, wt_r[...])

    orow = pl.BlockSpec((None, tm, 512), lambda b, i: (b, i, 0))
    drow = pl.BlockSpec((None, tm, D_MODEL), lambda b, i: (b, i, 0))
    grow = pl.BlockSpec((None, tm, 3 * D_MODEL), lambda b, i: (b, i, 0))
    wfull = pl.BlockSpec((512, D_MODEL), lambda b, i: (0, 0))
    wtfull = pl.BlockSpec((D_MODEL, 512), lambda b, i: (0, 0))
    return _call(
        body, name, (bsz, s // tm),
        [drow, pl.BlockSpec((None, tm, 3 * D_MODEL), lambda b, i: (b, i, gblk)), orow, orow, orow,
         wfull, wfull, wfull, wtfull, wtfull, wtfull],
        [grow, orow, orow, orow, drow, drow, drow],
        [_sds((bsz, s, 3 * D_MODEL), BF16)] + [_sds((bsz, s, 512), F32)] * 3 + [_sds((bsz, s, D_MODEL), BF16)] * 3,
    )(dm, p, oa, ob, oc, wa, wb, wc, wat, wbt, wct)


def _final_loss(x, w, target):
    bsz, s, d = x.shape
    ts = _pick(s, (256, 128, 64, 32, 16, 8))

    def body(x_ref, w_ref, t_ref, loss_ref, dx_ref, dw_ref):
        first = (pl.program_id(0) == 0) & (pl.program_id(1) == 0)
        y, vjp = jax.vjp(_rms, x_ref[...], w_ref[...])
        err = y - t_ref[...]
        dx, dw = vjp(err * (1.0 / d))
        dx_ref[...] = dx

        @pl.when(first)
        def _():
            loss_ref[...] = jnp.zeros_like(loss_ref)
            dw_ref[...] = jnp.zeros_like(dw_ref)

        loss_ref[...] += 0.5 * jnp.sum(jnp.sum(err * err, axis=1, keepdims=True), axis=0, keepdims=True) * (1.0 / d)
        dw_ref[...] += dw

    row = pl.BlockSpec((None, ts, d), lambda b, i: (b, i, 0))
    wspec = pl.BlockSpec((1, d), lambda b, i: (0, 0))
    return _call(body, "final_loss", (bsz, s // ts), [row, wspec, row],
                 [pl.BlockSpec((8, LANES), lambda b, i: (0, 0)), row, wspec],
                 [_sds((8, LANES), F32), _sds(x.shape, F32), _sds((1, d), F32)])(x, w, target)


def _unit_lower_inverses(ms):
    n = ms[0].shape[0]
    eye = (_iota((n, n), 0) == _iota((n, n), 1)).astype(F32)
    ps = [-m for m in ms]
    xs = [eye + p for p in ps]
    for _ in range(int(math.log2(n)) - 1):
        ps = [_mxu3(p, p, NN) for p in ps]
        xs = [x + _mxu3(x, p, NN) for x, p in zip(xs, ps)]
    return xs


@jax.custom_vjp
def _known_inverse(m, t):
    return t


_known_inverse.defvjp(lambda m, t: (t, t), lambda t, dt: (-_mxu3(t, _mxu3(dt, t, NT), TN), jnp.zeros_like(t)))


def _gdn_chunk(states, qkv, small, z, a_row, dt_row, nw, tinvs=None):
    c = qkv.shape[0]
    kw = GDN_HEADS * GDN_DK
    g_all = -jnp.exp(a_row) * _softplus(small + dt_row)
    beta_all = jax.nn.sigmoid(small)
    incl, strict = _tril(c), _tril(c, True)
    big_g_all = _mask_dot(incl.astype(BF16), g_all)
    heads = range(GDN_HEADS)
    col = lambda part, h: qkv[:, part * kw + h * GDN_DK:part * kw + (h + 1) * GDN_DK]
    unit = lambda t: t * lax.rsqrt(jnp.sum(t * t, axis=-1, keepdims=True) + EPS)
    q = [unit(col(0, h)) * (GDN_DK ** -0.5) for h in heads]
    k = [unit(col(1, h)) for h in heads]
    v = [col(2, h) for h in heads]
    gc = [_lane_col(big_g_all, SM_A + h) for h in heads]
    bc = [_lane_col(beta_all, SM_B + h) for h in heads]
    g_last = [jnp.sum(_lane_col(g_all, SM_A + h), axis=0, keepdims=True) for h in heads]
    decay = [jnp.where(incl, jnp.exp(jnp.where(incl, gc[h] - _col_to_row(gc[h]), 0.0)), 0.0) for h in heads]
    kb = [k[h] * bc[h] for h in heads]
    m = [jnp.where(strict, _bdot_nt(kb[h], k[h]) * decay[h], 0.0) for h in heads]
    if tinvs is None:
        tinv = _unit_lower_inverses(m)
    else:
        tinv = [_known_inverse(m[h], tinvs[h]) for h in heads]
    eg = [jnp.exp(gc[h]) for h in heads]
    u = [_dot3(tinv[h], v[h] * bc[h]) for h in heads]
    w = [_dot3(tinv[h], kb[h] * eg[h]) for h in heads]
    attn = [_bdot_nt(q[h], k[h]) * decay[h] for h in heads]
    v_new = [u[h] - _bdot(w[h], states[h]) for h in heads]
    o_st = [_bdot(q[h] * eg[h], states[h]) for h in heads]
    o = [o_st[h] + _bdot(attn[h], v_new[h]) for h in heads]
    grow = [_bdot_tn(k[h] * jnp.exp(g_last[h] - gc[h]), v_new[h]) for h in heads]
    new_states = [states[h] * jnp.exp(g_last[h]) + grow[h] for h in heads]
    outs = [_rms(o[h], nw) * _silu(z[:, h * GDN_DK:(h + 1) * GDN_DK]) for h in heads]
    return new_states, jnp.concatenate(outs, axis=1), tinv


def _gdn_specs(c):
    row = lambda w, blk: pl.BlockSpec((None, c, w), lambda b, n, blk=blk: (b, n, blk))
    prm = pl.BlockSpec((1, LANES), lambda b, n: (0, 0))
    return [row(1536, 0), row(LANES, P_SMALL // LANES), row(512, P_GZ // 512), prm, prm, prm]


def _state_spec():
    return pl.BlockSpec((None, None, 4, LANES, LANES), lambda b, n: (b, n, 0, 0, 0))


def _gdn_fwd(qkv_act, p, a_row, dt_row, nw, name):
    bsz, s, _ = qkv_act.shape
    c = GDN_CHUNK
    nc = s // c

    def body(qkv_ref, sm_ref, z_ref, a_ref, dt_ref, nw_ref, o_ref, st_ref, ti_ref, st_scr):
        @pl.when(pl.program_id(1) == 0)
        def _():
            st_scr[...] = jnp.zeros_like(st_scr)

        st_ref[...] = st_scr[...]
        states = [st_scr[h] for h in range(GDN_HEADS)]
        new_states, o, tinvs = _gdn_chunk(states, qkv_ref[...], sm_ref[...], z_ref[...], a_ref[...], dt_ref[...], nw_ref[...])
        for h in range(GDN_HEADS):
            st_scr[h] = new_states[h]
            ti_ref[h] = tinvs[h]
        o_ref[...] = o.astype(BF16)

    return _call(
        body, name, (bsz, nc), _gdn_specs(c),
        [pl.BlockSpec((None, c, 512), lambda b, n: (b, n, 0)), _state_spec(),
         pl.BlockSpec((None, None, 4, c, c), lambda b, n: (b, n, 0, 0, 0))],
        [_sds((bsz, s, 512), BF16), _sds((bsz, nc, 4, LANES, LANES), F32), _sds((bsz, nc, 4, c, c), F32)],
        scratch=[pltpu.VMEM((4, LANES, LANES), F32)],
    )(qkv_act, p, p, a_row, dt_row, nw)


def _gdn_bwd(do, qkv_act, p, a_row, dt_row, nw, st_all, ti_all, name):
    bsz, s, _ = qkv_act.shape
    c = GDN_CHUNK
    nc = s // c

    def body(qkv_ref, sm_ref, z_ref, a_ref, dt_ref, nw_ref, do_ref, st_ref, ti_ref,
             dqkv_ref, dsm_ref, dz_ref, da_ref, ddt_ref, dnw_ref, ds_scr):
        first = (pl.program_id(0) == 0) & (pl.program_id(1) == 0)

        @pl.when(pl.program_id(1) == 0)
        def _():
            ds_scr[...] = jnp.zeros_like(ds_scr)

        @pl.when(first)
        def _():
            da_ref[...] = jnp.zeros_like(da_ref)
            ddt_ref[...] = jnp.zeros_like(ddt_ref)
            dnw_ref[...] = jnp.zeros_like(dnw_ref)

        states = [st_ref[h] for h in range(GDN_HEADS)]
        tinvs = [ti_ref[h] for h in range(GDN_HEADS)]
        chunk = lambda *a: _gdn_chunk(*a, tinvs=tinvs)[:2]
        _, vjp = jax.vjp(chunk, states, qkv_ref[...], sm_ref[...], z_ref[...], a_ref[...], dt_ref[...], nw_ref[...])
        d_states, dqkv, dsm, dz, da, ddt, dnw = vjp(([ds_scr[h] for h in range(GDN_HEADS)], do_ref[...]))
        for h in range(GDN_HEADS):
            ds_scr[h] = d_states[h]
        dqkv_ref[...] = dqkv
        dsm_ref[...] = dsm
        dz_ref[...] = dz.astype(BF16)
        da_ref[...] += da
        ddt_ref[...] += ddt
        dnw_ref[...] += dnw

    rrow = lambda w, blk: pl.BlockSpec((None, c, w), lambda b, n, blk=blk: (b, nc - 1 - n, blk))
    prm = pl.BlockSpec((1, LANES), lambda b, n: (0, 0))
    return _call(
        body, name, (bsz, nc),
        [rrow(1536, 0), rrow(LANES, P_SMALL // LANES), rrow(512, P_GZ // 512), prm, prm, prm, rrow(512, 0),
         pl.BlockSpec((None, None, 4, LANES, LANES), lambda b, n: (b, nc - 1 - n, 0, 0, 0)),
         pl.BlockSpec((None, None, 4, c, c), lambda b, n: (b, nc - 1 - n, 0, 0, 0))],
        [rrow(1536, 0), rrow(LANES, 0), rrow(512, 0), prm, prm, prm],
        [_sds((bsz, s, 1536), F32), _sds((bsz, s, LANES), F32), _sds((bsz, s, 512), BF16)] + [_sds((1, LANES), F32)] * 3,
        scratch=[pltpu.VMEM((4, LANES, LANES), F32)],
    )(qkv_act, p, p, a_row, dt_row, nw, do, st_all, ti_all)


def _hgrn_block(states, q_raw, f_raw, i_raw, g_raw, lb, nw):
    n = q_raw.shape[0]
    c = HGRN_CHUNK
    r, cc = _iota((n, n), 0), _iota((n, n), 1)
    same = (r // c) == (cc // c)
    causal = same & (r >= cc)
    ref_row = (r // c) * c + (c // 2 - 1)
    run_sum = causal.astype(F32)
    rel_sum = run_sum - (same & (ref_row >= cc)).astype(F32)
    sums = jnp.concatenate([run_sum, rel_sum, same.astype(F32)], axis=0).astype(BF16)
    heads, chunks = range(HGRN_HEADS), range(n // c)
    hs = lambda t, h: t[:, h * HGRN_DK:(h + 1) * HGRN_DK]
    q = _silu(q_raw)
    logf = jnp.log(lb + (1.0 - lb) * jax.nn.sigmoid(f_raw))
    k = (1.0 - lb) * jax.nn.sigmoid(-f_raw)
    all_sums = _mask_dot(sums, logf)
    big_g, g_rel, g_tot = all_sums[:n], all_sums[n:2 * n], all_sums[2 * n:]
    q_rel, k_rel = q * jnp.exp(g_rel), k * jnp.exp(-g_rel)
    qg = q * jnp.exp(big_g)
    k_end = k * jnp.exp(g_tot - big_g)
    keep = [jnp.exp(g_tot[j * c:j * c + 1]) for j in chunks]
    scores = [_bdot_nt(hs(q_rel, h), hs(k_rel, h)) for h in heads]
    o_intra = [_bdot(jnp.where(causal, scores[h], 0.0), hs(i_raw, h)) for h in heads]
    grow = [[_bdot_tn(hs(i_raw, h)[j * c:(j + 1) * c], hs(k_end, h)[j * c:(j + 1) * c]) for j in chunks] for h in heads]
    entering, new_states = [], []
    for h in heads:
        st, per_chunk = states[h], []
        for j in chunks:
            per_chunk.append(st)
            st = st * hs(keep[j], h) + grow[h][j]
        entering.append(per_chunk)
        new_states.append(st)
    o_inter = [[_bdot_nt(hs(qg, h)[j * c:(j + 1) * c], entering[h][j]) for j in chunks] for h in heads]
    outs = [_rms(o_intra[h] + jnp.concatenate(o_inter[h], axis=0), nw) * _silu(hs(g_raw, h)) for h in heads]
    return new_states, jnp.concatenate(outs, axis=1)


def _hgrn_fwd(p, lb, nw, name):
    bsz, s, _ = p.shape
    n = HGRN_BLOCK
    nb = s // n

    def body(q_ref, f_ref, i_ref, g_ref, lb_ref, nw_ref, o_ref, st_ref, st_scr):
        @pl.when(pl.program_id(1) == 0)
        def _():
            st_scr[...] = jnp.zeros_like(st_scr)

        st_ref[...] = st_scr[...]
        states = [st_scr[h] for h in range(HGRN_HEADS)]
        new_states, o = _hgrn_block(states, q_ref[...], f_ref[...], i_ref[...], g_ref[...], lb_ref[...], nw_ref[...])
        for h in range(HGRN_HEADS):
            st_scr[h] = new_states[h]
        o_ref[...] = o.astype(BF16)

    row = lambda blk: pl.BlockSpec((None, n, 512), lambda b, i, blk=blk: (b, i, blk))
    return _call(
        body, name, (bsz, nb),
        [row(P_HQ // 512), row(P_HF // 512), row(P_HI // 512), row(P_HG // 512),
         pl.BlockSpec((1, 512), lambda b, i: (0, 0)), pl.BlockSpec((1, LANES), lambda b, i: (0, 0))],
        [row(0), _state_spec()],
        [_sds((bsz, s, 512), BF16), _sds((bsz, nb, 4, LANES, LANES), F32)],
        scratch=[pltpu.VMEM((4, LANES, LANES), F32)],
    )(p, p, p, p, lb, nw)


def _hgrn_bwd(do, p, lb, nw, st_all, name):
    bsz, s, _ = p.shape
    n = HGRN_BLOCK
    nb = s // n

    def body(q_ref, f_ref, i_ref, g_ref, lb_ref, nw_ref, do_ref, st_ref, dp_ref, dlb_ref, dnw_ref, ds_scr):
        first = (pl.program_id(0) == 0) & (pl.program_id(1) == 0)

        @pl.when(pl.program_id(1) == 0)
        def _():
            ds_scr[...] = jnp.zeros_like(ds_scr)

        @pl.when(first)
        def _():
            dlb_ref[...] = jnp.zeros_like(dlb_ref)
            dnw_ref[...] = jnp.zeros_like(dnw_ref)

        states = [st_ref[h] for h in range(HGRN_HEADS)]
        _, vjp = jax.vjp(_hgrn_block, states, q_ref[...], f_ref[...], i_ref[...], g_ref[...], lb_ref[...], nw_ref[...])
        d_states, dq, df, di, dg, dlb, dnw = vjp(([ds_scr[h] for h in range(HGRN_HEADS)], do_ref[...]))
        for h in range(HGRN_HEADS):
            ds_scr[h] = d_states[h]
        for j, t in enumerate((dq, df, di, dg)):
            dp_ref[:, j * 512:(j + 1) * 512] = t.astype(BF16)
        dlb_ref[...] += dlb
        dnw_ref[...] += dnw

    row = lambda blk: pl.BlockSpec((None, n, 512), lambda b, i, blk=blk: (b, nb - 1 - i, blk))
    return _call(
        body, name, (bsz, nb),
        [row(P_HQ // 512), row(P_HF // 512), row(P_HI // 512), row(P_HG // 512),
         pl.BlockSpec((1, 512), lambda b, i: (0, 0)), pl.BlockSpec((1, LANES), lambda b, i: (0, 0)), row(0),
         pl.BlockSpec((None, None, 4, LANES, LANES), lambda b, i: (b, nb - 1 - i, 0, 0, 0))],
        [pl.BlockSpec((None, n, 2048), lambda b, i: (b, nb - 1 - i, 0)),
         pl.BlockSpec((1, 512), lambda b, i: (0, 0)), pl.BlockSpec((1, LANES), lambda b, i: (0, 0))],
        [_sds((bsz, s, 2048), BF16), _sds((1, 512), F32), _sds((1, LANES), F32)],
        scratch=[pltpu.VMEM((4, LANES, LANES), F32)],
    )(p, p, p, p, lb, nw, do, st_all)


def _ssd_chunk(states, xbc, small, z, a_row, dt_row, d_row, nw):
    c = xbc.shape[0]
    incl = _tril(c)
    dt_all = _softplus(small + dt_row)
    da_all = dt_all * (-jnp.exp(a_row))
    spread = (_iota((LANES, SSD_INNER), 0) == SM_DT + _iota((LANES, SSD_INNER), 1) // SSD_HEAD_DIM).astype(BF16)
    both = _spread_dot(jnp.concatenate([dt_all, da_all], axis=0), spread)
    dt_e, da_e = both[:c], both[c:]
    acs_e = _mask_dot(incl.astype(BF16), da_e)
    last_e = jnp.sum(da_e, axis=0, keepdims=True)
    xs = xbc[:, :SSD_INNER]
    xdt = xs * dt_e
    gw = SSD_GROUPS * SSD_STATE
    lane = _iota((1, LANES), 1)
    pairs, groups = range(4), range(SSD_GROUPS)
    ps = lambda t, j: t[:, j * LANES:(j + 1) * LANES]
    bg = [xbc[:, SSD_INNER + g * SSD_STATE:SSD_INNER + (g + 1) * SSD_STATE] for g in groups]
    cg = [xbc[:, SSD_INNER + gw + g * SSD_STATE:SSD_INNER + gw + (g + 1) * SSD_STATE] for g in groups]
    cb = [_bdot_nt(cg[g], bg[g]) for g in groups]

    def seg(j, sub):
        ac = ps(acs_e, j)[:, sub * SSD_HEAD_DIM:sub * SSD_HEAD_DIM + 1]
        return jnp.where(incl, jnp.exp(jnp.where(incl, ac - _col_to_row(ac), 0.0)), 0.0)

    mine = [((lane // SSD_HEAD_DIM) == sub).astype(F32) for sub in range(2)]
    y_in = [[_bdot(cb[j // 2] * seg(j, sub), ps(xdt, j) * mine[sub]) for sub in range(2)] for j in pairs]
    y_st = [_bdot(cg[j // 2], states[j]) for j in pairs]
    grow = [_bdot_tn(bg[j // 2], ps(xdt, j) * jnp.exp(ps(last_e, j) - ps(acs_e, j))) for j in pairs]
    new_states = [states[j] * jnp.exp(ps(last_e, j)) + grow[j] for j in pairs]
    ys = [y_in[j][0] + y_in[j][1] + y_st[j] * jnp.exp(ps(acs_e, j)) + ps(d_row, j) * ps(xs, j) for j in pairs]
    yz = jnp.concatenate(ys, axis=1) * _silu(z)
    gwid = SSD_INNER // SSD_GROUPS
    outs = [_rms(yz[:, g * gwid:(g + 1) * gwid], nw[:, g * gwid:(g + 1) * gwid]) for g in range(SSD_GROUPS)]
    return new_states, jnp.concatenate(outs, axis=1)


def _ssd_fwd(xbc_act, p, a_row, dt_row, d_row, nw, name):
    bsz, s, _ = xbc_act.shape
    c = SSD_CHUNK
    nc = s // c

    def body(x_ref, sm_ref, z_ref, a_ref, dt_ref, d_ref, nw_ref, o_ref, st_ref, st_scr):
        @pl.when(pl.program_id(1) == 0)
        def _():
            st_scr[...] = jnp.zeros_like(st_scr)

        st_ref[...] = st_scr[...]
        states = [st_scr[h] for h in range(4)]
        new_states, o = _ssd_chunk(states, x_ref[...], sm_ref[...], z_ref[...], a_ref[...], dt_ref[...], d_ref[...], nw_ref[...])
        for h in range(4):
            st_scr[h] = new_states[h]
        o_ref[...] = o.astype(BF16)

    row = lambda w, blk: pl.BlockSpec((None, c, w), lambda b, n, blk=blk: (b, n, blk))
    prm = pl.BlockSpec((1, LANES), lambda b, n: (0, 0))
    prm5 = pl.BlockSpec((1, 512), lambda b, n: (0, 0))
    return _call(
        body, name, (bsz, nc),
        [row(1024, 0), row(LANES, P_SMALL // LANES), row(512, P_SZ // 512), prm, prm, prm5, prm5],
        [row(512, 0), _state_spec()],
        [_sds((bsz, s, 512), BF16), _sds((bsz, nc, 4, LANES, LANES), F32)],
        scratch=[pltpu.VMEM((4, LANES, LANES), F32)],
    )(xbc_act, p, p, a_row, dt_row, d_row, nw)


def _ssd_bwd(do, xbc_act, p, a_row, dt_row, d_row, nw, st_all, name):
    bsz, s, _ = xbc_act.shape
    c = SSD_CHUNK
    nc = s // c

    def body(x_ref, sm_ref, z_ref, a_ref, dt_ref, d_ref, nw_ref, do_ref, st_ref,
             dx_ref, dsm_ref, dz_ref, da_ref, ddt_ref, dd_ref, dnw_ref, ds_scr):
        first = (pl.program_id(0) == 0) & (pl.program_id(1) == 0)

        @pl.when(pl.program_id(1) == 0)
        def _():
            ds_scr[...] = jnp.zeros_like(ds_scr)

        @pl.when(first)
        def _():
            da_ref[...] = jnp.zeros_like(da_ref)
            ddt_ref[...] = jnp.zeros_like(ddt_ref)
            dd_ref[...] = jnp.zeros_like(dd_ref)
            dnw_ref[...] = jnp.zeros_like(dnw_ref)

        states = [st_ref[h] for h in range(4)]
        _, vjp = jax.vjp(_ssd_chunk, states, x_ref[...], sm_ref[...], z_ref[...], a_ref[...], dt_ref[...], d_ref[...], nw_ref[...])
        d_states, dx, dsm, dz, da, ddt, dd, dnw = vjp(([ds_scr[h] for h in range(4)], do_ref[...]))
        for h in range(4):
            ds_scr[h] = d_states[h]
        dx_ref[...] = dx
        dsm_ref[...] = dsm
        dz_ref[...] = dz.astype(BF16)
        da_ref[...] += da
        ddt_ref[...] += ddt
        dd_ref[...] += dd
        dnw_ref[...] += dnw

    row = lambda w, blk: pl.BlockSpec((None, c, w), lambda b, n, blk=blk: (b, nc - 1 - n, blk))
    prm = pl.BlockSpec((1, LANES), lambda b, n: (0, 0))
    prm5 = pl.BlockSpec((1, 512), lambda b, n: (0, 0))
    return _call(
        body, name, (bsz, nc),
        [row(1024, 0), row(LANES, P_SMALL // LANES), row(512, P_SZ // 512), prm, prm, prm5, prm5, row(512, 0),
         pl.BlockSpec((None, None, 4, LANES, LANES), lambda b, n: (b, nc - 1 - n, 0, 0, 0))],
        [row(1024, 0), row(LANES, 0), row(512, 0), prm, prm, prm5, prm5],
        [_sds((bsz, s, 1024), F32), _sds((bsz, s, LANES), F32), _sds((bsz, s, 512), BF16),
         _sds((1, LANES), F32), _sds((1, LANES), F32), _sds((1, 512), F32), _sds((1, 512), F32)],
        scratch=[pltpu.VMEM((4, LANES, LANES), F32)],
    )(xbc_act, p, p, a_row, dt_row, d_row, nw, do, st_all)


def _peer(k):
    x, y, c = lax.axis_index("x"), lax.axis_index("y"), lax.axis_index("c")
    px = 1 - x if k & 4 else x
    py = 1 - y if k & 2 else y
    pc = 1 - c if k & 1 else c
    return (px, py, pc), 4 * px + 2 * py + pc


def _my_index():
    return 4 * lax.axis_index("x") + 2 * lax.axis_index("y") + lax.axis_index("c")


def _exchange(arrays, name, gather):
    n = len(arrays)

    def body(*refs):
        ins, outs = refs[:n], refs[n:2 * n]
        send_sems, recv_sems, local_sems = refs[2 * n:]
        me = _my_index()

        def copy(i, k):
            peer, slot = _peer(k)
            src = ins[i] if gather else ins[i].at[slot]
            return pltpu.make_async_remote_copy(src_ref=src, dst_ref=outs[i].at[me], send_sem=send_sems.at[k - 1, i],
                                                recv_sem=recv_sems.at[k - 1, i], device_id=peer, device_id_type=MESH_ID)

        def arrival(i, k):
            peer, slot = _peer(k)
            src = ins[i] if gather else ins[i].at[slot]
            return pltpu.make_async_remote_copy(src_ref=src, dst_ref=outs[i].at[slot], send_sem=send_sems.at[k - 1, i],
                                                recv_sem=recv_sems.at[k - 1, i], device_id=peer, device_id_type=MESH_ID)

        mine = [pltpu.make_async_copy(ins[i] if gather else ins[i].at[me], outs[i].at[me], local_sems.at[i])
                for i in range(n)]
        sends = [copy(i, k) for k in range(1, N_DEV) for i in range(n)]
        for cp in mine + sends:
            cp.start()
        for k in range(1, N_DEV):
            for i in range(n):
                arrival(i, k).wait_recv()
        for cp in sends:
            cp.wait_send()
        for cp in mine:
            cp.wait()

    out_shape = [_sds(((N_DEV,) + a.shape) if gather else a.shape, a.dtype) for a in arrays]
    any_spec = pl.BlockSpec(memory_space=pl.ANY)
    return pl.pallas_call(
        body, name=name, out_shape=out_shape, in_specs=[any_spec] * n, out_specs=[any_spec] * n,
        scratch_shapes=[pltpu.SemaphoreType.DMA((N_DEV - 1, n)), pltpu.SemaphoreType.DMA((N_DEV - 1, n)),
                        pltpu.SemaphoreType.DMA((n,))],
    )(*arrays)


def _mesh_place():
    x, y, c = lax.axis_index("x"), lax.axis_index("y"), lax.axis_index("c")
    return (x, y, c), (x, y, 1 - c), [(1 - x, y), (x, 1 - y), (1 - x, 1 - y)]


def _run_exchange(body, name, arrays, out_shape, n_sems):
    n = len(arrays)
    any_spec = pl.BlockSpec(memory_space=pl.ANY)
    return pl.pallas_call(
        body, name=name, out_shape=out_shape, in_specs=[any_spec] * n, out_specs=[any_spec] * n,
        scratch_shapes=[pltpu.SemaphoreType.DMA((n_sems, n)), pltpu.SemaphoreType.DMA((n_sems, n)),
                        pltpu.SemaphoreType.DMA((n,))],
    )(*arrays)


def _gather_two_level(arrays, name):
    n = len(arrays)

    def body(*refs):
        ins, outs = refs[:n], refs[n:2 * n]
        send_sems, recv_sems, local_sems = refs[2 * n:]
        (x, y, c), sibling, chips = _mesh_place()
        slot = lambda px, py, pc: 4 * px + 2 * py + pc

        def copy(i, k, block, to, src=None):
            return pltpu.make_async_remote_copy(
                src_ref=outs[i].at[block] if src is None else src, dst_ref=outs[i].at[block],
                send_sem=send_sems.at[k, i], recv_sem=recv_sems.at[k, i], device_id=to, device_id_type=MESH_ID)

        me = slot(x, y, c)
        mine = [pltpu.make_async_copy(ins[i], outs[i].at[me], local_sems.at[i]) for i in range(n)]
        first = [copy(i, 0, me, sibling, src=ins[i]) for i in range(n)]
        first += [copy(i, 1 + j, me, (*chip, c), src=ins[i]) for j, chip in enumerate(chips) for i in range(n)]
        for cp in mine + first:
            cp.start()
        passed = []
        for j, chip in enumerate(chips):
            for i in range(n):
                copy(i, 1 + j, slot(*chip, c), (x, y, c)).wait_recv()
                cp = copy(i, 4 + j, slot(*chip, c), sibling)
                cp.start()
                passed.append(cp)
        for i in range(n):
            copy(i, 0, slot(x, y, 1 - c), (x, y, c)).wait_recv()
        for j, chip in enumerate(chips):
            for i in range(n):
                copy(i, 4 + j, slot(*chip, 1 - c), (x, y, c)).wait_recv()
        for cp in first + passed:
            cp.wait_send()
        for cp in mine:
            cp.wait()

    out_shape = [_sds((N_DEV,) + a.shape, a.dtype) for a in arrays]
    return _run_exchange(body, name, arrays, out_shape, 7)


def _sibling_swap(arrays, name):
    n = len(arrays)

    def body(*refs):
        ins, outs = refs[:n], refs[n:2 * n]
        send_sems, recv_sems, _ = refs[2 * n:]
        (x, y, c), sibling, _ = _mesh_place()
        copies = [pltpu.make_async_remote_copy(
            src_ref=ins[i].at[1 - c], dst_ref=outs[i], send_sem=send_sems.at[0, i], recv_sem=recv_sems.at[0, i],
            device_id=sibling, device_id_type=MESH_ID) for i in range(n)]
        for cp in copies:
            cp.start()
        for cp in copies:
            cp.wait()

    out_shape = [_sds(a.shape[1:], a.dtype) for a in arrays]
    return _run_exchange(body, name, arrays, out_shape, 1)


def _chip_scatter(arrays, name):
    n = len(arrays)

    def body(*refs):
        ins, outs = refs[:n], refs[n:2 * n]
        send_sems, recv_sems, local_sems = refs[2 * n:]
        (x, y, c), _, chips = _mesh_place()
        me = 2 * x + y
        mine = [pltpu.make_async_copy(ins[i].at[me], outs[i].at[me], local_sems.at[i]) for i in range(n)]
        sends = [pltpu.make_async_remote_copy(
            src_ref=ins[i].at[2 * chip[0] + chip[1]], dst_ref=outs[i].at[me], send_sem=send_sems.at[j, i],
            recv_sem=recv_sems.at[j, i], device_id=(*chip, c), device_id_type=MESH_ID)
            for j, chip in enumerate(chips) for i in range(n)]
        for cp in mine + sends:
            cp.start()
        for j, chip in enumerate(chips):
            for i in range(n):
                pltpu.make_async_remote_copy(
                    src_ref=ins[i].at[me], dst_ref=outs[i].at[2 * chip[0] + chip[1]], send_sem=send_sems.at[j, i],
                    recv_sem=recv_sems.at[j, i], device_id=(*chip, c), device_id_type=MESH_ID).wait_recv()
        for cp in sends:
            cp.wait_send()
        for cp in mine:
            cp.wait()

    out_shape = [_sds(a.shape, a.dtype) for a in arrays]
    return _run_exchange(body, name, arrays, out_shape, 3)


def _pair_sum(a, b, name):
    lead, rows, width = a.shape
    tr = _pick(rows, (256, 128, 64, 32, 16, 8)) if rows % 8 == 0 else rows

    def body(a_ref, b_ref, o_ref):
        o_ref[...] = (a_ref[...].astype(F32) + b_ref[...].astype(F32)).astype(o_ref.dtype)

    blk = pl.BlockSpec((None, tr, width), lambda l, i: (l, i, 0))
    return _call(body, name, (lead, rows // tr), [blk, blk], blk, _sds(a.shape, a.dtype))(a, b)


def _sum_adamw(gs, w, m, v, name):
    rows, width = w.shape
    slots = gs.shape[0]
    tr = _pick(rows, (128, 64, 32, 16, 8)) if rows % 8 == 0 else rows

    def body(g_ref, w_ref, m_ref, v_ref, go_ref, d_ref, mo_ref, vo_ref):
        g = g_ref[0].astype(F32)
        for i in range(1, slots):
            g = g + g_ref[i].astype(F32)
        m2 = ADAM_B1 * m_ref[...] + (1.0 - ADAM_B1) * g
        v2 = ADAM_B2 * v_ref[...] + (1.0 - ADAM_B2) * (g * g)
        m_hat = m2 / (1.0 - ADAM_B1 ** ADAM_STEP)
        v_hat = v2 / (1.0 - ADAM_B2 ** ADAM_STEP)
        go_ref[...] = g
        d_ref[...] = -ADAM_LR * (m_hat / (jnp.sqrt(v_hat) + ADAM_EPS) + ADAM_WD * w_ref[...])
        mo_ref[...] = m2
        vo_ref[...] = v2

    flat = pl.BlockSpec((tr, width), lambda i: (i, 0))
    return _call(body, name, (rows // tr,), [pl.BlockSpec((slots, tr, width), lambda i: (0, i, 0)), flat, flat, flat],
                 [flat] * 4, [_sds(w.shape, F32)] * 4)(gs, w, m, v)


MATMUL_WEIGHTS = ("w_in", "w_br_a", "w_br_b", "w_br_c", "w_out", "ffn_w_up", "ffn_w_down")
SPLIT = (
    ("w_in", (DEPTH, D_MODEL, 8720), 2),
    ("gdn_conv_w", (DEPTH, 4, 1536), 2), ("ssd_conv_w", (DEPTH, 4, 1024), 2),
    ("w_br_a", (DEPTH, 512, D_MODEL), 2), ("w_br_b", (DEPTH, 512, D_MODEL), 2), ("w_br_c", (DEPTH, 512, D_MODEL), 2),
    ("w_out", (DEPTH, D_MODEL, D_MODEL), 1), ("ffn_w_up", (DEPTH, D_MODEL, 2 * FFN_HIDDEN), 2),
    ("ffn_conv_w", (DEPTH, 3, 2 * FFN_HIDDEN), 2), ("ffn_w_down", (DEPTH, FFN_HIDDEN, D_MODEL), 1),
)
REPL = (
    ("b_ada", (DEPTH, 6 * D_MODEL)), ("norm1_w", (DEPTH, D_MODEL)), ("gdn_a_log", (DEPTH, 4)),
    ("gdn_dt_bias", (DEPTH, 4)), ("gdn_norm_w", (DEPTH, 128)), ("hgrn_lb_param", (DEPTH, 512)),
    ("hgrn_norm_w", (DEPTH, 128)), ("ssd_conv_b", (DEPTH, 1024)), ("ssd_a_log", (DEPTH, 8)),
    ("ssd_dt_bias", (DEPTH, 8)), ("ssd_d", (DEPTH, 8)), ("ssd_norm_w", (DEPTH, 512)), ("norm2_w", (DEPTH, D_MODEL)),
    ("ffn_conv_b", (DEPTH, 2 * FFN_HIDDEN)), ("final_norm_w", (D_MODEL,)),
)
WEIGHTS = ("w_ada", "b_ada", "norm1_w", "w_in", "gdn_conv_w", "gdn_a_log", "gdn_dt_bias", "gdn_norm_w",
           "hgrn_lb_param", "hgrn_norm_w", "ssd_conv_w", "ssd_conv_b", "ssd_a_log", "ssd_dt_bias", "ssd_d",
           "ssd_norm_w", "w_br_a", "w_br_b", "w_br_c", "w_out", "norm2_w", "ffn_w_up", "ffn_conv_w", "ffn_conv_b",
           "ffn_w_down", "final_norm_w")


def _block_shape(shape, axis):
    return tuple(d // N_DEV if i == axis else d for i, d in enumerate(shape))


def _join_blocks(gathered, shape, axis):
    return jnp.moveaxis(gathered, 0, axis).reshape(shape)


def _split_blocks(full, shape, axis):
    bs = _block_shape(shape, axis)
    t = full.reshape(shape[:axis] + (N_DEV, bs[axis]) + shape[axis + 1:])
    return jnp.moveaxis(t, axis, 0)


def _pack_repl(vals):
    parts = []
    for n, shape in REPL:
        size = math.prod(shape)
        parts.append(jnp.pad(vals[n].reshape(-1), (0, -(-size // PACK_W) * PACK_W - size)))
    cat = jnp.concatenate(parts)
    rows = -(-cat.shape[0] // (8 * PACK_W)) * 8
    return jnp.pad(cat, (0, rows * PACK_W - cat.shape[0])).reshape(rows, PACK_W)


def _unpack_repl(packed):
    flat, out, off = packed.reshape(-1), {}, 0
    for n, shape in REPL:
        size = math.prod(shape)
        out[n] = flat[off:off + size].reshape(shape)
        off += -(-size // PACK_W) * PACK_W
    return out


def _lane_row(vec, lane0):
    return jnp.pad(vec, (lane0, LANES - lane0 - vec.shape[0]))[None]


def _arrange_w_in(w):
    offs = [0]
    for sz in W_IN_SPLITS:
        offs.append(offs[-1] + sz)
    qkv, a, b, gz, hq, hf, hi, hg, sz_, xbc, dt, gate = [w[:, offs[i]:offs[i + 1]] for i in range(12)]
    pad = jnp.zeros((w.shape[0], P_WIDTH - P_SMALL - 16), w.dtype)
    return jnp.concatenate([qkv, gz, xbc, gate, hq, hf, hi, hg, sz_, a, b, dt, pad], axis=1)


def _restore_w_in(wp):
    cut = lambda o, n: wp[:, o:o + n]
    return jnp.concatenate([
        cut(P_QKV, 1536), cut(P_SMALL + SM_A, 4), cut(P_SMALL + SM_B, 4), cut(P_GZ, 512), cut(P_HQ, 512),
        cut(P_HF, 512), cut(P_HI, 512), cut(P_HG, 512), cut(P_SZ, 512), cut(P_XBC, 1024), cut(P_SMALL + SM_DT, 8),
        cut(P_GATE, 3072)], axis=1)


def _layer_consts(l, wf, wr, lower):
    t = lambda a: a.T
    k = {}
    k["n1w"], k["n2w"] = wr["norm1_w"][l][None], wr["norm2_w"][l][None]
    win = _arrange_w_in(wf["w_in"][l])
    k["win"], k["win_t"] = win, t(win)
    for n in ("w_br_a", "w_br_b", "w_br_c", "w_out", "ffn_w_up", "ffn_w_down"):
        k[n], k[n + "_t"] = wf[n][l], t(wf[n][l])
    k["gdn_conv_w"], k["gdn_conv_b"] = wf["gdn_conv_w"][l], jnp.zeros((1, 1536), F32)
    k["ssd_conv_w"], k["ssd_conv_b"] = wf["ssd_conv_w"][l], wr["ssd_conv_b"][l][None]
    k["ffn_conv_w"], k["ffn_conv_b"] = wf["ffn_conv_w"][l], wr["ffn_conv_b"][l][None]
    k["gdn_a"], k["gdn_dt"] = _lane_row(wr["gdn_a_log"][l], SM_A), _lane_row(wr["gdn_dt_bias"][l], SM_A)
    k["gdn_nw"], k["hgrn_nw"] = wr["gdn_norm_w"][l][None], wr["hgrn_norm_w"][l][None]
    k["ssd_a"], k["ssd_dt"] = _lane_row(wr["ssd_a_log"][l], SM_DT), _lane_row(wr["ssd_dt_bias"][l], SM_DT)
    k["ssd_d"] = jnp.repeat(wr["ssd_d"][l], SSD_HEAD_DIM)[None]
    k["ssd_nw"] = wr["ssd_norm_w"][l][None]
    k["lb"] = lower[l:l + 1]
    return k


def _layer_fwd(l, x, mod, k):
    bsz, s, d = x.shape
    t = bsz * s
    sv = {"x": x}
    sv["mod"] = [mod[:, None, i * d:(i + 1) * d] for i in range(6)]
    sh1, sc1, g1, sh2, sc2, g2 = sv["mod"]
    h1 = _norm_mod_fwd(x, k["n1w"], sh1, sc1, f"norm1_fwd{l}")
    p = _mm(h1.reshape(t, d), k["win"], F32, f"mm_in{l}").reshape(bsz, s, P_WIDTH)
    qkv_act = _conv_fwd(p, P_QKV, 1536, k["gdn_conv_w"], k["gdn_conv_b"], True, f"gdn_conv_fwd{l}")
    oa, st_a, ti_a = _gdn_fwd(qkv_act, p, k["gdn_a"], k["gdn_dt"], k["gdn_nw"], f"gdn_fwd{l}")
    ob, st_b = _hgrn_fwd(p, k["lb"], k["hgrn_nw"], f"hgrn_fwd{l}")
    xbc_act = _conv_fwd(p, P_XBC, 1024, k["ssd_conv_w"], k["ssd_conv_b"], True, f"ssd_conv_fwd{l}")
    oc, st_c = _ssd_fwd(xbc_act, p, k["ssd_a"], k["ssd_dt"], k["ssd_d"], k["ssd_nw"], f"ssd_fwd{l}")
    merged = _merge_fwd(p, oa, ob, oc, k["w_br_a"], k["w_br_b"], k["w_br_c"], f"merge_fwd{l}")
    mix = _mm(merged.reshape(t, d), k["w_out"], F32, f"mm_out{l}").reshape(bsz, s, d)
    x1 = _resid_fwd(x, mix, g1, f"resid1_fwd{l}")
    h2 = _norm_mod_fwd(x1, k["n2w"], sh2, sc2, f"norm2_fwd{l}")
    u_pre = _mm(h2.reshape(t, d), k["ffn_w_up"], F32, f"mm_up{l}").reshape(bsz, s, 2 * FFN_HIDDEN)
    a = _conv_glu_fwd(u_pre, k["ffn_conv_w"], k["ffn_conv_b"], f"ffn_conv_glu_fwd{l}")
    ffn = _mm(a.reshape(t, FFN_HIDDEN), k["ffn_w_down"], F32, f"mm_down{l}").reshape(bsz, s, d)
    x2 = _resid_fwd(x1, ffn, g2, f"resid2_fwd{l}")
    sv.update(h1=h1, p=p, qkv_act=qkv_act, oa=oa, st_a=st_a, ti_a=ti_a, ob=ob, st_b=st_b, xbc_act=xbc_act, oc=oc, st_c=st_c,
              merged=merged, mix=mix, x1=x1, h2=h2, u_pre=u_pre, a=a, ffn=ffn)
    return x2, sv


def _layer_bwd(l, dx2, k, sv):
    bsz, s, d = dx2.shape
    t = bsz * s
    f2 = 2 * FFN_HIDDEN
    sh1, sc1, g1, sh2, sc2, g2 = sv["mod"]
    tr = lambda a: a.reshape(t, -1).T
    g = {}
    dffn, dg2 = _gate_bwd(dx2, sv["ffn"], g2, f"gate2_bwd{l}")
    dffn2 = dffn.reshape(t, d)
    da = _mm(dffn2, k["ffn_w_down_t"], F32, f"mm_down_dx{l}").reshape(bsz, s, FFN_HIDDEN)
    g["ffn_w_down"] = _mm(tr(sv["a"]), dffn2, BF16, f"mm_down_dw{l}")
    du_pre, g["ffn_conv_w"], dfcb = _conv_glu_bwd(da, sv["u_pre"], k["ffn_conv_w"], k["ffn_conv_b"], f"ffn_conv_glu_bwd{l}")
    g["ffn_conv_b"] = dfcb[0]
    du2 = du_pre.reshape(t, f2)
    dh2 = _mm(du2, k["ffn_w_up_t"], F32, f"mm_up_dx{l}").reshape(bsz, s, d)
    g["ffn_w_up"] = _mm(tr(sv["h2"]), du2, BF16, f"mm_up_dw{l}")
    dx1, dn2w, dsh2, dsc2 = _norm_mod_bwd(sv["x1"], k["n2w"], sh2, sc2, dh2, dx2, f"norm2_bwd{l}")
    g["norm2_w"] = dn2w[0]
    dmix, dg1 = _gate_bwd(dx1, sv["mix"], g1, f"gate1_bwd{l}")
    dmix2 = dmix.reshape(t, d)
    dmerged = _mm(dmix2, k["w_out_t"], F32, f"mm_out_dx{l}").reshape(bsz, s, d)
    g["w_out"] = _mm(tr(sv["merged"]), dmix2, BF16, f"mm_out_dw{l}")
    p = sv["p"]
    dgate, doa, dob, doc, dya, dyb, dyc = _merge_bwd(
        dmerged, p, sv["oa"], sv["ob"], sv["oc"], k["w_br_a"], k["w_br_b"], k["w_br_c"],
        k["w_br_a_t"], k["w_br_b_t"], k["w_br_c_t"], f"merge_bwd{l}")
    g["w_br_a"] = _mm(tr(sv["oa"]), dya.reshape(t, d), BF16, f"mm_bra_dw{l}")
    g["w_br_b"] = _mm(tr(sv["ob"]), dyb.reshape(t, d), BF16, f"mm_brb_dw{l}")
    g["w_br_c"] = _mm(tr(sv["oc"]), dyc.reshape(t, d), BF16, f"mm_brc_dw{l}")
    dxbc_act, dsm_c, dsz, da_c, ddt_c, dd_c, dnw_c = _ssd_bwd(
        doc, sv["xbc_act"], p, k["ssd_a"], k["ssd_dt"], k["ssd_d"], k["ssd_nw"], sv["st_c"], f"ssd_bwd{l}")
    dxbc_raw, g["ssd_conv_w"], dscb = _conv_bwd(dxbc_act, p, P_XBC, 1024, k["ssd_conv_w"], k["ssd_conv_b"], True, f"ssd_conv_bwd{l}")
    g["ssd_conv_b"] = dscb[0]
    g["ssd_a_log"], g["ssd_dt_bias"] = da_c[0, SM_DT:SM_DT + 8], ddt_c[0, SM_DT:SM_DT + 8]
    g["ssd_d"] = dd_c.reshape(SSD_HEADS, SSD_HEAD_DIM).sum(axis=1)
    g["ssd_norm_w"] = dnw_c[0]
    dhg, dlb, dnw_b = _hgrn_bwd(dob, p, k["lb"], k["hgrn_nw"], sv["st_b"], f"hgrn_bwd{l}")
    g["hgrn_norm_w"] = dnw_b[0]
    dqkv_act, dsm_a, dgz, da_a, ddt_a, dnw_a = _gdn_bwd(
        doa, sv["qkv_act"], p, k["gdn_a"], k["gdn_dt"], k["gdn_nw"], sv["st_a"], sv["ti_a"], f"gdn_bwd{l}")
    dqkv_raw, g["gdn_conv_w"], _ = _conv_bwd(dqkv_act, p, P_QKV, 1536, k["gdn_conv_w"], k["gdn_conv_b"], True, f"gdn_conv_bwd{l}")
    g["gdn_a_log"], g["gdn_dt_bias"], g["gdn_norm_w"] = da_a[0, :4], ddt_a[0, :4], dnw_a[0]
    dsmall = jnp.pad((dsm_a + dsm_c).astype(BF16), ((0, 0), (0, 0), (0, P_WIDTH - P_SMALL - LANES)))
    dp = jnp.concatenate([dqkv_raw, dgz, dxbc_raw, dgate, dhg, dsz, dsmall], axis=-1).reshape(t, P_WIDTH)
    dh1 = _mm(dp, k["win_t"], F32, f"mm_in_dx{l}").reshape(bsz, s, d)
    g["w_in"] = _restore_w_in(_mm(tr(sv["h1"]), dp, BF16, f"mm_in_dw{l}"))
    dx, dn1w, dsh1, dsc1 = _norm_mod_bwd(sv["x"], k["n1w"], sh1, sc1, dh1, dx1, f"norm1_bwd{l}")
    g["norm1_w"] = dn1w[0]
    dmod = jnp.concatenate([dsh1, dsc1, dg1, dsh2, dsc2, dg2], axis=-1)[:, 0]
    return dx, g, dlb, dmod


def _local_step(x, mod, wf, wr, target):
    lower = _lb_fwd(wr["hgrn_lb_param"])
    ks = [_layer_consts(l, wf, wr, lower) for l in range(DEPTH)]
    saved = []
    h = x
    for l in range(DEPTH):
        h, sv = _layer_fwd(l, h, mod[l], ks[l])
        saved.append(sv)
    loss8, dh, dfnw = _final_loss(h, wr["final_norm_w"][None], target)
    per_layer, dlbs, dmods = [None] * DEPTH, [None] * DEPTH, [None] * DEPTH
    for l in reversed(range(DEPTH)):
        dh, per_layer[l], dlbs[l], dmods[l] = _layer_bwd(l, dh, ks[l], saved[l])
    grads = {n: jnp.stack([per_layer[l][n] for l in range(DEPTH)]) for n in per_layer[0]}
    grads["hgrn_lb_param"] = _lb_bwd(wr["hgrn_lb_param"], jnp.concatenate(dlbs, axis=0))
    grads["final_norm_w"] = dfnw[0]
    return loss8[0, 0], dh, grads, jnp.stack(dmods)


def kernel(x, c, w_ada, b_ada, norm1_w, w_in, gdn_conv_w, gdn_a_log, gdn_dt_bias, gdn_norm_w, hgrn_lb_param, hgrn_norm_w, ssd_conv_w, ssd_conv_b, ssd_a_log, ssd_dt_bias, ssd_d, ssd_norm_w, w_br_a, w_br_b, w_br_c, w_out, norm2_w, ffn_w_up, ffn_conv_w, ffn_conv_b, ffn_w_down, final_norm_w, loss_target, m_w_ada, m_b_ada, m_norm1_w, m_w_in, m_gdn_conv_w, m_gdn_a_log, m_gdn_dt_bias, m_gdn_norm_w, m_hgrn_lb_param, m_hgrn_norm_w, m_ssd_conv_w, m_ssd_conv_b, m_ssd_a_log, m_ssd_dt_bias, m_ssd_d, m_ssd_norm_w, m_w_br_a, m_w_br_b, m_w_br_c, m_w_out, m_norm2_w, m_ffn_w_up, m_ffn_conv_w, m_ffn_conv_b, m_ffn_w_down, m_final_norm_w, v_w_ada, v_b_ada, v_norm1_w, v_w_in, v_gdn_conv_w, v_gdn_a_log, v_gdn_dt_bias, v_gdn_norm_w, v_hgrn_lb_param, v_hgrn_norm_w, v_ssd_conv_w, v_ssd_conv_b, v_ssd_a_log, v_ssd_dt_bias, v_ssd_d, v_ssd_norm_w, v_w_br_a, v_w_br_b, v_w_br_c, v_w_out, v_norm2_w, v_ffn_w_up, v_ffn_conv_w, v_ffn_conv_b, v_ffn_w_down, v_final_norm_w):
    given = dict(locals())
    w = {n: given[n] for n in WEIGHTS}
    m = {n: given["m_" + n] for n in WEIGHTS}
    v = {n: given["v_" + n] for n in WEIGHTS}
    me = _my_index()
    bsz = c.shape[0]
    ncol = 6 * D_MODEL // N_DEV

    shards = [w[n].astype(BF16) if n in MATMUL_WEIGHTS else w[n] for n, _, _ in SPLIT] + [c]
    gathered = _gather_two_level(shards, "gather_weights")
    wf = {n: _join_blocks(g, shape, axis) for (n, shape, axis), g in zip(SPLIT, gathered)}
    c_all = gathered[-1].reshape(N_DEV * bsz, D_MODEL)

    b_cols = lax.dynamic_slice_in_dim(b_ada, me * ncol, ncol, axis=1)[:, None]
    mod_cols = _ada_fwd(c_all, w_ada, b_cols)
    send = mod_cols.reshape(DEPTH, N_DEV, bsz, ncol).transpose(1, 0, 2, 3)
    got = _exchange([send], "scatter_mod", False)[0]
    mod = got.transpose(1, 2, 0, 3).reshape(DEPTH, bsz, 6 * D_MODEL)

    loss, dx, grads, dmod = _local_step(x, mod, wf, w, loss_target)

    send = dmod.reshape(DEPTH, bsz, N_DEV, ncol).transpose(2, 0, 1, 3)
    got_dmod = _exchange([send], "scatter_dmod", False)[0]
    dmod_all = got_dmod.transpose(1, 0, 2, 3).reshape(DEPTH, N_DEV * bsz, ncol)
    g_w_ada, g_b_cols = _ada_bwd(c_all.T, dmod_all)

    core = lax.axis_index("c")
    by_core = []
    for n, shape, axis in SPLIT:
        parts = _split_blocks(grads[n], shape, axis).astype(BF16)
        parts = parts.reshape((N_DEV // 2, 2, -1, parts.shape[-1]))
        by_core.append(jnp.swapaxes(parts, 0, 1))
    from_sibling = _sibling_swap(by_core, "swap_grads")
    sums = [_pair_sum(lax.dynamic_index_in_dim(mine, core, 0, keepdims=False), theirs, f"pair_sum_{n}")
            for (n, _, _), mine, theirs in zip(SPLIT, by_core, from_sibling)]
    got = _chip_scatter(sums, "scatter_grads")
    grads["b_ada"] = lax.dynamic_update_slice_in_dim(jnp.zeros_like(b_ada), g_b_cols[:, 0], me * ncol, axis=1)

    out = {}
    slots = [(n, g8) for (n, _, _), g8 in zip(SPLIT, got)] + [("w_ada", g_w_ada[None])]
    for n, gs in slots:
        bs = w[n].shape
        two = lambda a: a.reshape(-1, bs[-1])
        res = _sum_adamw(gs.reshape(gs.shape[0], -1, bs[-1]), two(w[n]), two(m[n]), two(v[n]), f"adamw_{n}")
        out[n] = [r.reshape(bs) for r in res]
    r8 = _gather_two_level([_pack_repl(grads)], "gather_small_grads")[0]
    res = _sum_adamw(r8, _pack_repl(w), _pack_repl(m), _pack_repl(v), "adamw_repl")
    repl_out = [_unpack_repl(o) for o in res]
    pick = lambda i, n: out[n][i] if n in out else repl_out[i][n]
    loss = lax.psum(loss, ("x", "y", "c"))
    return (loss, dx, *[pick(i, n) for i in range(4) for n in WEIGHTS])
```

```python
import functools
import math

import jax
import jax.numpy as jnp
from jax import lax
from jax.experimental import pallas as pl
from jax.experimental.pallas import tpu as pltpu

F32, BF16 = jnp.float32, jnp.bfloat16
HI = lax.Precision.HIGHEST
MESH_ID = pl.DeviceIdType.MESH

N_DEV = 8
EPS = 1e-6
D_MODEL = 1024
DEPTH = 2
GDN_HEADS, GDN_DK, GDN_CHUNK = 4, 128, 64
HGRN_HEADS, HGRN_DK, HGRN_CHUNK, HGRN_BLOCK = 4, 128, 16, 128
SSD_HEADS, SSD_HEAD_DIM, SSD_GROUPS, SSD_STATE, SSD_CHUNK = 8, 64, 2, 128, 64
SSD_INNER = SSD_HEADS * SSD_HEAD_DIM
FFN_HIDDEN = 2816
LANES = 128
P_QKV, P_GZ, P_XBC, P_GATE, P_HQ, P_HF, P_HI, P_HG, P_SZ, P_SMALL, P_WIDTH = (
    0, 1536, 2048, 3072, 6144, 6656, 7168, 7680, 8192, 8704, 9216)
SM_A, SM_B, SM_DT = 0, 4, 8
W_IN_SPLITS = (1536, 4, 4, 512, 512, 512, 512, 512, 512, 1024, 8, 3072)

ADAM_LR, ADAM_B1, ADAM_B2, ADAM_EPS, ADAM_WD, ADAM_STEP = 0.001, 0.9, 0.999, 1e-08, 0.01, 10

V7X_VMEM_LIMIT = 56 * 1024 * 1024
PACK_W = 1024
PACK_ROWS = 128


def _call(body, name, grid, in_specs, out_specs, out_shape, scratch=()):
    return pl.pallas_call(
        body, name=name, grid=grid, in_specs=in_specs, out_specs=out_specs, out_shape=out_shape,
        scratch_shapes=list(scratch),
        compiler_params=pltpu.CompilerParams(
            dimension_semantics=("arbitrary",) * len(grid), vmem_limit_bytes=V7X_VMEM_LIMIT),
    )


def _pick(n, cands):
    for c in cands:
        if n % c == 0:
            return c
    raise ValueError(f"no tile for {n} among {cands}")


def _sds(shape, dtype):
    return jax.ShapeDtypeStruct(shape, dtype)


def _dot(a, b):
    return lax.dot_general(a, b, (((1,), (0,)), ((), ())), precision=HI, preferred_element_type=F32)


NN, NT, TN = (((1,), (0,)), ((), ())), (((1,), (1,)), ((), ())), (((0,), (0,)), ((), ()))


def _mxu(a, b, dims):
    return lax.dot_general(a.astype(BF16), b.astype(BF16), dims, preferred_element_type=F32)


@jax.custom_vjp
def _bdot(a, b):
    return _mxu(a, b, NN)


@jax.custom_vjp
def _bdot_nt(a, b):
    return _mxu(a, b, NT)


@jax.custom_vjp
def _bdot_tn(a, b):
    return _mxu(a, b, TN)


_bdot.defvjp(lambda a, b: (_mxu(a, b, NN), (a, b)), lambda r, d: (_mxu(d, r[1], NT), _mxu(r[0], d, TN)))
_bdot_nt.defvjp(lambda a, b: (_mxu(a, b, NT), (a, b)), lambda r, d: (_mxu(d, r[1], NN), _mxu(d, r[0], TN)))
_bdot_tn.defvjp(lambda a, b: (_mxu(a, b, TN), (a, b)), lambda r, d: (_mxu(r[1], d, NT), _mxu(r[0], d, NN)))


def _split(x):
    hi = x.astype(BF16)
    return hi, (x - hi.astype(F32)).astype(BF16)


def _mxu3(a, b, dims):
    ah, al = _split(a)
    bh, bl = _split(b)
    return _mxu(ah, bh, dims) + (_mxu(ah, bl, dims) + _mxu(al, bh, dims))


@jax.custom_vjp
def _dot3(a, b):
    return _mxu3(a, b, NN)


_dot3.defvjp(lambda a, b: (_mxu3(a, b, NN), (a, b)), lambda r, d: (_mxu3(d, r[1], NT), _mxu3(r[0], d, TN)))


def _pieces(x):
    x1 = x.astype(BF16)
    r1 = x - x1.astype(F32)
    x2 = r1.astype(BF16)
    return x1, x2, (r1 - x2.astype(F32)).astype(BF16)


def _mask_mxu(mask, x, dims):
    x1, x2, x3 = _pieces(x)
    return _mxu(mask, x1, dims) + (_mxu(mask, x2, dims) + _mxu(mask, x3, dims))


def _spread_mxu(x, mask, dims):
    x1, x2, x3 = _pieces(x)
    return _mxu(x1, mask, dims) + (_mxu(x2, mask, dims) + _mxu(x3, mask, dims))


@jax.custom_vjp
def _mask_dot(mask, x):
    return _mask_mxu(mask, x, NN)


@jax.custom_vjp
def _spread_dot(x, mask):
    return _spread_mxu(x, mask, NN)


_mask_dot.defvjp(lambda m, x: (_mask_mxu(m, x, NN), m), lambda m, d: (jnp.zeros_like(m), _mask_mxu(m, d, TN)))
_spread_dot.defvjp(lambda x, m: (_spread_mxu(x, m, NN), m), lambda m, d: (_spread_mxu(d, m, NT), jnp.zeros_like(m)))


def _iota(shape, axis):
    return lax.broadcasted_iota(jnp.int32, shape, axis)


def _silu(x):
    return x * jax.nn.sigmoid(x)


def _softplus(x):
    return jnp.maximum(x, 0.0) + jnp.log1p(jnp.exp(-jnp.abs(x)))


def _rms(x, w):
    return x * lax.rsqrt(jnp.mean(x * x, axis=-1, keepdims=True) + EPS) * w


def _lane_col(x, lane):
    m = (_iota(x.shape, 1) == lane).astype(F32)
    return jnp.sum(x * m, axis=1, keepdims=True)


def _col_to_row(c):
    n = c.shape[0]
    eye = (_iota((n, n), 0) == _iota((n, n), 1)).astype(F32)
    return jnp.sum(c * eye, axis=0, keepdims=True)


def _tril(n, strict=False):
    r, c = _iota((n, n), 0), _iota((n, n), 1)
    return (r > c) if strict else (r >= c)


def _mm(a, b, out_dtype, name):
    m, k = a.shape
    n = b.shape[1]
    tm = _pick(m, (1024, 1408, 512, 256, 128, 64, 32, 16, 8))
    tn = _pick(n, (1024, 1408, 768, 512, 384, 256, 128))
    tk = k if k <= 3072 else _pick(k, (1024, 768, 512, 384, 256, 128))
    nk = k // tk

    def body(a_ref, b_ref, o_ref, acc_ref):
        kk = pl.program_id(2)

        @pl.when(kk == 0)
        def _():
            acc_ref[...] = jnp.zeros_like(acc_ref)

        acc_ref[...] += _bdot(a_ref[...], b_ref[...])

        @pl.when(kk == nk - 1)
        def _():
            o_ref[...] = acc_ref[...].astype(out_dtype)

    return _call(
        body, name, (m // tm, n // tn, nk),
        [pl.BlockSpec((tm, tk), lambda i, j, kk: (i, kk)), pl.BlockSpec((tk, tn), lambda i, j, kk: (kk, j))],
        pl.BlockSpec((tm, tn), lambda i, j, kk: (i, j)), _sds((m, n), out_dtype),
        scratch=[pltpu.VMEM((tm, tn), F32)],
    )(a, b)


def _ada_fwd(c_all, w, b):
    depth, _, n = w.shape
    rows = c_all.shape[0]

    def body(c_ref, w_ref, b_ref, o_ref):
        o_ref[...] = _dot(_silu(c_ref[...]), w_ref[...]) + b_ref[...]

    return _call(
        body, "ada_fwd", (depth,),
        [pl.BlockSpec((rows, D_MODEL), lambda l: (0, 0)), pl.BlockSpec((None, D_MODEL, n), lambda l: (l, 0, 0)),
         pl.BlockSpec((None, 1, n), lambda l: (l, 0, 0))],
        pl.BlockSpec((None, rows, n), lambda l: (l, 0, 0)), _sds((depth, rows, n), F32),
    )(c_all, w, b)


def _ada_bwd(c_all_t, dmod):
    depth, rows, n = dmod.shape

    def body(ct_ref, dm_ref, dw_ref, db_ref):
        dm = dm_ref[...]
        dw_ref[...] = _dot(_silu(ct_ref[...]), dm)
        db_ref[...] = jnp.sum(dm, axis=0, keepdims=True)

    return _call(
        body, "ada_bwd", (depth,),
        [pl.BlockSpec((D_MODEL, rows), lambda l: (0, 0)), pl.BlockSpec((None, rows, n), lambda l: (l, 0, 0))],
        [pl.BlockSpec((None, D_MODEL, n), lambda l: (l, 0, 0)), pl.BlockSpec((None, 1, n), lambda l: (l, 0, 0))],
        [_sds((depth, D_MODEL, n), F32), _sds((depth, 1, n), F32)],
    )(c_all_t, dmod)


def _lb_fn(p):
    rows = [p[l:l + 1] for l in range(DEPTH)]
    mx = functools.reduce(jnp.maximum, rows)
    es = [jnp.exp(r - mx) for r in rows]
    tot = functools.reduce(lambda a, b: a + b, es)
    sm = [e / tot for e in es]
    out, run = [], None
    for l in range(DEPTH):
        run = sm[l] if run is None else run + sm[l]
        out.append(run - sm[0])
    return jnp.concatenate(out, axis=0)


def _lb_fwd(p):
    def body(p_ref, o_ref):
        o_ref[...] = _lb_fn(p_ref[...])

    full = pl.BlockSpec(p.shape, lambda i: (0, 0))
    return _call(body, "lb_fwd", (1,), [full], full, _sds(p.shape, F32))(p)


def _lb_bwd(p, d_lower):
    def body(p_ref, d_ref, o_ref):
        _, vjp = jax.vjp(_lb_fn, p_ref[...])
        o_ref[...] = vjp(d_ref[...])[0]

    full = pl.BlockSpec(p.shape, lambda i: (0, 0))
    return _call(body, "lb_bwd", (1,), [full, full], full, _sds(p.shape, F32))(p, d_lower)


def _norm_mod_fn(x, w, shift, scale):
    return _rms(x, w) * (1.0 + scale) + shift


def _norm_mod_fwd(x, w, shift, scale, name):
    bsz, s, d = x.shape
    ts = _pick(s, (256, 128, 64, 32, 16, 8))

    def body(x_ref, w_ref, sh_ref, sc_ref, o_ref):
        o_ref[...] = _norm_mod_fn(x_ref[...], w_ref[...], sh_ref[...], sc_ref[...]).astype(BF16)

    row = pl.BlockSpec((None, ts, d), lambda b, i: (b, i, 0))
    per_b = pl.BlockSpec((None, 1, d), lambda b, i: (b, 0, 0))
    return _call(body, name, (bsz, s // ts), [row, pl.BlockSpec((1, d), lambda b, i: (0, 0)), per_b, per_b],
                 row, _sds(x.shape, BF16))(x, w, shift, scale)


def _norm_mod_bwd(x, w, shift, scale, dh, carry, name):
    bsz, s, d = x.shape
    ts = _pick(s, (256, 128, 64, 32, 16, 8))

    def body(x_ref, w_ref, sh_ref, sc_ref, dh_ref, c_ref, dx_ref, dw_ref, dsh_ref, dsc_ref):
        b, i = pl.program_id(0), pl.program_id(1)
        _, vjp = jax.vjp(_norm_mod_fn, x_ref[...], w_ref[...], sh_ref[...], sc_ref[...])
        dx, dw, dsh, dsc = vjp(dh_ref[...])
        dx_ref[...] = dx + c_ref[...]

        @pl.when((b == 0) & (i == 0))
        def _():
            dw_ref[...] = jnp.zeros_like(dw_ref)

        @pl.when(i == 0)
        def _():
            dsh_ref[...] = jnp.zeros_like(dsh_ref)
            dsc_ref[...] = jnp.zeros_like(dsc_ref)

        dw_ref[...] += dw
        dsh_ref[...] += dsh
        dsc_ref[...] += dsc

    row = pl.BlockSpec((None, ts, d), lambda b, i: (b, i, 0))
    per_b = pl.BlockSpec((None, 1, d), lambda b, i: (b, 0, 0))
    wspec = pl.BlockSpec((1, d), lambda b, i: (0, 0))
    return _call(body, name, (bsz, s // ts), [row, wspec, per_b, per_b, row, row],
                 [row, wspec, per_b, per_b],
                 [_sds(x.shape, F32), _sds((1, d), F32), _sds((bsz, 1, d), F32), _sds((bsz, 1, d), F32)],
                 )(x, w, shift, scale, dh, carry)


def _resid_fwd(x, y, gate, name):
    bsz, s, d = x.shape
    ts = _pick(s, (512, 256, 128, 64, 32, 16, 8))

    def body(x_ref, y_ref, g_ref, o_ref):
        o_ref[...] = x_ref[...] + g_ref[...] * y_ref[...]

    row = pl.BlockSpec((None, ts, d), lambda b, i: (b, i, 0))
    per_b = pl.BlockSpec((None, 1, d), lambda b, i: (b, 0, 0))
    return _call(body, name, (bsz, s // ts), [row, row, per_b], row, _sds(x.shape, F32))(x, y, gate)


def _gate_bwd(dx, y, gate, name):
    bsz, s, d = dx.shape
    ts = _pick(s, (512, 256, 128, 64, 32, 16, 8))

    def body(dx_ref, y_ref, g_ref, dy_ref, dg_ref):
        dxv = dx_ref[...]
        dy_ref[...] = (dxv * g_ref[...]).astype(BF16)

        @pl.when(pl.program_id(1) == 0)
        def _():
            dg_ref[...] = jnp.zeros_like(dg_ref)

        dg_ref[...] += jnp.sum(dxv * y_ref[...], axis=0, keepdims=True)

    row = pl.BlockSpec((None, ts, d), lambda b, i: (b, i, 0))
    per_b = pl.BlockSpec((None, 1, d), lambda b, i: (b, 0, 0))
    return _call(body, name, (bsz, s // ts), [row, row, per_b], [row, per_b],
                 [_sds(dx.shape, BF16), _sds((bsz, 1, d), F32)])(dx, y, gate)


HALO = 8


def _conv_pre(xx, w_ref, b_ref, kw, rows):
    acc = w_ref[kw - 1:kw, :] * xx[HALO:HALO + rows]
    for k in range(kw - 1):
        acc = acc + w_ref[k:k + 1, :] * pltpu.roll(xx, kw - 1 - k, 0)[HALO:HALO + rows]
    return acc + b_ref[...]


def _conv_fwd(x, col0, width, w, b, act, name):
    bsz, s, _ = x.shape
    kw = w.shape[0]
    ts = _pick(s, (512, 256, 128, 64, 32, 16, 8))
    tc = _pick(width, (512, 256, 128))
    assert col0 % tc == 0
    c0 = col0 // tc
    hb = ts // HALO

    def body(x_ref, xp_ref, w_ref, b_ref, o_ref):
        i = pl.program_id(1)
        xp = jnp.where(i > 0, xp_ref[...], 0.0)
        xx = jnp.concatenate([xp, x_ref[...]], axis=0)
        pre = _conv_pre(xx, w_ref, b_ref, kw, ts)
        o_ref[...] = _silu(pre) if act else pre

    return _call(
        body, name, (bsz, s // ts, width // tc),
        [pl.BlockSpec((None, ts, tc), lambda bb, i, j: (bb, i, c0 + j)),
         pl.BlockSpec((None, HALO, tc), lambda bb, i, j: (bb, jnp.maximum(i * hb - 1, 0), c0 + j)),
         pl.BlockSpec((kw, tc), lambda bb, i, j: (0, j)), pl.BlockSpec((1, tc), lambda bb, i, j: (0, j))],
        pl.BlockSpec((None, ts, tc), lambda bb, i, j: (bb, i, j)), _sds((bsz, s, width), F32),
    )(x, x, w, b)


def _conv_bwd(dy, x, col0, width, w, b, act, name):
    bsz, s, _ = x.shape
    kw = w.shape[0]
    ts = _pick(s, (512, 256, 128, 64, 32, 16, 8))
    tc = _pick(width, (512, 256, 128))
    c0 = col0 // tc
    hb = ts // HALO
    nt = s // ts
    last_h = s // HALO - 1

    def body(x_ref, xp_ref, xn_ref, dy_ref, dyn_ref, w_ref, b_ref, dx_ref, dw_ref, db_ref):
        bb, i = pl.program_id(1), pl.program_id(2)
        xp = jnp.where(i > 0, xp_ref[...], 0.0)
        xx = jnp.concatenate([xp, x_ref[...], xn_ref[...]], axis=0)
        dyy = jnp.concatenate([dy_ref[...], jnp.where(i < nt - 1, dyn_ref[...], 0.0)], axis=0)
        n = ts + HALO
        if act:
            pre = _conv_pre(xx, w_ref, b_ref, kw, n)
            sg = jax.nn.sigmoid(pre)
            dpre = dyy * (sg * (1.0 + pre * (1.0 - sg)))
        else:
            dpre = dyy
        dx = w_ref[kw - 1:kw, :] * dpre[:ts]
        for k in range(kw - 1):
            dx = dx + w_ref[k:k + 1, :] * pltpu.roll(dpre, n - (kw - 1 - k), 0)[:ts]
        dx_ref[...] = dx.astype(BF16)

        @pl.when((bb == 0) & (i == 0))
        def _():
            dw_ref[...] = jnp.zeros_like(dw_ref)
            db_ref[...] = jnp.zeros_like(db_ref)

        dt = dpre[:ts]
        db_ref[...] += jnp.sum(dt, axis=0, keepdims=True)
        dw_ref[kw - 1:kw, :] += jnp.sum(dt * xx[HALO:HALO + ts], axis=0, keepdims=True)
        for k in range(kw - 1):
            xs = pltpu.roll(xx, kw - 1 - k, 0)[HALO:HALO + ts]
            dw_ref[k:k + 1, :] += jnp.sum(dt * xs, axis=0, keepdims=True)

    xspec = lambda f: pl.BlockSpec((None, HALO, tc), f)
    return _call(
        body, name, (width // tc, bsz, nt),
        [pl.BlockSpec((None, ts, tc), lambda j, bb, i: (bb, i, c0 + j)),
         xspec(lambda j, bb, i: (bb, jnp.maximum(i * hb - 1, 0), c0 + j)),
         xspec(lambda j, bb, i: (bb, jnp.minimum((i + 1) * hb, last_h), c0 + j)),
         pl.BlockSpec((None, ts, tc), lambda j, bb, i: (bb, i, j)),
         xspec(lambda j, bb, i: (bb, jnp.minimum((i + 1) * hb, last_h), j)),
         pl.BlockSpec((kw, tc), lambda j, bb, i: (0, j)), pl.BlockSpec((1, tc), lambda j, bb, i: (0, j))],
        [pl.BlockSpec((None, ts, tc), lambda j, bb, i: (bb, i, j)),
         pl.BlockSpec((kw, tc), lambda j, bb, i: (0, j)), pl.BlockSpec((1, tc), lambda j, bb, i: (0, j))],
        [_sds((bsz, s, width), BF16), _sds((kw, width), F32), _sds((1, width), F32)],
    )(x, x, x, dy, dy, w, b)


def _conv_glu_fwd(x, w, b, name):
    bsz, s, f2 = x.shape
    f = f2 // 2
    kw = w.shape[0]
    ts = _pick(s, (512, 256, 128, 64, 32, 16, 8))
    tc = _pick(f, (256, 128))
    nf = f // tc
    hb = ts // HALO

    def body(xg_ref, xgp_ref, xv_ref, xvp_ref, wg_ref, wv_ref, bg_ref, bv_ref, o_ref):
        i = pl.program_id(1)
        halves = []
        for x_ref, xp_ref, w_ref, b_ref in ((xg_ref, xgp_ref, wg_ref, bg_ref), (xv_ref, xvp_ref, wv_ref, bv_ref)):
            xx = jnp.concatenate([jnp.where(i > 0, xp_ref[...], 0.0), x_ref[...]], axis=0)
            halves.append(_conv_pre(xx, w_ref, b_ref, kw, ts))
        o_ref[...] = (_silu(halves[0]) * halves[1]).astype(BF16)

    tile = lambda off: pl.BlockSpec((None, ts, tc), lambda bb, i, j: (bb, i, off + j))
    prev = lambda off: pl.BlockSpec((None, HALO, tc), lambda bb, i, j: (bb, jnp.maximum(i * hb - 1, 0), off + j))
    wsp = lambda rows, off: pl.BlockSpec((rows, tc), lambda bb, i, j: (0, off + j))
    return _call(
        body, name, (bsz, s // ts, nf),
        [tile(0), prev(0), tile(nf), prev(nf), wsp(kw, 0), wsp(kw, nf), wsp(1, 0), wsp(1, nf)],
        pl.BlockSpec((None, ts, tc), lambda bb, i, j: (bb, i, j)), _sds((bsz, s, f), BF16),
    )(x, x, x, x, w, w, b, b)


def _conv_glu_bwd(da, x, w, b, name):
    bsz, s, f2 = x.shape
    f = f2 // 2
    kw = w.shape[0]
    ts = _pick(s, (512, 256, 128, 64, 32, 16, 8))
    tc = _pick(f, (256, 128))
    nf = f // tc
    hb = ts // HALO
    nt = s // ts
    last_h = s // HALO - 1
    n = ts + HALO

    def body(xg_ref, xgp_ref, xgn_ref, xv_ref, xvp_ref, xvn_ref, da_ref, dan_ref,
             wg_ref, wv_ref, bg_ref, bv_ref, wx_ref, dx_ref, dw_ref, db_ref):
        j, bb, i = pl.program_id(0), pl.program_id(1), pl.program_id(2)
        day = jnp.concatenate([da_ref[...], jnp.where(i < nt - 1, dan_ref[...], 0.0)], axis=0)
        xg = jnp.concatenate([jnp.where(i > 0, xgp_ref[...], 0.0), xg_ref[...], xgn_ref[...]], axis=0)
        pre_g = _conv_pre(xg, wg_ref, bg_ref, kw, n)
        sg = jax.nn.sigmoid(pre_g)

        @pl.when((bb == 0) & (i == 0))
        def _():
            dw_ref[...] = jnp.zeros_like(dw_ref)
            db_ref[...] = jnp.zeros_like(db_ref)

        def finish(dpre, xx):
            dx = wx_ref[kw - 1:kw, :] * dpre[:ts]
            for k in range(kw - 1):
                dx = dx + wx_ref[k:k + 1, :] * pltpu.roll(dpre, n - (kw - 1 - k), 0)[:ts]
            dx_ref[...] = dx.astype(BF16)
            dt = dpre[:ts]
            db_ref[...] += jnp.sum(dt, axis=0, keepdims=True)
            dw_ref[kw - 1:kw, :] += jnp.sum(dt * xx[HALO:HALO + ts], axis=0, keepdims=True)
            for k in range(kw - 1):
                dw_ref[k:k + 1, :] += jnp.sum(dt * pltpu.roll(xx, kw - 1 - k, 0)[HALO:HALO + ts], axis=0, keepdims=True)

        @pl.when(j < nf)
        def _():
            xv = jnp.concatenate([jnp.where(i > 0, xvp_ref[...], 0.0), xv_ref[...], xvn_ref[...]], axis=0)
            pre_v = _conv_pre(xv, wv_ref, bv_ref, kw, n)
            finish(day * pre_v * (sg * (1.0 + pre_g * (1.0 - sg))), xg)

        @pl.when(j >= nf)
        def _():
            xv = jnp.concatenate([jnp.where(i > 0, xvp_ref[...], 0.0), xv_ref[...], xvn_ref[...]], axis=0)
            finish(day * (pre_g * sg), xv)

    tile = lambda off: pl.BlockSpec((None, ts, tc), lambda j, bb, i: (bb, i, off + j % nf))
    prev = lambda off: pl.BlockSpec((None, HALO, tc), lambda j, bb, i: (bb, jnp.maximum(i * hb - 1, 0), off + j % nf))
    nxt = lambda off: pl.BlockSpec((None, HALO, tc), lambda j, bb, i: (bb, jnp.minimum((i + 1) * hb, last_h), off + j % nf))
    wsp = lambda rows, off: pl.BlockSpec((rows, tc), lambda j, bb, i: (0, off + j % nf))
    own = lambda rows: pl.BlockSpec((rows, tc), lambda j, bb, i: (0, j))
    return _call(
        body, name, (2 * nf, bsz, nt),
        [tile(0), prev(0), nxt(0), tile(nf), prev(nf), nxt(nf), tile(0), nxt(0),
         wsp(kw, 0), wsp(kw, nf), wsp(1, 0), wsp(1, nf), own(kw)],
        [pl.BlockSpec((None, ts, tc), lambda j, bb, i: (bb, i, j)), own(kw), own(1)],
        [_sds((bsz, s, f2), BF16), _sds((kw, f2), F32), _sds((1, f2), F32)],
    )(x, x, x, x, x, x, da, da, w, w, b, b, w)


def _merge_fwd(p, oa, ob, oc, wa, wb, wc, name):
    bsz, s, _ = p.shape
    tm = _pick(s, (256, 128, 64, 32, 16, 8))
    gblk = P_GATE // (3 * D_MODEL)

    def body(g_ref, oa_ref, ob_ref, oc_ref, wa_ref, wb_ref, wc_ref, o_ref):
        acc = None
        for i, (o_r, w_r) in enumerate(((oa_ref, wa_ref), (ob_ref, wb_ref), (oc_ref, wc_ref))):
            y = _bdot(o_r[...], w_r[...])
            t = jax.nn.sigmoid(g_ref[:, i * D_MODEL:(i + 1) * D_MODEL]) * y
            acc = t if acc is None else acc + t
        o_ref[...] = acc.astype(BF16)

    orow = pl.BlockSpec((None, tm, 512), lambda b, i: (b, i, 0))
    wfull = pl.BlockSpec((512, D_MODEL), lambda b, i: (0, 0))
    return _call(
        body, name, (bsz, s // tm),
        [pl.BlockSpec((None, tm, 3 * D_MODEL), lambda b, i: (b, i, gblk)), orow, orow, orow, wfull, wfull, wfull],
        pl.BlockSpec((None, tm, D_MODEL), lambda b, i: (b, i, 0)), _sds((bsz, s, D_MODEL), BF16),
    )(p, oa, ob, oc, wa, wb, wc)


def _merge_bwd(dm, p, oa, ob, oc, wa, wb, wc, wat, wbt, wct, name):
    bsz, s, _ = p.shape
    tm = _pick(s, (256, 128, 64, 32, 16, 8))
    gblk = P_GATE // (3 * D_MODEL)

    def body(dm_ref, g_ref, oa_ref, ob_ref, oc_ref, wa_ref, wb_ref, wc_ref, wat_ref, wbt_ref, wct_ref,
             dg_ref, doa_ref, dob_ref, doc_ref, dya_ref, dyb_ref, dyc_ref):
        dmv = dm_ref[...]
        trip = ((oa_ref, wa_ref, wat_ref, doa_ref, dya_ref), (ob_ref, wb_ref, wbt_ref, dob_ref, dyb_ref),
                (oc_ref, wc_ref, wct_ref, doc_ref, dyc_ref))
        for i, (o_r, w_r, wt_r, do_r, dy_r) in enumerate(trip):
            y = _bdot(o_r[...], w_r[...])
            sg = jax.nn.sigmoid(g_ref[:, i * D_MODEL:(i + 1) * D_MODEL])
            dg_ref[:, i * D_MODEL:(i + 1) * D_MODEL] = (dmv * y * sg * (1.0 - sg)).astype(BF16)
            dy = (dmv * sg).astype(BF16)
            dy_r[...] = dy
            do_r[...] = _bdot(dy, wt_r[...])

    orow = pl.BlockSpec((None, tm, 512), lambda b, i: (b, i, 0))
    drow = pl.BlockSpec((None, tm, D_MODEL), lambda b, i: (b, i, 0))
    grow = pl.BlockSpec((None, tm, 3 * D_MODEL), lambda b, i: (b, i, 0))
    wfull = pl.BlockSpec((512, D_MODEL), lambda b, i: (0, 0))
    wtfull = pl.BlockSpec((D_MODEL, 512), lambda b, i: (0, 0))
    return _call(
        body, name, (bsz, s // tm),
        [drow, pl.BlockSpec((None, tm, 3 * D_MODEL), lambda b, i: (b, i, gblk)), orow, orow, orow,
         wfull, wfull, wfull, wtfull, wtfull, wtfull],
        [grow, orow, orow, orow, drow, drow, drow],
        [_sds((bsz, s, 3 * D_MODEL), BF16)] + [_sds((bsz, s, 512), F32)] * 3 + [_sds((bsz, s, D_MODEL), BF16)] * 3,
    )(dm, p, oa, ob, oc, wa, wb, wc, wat, wbt, wct)


def _final_loss(x, w, target):
    bsz, s, d = x.shape
    ts = _pick(s, (256, 128, 64, 32, 16, 8))

    def body(x_ref, w_ref, t_ref, loss_ref, dx_ref, dw_ref):
        first = (pl.program_id(0) == 0) & (pl.program_id(1) == 0)
        y, vjp = jax.vjp(_rms, x_ref[...], w_ref[...])
        err = y - t_ref[...]
        dx, dw = vjp(err * (1.0 / d))
        dx_ref[...] = dx

        @pl.when(first)
        def _():
            loss_ref[...] = jnp.zeros_like(loss_ref)
            dw_ref[...] = jnp.zeros_like(dw_ref)

        loss_ref[...] += 0.5 * jnp.sum(jnp.sum(err * err, axis=1, keepdims=True), axis=0, keepdims=True) * (1.0 / d)
        dw_ref[...] += dw

    row = pl.BlockSpec((None, ts, d), lambda b, i: (b, i, 0))
    wspec = pl.BlockSpec((1, d), lambda b, i: (0, 0))
    return _call(body, "final_loss", (bsz, s // ts), [row, wspec, row],
                 [pl.BlockSpec((8, LANES), lambda b, i: (0, 0)), row, wspec],
                 [_sds((8, LANES), F32), _sds(x.shape, F32), _sds((1, d), F32)])(x, w, target)


def _unit_lower_inverses(ms):
    n = ms[0].shape[0]
    eye = (_iota((n, n), 0) == _iota((n, n), 1)).astype(F32)
    ps = [-m for m in ms]
    xs = [eye + p for p in ps]
    for _ in range(int(math.log2(n)) - 1):
        ps = [_mxu3(p, p, NN) for p in ps]
        xs = [x + _mxu3(x, p, NN) for x, p in zip(xs, ps)]
    return xs


@jax.custom_vjp
def _known_inverse(m, t):
    return t


_known_inverse.defvjp(lambda m, t: (t, t), lambda t, dt: (-_mxu3(t, _mxu3(dt, t, NT), TN), jnp.zeros_like(t)))


def _gdn_chunk(states, qkv, small, z, a_row, dt_row, nw, tinvs=None):
    nb = len(qkv)
    c = qkv[0].shape[0]
    kw = GDN_HEADS * GDN_DK
    incl, strict = _tril(c), _tril(c, True)
    g_all = [-jnp.exp(a_row) * _softplus(small[b] + dt_row) for b in range(nb)]
    beta_all = [jax.nn.sigmoid(small[b]) for b in range(nb)]
    big_g_all = [_mask_dot(incl.astype(BF16), g_all[b]) for b in range(nb)]
    items = [(b, h) for b in range(nb) for h in range(GDN_HEADS)]
    ids = range(len(items))
    col = lambda b, part, h: qkv[b][:, part * kw + h * GDN_DK:part * kw + (h + 1) * GDN_DK]
    unit = lambda t: t * lax.rsqrt(jnp.sum(t * t, axis=-1, keepdims=True) + EPS)
    q = [unit(col(b, 0, h)) * (GDN_DK ** -0.5) for b, h in items]
    k = [unit(col(b, 1, h)) for b, h in items]
    v = [col(b, 2, h) for b, h in items]
    gc = [_lane_col(big_g_all[b], SM_A + h) for b, h in items]
    bc = [_lane_col(beta_all[b], SM_B + h) for b, h in items]
    g_last = [jnp.sum(_lane_col(g_all[b], SM_A + h), axis=0, keepdims=True) for b, h in items]
    decay = [jnp.where(incl, jnp.exp(jnp.where(incl, gc[i] - _col_to_row(gc[i]), 0.0)), 0.0) for i in ids]
    kb = [k[i] * bc[i] for i in ids]
    m = [jnp.where(strict, _bdot_nt(kb[i], k[i]) * decay[i], 0.0) for i in ids]
    if tinvs is None:
        tinv = _unit_lower_inverses(m)
    else:
        tinv = [_known_inverse(m[i], tinvs[i]) for i in ids]
    eg = [jnp.exp(gc[i]) for i in ids]
    u = [_dot3(tinv[i], v[i] * bc[i]) for i in ids]
    w = [_dot3(tinv[i], kb[i] * eg[i]) for i in ids]
    attn = [_bdot_nt(q[i], k[i]) * decay[i] for i in ids]
    v_new = [u[i] - _bdot(w[i], states[i]) for i in ids]
    o_st = [_bdot(q[i] * eg[i], states[i]) for i in ids]
    o = [o_st[i] + _bdot(attn[i], v_new[i]) for i in ids]
    grow = [_bdot_tn(k[i] * jnp.exp(g_last[i] - gc[i]), v_new[i]) for i in ids]
    new_states = [states[i] * jnp.exp(g_last[i]) + grow[i] for i in ids]
    outs = [_rms(o[i], nw) * _silu(z[b][:, h * GDN_DK:(h + 1) * GDN_DK]) for i, (b, h) in enumerate(items)]
    per_seq = [jnp.concatenate(outs[b * GDN_HEADS:(b + 1) * GDN_HEADS], axis=1) for b in range(nb)]
    return new_states, per_seq, tinv


def _seq_items(bsz, heads):
    return [(b, h) for b in range(bsz) for h in range(heads)]


def _state_spec():
    return pl.BlockSpec((None, None, 4, LANES, LANES), lambda b, n: (b, n, 0, 0, 0))


def _gdn_fwd(qkv_act, p, a_row, dt_row, nw, name):
    bsz, s, _ = qkv_act.shape
    c = GDN_CHUNK
    nc = s // c
    items = _seq_items(bsz, GDN_HEADS)

    def body(qkv_ref, sm_ref, z_ref, a_ref, dt_ref, nw_ref, o_ref, st_ref, ti_ref, st_scr):
        @pl.when(pl.program_id(0) == 0)
        def _():
            st_scr[...] = jnp.zeros_like(st_scr)

        st_ref[...] = st_scr[...]
        seqs = range(bsz)
        new_states, o, tinvs = _gdn_chunk(
            [st_scr[b, h] for b, h in items], [qkv_ref[b] for b in seqs], [sm_ref[b] for b in seqs],
            [z_ref[b] for b in seqs], a_ref[...], dt_ref[...], nw_ref[...])
        for i, (b, h) in enumerate(items):
            st_scr[b, h] = new_states[i]
            ti_ref[b, h] = tinvs[i]
        for b in seqs:
            o_ref[b] = o[b].astype(BF16)

    row = lambda w, blk: pl.BlockSpec((bsz, c, w), lambda n, blk=blk: (0, n, blk))
    prm = pl.BlockSpec((1, LANES), lambda n: (0, 0))
    return _call(
        body, name, (nc,), [row(1536, 0), row(LANES, P_SMALL // LANES), row(512, P_GZ // 512), prm, prm, prm],
        [row(512, 0), pl.BlockSpec((bsz, None, 4, LANES, LANES), lambda n: (0, n, 0, 0, 0)),
         pl.BlockSpec((bsz, None, 4, c, c), lambda n: (0, n, 0, 0, 0))],
        [_sds((bsz, s, 512), BF16), _sds((bsz, nc, 4, LANES, LANES), F32), _sds((bsz, nc, 4, c, c), F32)],
        scratch=[pltpu.VMEM((bsz, 4, LANES, LANES), F32)],
    )(qkv_act, p, p, a_row, dt_row, nw)


def _gdn_bwd(do, qkv_act, p, a_row, dt_row, nw, st_all, ti_all, name):
    bsz, s, _ = qkv_act.shape
    c = GDN_CHUNK
    nc = s // c

    items = _seq_items(bsz, GDN_HEADS)

    def body(qkv_ref, sm_ref, z_ref, a_ref, dt_ref, nw_ref, do_ref, st_ref, ti_ref,
             dqkv_ref, dsm_ref, dz_ref, da_ref, ddt_ref, dnw_ref, ds_scr):
        @pl.when(pl.program_id(0) == 0)
        def _():
            ds_scr[...] = jnp.zeros_like(ds_scr)
            da_ref[...] = jnp.zeros_like(da_ref)
            ddt_ref[...] = jnp.zeros_like(ddt_ref)
            dnw_ref[...] = jnp.zeros_like(dnw_ref)

        seqs = range(bsz)
        tinvs = [ti_ref[b, h] for b, h in items]
        chunk = lambda *a: _gdn_chunk(*a, tinvs=tinvs)[:2]
        _, vjp = jax.vjp(chunk, [st_ref[b, h] for b, h in items], [qkv_ref[b] for b in seqs], [sm_ref[b] for b in seqs],
                         [z_ref[b] for b in seqs], a_ref[...], dt_ref[...], nw_ref[...])
        d_states, dqkv, dsm, dz, da, ddt, dnw = vjp(([ds_scr[b, h] for b, h in items], [do_ref[b] for b in seqs]))
        for i, (b, h) in enumerate(items):
            ds_scr[b, h] = d_states[i]
        for b in seqs:
            dqkv_ref[b] = dqkv[b]
            dsm_ref[b] = dsm[b]
            dz_ref[b] = dz[b].astype(BF16)
        da_ref[...] += da
        ddt_ref[...] += ddt
        dnw_ref[...] += dnw

    rrow = lambda w, blk: pl.BlockSpec((bsz, c, w), lambda n, blk=blk: (0, nc - 1 - n, blk))
    prm = pl.BlockSpec((1, LANES), lambda n: (0, 0))
    return _call(
        body, name, (nc,),
        [rrow(1536, 0), rrow(LANES, P_SMALL // LANES), rrow(512, P_GZ // 512), prm, prm, prm, rrow(512, 0),
         pl.BlockSpec((bsz, None, 4, LANES, LANES), lambda n: (0, nc - 1 - n, 0, 0, 0)),
         pl.BlockSpec((bsz, None, 4, c, c), lambda n: (0, nc - 1 - n, 0, 0, 0))],
        [rrow(1536, 0), rrow(LANES, 0), rrow(512, 0), prm, prm, prm],
        [_sds((bsz, s, 1536), F32), _sds((bsz, s, LANES), F32), _sds((bsz, s, 512), BF16)] + [_sds((1, LANES), F32)] * 3,
        scratch=[pltpu.VMEM((bsz, 4, LANES, LANES), F32)],
    )(qkv_act, p, p, a_row, dt_row, nw, do, st_all, ti_all)


def _hgrn_block(states, q_raw, f_raw, i_raw, g_raw, lb, nw):
    n = q_raw[0].shape[0]
    c = HGRN_CHUNK
    r, cc = _iota((n, n), 0), _iota((n, n), 1)
    same = (r // c) == (cc // c)
    causal = same & (r >= cc)
    ref_row = (r // c) * c + (c // 2 - 1)
    run_sum = causal.astype(F32)
    rel_sum = run_sum - (same & (ref_row >= cc)).astype(F32)
    sums = jnp.concatenate([run_sum, rel_sum, same.astype(F32)], axis=0).astype(BF16)
    seqs, chunks = range(len(q_raw)), range(n // c)
    items = _seq_items(len(q_raw), HGRN_HEADS)
    ids = range(len(items))
    hs = lambda t, h: t[:, h * HGRN_DK:(h + 1) * HGRN_DK]
    rows = lambda t, j: t[j * c:(j + 1) * c]
    q = [_silu(q_raw[b]) for b in seqs]
    logf = [jnp.log(lb + (1.0 - lb) * jax.nn.sigmoid(f_raw[b])) for b in seqs]
    k = [(1.0 - lb) * jax.nn.sigmoid(-f_raw[b]) for b in seqs]
    all_sums = [_mask_dot(sums, logf[b]) for b in seqs]
    big_g, g_rel, g_tot = ([t[i * n:(i + 1) * n] for t in all_sums] for i in range(3))
    q_rel = [q[b] * jnp.exp(g_rel[b]) for b in seqs]
    k_rel = [k[b] * jnp.exp(-g_rel[b]) for b in seqs]
    qg = [q[b] * jnp.exp(big_g[b]) for b in seqs]
    k_end = [k[b] * jnp.exp(g_tot[b] - big_g[b]) for b in seqs]
    keep = [[jnp.exp(g_tot[b][j * c:j * c + 1]) for j in chunks] for b in seqs]
    scores = [_bdot_nt(hs(q_rel[b], h), hs(k_rel[b], h)) for b, h in items]
    o_intra = [_bdot(jnp.where(causal, scores[i], 0.0), hs(i_raw[b], h)) for i, (b, h) in enumerate(items)]
    grow = [[_bdot_tn(rows(hs(i_raw[b], h), j), rows(hs(k_end[b], h), j)) for j in chunks] for b, h in items]
    entering, new_states = [], []
    for i, (b, h) in enumerate(items):
        st, per_chunk = states[i], []
        for j in chunks:
            per_chunk.append(st)
            st = st * hs(keep[b][j], h) + grow[i][j]
        entering.append(per_chunk)
        new_states.append(st)
    o_inter = [[_bdot_nt(rows(hs(qg[b], h), j), entering[i][j]) for j in chunks] for i, (b, h) in enumerate(items)]
    outs = [_rms(o_intra[i] + jnp.concatenate(o_inter[i], axis=0), nw) * _silu(hs(g_raw[b], h))
            for i, (b, h) in enumerate(items)]
    return new_states, [jnp.concatenate(outs[b * HGRN_HEADS:(b + 1) * HGRN_HEADS], axis=1) for b in seqs]


def _hgrn_fwd(p, lb, nw, name):
    bsz, s, _ = p.shape
    n = HGRN_BLOCK
    nb = s // n

    items = _seq_items(bsz, HGRN_HEADS)

    def body(q_ref, f_ref, i_ref, g_ref, lb_ref, nw_ref, o_ref, st_ref, st_scr):
        @pl.when(pl.program_id(0) == 0)
        def _():
            st_scr[...] = jnp.zeros_like(st_scr)

        st_ref[...] = st_scr[...]
        per_seq = lambda ref: [ref[b] for b in range(bsz)]
        new_states, o = _hgrn_block([st_scr[b, h] for b, h in items], per_seq(q_ref), per_seq(f_ref), per_seq(i_ref),
                                    per_seq(g_ref), lb_ref[...], nw_ref[...])
        for i, (b, h) in enumerate(items):
            st_scr[b, h] = new_states[i]
        for b in range(bsz):
            o_ref[b] = o[b].astype(BF16)

    row = lambda blk: pl.BlockSpec((bsz, n, 512), lambda i, blk=blk: (0, i, blk))
    return _call(
        body, name, (nb,),
        [row(P_HQ // 512), row(P_HF // 512), row(P_HI // 512), row(P_HG // 512),
         pl.BlockSpec((1, 512), lambda i: (0, 0)), pl.BlockSpec((1, LANES), lambda i: (0, 0))],
        [row(0), pl.BlockSpec((bsz, None, 4, LANES, LANES), lambda i: (0, i, 0, 0, 0))],
        [_sds((bsz, s, 512), BF16), _sds((bsz, nb, 4, LANES, LANES), F32)],
        scratch=[pltpu.VMEM((bsz, 4, LANES, LANES), F32)],
    )(p, p, p, p, lb, nw)


def _hgrn_bwd(do, p, lb, nw, st_all, name):
    bsz, s, _ = p.shape
    n = HGRN_BLOCK
    nb = s // n

    items = _seq_items(bsz, HGRN_HEADS)

    def body(q_ref, f_ref, i_ref, g_ref, lb_ref, nw_ref, do_ref, st_ref, dp_ref, dlb_ref, dnw_ref, ds_scr):
        @pl.when(pl.program_id(0) == 0)
        def _():
            ds_scr[...] = jnp.zeros_like(ds_scr)
            dlb_ref[...] = jnp.zeros_like(dlb_ref)
            dnw_ref[...] = jnp.zeros_like(dnw_ref)

        per_seq = lambda ref: [ref[b] for b in range(bsz)]
        _, vjp = jax.vjp(_hgrn_block, [st_ref[b, h] for b, h in items], per_seq(q_ref), per_seq(f_ref), per_seq(i_ref),
                         per_seq(g_ref), lb_ref[...], nw_ref[...])
        d_states, dq, df, di, dg, dlb, dnw = vjp(([ds_scr[b, h] for b, h in items], per_seq(do_ref)))
        for i, (b, h) in enumerate(items):
            ds_scr[b, h] = d_states[i]
        for b in range(bsz):
            for j, t in enumerate((dq, df, di, dg)):
                dp_ref[b, :, j * 512:(j + 1) * 512] = t[b].astype(BF16)
        dlb_ref[...] += dlb
        dnw_ref[...] += dnw

    row = lambda blk: pl.BlockSpec((bsz, n, 512), lambda i, blk=blk: (0, nb - 1 - i, blk))
    return _call(
        body, name, (nb,),
        [row(P_HQ // 512), row(P_HF // 512), row(P_HI // 512), row(P_HG // 512),
         pl.BlockSpec((1, 512), lambda i: (0, 0)), pl.BlockSpec((1, LANES), lambda i: (0, 0)), row(0),
         pl.BlockSpec((bsz, None, 4, LANES, LANES), lambda i: (0, nb - 1 - i, 0, 0, 0))],
        [pl.BlockSpec((bsz, n, 2048), lambda i: (0, nb - 1 - i, 0)),
         pl.BlockSpec((1, 512), lambda i: (0, 0)), pl.BlockSpec((1, LANES), lambda i: (0, 0))],
        [_sds((bsz, s, 2048), BF16), _sds((1, 512), F32), _sds((1, LANES), F32)],
        scratch=[pltpu.VMEM((bsz, 4, LANES, LANES), F32)],
    )(p, p, p, p, lb, nw, do, st_all)


def _ssd_chunk(states, xbc, small, z, a_row, dt_row, d_row, nw):
    c = xbc.shape[0]
    incl = _tril(c)
    dt_all = _softplus(small + dt_row)
    da_all = dt_all * (-jnp.exp(a_row))
    spread = (_iota((LANES, SSD_INNER), 0) == SM_DT + _iota((LANES, SSD_INNER), 1) // SSD_HEAD_DIM).astype(BF16)
    both = _spread_dot(jnp.concatenate([dt_all, da_all], axis=0), spread)
    dt_e, da_e = both[:c], both[c:]
    acs_e = _mask_dot(incl.astype(BF16), da_e)
    last_e = jnp.sum(da_e, axis=0, keepdims=True)
    xs = xbc[:, :SSD_INNER]
    xdt = xs * dt_e
    gw = SSD_GROUPS * SSD_STATE
    lane = _iota((1, LANES), 1)
    pairs, groups = range(4), range(SSD_GROUPS)
    ps = lambda t, j: t[:, j * LANES:(j + 1) * LANES]
    bg = [xbc[:, SSD_INNER + g * SSD_STATE:SSD_INNER + (g + 1) * SSD_STATE] for g in groups]
    cg = [xbc[:, SSD_INNER + gw + g * SSD_STATE:SSD_INNER + gw + (g + 1) * SSD_STATE] for g in groups]
    cb = [_bdot_nt(cg[g], bg[g]) for g in groups]

    def seg(j, sub):
        ac = ps(acs_e, j)[:, sub * SSD_HEAD_DIM:sub * SSD_HEAD_DIM + 1]
        return jnp.where(incl, jnp.exp(jnp.where(incl, ac - _col_to_row(ac), 0.0)), 0.0)

    mine = [((lane // SSD_HEAD_DIM) == sub).astype(F32) for sub in range(2)]
    y_in = [[_bdot(cb[j // 2] * seg(j, sub), ps(xdt, j) * mine[sub]) for sub in range(2)] for j in pairs]
    y_st = [_bdot(cg[j // 2], states[j]) for j in pairs]
    grow = [_bdot_tn(bg[j // 2], ps(xdt, j) * jnp.exp(ps(last_e, j) - ps(acs_e, j))) for j in pairs]
    new_states = [states[j] * jnp.exp(ps(last_e, j)) + grow[j] for j in pairs]
    ys = [y_in[j][0] + y_in[j][1] + y_st[j] * jnp.exp(ps(acs_e, j)) + ps(d_row, j) * ps(xs, j) for j in pairs]
    yz = jnp.concatenate(ys, axis=1) * _silu(z)
    gwid = SSD_INNER // SSD_GROUPS
    outs = [_rms(yz[:, g * gwid:(g + 1) * gwid], nw[:, g * gwid:(g + 1) * gwid]) for g in range(SSD_GROUPS)]
    return new_states, jnp.concatenate(outs, axis=1)


def _ssd_fwd(xbc_act, p, a_row, dt_row, d_row, nw, name):
    bsz, s, _ = xbc_act.shape
    c = SSD_CHUNK
    nc = s // c

    def body(x_ref, sm_ref, z_ref, a_ref, dt_ref, d_ref, nw_ref, o_ref, st_ref, st_scr):
        @pl.when(pl.program_id(1) == 0)
        def _():
            st_scr[...] = jnp.zeros_like(st_scr)

        st_ref[...] = st_scr[...]
        states = [st_scr[h] for h in range(4)]
        new_states, o = _ssd_chunk(states, x_ref[...], sm_ref[...], z_ref[...], a_ref[...], dt_ref[...], d_ref[...], nw_ref[...])
        for h in range(4):
            st_scr[h] = new_states[h]
        o_ref[...] = o.astype(BF16)

    row = lambda w, blk: pl.BlockSpec((None, c, w), lambda b, n, blk=blk: (b, n, blk))
    prm = pl.BlockSpec((1, LANES), lambda b, n: (0, 0))
    prm5 = pl.BlockSpec((1, 512), lambda b, n: (0, 0))
    return _call(
        body, name, (bsz, nc),
        [row(1024, 0), row(LANES, P_SMALL // LANES), row(512, P_SZ // 512), prm, prm, prm5, prm5],
        [row(512, 0), _state_spec()],
        [_sds((bsz, s, 512), BF16), _sds((bsz, nc, 4, LANES, LANES), F32)],
        scratch=[pltpu.VMEM((4, LANES, LANES), F32)],
    )(xbc_act, p, p, a_row, dt_row, d_row, nw)


def _ssd_bwd(do, xbc_act, p, a_row, dt_row, d_row, nw, st_all, name):
    bsz, s, _ = xbc_act.shape
    c = SSD_CHUNK
    nc = s // c

    def body(x_ref, sm_ref, z_ref, a_ref, dt_ref, d_ref, nw_ref, do_ref, st_ref,
             dx_ref, dsm_ref, dz_ref, da_ref, ddt_ref, dd_ref, dnw_ref, ds_scr):
        first = (pl.program_id(0) == 0) & (pl.program_id(1) == 0)

        @pl.when(pl.program_id(1) == 0)
        def _():
            ds_scr[...] = jnp.zeros_like(ds_scr)

        @pl.when(first)
        def _():
            da_ref[...] = jnp.zeros_like(da_ref)
            ddt_ref[...] = jnp.zeros_like(ddt_ref)
            dd_ref[...] = jnp.zeros_like(dd_ref)
            dnw_ref[...] = jnp.zeros_like(dnw_ref)

        states = [st_ref[h] for h in range(4)]
        _, vjp = jax.vjp(_ssd_chunk, states, x_ref[...], sm_ref[...], z_ref[...], a_ref[...], dt_ref[...], d_ref[...], nw_ref[...])
        d_states, dx, dsm, dz, da, ddt, dd, dnw = vjp(([ds_scr[h] for h in range(4)], do_ref[...]))
        for h in range(4):
            ds_scr[h] = d_states[h]
        dx_ref[...] = dx
        dsm_ref[...] = dsm
        dz_ref[...] = dz.astype(BF16)
        da_ref[...] += da
        ddt_ref[...] += ddt
        dd_ref[...] += dd
        dnw_ref[...] += dnw

    row = lambda w, blk: pl.BlockSpec((None, c, w), lambda b, n, blk=blk: (b, nc - 1 - n, blk))
    prm = pl.BlockSpec((1, LANES), lambda b, n: (0, 0))
    prm5 = pl.BlockSpec((1, 512), lambda b, n: (0, 0))
    return _call(
        body, name, (bsz, nc),
        [row(1024, 0), row(LANES, P_SMALL // LANES), row(512, P_SZ // 512), prm, prm, prm5, prm5, row(512, 0),
         pl.BlockSpec((None, None, 4, LANES, LANES), lambda b, n: (b, nc - 1 - n, 0, 0, 0))],
        [row(1024, 0), row(LANES, 0), row(512, 0), prm, prm, prm5, prm5],
        [_sds((bsz, s, 1024), F32), _sds((bsz, s, LANES), F32), _sds((bsz, s, 512), BF16),
         _sds((1, LANES), F32), _sds((1, LANES), F32), _sds((1, 512), F32), _sds((1, 512), F32)],
        scratch=[pltpu.VMEM((4, LANES, LANES), F32)],
    )(xbc_act, p, p, a_row, dt_row, d_row, nw, do, st_all)


def _peer(k):
    x, y, c = lax.axis_index("x"), lax.axis_index("y"), lax.axis_index("c")
    px = 1 - x if k & 4 else x
    py = 1 - y if k & 2 else y
    pc = 1 - c if k & 1 else c
    return (px, py, pc), 4 * px + 2 * py + pc


def _my_index():
    return 4 * lax.axis_index("x") + 2 * lax.axis_index("y") + lax.axis_index("c")


def _exchange(arrays, name, gather):
    n = len(arrays)

    def body(*refs):
        ins, outs = refs[:n], refs[n:2 * n]
        send_sems, recv_sems, local_sems = refs[2 * n:]
        me = _my_index()

        def copy(i, k):
            peer, slot = _peer(k)
            src = ins[i] if gather else ins[i].at[slot]
            return pltpu.make_async_remote_copy(src_ref=src, dst_ref=outs[i].at[me], send_sem=send_sems.at[k - 1, i],
                                                recv_sem=recv_sems.at[k - 1, i], device_id=peer, device_id_type=MESH_ID)

        def arrival(i, k):
            peer, slot = _peer(k)
            src = ins[i] if gather else ins[i].at[slot]
            return pltpu.make_async_remote_copy(src_ref=src, dst_ref=outs[i].at[slot], send_sem=send_sems.at[k - 1, i],
                                                recv_sem=recv_sems.at[k - 1, i], device_id=peer, device_id_type=MESH_ID)

        mine = [pltpu.make_async_copy(ins[i] if gather else ins[i].at[me], outs[i].at[me], local_sems.at[i])
                for i in range(n)]
        sends = [copy(i, k) for k in range(1, N_DEV) for i in range(n)]
        for cp in mine + sends:
            cp.start()
        for k in range(1, N_DEV):
            for i in range(n):
                arrival(i, k).wait_recv()
        for cp in sends:
            cp.wait_send()
        for cp in mine:
            cp.wait()

    out_shape = [_sds(((N_DEV,) + a.shape) if gather else a.shape, a.dtype) for a in arrays]
    any_spec = pl.BlockSpec(memory_space=pl.ANY)
    return pl.pallas_call(
        body, name=name, out_shape=out_shape, in_specs=[any_spec] * n, out_specs=[any_spec] * n,
        scratch_shapes=[pltpu.SemaphoreType.DMA((N_DEV - 1, n)), pltpu.SemaphoreType.DMA((N_DEV - 1, n)),
                        pltpu.SemaphoreType.DMA((n,))],
    )(*arrays)


def _mesh_place():
    x, y, c = lax.axis_index("x"), lax.axis_index("y"), lax.axis_index("c")
    return (x, y, c), (x, y, 1 - c), [(1 - x, y), (x, 1 - y), (1 - x, 1 - y)]


def _run_exchange(body, name, arrays, out_shape, n_sems):
    n = len(arrays)
    any_spec = pl.BlockSpec(memory_space=pl.ANY)
    return pl.pallas_call(
        body, name=name, out_shape=out_shape, in_specs=[any_spec] * n, out_specs=[any_spec] * n,
        scratch_shapes=[pltpu.SemaphoreType.DMA((n_sems, n)), pltpu.SemaphoreType.DMA((n_sems, n)),
                        pltpu.SemaphoreType.DMA((n,))],
    )(*arrays)


def _gather_two_level(arrays, name):
    n = len(arrays)

    def body(*refs):
        ins, outs = refs[:n], refs[n:2 * n]
        send_sems, recv_sems, local_sems = refs[2 * n:]
        (x, y, c), sibling, chips = _mesh_place()
        slot = lambda px, py, pc: 4 * px + 2 * py + pc

        def copy(i, k, block, to, src=None):
            return pltpu.make_async_remote_copy(
                src_ref=outs[i].at[block] if src is None else src, dst_ref=outs[i].at[block],
                send_sem=send_sems.at[k, i], recv_sem=recv_sems.at[k, i], device_id=to, device_id_type=MESH_ID)

        me = slot(x, y, c)
        mine = [pltpu.make_async_copy(ins[i], outs[i].at[me], local_sems.at[i]) for i in range(n)]
        first = [copy(i, 0, me, sibling, src=ins[i]) for i in range(n)]
        first += [copy(i, 1 + j, me, (*chip, c), src=ins[i]) for j, chip in enumerate(chips) for i in range(n)]
        for cp in mine + first:
            cp.start()
        passed = []
        for j, chip in enumerate(chips):
            for i in range(n):
                copy(i, 1 + j, slot(*chip, c), (x, y, c)).wait_recv()
                cp = copy(i, 4 + j, slot(*chip, c), sibling)
                cp.start()
                passed.append(cp)
        for i in range(n):
            copy(i, 0, slot(x, y, 1 - c), (x, y, c)).wait_recv()
        for j, chip in enumerate(chips):
            for i in range(n):
                copy(i, 4 + j, slot(*chip, 1 - c), (x, y, c)).wait_recv()
        for cp in first + passed:
            cp.wait_send()
        for cp in mine:
            cp.wait()

    out_shape = [_sds((N_DEV,) + a.shape, a.dtype) for a in arrays]
    return _run_exchange(body, name, arrays, out_shape, 7)


def _sibling_swap(arrays, name):
    n = len(arrays)

    def body(*refs):
        ins, outs = refs[:n], refs[n:2 * n]
        send_sems, recv_sems, _ = refs[2 * n:]
        (x, y, c), sibling, _ = _mesh_place()
        copies = [pltpu.make_async_remote_copy(
            src_ref=ins[i].at[1 - c], dst_ref=outs[i], send_sem=send_sems.at[0, i], recv_sem=recv_sems.at[0, i],
            device_id=sibling, device_id_type=MESH_ID) for i in range(n)]
        for cp in copies:
            cp.start()
        for cp in copies:
            cp.wait()

    out_shape = [_sds(a.shape[1:], a.dtype) for a in arrays]
    return _run_exchange(body, name, arrays, out_shape, 1)


def _chip_scatter(arrays, name):
    n = len(arrays)

    def body(*refs):
        ins, outs = refs[:n], refs[n:2 * n]
        send_sems, recv_sems, local_sems = refs[2 * n:]
        (x, y, c), _, chips = _mesh_place()
        me = 2 * x + y
        mine = [pltpu.make_async_copy(ins[i].at[me], outs[i].at[me], local_sems.at[i]) for i in range(n)]
        sends = [pltpu.make_async_remote_copy(
            src_ref=ins[i].at[2 * chip[0] + chip[1]], dst_ref=outs[i].at[me], send_sem=send_sems.at[j, i],
            recv_sem=recv_sems.at[j, i], device_id=(*chip, c), device_id_type=MESH_ID)
            for j, chip in enumerate(chips) for i in range(n)]
        for cp in mine + sends:
            cp.start()
        for j, chip in enumerate(chips):
            for i in range(n):
                pltpu.make_async_remote_copy(
                    src_ref=ins[i].at[me], dst_ref=outs[i].at[2 * chip[0] + chip[1]], send_sem=send_sems.at[j, i],
                    recv_sem=recv_sems.at[j, i], device_id=(*chip, c), device_id_type=MESH_ID).wait_recv()
        for cp in sends:
            cp.wait_send()
        for cp in mine:
            cp.wait()

    out_shape = [_sds(a.shape, a.dtype) for a in arrays]
    return _run_exchange(body, name, arrays, out_shape, 3)


def _pair_sum(a, b, name):
    lead, rows, width = a.shape
    tr = _pick(rows, (256, 128, 64, 32, 16, 8)) if rows % 8 == 0 else rows

    def body(a_ref, b_ref, o_ref):
        o_ref[...] = (a_ref[...].astype(F32) + b_ref[...].astype(F32)).astype(o_ref.dtype)

    blk = pl.BlockSpec((None, tr, width), lambda l, i: (l, i, 0))
    return _call(body, name, (lead, rows // tr), [blk, blk], blk, _sds(a.shape, a.dtype))(a, b)


def _sum_adamw(gs, w, m, v, name):
    rows, width = w.shape
    slots = gs.shape[0]
    tr = _pick(rows, (128, 64, 32, 16, 8)) if rows % 8 == 0 else rows

    def body(g_ref, w_ref, m_ref, v_ref, go_ref, d_ref, mo_ref, vo_ref):
        g = g_ref[0].astype(F32)
        for i in range(1, slots):
            g = g + g_ref[i].astype(F32)
        m2 = ADAM_B1 * m_ref[...] + (1.0 - ADAM_B1) * g
        v2 = ADAM_B2 * v_ref[...] + (1.0 - ADAM_B2) * (g * g)
        m_hat = m2 / (1.0 - ADAM_B1 ** ADAM_STEP)
        v_hat = v2 / (1.0 - ADAM_B2 ** ADAM_STEP)
        go_ref[...] = g
        d_ref[...] = -ADAM_LR * (m_hat / (jnp.sqrt(v_hat) + ADAM_EPS) + ADAM_WD * w_ref[...])
        mo_ref[...] = m2
        vo_ref[...] = v2

    flat = pl.BlockSpec((tr, width), lambda i: (i, 0))
    return _call(body, name, (rows // tr,), [pl.BlockSpec((slots, tr, width), lambda i: (0, i, 0)), flat, flat, flat],
                 [flat] * 4, [_sds(w.shape, F32)] * 4)(gs, w, m, v)


MATMUL_WEIGHTS = ("w_in", "w_br_a", "w_br_b", "w_br_c", "w_out", "ffn_w_up", "ffn_w_down")
SPLIT = (
    ("w_in", (DEPTH, D_MODEL, 8720), 2),
    ("gdn_conv_w", (DEPTH, 4, 1536), 2), ("ssd_conv_w", (DEPTH, 4, 1024), 2),
    ("w_br_a", (DEPTH, 512, D_MODEL), 2), ("w_br_b", (DEPTH, 512, D_MODEL), 2), ("w_br_c", (DEPTH, 512, D_MODEL), 2),
    ("w_out", (DEPTH, D_MODEL, D_MODEL), 1), ("ffn_w_up", (DEPTH, D_MODEL, 2 * FFN_HIDDEN), 2),
    ("ffn_conv_w", (DEPTH, 3, 2 * FFN_HIDDEN), 2), ("ffn_w_down", (DEPTH, FFN_HIDDEN, D_MODEL), 1),
)
REPL = (
    ("b_ada", (DEPTH, 6 * D_MODEL)), ("norm1_w", (DEPTH, D_MODEL)), ("gdn_a_log", (DEPTH, 4)),
    ("gdn_dt_bias", (DEPTH, 4)), ("gdn_norm_w", (DEPTH, 128)), ("hgrn_lb_param", (DEPTH, 512)),
    ("hgrn_norm_w", (DEPTH, 128)), ("ssd_conv_b", (DEPTH, 1024)), ("ssd_a_log", (DEPTH, 8)),
    ("ssd_dt_bias", (DEPTH, 8)), ("ssd_d", (DEPTH, 8)), ("ssd_norm_w", (DEPTH, 512)), ("norm2_w", (DEPTH, D_MODEL)),
    ("ffn_conv_b", (DEPTH, 2 * FFN_HIDDEN)), ("final_norm_w", (D_MODEL,)),
)
WEIGHTS = ("w_ada", "b_ada", "norm1_w", "w_in", "gdn_conv_w", "gdn_a_log", "gdn_dt_bias", "gdn_norm_w",
           "hgrn_lb_param", "hgrn_norm_w", "ssd_conv_w", "ssd_conv_b", "ssd_a_log", "ssd_dt_bias", "ssd_d",
           "ssd_norm_w", "w_br_a", "w_br_b", "w_br_c", "w_out", "norm2_w", "ffn_w_up", "ffn_conv_w", "ffn_conv_b",
           "ffn_w_down", "final_norm_w")


def _block_shape(shape, axis):
    return tuple(d // N_DEV if i == axis else d for i, d in enumerate(shape))


def _join_blocks(gathered, shape, axis):
    return jnp.moveaxis(gathered, 0, axis).reshape(shape)


def _split_blocks(full, shape, axis):
    bs = _block_shape(shape, axis)
    t = full.reshape(shape[:axis] + (N_DEV, bs[axis]) + shape[axis + 1:])
    return jnp.moveaxis(t, axis, 0)


def _pack_repl(vals):
    parts = []
    for n, shape in REPL:
        size = math.prod(shape)
        parts.append(jnp.pad(vals[n].reshape(-1), (0, -(-size // PACK_W) * PACK_W - size)))
    cat = jnp.concatenate(parts)
    rows = -(-cat.shape[0] // (8 * PACK_W)) * 8
    return jnp.pad(cat, (0, rows * PACK_W - cat.shape[0])).reshape(rows, PACK_W)


def _unpack_repl(packed):
    flat, out, off = packed.reshape(-1), {}, 0
    for n, shape in REPL:
        size = math.prod(shape)
        out[n] = flat[off:off + size].reshape(shape)
        off += -(-size // PACK_W) * PACK_W
    return out


def _lane_row(vec, lane0):
    return jnp.pad(vec, (lane0, LANES - lane0 - vec.shape[0]))[None]


def _arrange_w_in(w):
    offs = [0]
    for sz in W_IN_SPLITS:
        offs.append(offs[-1] + sz)
    qkv, a, b, gz, hq, hf, hi, hg, sz_, xbc, dt, gate = [w[:, offs[i]:offs[i + 1]] for i in range(12)]
    pad = jnp.zeros((w.shape[0], P_WIDTH - P_SMALL - 16), w.dtype)
    return jnp.concatenate([qkv, gz, xbc, gate, hq, hf, hi, hg, sz_, a, b, dt, pad], axis=1)


def _restore_w_in(wp):
    cut = lambda o, n: wp[:, o:o + n]
    return jnp.concatenate([
        cut(P_QKV, 1536), cut(P_SMALL + SM_A, 4), cut(P_SMALL + SM_B, 4), cut(P_GZ, 512), cut(P_HQ, 512),
        cut(P_HF, 512), cut(P_HI, 512), cut(P_HG, 512), cut(P_SZ, 512), cut(P_XBC, 1024), cut(P_SMALL + SM_DT, 8),
        cut(P_GATE, 3072)], axis=1)


def _layer_consts(l, wf, wr, lower):
    t = lambda a: a.T
    k = {}
    k["n1w"], k["n2w"] = wr["norm1_w"][l][None], wr["norm2_w"][l][None]
    win = _arrange_w_in(wf["w_in"][l])
    k["win"], k["win_t"] = win, t(win)
    for n in ("w_br_a", "w_br_b", "w_br_c", "w_out", "ffn_w_up", "ffn_w_down"):
        k[n], k[n + "_t"] = wf[n][l], t(wf[n][l])
    k["gdn_conv_w"], k["gdn_conv_b"] = wf["gdn_conv_w"][l], jnp.zeros((1, 1536), F32)
    k["ssd_conv_w"], k["ssd_conv_b"] = wf["ssd_conv_w"][l], wr["ssd_conv_b"][l][None]
    k["ffn_conv_w"], k["ffn_conv_b"] = wf["ffn_conv_w"][l], wr["ffn_conv_b"][l][None]
    k["gdn_a"], k["gdn_dt"] = _lane_row(wr["gdn_a_log"][l], SM_A), _lane_row(wr["gdn_dt_bias"][l], SM_A)
    k["gdn_nw"], k["hgrn_nw"] = wr["gdn_norm_w"][l][None], wr["hgrn_norm_w"][l][None]
    k["ssd_a"], k["ssd_dt"] = _lane_row(wr["ssd_a_log"][l], SM_DT), _lane_row(wr["ssd_dt_bias"][l], SM_DT)
    k["ssd_d"] = jnp.repeat(wr["ssd_d"][l], SSD_HEAD_DIM)[None]
    k["ssd_nw"] = wr["ssd_norm_w"][l][None]
    k["lb"] = lower[l:l + 1]
    return k


def _layer_fwd(l, x, mod, k):
    bsz, s, d = x.shape
    t = bsz * s
    sv = {"x": x}
    sv["mod"] = [mod[:, None, i * d:(i + 1) * d] for i in range(6)]
    sh1, sc1, g1, sh2, sc2, g2 = sv["mod"]
    h1 = _norm_mod_fwd(x, k["n1w"], sh1, sc1, f"norm1_fwd{l}")
    p = _mm(h1.reshape(t, d), k["win"], F32, f"mm_in{l}").reshape(bsz, s, P_WIDTH)
    qkv_act = _conv_fwd(p, P_QKV, 1536, k["gdn_conv_w"], k["gdn_conv_b"], True, f"gdn_conv_fwd{l}")
    oa, st_a, ti_a = _gdn_fwd(qkv_act, p, k["gdn_a"], k["gdn_dt"], k["gdn_nw"], f"gdn_fwd{l}")
    ob, st_b = _hgrn_fwd(p, k["lb"], k["hgrn_nw"], f"hgrn_fwd{l}")
    xbc_act = _conv_fwd(p, P_XBC, 1024, k["ssd_conv_w"], k["ssd_conv_b"], True, f"ssd_conv_fwd{l}")
    oc, st_c = _ssd_fwd(xbc_act, p, k["ssd_a"], k["ssd_dt"], k["ssd_d"], k["ssd_nw"], f"ssd_fwd{l}")
    merged = _merge_fwd(p, oa, ob, oc, k["w_br_a"], k["w_br_b"], k["w_br_c"], f"merge_fwd{l}")
    mix = _mm(merged.reshape(t, d), k["w_out"], F32, f"mm_out{l}").reshape(bsz, s, d)
    x1 = _resid_fwd(x, mix, g1, f"resid1_fwd{l}")
    h2 = _norm_mod_fwd(x1, k["n2w"], sh2, sc2, f"norm2_fwd{l}")
    u_pre = _mm(h2.reshape(t, d), k["ffn_w_up"], F32, f"mm_up{l}").reshape(bsz, s, 2 * FFN_HIDDEN)
    a = _conv_glu_fwd(u_pre, k["ffn_conv_w"], k["ffn_conv_b"], f"ffn_conv_glu_fwd{l}")
    ffn = _mm(a.reshape(t, FFN_HIDDEN), k["ffn_w_down"], F32, f"mm_down{l}").reshape(bsz, s, d)
    x2 = _resid_fwd(x1, ffn, g2, f"resid2_fwd{l}")
    sv.update(h1=h1, p=p, qkv_act=qkv_act, oa=oa, st_a=st_a, ti_a=ti_a, ob=ob, st_b=st_b, xbc_act=xbc_act, oc=oc, st_c=st_c,
              merged=merged, mix=mix, x1=x1, h2=h2, u_pre=u_pre, a=a, ffn=ffn)
    return x2, sv


def _layer_bwd(l, dx2, k, sv):
    bsz, s, d = dx2.shape
    t = bsz * s
    f2 = 2 * FFN_HIDDEN
    sh1, sc1, g1, sh2, sc2, g2 = sv["mod"]
    tr = lambda a: a.reshape(t, -1).T
    g = {}
    dffn, dg2 = _gate_bwd(dx2, sv["ffn"], g2, f"gate2_bwd{l}")
    dffn2 = dffn.reshape(t, d)
    da = _mm(dffn2, k["ffn_w_down_t"], F32, f"mm_down_dx{l}").reshape(bsz, s, FFN_HIDDEN)
    g["ffn_w_down"] = _mm(tr(sv["a"]), dffn2, BF16, f"mm_down_dw{l}")
    du_pre, g["ffn_conv_w"], dfcb = _conv_glu_bwd(da, sv["u_pre"], k["ffn_conv_w"], k["ffn_conv_b"], f"ffn_conv_glu_bwd{l}")
    g["ffn_conv_b"] = dfcb[0]
    du2 = du_pre.reshape(t, f2)
    dh2 = _mm(du2, k["ffn_w_up_t"], F32, f"mm_up_dx{l}").reshape(bsz, s, d)
    g["ffn_w_up"] = _mm(tr(sv["h2"]), du2, BF16, f"mm_up_dw{l}")
    dx1, dn2w, dsh2, dsc2 = _norm_mod_bwd(sv["x1"], k["n2w"], sh2, sc2, dh2, dx2, f"norm2_bwd{l}")
    g["norm2_w"] = dn2w[0]
    dmix, dg1 = _gate_bwd(dx1, sv["mix"], g1, f"gate1_bwd{l}")
    dmix2 = dmix.reshape(t, d)
    dmerged = _mm(dmix2, k["w_out_t"], F32, f"mm_out_dx{l}").reshape(bsz, s, d)
    g["w_out"] = _mm(tr(sv["merged"]), dmix2, BF16, f"mm_out_dw{l}")
    p = sv["p"]
    dgate, doa, dob, doc, dya, dyb, dyc = _merge_bwd(
        dmerged, p, sv["oa"], sv["ob"], sv["oc"], k["w_br_a"], k["w_br_b"], k["w_br_c"],
        k["w_br_a_t"], k["w_br_b_t"], k["w_br_c_t"], f"merge_bwd{l}")
    g["w_br_a"] = _mm(tr(sv["oa"]), dya.reshape(t, d), BF16, f"mm_bra_dw{l}")
    g["w_br_b"] = _mm(tr(sv["ob"]), dyb.reshape(t, d), BF16, f"mm_brb_dw{l}")
    g["w_br_c"] = _mm(tr(sv["oc"]), dyc.reshape(t, d), BF16, f"mm_brc_dw{l}")
    dxbc_act, dsm_c, dsz, da_c, ddt_c, dd_c, dnw_c = _ssd_bwd(
        doc, sv["xbc_act"], p, k["ssd_a"], k["ssd_dt"], k["ssd_d"], k["ssd_nw"], sv["st_c"], f"ssd_bwd{l}")
    dxbc_raw, g["ssd_conv_w"], dscb = _conv_bwd(dxbc_act, p, P_XBC, 1024, k["ssd_conv_w"], k["ssd_conv_b"], True, f"ssd_conv_bwd{l}")
    g["ssd_conv_b"] = dscb[0]
    g["ssd_a_log"], g["ssd_dt_bias"] = da_c[0, SM_DT:SM_DT + 8], ddt_c[0, SM_DT:SM_DT + 8]
    g["ssd_d"] = dd_c.reshape(SSD_HEADS, SSD_HEAD_DIM).sum(axis=1)
    g["ssd_norm_w"] = dnw_c[0]
    dhg, dlb, dnw_b = _hgrn_bwd(dob, p, k["lb"], k["hgrn_nw"], sv["st_b"], f"hgrn_bwd{l}")
    g["hgrn_norm_w"] = dnw_b[0]
    dqkv_act, dsm_a, dgz, da_a, ddt_a, dnw_a = _gdn_bwd(
        doa, sv["qkv_act"], p, k["gdn_a"], k["gdn_dt"], k["gdn_nw"], sv["st_a"], sv["ti_a"], f"gdn_bwd{l}")
    dqkv_raw, g["gdn_conv_w"], _ = _conv_bwd(dqkv_act, p, P_QKV, 1536, k["gdn_conv_w"], k["gdn_conv_b"], True, f"gdn_conv_bwd{l}")
    g["gdn_a_log"], g["gdn_dt_bias"], g["gdn_norm_w"] = da_a[0, :4], ddt_a[0, :4], dnw_a[0]
    dsmall = jnp.pad((dsm_a + dsm_c).astype(BF16), ((0, 0), (0, 0), (0, P_WIDTH - P_SMALL - LANES)))
    dp = jnp.concatenate([dqkv_raw, dgz, dxbc_raw, dgate, dhg, dsz, dsmall], axis=-1).reshape(t, P_WIDTH)
    dh1 = _mm(dp, k["win_t"], F32, f"mm_in_dx{l}").reshape(bsz, s, d)
    g["w_in"] = _restore_w_in(_mm(tr(sv["h1"]), dp, BF16, f"mm_in_dw{l}"))
    dx, dn1w, dsh1, dsc1 = _norm_mod_bwd(sv["x"], k["n1w"], sh1, sc1, dh1, dx1, f"norm1_bwd{l}")
    g["norm1_w"] = dn1w[0]
    dmod = jnp.concatenate([dsh1, dsc1, dg1, dsh2, dsc2, dg2], axis=-1)[:, 0]
    return dx, g, dlb, dmod


def _local_step(x, mod, wf, wr, target):
    lower = _lb_fwd(wr["hgrn_lb_param"])
    ks = [_layer_consts(l, wf, wr, lower) for l in range(DEPTH)]
    saved = []
    h = x
    for l in range(DEPTH):
        h, sv = _layer_fwd(l, h, mod[l], ks[l])
        saved.append(sv)
    loss8, dh, dfnw = _final_loss(h, wr["final_norm_w"][None], target)
    per_layer, dlbs, dmods = [None] * DEPTH, [None] * DEPTH, [None] * DEPTH
    for l in reversed(range(DEPTH)):
        dh, per_layer[l], dlbs[l], dmods[l] = _layer_bwd(l, dh, ks[l], saved[l])
    grads = {n: jnp.stack([per_layer[l][n] for l in range(DEPTH)]) for n in per_layer[0]}
    grads["hgrn_lb_param"] = _lb_bwd(wr["hgrn_lb_param"], jnp.concatenate(dlbs, axis=0))
    grads["final_norm_w"] = dfnw[0]
    return loss8[0, 0], dh, grads, jnp.stack(dmods)


def kernel(x, c, w_ada, b_ada, norm1_w, w_in, gdn_conv_w, gdn_a_log, gdn_dt_bias, gdn_norm_w, hgrn_lb_param, hgrn_norm_w, ssd_conv_w, ssd_conv_b, ssd_a_log, ssd_dt_bias, ssd_d, ssd_norm_w, w_br_a, w_br_b, w_br_c, w_out, norm2_w, ffn_w_up, ffn_conv_w, ffn_conv_b, ffn_w_down, final_norm_w, loss_target, m_w_ada, m_b_ada, m_norm1_w, m_w_in, m_gdn_conv_w, m_gdn_a_log, m_gdn_dt_bias, m_gdn_norm_w, m_hgrn_lb_param, m_hgrn_norm_w, m_ssd_conv_w, m_ssd_conv_b, m_ssd_a_log, m_ssd_dt_bias, m_ssd_d, m_ssd_norm_w, m_w_br_a, m_w_br_b, m_w_br_c, m_w_out, m_norm2_w, m_ffn_w_up, m_ffn_conv_w, m_ffn_conv_b, m_ffn_w_down, m_final_norm_w, v_w_ada, v_b_ada, v_norm1_w, v_w_in, v_gdn_conv_w, v_gdn_a_log, v_gdn_dt_bias, v_gdn_norm_w, v_hgrn_lb_param, v_hgrn_norm_w, v_ssd_conv_w, v_ssd_conv_b, v_ssd_a_log, v_ssd_dt_bias, v_ssd_d, v_ssd_norm_w, v_w_br_a, v_w_br_b, v_w_br_c, v_w_out, v_norm2_w, v_ffn_w_up, v_ffn_conv_w, v_ffn_conv_b, v_ffn_w_down, v_final_norm_w):
    given = dict(locals())
    w = {n: given[n] for n in WEIGHTS}
    m = {n: given["m_" + n] for n in WEIGHTS}
    v = {n: given["v_" + n] for n in WEIGHTS}
    me = _my_index()
    bsz = c.shape[0]
    ncol = 6 * D_MODEL // N_DEV

    shards = [w[n].astype(BF16) if n in MATMUL_WEIGHTS else w[n] for n, _, _ in SPLIT] + [c]
    gathered = _gather_two_level(shards, "gather_weights")
    wf = {n: _join_blocks(g, shape, axis) for (n, shape, axis), g in zip(SPLIT, gathered)}
    c_all = gathered[-1].reshape(N_DEV * bsz, D_MODEL)

    b_cols = lax.dynamic_slice_in_dim(b_ada, me * ncol, ncol, axis=1)[:, None]
    mod_cols = _ada_fwd(c_all, w_ada, b_cols)
    send = mod_cols.reshape(DEPTH, N_DEV, bsz, ncol).transpose(1, 0, 2, 3)
    got = _exchange([send], "scatter_mod", False)[0]
    mod = got.transpose(1, 2, 0, 3).reshape(DEPTH, bsz, 6 * D_MODEL)

    loss, dx, grads, dmod = _local_step(x, mod, wf, w, loss_target)

    send = dmod.reshape(DEPTH, bsz, N_DEV, ncol).transpose(2, 0, 1, 3)
    got_dmod = _exchange([send], "scatter_dmod", False)[0]
    dmod_all = got_dmod.transpose(1, 0, 2, 3).reshape(DEPTH, N_DEV * bsz, ncol)
    g_w_ada, g_b_cols = _ada_bwd(c_all.T, dmod_all)

    core = lax.axis_index("c")
    by_core = []
    for n, shape, axis in SPLIT:
        parts = _split_blocks(grads[n], shape, axis).astype(BF16)
        parts = parts.reshape((N_DEV // 2, 2, -1, parts.shape[-1]))
        by_core.append(jnp.swapaxes(parts, 0, 1))
    from_sibling = _sibling_swap(by_core, "swap_grads")
    sums = [_pair_sum(lax.dynamic_index_in_dim(mine, core, 0, keepdims=False), theirs, f"pair_sum_{n}")
            for (n, _, _), mine, theirs in zip(SPLIT, by_core, from_sibling)]
    got = _chip_scatter(sums, "scatter_grads")
    grads["b_ada"] = lax.dynamic_update_slice_in_dim(jnp.zeros_like(b_ada), g_b_cols[:, 0], me * ncol, axis=1)

    out = {}
    slots = [(n, g8) for (n, _, _), g8 in zip(SPLIT, got)] + [("w_ada", g_w_ada[None])]
    for n, gs in slots:
        bs = w[n].shape
        two = lambda a: a.reshape(-1, bs[-1])
        res = _sum_adamw(gs.reshape(gs.shape[0], -1, bs[-1]), two(w[n]), two(m[n]), two(v[n]), f"adamw_{n}")
        out[n] = [r.reshape(bs) for r in res]
    r8 = _gather_two_level([_pack_repl(grads)], "gather_small_grads")[0]
    res = _sum_adamw(r8, _pack_repl(w), _pack_repl(m), _pack_repl(v), "adamw_repl")
    repl_out = [_unpack_repl(o) for o in res]
    pick = lambda i, n: out[n][i] if n in out else repl_out[i][n]
    loss = lax.psum(loss, ("x", "y", "c"))
    return (loss, dx, *[pick(i, n) for i in range(4) for n in WEIGHTS])
```

```python
import functools
import math

import jax
import jax.numpy as jnp
from jax import lax
from jax.experimental import pallas as pl
from jax.experimental.pallas import tpu as pltpu

F32, BF16 = jnp.float32, jnp.bfloat16
HI = lax.Precision.HIGHEST
MESH_ID = pl.DeviceIdType.MESH

N_DEV = 8
EPS = 1e-6
D_MODEL = 1024
DEPTH = 2
GDN_HEADS, GDN_DK, GDN_CHUNK = 4, 128, 64
HGRN_HEADS, HGRN_DK, HGRN_CHUNK, HGRN_BLOCK = 4, 128, 16, 128
SSD_HEADS, SSD_HEAD_DIM, SSD_GROUPS, SSD_STATE, SSD_CHUNK = 8, 64, 2, 128, 64
SSD_INNER = SSD_HEADS * SSD_HEAD_DIM
FFN_HIDDEN = 2816
LANES = 128
P_QKV, P_GZ, P_XBC, P_GATE, P_HQ, P_HF, P_HI, P_HG, P_SZ, P_SMALL, P_WIDTH = (
    0, 1536, 2048, 3072, 6144, 6656, 7168, 7680, 8192, 8704, 9216)
SM_A, SM_B, SM_DT = 0, 4, 8
W_IN_SPLITS = (1536, 4, 4, 512, 512, 512, 512, 512, 512, 1024, 8, 3072)

ADAM_LR, ADAM_B1, ADAM_B2, ADAM_EPS, ADAM_WD, ADAM_STEP = 0.001, 0.9, 0.999, 1e-08, 0.01, 10

V7X_VMEM_LIMIT = 56 * 1024 * 1024
PACK_W = 1024
PACK_ROWS = 128


def _call(body, name, grid, in_specs, out_specs, out_shape, scratch=()):
    return pl.pallas_call(
        body, name=name, grid=grid, in_specs=in_specs, out_specs=out_specs, out_shape=out_shape,
        scratch_shapes=list(scratch),
        compiler_params=pltpu.CompilerParams(
            dimension_semantics=("arbitrary",) * len(grid), vmem_limit_bytes=V7X_VMEM_LIMIT),
    )


def _pick(n, cands):
    for c in cands:
        if n % c == 0:
            return c
    raise ValueError(f"no tile for {n} among {cands}")


def _sds(shape, dtype):
    return jax.ShapeDtypeStruct(shape, dtype)


def _dot(a, b):
    return lax.dot_general(a, b, (((1,), (0,)), ((), ())), precision=HI, preferred_element_type=F32)


NN, NT, TN = (((1,), (0,)), ((), ())), (((1,), (1,)), ((), ())), (((0,), (0,)), ((), ()))


def _mxu(a, b, dims):
    return lax.dot_general(a.astype(BF16), b.astype(BF16), dims, preferred_element_type=F32)


@jax.custom_vjp
def _bdot(a, b):
    return _mxu(a, b, NN)


@jax.custom_vjp
def _bdot_nt(a, b):
    return _mxu(a, b, NT)


@jax.custom_vjp
def _bdot_tn(a, b):
    return _mxu(a, b, TN)


_bdot.defvjp(lambda a, b: (_mxu(a, b, NN), (a, b)), lambda r, d: (_mxu(d, r[1], NT), _mxu(r[0], d, TN)))
_bdot_nt.defvjp(lambda a, b: (_mxu(a, b, NT), (a, b)), lambda r, d: (_mxu(d, r[1], NN), _mxu(d, r[0], TN)))
_bdot_tn.defvjp(lambda a, b: (_mxu(a, b, TN), (a, b)), lambda r, d: (_mxu(r[1], d, NT), _mxu(r[0], d, NN)))


def _split(x):
    hi = x.astype(BF16)
    return hi, (x - hi.astype(F32)).astype(BF16)


def _mxu3(a, b, dims):
    ah, al = _split(a)
    bh, bl = _split(b)
    return _mxu(ah, bh, dims) + (_mxu(ah, bl, dims) + _mxu(al, bh, dims))


@jax.custom_vjp
def _dot3(a, b):
    return _mxu3(a, b, NN)


_dot3.defvjp(lambda a, b: (_mxu3(a, b, NN), (a, b)), lambda r, d: (_mxu3(d, r[1], NT), _mxu3(r[0], d, TN)))


def _pieces(x):
    x1 = x.astype(BF16)
    r1 = x - x1.astype(F32)
    x2 = r1.astype(BF16)
    return x1, x2, (r1 - x2.astype(F32)).astype(BF16)


def _mask_mxu(mask, x, dims):
    x1, x2, x3 = _pieces(x)
    return _mxu(mask, x1, dims) + (_mxu(mask, x2, dims) + _mxu(mask, x3, dims))


def _spread_mxu(x, mask, dims):
    x1, x2, x3 = _pieces(x)
    return _mxu(x1, mask, dims) + (_mxu(x2, mask, dims) + _mxu(x3, mask, dims))


@jax.custom_vjp
def _mask_dot(mask, x):
    return _mask_mxu(mask, x, NN)


@jax.custom_vjp
def _spread_dot(x, mask):
    return _spread_mxu(x, mask, NN)


_mask_dot.defvjp(lambda m, x: (_mask_mxu(m, x, NN), m), lambda m, d: (jnp.zeros_like(m), _mask_mxu(m, d, TN)))
_spread_dot.defvjp(lambda x, m: (_spread_mxu(x, m, NN), m), lambda m, d: (_spread_mxu(d, m, NT), jnp.zeros_like(m)))


def _iota(shape, axis):
    return lax.broadcasted_iota(jnp.int32, shape, axis)


def _silu(x):
    return x * jax.nn.sigmoid(x)


def _softplus(x):
    return jnp.maximum(x, 0.0) + jnp.log1p(jnp.exp(-jnp.abs(x)))


def _rms(x, w):
    return x * lax.rsqrt(jnp.mean(x * x, axis=-1, keepdims=True) + EPS) * w


def _lane_col(x, lane):
    m = (_iota(x.shape, 1) == lane).astype(F32)
    return jnp.sum(x * m, axis=1, keepdims=True)


def _col_to_row(c):
    n = c.shape[0]
    eye = (_iota((n, n), 0) == _iota((n, n), 1)).astype(F32)
    return jnp.sum(c * eye, axis=0, keepdims=True)


def _tril(n, strict=False):
    r, c = _iota((n, n), 0), _iota((n, n), 1)
    return (r > c) if strict else (r >= c)


def _mm(a, b, out_dtype, name):
    m, k = a.shape
    n = b.shape[1]
    tm = _pick(m, (1024, 1408, 512, 256, 128, 64, 32, 16, 8))
    tn = _pick(n, (1024, 1408, 768, 512, 384, 256, 128))
    tk = k if k <= 3072 else _pick(k, (1024, 768, 512, 384, 256, 128))
    nk = k // tk

    def body_one(a_ref, b_ref, o_ref):
        o_ref[...] = _bdot(a_ref[...], b_ref[...]).astype(out_dtype)

    def body(a_ref, b_ref, o_ref, acc_ref):
        kk = pl.program_id(2)

        @pl.when(kk == 0)
        def _():
            acc_ref[...] = jnp.zeros_like(acc_ref)

        acc_ref[...] += _bdot(a_ref[...], b_ref[...])

        @pl.when(kk == nk - 1)
        def _():
            o_ref[...] = acc_ref[...].astype(out_dtype)

    return _call(
        body_one if nk == 1 else body, name, (m // tm, n // tn, nk),
        [pl.BlockSpec((tm, tk), lambda i, j, kk: (i, kk)), pl.BlockSpec((tk, tn), lambda i, j, kk: (kk, j))],
        pl.BlockSpec((tm, tn), lambda i, j, kk: (i, j)), _sds((m, n), out_dtype),
        scratch=[] if nk == 1 else [pltpu.VMEM((tm, tn), F32)],
    )(a, b)


def _ada_fwd(c_all, w, b):
    depth, _, n = w.shape
    rows = c_all.shape[0]

    def body(c_ref, w_ref, b_ref, o_ref):
        o_ref[...] = _dot(_silu(c_ref[...]), w_ref[...]) + b_ref[...]

    return _call(
        body, "ada_fwd", (depth,),
        [pl.BlockSpec((rows, D_MODEL), lambda l: (0, 0)), pl.BlockSpec((None, D_MODEL, n), lambda l: (l, 0, 0)),
         pl.BlockSpec((None, 1, n), lambda l: (l, 0, 0))],
        pl.BlockSpec((None, rows, n), lambda l: (l, 0, 0)), _sds((depth, rows, n), F32),
    )(c_all, w, b)


def _ada_bwd(c_all_t, dmod):
    depth, rows, n = dmod.shape

    def body(ct_ref, dm_ref, dw_ref, db_ref):
        dm = dm_ref[...]
        dw_ref[...] = _dot(_silu(ct_ref[...]), dm)
        db_ref[...] = jnp.sum(dm, axis=0, keepdims=True)

    return _call(
        body, "ada_bwd", (depth,),
        [pl.BlockSpec((D_MODEL, rows), lambda l: (0, 0)), pl.BlockSpec((None, rows, n), lambda l: (l, 0, 0))],
        [pl.BlockSpec((None, D_MODEL, n), lambda l: (l, 0, 0)), pl.BlockSpec((None, 1, n), lambda l: (l, 0, 0))],
        [_sds((depth, D_MODEL, n), F32), _sds((depth, 1, n), F32)],
    )(c_all_t, dmod)


def _lb_fn(p):
    rows = [p[l:l + 1] for l in range(DEPTH)]
    mx = functools.reduce(jnp.maximum, rows)
    es = [jnp.exp(r - mx) for r in rows]
    tot = functools.reduce(lambda a, b: a + b, es)
    sm = [e / tot for e in es]
    out, run = [], None
    for l in range(DEPTH):
        run = sm[l] if run is None else run + sm[l]
        out.append(run - sm[0])
    return jnp.concatenate(out, axis=0)


def _lb_fwd(p):
    def body(p_ref, o_ref):
        o_ref[...] = _lb_fn(p_ref[...])

    full = pl.BlockSpec(p.shape, lambda i: (0, 0))
    return _call(body, "lb_fwd", (1,), [full], full, _sds(p.shape, F32))(p)


def _lb_bwd(p, d_lower):
    def body(p_ref, d_ref, o_ref):
        _, vjp = jax.vjp(_lb_fn, p_ref[...])
        o_ref[...] = vjp(d_ref[...])[0]

    full = pl.BlockSpec(p.shape, lambda i: (0, 0))
    return _call(body, "lb_bwd", (1,), [full, full], full, _sds(p.shape, F32))(p, d_lower)


def _norm_mod_fn(x, w, shift, scale):
    return _rms(x, w) * (1.0 + scale) + shift


def _norm_mod_fwd(x, w, shift, scale, name):
    bsz, s, d = x.shape
    ts = _pick(s, (256, 128, 64, 32, 16, 8))

    def body(x_ref, w_ref, sh_ref, sc_ref, o_ref):
        o_ref[...] = _norm_mod_fn(x_ref[...], w_ref[...], sh_ref[...], sc_ref[...]).astype(BF16)

    row = pl.BlockSpec((None, ts, d), lambda b, i: (b, i, 0))
    per_b = pl.BlockSpec((None, 1, d), lambda b, i: (b, 0, 0))
    return _call(body, name, (bsz, s // ts), [row, pl.BlockSpec((1, d), lambda b, i: (0, 0)), per_b, per_b],
                 row, _sds(x.shape, BF16))(x, w, shift, scale)


def _norm_mod_bwd(x, w, shift, scale, dh, carry, name):
    bsz, s, d = x.shape
    ts = _pick(s, (256, 128, 64, 32, 16, 8))

    def body(x_ref, w_ref, sh_ref, sc_ref, dh_ref, c_ref, dx_ref, dw_ref, dsh_ref, dsc_ref):
        b, i = pl.program_id(0), pl.program_id(1)
        _, vjp = jax.vjp(_norm_mod_fn, x_ref[...], w_ref[...], sh_ref[...], sc_ref[...])
        dx, dw, dsh, dsc = vjp(dh_ref[...])
        dx_ref[...] = dx + c_ref[...]

        @pl.when((b == 0) & (i == 0))
        def _():
            dw_ref[...] = jnp.zeros_like(dw_ref)

        @pl.when(i == 0)
        def _():
            dsh_ref[...] = jnp.zeros_like(dsh_ref)
            dsc_ref[...] = jnp.zeros_like(dsc_ref)

        dw_ref[...] += dw
        dsh_ref[...] += dsh
        dsc_ref[...] += dsc

    row = pl.BlockSpec((None, ts, d), lambda b, i: (b, i, 0))
    per_b = pl.BlockSpec((None, 1, d), lambda b, i: (b, 0, 0))
    wspec = pl.BlockSpec((1, d), lambda b, i: (0, 0))
    return _call(body, name, (bsz, s // ts), [row, wspec, per_b, per_b, row, row],
                 [row, wspec, per_b, per_b],
                 [_sds(x.shape, F32), _sds((1, d), F32), _sds((bsz, 1, d), F32), _sds((bsz, 1, d), F32)],
                 )(x, w, shift, scale, dh, carry)


def _resid_fwd(x, y, gate, name):
    bsz, s, d = x.shape
    ts = _pick(s, (512, 256, 128, 64, 32, 16, 8))

    def body(x_ref, y_ref, g_ref, o_ref):
        o_ref[...] = x_ref[...] + g_ref[...] * y_ref[...]

    row = pl.BlockSpec((None, ts, d), lambda b, i: (b, i, 0))
    per_b = pl.BlockSpec((None, 1, d), lambda b, i: (b, 0, 0))
    return _call(body, name, (bsz, s // ts), [row, row, per_b], row, _sds(x.shape, F32))(x, y, gate)


def _gate_bwd(dx, y, gate, name):
    bsz, s, d = dx.shape
    ts = _pick(s, (512, 256, 128, 64, 32, 16, 8))

    def body(dx_ref, y_ref, g_ref, dy_ref, dg_ref):
        dxv = dx_ref[...]
        dy_ref[...] = (dxv * g_ref[...]).astype(BF16)

        @pl.when(pl.program_id(1) == 0)
        def _():
            dg_ref[...] = jnp.zeros_like(dg_ref)

        dg_ref[...] += jnp.sum(dxv * y_ref[...], axis=0, keepdims=True)

    row = pl.BlockSpec((None, ts, d), lambda b, i: (b, i, 0))
    per_b = pl.BlockSpec((None, 1, d), lambda b, i: (b, 0, 0))
    return _call(body, name, (bsz, s // ts), [row, row, per_b], [row, per_b],
                 [_sds(dx.shape, BF16), _sds((bsz, 1, d), F32)])(dx, y, gate)


HALO = 8


def _conv_pre(xx, w_ref, b_ref, kw, rows):
    acc = w_ref[kw - 1:kw, :] * xx[HALO:HALO + rows]
    for k in range(kw - 1):
        acc = acc + w_ref[k:k + 1, :] * pltpu.roll(xx, kw - 1 - k, 0)[HALO:HALO + rows]
    return acc + b_ref[...]


def _conv_fwd(x, col0, width, w, b, act, name):
    bsz, s, _ = x.shape
    kw = w.shape[0]
    ts = _pick(s, (512, 256, 128, 64, 32, 16, 8))
    tc = _pick(width, (512, 256, 128))
    assert col0 % tc == 0
    c0 = col0 // tc
    hb = ts // HALO

    def body(x_ref, xp_ref, w_ref, b_ref, o_ref):
        i = pl.program_id(1)
        xp = jnp.where(i > 0, xp_ref[...], 0.0)
        xx = jnp.concatenate([xp, x_ref[...]], axis=0)
        pre = _conv_pre(xx, w_ref, b_ref, kw, ts)
        o_ref[...] = _silu(pre) if act else pre

    return _call(
        body, name, (bsz, s // ts, width // tc),
        [pl.BlockSpec((None, ts, tc), lambda bb, i, j: (bb, i, c0 + j)),
         pl.BlockSpec((None, HALO, tc), lambda bb, i, j: (bb, jnp.maximum(i * hb - 1, 0), c0 + j)),
         pl.BlockSpec((kw, tc), lambda bb, i, j: (0, j)), pl.BlockSpec((1, tc), lambda bb, i, j: (0, j))],
        pl.BlockSpec((None, ts, tc), lambda bb, i, j: (bb, i, j)), _sds((bsz, s, width), F32),
    )(x, x, w, b)


def _conv_bwd(dy, x, col0, width, w, b, act, name):
    bsz, s, _ = x.shape
    kw = w.shape[0]
    ts = _pick(s, (512, 256, 128, 64, 32, 16, 8))
    tc = _pick(width, (512, 256, 128))
    c0 = col0 // tc
    hb = ts // HALO
    nt = s // ts
    last_h = s // HALO - 1

    def body(x_ref, xp_ref, xn_ref, dy_ref, dyn_ref, w_ref, b_ref, dx_ref, dw_ref, db_ref):
        bb, i = pl.program_id(1), pl.program_id(2)
        xp = jnp.where(i > 0, xp_ref[...], 0.0)
        xx = jnp.concatenate([xp, x_ref[...], xn_ref[...]], axis=0)
        dyy = jnp.concatenate([dy_ref[...], jnp.where(i < nt - 1, dyn_ref[...], 0.0)], axis=0)
        n = ts + HALO
        if act:
            pre = _conv_pre(xx, w_ref, b_ref, kw, n)
            sg = jax.nn.sigmoid(pre)
            dpre = dyy * (sg * (1.0 + pre * (1.0 - sg)))
        else:
            dpre = dyy
        dx = w_ref[kw - 1:kw, :] * dpre[:ts]
        for k in range(kw - 1):
            dx = dx + w_ref[k:k + 1, :] * pltpu.roll(dpre, n - (kw - 1 - k), 0)[:ts]
        dx_ref[...] = dx.astype(BF16)

        @pl.when((bb == 0) & (i == 0))
        def _():
            dw_ref[...] = jnp.zeros_like(dw_ref)
            db_ref[...] = jnp.zeros_like(db_ref)

        dt = dpre[:ts]
        db_ref[...] += jnp.sum(dt, axis=0, keepdims=True)
        dw_ref[kw - 1:kw, :] += jnp.sum(dt * xx[HALO:HALO + ts], axis=0, keepdims=True)
        for k in range(kw - 1):
            xs = pltpu.roll(xx, kw - 1 - k, 0)[HALO:HALO + ts]
            dw_ref[k:k + 1, :] += jnp.sum(dt * xs, axis=0, keepdims=True)

    xspec = lambda f: pl.BlockSpec((None, HALO, tc), f)
    return _call(
        body, name, (width // tc, bsz, nt),
        [pl.BlockSpec((None, ts, tc), lambda j, bb, i: (bb, i, c0 + j)),
         xspec(lambda j, bb, i: (bb, jnp.maximum(i * hb - 1, 0), c0 + j)),
         xspec(lambda j, bb, i: (bb, jnp.minimum((i + 1) * hb, last_h), c0 + j)),
         pl.BlockSpec((None, ts, tc), lambda j, bb, i: (bb, i, j)),
         xspec(lambda j, bb, i: (bb, jnp.minimum((i + 1) * hb, last_h), j)),
         pl.BlockSpec((kw, tc), lambda j, bb, i: (0, j)), pl.BlockSpec((1, tc), lambda j, bb, i: (0, j))],
        [pl.BlockSpec((None, ts, tc), lambda j, bb, i: (bb, i, j)),
         pl.BlockSpec((kw, tc), lambda j, bb, i: (0, j)), pl.BlockSpec((1, tc), lambda j, bb, i: (0, j))],
        [_sds((bsz, s, width), BF16), _sds((kw, width), F32), _sds((1, width), F32)],
    )(x, x, x, dy, dy, w, b)


def _conv_glu_fwd(x, w, b, name):
    bsz, s, f2 = x.shape
    f = f2 // 2
    kw = w.shape[0]
    ts = _pick(s, (512, 256, 128, 64, 32, 16, 8))
    tc = _pick(f, (256, 128))
    nf = f // tc
    hb = ts // HALO

    def body(xg_ref, xgp_ref, xv_ref, xvp_ref, wg_ref, wv_ref, bg_ref, bv_ref, o_ref):
        i = pl.program_id(1)
        halves = []
        for x_ref, xp_ref, w_ref, b_ref in ((xg_ref, xgp_ref, wg_ref, bg_ref), (xv_ref, xvp_ref, wv_ref, bv_ref)):
            xx = jnp.concatenate([jnp.where(i > 0, xp_ref[...], 0.0), x_ref[...]], axis=0)
            halves.append(_conv_pre(xx, w_ref, b_ref, kw, ts))
        o_ref[...] = (_silu(halves[0]) * halves[1]).astype(BF16)

    tile = lambda off: pl.BlockSpec((None, ts, tc), lambda bb, i, j: (bb, i, off + j))
    prev = lambda off: pl.BlockSpec((None, HALO, tc), lambda bb, i, j: (bb, jnp.maximum(i * hb - 1, 0), off + j))
    wsp = lambda rows, off: pl.BlockSpec((rows, tc), lambda bb, i, j: (0, off + j))
    return _call(
        body, name, (bsz, s // ts, nf),
        [tile(0), prev(0), tile(nf), prev(nf), wsp(kw, 0), wsp(kw, nf), wsp(1, 0), wsp(1, nf)],
        pl.BlockSpec((None, ts, tc), lambda bb, i, j: (bb, i, j)), _sds((bsz, s, f), BF16),
    )(x, x, x, x, w, w, b, b)


def _conv_glu_bwd(da, x, w, b, name):
    bsz, s, f2 = x.shape
    f = f2 // 2
    kw = w.shape[0]
    ts = _pick(s, (512, 256, 128, 64, 32, 16, 8))
    tc = _pick(f, (256, 128))
    nf = f // tc
    hb = ts // HALO
    nt = s // ts
    last_h = s // HALO - 1
    n = ts + HALO

    def body(xg_ref, xgp_ref, xgn_ref, xv_ref, xvp_ref, xvn_ref, da_ref, dan_ref,
             wg_ref, wv_ref, bg_ref, bv_ref, wx_ref, dx_ref, dw_ref, db_ref):
        j, bb, i = pl.program_id(0), pl.program_id(1), pl.program_id(2)
        day = jnp.concatenate([da_ref[...], jnp.where(i < nt - 1, dan_ref[...], 0.0)], axis=0)
        xg = jnp.concatenate([jnp.where(i > 0, xgp_ref[...], 0.0), xg_ref[...], xgn_ref[...]], axis=0)
        pre_g = _conv_pre(xg, wg_ref, bg_ref, kw, n)
        sg = jax.nn.sigmoid(pre_g)

        @pl.when((bb == 0) & (i == 0))
        def _():
            dw_ref[...] = jnp.zeros_like(dw_ref)
            db_ref[...] = jnp.zeros_like(db_ref)

        def finish(dpre, xx):
            dx = wx_ref[kw - 1:kw, :] * dpre[:ts]
            for k in range(kw - 1):
                dx = dx + wx_ref[k:k + 1, :] * pltpu.roll(dpre, n - (kw - 1 - k), 0)[:ts]
            dx_ref[...] = dx.astype(BF16)
            dt = dpre[:ts]
            db_ref[...] += jnp.sum(dt, axis=0, keepdims=True)
            dw_ref[kw - 1:kw, :] += jnp.sum(dt * xx[HALO:HALO + ts], axis=0, keepdims=True)
            for k in range(kw - 1):
                dw_ref[k:k + 1, :] += jnp.sum(dt * pltpu.roll(xx, kw - 1 - k, 0)[HALO:HALO + ts], axis=0, keepdims=True)

        @pl.when(j < nf)
        def _():
            xv = jnp.concatenate([jnp.where(i > 0, xvp_ref[...], 0.0), xv_ref[...], xvn_ref[...]], axis=0)
            pre_v = _conv_pre(xv, wv_ref, bv_ref, kw, n)
            finish(day * pre_v * (sg * (1.0 + pre_g * (1.0 - sg))), xg)

        @pl.when(j >= nf)
        def _():
            xv = jnp.concatenate([jnp.where(i > 0, xvp_ref[...], 0.0), xv_ref[...], xvn_ref[...]], axis=0)
            finish(day * (pre_g * sg), xv)

    tile = lambda off: pl.BlockSpec((None, ts, tc), lambda j, bb, i: (bb, i, off + j % nf))
    prev = lambda off: pl.BlockSpec((None, HALO, tc), lambda j, bb, i: (bb, jnp.maximum(i * hb - 1, 0), off + j % nf))
    nxt = lambda off: pl.BlockSpec((None, HALO, tc), lambda j, bb, i: (bb, jnp.minimum((i + 1) * hb, last_h), off + j % nf))
    wsp = lambda rows, off: pl.BlockSpec((rows, tc), lambda j, bb, i: (0, off + j % nf))
    own = lambda rows: pl.BlockSpec((rows, tc), lambda j, bb, i: (0, j))
    return _call(
        body, name, (2 * nf, bsz, nt),
        [tile(0), prev(0), nxt(0), tile(nf), prev(nf), nxt(nf), tile(0), nxt(0),
         wsp(kw, 0), wsp(kw, nf), wsp(1, 0), wsp(1, nf), own(kw)],
        [pl.BlockSpec((None, ts, tc), lambda j, bb, i: (bb, i, j)), own(kw), own(1)],
        [_sds((bsz, s, f2), BF16), _sds((kw, f2), F32), _sds((1, f2), F32)],
    )(x, x, x, x, x, x, da, da, w, w, b, b, w)


def _merge_fwd(p, oa, ob, oc, wa, wb, wc, name):
    bsz, s, _ = p.shape
    tm = _pick(s, (256, 128, 64, 32, 16, 8))
    gblk = P_GATE // (3 * D_MODEL)

    def body(g_ref, oa_ref, ob_ref, oc_ref, wa_ref, wb_ref, wc_ref, o_ref):
        acc = None
        for i, (o_r, w_r) in enumerate(((oa_ref, wa_ref), (ob_ref, wb_ref), (oc_ref, wc_ref))):
            y = _bdot(o_r[...], w_r[...])
            t = jax.nn.sigmoid(g_ref[:, i * D_MODEL:(i + 1) * D_MODEL]) * y
            acc = t if acc is None else acc + t
        o_ref[...] = acc.astype(BF16)

    orow = pl.BlockSpec((None, tm, 512), lambda b, i: (b, i, 0))
    wfull = pl.BlockSpec((512, D_MODEL), lambda b, i: (0, 0))
    return _call(
        body, name, (bsz, s // tm),
        [pl.BlockSpec((None, tm, 3 * D_MODEL), lambda b, i: (b, i, gblk)), orow, orow, orow, wfull, wfull, wfull],
        pl.BlockSpec((None, tm, D_MODEL), lambda b, i: (b, i, 0)), _sds((bsz, s, D_MODEL), BF16),
    )(p, oa, ob, oc, wa, wb, wc)


def _merge_bwd(dm, p, oa, ob, oc, wa, wb, wc, wat, wbt, wct, name):
    bsz, s, _ = p.shape
    tm = _pick(s, (256, 128, 64, 32, 16, 8))
    gblk = P_GATE // (3 * D_MODEL)

    def body(dm_ref, g_ref, oa_ref, ob_ref, oc_ref, wa_ref, wb_ref, wc_ref, wat_ref, wbt_ref, wct_ref,
             dg_ref, doa_ref, dob_ref, doc_ref, dya_ref, dyb_ref, dyc_ref):
        dmv = dm_ref[...]
        trip = ((oa_ref, wa_ref, wat_ref, doa_ref, dya_ref), (ob_ref, wb_ref, wbt_ref, dob_ref, dyb_ref),
                (oc_ref, wc_ref, wct_ref, doc_ref, dyc_ref))
        for i, (o_r, w_r, wt_r, do_r, dy_r) in enumerate(trip):
            y = _bdot(o_r[...], w_r[...])
            sg = jax.nn.sigmoid(g_ref[:, i * D_MODEL:(i + 1) * D_MODEL])
            dg_ref[:, i * D_MODEL:(i + 1) * D_MODEL] = (dmv * y * sg * (1.0 - sg)).astype(BF16)
            dy = (dmv * sg).astype(BF16)
            dy_r[...] = dy
            do_r[...] = _bdot(dy, wt_r[...])

    orow = pl.BlockSpec((None, tm, 512), lambda b, i: (b, i, 0))
    drow = pl.BlockSpec((None, tm, D_MODEL), lambda b, i: (b, i, 0))
    grow = pl.BlockSpec((None, tm, 3 * D_MODEL), lambda b, i: (b, i, 0))
    wfull = pl.BlockSpec((512, D_MODEL), lambda b, i: (0, 0))
    wtfull = pl.BlockSpec((D_MODEL, 512), lambda b, i: (0, 0))
    return _call(
        body, name, (bsz, s // tm),
        [drow, pl.BlockSpec((None, tm, 3 * D_MODEL), lambda b, i: (b, i, gblk)), orow, orow, orow,
         wfull, wfull, wfull, wtfull, wtfull, wtfull],
        [grow, orow, orow, orow, drow, drow, drow],
        [_sds((bsz, s, 3 * D_MODEL), BF16)] + [_sds((bsz, s, 512), F32)] * 3 + [_sds((bsz, s, D_MODEL), BF16)] * 3,
    )(dm, p, oa, ob, oc, wa, wb, wc, wat, wbt, wct)


def _final_loss(x, w, target):
    bsz, s, d = x.shape
    ts = _pick(s, (256, 128, 64, 32, 16, 8))

    def body(x_ref, w_ref, t_ref, loss_ref, dx_ref, dw_ref):
        first = (pl.program_id(0) == 0) & (pl.program_id(1) == 0)
        y, vjp = jax.vjp(_rms, x_ref[...], w_ref[...])
        err = y - t_ref[...]
        dx, dw = vjp(err * (1.0 / d))
        dx_ref[...] = dx

        @pl.when(first)
        def _():
            loss_ref[...] = jnp.zeros_like(loss_ref)
            dw_ref[...] = jnp.zeros_like(dw_ref)

        loss_ref[...] += 0.5 * jnp.sum(jnp.sum(err * err, axis=1, keepdims=True), axis=0, keepdims=True) * (1.0 / d)
        dw_ref[...] += dw

    row = pl.BlockSpec((None, ts, d), lambda b, i: (b, i, 0))
    wspec = pl.BlockSpec((1, d), lambda b, i: (0, 0))
    return _call(body, "final_loss", (bsz, s // ts), [row, wspec, row],
                 [pl.BlockSpec((8, LANES), lambda b, i: (0, 0)), row, wspec],
                 [_sds((8, LANES), F32), _sds(x.shape, F32), _sds((1, d), F32)])(x, w, target)


def _unit_lower_inverses(ms):
    n = ms[0].shape[0]
    eye = (_iota((n, n), 0) == _iota((n, n), 1)).astype(F32)
    ps = [-m for m in ms]
    xs = [eye + p for p in ps]
    for _ in range(int(math.log2(n)) - 1):
        ps = [_mxu3(p, p, NN) for p in ps]
        xs = [x + _mxu3(x, p, NN) for x, p in zip(xs, ps)]
    return xs


@jax.custom_vjp
def _known_inverse(m, t):
    return t


_known_inverse.defvjp(lambda m, t: (t, t), lambda t, dt: (-_mxu3(t, _mxu3(dt, t, NT), TN), jnp.zeros_like(t)))


def _gdn_chunk(states, qkv, small, z, a_row, dt_row, nw, tinvs=None):
    nb = len(qkv)
    c = qkv[0].shape[0]
    kw = GDN_HEADS * GDN_DK
    incl, strict = _tril(c), _tril(c, True)
    g_all = [-jnp.exp(a_row) * _softplus(small[b] + dt_row) for b in range(nb)]
    beta_all = [jax.nn.sigmoid(small[b]) for b in range(nb)]
    big_g_all = [_mask_dot(incl.astype(BF16), g_all[b]) for b in range(nb)]
    items = [(b, h) for b in range(nb) for h in range(GDN_HEADS)]
    ids = range(len(items))
    col = lambda b, part, h: qkv[b][:, part * kw + h * GDN_DK:part * kw + (h + 1) * GDN_DK]
    unit = lambda t: t * lax.rsqrt(jnp.sum(t * t, axis=-1, keepdims=True) + EPS)
    q = [unit(col(b, 0, h)) * (GDN_DK ** -0.5) for b, h in items]
    k = [unit(col(b, 1, h)) for b, h in items]
    v = [col(b, 2, h) for b, h in items]
    gc = [_lane_col(big_g_all[b], SM_A + h) for b, h in items]
    bc = [_lane_col(beta_all[b], SM_B + h) for b, h in items]
    g_last = [jnp.sum(_lane_col(g_all[b], SM_A + h), axis=0, keepdims=True) for b, h in items]
    decay = [jnp.where(incl, jnp.exp(jnp.where(incl, gc[i] - _col_to_row(gc[i]), 0.0)), 0.0) for i in ids]
    kb = [k[i] * bc[i] for i in ids]
    m = [jnp.where(strict, _bdot_nt(kb[i], k[i]) * decay[i], 0.0) for i in ids]
    if tinvs is None:
        tinv = _unit_lower_inverses(m)
    else:
        tinv = [_known_inverse(m[i], tinvs[i]) for i in ids]
    eg = [jnp.exp(gc[i]) for i in ids]
    u = [_dot3(tinv[i], v[i] * bc[i]) for i in ids]
    w = [_dot3(tinv[i], kb[i] * eg[i]) for i in ids]
    attn = [_bdot_nt(q[i], k[i]) * decay[i] for i in ids]
    v_new = [u[i] - _bdot(w[i], states[i]) for i in ids]
    o_st = [_bdot(q[i] * eg[i], states[i]) for i in ids]
    o = [o_st[i] + _bdot(attn[i], v_new[i]) for i in ids]
    grow = [_bdot_tn(k[i] * jnp.exp(g_last[i] - gc[i]), v_new[i]) for i in ids]
    new_states = [states[i] * jnp.exp(g_last[i]) + grow[i] for i in ids]
    outs = [_rms(o[i], nw) * _silu(z[b][:, h * GDN_DK:(h + 1) * GDN_DK]) for i, (b, h) in enumerate(items)]
    per_seq = [jnp.concatenate(outs[b * GDN_HEADS:(b + 1) * GDN_HEADS], axis=1) for b in range(nb)]
    return new_states, per_seq, tinv


def _seq_items(bsz, heads):
    return [(b, h) for b in range(bsz) for h in range(heads)]


def _gdn_fwd(qkv_act, p, a_row, dt_row, nw, name):
    bsz, s, _ = qkv_act.shape
    c = GDN_CHUNK
    nc = s // c
    items = _seq_items(bsz, GDN_HEADS)

    def body(qkv_ref, sm_ref, z_ref, a_ref, dt_ref, nw_ref, o_ref, st_ref, ti_ref, st_scr):
        @pl.when(pl.program_id(0) == 0)
        def _():
            st_scr[...] = jnp.zeros_like(st_scr)

        st_ref[...] = st_scr[...]
        seqs = range(bsz)
        new_states, o, tinvs = _gdn_chunk(
            [st_scr[b, h] for b, h in items], [qkv_ref[b] for b in seqs], [sm_ref[b] for b in seqs],
            [z_ref[b] for b in seqs], a_ref[...], dt_ref[...], nw_ref[...])
        for i, (b, h) in enumerate(items):
            st_scr[b, h] = new_states[i]
            ti_ref[b, h] = tinvs[i]
        for b in seqs:
            o_ref[b] = o[b].astype(BF16)

    row = lambda w, blk: pl.BlockSpec((bsz, c, w), lambda n, blk=blk: (0, n, blk))
    prm = pl.BlockSpec((1, LANES), lambda n: (0, 0))
    return _call(
        body, name, (nc,), [row(1536, 0), row(LANES, P_SMALL // LANES), row(512, P_GZ // 512), prm, prm, prm],
        [row(512, 0), pl.BlockSpec((bsz, None, 4, LANES, LANES), lambda n: (0, n, 0, 0, 0)),
         pl.BlockSpec((bsz, None, 4, c, c), lambda n: (0, n, 0, 0, 0))],
        [_sds((bsz, s, 512), BF16), _sds((bsz, nc, 4, LANES, LANES), F32), _sds((bsz, nc, 4, c, c), F32)],
        scratch=[pltpu.VMEM((bsz, 4, LANES, LANES), F32)],
    )(qkv_act, p, p, a_row, dt_row, nw)


def _gdn_bwd(do, qkv_act, p, a_row, dt_row, nw, st_all, ti_all, name):
    bsz, s, _ = qkv_act.shape
    c = GDN_CHUNK
    nc = s // c

    items = _seq_items(bsz, GDN_HEADS)

    def body(qkv_ref, sm_ref, z_ref, a_ref, dt_ref, nw_ref, do_ref, st_ref, ti_ref,
             dqkv_ref, dsm_ref, dz_ref, da_ref, ddt_ref, dnw_ref, ds_scr):
        @pl.when(pl.program_id(0) == 0)
        def _():
            ds_scr[...] = jnp.zeros_like(ds_scr)
            da_ref[...] = jnp.zeros_like(da_ref)
            ddt_ref[...] = jnp.zeros_like(ddt_ref)
            dnw_ref[...] = jnp.zeros_like(dnw_ref)

        seqs = range(bsz)
        tinvs = [ti_ref[b, h] for b, h in items]
        chunk = lambda *a: _gdn_chunk(*a, tinvs=tinvs)[:2]
        _, vjp = jax.vjp(chunk, [st_ref[b, h] for b, h in items], [qkv_ref[b] for b in seqs], [sm_ref[b] for b in seqs],
                         [z_ref[b] for b in seqs], a_ref[...], dt_ref[...], nw_ref[...])
        d_states, dqkv, dsm, dz, da, ddt, dnw = vjp(([ds_scr[b, h] for b, h in items], [do_ref[b] for b in seqs]))
        for i, (b, h) in enumerate(items):
            ds_scr[b, h] = d_states[i]
        for b in seqs:
            dqkv_ref[b] = dqkv[b]
            dsm_ref[b] = dsm[b]
            dz_ref[b] = dz[b].astype(BF16)
        da_ref[...] += da
        ddt_ref[...] += ddt
        dnw_ref[...] += dnw

    rrow = lambda w, blk: pl.BlockSpec((bsz, c, w), lambda n, blk=blk: (0, nc - 1 - n, blk))
    prm = pl.BlockSpec((1, LANES), lambda n: (0, 0))
    return _call(
        body, name, (nc,),
        [rrow(1536, 0), rrow(LANES, P_SMALL // LANES), rrow(512, P_GZ // 512), prm, prm, prm, rrow(512, 0),
         pl.BlockSpec((bsz, None, 4, LANES, LANES), lambda n: (0, nc - 1 - n, 0, 0, 0)),
         pl.BlockSpec((bsz, None, 4, c, c), lambda n: (0, nc - 1 - n, 0, 0, 0))],
        [rrow(1536, 0), rrow(LANES, 0), rrow(512, 0), prm, prm, prm],
        [_sds((bsz, s, 1536), F32), _sds((bsz, s, LANES), F32), _sds((bsz, s, 512), BF16)] + [_sds((1, LANES), F32)] * 3,
        scratch=[pltpu.VMEM((bsz, 4, LANES, LANES), F32)],
    )(qkv_act, p, p, a_row, dt_row, nw, do, st_all, ti_all)


def _hgrn_block(states, q_raw, f_raw, i_raw, g_raw, lb, nw):
    n = q_raw[0].shape[0]
    c = HGRN_CHUNK
    r, cc = _iota((n, n), 0), _iota((n, n), 1)
    same = (r // c) == (cc // c)
    causal = same & (r >= cc)
    ref_row = (r // c) * c + (c // 2 - 1)
    run_sum = causal.astype(F32)
    rel_sum = run_sum - (same & (ref_row >= cc)).astype(F32)
    sums = jnp.concatenate([run_sum, rel_sum, same.astype(F32)], axis=0).astype(BF16)
    seqs, chunks = range(len(q_raw)), range(n // c)
    items = _seq_items(len(q_raw), HGRN_HEADS)
    hs = lambda t, h: t[:, h * HGRN_DK:(h + 1) * HGRN_DK]
    rows = lambda t, j: t[j * c:(j + 1) * c]
    q = [_silu(q_raw[b]) for b in seqs]
    logf = [jnp.log(lb + (1.0 - lb) * jax.nn.sigmoid(f_raw[b])) for b in seqs]
    k = [(1.0 - lb) * jax.nn.sigmoid(-f_raw[b]) for b in seqs]
    all_sums = [_mask_dot(sums, logf[b]) for b in seqs]
    big_g, g_rel, g_tot = ([t[i * n:(i + 1) * n] for t in all_sums] for i in range(3))
    q_rel = [q[b] * jnp.exp(g_rel[b]) for b in seqs]
    k_rel = [k[b] * jnp.exp(-g_rel[b]) for b in seqs]
    qg = [q[b] * jnp.exp(big_g[b]) for b in seqs]
    k_end = [k[b] * jnp.exp(g_tot[b] - big_g[b]) for b in seqs]
    keep = [[jnp.exp(g_tot[b][j * c:j * c + 1]) for j in chunks] for b in seqs]
    scores = [_bdot_nt(hs(q_rel[b], h), hs(k_rel[b], h)) for b, h in items]
    o_intra = [_bdot(jnp.where(causal, scores[i], 0.0), hs(i_raw[b], h)) for i, (b, h) in enumerate(items)]
    grow = [[_bdot_tn(rows(hs(i_raw[b], h), j), rows(hs(k_end[b], h), j)) for j in chunks] for b, h in items]
    entering, new_states = [], []
    for i, (b, h) in enumerate(items):
        st, per_chunk = states[i], []
        for j in chunks:
            per_chunk.append(st)
            st = st * hs(keep[b][j], h) + grow[i][j]
        entering.append(per_chunk)
        new_states.append(st)
    o_inter = [[_bdot_nt(rows(hs(qg[b], h), j), entering[i][j]) for j in chunks] for i, (b, h) in enumerate(items)]
    outs = [_rms(o_intra[i] + jnp.concatenate(o_inter[i], axis=0), nw) * _silu(hs(g_raw[b], h))
            for i, (b, h) in enumerate(items)]
    return new_states, [jnp.concatenate(outs[b * HGRN_HEADS:(b + 1) * HGRN_HEADS], axis=1) for b in seqs]


def _hgrn_fwd(p, lb, nw, name):
    bsz, s, _ = p.shape
    n = HGRN_BLOCK
    nb = s // n

    items = _seq_items(bsz, HGRN_HEADS)

    def body(q_ref, f_ref, i_ref, g_ref, lb_ref, nw_ref, o_ref, st_ref, st_scr):
        @pl.when(pl.program_id(0) == 0)
        def _():
            st_scr[...] = jnp.zeros_like(st_scr)

        st_ref[...] = st_scr[...]
        per_seq = lambda ref: [ref[b] for b in range(bsz)]
        new_states, o = _hgrn_block([st_scr[b, h] for b, h in items], per_seq(q_ref), per_seq(f_ref), per_seq(i_ref),
                                    per_seq(g_ref), lb_ref[...], nw_ref[...])
        for i, (b, h) in enumerate(items):
            st_scr[b, h] = new_states[i]
        for b in range(bsz):
            o_ref[b] = o[b].astype(BF16)

    row = lambda blk: pl.BlockSpec((bsz, n, 512), lambda i, blk=blk: (0, i, blk))
    return _call(
        body, name, (nb,),
        [row(P_HQ // 512), row(P_HF // 512), row(P_HI // 512), row(P_HG // 512),
         pl.BlockSpec((1, 512), lambda i: (0, 0)), pl.BlockSpec((1, LANES), lambda i: (0, 0))],
        [row(0), pl.BlockSpec((bsz, None, 4, LANES, LANES), lambda i: (0, i, 0, 0, 0))],
        [_sds((bsz, s, 512), BF16), _sds((bsz, nb, 4, LANES, LANES), F32)],
        scratch=[pltpu.VMEM((bsz, 4, LANES, LANES), F32)],
    )(p, p, p, p, lb, nw)


def _hgrn_bwd(do, p, lb, nw, st_all, name):
    bsz, s, _ = p.shape
    n = HGRN_BLOCK
    nb = s // n

    items = _seq_items(bsz, HGRN_HEADS)

    def body(q_ref, f_ref, i_ref, g_ref, lb_ref, nw_ref, do_ref, st_ref, dp_ref, dlb_ref, dnw_ref, ds_scr):
        @pl.when(pl.program_id(0) == 0)
        def _():
            ds_scr[...] = jnp.zeros_like(ds_scr)
            dlb_ref[...] = jnp.zeros_like(dlb_ref)
            dnw_ref[...] = jnp.zeros_like(dnw_ref)

        per_seq = lambda ref: [ref[b] for b in range(bsz)]
        _, vjp = jax.vjp(_hgrn_block, [st_ref[b, h] for b, h in items], per_seq(q_ref), per_seq(f_ref), per_seq(i_ref),
                         per_seq(g_ref), lb_ref[...], nw_ref[...])
        d_states, dq, df, di, dg, dlb, dnw = vjp(([ds_scr[b, h] for b, h in items], per_seq(do_ref)))
        for i, (b, h) in enumerate(items):
            ds_scr[b, h] = d_states[i]
        for b in range(bsz):
            for j, t in enumerate((dq, df, di, dg)):
                dp_ref[b, :, j * 512:(j + 1) * 512] = t[b].astype(BF16)
        dlb_ref[...] += dlb
        dnw_ref[...] += dnw

    row = lambda blk: pl.BlockSpec((bsz, n, 512), lambda i, blk=blk: (0, nb - 1 - i, blk))
    return _call(
        body, name, (nb,),
        [row(P_HQ // 512), row(P_HF // 512), row(P_HI // 512), row(P_HG // 512),
         pl.BlockSpec((1, 512), lambda i: (0, 0)), pl.BlockSpec((1, LANES), lambda i: (0, 0)), row(0),
         pl.BlockSpec((bsz, None, 4, LANES, LANES), lambda i: (0, nb - 1 - i, 0, 0, 0))],
        [pl.BlockSpec((bsz, n, 2048), lambda i: (0, nb - 1 - i, 0)),
         pl.BlockSpec((1, 512), lambda i: (0, 0)), pl.BlockSpec((1, LANES), lambda i: (0, 0))],
        [_sds((bsz, s, 2048), BF16), _sds((1, 512), F32), _sds((1, LANES), F32)],
        scratch=[pltpu.VMEM((bsz, 4, LANES, LANES), F32)],
    )(p, p, p, p, lb, nw, do, st_all)


def _ssd_chunk(states, xbc, small, z, a_row, dt_row, d_row, nw):
    seqs = range(len(xbc))
    c = xbc[0].shape[0]
    incl = _tril(c)
    spread = (_iota((LANES, SSD_INNER), 0) == SM_DT + _iota((LANES, SSD_INNER), 1) // SSD_HEAD_DIM).astype(BF16)
    dt_all = [_softplus(small[b] + dt_row) for b in seqs]
    both = [_spread_dot(jnp.concatenate([dt_all[b], dt_all[b] * (-jnp.exp(a_row))], axis=0), spread) for b in seqs]
    dt_e, da_e = [t[:c] for t in both], [t[c:] for t in both]
    acs_e = [_mask_dot(incl.astype(BF16), da_e[b]) for b in seqs]
    last_e = [jnp.sum(da_e[b], axis=0, keepdims=True) for b in seqs]
    xs = [xbc[b][:, :SSD_INNER] for b in seqs]
    xdt = [xs[b] * dt_e[b] for b in seqs]
    gw = SSD_GROUPS * SSD_STATE
    lane = _iota((1, LANES), 1)
    items = _seq_items(len(xbc), 4)
    grp = [(b, g) for b in seqs for g in range(SSD_GROUPS)]
    ps = lambda t, j: t[:, j * LANES:(j + 1) * LANES]
    bg = {(b, g): xbc[b][:, SSD_INNER + g * SSD_STATE:SSD_INNER + (g + 1) * SSD_STATE] for b, g in grp}
    cg = {(b, g): xbc[b][:, SSD_INNER + gw + g * SSD_STATE:SSD_INNER + gw + (g + 1) * SSD_STATE] for b, g in grp}
    cb = {bgk: _bdot_nt(cg[bgk], bg[bgk]) for bgk in grp}

    def seg(b, j, sub):
        ac = ps(acs_e[b], j)[:, sub * SSD_HEAD_DIM:sub * SSD_HEAD_DIM + 1]
        return jnp.where(incl, jnp.exp(jnp.where(incl, ac - _col_to_row(ac), 0.0)), 0.0)

    mine = [((lane // SSD_HEAD_DIM) == sub).astype(F32) for sub in range(2)]
    y_in = [[_bdot(cb[b, j // 2] * seg(b, j, sub), ps(xdt[b], j) * mine[sub]) for sub in range(2)] for b, j in items]
    y_st = [_bdot(cg[b, j // 2], states[i]) for i, (b, j) in enumerate(items)]
    grow = [_bdot_tn(bg[b, j // 2], ps(xdt[b], j) * jnp.exp(ps(last_e[b], j) - ps(acs_e[b], j))) for b, j in items]
    new_states = [states[i] * jnp.exp(ps(last_e[b], j)) + grow[i] for i, (b, j) in enumerate(items)]
    ys = [y_in[i][0] + y_in[i][1] + y_st[i] * jnp.exp(ps(acs_e[b], j)) + ps(d_row, j) * ps(xs[b], j)
          for i, (b, j) in enumerate(items)]
    gwid = SSD_INNER // SSD_GROUPS
    outs = []
    for b in seqs:
        yz = jnp.concatenate(ys[4 * b:4 * b + 4], axis=1) * _silu(z[b])
        outs.append(jnp.concatenate(
            [_rms(yz[:, g * gwid:(g + 1) * gwid], nw[:, g * gwid:(g + 1) * gwid]) for g in range(SSD_GROUPS)], axis=1))
    return new_states, outs


def _ssd_fwd(xbc_act, p, a_row, dt_row, d_row, nw, name):
    bsz, s, _ = xbc_act.shape
    c = SSD_CHUNK
    nc = s // c

    items = _seq_items(bsz, 4)

    def body(x_ref, sm_ref, z_ref, a_ref, dt_ref, d_ref, nw_ref, o_ref, st_ref, st_scr):
        @pl.when(pl.program_id(0) == 0)
        def _():
            st_scr[...] = jnp.zeros_like(st_scr)

        st_ref[...] = st_scr[...]
        per_seq = lambda ref: [ref[b] for b in range(bsz)]
        new_states, o = _ssd_chunk([st_scr[b, j] for b, j in items], per_seq(x_ref), per_seq(sm_ref), per_seq(z_ref),
                                   a_ref[...], dt_ref[...], d_ref[...], nw_ref[...])
        for i, (b, j) in enumerate(items):
            st_scr[b, j] = new_states[i]
        for b in range(bsz):
            o_ref[b] = o[b].astype(BF16)

    row = lambda w, blk: pl.BlockSpec((bsz, c, w), lambda n, blk=blk: (0, n, blk))
    prm = pl.BlockSpec((1, LANES), lambda n: (0, 0))
    prm5 = pl.BlockSpec((1, 512), lambda n: (0, 0))
    return _call(
        body, name, (nc,),
        [row(1024, 0), row(LANES, P_SMALL // LANES), row(512, P_SZ // 512), prm, prm, prm5, prm5],
        [row(512, 0), pl.BlockSpec((bsz, None, 4, LANES, LANES), lambda n: (0, n, 0, 0, 0))],
        [_sds((bsz, s, 512), BF16), _sds((bsz, nc, 4, LANES, LANES), F32)],
        scratch=[pltpu.VMEM((bsz, 4, LANES, LANES), F32)],
    )(xbc_act, p, p, a_row, dt_row, d_row, nw)


def _ssd_bwd(do, xbc_act, p, a_row, dt_row, d_row, nw, st_all, name):
    bsz, s, _ = xbc_act.shape
    c = SSD_CHUNK
    nc = s // c

    items = _seq_items(bsz, 4)

    def body(x_ref, sm_ref, z_ref, a_ref, dt_ref, d_ref, nw_ref, do_ref, st_ref,
             dx_ref, dsm_ref, dz_ref, da_ref, ddt_ref, dd_ref, dnw_ref, ds_scr):
        @pl.when(pl.program_id(0) == 0)
        def _():
            ds_scr[...] = jnp.zeros_like(ds_scr)
            da_ref[...] = jnp.zeros_like(da_ref)
            ddt_ref[...] = jnp.zeros_like(ddt_ref)
            dd_ref[...] = jnp.zeros_like(dd_ref)
            dnw_ref[...] = jnp.zeros_like(dnw_ref)

        per_seq = lambda ref: [ref[b] for b in range(bsz)]
        _, vjp = jax.vjp(_ssd_chunk, [st_ref[b, j] for b, j in items], per_seq(x_ref), per_seq(sm_ref), per_seq(z_ref),
                         a_ref[...], dt_ref[...], d_ref[...], nw_ref[...])
        d_states, dx, dsm, dz, da, ddt, dd, dnw = vjp(([ds_scr[b, j] for b, j in items], per_seq(do_ref)))
        for i, (b, j) in enumerate(items):
            ds_scr[b, j] = d_states[i]
        for b in range(bsz):
            dx_ref[b] = dx[b]
            dsm_ref[b] = dsm[b]
            dz_ref[b] = dz[b].astype(BF16)
        da_ref[...] += da
        ddt_ref[...] += ddt
        dd_ref[...] += dd
        dnw_ref[...] += dnw

    row = lambda w, blk: pl.BlockSpec((bsz, c, w), lambda n, blk=blk: (0, nc - 1 - n, blk))
    prm = pl.BlockSpec((1, LANES), lambda n: (0, 0))
    prm5 = pl.BlockSpec((1, 512), lambda n: (0, 0))
    return _call(
        body, name, (nc,),
        [row(1024, 0), row(LANES, P_SMALL // LANES), row(512, P_SZ // 512), prm, prm, prm5, prm5, row(512, 0),
         pl.BlockSpec((bsz, None, 4, LANES, LANES), lambda n: (0, nc - 1 - n, 0, 0, 0))],
        [row(1024, 0), row(LANES, 0), row(512, 0), prm, prm, prm5, prm5],
        [_sds((bsz, s, 1024), F32), _sds((bsz, s, LANES), F32), _sds((bsz, s, 512), BF16),
         _sds((1, LANES), F32), _sds((1, LANES), F32), _sds((1, 512), F32), _sds((1, 512), F32)],
        scratch=[pltpu.VMEM((bsz, 4, LANES, LANES), F32)],
    )(xbc_act, p, p, a_row, dt_row, d_row, nw, do, st_all)


def _peer(k):
    x, y, c = lax.axis_index("x"), lax.axis_index("y"), lax.axis_index("c")
    px = 1 - x if k & 4 else x
    py = 1 - y if k & 2 else y
    pc = 1 - c if k & 1 else c
    return (px, py, pc), 4 * px + 2 * py + pc


def _my_index():
    return 4 * lax.axis_index("x") + 2 * lax.axis_index("y") + lax.axis_index("c")


def _exchange(arrays, name, gather):
    n = len(arrays)

    def body(*refs):
        ins, outs = refs[:n], refs[n:2 * n]
        send_sems, recv_sems, local_sems = refs[2 * n:]
        me = _my_index()

        def copy(i, k):
            peer, slot = _peer(k)
            src = ins[i] if gather else ins[i].at[slot]
            return pltpu.make_async_remote_copy(src_ref=src, dst_ref=outs[i].at[me], send_sem=send_sems.at[k - 1, i],
                                                recv_sem=recv_sems.at[k - 1, i], device_id=peer, device_id_type=MESH_ID)

        def arrival(i, k):
            peer, slot = _peer(k)
            src = ins[i] if gather else ins[i].at[slot]
            return pltpu.make_async_remote_copy(src_ref=src, dst_ref=outs[i].at[slot], send_sem=send_sems.at[k - 1, i],
                                                recv_sem=recv_sems.at[k - 1, i], device_id=peer, device_id_type=MESH_ID)

        mine = [pltpu.make_async_copy(ins[i] if gather else ins[i].at[me], outs[i].at[me], local_sems.at[i])
                for i in range(n)]
        sends = [copy(i, k) for k in range(1, N_DEV) for i in range(n)]
        for cp in mine + sends:
            cp.start()
        for k in range(1, N_DEV):
            for i in range(n):
                arrival(i, k).wait_recv()
        for cp in sends:
            cp.wait_send()
        for cp in mine:
            cp.wait()

    out_shape = [_sds(((N_DEV,) + a.shape) if gather else a.shape, a.dtype) for a in arrays]
    any_spec = pl.BlockSpec(memory_space=pl.ANY)
    return pl.pallas_call(
        body, name=name, out_shape=out_shape, in_specs=[any_spec] * n, out_specs=[any_spec] * n,
        scratch_shapes=[pltpu.SemaphoreType.DMA((N_DEV - 1, n)), pltpu.SemaphoreType.DMA((N_DEV - 1, n)),
                        pltpu.SemaphoreType.DMA((n,))],
    )(*arrays)


def _mesh_place():
    x, y, c = lax.axis_index("x"), lax.axis_index("y"), lax.axis_index("c")
    return (x, y, c), (x, y, 1 - c), [(1 - x, y), (x, 1 - y), (1 - x, 1 - y)]


def _run_exchange(body, name, arrays, out_shape, n_sems):
    n = len(arrays)
    any_spec = pl.BlockSpec(memory_space=pl.ANY)
    return pl.pallas_call(
        body, name=name, out_shape=out_shape, in_specs=[any_spec] * n, out_specs=[any_spec] * n,
        scratch_shapes=[pltpu.SemaphoreType.DMA((n_sems, n)), pltpu.SemaphoreType.DMA((n_sems, n)),
                        pltpu.SemaphoreType.DMA((n,))],
    )(*arrays)


def _gather_two_level(arrays, name):
    n = len(arrays)

    def body(*refs):
        ins, outs = refs[:n], refs[n:2 * n]
        send_sems, recv_sems, local_sems = refs[2 * n:]
        (x, y, c), sibling, chips = _mesh_place()
        slot = lambda px, py, pc: 4 * px + 2 * py + pc

        def copy(i, k, block, to, src=None):
            return pltpu.make_async_remote_copy(
                src_ref=outs[i].at[block] if src is None else src, dst_ref=outs[i].at[block],
                send_sem=send_sems.at[k, i], recv_sem=recv_sems.at[k, i], device_id=to, device_id_type=MESH_ID)

        me = slot(x, y, c)
        mine = [pltpu.make_async_copy(ins[i], outs[i].at[me], local_sems.at[i]) for i in range(n)]
        first = [copy(i, 0, me, sibling, src=ins[i]) for i in range(n)]
        first += [copy(i, 1 + j, me, (*chip, c), src=ins[i]) for j, chip in enumerate(chips) for i in range(n)]
        for cp in mine + first:
            cp.start()
        passed = []
        for j, chip in enumerate(chips):
            for i in range(n):
                copy(i, 1 + j, slot(*chip, c), (x, y, c)).wait_recv()
                cp = copy(i, 4 + j, slot(*chip, c), sibling)
                cp.start()
                passed.append(cp)
        for i in range(n):
            copy(i, 0, slot(x, y, 1 - c), (x, y, c)).wait_recv()
        for j, chip in enumerate(chips):
            for i in range(n):
                copy(i, 4 + j, slot(*chip, 1 - c), (x, y, c)).wait_recv()
        for cp in first + passed:
            cp.wait_send()
        for cp in mine:
            cp.wait()

    out_shape = [_sds((N_DEV,) + a.shape, a.dtype) for a in arrays]
    return _run_exchange(body, name, arrays, out_shape, 7)


def _sibling_swap(arrays, name):
    n = len(arrays)

    def body(*refs):
        ins, outs = refs[:n], refs[n:2 * n]
        send_sems, recv_sems, _ = refs[2 * n:]
        (x, y, c), sibling, _ = _mesh_place()
        copies = [pltpu.make_async_remote_copy(
            src_ref=ins[i].at[1 - c], dst_ref=outs[i], send_sem=send_sems.at[0, i], recv_sem=recv_sems.at[0, i],
            device_id=sibling, device_id_type=MESH_ID) for i in range(n)]
        for cp in copies:
            cp.start()
        for cp in copies:
            cp.wait()

    out_shape = [_sds(a.shape[1:], a.dtype) for a in arrays]
    return _run_exchange(body, name, arrays, out_shape, 1)


def _chip_scatter(arrays, name):
    n = len(arrays)

    def body(*refs):
        ins, outs = refs[:n], refs[n:2 * n]
        send_sems, recv_sems, local_sems = refs[2 * n:]
        (x, y, c), _, chips = _mesh_place()
        me = 2 * x + y
        mine = [pltpu.make_async_copy(ins[i].at[me], outs[i].at[me], local_sems.at[i]) for i in range(n)]
        sends = [pltpu.make_async_remote_copy(
            src_ref=ins[i].at[2 * chip[0] + chip[1]], dst_ref=outs[i].at[me], send_sem=send_sems.at[j, i],
            recv_sem=recv_sems.at[j, i], device_id=(*chip, c), device_id_type=MESH_ID)
            for j, chip in enumerate(chips) for i in range(n)]
        for cp in mine + sends:
            cp.start()
        for j, chip in enumerate(chips):
            for i in range(n):
                pltpu.make_async_remote_copy(
                    src_ref=ins[i].at[me], dst_ref=outs[i].at[2 * chip[0] + chip[1]], send_sem=send_sems.at[j, i],
                    recv_sem=recv_sems.at[j, i], device_id=(*chip, c), device_id_type=MESH_ID).wait_recv()
        for cp in sends:
            cp.wait_send()
        for cp in mine:
            cp.wait()

    out_shape = [_sds(a.shape, a.dtype) for a in arrays]
    return _run_exchange(body, name, arrays, out_shape, 3)


def _pair_sum(a, b, name):
    lead, rows, width = a.shape
    tr = _pick(rows, (256, 128, 64, 32, 16, 8)) if rows % 8 == 0 else rows

    def body(a_ref, b_ref, o_ref):
        o_ref[...] = (a_ref[...].astype(F32) + b_ref[...].astype(F32)).astype(o_ref.dtype)

    blk = pl.BlockSpec((None, tr, width), lambda l, i: (l, i, 0))
    return _call(body, name, (lead, rows // tr), [blk, blk], blk, _sds(a.shape, a.dtype))(a, b)


def _sum_adamw(gs, w, m, v, name):
    lead, rows, width = w.shape
    slots = gs.shape[0]
    tr = _pick(rows, (128, 64, 32, 16, 8)) if rows % 8 == 0 else rows

    def body(g_ref, w_ref, m_ref, v_ref, go_ref, d_ref, mo_ref, vo_ref):
        g = g_ref[0].astype(F32)
        for i in range(1, slots):
            g = g + g_ref[i].astype(F32)
        m2 = ADAM_B1 * m_ref[...] + (1.0 - ADAM_B1) * g
        v2 = ADAM_B2 * v_ref[...] + (1.0 - ADAM_B2) * (g * g)
        m_hat = m2 / (1.0 - ADAM_B1 ** ADAM_STEP)
        v_hat = v2 / (1.0 - ADAM_B2 ** ADAM_STEP)
        go_ref[...] = g
        d_ref[...] = -ADAM_LR * (m_hat / (jnp.sqrt(v_hat) + ADAM_EPS) + ADAM_WD * w_ref[...])
        mo_ref[...] = m2
        vo_ref[...] = v2

    blk = pl.BlockSpec((None, tr, width), lambda l, i: (l, i, 0))
    return _call(body, name, (lead, rows // tr),
                 [pl.BlockSpec((slots, None, tr, width), lambda l, i: (0, l, i, 0)), blk, blk, blk],
                 [blk] * 4, [_sds(w.shape, F32)] * 4)(gs, w, m, v)


MATMUL_WEIGHTS = ("w_in", "w_br_a", "w_br_b", "w_br_c", "w_out", "ffn_w_up", "ffn_w_down")
SPLIT = (
    ("w_in", (DEPTH, D_MODEL, 8720), 2),
    ("gdn_conv_w", (DEPTH, 4, 1536), 2), ("ssd_conv_w", (DEPTH, 4, 1024), 2),
    ("w_br_a", (DEPTH, 512, D_MODEL), 2), ("w_br_b", (DEPTH, 512, D_MODEL), 2), ("w_br_c", (DEPTH, 512, D_MODEL), 2),
    ("w_out", (DEPTH, D_MODEL, D_MODEL), 1), ("ffn_w_up", (DEPTH, D_MODEL, 2 * FFN_HIDDEN), 2),
    ("ffn_conv_w", (DEPTH, 3, 2 * FFN_HIDDEN), 2), ("ffn_w_down", (DEPTH, FFN_HIDDEN, D_MODEL), 1),
)
REPL = (
    ("b_ada", (DEPTH, 6 * D_MODEL)), ("norm1_w", (DEPTH, D_MODEL)), ("gdn_a_log", (DEPTH, 4)),
    ("gdn_dt_bias", (DEPTH, 4)), ("gdn_norm_w", (DEPTH, 128)), ("hgrn_lb_param", (DEPTH, 512)),
    ("hgrn_norm_w", (DEPTH, 128)), ("ssd_conv_b", (DEPTH, 1024)), ("ssd_a_log", (DEPTH, 8)),
    ("ssd_dt_bias", (DEPTH, 8)), ("ssd_d", (DEPTH, 8)), ("ssd_norm_w", (DEPTH, 512)), ("norm2_w", (DEPTH, D_MODEL)),
    ("ffn_conv_b", (DEPTH, 2 * FFN_HIDDEN)), ("final_norm_w", (D_MODEL,)),
)
WEIGHTS = ("w_ada", "b_ada", "norm1_w", "w_in", "gdn_conv_w", "gdn_a_log", "gdn_dt_bias", "gdn_norm_w",
           "hgrn_lb_param", "hgrn_norm_w", "ssd_conv_w", "ssd_conv_b", "ssd_a_log", "ssd_dt_bias", "ssd_d",
           "ssd_norm_w", "w_br_a", "w_br_b", "w_br_c", "w_out", "norm2_w", "ffn_w_up", "ffn_conv_w", "ffn_conv_b",
           "ffn_w_down", "final_norm_w")


def _block_shape(shape, axis):
    return tuple(d // N_DEV if i == axis else d for i, d in enumerate(shape))


def _join_blocks(gathered, shape, axis):
    return jnp.moveaxis(gathered, 0, axis).reshape(shape)


def _split_blocks(full, shape, axis):
    bs = _block_shape(shape, axis)
    t = full.reshape(shape[:axis] + (N_DEV, bs[axis]) + shape[axis + 1:])
    return jnp.moveaxis(t, axis, 0)


def _pack_repl(vals):
    parts = []
    for n, shape in REPL:
        size = math.prod(shape)
        parts.append(jnp.pad(vals[n].reshape(-1), (0, -(-size // PACK_W) * PACK_W - size)))
    cat = jnp.concatenate(parts)
    rows = -(-cat.shape[0] // (8 * PACK_W)) * 8
    return jnp.pad(cat, (0, rows * PACK_W - cat.shape[0])).reshape(rows, PACK_W)


def _unpack_repl(packed):
    flat, out, off = packed.reshape(-1), {}, 0
    for n, shape in REPL:
        size = math.prod(shape)
        out[n] = flat[off:off + size].reshape(shape)
        off += -(-size // PACK_W) * PACK_W
    return out


def _lane_row(vec, lane0):
    return jnp.pad(vec, (lane0, LANES - lane0 - vec.shape[0]))[None]


def _arrange_w_in(w):
    offs = [0]
    for sz in W_IN_SPLITS:
        offs.append(offs[-1] + sz)
    qkv, a, b, gz, hq, hf, hi, hg, sz_, xbc, dt, gate = [w[:, offs[i]:offs[i + 1]] for i in range(12)]
    pad = jnp.zeros((w.shape[0], P_WIDTH - P_SMALL - 16), w.dtype)
    return jnp.concatenate([qkv, gz, xbc, gate, hq, hf, hi, hg, sz_, a, b, dt, pad], axis=1)


def _restore_w_in(wp):
    cut = lambda o, n: wp[:, o:o + n]
    return jnp.concatenate([
        cut(P_QKV, 1536), cut(P_SMALL + SM_A, 4), cut(P_SMALL + SM_B, 4), cut(P_GZ, 512), cut(P_HQ, 512),
        cut(P_HF, 512), cut(P_HI, 512), cut(P_HG, 512), cut(P_SZ, 512), cut(P_XBC, 1024), cut(P_SMALL + SM_DT, 8),
        cut(P_GATE, 3072)], axis=1)


def _layer_consts(l, wf, wr, lower):
    t = lambda a: a.T
    k = {}
    k["n1w"], k["n2w"] = wr["norm1_w"][l][None], wr["norm2_w"][l][None]
    win = _arrange_w_in(wf["w_in"][l])
    k["win"], k["win_t"] = win, t(win)
    for n in ("w_br_a", "w_br_b", "w_br_c", "w_out", "ffn_w_up", "ffn_w_down"):
        k[n], k[n + "_t"] = wf[n][l], t(wf[n][l])
    k["gdn_conv_w"], k["gdn_conv_b"] = wf["gdn_conv_w"][l], jnp.zeros((1, 1536), F32)
    k["ssd_conv_w"], k["ssd_conv_b"] = wf["ssd_conv_w"][l], wr["ssd_conv_b"][l][None]
    k["ffn_conv_w"], k["ffn_conv_b"] = wf["ffn_conv_w"][l], wr["ffn_conv_b"][l][None]
    k["gdn_a"], k["gdn_dt"] = _lane_row(wr["gdn_a_log"][l], SM_A), _lane_row(wr["gdn_dt_bias"][l], SM_A)
    k["gdn_nw"], k["hgrn_nw"] = wr["gdn_norm_w"][l][None], wr["hgrn_norm_w"][l][None]
    k["ssd_a"], k["ssd_dt"] = _lane_row(wr["ssd_a_log"][l], SM_DT), _lane_row(wr["ssd_dt_bias"][l], SM_DT)
    k["ssd_d"] = jnp.repeat(wr["ssd_d"][l], SSD_HEAD_DIM)[None]
    k["ssd_nw"] = wr["ssd_norm_w"][l][None]
    k["lb"] = lower[l:l + 1]
    return k


def _layer_fwd(l, x, mod, k):
    bsz, s, d = x.shape
    t = bsz * s
    sv = {"x": x}
    sv["mod"] = [mod[:, None, i * d:(i + 1) * d] for i in range(6)]
    sh1, sc1, g1, sh2, sc2, g2 = sv["mod"]
    h1 = _norm_mod_fwd(x, k["n1w"], sh1, sc1, f"norm1_fwd{l}")
    p = _mm(h1.reshape(t, d), k["win"], F32, f"mm_in{l}").reshape(bsz, s, P_WIDTH)
    qkv_act = _conv_fwd(p, P_QKV, 1536, k["gdn_conv_w"], k["gdn_conv_b"], True, f"gdn_conv_fwd{l}")
    oa, st_a, ti_a = _gdn_fwd(qkv_act, p, k["gdn_a"], k["gdn_dt"], k["gdn_nw"], f"gdn_fwd{l}")
    ob, st_b = _hgrn_fwd(p, k["lb"], k["hgrn_nw"], f"hgrn_fwd{l}")
    xbc_act = _conv_fwd(p, P_XBC, 1024, k["ssd_conv_w"], k["ssd_conv_b"], True, f"ssd_conv_fwd{l}")
    oc, st_c = _ssd_fwd(xbc_act, p, k["ssd_a"], k["ssd_dt"], k["ssd_d"], k["ssd_nw"], f"ssd_fwd{l}")
    merged = _merge_fwd(p, oa, ob, oc, k["w_br_a"], k["w_br_b"], k["w_br_c"], f"merge_fwd{l}")
    mix = _mm(merged.reshape(t, d), k["w_out"], F32, f"mm_out{l}").reshape(bsz, s, d)
    x1 = _resid_fwd(x, mix, g1, f"resid1_fwd{l}")
    h2 = _norm_mod_fwd(x1, k["n2w"], sh2, sc2, f"norm2_fwd{l}")
    u_pre = _mm(h2.reshape(t, d), k["ffn_w_up"], F32, f"mm_up{l}").reshape(bsz, s, 2 * FFN_HIDDEN)
    a = _conv_glu_fwd(u_pre, k["ffn_conv_w"], k["ffn_conv_b"], f"ffn_conv_glu_fwd{l}")
    ffn = _mm(a.reshape(t, FFN_HIDDEN), k["ffn_w_down"], F32, f"mm_down{l}").reshape(bsz, s, d)
    x2 = _resid_fwd(x1, ffn, g2, f"resid2_fwd{l}")
    sv.update(h1=h1, p=p, qkv_act=qkv_act, oa=oa, st_a=st_a, ti_a=ti_a, ob=ob, st_b=st_b, xbc_act=xbc_act, oc=oc, st_c=st_c,
              merged=merged, mix=mix, x1=x1, h2=h2, u_pre=u_pre, a=a, ffn=ffn)
    return x2, sv


def _layer_bwd(l, dx2, k, sv):
    bsz, s, d = dx2.shape
    t = bsz * s
    f2 = 2 * FFN_HIDDEN
    sh1, sc1, g1, sh2, sc2, g2 = sv["mod"]
    tr = lambda a: a.reshape(t, -1).T
    g = {}
    dffn, dg2 = _gate_bwd(dx2, sv["ffn"], g2, f"gate2_bwd{l}")
    dffn2 = dffn.reshape(t, d)
    da = _mm(dffn2, k["ffn_w_down_t"], F32, f"mm_down_dx{l}").reshape(bsz, s, FFN_HIDDEN)
    g["ffn_w_down"] = _mm(tr(sv["a"]), dffn2, BF16, f"mm_down_dw{l}")
    du_pre, g["ffn_conv_w"], dfcb = _conv_glu_bwd(da, sv["u_pre"], k["ffn_conv_w"], k["ffn_conv_b"], f"ffn_conv_glu_bwd{l}")
    g["ffn_conv_b"] = dfcb[0]
    du2 = du_pre.reshape(t, f2)
    dh2 = _mm(du2, k["ffn_w_up_t"], F32, f"mm_up_dx{l}").reshape(bsz, s, d)
    g["ffn_w_up"] = _mm(tr(sv["h2"]), du2, BF16, f"mm_up_dw{l}")
    dx1, dn2w, dsh2, dsc2 = _norm_mod_bwd(sv["x1"], k["n2w"], sh2, sc2, dh2, dx2, f"norm2_bwd{l}")
    g["norm2_w"] = dn2w[0]
    dmix, dg1 = _gate_bwd(dx1, sv["mix"], g1, f"gate1_bwd{l}")
    dmix2 = dmix.reshape(t, d)
    dmerged = _mm(dmix2, k["w_out_t"], F32, f"mm_out_dx{l}").reshape(bsz, s, d)
    g["w_out"] = _mm(tr(sv["merged"]), dmix2, BF16, f"mm_out_dw{l}")
    p = sv["p"]
    dgate, doa, dob, doc, dya, dyb, dyc = _merge_bwd(
        dmerged, p, sv["oa"], sv["ob"], sv["oc"], k["w_br_a"], k["w_br_b"], k["w_br_c"],
        k["w_br_a_t"], k["w_br_b_t"], k["w_br_c_t"], f"merge_bwd{l}")
    g["w_br_a"] = _mm(tr(sv["oa"]), dya.reshape(t, d), BF16, f"mm_bra_dw{l}")
    g["w_br_b"] = _mm(tr(sv["ob"]), dyb.reshape(t, d), BF16, f"mm_brb_dw{l}")
    g["w_br_c"] = _mm(tr(sv["oc"]), dyc.reshape(t, d), BF16, f"mm_brc_dw{l}")
    dxbc_act, dsm_c, dsz, da_c, ddt_c, dd_c, dnw_c = _ssd_bwd(
        doc, sv["xbc_act"], p, k["ssd_a"], k["ssd_dt"], k["ssd_d"], k["ssd_nw"], sv["st_c"], f"ssd_bwd{l}")
    dxbc_raw, g["ssd_conv_w"], dscb = _conv_bwd(dxbc_act, p, P_XBC, 1024, k["ssd_conv_w"], k["ssd_conv_b"], True, f"ssd_conv_bwd{l}")
    g["ssd_conv_b"] = dscb[0]
    g["ssd_a_log"], g["ssd_dt_bias"] = da_c[0, SM_DT:SM_DT + 8], ddt_c[0, SM_DT:SM_DT + 8]
    g["ssd_d"] = dd_c.reshape(SSD_HEADS, SSD_HEAD_DIM).sum(axis=1)
    g["ssd_norm_w"] = dnw_c[0]
    dhg, dlb, dnw_b = _hgrn_bwd(dob, p, k["lb"], k["hgrn_nw"], sv["st_b"], f"hgrn_bwd{l}")
    g["hgrn_norm_w"] = dnw_b[0]
    dqkv_act, dsm_a, dgz, da_a, ddt_a, dnw_a = _gdn_bwd(
        doa, sv["qkv_act"], p, k["gdn_a"], k["gdn_dt"], k["gdn_nw"], sv["st_a"], sv["ti_a"], f"gdn_bwd{l}")
    dqkv_raw, g["gdn_conv_w"], _ = _conv_bwd(dqkv_act, p, P_QKV, 1536, k["gdn_conv_w"], k["gdn_conv_b"], True, f"gdn_conv_bwd{l}")
    g["gdn_a_log"], g["gdn_dt_bias"], g["gdn_norm_w"] = da_a[0, :4], ddt_a[0, :4], dnw_a[0]
    dsmall = jnp.pad((dsm_a + dsm_c).astype(BF16), ((0, 0), (0, 0), (0, P_WIDTH - P_SMALL - LANES)))
    dp = jnp.concatenate([dqkv_raw, dgz, dxbc_raw, dgate, dhg, dsz, dsmall], axis=-1).reshape(t, P_WIDTH)
    dh1 = _mm(dp, k["win_t"], F32, f"mm_in_dx{l}").reshape(bsz, s, d)
    g["w_in"] = _restore_w_in(_mm(tr(sv["h1"]), dp, BF16, f"mm_in_dw{l}"))
    dx, dn1w, dsh1, dsc1 = _norm_mod_bwd(sv["x"], k["n1w"], sh1, sc1, dh1, dx1, f"norm1_bwd{l}")
    g["norm1_w"] = dn1w[0]
    dmod = jnp.concatenate([dsh1, dsc1, dg1, dsh2, dsc2, dg2], axis=-1)[:, 0]
    return dx, g, dlb, dmod


def _local_step(x, mod, wf, wr, target):
    lower = _lb_fwd(wr["hgrn_lb_param"])
    ks = [_layer_consts(l, wf, wr, lower) for l in range(DEPTH)]
    saved = []
    h = x
    for l in range(DEPTH):
        h, sv = _layer_fwd(l, h, mod[l], ks[l])
        saved.append(sv)
    loss8, dh, dfnw = _final_loss(h, wr["final_norm_w"][None], target)
    per_layer, dlbs, dmods = [None] * DEPTH, [None] * DEPTH, [None] * DEPTH
    for l in reversed(range(DEPTH)):
        dh, per_layer[l], dlbs[l], dmods[l] = _layer_bwd(l, dh, ks[l], saved[l])
    grads = {n: jnp.stack([per_layer[l][n] for l in range(DEPTH)]) for n in per_layer[0]}
    grads["hgrn_lb_param"] = _lb_bwd(wr["hgrn_lb_param"], jnp.concatenate(dlbs, axis=0))
    grads["final_norm_w"] = dfnw[0]
    return loss8[0, 0], dh, grads, jnp.stack(dmods)


def kernel(x, c, w_ada, b_ada, norm1_w, w_in, gdn_conv_w, gdn_a_log, gdn_dt_bias, gdn_norm_w, hgrn_lb_param, hgrn_norm_w, ssd_conv_w, ssd_conv_b, ssd_a_log, ssd_dt_bias, ssd_d, ssd_norm_w, w_br_a, w_br_b, w_br_c, w_out, norm2_w, ffn_w_up, ffn_conv_w, ffn_conv_b, ffn_w_down, final_norm_w, loss_target, m_w_ada, m_b_ada, m_norm1_w, m_w_in, m_gdn_conv_w, m_gdn_a_log, m_gdn_dt_bias, m_gdn_norm_w, m_hgrn_lb_param, m_hgrn_norm_w, m_ssd_conv_w, m_ssd_conv_b, m_ssd_a_log, m_ssd_dt_bias, m_ssd_d, m_ssd_norm_w, m_w_br_a, m_w_br_b, m_w_br_c, m_w_out, m_norm2_w, m_ffn_w_up, m_ffn_conv_w, m_ffn_conv_b, m_ffn_w_down, m_final_norm_w, v_w_ada, v_b_ada, v_norm1_w, v_w_in, v_gdn_conv_w, v_gdn_a_log, v_gdn_dt_bias, v_gdn_norm_w, v_hgrn_lb_param, v_hgrn_norm_w, v_ssd_conv_w, v_ssd_conv_b, v_ssd_a_log, v_ssd_dt_bias, v_ssd_d, v_ssd_norm_w, v_w_br_a, v_w_br_b, v_w_br_c, v_w_out, v_norm2_w, v_ffn_w_up, v_ffn_conv_w, v_ffn_conv_b, v_ffn_w_down, v_final_norm_w):
    given = dict(locals())
    w = {n: given[n] for n in WEIGHTS}
    m = {n: given["m_" + n] for n in WEIGHTS}
    v = {n: given["v_" + n] for n in WEIGHTS}
    me = _my_index()
    bsz = c.shape[0]
    ncol = 6 * D_MODEL // N_DEV

    shards = [w[n].astype(BF16) if n in MATMUL_WEIGHTS else w[n] for n, _, _ in SPLIT] + [c]
    gathered = _gather_two_level(shards, "gather_weights")
    wf = {n: _join_blocks(g, shape, axis) for (n, shape, axis), g in zip(SPLIT, gathered)}
    c_all = gathered[-1].reshape(N_DEV * bsz, D_MODEL)

    b_cols = lax.dynamic_slice_in_dim(b_ada, me * ncol, ncol, axis=1)[:, None]
    mod_cols = _ada_fwd(c_all, w_ada, b_cols)
    send = mod_cols.reshape(DEPTH, N_DEV, bsz, ncol).transpose(1, 0, 2, 3)
    got = _exchange([send], "scatter_mod", False)[0]
    mod = got.transpose(1, 2, 0, 3).reshape(DEPTH, bsz, 6 * D_MODEL)

    loss, dx, grads, dmod = _local_step(x, mod, wf, w, loss_target)

    send = dmod.reshape(DEPTH, bsz, N_DEV, ncol).transpose(2, 0, 1, 3)
    got_dmod = _exchange([send], "scatter_dmod", False)[0]
    dmod_all = got_dmod.transpose(1, 0, 2, 3).reshape(DEPTH, N_DEV * bsz, ncol)
    g_w_ada, g_b_cols = _ada_bwd(c_all.T, dmod_all)

    core = lax.axis_index("c")
    by_core = []
    for n, shape, axis in SPLIT:
        parts = _split_blocks(grads[n], shape, axis).astype(BF16)
        parts = parts.reshape((N_DEV // 2, 2, -1, parts.shape[-1]))
        by_core.append(jnp.swapaxes(parts, 0, 1))
    from_sibling = _sibling_swap(by_core, "swap_grads")
    sums = [_pair_sum(lax.dynamic_index_in_dim(mine, core, 0, keepdims=False), theirs, f"pair_sum_{n}")
            for (n, _, _), mine, theirs in zip(SPLIT, by_core, from_sibling)]
    got = _chip_scatter(sums, "scatter_grads")
    grads["b_ada"] = lax.dynamic_update_slice_in_dim(jnp.zeros_like(b_ada), g_b_cols[:, 0], me * ncol, axis=1)

    out = {}
    slots = [(n, g8) for (n, _, _), g8 in zip(SPLIT, got)] + [("w_ada", g_w_ada[None])]
    for n, gs in slots:
        out[n] = _sum_adamw(gs.reshape((gs.shape[0],) + w[n].shape), w[n], m[n], v[n], f"adamw_{n}")
    r8 = _gather_two_level([_pack_repl(grads)], "gather_small_grads")[0]
    res = _sum_adamw(r8[:, None], _pack_repl(w)[None], _pack_repl(m)[None], _pack_repl(v)[None], "adamw_repl")
    repl_out = [_unpack_repl(o[0]) for o in res]
    pick = lambda i, n: out[n][i] if n in out else repl_out[i][n]
    loss = lax.psum(loss, ("x", "y", "c"))
    return (loss, dx, *[pick(i, n) for i in range(4) for n in WEIGHTS])
```

```python
import functools
import math

import jax
import jax.numpy as jnp
from jax import lax
from jax.experimental import pallas as pl
from jax.experimental.pallas import tpu as pltpu

F32, BF16 = jnp.float32, jnp.bfloat16
HI = lax.Precision.HIGHEST
MESH_ID = pl.DeviceIdType.MESH

N_DEV = 8
EPS = 1e-6
D_MODEL = 1024
DEPTH = 2
GDN_HEADS, GDN_DK, GDN_CHUNK = 4, 128, 64
HGRN_HEADS, HGRN_DK, HGRN_CHUNK, HGRN_BLOCK = 4, 128, 16, 128
SSD_HEADS, SSD_HEAD_DIM, SSD_GROUPS, SSD_STATE, SSD_CHUNK = 8, 64, 2, 128, 64
SSD_INNER = SSD_HEADS * SSD_HEAD_DIM
FFN_HIDDEN = 2816
LANES = 128
P_QKV, P_GZ, P_XBC, P_GATE, P_HQ, P_HF, P_HI, P_HG, P_SZ, P_SMALL, P_WIDTH = (
    0, 1536, 2048, 3072, 6144, 6656, 7168, 7680, 8192, 8704, 9216)
SM_A, SM_B, SM_DT = 0, 4, 8
W_IN_SPLITS = (1536, 4, 4, 512, 512, 512, 512, 512, 512, 1024, 8, 3072)

ADAM_LR, ADAM_B1, ADAM_B2, ADAM_EPS, ADAM_WD, ADAM_STEP = 0.001, 0.9, 0.999, 1e-08, 0.01, 10

V7X_VMEM_LIMIT = 56 * 1024 * 1024
PACK_W = 1024
PACK_ROWS = 128


def _call(body, name, grid, in_specs, out_specs, out_shape, scratch=()):
    return pl.pallas_call(
        body, name=name, grid=grid, in_specs=in_specs, out_specs=out_specs, out_shape=out_shape,
        scratch_shapes=list(scratch),
        compiler_params=pltpu.CompilerParams(
            dimension_semantics=("arbitrary",) * len(grid), vmem_limit_bytes=V7X_VMEM_LIMIT),
    )


def _pick(n, cands):
    for c in cands:
        if n % c == 0:
            return c
    raise ValueError(f"no tile for {n} among {cands}")


def _sds(shape, dtype):
    return jax.ShapeDtypeStruct(shape, dtype)


def _dot(a, b):
    return lax.dot_general(a, b, (((1,), (0,)), ((), ())), precision=HI, preferred_element_type=F32)


NN, NT, TN = (((1,), (0,)), ((), ())), (((1,), (1,)), ((), ())), (((0,), (0,)), ((), ()))


def _mxu(a, b, dims):
    return lax.dot_general(a.astype(BF16), b.astype(BF16), dims, preferred_element_type=F32)


@jax.custom_vjp
def _bdot(a, b):
    return _mxu(a, b, NN)


@jax.custom_vjp
def _bdot_nt(a, b):
    return _mxu(a, b, NT)


@jax.custom_vjp
def _bdot_tn(a, b):
    return _mxu(a, b, TN)


_bdot.defvjp(lambda a, b: (_mxu(a, b, NN), (a, b)), lambda r, d: (_mxu(d, r[1], NT), _mxu(r[0], d, TN)))
_bdot_nt.defvjp(lambda a, b: (_mxu(a, b, NT), (a, b)), lambda r, d: (_mxu(d, r[1], NN), _mxu(d, r[0], TN)))
_bdot_tn.defvjp(lambda a, b: (_mxu(a, b, TN), (a, b)), lambda r, d: (_mxu(r[1], d, NT), _mxu(r[0], d, NN)))


def _split(x):
    hi = x.astype(BF16)
    return hi, (x - hi.astype(F32)).astype(BF16)


def _mxu3(a, b, dims):
    ah, al = _split(a)
    bh, bl = _split(b)
    return _mxu(ah, bh, dims) + (_mxu(ah, bl, dims) + _mxu(al, bh, dims))


@jax.custom_vjp
def _dot3(a, b):
    return _mxu3(a, b, NN)


_dot3.defvjp(lambda a, b: (_mxu3(a, b, NN), (a, b)), lambda r, d: (_mxu3(d, r[1], NT), _mxu3(r[0], d, TN)))


def _pieces(x):
    x1 = x.astype(BF16)
    r1 = x - x1.astype(F32)
    x2 = r1.astype(BF16)
    return x1, x2, (r1 - x2.astype(F32)).astype(BF16)


def _mask_mxu(mask, x, dims):
    x1, x2, x3 = _pieces(x)
    return _mxu(mask, x1, dims) + (_mxu(mask, x2, dims) + _mxu(mask, x3, dims))


def _spread_mxu(x, mask, dims):
    x1, x2, x3 = _pieces(x)
    return _mxu(x1, mask, dims) + (_mxu(x2, mask, dims) + _mxu(x3, mask, dims))


@jax.custom_vjp
def _mask_dot(mask, x):
    return _mask_mxu(mask, x, NN)


@jax.custom_vjp
def _spread_dot(x, mask):
    return _spread_mxu(x, mask, NN)


_mask_dot.defvjp(lambda m, x: (_mask_mxu(m, x, NN), m), lambda m, d: (jnp.zeros_like(m), _mask_mxu(m, d, TN)))
_spread_dot.defvjp(lambda x, m: (_spread_mxu(x, m, NN), m), lambda m, d: (_spread_mxu(d, m, NT), jnp.zeros_like(m)))


def _iota(shape, axis):
    return lax.broadcasted_iota(jnp.int32, shape, axis)


def _silu(x):
    return x * jax.nn.sigmoid(x)


def _softplus(x):
    return jnp.maximum(x, 0.0) + jnp.log1p(jnp.exp(-jnp.abs(x)))


def _rms(x, w):
    return x * lax.rsqrt(jnp.mean(x * x, axis=-1, keepdims=True) + EPS) * w


def _lane_col(x, lane):
    m = (_iota(x.shape, 1) == lane).astype(F32)
    return jnp.sum(x * m, axis=1, keepdims=True)


def _col_to_row(c):
    n = c.shape[0]
    eye = (_iota((n, n), 0) == _iota((n, n), 1)).astype(F32)
    return jnp.sum(c * eye, axis=0, keepdims=True)


def _tril(n, strict=False):
    r, c = _iota((n, n), 0), _iota((n, n), 1)
    return (r > c) if strict else (r >= c)


def _mm(a, b, out_dtype, name):
    m, k = a.shape
    n = b.shape[1]
    tm = _pick(m, (1024, 1408, 512, 256, 128, 64, 32, 16, 8))
    tn = _pick(n, (1024, 1408, 768, 512, 384, 256, 128))
    tk = k if k <= 3072 else _pick(k, (1024, 768, 512, 384, 256, 128))
    nk = k // tk

    def body_one(a_ref, b_ref, o_ref):
        o_ref[...] = _bdot(a_ref[...], b_ref[...]).astype(out_dtype)

    def body(a_ref, b_ref, o_ref, acc_ref):
        kk = pl.program_id(2)

        @pl.when(kk == 0)
        def _():
            acc_ref[...] = jnp.zeros_like(acc_ref)

        acc_ref[...] += _bdot(a_ref[...], b_ref[...])

        @pl.when(kk == nk - 1)
        def _():
            o_ref[...] = acc_ref[...].astype(out_dtype)

    return _call(
        body_one if nk == 1 else body, name, (m // tm, n // tn, nk),
        [pl.BlockSpec((tm, tk), lambda i, j, kk: (i, kk)), pl.BlockSpec((tk, tn), lambda i, j, kk: (kk, j))],
        pl.BlockSpec((tm, tn), lambda i, j, kk: (i, j)), _sds((m, n), out_dtype),
        scratch=[] if nk == 1 else [pltpu.VMEM((tm, tn), F32)],
    )(a, b)


def _ada_fwd(c_all, w, b):
    depth, _, n = w.shape
    rows = c_all.shape[0]

    def body(c_ref, w_ref, b_ref, o_ref):
        o_ref[...] = _dot(_silu(c_ref[...]), w_ref[...]) + b_ref[...]

    return _call(
        body, "ada_fwd", (depth,),
        [pl.BlockSpec((rows, D_MODEL), lambda l: (0, 0)), pl.BlockSpec((None, D_MODEL, n), lambda l: (l, 0, 0)),
         pl.BlockSpec((None, 1, n), lambda l: (l, 0, 0))],
        pl.BlockSpec((None, rows, n), lambda l: (l, 0, 0)), _sds((depth, rows, n), F32),
    )(c_all, w, b)


def _ada_bwd(c_all_t, dmod):
    depth, rows, n = dmod.shape

    def body(ct_ref, dm_ref, dw_ref, db_ref):
        dm = dm_ref[...]
        dw_ref[...] = _dot(_silu(ct_ref[...]), dm)
        db_ref[...] = jnp.sum(dm, axis=0, keepdims=True)

    return _call(
        body, "ada_bwd", (depth,),
        [pl.BlockSpec((D_MODEL, rows), lambda l: (0, 0)), pl.BlockSpec((None, rows, n), lambda l: (l, 0, 0))],
        [pl.BlockSpec((None, D_MODEL, n), lambda l: (l, 0, 0)), pl.BlockSpec((None, 1, n), lambda l: (l, 0, 0))],
        [_sds((depth, D_MODEL, n), F32), _sds((depth, 1, n), F32)],
    )(c_all_t, dmod)


def _lb_fn(p):
    rows = [p[l:l + 1] for l in range(DEPTH)]
    mx = functools.reduce(jnp.maximum, rows)
    es = [jnp.exp(r - mx) for r in rows]
    tot = functools.reduce(lambda a, b: a + b, es)
    sm = [e / tot for e in es]
    out, run = [], None
    for l in range(DEPTH):
        run = sm[l] if run is None else run + sm[l]
        out.append(run - sm[0])
    return jnp.concatenate(out, axis=0)


def _lb_fwd(p):
    def body(p_ref, o_ref):
        o_ref[...] = _lb_fn(p_ref[...])

    full = pl.BlockSpec(p.shape, lambda i: (0, 0))
    return _call(body, "lb_fwd", (1,), [full], full, _sds(p.shape, F32))(p)


def _lb_bwd(p, d_lower):
    def body(p_ref, d_ref, o_ref):
        _, vjp = jax.vjp(_lb_fn, p_ref[...])
        o_ref[...] = vjp(d_ref[...])[0]

    full = pl.BlockSpec(p.shape, lambda i: (0, 0))
    return _call(body, "lb_bwd", (1,), [full, full], full, _sds(p.shape, F32))(p, d_lower)


def _norm_mod_fn(x, w, shift, scale):
    return _rms(x, w) * (1.0 + scale) + shift


def _norm_mod_fwd(x, w, shift, scale, name):
    bsz, s, d = x.shape
    ts = _pick(s, (256, 128, 64, 32, 16, 8))

    def body(x_ref, w_ref, sh_ref, sc_ref, o_ref):
        o_ref[...] = _norm_mod_fn(x_ref[...], w_ref[...], sh_ref[...], sc_ref[...]).astype(BF16)

    row = pl.BlockSpec((None, ts, d), lambda b, i: (b, i, 0))
    per_b = pl.BlockSpec((None, 1, d), lambda b, i: (b, 0, 0))
    return _call(body, name, (bsz, s // ts), [row, pl.BlockSpec((1, d), lambda b, i: (0, 0)), per_b, per_b],
                 row, _sds(x.shape, BF16))(x, w, shift, scale)


def _norm_mod_bwd(x, w, shift, scale, dh, carry, name):
    bsz, s, d = x.shape
    ts = _pick(s, (256, 128, 64, 32, 16, 8))

    def body(x_ref, w_ref, sh_ref, sc_ref, dh_ref, c_ref, dx_ref, dw_ref, dsh_ref, dsc_ref):
        b, i = pl.program_id(0), pl.program_id(1)
        _, vjp = jax.vjp(_norm_mod_fn, x_ref[...], w_ref[...], sh_ref[...], sc_ref[...])
        dx, dw, dsh, dsc = vjp(dh_ref[...])
        dx_ref[...] = dx + c_ref[...]

        @pl.when((b == 0) & (i == 0))
        def _():
            dw_ref[...] = jnp.zeros_like(dw_ref)

        @pl.when(i == 0)
        def _():
            dsh_ref[...] = jnp.zeros_like(dsh_ref)
            dsc_ref[...] = jnp.zeros_like(dsc_ref)

        dw_ref[...] += dw
        dsh_ref[...] += dsh
        dsc_ref[...] += dsc

    row = pl.BlockSpec((None, ts, d), lambda b, i: (b, i, 0))
    per_b = pl.BlockSpec((None, 1, d), lambda b, i: (b, 0, 0))
    wspec = pl.BlockSpec((1, d), lambda b, i: (0, 0))
    return _call(body, name, (bsz, s // ts), [row, wspec, per_b, per_b, row, row],
                 [row, wspec, per_b, per_b],
                 [_sds(x.shape, F32), _sds((1, d), F32), _sds((bsz, 1, d), F32), _sds((bsz, 1, d), F32)],
                 )(x, w, shift, scale, dh, carry)


def _resid_fwd(x, y, gate, name):
    bsz, s, d = x.shape
    ts = _pick(s, (512, 256, 128, 64, 32, 16, 8))

    def body(x_ref, y_ref, g_ref, o_ref):
        o_ref[...] = x_ref[...] + g_ref[...] * y_ref[...]

    row = pl.BlockSpec((None, ts, d), lambda b, i: (b, i, 0))
    per_b = pl.BlockSpec((None, 1, d), lambda b, i: (b, 0, 0))
    return _call(body, name, (bsz, s // ts), [row, row, per_b], row, _sds(x.shape, F32))(x, y, gate)


def _gate_bwd(dx, y, gate, name):
    bsz, s, d = dx.shape
    ts = _pick(s, (512, 256, 128, 64, 32, 16, 8))

    def body(dx_ref, y_ref, g_ref, dy_ref, dg_ref):
        dxv = dx_ref[...]
        dy_ref[...] = (dxv * g_ref[...]).astype(BF16)

        @pl.when(pl.program_id(1) == 0)
        def _():
            dg_ref[...] = jnp.zeros_like(dg_ref)

        dg_ref[...] += jnp.sum(dxv * y_ref[...], axis=0, keepdims=True)

    row = pl.BlockSpec((None, ts, d), lambda b, i: (b, i, 0))
    per_b = pl.BlockSpec((None, 1, d), lambda b, i: (b, 0, 0))
    return _call(body, name, (bsz, s // ts), [row, row, per_b], [row, per_b],
                 [_sds(dx.shape, BF16), _sds((bsz, 1, d), F32)])(dx, y, gate)


HALO = 8


def _conv_pre(xx, w_ref, b_ref, kw, rows):
    acc = w_ref[kw - 1:kw, :] * xx[HALO:HALO + rows]
    for k in range(kw - 1):
        acc = acc + w_ref[k:k + 1, :] * pltpu.roll(xx, kw - 1 - k, 0)[HALO:HALO + rows]
    return acc + b_ref[...]


def _conv_fwd(x, col0, width, w, b, act, name):
    bsz, s, _ = x.shape
    kw = w.shape[0]
    ts = _pick(s, (512, 256, 128, 64, 32, 16, 8))
    tc = _pick(width, (512, 256, 128))
    assert col0 % tc == 0
    c0 = col0 // tc
    hb = ts // HALO

    def body(x_ref, xp_ref, w_ref, b_ref, o_ref):
        i = pl.program_id(1)
        xp = jnp.where(i > 0, xp_ref[...], 0.0)
        xx = jnp.concatenate([xp, x_ref[...]], axis=0)
        pre = _conv_pre(xx, w_ref, b_ref, kw, ts)
        o_ref[...] = _silu(pre) if act else pre

    return _call(
        body, name, (bsz, s // ts, width // tc),
        [pl.BlockSpec((None, ts, tc), lambda bb, i, j: (bb, i, c0 + j)),
         pl.BlockSpec((None, HALO, tc), lambda bb, i, j: (bb, jnp.maximum(i * hb - 1, 0), c0 + j)),
         pl.BlockSpec((kw, tc), lambda bb, i, j: (0, j)), pl.BlockSpec((1, tc), lambda bb, i, j: (0, j))],
        pl.BlockSpec((None, ts, tc), lambda bb, i, j: (bb, i, j)), _sds((bsz, s, width), F32),
    )(x, x, w, b)


def _conv_bwd(dy, x, col0, width, w, b, act, name):
    bsz, s, _ = x.shape
    kw = w.shape[0]
    ts = _pick(s, (512, 256, 128, 64, 32, 16, 8))
    tc = _pick(width, (512, 256, 128))
    c0 = col0 // tc
    hb = ts // HALO
    nt = s // ts
    last_h = s // HALO - 1

    def body(x_ref, xp_ref, xn_ref, dy_ref, dyn_ref, w_ref, b_ref, dx_ref, dw_ref, db_ref):
        bb, i = pl.program_id(1), pl.program_id(2)
        xp = jnp.where(i > 0, xp_ref[...], 0.0)
        xx = jnp.concatenate([xp, x_ref[...], xn_ref[...]], axis=0)
        dyy = jnp.concatenate([dy_ref[...], jnp.where(i < nt - 1, dyn_ref[...], 0.0)], axis=0)
        n = ts + HALO
        if act:
            pre = _conv_pre(xx, w_ref, b_ref, kw, n)
            sg = jax.nn.sigmoid(pre)
            dpre = dyy * (sg * (1.0 + pre * (1.0 - sg)))
        else:
            dpre = dyy
        dx = w_ref[kw - 1:kw, :] * dpre[:ts]
        for k in range(kw - 1):
            dx = dx + w_ref[k:k + 1, :] * pltpu.roll(dpre, n - (kw - 1 - k), 0)[:ts]
        dx_ref[...] = dx.astype(BF16)

        @pl.when((bb == 0) & (i == 0))
        def _():
            dw_ref[...] = jnp.zeros_like(dw_ref)
            db_ref[...] = jnp.zeros_like(db_ref)

        dt = dpre[:ts]
        db_ref[...] += jnp.sum(dt, axis=0, keepdims=True)
        dw_ref[kw - 1:kw, :] += jnp.sum(dt * xx[HALO:HALO + ts], axis=0, keepdims=True)
        for k in range(kw - 1):
            xs = pltpu.roll(xx, kw - 1 - k, 0)[HALO:HALO + ts]
            dw_ref[k:k + 1, :] += jnp.sum(dt * xs, axis=0, keepdims=True)

    xspec = lambda f: pl.BlockSpec((None, HALO, tc), f)
    return _call(
        body, name, (width // tc, bsz, nt),
        [pl.BlockSpec((None, ts, tc), lambda j, bb, i: (bb, i, c0 + j)),
         xspec(lambda j, bb, i: (bb, jnp.maximum(i * hb - 1, 0), c0 + j)),
         xspec(lambda j, bb, i: (bb, jnp.minimum((i + 1) * hb, last_h), c0 + j)),
         pl.BlockSpec((None, ts, tc), lambda j, bb, i: (bb, i, j)),
         xspec(lambda j, bb, i: (bb, jnp.minimum((i + 1) * hb, last_h), j)),
         pl.BlockSpec((kw, tc), lambda j, bb, i: (0, j)), pl.BlockSpec((1, tc), lambda j, bb, i: (0, j))],
        [pl.BlockSpec((None, ts, tc), lambda j, bb, i: (bb, i, j)),
         pl.BlockSpec((kw, tc), lambda j, bb, i: (0, j)), pl.BlockSpec((1, tc), lambda j, bb, i: (0, j))],
        [_sds((bsz, s, width), BF16), _sds((kw, width), F32), _sds((1, width), F32)],
    )(x, x, x, dy, dy, w, b)


def _conv_glu_fwd(x, w, b, name):
    bsz, s, f2 = x.shape
    f = f2 // 2
    kw = w.shape[0]
    ts = _pick(s, (512, 256, 128, 64, 32, 16, 8))
    tc = _pick(f, (256, 128))
    nf = f // tc
    hb = ts // HALO

    def body(xg_ref, xgp_ref, xv_ref, xvp_ref, wg_ref, wv_ref, bg_ref, bv_ref, o_ref):
        i = pl.program_id(1)
        halves = []
        for x_ref, xp_ref, w_ref, b_ref in ((xg_ref, xgp_ref, wg_ref, bg_ref), (xv_ref, xvp_ref, wv_ref, bv_ref)):
            xx = jnp.concatenate([jnp.where(i > 0, xp_ref[...], 0.0), x_ref[...]], axis=0)
            halves.append(_conv_pre(xx, w_ref, b_ref, kw, ts))
        o_ref[...] = (_silu(halves[0]) * halves[1]).astype(BF16)

    tile = lambda off: pl.BlockSpec((None, ts, tc), lambda bb, i, j: (bb, i, off + j))
    prev = lambda off: pl.BlockSpec((None, HALO, tc), lambda bb, i, j: (bb, jnp.maximum(i * hb - 1, 0), off + j))
    wsp = lambda rows, off: pl.BlockSpec((rows, tc), lambda bb, i, j: (0, off + j))
    return _call(
        body, name, (bsz, s // ts, nf),
        [tile(0), prev(0), tile(nf), prev(nf), wsp(kw, 0), wsp(kw, nf), wsp(1, 0), wsp(1, nf)],
        pl.BlockSpec((None, ts, tc), lambda bb, i, j: (bb, i, j)), _sds((bsz, s, f), BF16),
    )(x, x, x, x, w, w, b, b)


def _conv_glu_bwd(da, x, w, b, name):
    bsz, s, f2 = x.shape
    f = f2 // 2
    kw = w.shape[0]
    ts = _pick(s, (512, 256, 128, 64, 32, 16, 8))
    tc = _pick(f, (256, 128))
    nf = f // tc
    hb = ts // HALO
    nt = s // ts
    last_h = s // HALO - 1
    n = ts + HALO

    def body(xg_ref, xgp_ref, xgn_ref, xv_ref, xvp_ref, xvn_ref, da_ref, dan_ref,
             wg_ref, wv_ref, bg_ref, bv_ref, wx_ref, dx_ref, dw_ref, db_ref):
        j, bb, i = pl.program_id(0), pl.program_id(1), pl.program_id(2)
        day = jnp.concatenate([da_ref[...], jnp.where(i < nt - 1, dan_ref[...], 0.0)], axis=0)
        xg = jnp.concatenate([jnp.where(i > 0, xgp_ref[...], 0.0), xg_ref[...], xgn_ref[...]], axis=0)
        pre_g = _conv_pre(xg, wg_ref, bg_ref, kw, n)
        sg = jax.nn.sigmoid(pre_g)

        @pl.when((bb == 0) & (i == 0))
        def _():
            dw_ref[...] = jnp.zeros_like(dw_ref)
            db_ref[...] = jnp.zeros_like(db_ref)

        def finish(dpre, xx):
            dx = wx_ref[kw - 1:kw, :] * dpre[:ts]
            for k in range(kw - 1):
                dx = dx + wx_ref[k:k + 1, :] * pltpu.roll(dpre, n - (kw - 1 - k), 0)[:ts]
            dx_ref[...] = dx.astype(BF16)
            dt = dpre[:ts]
            db_ref[...] += jnp.sum(dt, axis=0, keepdims=True)
            dw_ref[kw - 1:kw, :] += jnp.sum(dt * xx[HALO:HALO + ts], axis=0, keepdims=True)
            for k in range(kw - 1):
                dw_ref[k:k + 1, :] += jnp.sum(dt * pltpu.roll(xx, kw - 1 - k, 0)[HALO:HALO + ts], axis=0, keepdims=True)

        @pl.when(j < nf)
        def _():
            xv = jnp.concatenate([jnp.where(i > 0, xvp_ref[...], 0.0), xv_ref[...], xvn_ref[...]], axis=0)
            pre_v = _conv_pre(xv, wv_ref, bv_ref, kw, n)
            finish(day * pre_v * (sg * (1.0 + pre_g * (1.0 - sg))), xg)

        @pl.when(j >= nf)
        def _():
            xv = jnp.concatenate([jnp.where(i > 0, xvp_ref[...], 0.0), xv_ref[...], xvn_ref[...]], axis=0)
            finish(day * (pre_g * sg), xv)

    tile = lambda off: pl.BlockSpec((None, ts, tc), lambda j, bb, i: (bb, i, off + j % nf))
    prev = lambda off: pl.BlockSpec((None, HALO, tc), lambda j, bb, i: (bb, jnp.maximum(i * hb - 1, 0), off + j % nf))
    nxt = lambda off: pl.BlockSpec((None, HALO, tc), lambda j, bb, i: (bb, jnp.minimum((i + 1) * hb, last_h), off + j % nf))
    wsp = lambda rows, off: pl.BlockSpec((rows, tc), lambda j, bb, i: (0, off + j % nf))
    own = lambda rows: pl.BlockSpec((rows, tc), lambda j, bb, i: (0, j))
    return _call(
        body, name, (2 * nf, bsz, nt),
        [tile(0), prev(0), nxt(0), tile(nf), prev(nf), nxt(nf), tile(0), nxt(0),
         wsp(kw, 0), wsp(kw, nf), wsp(1, 0), wsp(1, nf), own(kw)],
        [pl.BlockSpec((None, ts, tc), lambda j, bb, i: (bb, i, j)), own(kw), own(1)],
        [_sds((bsz, s, f2), BF16), _sds((kw, f2), F32), _sds((1, f2), F32)],
    )(x, x, x, x, x, x, da, da, w, w, b, b, w)


def _merge_fwd(p, oa, ob, oc, wa, wb, wc, name):
    bsz, s, _ = p.shape
    tm = _pick(s, (256, 128, 64, 32, 16, 8))
    gblk = P_GATE // (3 * D_MODEL)

    def body(g_ref, oa_ref, ob_ref, oc_ref, wa_ref, wb_ref, wc_ref, o_ref):
        acc = None
        for i, (o_r, w_r) in enumerate(((oa_ref, wa_ref), (ob_ref, wb_ref), (oc_ref, wc_ref))):
            y = _bdot(o_r[...], w_r[...])
            t = jax.nn.sigmoid(g_ref[:, i * D_MODEL:(i + 1) * D_MODEL]) * y
            acc = t if acc is None else acc + t
        o_ref[...] = acc.astype(BF16)

    orow = pl.BlockSpec((None, tm, 512), lambda b, i: (b, i, 0))
    wfull = pl.BlockSpec((512, D_MODEL), lambda b, i: (0, 0))
    return _call(
        body, name, (bsz, s // tm),
        [pl.BlockSpec((None, tm, 3 * D_MODEL), lambda b, i: (b, i, gblk)), orow, orow, orow, wfull, wfull, wfull],
        pl.BlockSpec((None, tm, D_MODEL), lambda b, i: (b, i, 0)), _sds((bsz, s, D_MODEL), BF16),
    )(p, oa, ob, oc, wa, wb, wc)


def _merge_bwd(dm, p, oa, ob, oc, wa, wb, wc, wat, wbt, wct, name):
    bsz, s, _ = p.shape
    tm = _pick(s, (256, 128, 64, 32, 16, 8))
    gblk = P_GATE // (3 * D_MODEL)

    def body(dm_ref, g_ref, oa_ref, ob_ref, oc_ref, wa_ref, wb_ref, wc_ref, wat_ref, wbt_ref, wct_ref,
             dg_ref, doa_ref, dob_ref, doc_ref, dya_ref, dyb_ref, dyc_ref):
        dmv = dm_ref[...]
        trip = ((oa_ref, wa_ref, wat_ref, doa_ref, dya_ref), (ob_ref, wb_ref, wbt_ref, dob_ref, dyb_ref),
                (oc_ref, wc_ref, wct_ref, doc_ref, dyc_ref))
        for i, (o_r, w_r, wt_r, do_r, dy_r) in enumerate(trip):
            y = _bdot(o_r[...], w_r[...])
            sg = jax.nn.sigmoid(g_ref[:, i * D_MODEL:(i + 1) * D_MODEL])
            dg_ref[:, i * D_MODEL:(i + 1) * D_MODEL] = (dmv * y * sg * (1.0 - sg)).astype(BF16)
            dy = (dmv * sg).astype(BF16)
            dy_r[...] = dy
            do_r[...] = _bdot(dy, wt_r[...])

    orow = pl.BlockSpec((None, tm, 512), lambda b, i: (b, i, 0))
    drow = pl.BlockSpec((None, tm, D_MODEL), lambda b, i: (b, i, 0))
    grow = pl.BlockSpec((None, tm, 3 * D_MODEL), lambda b, i: (b, i, 0))
    wfull = pl.BlockSpec((512, D_MODEL), lambda b, i: (0, 0))
    wtfull = pl.BlockSpec((D_MODEL, 512), lambda b, i: (0, 0))
    return _call(
        body, name, (bsz, s // tm),
        [drow, pl.BlockSpec((None, tm, 3 * D_MODEL), lambda b, i: (b, i, gblk)), orow, orow, orow,
         wfull, wfull, wfull, wtfull, wtfull, wtfull],
        [grow, orow, orow, orow, drow, drow, drow],
        [_sds((bsz, s, 3 * D_MODEL), BF16)] + [_sds((bsz, s, 512), F32)] * 3 + [_sds((bsz, s, D_MODEL), BF16)] * 3,
    )(dm, p, oa, ob, oc, wa, wb, wc, wat, wbt, wct)


def _final_loss(x, w, target):
    bsz, s, d = x.shape
    ts = _pick(s, (256, 128, 64, 32, 16, 8))

    def body(x_ref, w_ref, t_ref, loss_ref, dx_ref, dw_ref):
        first = (pl.program_id(0) == 0) & (pl.program_id(1) == 0)
        y, vjp = jax.vjp(_rms, x_ref[...], w_ref[...])
        err = y - t_ref[...]
        dx, dw = vjp(err * (1.0 / d))
        dx_ref[...] = dx

        @pl.when(first)
        def _():
            loss_ref[...] = jnp.zeros_like(loss_ref)
            dw_ref[...] = jnp.zeros_like(dw_ref)

        loss_ref[...] += 0.5 * jnp.sum(jnp.sum(err * err, axis=1, keepdims=True), axis=0, keepdims=True) * (1.0 / d)
        dw_ref[...] += dw

    row = pl.BlockSpec((None, ts, d), lambda b, i: (b, i, 0))
    wspec = pl.BlockSpec((1, d), lambda b, i: (0, 0))
    return _call(body, "final_loss", (bsz, s // ts), [row, wspec, row],
                 [pl.BlockSpec((8, LANES), lambda b, i: (0, 0)), row, wspec],
                 [_sds((8, LANES), F32), _sds(x.shape, F32), _sds((1, d), F32)])(x, w, target)


def _unit_lower_inverses(ms):
    n = ms[0].shape[0]
    eye = (_iota((n, n), 0) == _iota((n, n), 1)).astype(F32)
    ps = [-m for m in ms]
    xs = [eye + p for p in ps]
    for _ in range(int(math.log2(n)) - 1):
        ps = [_mxu3(p, p, NN) for p in ps]
        xs = [x + _mxu3(x, p, NN) for x, p in zip(xs, ps)]
    return xs


@jax.custom_vjp
def _known_inverse(m, t):
    return t


_known_inverse.defvjp(lambda m, t: (t, t), lambda t, dt: (-_mxu3(t, _mxu3(dt, t, NT), TN), jnp.zeros_like(t)))


def _gdn_chunk(states, qkv, small, z, a_row, dt_row, nw, tinvs=None):
    nb = len(qkv)
    c = qkv[0].shape[0]
    kw = GDN_HEADS * GDN_DK
    incl, strict = _tril(c), _tril(c, True)
    g_all = [-jnp.exp(a_row) * _softplus(small[b] + dt_row) for b in range(nb)]
    beta_all = [jax.nn.sigmoid(small[b]) for b in range(nb)]
    big_g_all = [_mask_dot(incl.astype(BF16), g_all[b]) for b in range(nb)]
    items = [(b, h) for b in range(nb) for h in range(GDN_HEADS)]
    ids = range(len(items))
    col = lambda b, part, h: qkv[b][:, part * kw + h * GDN_DK:part * kw + (h + 1) * GDN_DK]
    unit = lambda t: t * lax.rsqrt(jnp.sum(t * t, axis=-1, keepdims=True) + EPS)
    q = [unit(col(b, 0, h)) * (GDN_DK ** -0.5) for b, h in items]
    k = [unit(col(b, 1, h)) for b, h in items]
    v = [col(b, 2, h) for b, h in items]
    gc = [_lane_col(big_g_all[b], SM_A + h) for b, h in items]
    bc = [_lane_col(beta_all[b], SM_B + h) for b, h in items]
    g_last = [jnp.sum(_lane_col(g_all[b], SM_A + h), axis=0, keepdims=True) for b, h in items]
    decay = [jnp.where(incl, jnp.exp(jnp.where(incl, gc[i] - _col_to_row(gc[i]), 0.0)), 0.0) for i in ids]
    kb = [k[i] * bc[i] for i in ids]
    m = [jnp.where(strict, _bdot_nt(kb[i], k[i]) * decay[i], 0.0) for i in ids]
    if tinvs is None:
        tinv = _unit_lower_inverses(m)
    else:
        tinv = [_known_inverse(m[i], tinvs[i]) for i in ids]
    eg = [jnp.exp(gc[i]) for i in ids]
    u = [_dot3(tinv[i], v[i] * bc[i]) for i in ids]
    w = [_dot3(tinv[i], kb[i] * eg[i]) for i in ids]
    attn = [_bdot_nt(q[i], k[i]) * decay[i] for i in ids]
    v_new = [u[i] - _bdot(w[i], states[i]) for i in ids]
    o_st = [_bdot(q[i] * eg[i], states[i]) for i in ids]
    o = [o_st[i] + _bdot(attn[i], v_new[i]) for i in ids]
    grow = [_bdot_tn(k[i] * jnp.exp(g_last[i] - gc[i]), v_new[i]) for i in ids]
    new_states = [states[i] * jnp.exp(g_last[i]) + grow[i] for i in ids]
    outs = [_rms(o[i], nw) * _silu(z[b][:, h * GDN_DK:(h + 1) * GDN_DK]) for i, (b, h) in enumerate(items)]
    per_seq = [jnp.concatenate(outs[b * GDN_HEADS:(b + 1) * GDN_HEADS], axis=1) for b in range(nb)]
    return new_states, per_seq, tinv


def _seq_items(bsz, heads):
    return [(b, h) for b in range(bsz) for h in range(heads)]


def _gdn_fwd(qkv_act, p, a_row, dt_row, nw, name):
    bsz, s, _ = qkv_act.shape
    c = GDN_CHUNK
    nc = s // c
    items = _seq_items(bsz, GDN_HEADS)

    def body(qkv_ref, sm_ref, z_ref, a_ref, dt_ref, nw_ref, o_ref, st_ref, ti_ref, st_scr):
        @pl.when(pl.program_id(0) == 0)
        def _():
            st_scr[...] = jnp.zeros_like(st_scr)

        st_ref[...] = st_scr[...]
        seqs = range(bsz)
        new_states, o, tinvs = _gdn_chunk(
            [st_scr[b, h] for b, h in items], [qkv_ref[b] for b in seqs], [sm_ref[b] for b in seqs],
            [z_ref[b] for b in seqs], a_ref[...], dt_ref[...], nw_ref[...])
        for i, (b, h) in enumerate(items):
            st_scr[b, h] = new_states[i]
            ti_ref[b, h] = tinvs[i]
        for b in seqs:
            o_ref[b] = o[b].astype(BF16)

    row = lambda w, blk: pl.BlockSpec((bsz, c, w), lambda n, blk=blk: (0, n, blk))
    prm = pl.BlockSpec((1, LANES), lambda n: (0, 0))
    return _call(
        body, name, (nc,), [row(1536, 0), row(LANES, P_SMALL // LANES), row(512, P_GZ // 512), prm, prm, prm],
        [row(512, 0), pl.BlockSpec((bsz, None, 4, LANES, LANES), lambda n: (0, n, 0, 0, 0)),
         pl.BlockSpec((bsz, None, 4, c, c), lambda n: (0, n, 0, 0, 0))],
        [_sds((bsz, s, 512), BF16), _sds((bsz, nc, 4, LANES, LANES), F32), _sds((bsz, nc, 4, c, c), F32)],
        scratch=[pltpu.VMEM((bsz, 4, LANES, LANES), F32)],
    )(qkv_act, p, p, a_row, dt_row, nw)


def _gdn_bwd(do, qkv_act, p, a_row, dt_row, nw, st_all, ti_all, name):
    bsz, s, _ = qkv_act.shape
    c = GDN_CHUNK
    nc = s // c

    items = _seq_items(bsz, GDN_HEADS)

    def body(qkv_ref, sm_ref, z_ref, a_ref, dt_ref, nw_ref, do_ref, st_ref, ti_ref,
             dqkv_ref, dsm_ref, dz_ref, da_ref, ddt_ref, dnw_ref, ds_scr):
        @pl.when(pl.program_id(0) == 0)
        def _():
            ds_scr[...] = jnp.zeros_like(ds_scr)
            da_ref[...] = jnp.zeros_like(da_ref)
            ddt_ref[...] = jnp.zeros_like(ddt_ref)
            dnw_ref[...] = jnp.zeros_like(dnw_ref)

        seqs = range(bsz)
        tinvs = [ti_ref[b, h] for b, h in items]
        chunk = lambda *a: _gdn_chunk(*a, tinvs=tinvs)[:2]
        _, vjp = jax.vjp(chunk, [st_ref[b, h] for b, h in items], [qkv_ref[b] for b in seqs], [sm_ref[b] for b in seqs],
                         [z_ref[b] for b in seqs], a_ref[...], dt_ref[...], nw_ref[...])
        d_states, dqkv, dsm, dz, da, ddt, dnw = vjp(([ds_scr[b, h] for b, h in items], [do_ref[b] for b in seqs]))
        for i, (b, h) in enumerate(items):
            ds_scr[b, h] = d_states[i]
        for b in seqs:
            dqkv_ref[b] = dqkv[b]
            dsm_ref[b] = dsm[b]
            dz_ref[b] = dz[b].astype(BF16)
        da_ref[...] += da
        ddt_ref[...] += ddt
        dnw_ref[...] += dnw

    rrow = lambda w, blk: pl.BlockSpec((bsz, c, w), lambda n, blk=blk: (0, nc - 1 - n, blk))
    prm = pl.BlockSpec((1, LANES), lambda n: (0, 0))
    return _call(
        body, name, (nc,),
        [rrow(1536, 0), rrow(LANES, P_SMALL // LANES), rrow(512, P_GZ // 512), prm, prm, prm, rrow(512, 0),
         pl.BlockSpec((bsz, None, 4, LANES, LANES), lambda n: (0, nc - 1 - n, 0, 0, 0)),
         pl.BlockSpec((bsz, None, 4, c, c), lambda n: (0, nc - 1 - n, 0, 0, 0))],
        [rrow(1536, 0), rrow(LANES, 0), rrow(512, 0), prm, prm, prm],
        [_sds((bsz, s, 1536), F32), _sds((bsz, s, LANES), F32), _sds((bsz, s, 512), BF16)] + [_sds((1, LANES), F32)] * 3,
        scratch=[pltpu.VMEM((bsz, 4, LANES, LANES), F32)],
    )(qkv_act, p, p, a_row, dt_row, nw, do, st_all, ti_all)


def _hgrn_block(states, q_raw, f_raw, i_raw, g_raw, lb, nw):
    n = q_raw[0].shape[0]
    c = HGRN_CHUNK
    r, cc = _iota((n, n), 0), _iota((n, n), 1)
    same = (r // c) == (cc // c)
    causal = same & (r >= cc)
    ref_row = (r // c) * c + (c // 2 - 1)
    run_sum = causal.astype(F32)
    rel_sum = run_sum - (same & (ref_row >= cc)).astype(F32)
    sums = jnp.concatenate([run_sum, rel_sum, same.astype(F32)], axis=0).astype(BF16)
    seqs, chunks = range(len(q_raw)), range(n // c)
    items = _seq_items(len(q_raw), HGRN_HEADS)
    hs = lambda t, h: t[:, h * HGRN_DK:(h + 1) * HGRN_DK]
    rows = lambda t, j: t[j * c:(j + 1) * c]
    q = [_silu(q_raw[b]) for b in seqs]
    logf = [jnp.log(lb + (1.0 - lb) * jax.nn.sigmoid(f_raw[b])) for b in seqs]
    k = [(1.0 - lb) * jax.nn.sigmoid(-f_raw[b]) for b in seqs]
    all_sums = [_mask_dot(sums, logf[b]) for b in seqs]
    big_g, g_rel, g_tot = ([t[i * n:(i + 1) * n] for t in all_sums] for i in range(3))
    q_rel = [q[b] * jnp.exp(g_rel[b]) for b in seqs]
    k_rel = [k[b] * jnp.exp(-g_rel[b]) for b in seqs]
    qg = [q[b] * jnp.exp(big_g[b]) for b in seqs]
    k_end = [k[b] * jnp.exp(g_tot[b] - big_g[b]) for b in seqs]
    keep = [[jnp.exp(g_tot[b][j * c:j * c + 1]) for j in chunks] for b in seqs]
    scores = [_bdot_nt(hs(q_rel[b], h), hs(k_rel[b], h)) for b, h in items]
    o_intra = [_bdot(jnp.where(causal, scores[i], 0.0), hs(i_raw[b], h)) for i, (b, h) in enumerate(items)]
    grow = [[_bdot_tn(rows(hs(i_raw[b], h), j), rows(hs(k_end[b], h), j)) for j in chunks] for b, h in items]
    entering, new_states = [], []
    for i, (b, h) in enumerate(items):
        st, per_chunk = states[i], []
        for j in chunks:
            per_chunk.append(st)
            st = st * hs(keep[b][j], h) + grow[i][j]
        entering.append(per_chunk)
        new_states.append(st)
    o_inter = [[_bdot_nt(rows(hs(qg[b], h), j), entering[i][j]) for j in chunks] for i, (b, h) in enumerate(items)]
    outs = [_rms(o_intra[i] + jnp.concatenate(o_inter[i], axis=0), nw) * _silu(hs(g_raw[b], h))
            for i, (b, h) in enumerate(items)]
    return new_states, [jnp.concatenate(outs[b * HGRN_HEADS:(b + 1) * HGRN_HEADS], axis=1) for b in seqs]


def _hgrn_fwd(p, lb, nw, name):
    bsz, s, _ = p.shape
    n = HGRN_BLOCK
    nb = s // n

    items = _seq_items(bsz, HGRN_HEADS)

    def body(q_ref, f_ref, i_ref, g_ref, lb_ref, nw_ref, o_ref, st_ref, st_scr):
        @pl.when(pl.program_id(0) == 0)
        def _():
            st_scr[...] = jnp.zeros_like(st_scr)

        st_ref[...] = st_scr[...]
        per_seq = lambda ref: [ref[b] for b in range(bsz)]
        new_states, o = _hgrn_block([st_scr[b, h] for b, h in items], per_seq(q_ref), per_seq(f_ref), per_seq(i_ref),
                                    per_seq(g_ref), lb_ref[...], nw_ref[...])
        for i, (b, h) in enumerate(items):
            st_scr[b, h] = new_states[i]
        for b in range(bsz):
            o_ref[b] = o[b].astype(BF16)

    row = lambda blk: pl.BlockSpec((bsz, n, 512), lambda i, blk=blk: (0, i, blk))
    return _call(
        body, name, (nb,),
        [row(P_HQ // 512), row(P_HF // 512), row(P_HI // 512), row(P_HG // 512),
         pl.BlockSpec((1, 512), lambda i: (0, 0)), pl.BlockSpec((1, LANES), lambda i: (0, 0))],
        [row(0), pl.BlockSpec((bsz, None, 4, LANES, LANES), lambda i: (0, i, 0, 0, 0))],
        [_sds((bsz, s, 512), BF16), _sds((bsz, nb, 4, LANES, LANES), F32)],
        scratch=[pltpu.VMEM((bsz, 4, LANES, LANES), F32)],
    )(p, p, p, p, lb, nw)


def _hgrn_bwd(do, p, lb, nw, st_all, name):
    bsz, s, _ = p.shape
    n = HGRN_BLOCK
    nb = s // n

    items = _seq_items(bsz, HGRN_HEADS)

    def body(q_ref, f_ref, i_ref, g_ref, lb_ref, nw_ref, do_ref, st_ref, dp_ref, dlb_ref, dnw_ref, ds_scr):
        @pl.when(pl.program_id(0) == 0)
        def _():
            ds_scr[...] = jnp.zeros_like(ds_scr)
            dlb_ref[...] = jnp.zeros_like(dlb_ref)
            dnw_ref[...] = jnp.zeros_like(dnw_ref)

        per_seq = lambda ref: [ref[b] for b in range(bsz)]
        _, vjp = jax.vjp(_hgrn_block, [st_ref[b, h] for b, h in items], per_seq(q_ref), per_seq(f_ref), per_seq(i_ref),
                         per_seq(g_ref), lb_ref[...], nw_ref[...])
        d_states, dq, df, di, dg, dlb, dnw = vjp(([ds_scr[b, h] for b, h in items], per_seq(do_ref)))
        for i, (b, h) in enumerate(items):
            ds_scr[b, h] = d_states[i]
        for b in range(bsz):
            for j, t in enumerate((dq, df, di, dg)):
                dp_ref[b, :, j * 512:(j + 1) * 512] = t[b].astype(BF16)
        dlb_ref[...] += dlb
        dnw_ref[...] += dnw

    row = lambda blk: pl.BlockSpec((bsz, n, 512), lambda i, blk=blk: (0, nb - 1 - i, blk))
    return _call(
        body, name, (nb,),
        [row(P_HQ // 512), row(P_HF // 512), row(P_HI // 512), row(P_HG // 512),
         pl.BlockSpec((1, 512), lambda i: (0, 0)), pl.BlockSpec((1, LANES), lambda i: (0, 0)), row(0),
         pl.BlockSpec((bsz, None, 4, LANES, LANES), lambda i: (0, nb - 1 - i, 0, 0, 0))],
        [pl.BlockSpec((bsz, n, 2048), lambda i: (0, nb - 1 - i, 0)),
         pl.BlockSpec((1, 512), lambda i: (0, 0)), pl.BlockSpec((1, LANES), lambda i: (0, 0))],
        [_sds((bsz, s, 2048), BF16), _sds((1, 512), F32), _sds((1, LANES), F32)],
        scratch=[pltpu.VMEM((bsz, 4, LANES, LANES), F32)],
    )(p, p, p, p, lb, nw, do, st_all)


def _ssd_chunk(states, xbc, small, z, a_row, dt_row, d_row, nw):
    seqs = range(len(xbc))
    c = xbc[0].shape[0]
    incl = _tril(c)
    spread = (_iota((LANES, SSD_INNER), 0) == SM_DT + _iota((LANES, SSD_INNER), 1) // SSD_HEAD_DIM).astype(BF16)
    dt_all = [_softplus(small[b] + dt_row) for b in seqs]
    both = [_spread_dot(jnp.concatenate([dt_all[b], dt_all[b] * (-jnp.exp(a_row))], axis=0), spread) for b in seqs]
    dt_e, da_e = [t[:c] for t in both], [t[c:] for t in both]
    acs_e = [_mask_dot(incl.astype(BF16), da_e[b]) for b in seqs]
    last_e = [jnp.sum(da_e[b], axis=0, keepdims=True) for b in seqs]
    xs = [xbc[b][:, :SSD_INNER] for b in seqs]
    xdt = [xs[b] * dt_e[b] for b in seqs]
    gw = SSD_GROUPS * SSD_STATE
    lane = _iota((1, LANES), 1)
    items = _seq_items(len(xbc), 4)
    grp = [(b, g) for b in seqs for g in range(SSD_GROUPS)]
    ps = lambda t, j: t[:, j * LANES:(j + 1) * LANES]
    bg = {(b, g): xbc[b][:, SSD_INNER + g * SSD_STATE:SSD_INNER + (g + 1) * SSD_STATE] for b, g in grp}
    cg = {(b, g): xbc[b][:, SSD_INNER + gw + g * SSD_STATE:SSD_INNER + gw + (g + 1) * SSD_STATE] for b, g in grp}
    cb = {bgk: _bdot_nt(cg[bgk], bg[bgk]) for bgk in grp}

    def seg(b, j, sub):
        ac = ps(acs_e[b], j)[:, sub * SSD_HEAD_DIM:sub * SSD_HEAD_DIM + 1]
        return jnp.where(incl, jnp.exp(jnp.where(incl, ac - _col_to_row(ac), 0.0)), 0.0)

    mine = [((lane // SSD_HEAD_DIM) == sub).astype(F32) for sub in range(2)]
    y_in = [[_bdot(cb[b, j // 2] * seg(b, j, sub), ps(xdt[b], j) * mine[sub]) for sub in range(2)] for b, j in items]
    y_st = [_bdot(cg[b, j // 2], states[i]) for i, (b, j) in enumerate(items)]
    grow = [_bdot_tn(bg[b, j // 2], ps(xdt[b], j) * jnp.exp(ps(last_e[b], j) - ps(acs_e[b], j))) for b, j in items]
    new_states = [states[i] * jnp.exp(ps(last_e[b], j)) + grow[i] for i, (b, j) in enumerate(items)]
    ys = [y_in[i][0] + y_in[i][1] + y_st[i] * jnp.exp(ps(acs_e[b], j)) + ps(d_row, j) * ps(xs[b], j)
          for i, (b, j) in enumerate(items)]
    gwid = SSD_INNER // SSD_GROUPS
    outs = []
    for b in seqs:
        yz = jnp.concatenate(ys[4 * b:4 * b + 4], axis=1) * _silu(z[b])
        outs.append(jnp.concatenate(
            [_rms(yz[:, g * gwid:(g + 1) * gwid], nw[:, g * gwid:(g + 1) * gwid]) for g in range(SSD_GROUPS)], axis=1))
    return new_states, outs


def _ssd_fwd(xbc_act, p, a_row, dt_row, d_row, nw, name):
    bsz, s, _ = xbc_act.shape
    c = SSD_CHUNK
    nc = s // c

    items = _seq_items(bsz, 4)

    def body(x_ref, sm_ref, z_ref, a_ref, dt_ref, d_ref, nw_ref, o_ref, st_ref, st_scr):
        @pl.when(pl.program_id(0) == 0)
        def _():
            st_scr[...] = jnp.zeros_like(st_scr)

        st_ref[...] = st_scr[...]
        per_seq = lambda ref: [ref[b] for b in range(bsz)]
        new_states, o = _ssd_chunk([st_scr[b, j] for b, j in items], per_seq(x_ref), per_seq(sm_ref), per_seq(z_ref),
                                   a_ref[...], dt_ref[...], d_ref[...], nw_ref[...])
        for i, (b, j) in enumerate(items):
            st_scr[b, j] = new_states[i]
        for b in range(bsz):
            o_ref[b] = o[b].astype(BF16)

    row = lambda w, blk: pl.BlockSpec((bsz, c, w), lambda n, blk=blk: (0, n, blk))
    prm = pl.BlockSpec((1, LANES), lambda n: (0, 0))
    prm5 = pl.BlockSpec((1, 512), lambda n: (0, 0))
    return _call(
        body, name, (nc,),
        [row(1024, 0), row(LANES, P_SMALL // LANES), row(512, P_SZ // 512), prm, prm, prm5, prm5],
        [row(512, 0), pl.BlockSpec((bsz, None, 4, LANES, LANES), lambda n: (0, n, 0, 0, 0))],
        [_sds((bsz, s, 512), BF16), _sds((bsz, nc, 4, LANES, LANES), F32)],
        scratch=[pltpu.VMEM((bsz, 4, LANES, LANES), F32)],
    )(xbc_act, p, p, a_row, dt_row, d_row, nw)


def _ssd_bwd(do, xbc_act, p, a_row, dt_row, d_row, nw, st_all, name):
    bsz, s, _ = xbc_act.shape
    c = SSD_CHUNK
    nc = s // c

    items = _seq_items(bsz, 4)

    def body(x_ref, sm_ref, z_ref, a_ref, dt_ref, d_ref, nw_ref, do_ref, st_ref,
             dx_ref, dsm_ref, dz_ref, da_ref, ddt_ref, dd_ref, dnw_ref, ds_scr):
        @pl.when(pl.program_id(0) == 0)
        def _():
            ds_scr[...] = jnp.zeros_like(ds_scr)
            da_ref[...] = jnp.zeros_like(da_ref)
            ddt_ref[...] = jnp.zeros_like(ddt_ref)
            dd_ref[...] = jnp.zeros_like(dd_ref)
            dnw_ref[...] = jnp.zeros_like(dnw_ref)

        per_seq = lambda ref: [ref[b] for b in range(bsz)]
        _, vjp = jax.vjp(_ssd_chunk, [st_ref[b, j] for b, j in items], per_seq(x_ref), per_seq(sm_ref), per_seq(z_ref),
                         a_ref[...], dt_ref[...], d_ref[...], nw_ref[...])
        d_states, dx, dsm, dz, da, ddt, dd, dnw = vjp(([ds_scr[b, j] for b, j in items], per_seq(do_ref)))
        for i, (b, j) in enumerate(items):
            ds_scr[b, j] = d_states[i]
        for b in range(bsz):
            dx_ref[b] = dx[b]
            dsm_ref[b] = dsm[b]
            dz_ref[b] = dz[b].astype(BF16)
        da_ref[...] += da
        ddt_ref[...] += ddt
        dd_ref[...] += dd
        dnw_ref[...] += dnw

    row = lambda w, blk: pl.BlockSpec((bsz, c, w), lambda n, blk=blk: (0, nc - 1 - n, blk))
    prm = pl.BlockSpec((1, LANES), lambda n: (0, 0))
    prm5 = pl.BlockSpec((1, 512), lambda n: (0, 0))
    return _call(
        body, name, (nc,),
        [row(1024, 0), row(LANES, P_SMALL // LANES), row(512, P_SZ // 512), prm, prm, prm5, prm5, row(512, 0),
         pl.BlockSpec((bsz, None, 4, LANES, LANES), lambda n: (0, nc - 1 - n, 0, 0, 0))],
        [row(1024, 0), row(LANES, 0), row(512, 0), prm, prm, prm5, prm5],
        [_sds((bsz, s, 1024), F32), _sds((bsz, s, LANES), F32), _sds((bsz, s, 512), BF16),
         _sds((1, LANES), F32), _sds((1, LANES), F32), _sds((1, 512), F32), _sds((1, 512), F32)],
        scratch=[pltpu.VMEM((bsz, 4, LANES, LANES), F32)],
    )(xbc_act, p, p, a_row, dt_row, d_row, nw, do, st_all)


def _peer(k):
    x, y, c = lax.axis_index("x"), lax.axis_index("y"), lax.axis_index("c")
    px = 1 - x if k & 4 else x
    py = 1 - y if k & 2 else y
    pc = 1 - c if k & 1 else c
    return (px, py, pc), 4 * px + 2 * py + pc


def _my_index():
    return 4 * lax.axis_index("x") + 2 * lax.axis_index("y") + lax.axis_index("c")


def _exchange(arrays, name, gather):
    n = len(arrays)

    def body(*refs):
        ins, outs = refs[:n], refs[n:2 * n]
        send_sems, recv_sems, local_sems = refs[2 * n:]
        me = _my_index()

        def copy(i, k):
            peer, slot = _peer(k)
            src = ins[i] if gather else ins[i].at[slot]
            return pltpu.make_async_remote_copy(src_ref=src, dst_ref=outs[i].at[me], send_sem=send_sems.at[k - 1, i],
                                                recv_sem=recv_sems.at[k - 1, i], device_id=peer, device_id_type=MESH_ID)

        def arrival(i, k):
            peer, slot = _peer(k)
            src = ins[i] if gather else ins[i].at[slot]
            return pltpu.make_async_remote_copy(src_ref=src, dst_ref=outs[i].at[slot], send_sem=send_sems.at[k - 1, i],
                                                recv_sem=recv_sems.at[k - 1, i], device_id=peer, device_id_type=MESH_ID)

        mine = [pltpu.make_async_copy(ins[i] if gather else ins[i].at[me], outs[i].at[me], local_sems.at[i])
                for i in range(n)]
        sends = [copy(i, k) for k in range(1, N_DEV) for i in range(n)]
        for cp in mine + sends:
            cp.start()
        for k in range(1, N_DEV):
            for i in range(n):
                arrival(i, k).wait_recv()
        for cp in sends:
            cp.wait_send()
        for cp in mine:
            cp.wait()

    out_shape = [_sds(((N_DEV,) + a.shape) if gather else a.shape, a.dtype) for a in arrays]
    any_spec = pl.BlockSpec(memory_space=pl.ANY)
    return pl.pallas_call(
        body, name=name, out_shape=out_shape, in_specs=[any_spec] * n, out_specs=[any_spec] * n,
        scratch_shapes=[pltpu.SemaphoreType.DMA((N_DEV - 1, n)), pltpu.SemaphoreType.DMA((N_DEV - 1, n)),
                        pltpu.SemaphoreType.DMA((n,))],
    )(*arrays)


def _mesh_place():
    x, y, c = lax.axis_index("x"), lax.axis_index("y"), lax.axis_index("c")
    return (x, y, c), (x, y, 1 - c), [(1 - x, y), (x, 1 - y), (1 - x, 1 - y)]


def _run_exchange(body, name, arrays, out_shape, n_sems):
    n = len(arrays)
    any_spec = pl.BlockSpec(memory_space=pl.ANY)
    return pl.pallas_call(
        body, name=name, out_shape=out_shape, in_specs=[any_spec] * n, out_specs=[any_spec] * n,
        scratch_shapes=[pltpu.SemaphoreType.DMA((n_sems, n)), pltpu.SemaphoreType.DMA((n_sems, n)),
                        pltpu.SemaphoreType.DMA((n,))],
    )(*arrays)


def _gather_two_level(arrays, name):
    n = len(arrays)

    def body(*refs):
        ins, outs = refs[:n], refs[n:2 * n]
        send_sems, recv_sems, local_sems = refs[2 * n:]
        (x, y, c), sibling, chips = _mesh_place()
        slot = lambda px, py, pc: 4 * px + 2 * py + pc

        def copy(i, k, block, to, src=None):
            return pltpu.make_async_remote_copy(
                src_ref=outs[i].at[block] if src is None else src, dst_ref=outs[i].at[block],
                send_sem=send_sems.at[k, i], recv_sem=recv_sems.at[k, i], device_id=to, device_id_type=MESH_ID)

        me = slot(x, y, c)
        mine = [pltpu.make_async_copy(ins[i], outs[i].at[me], local_sems.at[i]) for i in range(n)]
        first = [copy(i, 0, me, sibling, src=ins[i]) for i in range(n)]
        first += [copy(i, 1 + j, me, (*chip, c), src=ins[i]) for j, chip in enumerate(chips) for i in range(n)]
        for cp in mine + first:
            cp.start()
        passed = []
        for j, chip in enumerate(chips):
            for i in range(n):
                copy(i, 1 + j, slot(*chip, c), (x, y, c)).wait_recv()
                cp = copy(i, 4 + j, slot(*chip, c), sibling)
                cp.start()
                passed.append(cp)
        for i in range(n):
            copy(i, 0, slot(x, y, 1 - c), (x, y, c)).wait_recv()
        for j, chip in enumerate(chips):
            for i in range(n):
                copy(i, 4 + j, slot(*chip, 1 - c), (x, y, c)).wait_recv()
        for cp in first + passed:
            cp.wait_send()
        for cp in mine:
            cp.wait()

    out_shape = [_sds((N_DEV,) + a.shape, a.dtype) for a in arrays]
    return _run_exchange(body, name, arrays, out_shape, 7)


def _sibling_swap(arrays, name):
    n = len(arrays)

    def body(*refs):
        ins, outs = refs[:n], refs[n:2 * n]
        send_sems, recv_sems, _ = refs[2 * n:]
        (x, y, c), sibling, _ = _mesh_place()
        copies = [pltpu.make_async_remote_copy(
            src_ref=ins[i].at[1 - c], dst_ref=outs[i], send_sem=send_sems.at[0, i], recv_sem=recv_sems.at[0, i],
            device_id=sibling, device_id_type=MESH_ID) for i in range(n)]
        for cp in copies:
            cp.start()
        for cp in copies:
            cp.wait()

    out_shape = [_sds(a.shape[1:], a.dtype) for a in arrays]
    return _run_exchange(body, name, arrays, out_shape, 1)


def _chip_scatter(arrays, name):
    n = len(arrays)

    def body(*refs):
        ins, outs = refs[:n], refs[n:2 * n]
        send_sems, recv_sems, local_sems = refs[2 * n:]
        (x, y, c), _, chips = _mesh_place()
        me = 2 * x + y
        mine = [pltpu.make_async_copy(ins[i].at[me], outs[i].at[me], local_sems.at[i]) for i in range(n)]
        sends = [pltpu.make_async_remote_copy(
            src_ref=ins[i].at[2 * chip[0] + chip[1]], dst_ref=outs[i].at[me], send_sem=send_sems.at[j, i],
            recv_sem=recv_sems.at[j, i], device_id=(*chip, c), device_id_type=MESH_ID)
            for j, chip in enumerate(chips) for i in range(n)]
        for cp in mine + sends:
            cp.start()
        for j, chip in enumerate(chips):
            for i in range(n):
                pltpu.make_async_remote_copy(
                    src_ref=ins[i].at[me], dst_ref=outs[i].at[2 * chip[0] + chip[1]], send_sem=send_sems.at[j, i],
                    recv_sem=recv_sems.at[j, i], device_id=(*chip, c), device_id_type=MESH_ID).wait_recv()
        for cp in sends:
            cp.wait_send()
        for cp in mine:
            cp.wait()

    out_shape = [_sds(a.shape, a.dtype) for a in arrays]
    return _run_exchange(body, name, arrays, out_shape, 3)


def _pair_sum(a, b, name):
    lead, rows, width = a.shape
    tr = _pick(rows, (256, 128, 64, 32, 16, 8)) if rows % 8 == 0 else rows

    def body(a_ref, b_ref, o_ref):
        o_ref[...] = (a_ref[...].astype(F32) + b_ref[...].astype(F32)).astype(o_ref.dtype)

    blk = pl.BlockSpec((None, tr, width), lambda l, i: (l, i, 0))
    return _call(body, name, (lead, rows // tr), [blk, blk], blk, _sds(a.shape, a.dtype))(a, b)


def _sum_adamw(gs, w, m, v, name):
    lead, rows, width = w.shape
    slots = gs.shape[0]
    tr = _pick(rows, (128, 64, 32, 16, 8)) if rows % 8 == 0 else rows

    def body(g_ref, w_ref, m_ref, v_ref, go_ref, d_ref, mo_ref, vo_ref):
        g = g_ref[0].astype(F32)
        for i in range(1, slots):
            g = g + g_ref[i].astype(F32)
        m2 = ADAM_B1 * m_ref[...] + (1.0 - ADAM_B1) * g
        v2 = ADAM_B2 * v_ref[...] + (1.0 - ADAM_B2) * (g * g)
        m_hat = m2 / (1.0 - ADAM_B1 ** ADAM_STEP)
        v_hat = v2 / (1.0 - ADAM_B2 ** ADAM_STEP)
        go_ref[...] = g
        d_ref[...] = -ADAM_LR * (m_hat / (jnp.sqrt(v_hat) + ADAM_EPS) + ADAM_WD * w_ref[...])
        mo_ref[...] = m2
        vo_ref[...] = v2

    blk = pl.BlockSpec((None, tr, width), lambda l, i: (l, i, 0))
    return _call(body, name, (lead, rows // tr),
                 [pl.BlockSpec((slots, None, tr, width), lambda l, i: (0, l, i, 0)), blk, blk, blk],
                 [blk] * 4, [_sds(w.shape, F32)] * 4)(gs, w, m, v)


MATMUL_WEIGHTS = ("w_in", "w_br_a", "w_br_b", "w_br_c", "w_out", "ffn_w_up", "ffn_w_down")
UNALIGNED = ("w_in", "ffn_w_up")
SPLIT = (
    ("w_in", (DEPTH, D_MODEL, 8720), 2),
    ("gdn_conv_w", (DEPTH, 4, 1536), 2), ("ssd_conv_w", (DEPTH, 4, 1024), 2),
    ("w_br_a", (DEPTH, 512, D_MODEL), 2), ("w_br_b", (DEPTH, 512, D_MODEL), 2), ("w_br_c", (DEPTH, 512, D_MODEL), 2),
    ("w_out", (DEPTH, D_MODEL, D_MODEL), 1), ("ffn_w_up", (DEPTH, D_MODEL, 2 * FFN_HIDDEN), 2),
    ("ffn_conv_w", (DEPTH, 3, 2 * FFN_HIDDEN), 2), ("ffn_w_down", (DEPTH, FFN_HIDDEN, D_MODEL), 1),
)
REPL = (
    ("b_ada", (DEPTH, 6 * D_MODEL)), ("norm1_w", (DEPTH, D_MODEL)), ("gdn_a_log", (DEPTH, 4)),
    ("gdn_dt_bias", (DEPTH, 4)), ("gdn_norm_w", (DEPTH, 128)), ("hgrn_lb_param", (DEPTH, 512)),
    ("hgrn_norm_w", (DEPTH, 128)), ("ssd_conv_b", (DEPTH, 1024)), ("ssd_a_log", (DEPTH, 8)),
    ("ssd_dt_bias", (DEPTH, 8)), ("ssd_d", (DEPTH, 8)), ("ssd_norm_w", (DEPTH, 512)), ("norm2_w", (DEPTH, D_MODEL)),
    ("ffn_conv_b", (DEPTH, 2 * FFN_HIDDEN)), ("final_norm_w", (D_MODEL,)),
)
WEIGHTS = ("w_ada", "b_ada", "norm1_w", "w_in", "gdn_conv_w", "gdn_a_log", "gdn_dt_bias", "gdn_norm_w",
           "hgrn_lb_param", "hgrn_norm_w", "ssd_conv_w", "ssd_conv_b", "ssd_a_log", "ssd_dt_bias", "ssd_d",
           "ssd_norm_w", "w_br_a", "w_br_b", "w_br_c", "w_out", "norm2_w", "ffn_w_up", "ffn_conv_w", "ffn_conv_b",
           "ffn_w_down", "final_norm_w")


def _block_shape(shape, axis):
    return tuple(d // N_DEV if i == axis else d for i, d in enumerate(shape))


def _join_blocks(gathered, shape, axis):
    return jnp.moveaxis(gathered, 0, axis).reshape(shape)


def _split_blocks(full, shape, axis):
    bs = _block_shape(shape, axis)
    t = full.reshape(shape[:axis] + (N_DEV, bs[axis]) + shape[axis + 1:])
    return jnp.moveaxis(t, axis, 0)


def _pack_repl(vals):
    parts = []
    for n, shape in REPL:
        size = math.prod(shape)
        parts.append(jnp.pad(vals[n].reshape(-1), (0, -(-size // PACK_W) * PACK_W - size)))
    cat = jnp.concatenate(parts)
    rows = -(-cat.shape[0] // (8 * PACK_W)) * 8
    return jnp.pad(cat, (0, rows * PACK_W - cat.shape[0])).reshape(rows, PACK_W)


def _unpack_repl(packed):
    flat, out, off = packed.reshape(-1), {}, 0
    for n, shape in REPL:
        size = math.prod(shape)
        out[n] = flat[off:off + size].reshape(shape)
        off += -(-size // PACK_W) * PACK_W
    return out


def _lane_row(vec, lane0):
    return jnp.pad(vec, (lane0, LANES - lane0 - vec.shape[0]))[None]


def _arrange_w_in(w):
    offs = [0]
    for sz in W_IN_SPLITS:
        offs.append(offs[-1] + sz)
    qkv, a, b, gz, hq, hf, hi, hg, sz_, xbc, dt, gate = [w[:, offs[i]:offs[i + 1]] for i in range(12)]
    pad = jnp.zeros((w.shape[0], P_WIDTH - P_SMALL - 16), w.dtype)
    return jnp.concatenate([qkv, gz, xbc, gate, hq, hf, hi, hg, sz_, a, b, dt, pad], axis=1)


def _restore_w_in(wp):
    cut = lambda o, n: wp[:, o:o + n]
    return jnp.concatenate([
        cut(P_QKV, 1536), cut(P_SMALL + SM_A, 4), cut(P_SMALL + SM_B, 4), cut(P_GZ, 512), cut(P_HQ, 512),
        cut(P_HF, 512), cut(P_HI, 512), cut(P_HG, 512), cut(P_SZ, 512), cut(P_XBC, 1024), cut(P_SMALL + SM_DT, 8),
        cut(P_GATE, 3072)], axis=1)


def _join_cols(gathered, arrange, name):
    _, depth, rows, cols = gathered.shape
    tr = _pick(rows, (256, 128, 64, 32, 16, 8))
    width = P_WIDTH if arrange else N_DEV * cols

    def body(g_ref, o_ref):
        row = jnp.concatenate([g_ref[d] for d in range(N_DEV)], axis=1)
        o_ref[...] = _arrange_w_in(row) if arrange else row

    return _call(body, name, (depth, rows // tr),
                 [pl.BlockSpec((N_DEV, None, tr, cols), lambda l, i: (0, l, i, 0))],
                 pl.BlockSpec((None, tr, width), lambda l, i: (l, i, 0)), _sds((depth, rows, width), gathered.dtype),
                 )(gathered)


def _split_cols(per_layer, restore, cols, name):
    depth = len(per_layer)
    rows = per_layer[0].shape[0]
    tr = _pick(rows, (256, 128, 64, 32, 16, 8))
    nt = rows // tr

    def body(*refs):
        o_ref = refs[depth]
        for l in range(depth):
            @pl.when(pl.program_id(0) == l)
            def _(l=l):
                row = _restore_w_in(refs[l][...]) if restore else refs[l][...]
                for d in range(N_DEV):
                    o_ref[d % 2, d // 2] = row[:, d * cols:(d + 1) * cols]

    return _call(body, name, (depth, nt),
                 [pl.BlockSpec((tr, a.shape[1]), lambda l, i: (i, 0)) for a in per_layer],
                 pl.BlockSpec((2, N_DEV // 2, tr, cols), lambda l, i: (0, 0, l * nt + i, 0)),
                 _sds((2, N_DEV // 2, depth * rows, cols), per_layer[0].dtype))(*per_layer)


def _layer_consts(l, wf, wr, lower):
    t = lambda a: a.T
    k = {}
    k["n1w"], k["n2w"] = wr["norm1_w"][l][None], wr["norm2_w"][l][None]
    k["win"], k["win_t"] = wf["w_in"][l], t(wf["w_in"][l])
    for n in ("w_br_a", "w_br_b", "w_br_c", "w_out", "ffn_w_up", "ffn_w_down"):
        k[n], k[n + "_t"] = wf[n][l], t(wf[n][l])
    k["gdn_conv_w"], k["gdn_conv_b"] = wf["gdn_conv_w"][l], jnp.zeros((1, 1536), F32)
    k["ssd_conv_w"], k["ssd_conv_b"] = wf["ssd_conv_w"][l], wr["ssd_conv_b"][l][None]
    k["ffn_conv_w"], k["ffn_conv_b"] = wf["ffn_conv_w"][l], wr["ffn_conv_b"][l][None]
    k["gdn_a"], k["gdn_dt"] = _lane_row(wr["gdn_a_log"][l], SM_A), _lane_row(wr["gdn_dt_bias"][l], SM_A)
    k["gdn_nw"], k["hgrn_nw"] = wr["gdn_norm_w"][l][None], wr["hgrn_norm_w"][l][None]
    k["ssd_a"], k["ssd_dt"] = _lane_row(wr["ssd_a_log"][l], SM_DT), _lane_row(wr["ssd_dt_bias"][l], SM_DT)
    k["ssd_d"] = jnp.repeat(wr["ssd_d"][l], SSD_HEAD_DIM)[None]
    k["ssd_nw"] = wr["ssd_norm_w"][l][None]
    k["lb"] = lower[l:l + 1]
    return k


def _layer_fwd(l, x, mod, k):
    bsz, s, d = x.shape
    t = bsz * s
    sv = {"x": x}
    sv["mod"] = [mod[:, None, i * d:(i + 1) * d] for i in range(6)]
    sh1, sc1, g1, sh2, sc2, g2 = sv["mod"]
    h1 = _norm_mod_fwd(x, k["n1w"], sh1, sc1, f"norm1_fwd{l}")
    p = _mm(h1.reshape(t, d), k["win"], F32, f"mm_in{l}").reshape(bsz, s, P_WIDTH)
    qkv_act = _conv_fwd(p, P_QKV, 1536, k["gdn_conv_w"], k["gdn_conv_b"], True, f"gdn_conv_fwd{l}")
    oa, st_a, ti_a = _gdn_fwd(qkv_act, p, k["gdn_a"], k["gdn_dt"], k["gdn_nw"], f"gdn_fwd{l}")
    ob, st_b = _hgrn_fwd(p, k["lb"], k["hgrn_nw"], f"hgrn_fwd{l}")
    xbc_act = _conv_fwd(p, P_XBC, 1024, k["ssd_conv_w"], k["ssd_conv_b"], True, f"ssd_conv_fwd{l}")
    oc, st_c = _ssd_fwd(xbc_act, p, k["ssd_a"], k["ssd_dt"], k["ssd_d"], k["ssd_nw"], f"ssd_fwd{l}")
    merged = _merge_fwd(p, oa, ob, oc, k["w_br_a"], k["w_br_b"], k["w_br_c"], f"merge_fwd{l}")
    mix = _mm(merged.reshape(t, d), k["w_out"], F32, f"mm_out{l}").reshape(bsz, s, d)
    x1 = _resid_fwd(x, mix, g1, f"resid1_fwd{l}")
    h2 = _norm_mod_fwd(x1, k["n2w"], sh2, sc2, f"norm2_fwd{l}")
    u_pre = _mm(h2.reshape(t, d), k["ffn_w_up"], F32, f"mm_up{l}").reshape(bsz, s, 2 * FFN_HIDDEN)
    a = _conv_glu_fwd(u_pre, k["ffn_conv_w"], k["ffn_conv_b"], f"ffn_conv_glu_fwd{l}")
    ffn = _mm(a.reshape(t, FFN_HIDDEN), k["ffn_w_down"], F32, f"mm_down{l}").reshape(bsz, s, d)
    x2 = _resid_fwd(x1, ffn, g2, f"resid2_fwd{l}")
    sv.update(h1=h1, p=p, qkv_act=qkv_act, oa=oa, st_a=st_a, ti_a=ti_a, ob=ob, st_b=st_b, xbc_act=xbc_act, oc=oc, st_c=st_c,
              merged=merged, mix=mix, x1=x1, h2=h2, u_pre=u_pre, a=a, ffn=ffn)
    return x2, sv


def _layer_bwd(l, dx2, k, sv):
    bsz, s, d = dx2.shape
    t = bsz * s
    f2 = 2 * FFN_HIDDEN
    sh1, sc1, g1, sh2, sc2, g2 = sv["mod"]
    tr = lambda a: a.reshape(t, -1).T
    g = {}
    dffn, dg2 = _gate_bwd(dx2, sv["ffn"], g2, f"gate2_bwd{l}")
    dffn2 = dffn.reshape(t, d)
    da = _mm(dffn2, k["ffn_w_down_t"], F32, f"mm_down_dx{l}").reshape(bsz, s, FFN_HIDDEN)
    g["ffn_w_down"] = _mm(tr(sv["a"]), dffn2, BF16, f"mm_down_dw{l}")
    du_pre, g["ffn_conv_w"], dfcb = _conv_glu_bwd(da, sv["u_pre"], k["ffn_conv_w"], k["ffn_conv_b"], f"ffn_conv_glu_bwd{l}")
    g["ffn_conv_b"] = dfcb[0]
    du2 = du_pre.reshape(t, f2)
    dh2 = _mm(du2, k["ffn_w_up_t"], F32, f"mm_up_dx{l}").reshape(bsz, s, d)
    g["ffn_w_up"] = _mm(tr(sv["h2"]), du2, BF16, f"mm_up_dw{l}")
    dx1, dn2w, dsh2, dsc2 = _norm_mod_bwd(sv["x1"], k["n2w"], sh2, sc2, dh2, dx2, f"norm2_bwd{l}")
    g["norm2_w"] = dn2w[0]
    dmix, dg1 = _gate_bwd(dx1, sv["mix"], g1, f"gate1_bwd{l}")
    dmix2 = dmix.reshape(t, d)
    dmerged = _mm(dmix2, k["w_out_t"], F32, f"mm_out_dx{l}").reshape(bsz, s, d)
    g["w_out"] = _mm(tr(sv["merged"]), dmix2, BF16, f"mm_out_dw{l}")
    p = sv["p"]
    dgate, doa, dob, doc, dya, dyb, dyc = _merge_bwd(
        dmerged, p, sv["oa"], sv["ob"], sv["oc"], k["w_br_a"], k["w_br_b"], k["w_br_c"],
        k["w_br_a_t"], k["w_br_b_t"], k["w_br_c_t"], f"merge_bwd{l}")
    g["w_br_a"] = _mm(tr(sv["oa"]), dya.reshape(t, d), BF16, f"mm_bra_dw{l}")
    g["w_br_b"] = _mm(tr(sv["ob"]), dyb.reshape(t, d), BF16, f"mm_brb_dw{l}")
    g["w_br_c"] = _mm(tr(sv["oc"]), dyc.reshape(t, d), BF16, f"mm_brc_dw{l}")
    dxbc_act, dsm_c, dsz, da_c, ddt_c, dd_c, dnw_c = _ssd_bwd(
        doc, sv["xbc_act"], p, k["ssd_a"], k["ssd_dt"], k["ssd_d"], k["ssd_nw"], sv["st_c"], f"ssd_bwd{l}")
    dxbc_raw, g["ssd_conv_w"], dscb = _conv_bwd(dxbc_act, p, P_XBC, 1024, k["ssd_conv_w"], k["ssd_conv_b"], True, f"ssd_conv_bwd{l}")
    g["ssd_conv_b"] = dscb[0]
    g["ssd_a_log"], g["ssd_dt_bias"] = da_c[0, SM_DT:SM_DT + 8], ddt_c[0, SM_DT:SM_DT + 8]
    g["ssd_d"] = dd_c.reshape(SSD_HEADS, SSD_HEAD_DIM).sum(axis=1)
    g["ssd_norm_w"] = dnw_c[0]
    dhg, dlb, dnw_b = _hgrn_bwd(dob, p, k["lb"], k["hgrn_nw"], sv["st_b"], f"hgrn_bwd{l}")
    g["hgrn_norm_w"] = dnw_b[0]
    dqkv_act, dsm_a, dgz, da_a, ddt_a, dnw_a = _gdn_bwd(
        doa, sv["qkv_act"], p, k["gdn_a"], k["gdn_dt"], k["gdn_nw"], sv["st_a"], sv["ti_a"], f"gdn_bwd{l}")
    dqkv_raw, g["gdn_conv_w"], _ = _conv_bwd(dqkv_act, p, P_QKV, 1536, k["gdn_conv_w"], k["gdn_conv_b"], True, f"gdn_conv_bwd{l}")
    g["gdn_a_log"], g["gdn_dt_bias"], g["gdn_norm_w"] = da_a[0, :4], ddt_a[0, :4], dnw_a[0]
    dsmall = jnp.pad((dsm_a + dsm_c).astype(BF16), ((0, 0), (0, 0), (0, P_WIDTH - P_SMALL - LANES)))
    dp = jnp.concatenate([dqkv_raw, dgz, dxbc_raw, dgate, dhg, dsz, dsmall], axis=-1).reshape(t, P_WIDTH)
    dh1 = _mm(dp, k["win_t"], F32, f"mm_in_dx{l}").reshape(bsz, s, d)
    g["w_in"] = _mm(tr(sv["h1"]), dp, BF16, f"mm_in_dw{l}")
    dx, dn1w, dsh1, dsc1 = _norm_mod_bwd(sv["x"], k["n1w"], sh1, sc1, dh1, dx1, f"norm1_bwd{l}")
    g["norm1_w"] = dn1w[0]
    dmod = jnp.concatenate([dsh1, dsc1, dg1, dsh2, dsc2, dg2], axis=-1)[:, 0]
    return dx, g, dlb, dmod


def _local_step(x, mod, wf, wr, target):
    lower = _lb_fwd(wr["hgrn_lb_param"])
    ks = [_layer_consts(l, wf, wr, lower) for l in range(DEPTH)]
    saved = []
    h = x
    for l in range(DEPTH):
        h, sv = _layer_fwd(l, h, mod[l], ks[l])
        saved.append(sv)
    loss8, dh, dfnw = _final_loss(h, wr["final_norm_w"][None], target)
    per_layer, dlbs, dmods = [None] * DEPTH, [None] * DEPTH, [None] * DEPTH
    for l in reversed(range(DEPTH)):
        dh, per_layer[l], dlbs[l], dmods[l] = _layer_bwd(l, dh, ks[l], saved[l])
    grads = {n: [per_layer[l][n] for l in range(DEPTH)] for n in per_layer[0]}
    grads = {n: g if n in UNALIGNED else jnp.stack(g) for n, g in grads.items()}
    grads["hgrn_lb_param"] = _lb_bwd(wr["hgrn_lb_param"], jnp.concatenate(dlbs, axis=0))
    grads["final_norm_w"] = dfnw[0]
    return loss8[0, 0], dh, grads, jnp.stack(dmods)


def kernel(x, c, w_ada, b_ada, norm1_w, w_in, gdn_conv_w, gdn_a_log, gdn_dt_bias, gdn_norm_w, hgrn_lb_param, hgrn_norm_w, ssd_conv_w, ssd_conv_b, ssd_a_log, ssd_dt_bias, ssd_d, ssd_norm_w, w_br_a, w_br_b, w_br_c, w_out, norm2_w, ffn_w_up, ffn_conv_w, ffn_conv_b, ffn_w_down, final_norm_w, loss_target, m_w_ada, m_b_ada, m_norm1_w, m_w_in, m_gdn_conv_w, m_gdn_a_log, m_gdn_dt_bias, m_gdn_norm_w, m_hgrn_lb_param, m_hgrn_norm_w, m_ssd_conv_w, m_ssd_conv_b, m_ssd_a_log, m_ssd_dt_bias, m_ssd_d, m_ssd_norm_w, m_w_br_a, m_w_br_b, m_w_br_c, m_w_out, m_norm2_w, m_ffn_w_up, m_ffn_conv_w, m_ffn_conv_b, m_ffn_w_down, m_final_norm_w, v_w_ada, v_b_ada, v_norm1_w, v_w_in, v_gdn_conv_w, v_gdn_a_log, v_gdn_dt_bias, v_gdn_norm_w, v_hgrn_lb_param, v_hgrn_norm_w, v_ssd_conv_w, v_ssd_conv_b, v_ssd_a_log, v_ssd_dt_bias, v_ssd_d, v_ssd_norm_w, v_w_br_a, v_w_br_b, v_w_br_c, v_w_out, v_norm2_w, v_ffn_w_up, v_ffn_conv_w, v_ffn_conv_b, v_ffn_w_down, v_final_norm_w):
    given = dict(locals())
    w = {n: given[n] for n in WEIGHTS}
    m = {n: given["m_" + n] for n in WEIGHTS}
    v = {n: given["v_" + n] for n in WEIGHTS}
    me = _my_index()
    bsz = c.shape[0]
    ncol = 6 * D_MODEL // N_DEV

    shards = [w[n].astype(BF16) if n in MATMUL_WEIGHTS else w[n] for n, _, _ in SPLIT] + [c]
    gathered = _gather_two_level(shards, "gather_weights")
    wf = {n: _join_cols(g, n == "w_in", f"join_{n}") if n in UNALIGNED else _join_blocks(g, shape, axis)
          for (n, shape, axis), g in zip(SPLIT, gathered)}
    c_all = gathered[-1].reshape(N_DEV * bsz, D_MODEL)

    b_cols = lax.dynamic_slice_in_dim(b_ada, me * ncol, ncol, axis=1)[:, None]
    mod_cols = _ada_fwd(c_all, w_ada, b_cols)
    send = mod_cols.reshape(DEPTH, N_DEV, bsz, ncol).transpose(1, 0, 2, 3)
    got = _exchange([send], "scatter_mod", False)[0]
    mod = got.transpose(1, 2, 0, 3).reshape(DEPTH, bsz, 6 * D_MODEL)

    loss, dx, grads, dmod = _local_step(x, mod, wf, w, loss_target)

    send = dmod.reshape(DEPTH, bsz, N_DEV, ncol).transpose(2, 0, 1, 3)
    got_dmod = _exchange([send], "scatter_dmod", False)[0]
    dmod_all = got_dmod.transpose(1, 0, 2, 3).reshape(DEPTH, N_DEV * bsz, ncol)
    g_w_ada, g_b_cols = _ada_bwd(c_all.T, dmod_all)

    core = lax.axis_index("c")
    by_core = []
    for n, shape, axis in SPLIT:
        if n in UNALIGNED:
            by_core.append(_split_cols(grads[n], n == "w_in", shape[axis] // N_DEV, f"split_{n}"))
            continue
        parts = _split_blocks(grads[n], shape, axis).astype(BF16)
        parts = parts.reshape((N_DEV // 2, 2, -1, parts.shape[-1]))
        by_core.append(jnp.swapaxes(parts, 0, 1))
    from_sibling = _sibling_swap(by_core, "swap_grads")
    sums = [_pair_sum(lax.dynamic_index_in_dim(mine, core, 0, keepdims=False), theirs, f"pair_sum_{n}")
            for (n, _, _), mine, theirs in zip(SPLIT, by_core, from_sibling)]
    got = _chip_scatter(sums, "scatter_grads")
    grads["b_ada"] = lax.dynamic_update_slice_in_dim(jnp.zeros_like(b_ada), g_b_cols[:, 0], me * ncol, axis=1)

    out = {}
    slots = [(n, g8) for (n, _, _), g8 in zip(SPLIT, got)] + [("w_ada", g_w_ada[None])]
    for n, gs in slots:
        out[n] = _sum_adamw(gs.reshape((gs.shape[0],) + w[n].shape), w[n], m[n], v[n], f"adamw_{n}")
    r8 = _gather_two_level([_pack_repl(grads)], "gather_small_grads")[0]
    res = _sum_adamw(r8[:, None], _pack_repl(w)[None], _pack_repl(m)[None], _pack_repl(v)[None], "adamw_repl")
    repl_out = [_unpack_repl(o[0]) for o in res]
    pick = lambda i, n: out[n][i] if n in out else repl_out[i][n]
    loss = lax.psum(loss, ("x", "y", "c"))
    return (loss, dx, *[pick(i, n) for i in range(4) for n in WEIGHTS])
```

```python
import functools
import math

import jax
import jax.numpy as jnp
from jax import lax
from jax.experimental import pallas as pl
from jax.experimental.pallas import tpu as pltpu

F32, BF16 = jnp.float32, jnp.bfloat16
HI = lax.Precision.HIGHEST
MESH_ID = pl.DeviceIdType.MESH

N_DEV = 8
EPS = 1e-6
D_MODEL = 1024
DEPTH = 2
GDN_HEADS, GDN_DK, GDN_CHUNK = 4, 128, 64
HGRN_HEADS, HGRN_DK, HGRN_CHUNK, HGRN_BLOCK = 4, 128, 16, 128
SSD_HEADS, SSD_HEAD_DIM, SSD_GROUPS, SSD_STATE, SSD_CHUNK = 8, 64, 2, 128, 64
SSD_INNER = SSD_HEADS * SSD_HEAD_DIM
FFN_HIDDEN = 2816
LANES = 128
P_QKV, P_GZ, P_XBC, P_GATE, P_HQ, P_HF, P_HI, P_HG, P_SZ, P_SMALL, P_WIDTH = (
    0, 1536, 2048, 3072, 6144, 6656, 7168, 7680, 8192, 8704, 9216)
SM_A, SM_B, SM_DT = 0, 4, 8
W_IN_SPLITS = (1536, 4, 4, 512, 512, 512, 512, 512, 512, 1024, 8, 3072)

ADAM_LR, ADAM_B1, ADAM_B2, ADAM_EPS, ADAM_WD, ADAM_STEP = 0.001, 0.9, 0.999, 1e-08, 0.01, 10

V7X_VMEM_LIMIT = 56 * 1024 * 1024
PACK_W = 1024


def _call(body, name, grid, in_specs, out_specs, out_shape, scratch=()):
    return pl.pallas_call(
        body, name=name, grid=grid, in_specs=in_specs, out_specs=out_specs, out_shape=out_shape,
        scratch_shapes=list(scratch),
        compiler_params=pltpu.CompilerParams(
            dimension_semantics=("arbitrary",) * len(grid), vmem_limit_bytes=V7X_VMEM_LIMIT),
    )


def _pick(n, cands):
    for c in cands:
        if n % c == 0:
            return c
    raise ValueError(f"no tile for {n} among {cands}")


def _sds(shape, dtype):
    return jax.ShapeDtypeStruct(shape, dtype)


def _dot(a, b):
    return lax.dot_general(a, b, (((1,), (0,)), ((), ())), precision=HI, preferred_element_type=F32)


NN, NT, TN = (((1,), (0,)), ((), ())), (((1,), (1,)), ((), ())), (((0,), (0,)), ((), ()))


def _mxu(a, b, dims):
    return lax.dot_general(a.astype(BF16), b.astype(BF16), dims, preferred_element_type=F32)


@jax.custom_vjp
def _bdot(a, b):
    return _mxu(a, b, NN)


@jax.custom_vjp
def _bdot_nt(a, b):
    return _mxu(a, b, NT)


@jax.custom_vjp
def _bdot_tn(a, b):
    return _mxu(a, b, TN)


_bdot.defvjp(lambda a, b: (_mxu(a, b, NN), (a, b)), lambda r, d: (_mxu(d, r[1], NT), _mxu(r[0], d, TN)))
_bdot_nt.defvjp(lambda a, b: (_mxu(a, b, NT), (a, b)), lambda r, d: (_mxu(d, r[1], NN), _mxu(d, r[0], TN)))
_bdot_tn.defvjp(lambda a, b: (_mxu(a, b, TN), (a, b)), lambda r, d: (_mxu(r[1], d, NT), _mxu(r[0], d, NN)))


def _split(x):
    hi = x.astype(BF16)
    return hi, (x - hi.astype(F32)).astype(BF16)


def _mxu3(a, b, dims):
    ah, al = _split(a)
    bh, bl = _split(b)
    return _mxu(ah, bh, dims) + (_mxu(ah, bl, dims) + _mxu(al, bh, dims))


@jax.custom_vjp
def _dot3(a, b):
    return _mxu3(a, b, NN)


_dot3.defvjp(lambda a, b: (_mxu3(a, b, NN), (a, b)), lambda r, d: (_mxu3(d, r[1], NT), _mxu3(r[0], d, TN)))


def _pieces(x):
    x1 = x.astype(BF16)
    r1 = x - x1.astype(F32)
    x2 = r1.astype(BF16)
    return x1, x2, (r1 - x2.astype(F32)).astype(BF16)


def _mask_mxu(mask, x, dims):
    x1, x2, x3 = _pieces(x)
    return _mxu(mask, x1, dims) + (_mxu(mask, x2, dims) + _mxu(mask, x3, dims))


def _spread_mxu(x, mask, dims):
    x1, x2, x3 = _pieces(x)
    return _mxu(x1, mask, dims) + (_mxu(x2, mask, dims) + _mxu(x3, mask, dims))


@jax.custom_vjp
def _mask_dot(mask, x):
    return _mask_mxu(mask, x, NN)


@jax.custom_vjp
def _spread_dot(x, mask):
    return _spread_mxu(x, mask, NN)


_mask_dot.defvjp(lambda m, x: (_mask_mxu(m, x, NN), m), lambda m, d: (jnp.zeros_like(m), _mask_mxu(m, d, TN)))
_spread_dot.defvjp(lambda x, m: (_spread_mxu(x, m, NN), m), lambda m, d: (_spread_mxu(d, m, NT), jnp.zeros_like(m)))


def _iota(shape, axis):
    return lax.broadcasted_iota(jnp.int32, shape, axis)


def _silu(x):
    return x * jax.nn.sigmoid(x)


def _softplus(x):
    return jnp.maximum(x, 0.0) + jnp.log1p(jnp.exp(-jnp.abs(x)))


def _rms(x, w):
    return x * lax.rsqrt(jnp.mean(x * x, axis=-1, keepdims=True) + EPS) * w


def _lane_col(x, lane):
    m = (_iota(x.shape, 1) == lane).astype(F32)
    return jnp.sum(x * m, axis=1, keepdims=True)


def _col_to_row(c):
    n = c.shape[0]
    eye = (_iota((n, n), 0) == _iota((n, n), 1)).astype(F32)
    return jnp.sum(c * eye, axis=0, keepdims=True)


def _tril(n, strict=False):
    r, c = _iota((n, n), 0), _iota((n, n), 1)
    return (r > c) if strict else (r >= c)


def _mm(a, b, out_dtype, name):
    m, k = a.shape
    n = b.shape[1]
    tm = _pick(m, (1024, 1408, 512, 256, 128, 64, 32, 16, 8))
    tn = _pick(n, (1024, 1408, 768, 512, 384, 256, 128))
    tk = k if k <= 3072 else _pick(k, (1024, 768, 512, 384, 256, 128))
    nk = k // tk

    def body_one(a_ref, b_ref, o_ref):
        o_ref[...] = _bdot(a_ref[...], b_ref[...]).astype(out_dtype)

    def body(a_ref, b_ref, o_ref, acc_ref):
        kk = pl.program_id(2)

        @pl.when(kk == 0)
        def _():
            acc_ref[...] = jnp.zeros_like(acc_ref)

        acc_ref[...] += _bdot(a_ref[...], b_ref[...])

        @pl.when(kk == nk - 1)
        def _():
            o_ref[...] = acc_ref[...].astype(out_dtype)

    return _call(
        body_one if nk == 1 else body, name, (m // tm, n // tn, nk),
        [pl.BlockSpec((tm, tk), lambda i, j, kk: (i, kk)), pl.BlockSpec((tk, tn), lambda i, j, kk: (kk, j))],
        pl.BlockSpec((tm, tn), lambda i, j, kk: (i, j)), _sds((m, n), out_dtype),
        scratch=[] if nk == 1 else [pltpu.VMEM((tm, tn), F32)],
    )(a, b)


def _ada_fwd(c_all, w, b):
    depth, _, n = w.shape
    rows = c_all.shape[0]

    def body(c_ref, w_ref, b_ref, o_ref):
        o_ref[...] = _dot(_silu(c_ref[...]), w_ref[...]) + b_ref[...]

    return _call(
        body, "ada_fwd", (depth,),
        [pl.BlockSpec((rows, D_MODEL), lambda l: (0, 0)), pl.BlockSpec((None, D_MODEL, n), lambda l: (l, 0, 0)),
         pl.BlockSpec((None, 1, n), lambda l: (l, 0, 0))],
        pl.BlockSpec((None, rows, n), lambda l: (l, 0, 0)), _sds((depth, rows, n), F32),
    )(c_all, w, b)


def _ada_bwd(c_all_t, dmod):
    depth, rows, n = dmod.shape

    def body(ct_ref, dm_ref, dw_ref, db_ref):
        dm = dm_ref[...]
        dw_ref[...] = _dot(_silu(ct_ref[...]), dm)
        db_ref[...] = jnp.sum(dm, axis=0, keepdims=True)

    return _call(
        body, "ada_bwd", (depth,),
        [pl.BlockSpec((D_MODEL, rows), lambda l: (0, 0)), pl.BlockSpec((None, rows, n), lambda l: (l, 0, 0))],
        [pl.BlockSpec((None, D_MODEL, n), lambda l: (l, 0, 0)), pl.BlockSpec((None, 1, n), lambda l: (l, 0, 0))],
        [_sds((depth, D_MODEL, n), F32), _sds((depth, 1, n), F32)],
    )(c_all_t, dmod)


def _lb_fn(p):
    rows = [p[l:l + 1] for l in range(DEPTH)]
    mx = functools.reduce(jnp.maximum, rows)
    es = [jnp.exp(r - mx) for r in rows]
    tot = functools.reduce(lambda a, b: a + b, es)
    sm = [e / tot for e in es]
    out, run = [], None
    for l in range(DEPTH):
        run = sm[l] if run is None else run + sm[l]
        out.append(run - sm[0])
    return jnp.concatenate(out, axis=0)


def _lb_fwd(p):
    def body(p_ref, o_ref):
        o_ref[...] = _lb_fn(p_ref[...])

    full = pl.BlockSpec(p.shape, lambda i: (0, 0))
    return _call(body, "lb_fwd", (1,), [full], full, _sds(p.shape, F32))(p)


def _lb_bwd(p, d_lower):
    def body(p_ref, d_ref, o_ref):
        _, vjp = jax.vjp(_lb_fn, p_ref[...])
        o_ref[...] = vjp(d_ref[...])[0]

    full = pl.BlockSpec(p.shape, lambda i: (0, 0))
    return _call(body, "lb_bwd", (1,), [full, full], full, _sds(p.shape, F32))(p, d_lower)


def _norm_mod_fn(x, w, shift, scale):
    return _rms(x, w) * (1.0 + scale) + shift


def _norm_mod_fwd(x, w, shift, scale, name):
    bsz, s, d = x.shape
    ts = _pick(s, (256, 128, 64, 32, 16, 8))

    def body(x_ref, w_ref, sh_ref, sc_ref, o_ref):
        o_ref[...] = _norm_mod_fn(x_ref[...], w_ref[...], sh_ref[...], sc_ref[...]).astype(BF16)

    row = pl.BlockSpec((None, ts, d), lambda b, i: (b, i, 0))
    per_b = pl.BlockSpec((None, 1, d), lambda b, i: (b, 0, 0))
    return _call(body, name, (bsz, s // ts), [row, pl.BlockSpec((1, d), lambda b, i: (0, 0)), per_b, per_b],
                 row, _sds(x.shape, BF16))(x, w, shift, scale)


def _norm_mod_bwd(x, w, shift, scale, dh, carry, name):
    bsz, s, d = x.shape
    ts = _pick(s, (256, 128, 64, 32, 16, 8))

    def body(x_ref, w_ref, sh_ref, sc_ref, dh_ref, c_ref, dx_ref, dw_ref, dsh_ref, dsc_ref):
        b, i = pl.program_id(0), pl.program_id(1)
        _, vjp = jax.vjp(_norm_mod_fn, x_ref[...], w_ref[...], sh_ref[...], sc_ref[...])
        dx, dw, dsh, dsc = vjp(dh_ref[...])
        dx_ref[...] = dx + c_ref[...]

        @pl.when((b == 0) & (i == 0))
        def _():
            dw_ref[...] = jnp.zeros_like(dw_ref)

        @pl.when(i == 0)
        def _():
            dsh_ref[...] = jnp.zeros_like(dsh_ref)
            dsc_ref[...] = jnp.zeros_like(dsc_ref)

        dw_ref[...] += dw
        dsh_ref[...] += dsh
        dsc_ref[...] += dsc

    row = pl.BlockSpec((None, ts, d), lambda b, i: (b, i, 0))
    per_b = pl.BlockSpec((None, 1, d), lambda b, i: (b, 0, 0))
    wspec = pl.BlockSpec((1, d), lambda b, i: (0, 0))
    return _call(body, name, (bsz, s // ts), [row, wspec, per_b, per_b, row, row],
                 [row, wspec, per_b, per_b],
                 [_sds(x.shape, F32), _sds((1, d), F32), _sds((bsz, 1, d), F32), _sds((bsz, 1, d), F32)],
                 )(x, w, shift, scale, dh, carry)


def _resid_fwd(x, y, gate, name):
    bsz, s, d = x.shape
    ts = _pick(s, (512, 256, 128, 64, 32, 16, 8))

    def body(x_ref, y_ref, g_ref, o_ref):
        o_ref[...] = x_ref[...] + g_ref[...] * y_ref[...]

    row = pl.BlockSpec((None, ts, d), lambda b, i: (b, i, 0))
    per_b = pl.BlockSpec((None, 1, d), lambda b, i: (b, 0, 0))
    return _call(body, name, (bsz, s // ts), [row, row, per_b], row, _sds(x.shape, F32))(x, y, gate)


def _gate_bwd(dx, y, gate, name):
    bsz, s, d = dx.shape
    ts = _pick(s, (512, 256, 128, 64, 32, 16, 8))

    def body(dx_ref, y_ref, g_ref, dy_ref, dg_ref):
        dxv = dx_ref[...]
        dy_ref[...] = (dxv * g_ref[...]).astype(BF16)

        @pl.when(pl.program_id(1) == 0)
        def _():
            dg_ref[...] = jnp.zeros_like(dg_ref)

        dg_ref[...] += jnp.sum(dxv * y_ref[...], axis=0, keepdims=True)

    row = pl.BlockSpec((None, ts, d), lambda b, i: (b, i, 0))
    per_b = pl.BlockSpec((None, 1, d), lambda b, i: (b, 0, 0))
    return _call(body, name, (bsz, s // ts), [row, row, per_b], [row, per_b],
                 [_sds(dx.shape, BF16), _sds((bsz, 1, d), F32)])(dx, y, gate)


HALO = 8


def _conv_pre(xx, w_ref, b_ref, kw, rows):
    acc = w_ref[kw - 1:kw, :] * xx[HALO:HALO + rows]
    for k in range(kw - 1):
        acc = acc + w_ref[k:k + 1, :] * pltpu.roll(xx, kw - 1 - k, 0)[HALO:HALO + rows]
    return acc + b_ref[...]


def _conv_fwd(x, col0, width, w, b, name):
    bsz, s, _ = x.shape
    kw = w.shape[0]
    ts = _pick(s, (512, 256, 128, 64, 32, 16, 8))
    tc = _pick(width, (512, 256, 128))
    assert col0 % tc == 0
    c0 = col0 // tc
    hb = ts // HALO

    def body(x_ref, xp_ref, w_ref, b_ref, o_ref):
        i = pl.program_id(1)
        xp = jnp.where(i > 0, xp_ref[...], 0.0)
        xx = jnp.concatenate([xp, x_ref[...]], axis=0)
        pre = _conv_pre(xx, w_ref, b_ref, kw, ts)
        o_ref[...] = _silu(pre)

    return _call(
        body, name, (bsz, s // ts, width // tc),
        [pl.BlockSpec((None, ts, tc), lambda bb, i, j: (bb, i, c0 + j)),
         pl.BlockSpec((None, HALO, tc), lambda bb, i, j: (bb, jnp.maximum(i * hb - 1, 0), c0 + j)),
         pl.BlockSpec((kw, tc), lambda bb, i, j: (0, j)), pl.BlockSpec((1, tc), lambda bb, i, j: (0, j))],
        pl.BlockSpec((None, ts, tc), lambda bb, i, j: (bb, i, j)), _sds((bsz, s, width), F32),
    )(x, x, w, b)


def _conv_bwd(dy, x, col0, width, w, b, name):
    bsz, s, _ = x.shape
    kw = w.shape[0]
    ts = _pick(s, (512, 256, 128, 64, 32, 16, 8))
    tc = _pick(width, (512, 256, 128))
    c0 = col0 // tc
    hb = ts // HALO
    nt = s // ts
    last_h = s // HALO - 1

    def body(x_ref, xp_ref, xn_ref, dy_ref, dyn_ref, w_ref, b_ref, dx_ref, dw_ref, db_ref):
        bb, i = pl.program_id(1), pl.program_id(2)
        xp = jnp.where(i > 0, xp_ref[...], 0.0)
        xx = jnp.concatenate([xp, x_ref[...], xn_ref[...]], axis=0)
        dyy = jnp.concatenate([dy_ref[...], jnp.where(i < nt - 1, dyn_ref[...], 0.0)], axis=0)
        n = ts + HALO
        pre = _conv_pre(xx, w_ref, b_ref, kw, n)
        sg = jax.nn.sigmoid(pre)
        dpre = dyy * (sg * (1.0 + pre * (1.0 - sg)))
        dx = w_ref[kw - 1:kw, :] * dpre[:ts]
        for k in range(kw - 1):
            dx = dx + w_ref[k:k + 1, :] * pltpu.roll(dpre, n - (kw - 1 - k), 0)[:ts]
        dx_ref[...] = dx.astype(BF16)

        @pl.when((bb == 0) & (i == 0))
        def _():
            dw_ref[...] = jnp.zeros_like(dw_ref)
            db_ref[...] = jnp.zeros_like(db_ref)

        dt = dpre[:ts]
        db_ref[...] += jnp.sum(dt, axis=0, keepdims=True)
        dw_ref[kw - 1:kw, :] += jnp.sum(dt * xx[HALO:HALO + ts], axis=0, keepdims=True)
        for k in range(kw - 1):
            xs = pltpu.roll(xx, kw - 1 - k, 0)[HALO:HALO + ts]
            dw_ref[k:k + 1, :] += jnp.sum(dt * xs, axis=0, keepdims=True)

    xspec = lambda f: pl.BlockSpec((None, HALO, tc), f)
    return _call(
        body, name, (width // tc, bsz, nt),
        [pl.BlockSpec((None, ts, tc), lambda j, bb, i: (bb, i, c0 + j)),
         xspec(lambda j, bb, i: (bb, jnp.maximum(i * hb - 1, 0), c0 + j)),
         xspec(lambda j, bb, i: (bb, jnp.minimum((i + 1) * hb, last_h), c0 + j)),
         pl.BlockSpec((None, ts, tc), lambda j, bb, i: (bb, i, j)),
         xspec(lambda j, bb, i: (bb, jnp.minimum((i + 1) * hb, last_h), j)),
         pl.BlockSpec((kw, tc), lambda j, bb, i: (0, j)), pl.BlockSpec((1, tc), lambda j, bb, i: (0, j))],
        [pl.BlockSpec((None, ts, tc), lambda j, bb, i: (bb, i, j)),
         pl.BlockSpec((kw, tc), lambda j, bb, i: (0, j)), pl.BlockSpec((1, tc), lambda j, bb, i: (0, j))],
        [_sds((bsz, s, width), BF16), _sds((kw, width), F32), _sds((1, width), F32)],
    )(x, x, x, dy, dy, w, b)


def _conv_glu_fwd(x, w, b, name):
    bsz, s, f2 = x.shape
    f = f2 // 2
    kw = w.shape[0]
    ts = _pick(s, (1024, 512, 256, 128, 64, 32, 16, 8))
    tc = _pick(f, (256, 128))
    nf = f // tc
    hb = ts // HALO

    def body(xg_ref, xgp_ref, xv_ref, xvp_ref, wg_ref, wv_ref, bg_ref, bv_ref, o_ref):
        i = pl.program_id(1)
        halves = []
        for x_ref, xp_ref, w_ref, b_ref in ((xg_ref, xgp_ref, wg_ref, bg_ref), (xv_ref, xvp_ref, wv_ref, bv_ref)):
            xx = jnp.concatenate([jnp.where(i > 0, xp_ref[...], 0.0), x_ref[...]], axis=0)
            halves.append(_conv_pre(xx, w_ref, b_ref, kw, ts))
        o_ref[...] = (_silu(halves[0]) * halves[1]).astype(BF16)

    tile = lambda off: pl.BlockSpec((None, ts, tc), lambda bb, i, j: (bb, i, off + j))
    prev = lambda off: pl.BlockSpec((None, HALO, tc), lambda bb, i, j: (bb, jnp.maximum(i * hb - 1, 0), off + j))
    wsp = lambda rows, off: pl.BlockSpec((rows, tc), lambda bb, i, j: (0, off + j))
    return _call(
        body, name, (bsz, s // ts, nf),
        [tile(0), prev(0), tile(nf), prev(nf), wsp(kw, 0), wsp(kw, nf), wsp(1, 0), wsp(1, nf)],
        pl.BlockSpec((None, ts, tc), lambda bb, i, j: (bb, i, j)), _sds((bsz, s, f), BF16),
    )(x, x, x, x, w, w, b, b)


def _conv_glu_bwd(da, x, w, b, name):
    bsz, s, f2 = x.shape
    f = f2 // 2
    kw = w.shape[0]
    ts = _pick(s, (1024, 512, 256, 128, 64, 32, 16, 8))
    tc = _pick(f, (256, 128))
    nf = f // tc
    hb = ts // HALO
    nt = s // ts
    last_h = s // HALO - 1
    n = ts + HALO

    def body(xg_ref, xgp_ref, xgn_ref, xv_ref, xvp_ref, xvn_ref, da_ref, dan_ref,
             wg_ref, wv_ref, bg_ref, bv_ref, wx_ref, dx_ref, dw_ref, db_ref):
        j, bb, i = pl.program_id(0), pl.program_id(1), pl.program_id(2)
        day = jnp.concatenate([da_ref[...], jnp.where(i < nt - 1, dan_ref[...], 0.0)], axis=0)
        xg = jnp.concatenate([jnp.where(i > 0, xgp_ref[...], 0.0), xg_ref[...], xgn_ref[...]], axis=0)
        pre_g = _conv_pre(xg, wg_ref, bg_ref, kw, n)
        sg = jax.nn.sigmoid(pre_g)

        @pl.when((bb == 0) & (i == 0))
        def _():
            dw_ref[...] = jnp.zeros_like(dw_ref)
            db_ref[...] = jnp.zeros_like(db_ref)

        def finish(dpre, xx):
            dx = wx_ref[kw - 1:kw, :] * dpre[:ts]
            for k in range(kw - 1):
                dx = dx + wx_ref[k:k + 1, :] * pltpu.roll(dpre, n - (kw - 1 - k), 0)[:ts]
            dx_ref[...] = dx.astype(BF16)
            dt = dpre[:ts]
            db_ref[...] += jnp.sum(dt, axis=0, keepdims=True)
            dw_ref[kw - 1:kw, :] += jnp.sum(dt * xx[HALO:HALO + ts], axis=0, keepdims=True)
            for k in range(kw - 1):
                dw_ref[k:k + 1, :] += jnp.sum(dt * pltpu.roll(xx, kw - 1 - k, 0)[HALO:HALO + ts], axis=0, keepdims=True)

        @pl.when(j < nf)
        def _():
            xv = jnp.concatenate([jnp.where(i > 0, xvp_ref[...], 0.0), xv_ref[...], xvn_ref[...]], axis=0)
            pre_v = _conv_pre(xv, wv_ref, bv_ref, kw, n)
            finish(day * pre_v * (sg * (1.0 + pre_g * (1.0 - sg))), xg)

        @pl.when(j >= nf)
        def _():
            xv = jnp.concatenate([jnp.where(i > 0, xvp_ref[...], 0.0), xv_ref[...], xvn_ref[...]], axis=0)
            finish(day * (pre_g * sg), xv)

    tile = lambda off: pl.BlockSpec((None, ts, tc), lambda j, bb, i: (bb, i, off + j % nf))
    prev = lambda off: pl.BlockSpec((None, HALO, tc), lambda j, bb, i: (bb, jnp.maximum(i * hb - 1, 0), off + j % nf))
    nxt = lambda off: pl.BlockSpec((None, HALO, tc), lambda j, bb, i: (bb, jnp.minimum((i + 1) * hb, last_h), off + j % nf))
    wsp = lambda rows, off: pl.BlockSpec((rows, tc), lambda j, bb, i: (0, off + j % nf))
    own = lambda rows: pl.BlockSpec((rows, tc), lambda j, bb, i: (0, j))
    return _call(
        body, name, (2 * nf, bsz, nt),
        [tile(0), prev(0), nxt(0), tile(nf), prev(nf), nxt(nf), tile(0), nxt(0),
         wsp(kw, 0), wsp(kw, nf), wsp(1, 0), wsp(1, nf), own(kw)],
        [pl.BlockSpec((None, ts, tc), lambda j, bb, i: (bb, i, j)), own(kw), own(1)],
        [_sds((bsz, s, f2), BF16), _sds((kw, f2), F32), _sds((1, f2), F32)],
    )(x, x, x, x, x, x, da, da, w, w, b, b, w)


def _merge_fwd(p, oa, ob, oc, wa, wb, wc, name):
    bsz, s, _ = p.shape
    tm = _pick(s, (256, 128, 64, 32, 16, 8))
    gblk = P_GATE // (3 * D_MODEL)

    def body(g_ref, oa_ref, ob_ref, oc_ref, wa_ref, wb_ref, wc_ref, o_ref):
        acc = None
        for i, (o_r, w_r) in enumerate(((oa_ref, wa_ref), (ob_ref, wb_ref), (oc_ref, wc_ref))):
            y = _bdot(o_r[...], w_r[...])
            t = jax.nn.sigmoid(g_ref[:, i * D_MODEL:(i + 1) * D_MODEL]) * y
            acc = t if acc is None else acc + t
        o_ref[...] = acc.astype(BF16)

    orow = pl.BlockSpec((None, tm, 512), lambda b, i: (b, i, 0))
    wfull = pl.BlockSpec((512, D_MODEL), lambda b, i: (0, 0))
    return _call(
        body, name, (bsz, s // tm),
        [pl.BlockSpec((None, tm, 3 * D_MODEL), lambda b, i: (b, i, gblk)), orow, orow, orow, wfull, wfull, wfull],
        pl.BlockSpec((None, tm, D_MODEL), lambda b, i: (b, i, 0)), _sds((bsz, s, D_MODEL), BF16),
    )(p, oa, ob, oc, wa, wb, wc)


def _merge_bwd(dm, p, oa, ob, oc, wa, wb, wc, wat, wbt, wct, name):
    bsz, s, _ = p.shape
    tm = _pick(s, (256, 128, 64, 32, 16, 8))
    gblk = P_GATE // (3 * D_MODEL)

    def body(dm_ref, g_ref, oa_ref, ob_ref, oc_ref, wa_ref, wb_ref, wc_ref, wat_ref, wbt_ref, wct_ref,
             dg_ref, doa_ref, dob_ref, doc_ref, dya_ref, dyb_ref, dyc_ref):
        dmv = dm_ref[...]
        trip = ((oa_ref, wa_ref, wat_ref, doa_ref, dya_ref), (ob_ref, wb_ref, wbt_ref, dob_ref, dyb_ref),
                (oc_ref, wc_ref, wct_ref, doc_ref, dyc_ref))
        for i, (o_r, w_r, wt_r, do_r, dy_r) in enumerate(trip):
            y = _bdot(o_r[...], w_r[...])
            sg = jax.nn.sigmoid(g_ref[:, i * D_MODEL:(i + 1) * D_MODEL])
            dg_ref[:, i * D_MODEL:(i + 1) * D_MODEL] = (dmv * y * sg * (1.0 - sg)).astype(BF16)
            dy = (dmv * sg).astype(BF16)
            dy_r[...] = dy
            do_r[...] = _bdot(dy, wt_r[...])

    orow = pl.BlockSpec((None, tm, 512), lambda b, i: (b, i, 0))
    drow = pl.BlockSpec((None, tm, D_MODEL), lambda b, i: (b, i, 0))
    grow = pl.BlockSpec((None, tm, 3 * D_MODEL), lambda b, i: (b, i, 0))
    wfull = pl.BlockSpec((512, D_MODEL), lambda b, i: (0, 0))
    wtfull = pl.BlockSpec((D_MODEL, 512), lambda b, i: (0, 0))
    return _call(
        body, name, (bsz, s // tm),
        [drow, pl.BlockSpec((None, tm, 3 * D_MODEL), lambda b, i: (b, i, gblk)), orow, orow, orow,
         wfull, wfull, wfull, wtfull, wtfull, wtfull],
        [grow, orow, orow, orow, drow, drow, drow],
        [_sds((bsz, s, 3 * D_MODEL), BF16)] + [_sds((bsz, s, 512), F32)] * 3 + [_sds((bsz, s, D_MODEL), BF16)] * 3,
    )(dm, p, oa, ob, oc, wa, wb, wc, wat, wbt, wct)


def _final_loss(x, w, target):
    bsz, s, d = x.shape
    ts = _pick(s, (256, 128, 64, 32, 16, 8))

    def body(x_ref, w_ref, t_ref, loss_ref, dx_ref, dw_ref):
        first = (pl.program_id(0) == 0) & (pl.program_id(1) == 0)
        y, vjp = jax.vjp(_rms, x_ref[...], w_ref[...])
        err = y - t_ref[...]
        dx, dw = vjp(err * (1.0 / d))
        dx_ref[...] = dx

        @pl.when(first)
        def _():
            loss_ref[...] = jnp.zeros_like(loss_ref)
            dw_ref[...] = jnp.zeros_like(dw_ref)

        loss_ref[...] += 0.5 * jnp.sum(jnp.sum(err * err, axis=1, keepdims=True), axis=0, keepdims=True) * (1.0 / d)
        dw_ref[...] += dw

    row = pl.BlockSpec((None, ts, d), lambda b, i: (b, i, 0))
    wspec = pl.BlockSpec((1, d), lambda b, i: (0, 0))
    return _call(body, "final_loss", (bsz, s // ts), [row, wspec, row],
                 [pl.BlockSpec((8, LANES), lambda b, i: (0, 0)), row, wspec],
                 [_sds((8, LANES), F32), _sds(x.shape, F32), _sds((1, d), F32)])(x, w, target)


def _unit_lower_inverses(ms):
    n = ms[0].shape[0]
    eye = (_iota((n, n), 0) == _iota((n, n), 1)).astype(F32)
    ps = [-m for m in ms]
    xs = [eye + p for p in ps]
    for _ in range(int(math.log2(n)) - 1):
        ps = [_mxu3(p, p, NN) for p in ps]
        xs = [x + _mxu3(x, p, NN) for x, p in zip(xs, ps)]
    return xs


@jax.custom_vjp
def _known_inverse(m, t):
    return t


_known_inverse.defvjp(lambda m, t: (t, t), lambda t, dt: (-_mxu3(t, _mxu3(dt, t, NT), TN), jnp.zeros_like(t)))


def _gdn_chunk(states, qkv, small, z, a_row, dt_row, nw, tinvs=None):
    nb = len(qkv)
    c = qkv[0].shape[0]
    kw = GDN_HEADS * GDN_DK
    incl, strict = _tril(c), _tril(c, True)
    g_all = [-jnp.exp(a_row) * _softplus(small[b] + dt_row) for b in range(nb)]
    beta_all = [jax.nn.sigmoid(small[b]) for b in range(nb)]
    big_g_all = [_mask_dot(incl.astype(BF16), g_all[b]) for b in range(nb)]
    items = [(b, h) for b in range(nb) for h in range(GDN_HEADS)]
    ids = range(len(items))
    col = lambda b, part, h: qkv[b][:, part * kw + h * GDN_DK:part * kw + (h + 1) * GDN_DK]
    unit = lambda t: t * lax.rsqrt(jnp.sum(t * t, axis=-1, keepdims=True) + EPS)
    q = [unit(col(b, 0, h)) * (GDN_DK ** -0.5) for b, h in items]
    k = [unit(col(b, 1, h)) for b, h in items]
    v = [col(b, 2, h) for b, h in items]
    gc = [_lane_col(big_g_all[b], SM_A + h) for b, h in items]
    bc = [_lane_col(beta_all[b], SM_B + h) for b, h in items]
    g_last = [jnp.sum(_lane_col(g_all[b], SM_A + h), axis=0, keepdims=True) for b, h in items]
    decay = [jnp.where(incl, jnp.exp(jnp.where(incl, gc[i] - _col_to_row(gc[i]), 0.0)), 0.0) for i in ids]
    kb = [k[i] * bc[i] for i in ids]
    m = [jnp.where(strict, _bdot_nt(kb[i], k[i]) * decay[i], 0.0) for i in ids]
    if tinvs is None:
        tinv = _unit_lower_inverses(m)
    else:
        tinv = [_known_inverse(m[i], tinvs[i]) for i in ids]
    eg = [jnp.exp(gc[i]) for i in ids]
    u = [_dot3(tinv[i], v[i] * bc[i]) for i in ids]
    w = [_dot3(tinv[i], kb[i] * eg[i]) for i in ids]
    attn = [_bdot_nt(q[i], k[i]) * decay[i] for i in ids]
    v_new = [u[i] - _bdot(w[i], states[i]) for i in ids]
    o_st = [_bdot(q[i] * eg[i], states[i]) for i in ids]
    o = [o_st[i] + _bdot(attn[i], v_new[i]) for i in ids]
    grow = [_bdot_tn(k[i] * jnp.exp(g_last[i] - gc[i]), v_new[i]) for i in ids]
    new_states = [states[i] * jnp.exp(g_last[i]) + grow[i] for i in ids]
    outs = [_rms(o[i], nw) * _silu(z[b][:, h * GDN_DK:(h + 1) * GDN_DK]) for i, (b, h) in enumerate(items)]
    per_seq = [jnp.concatenate(outs[b * GDN_HEADS:(b + 1) * GDN_HEADS], axis=1) for b in range(nb)]
    return new_states, per_seq, tinv


def _seq_items(bsz, heads):
    return [(b, h) for b in range(bsz) for h in range(heads)]


def _gdn_fwd(qkv_act, p, a_row, dt_row, nw, name):
    bsz, s, _ = qkv_act.shape
    c = GDN_CHUNK
    nc = s // c
    items = _seq_items(bsz, GDN_HEADS)

    def body(qkv_ref, sm_ref, z_ref, a_ref, dt_ref, nw_ref, o_ref, st_ref, ti_ref, st_scr):
        @pl.when(pl.program_id(0) == 0)
        def _():
            st_scr[...] = jnp.zeros_like(st_scr)

        st_ref[...] = st_scr[...]
        seqs = range(bsz)
        new_states, o, tinvs = _gdn_chunk(
            [st_scr[b, h] for b, h in items], [qkv_ref[b] for b in seqs], [sm_ref[b] for b in seqs],
            [z_ref[b] for b in seqs], a_ref[...], dt_ref[...], nw_ref[...])
        for i, (b, h) in enumerate(items):
            st_scr[b, h] = new_states[i]
            ti_ref[b, h] = tinvs[i]
        for b in seqs:
            o_ref[b] = o[b].astype(BF16)

    row = lambda w, blk: pl.BlockSpec((bsz, c, w), lambda n, blk=blk: (0, n, blk))
    prm = pl.BlockSpec((1, LANES), lambda n: (0, 0))
    return _call(
        body, name, (nc,), [row(1536, 0), row(LANES, P_SMALL // LANES), row(512, P_GZ // 512), prm, prm, prm],
        [row(512, 0), pl.BlockSpec((bsz, None, 4, LANES, LANES), lambda n: (0, n, 0, 0, 0)),
         pl.BlockSpec((bsz, None, 4, c, c), lambda n: (0, n, 0, 0, 0))],
        [_sds((bsz, s, 512), BF16), _sds((bsz, nc, 4, LANES, LANES), F32), _sds((bsz, nc, 4, c, c), F32)],
        scratch=[pltpu.VMEM((bsz, 4, LANES, LANES), F32)],
    )(qkv_act, p, p, a_row, dt_row, nw)


def _gdn_bwd(do, qkv_act, p, a_row, dt_row, nw, st_all, ti_all, name):
    bsz, s, _ = qkv_act.shape
    c = GDN_CHUNK
    nc = s // c

    items = _seq_items(bsz, GDN_HEADS)

    def body(qkv_ref, sm_ref, z_ref, a_ref, dt_ref, nw_ref, do_ref, st_ref, ti_ref,
             dqkv_ref, dsm_ref, dz_ref, da_ref, ddt_ref, dnw_ref, ds_scr):
        @pl.when(pl.program_id(0) == 0)
        def _():
            ds_scr[...] = jnp.zeros_like(ds_scr)
            da_ref[...] = jnp.zeros_like(da_ref)
            ddt_ref[...] = jnp.zeros_like(ddt_ref)
            dnw_ref[...] = jnp.zeros_like(dnw_ref)

        seqs = range(bsz)
        tinvs = [ti_ref[b, h] for b, h in items]
        chunk = lambda *a: _gdn_chunk(*a, tinvs=tinvs)[:2]
        _, vjp = jax.vjp(chunk, [st_ref[b, h] for b, h in items], [qkv_ref[b] for b in seqs], [sm_ref[b] for b in seqs],
                         [z_ref[b] for b in seqs], a_ref[...], dt_ref[...], nw_ref[...])
        d_states, dqkv, dsm, dz, da, ddt, dnw = vjp(([ds_scr[b, h] for b, h in items], [do_ref[b] for b in seqs]))
        for i, (b, h) in enumerate(items):
            ds_scr[b, h] = d_states[i]
        for b in seqs:
            dqkv_ref[b] = dqkv[b]
            dsm_ref[b] = dsm[b]
            dz_ref[b] = dz[b].astype(BF16)
        da_ref[...] += da
        ddt_ref[...] += ddt
        dnw_ref[...] += dnw

    rrow = lambda w, blk: pl.BlockSpec((bsz, c, w), lambda n, blk=blk: (0, nc - 1 - n, blk))
    prm = pl.BlockSpec((1, LANES), lambda n: (0, 0))
    return _call(
        body, name, (nc,),
        [rrow(1536, 0), rrow(LANES, P_SMALL // LANES), rrow(512, P_GZ // 512), prm, prm, prm, rrow(512, 0),
         pl.BlockSpec((bsz, None, 4, LANES, LANES), lambda n: (0, nc - 1 - n, 0, 0, 0)),
         pl.BlockSpec((bsz, None, 4, c, c), lambda n: (0, nc - 1 - n, 0, 0, 0))],
        [rrow(1536, 0), rrow(LANES, 0), rrow(512, 0), prm, prm, prm],
        [_sds((bsz, s, 1536), F32), _sds((bsz, s, LANES), F32), _sds((bsz, s, 512), BF16)] + [_sds((1, LANES), F32)] * 3,
        scratch=[pltpu.VMEM((bsz, 4, LANES, LANES), F32)],
    )(qkv_act, p, p, a_row, dt_row, nw, do, st_all, ti_all)


def _hgrn_block(states, q_raw, f_raw, i_raw, g_raw, lb, nw):
    n = q_raw[0].shape[0]
    c = HGRN_CHUNK
    r, cc = _iota((n, n), 0), _iota((n, n), 1)
    same = (r // c) == (cc // c)
    causal = same & (r >= cc)
    ref_row = (r // c) * c + (c // 2 - 1)
    run_sum = causal.astype(F32)
    rel_sum = run_sum - (same & (ref_row >= cc)).astype(F32)
    sums = jnp.concatenate([run_sum, rel_sum, same.astype(F32)], axis=0).astype(BF16)
    seqs, chunks = range(len(q_raw)), range(n // c)
    items = _seq_items(len(q_raw), HGRN_HEADS)
    hs = lambda t, h: t[:, h * HGRN_DK:(h + 1) * HGRN_DK]
    rows = lambda t, j: t[j * c:(j + 1) * c]
    q = [_silu(q_raw[b]) for b in seqs]
    logf = [jnp.log(lb + (1.0 - lb) * jax.nn.sigmoid(f_raw[b])) for b in seqs]
    k = [(1.0 - lb) * jax.nn.sigmoid(-f_raw[b]) for b in seqs]
    all_sums = [_mask_dot(sums, logf[b]) for b in seqs]
    big_g, g_rel, g_tot = ([t[i * n:(i + 1) * n] for t in all_sums] for i in range(3))
    q_rel = [q[b] * jnp.exp(g_rel[b]) for b in seqs]
    k_rel = [k[b] * jnp.exp(-g_rel[b]) for b in seqs]
    qg = [q[b] * jnp.exp(big_g[b]) for b in seqs]
    k_end = [k[b] * jnp.exp(g_tot[b] - big_g[b]) for b in seqs]
    keep = [[jnp.exp(g_tot[b][j * c:j * c + 1]) for j in chunks] for b in seqs]
    scores = [_bdot_nt(hs(q_rel[b], h), hs(k_rel[b], h)) for b, h in items]
    o_intra = [_bdot(jnp.where(causal, scores[i], 0.0), hs(i_raw[b], h)) for i, (b, h) in enumerate(items)]
    grow = [[_bdot_tn(rows(hs(i_raw[b], h), j), rows(hs(k_end[b], h), j)) for j in chunks] for b, h in items]
    entering, new_states = [], []
    for i, (b, h) in enumerate(items):
        st, per_chunk = states[i], []
        for j in chunks:
            per_chunk.append(st)
            st = st * hs(keep[b][j], h) + grow[i][j]
        entering.append(per_chunk)
        new_states.append(st)
    o_inter = [[_bdot_nt(rows(hs(qg[b], h), j), entering[i][j]) for j in chunks] for i, (b, h) in enumerate(items)]
    outs = [_rms(o_intra[i] + jnp.concatenate(o_inter[i], axis=0), nw) * _silu(hs(g_raw[b], h))
            for i, (b, h) in enumerate(items)]
    return new_states, [jnp.concatenate(outs[b * HGRN_HEADS:(b + 1) * HGRN_HEADS], axis=1) for b in seqs]


def _hgrn_fwd(p, lb, nw, name):
    bsz, s, _ = p.shape
    n = HGRN_BLOCK
    nb = s // n

    items = _seq_items(bsz, HGRN_HEADS)

    def body(q_ref, f_ref, i_ref, g_ref, lb_ref, nw_ref, o_ref, st_ref, st_scr):
        @pl.when(pl.program_id(0) == 0)
        def _():
            st_scr[...] = jnp.zeros_like(st_scr)

        st_ref[...] = st_scr[...]
        per_seq = lambda ref: [ref[b] for b in range(bsz)]
        new_states, o = _hgrn_block([st_scr[b, h] for b, h in items], per_seq(q_ref), per_seq(f_ref), per_seq(i_ref),
                                    per_seq(g_ref), lb_ref[...], nw_ref[...])
        for i, (b, h) in enumerate(items):
            st_scr[b, h] = new_states[i]
        for b in range(bsz):
            o_ref[b] = o[b].astype(BF16)

    row = lambda blk: pl.BlockSpec((bsz, n, 512), lambda i, blk=blk: (0, i, blk))
    return _call(
        body, name, (nb,),
        [row(P_HQ // 512), row(P_HF // 512), row(P_HI // 512), row(P_HG // 512),
         pl.BlockSpec((1, 512), lambda i: (0, 0)), pl.BlockSpec((1, LANES), lambda i: (0, 0))],
        [row(0), pl.BlockSpec((bsz, None, 4, LANES, LANES), lambda i: (0, i, 0, 0, 0))],
        [_sds((bsz, s, 512), BF16), _sds((bsz, nb, 4, LANES, LANES), F32)],
        scratch=[pltpu.VMEM((bsz, 4, LANES, LANES), F32)],
    )(p, p, p, p, lb, nw)


def _hgrn_bwd(do, p, lb, nw, st_all, name):
    bsz, s, _ = p.shape
    n = HGRN_BLOCK
    nb = s // n

    items = _seq_items(bsz, HGRN_HEADS)

    def body(q_ref, f_ref, i_ref, g_ref, lb_ref, nw_ref, do_ref, st_ref, dp_ref, dlb_ref, dnw_ref, ds_scr):
        @pl.when(pl.program_id(0) == 0)
        def _():
            ds_scr[...] = jnp.zeros_like(ds_scr)
            dlb_ref[...] = jnp.zeros_like(dlb_ref)
            dnw_ref[...] = jnp.zeros_like(dnw_ref)

        per_seq = lambda ref: [ref[b] for b in range(bsz)]
        _, vjp = jax.vjp(_hgrn_block, [st_ref[b, h] for b, h in items], per_seq(q_ref), per_seq(f_ref), per_seq(i_ref),
                         per_seq(g_ref), lb_ref[...], nw_ref[...])
        d_states, dq, df, di, dg, dlb, dnw = vjp(([ds_scr[b, h] for b, h in items], per_seq(do_ref)))
        for i, (b, h) in enumerate(items):
            ds_scr[b, h] = d_states[i]
        for b in range(bsz):
            for j, t in enumerate((dq, df, di, dg)):
                dp_ref[b, :, j * 512:(j + 1) * 512] = t[b].astype(BF16)
        dlb_ref[...] += dlb
        dnw_ref[...] += dnw

    row = lambda blk: pl.BlockSpec((bsz, n, 512), lambda i, blk=blk: (0, nb - 1 - i, blk))
    return _call(
        body, name, (nb,),
        [row(P_HQ // 512), row(P_HF // 512), row(P_HI // 512), row(P_HG // 512),
         pl.BlockSpec((1, 512), lambda i: (0, 0)), pl.BlockSpec((1, LANES), lambda i: (0, 0)), row(0),
         pl.BlockSpec((bsz, None, 4, LANES, LANES), lambda i: (0, nb - 1 - i, 0, 0, 0))],
        [pl.BlockSpec((bsz, n, 2048), lambda i: (0, nb - 1 - i, 0)),
         pl.BlockSpec((1, 512), lambda i: (0, 0)), pl.BlockSpec((1, LANES), lambda i: (0, 0))],
        [_sds((bsz, s, 2048), BF16), _sds((1, 512), F32), _sds((1, LANES), F32)],
        scratch=[pltpu.VMEM((bsz, 4, LANES, LANES), F32)],
    )(p, p, p, p, lb, nw, do, st_all)


def _ssd_chunk(states, xbc, small, z, a_row, dt_row, d_row, nw):
    seqs = range(len(xbc))
    c = xbc[0].shape[0]
    incl = _tril(c)
    spread = (_iota((LANES, SSD_INNER), 0) == SM_DT + _iota((LANES, SSD_INNER), 1) // SSD_HEAD_DIM).astype(BF16)
    dt_all = [_softplus(small[b] + dt_row) for b in seqs]
    both = [_spread_dot(jnp.concatenate([dt_all[b], dt_all[b] * (-jnp.exp(a_row))], axis=0), spread) for b in seqs]
    dt_e, da_e = [t[:c] for t in both], [t[c:] for t in both]
    acs_e = [_mask_dot(incl.astype(BF16), da_e[b]) for b in seqs]
    last_e = [jnp.sum(da_e[b], axis=0, keepdims=True) for b in seqs]
    xs = [xbc[b][:, :SSD_INNER] for b in seqs]
    xdt = [xs[b] * dt_e[b] for b in seqs]
    gw = SSD_GROUPS * SSD_STATE
    lane = _iota((1, LANES), 1)
    items = _seq_items(len(xbc), 4)
    grp = [(b, g) for b in seqs for g in range(SSD_GROUPS)]
    ps = lambda t, j: t[:, j * LANES:(j + 1) * LANES]
    bg = {(b, g): xbc[b][:, SSD_INNER + g * SSD_STATE:SSD_INNER + (g + 1) * SSD_STATE] for b, g in grp}
    cg = {(b, g): xbc[b][:, SSD_INNER + gw + g * SSD_STATE:SSD_INNER + gw + (g + 1) * SSD_STATE] for b, g in grp}
    cb = {bgk: _bdot_nt(cg[bgk], bg[bgk]) for bgk in grp}

    def seg(b, j, sub):
        ac = ps(acs_e[b], j)[:, sub * SSD_HEAD_DIM:sub * SSD_HEAD_DIM + 1]
        return jnp.where(incl, jnp.exp(jnp.where(incl, ac - _col_to_row(ac), 0.0)), 0.0)

    mine = [((lane // SSD_HEAD_DIM) == sub).astype(F32) for sub in range(2)]
    y_in = [[_bdot(cb[b, j // 2] * seg(b, j, sub), ps(xdt[b], j) * mine[sub]) for sub in range(2)] for b, j in items]
    y_st = [_bdot(cg[b, j // 2], states[i]) for i, (b, j) in enumerate(items)]
    grow = [_bdot_tn(bg[b, j // 2], ps(xdt[b], j) * jnp.exp(ps(last_e[b], j) - ps(acs_e[b], j))) for b, j in items]
    new_states = [states[i] * jnp.exp(ps(last_e[b], j)) + grow[i] for i, (b, j) in enumerate(items)]
    ys = [y_in[i][0] + y_in[i][1] + y_st[i] * jnp.exp(ps(acs_e[b], j)) + ps(d_row, j) * ps(xs[b], j)
          for i, (b, j) in enumerate(items)]
    gwid = SSD_INNER // SSD_GROUPS
    outs = []
    for b in seqs:
        yz = jnp.concatenate(ys[4 * b:4 * b + 4], axis=1) * _silu(z[b])
        outs.append(jnp.concatenate(
            [_rms(yz[:, g * gwid:(g + 1) * gwid], nw[:, g * gwid:(g + 1) * gwid]) for g in range(SSD_GROUPS)], axis=1))
    return new_states, outs


def _ssd_fwd(xbc_act, p, a_row, dt_row, d_row, nw, name):
    bsz, s, _ = xbc_act.shape
    c = SSD_CHUNK
    nc = s // c

    items = _seq_items(bsz, 4)

    def body(x_ref, sm_ref, z_ref, a_ref, dt_ref, d_ref, nw_ref, o_ref, st_ref, st_scr):
        @pl.when(pl.program_id(0) == 0)
        def _():
            st_scr[...] = jnp.zeros_like(st_scr)

        st_ref[...] = st_scr[...]
        per_seq = lambda ref: [ref[b] for b in range(bsz)]
        new_states, o = _ssd_chunk([st_scr[b, j] for b, j in items], per_seq(x_ref), per_seq(sm_ref), per_seq(z_ref),
                                   a_ref[...], dt_ref[...], d_ref[...], nw_ref[...])
        for i, (b, j) in enumerate(items):
            st_scr[b, j] = new_states[i]
        for b in range(bsz):
            o_ref[b] = o[b].astype(BF16)

    row = lambda w, blk: pl.BlockSpec((bsz, c, w), lambda n, blk=blk: (0, n, blk))
    prm = pl.BlockSpec((1, LANES), lambda n: (0, 0))
    prm5 = pl.BlockSpec((1, 512), lambda n: (0, 0))
    return _call(
        body, name, (nc,),
        [row(1024, 0), row(LANES, P_SMALL // LANES), row(512, P_SZ // 512), prm, prm, prm5, prm5],
        [row(512, 0), pl.BlockSpec((bsz, None, 4, LANES, LANES), lambda n: (0, n, 0, 0, 0))],
        [_sds((bsz, s, 512), BF16), _sds((bsz, nc, 4, LANES, LANES), F32)],
        scratch=[pltpu.VMEM((bsz, 4, LANES, LANES), F32)],
    )(xbc_act, p, p, a_row, dt_row, d_row, nw)


def _ssd_bwd(do, xbc_act, p, a_row, dt_row, d_row, nw, st_all, name):
    bsz, s, _ = xbc_act.shape
    c = SSD_CHUNK
    nc = s // c

    items = _seq_items(bsz, 4)

    def body(x_ref, sm_ref, z_ref, a_ref, dt_ref, d_ref, nw_ref, do_ref, st_ref,
             dx_ref, dsm_ref, dz_ref, da_ref, ddt_ref, dd_ref, dnw_ref, ds_scr):
        @pl.when(pl.program_id(0) == 0)
        def _():
            ds_scr[...] = jnp.zeros_like(ds_scr)
            da_ref[...] = jnp.zeros_like(da_ref)
            ddt_ref[...] = jnp.zeros_like(ddt_ref)
            dd_ref[...] = jnp.zeros_like(dd_ref)
            dnw_ref[...] = jnp.zeros_like(dnw_ref)

        per_seq = lambda ref: [ref[b] for b in range(bsz)]
        _, vjp = jax.vjp(_ssd_chunk, [st_ref[b, j] for b, j in items], per_seq(x_ref), per_seq(sm_ref), per_seq(z_ref),
                         a_ref[...], dt_ref[...], d_ref[...], nw_ref[...])
        d_states, dx, dsm, dz, da, ddt, dd, dnw = vjp(([ds_scr[b, j] for b, j in items], per_seq(do_ref)))
        for i, (b, j) in enumerate(items):
            ds_scr[b, j] = d_states[i]
        for b in range(bsz):
            dx_ref[b] = dx[b]
            dsm_ref[b] = dsm[b]
            dz_ref[b] = dz[b].astype(BF16)
        da_ref[...] += da
        ddt_ref[...] += ddt
        dd_ref[...] += dd
        dnw_ref[...] += dnw

    row = lambda w, blk: pl.BlockSpec((bsz, c, w), lambda n, blk=blk: (0, nc - 1 - n, blk))
    prm = pl.BlockSpec((1, LANES), lambda n: (0, 0))
    prm5 = pl.BlockSpec((1, 512), lambda n: (0, 0))
    return _call(
        body, name, (nc,),
        [row(1024, 0), row(LANES, P_SMALL // LANES), row(512, P_SZ // 512), prm, prm, prm5, prm5, row(512, 0),
         pl.BlockSpec((bsz, None, 4, LANES, LANES), lambda n: (0, nc - 1 - n, 0, 0, 0))],
        [row(1024, 0), row(LANES, 0), row(512, 0), prm, prm, prm5, prm5],
        [_sds((bsz, s, 1024), F32), _sds((bsz, s, LANES), F32), _sds((bsz, s, 512), BF16),
         _sds((1, LANES), F32), _sds((1, LANES), F32), _sds((1, 512), F32), _sds((1, 512), F32)],
        scratch=[pltpu.VMEM((bsz, 4, LANES, LANES), F32)],
    )(xbc_act, p, p, a_row, dt_row, d_row, nw, do, st_all)


def _peer(k):
    x, y, c = lax.axis_index("x"), lax.axis_index("y"), lax.axis_index("c")
    px = 1 - x if k & 4 else x
    py = 1 - y if k & 2 else y
    pc = 1 - c if k & 1 else c
    return (px, py, pc), 4 * px + 2 * py + pc


def _my_index():
    return 4 * lax.axis_index("x") + 2 * lax.axis_index("y") + lax.axis_index("c")


def _mesh_place():
    x, y, c = lax.axis_index("x"), lax.axis_index("y"), lax.axis_index("c")
    return (x, y, c), (x, y, 1 - c), [(1 - x, y), (x, 1 - y), (1 - x, 1 - y)]


def _run_exchange(body, name, arrays, out_shape, n_sems):
    n = len(arrays)
    any_spec = pl.BlockSpec(memory_space=pl.ANY)
    return pl.pallas_call(
        body, name=name, out_shape=out_shape, in_specs=[any_spec] * n, out_specs=[any_spec] * n,
        scratch_shapes=[pltpu.SemaphoreType.DMA((n_sems, n)), pltpu.SemaphoreType.DMA((n_sems, n)),
                        pltpu.SemaphoreType.DMA((n,))],
    )(*arrays)


def _all_to_all(arrays, name):
    n = len(arrays)

    def body(*refs):
        ins, outs = refs[:n], refs[n:2 * n]
        send_sems, recv_sems, local_sems = refs[2 * n:]
        me = _my_index()

        def copy(i, k, arriving):
            peer, slot = _peer(k)
            return pltpu.make_async_remote_copy(
                src_ref=ins[i].at[slot], dst_ref=outs[i].at[slot if arriving else me], send_sem=send_sems.at[k - 1, i],
                recv_sem=recv_sems.at[k - 1, i], device_id=peer, device_id_type=MESH_ID)

        mine = [pltpu.make_async_copy(ins[i].at[me], outs[i].at[me], local_sems.at[i]) for i in range(n)]
        sends = [copy(i, k, False) for k in range(1, N_DEV) for i in range(n)]
        for cp in mine + sends:
            cp.start()
        for k in range(1, N_DEV):
            for i in range(n):
                copy(i, k, True).wait_recv()
        for cp in sends:
            cp.wait_send()
        for cp in mine:
            cp.wait()

    return _run_exchange(body, name, arrays, [_sds(a.shape, a.dtype) for a in arrays], N_DEV - 1)


def _gather_two_level(arrays, name):
    n = len(arrays)

    def body(*refs):
        ins, outs = refs[:n], refs[n:2 * n]
        send_sems, recv_sems, local_sems = refs[2 * n:]
        (x, y, c), sibling, chips = _mesh_place()
        slot = lambda px, py, pc: 4 * px + 2 * py + pc

        def copy(i, k, block, to, src=None):
            return pltpu.make_async_remote_copy(
                src_ref=outs[i].at[block] if src is None else src, dst_ref=outs[i].at[block],
                send_sem=send_sems.at[k, i], recv_sem=recv_sems.at[k, i], device_id=to, device_id_type=MESH_ID)

        me = slot(x, y, c)
        mine = [pltpu.make_async_copy(ins[i], outs[i].at[me], local_sems.at[i]) for i in range(n)]
        first = [copy(i, 0, me, sibling, src=ins[i]) for i in range(n)]
        first += [copy(i, 1 + j, me, (*chip, c), src=ins[i]) for j, chip in enumerate(chips) for i in range(n)]
        for cp in mine + first:
            cp.start()
        passed = []
        for j, chip in enumerate(chips):
            for i in range(n):
                copy(i, 1 + j, slot(*chip, c), (x, y, c)).wait_recv()
                cp = copy(i, 4 + j, slot(*chip, c), sibling)
                cp.start()
                passed.append(cp)
        for i in range(n):
            copy(i, 0, slot(x, y, 1 - c), (x, y, c)).wait_recv()
        for j, chip in enumerate(chips):
            for i in range(n):
                copy(i, 4 + j, slot(*chip, 1 - c), (x, y, c)).wait_recv()
        for cp in first + passed:
            cp.wait_send()
        for cp in mine:
            cp.wait()

    out_shape = [_sds((N_DEV,) + a.shape, a.dtype) for a in arrays]
    return _run_exchange(body, name, arrays, out_shape, 7)


def _sibling_swap(arrays, name):
    n = len(arrays)

    def body(*refs):
        ins, outs = refs[:n], refs[n:2 * n]
        send_sems, recv_sems, _ = refs[2 * n:]
        (x, y, c), sibling, _ = _mesh_place()
        copies = [pltpu.make_async_remote_copy(
            src_ref=ins[i].at[1 - c], dst_ref=outs[i], send_sem=send_sems.at[0, i], recv_sem=recv_sems.at[0, i],
            device_id=sibling, device_id_type=MESH_ID) for i in range(n)]
        for cp in copies:
            cp.start()
        for cp in copies:
            cp.wait()

    out_shape = [_sds(a.shape[1:], a.dtype) for a in arrays]
    return _run_exchange(body, name, arrays, out_shape, 1)


def _chip_scatter(arrays, name):
    n = len(arrays)

    def body(*refs):
        ins, outs = refs[:n], refs[n:2 * n]
        send_sems, recv_sems, local_sems = refs[2 * n:]
        (x, y, c), _, chips = _mesh_place()
        me = 2 * x + y
        mine = [pltpu.make_async_copy(ins[i].at[me], outs[i].at[me], local_sems.at[i]) for i in range(n)]
        sends = [pltpu.make_async_remote_copy(
            src_ref=ins[i].at[2 * chip[0] + chip[1]], dst_ref=outs[i].at[me], send_sem=send_sems.at[j, i],
            recv_sem=recv_sems.at[j, i], device_id=(*chip, c), device_id_type=MESH_ID)
            for j, chip in enumerate(chips) for i in range(n)]
        for cp in mine + sends:
            cp.start()
        for j, chip in enumerate(chips):
            for i in range(n):
                pltpu.make_async_remote_copy(
                    src_ref=ins[i].at[me], dst_ref=outs[i].at[2 * chip[0] + chip[1]], send_sem=send_sems.at[j, i],
                    recv_sem=recv_sems.at[j, i], device_id=(*chip, c), device_id_type=MESH_ID).wait_recv()
        for cp in sends:
            cp.wait_send()
        for cp in mine:
            cp.wait()

    out_shape = [_sds(a.shape, a.dtype) for a in arrays]
    return _run_exchange(body, name, arrays, out_shape, 3)


def _pair_sum(a, b, name):
    lead, rows, width = a.shape
    tr = _pick(rows, (256, 128, 64, 32, 16, 8)) if rows % 8 == 0 else rows

    def body(a_ref, b_ref, o_ref):
        o_ref[...] = (a_ref[...].astype(F32) + b_ref[...].astype(F32)).astype(o_ref.dtype)

    blk = pl.BlockSpec((None, tr, width), lambda l, i: (l, i, 0))
    return _call(body, name, (lead, rows // tr), [blk, blk], blk, _sds(a.shape, a.dtype))(a, b)


def _sum_adamw(gs, w, m, v, name):
    lead, rows, width = w.shape
    slots = gs.shape[0]
    tr = _pick(rows, (128, 64, 32, 16, 8)) if rows % 8 == 0 else rows

    def body(g_ref, w_ref, m_ref, v_ref, go_ref, d_ref, mo_ref, vo_ref):
        g = g_ref[0].astype(F32)
        for i in range(1, slots):
            g = g + g_ref[i].astype(F32)
        m2 = ADAM_B1 * m_ref[...] + (1.0 - ADAM_B1) * g
        v2 = ADAM_B2 * v_ref[...] + (1.0 - ADAM_B2) * (g * g)
        m_hat = m2 / (1.0 - ADAM_B1 ** ADAM_STEP)
        v_hat = v2 / (1.0 - ADAM_B2 ** ADAM_STEP)
        go_ref[...] = g
        d_ref[...] = -ADAM_LR * (m_hat / (jnp.sqrt(v_hat) + ADAM_EPS) + ADAM_WD * w_ref[...])
        mo_ref[...] = m2
        vo_ref[...] = v2

    blk = pl.BlockSpec((None, tr, width), lambda l, i: (l, i, 0))
    return _call(body, name, (lead, rows // tr),
                 [pl.BlockSpec((slots, None, tr, width), lambda l, i: (0, l, i, 0)), blk, blk, blk],
                 [blk] * 4, [_sds(w.shape, F32)] * 4)(gs, w, m, v)


MATMUL_WEIGHTS = ("w_in", "w_br_a", "w_br_b", "w_br_c", "w_out", "ffn_w_up", "ffn_w_down")
UNALIGNED = ("w_in", "ffn_w_up")
SPLIT = (
    ("w_in", (DEPTH, D_MODEL, 8720), 2),
    ("gdn_conv_w", (DEPTH, 4, 1536), 2), ("ssd_conv_w", (DEPTH, 4, 1024), 2),
    ("w_br_a", (DEPTH, 512, D_MODEL), 2), ("w_br_b", (DEPTH, 512, D_MODEL), 2), ("w_br_c", (DEPTH, 512, D_MODEL), 2),
    ("w_out", (DEPTH, D_MODEL, D_MODEL), 1), ("ffn_w_up", (DEPTH, D_MODEL, 2 * FFN_HIDDEN), 2),
    ("ffn_conv_w", (DEPTH, 3, 2 * FFN_HIDDEN), 2), ("ffn_w_down", (DEPTH, FFN_HIDDEN, D_MODEL), 1),
)
REPL = (
    ("b_ada", (DEPTH, 6 * D_MODEL)), ("norm1_w", (DEPTH, D_MODEL)), ("gdn_a_log", (DEPTH, 4)),
    ("gdn_dt_bias", (DEPTH, 4)), ("gdn_norm_w", (DEPTH, 128)), ("hgrn_lb_param", (DEPTH, 512)),
    ("hgrn_norm_w", (DEPTH, 128)), ("ssd_conv_b", (DEPTH, 1024)), ("ssd_a_log", (DEPTH, 8)),
    ("ssd_dt_bias", (DEPTH, 8)), ("ssd_d", (DEPTH, 8)), ("ssd_norm_w", (DEPTH, 512)), ("norm2_w", (DEPTH, D_MODEL)),
    ("ffn_conv_b", (DEPTH, 2 * FFN_HIDDEN)), ("final_norm_w", (D_MODEL,)),
)
WEIGHTS = ("w_ada", "b_ada", "norm1_w", "w_in", "gdn_conv_w", "gdn_a_log", "gdn_dt_bias", "gdn_norm_w",
           "hgrn_lb_param", "hgrn_norm_w", "ssd_conv_w", "ssd_conv_b", "ssd_a_log", "ssd_dt_bias", "ssd_d",
           "ssd_norm_w", "w_br_a", "w_br_b", "w_br_c", "w_out", "norm2_w", "ffn_w_up", "ffn_conv_w", "ffn_conv_b",
           "ffn_w_down", "final_norm_w")


def _block_shape(shape, axis):
    return tuple(d // N_DEV if i == axis else d for i, d in enumerate(shape))


def _join_blocks(gathered, shape, axis):
    return jnp.moveaxis(gathered, 0, axis).reshape(shape)


def _split_blocks(full, shape, axis):
    bs = _block_shape(shape, axis)
    t = full.reshape(shape[:axis] + (N_DEV, bs[axis]) + shape[axis + 1:])
    return jnp.moveaxis(t, axis, 0)


def _pack_repl(vals):
    parts = []
    for n, shape in REPL:
        size = math.prod(shape)
        parts.append(jnp.pad(vals[n].reshape(-1), (0, -(-size // PACK_W) * PACK_W - size)))
    cat = jnp.concatenate(parts)
    rows = -(-cat.shape[0] // (8 * PACK_W)) * 8
    return jnp.pad(cat, (0, rows * PACK_W - cat.shape[0])).reshape(rows, PACK_W)


def _unpack_repl(packed):
    flat, out, off = packed.reshape(-1), {}, 0
    for n, shape in REPL:
        size = math.prod(shape)
        out[n] = flat[off:off + size].reshape(shape)
        off += -(-size // PACK_W) * PACK_W
    return out


def _lane_row(vec, lane0):
    return jnp.pad(vec, (lane0, LANES - lane0 - vec.shape[0]))[None]


def _arrange_w_in(w):
    offs = [0]
    for sz in W_IN_SPLITS:
        offs.append(offs[-1] + sz)
    qkv, a, b, gz, hq, hf, hi, hg, sz_, xbc, dt, gate = [w[:, offs[i]:offs[i + 1]] for i in range(12)]
    pad = jnp.zeros((w.shape[0], P_WIDTH - P_SMALL - 16), w.dtype)
    return jnp.concatenate([qkv, gz, xbc, gate, hq, hf, hi, hg, sz_, a, b, dt, pad], axis=1)


def _restore_w_in(wp):
    cut = lambda o, n: wp[:, o:o + n]
    return jnp.concatenate([
        cut(P_QKV, 1536), cut(P_SMALL + SM_A, 4), cut(P_SMALL + SM_B, 4), cut(P_GZ, 512), cut(P_HQ, 512),
        cut(P_HF, 512), cut(P_HI, 512), cut(P_HG, 512), cut(P_SZ, 512), cut(P_XBC, 1024), cut(P_SMALL + SM_DT, 8),
        cut(P_GATE, 3072)], axis=1)


def _join_cols(gathered, arrange, name):
    _, depth, rows, cols = gathered.shape
    tr = _pick(rows, (256, 128, 64, 32, 16, 8))
    width = P_WIDTH if arrange else N_DEV * cols

    def body(g_ref, o_ref):
        row = jnp.concatenate([g_ref[d] for d in range(N_DEV)], axis=1)
        o_ref[...] = _arrange_w_in(row) if arrange else row

    return _call(body, name, (depth, rows // tr),
                 [pl.BlockSpec((N_DEV, None, tr, cols), lambda l, i: (0, l, i, 0))],
                 pl.BlockSpec((None, tr, width), lambda l, i: (l, i, 0)), _sds((depth, rows, width), gathered.dtype),
                 )(gathered)


def _split_cols(per_layer, restore, cols, name):
    depth = len(per_layer)
    rows = per_layer[0].shape[0]
    tr = _pick(rows, (256, 128, 64, 32, 16, 8))
    nt = rows // tr

    def body(*refs):
        o_ref = refs[depth]
        for l in range(depth):
            @pl.when(pl.program_id(0) == l)
            def _(l=l):
                row = _restore_w_in(refs[l][...]) if restore else refs[l][...]
                for d in range(N_DEV):
                    o_ref[d % 2, d // 2] = row[:, d * cols:(d + 1) * cols]

    return _call(body, name, (depth, nt),
                 [pl.BlockSpec((tr, a.shape[1]), lambda l, i: (i, 0)) for a in per_layer],
                 pl.BlockSpec((2, N_DEV // 2, tr, cols), lambda l, i: (0, 0, l * nt + i, 0)),
                 _sds((2, N_DEV // 2, depth * rows, cols), per_layer[0].dtype))(*per_layer)


def _layer_consts(l, wf, wr, lower):
    t = lambda a: a.T
    k = {}
    k["n1w"], k["n2w"] = wr["norm1_w"][l][None], wr["norm2_w"][l][None]
    k["win"], k["win_t"] = wf["w_in"][l], t(wf["w_in"][l])
    for n in ("w_br_a", "w_br_b", "w_br_c", "w_out", "ffn_w_up", "ffn_w_down"):
        k[n], k[n + "_t"] = wf[n][l], t(wf[n][l])
    k["gdn_conv_w"], k["gdn_conv_b"] = wf["gdn_conv_w"][l], jnp.zeros((1, 1536), F32)
    k["ssd_conv_w"], k["ssd_conv_b"] = wf["ssd_conv_w"][l], wr["ssd_conv_b"][l][None]
    k["ffn_conv_w"], k["ffn_conv_b"] = wf["ffn_conv_w"][l], wr["ffn_conv_b"][l][None]
    k["gdn_a"], k["gdn_dt"] = _lane_row(wr["gdn_a_log"][l], SM_A), _lane_row(wr["gdn_dt_bias"][l], SM_A)
    k["gdn_nw"], k["hgrn_nw"] = wr["gdn_norm_w"][l][None], wr["hgrn_norm_w"][l][None]
    k["ssd_a"], k["ssd_dt"] = _lane_row(wr["ssd_a_log"][l], SM_DT), _lane_row(wr["ssd_dt_bias"][l], SM_DT)
    k["ssd_d"] = jnp.repeat(wr["ssd_d"][l], SSD_HEAD_DIM)[None]
    k["ssd_nw"] = wr["ssd_norm_w"][l][None]
    k["lb"] = lower[l:l + 1]
    return k


def _layer_fwd(l, x, mod, k):
    bsz, s, d = x.shape
    t = bsz * s
    sv = {"x": x}
    sv["mod"] = [mod[:, None, i * d:(i + 1) * d] for i in range(6)]
    sh1, sc1, g1, sh2, sc2, g2 = sv["mod"]
    h1 = _norm_mod_fwd(x, k["n1w"], sh1, sc1, f"norm1_fwd{l}")
    p = _mm(h1.reshape(t, d), k["win"], F32, f"mm_in{l}").reshape(bsz, s, P_WIDTH)
    qkv_act = _conv_fwd(p, P_QKV, 1536, k["gdn_conv_w"], k["gdn_conv_b"], f"gdn_conv_fwd{l}")
    oa, st_a, ti_a = _gdn_fwd(qkv_act, p, k["gdn_a"], k["gdn_dt"], k["gdn_nw"], f"gdn_fwd{l}")
    ob, st_b = _hgrn_fwd(p, k["lb"], k["hgrn_nw"], f"hgrn_fwd{l}")
    xbc_act = _conv_fwd(p, P_XBC, 1024, k["ssd_conv_w"], k["ssd_conv_b"], f"ssd_conv_fwd{l}")
    oc, st_c = _ssd_fwd(xbc_act, p, k["ssd_a"], k["ssd_dt"], k["ssd_d"], k["ssd_nw"], f"ssd_fwd{l}")
    merged = _merge_fwd(p, oa, ob, oc, k["w_br_a"], k["w_br_b"], k["w_br_c"], f"merge_fwd{l}")
    mix = _mm(merged.reshape(t, d), k["w_out"], F32, f"mm_out{l}").reshape(bsz, s, d)
    x1 = _resid_fwd(x, mix, g1, f"resid1_fwd{l}")
    h2 = _norm_mod_fwd(x1, k["n2w"], sh2, sc2, f"norm2_fwd{l}")
    u_pre = _mm(h2.reshape(t, d), k["ffn_w_up"], F32, f"mm_up{l}").reshape(bsz, s, 2 * FFN_HIDDEN)
    a = _conv_glu_fwd(u_pre, k["ffn_conv_w"], k["ffn_conv_b"], f"ffn_conv_glu_fwd{l}")
    ffn = _mm(a.reshape(t, FFN_HIDDEN), k["ffn_w_down"], F32, f"mm_down{l}").reshape(bsz, s, d)
    x2 = _resid_fwd(x1, ffn, g2, f"resid2_fwd{l}")
    sv.update(h1=h1, p=p, qkv_act=qkv_act, oa=oa, st_a=st_a, ti_a=ti_a, ob=ob, st_b=st_b, xbc_act=xbc_act, oc=oc, st_c=st_c,
              merged=merged, mix=mix, x1=x1, h2=h2, u_pre=u_pre, a=a, ffn=ffn)
    return x2, sv


def _layer_bwd(l, dx2, k, sv):
    bsz, s, d = dx2.shape
    t = bsz * s
    f2 = 2 * FFN_HIDDEN
    sh1, sc1, g1, sh2, sc2, g2 = sv["mod"]
    tr = lambda a: a.reshape(t, -1).T
    g = {}
    dffn, dg2 = _gate_bwd(dx2, sv["ffn"], g2, f"gate2_bwd{l}")
    dffn2 = dffn.reshape(t, d)
    da = _mm(dffn2, k["ffn_w_down_t"], F32, f"mm_down_dx{l}").reshape(bsz, s, FFN_HIDDEN)
    g["ffn_w_down"] = _mm(tr(sv["a"]), dffn2, BF16, f"mm_down_dw{l}")
    du_pre, g["ffn_conv_w"], dfcb = _conv_glu_bwd(da, sv["u_pre"], k["ffn_conv_w"], k["ffn_conv_b"], f"ffn_conv_glu_bwd{l}")
    g["ffn_conv_b"] = dfcb[0]
    du2 = du_pre.reshape(t, f2)
    dh2 = _mm(du2, k["ffn_w_up_t"], F32, f"mm_up_dx{l}").reshape(bsz, s, d)
    g["ffn_w_up"] = _mm(tr(sv["h2"]), du2, BF16, f"mm_up_dw{l}")
    dx1, dn2w, dsh2, dsc2 = _norm_mod_bwd(sv["x1"], k["n2w"], sh2, sc2, dh2, dx2, f"norm2_bwd{l}")
    g["norm2_w"] = dn2w[0]
    dmix, dg1 = _gate_bwd(dx1, sv["mix"], g1, f"gate1_bwd{l}")
    dmix2 = dmix.reshape(t, d)
    dmerged = _mm(dmix2, k["w_out_t"], F32, f"mm_out_dx{l}").reshape(bsz, s, d)
    g["w_out"] = _mm(tr(sv["merged"]), dmix2, BF16, f"mm_out_dw{l}")
    p = sv["p"]
    dgate, doa, dob, doc, dya, dyb, dyc = _merge_bwd(
        dmerged, p, sv["oa"], sv["ob"], sv["oc"], k["w_br_a"], k["w_br_b"], k["w_br_c"],
        k["w_br_a_t"], k["w_br_b_t"], k["w_br_c_t"], f"merge_bwd{l}")
    g["w_br_a"] = _mm(tr(sv["oa"]), dya.reshape(t, d), BF16, f"mm_bra_dw{l}")
    g["w_br_b"] = _mm(tr(sv["ob"]), dyb.reshape(t, d), BF16, f"mm_brb_dw{l}")
    g["w_br_c"] = _mm(tr(sv["oc"]), dyc.reshape(t, d), BF16, f"mm_brc_dw{l}")
    dxbc_act, dsm_c, dsz, da_c, ddt_c, dd_c, dnw_c = _ssd_bwd(
        doc, sv["xbc_act"], p, k["ssd_a"], k["ssd_dt"], k["ssd_d"], k["ssd_nw"], sv["st_c"], f"ssd_bwd{l}")
    dxbc_raw, g["ssd_conv_w"], dscb = _conv_bwd(dxbc_act, p, P_XBC, 1024, k["ssd_conv_w"], k["ssd_conv_b"], f"ssd_conv_bwd{l}")
    g["ssd_conv_b"] = dscb[0]
    g["ssd_a_log"], g["ssd_dt_bias"] = da_c[0, SM_DT:SM_DT + 8], ddt_c[0, SM_DT:SM_DT + 8]
    g["ssd_d"] = dd_c.reshape(SSD_HEADS, SSD_HEAD_DIM).sum(axis=1)
    g["ssd_norm_w"] = dnw_c[0]
    dhg, dlb, dnw_b = _hgrn_bwd(dob, p, k["lb"], k["hgrn_nw"], sv["st_b"], f"hgrn_bwd{l}")
    g["hgrn_norm_w"] = dnw_b[0]
    dqkv_act, dsm_a, dgz, da_a, ddt_a, dnw_a = _gdn_bwd(
        doa, sv["qkv_act"], p, k["gdn_a"], k["gdn_dt"], k["gdn_nw"], sv["st_a"], sv["ti_a"], f"gdn_bwd{l}")
    dqkv_raw, g["gdn_conv_w"], _ = _conv_bwd(dqkv_act, p, P_QKV, 1536, k["gdn_conv_w"], k["gdn_conv_b"], f"gdn_conv_bwd{l}")
    g["gdn_a_log"], g["gdn_dt_bias"], g["gdn_norm_w"] = da_a[0, :4], ddt_a[0, :4], dnw_a[0]
    dsmall = jnp.pad((dsm_a + dsm_c).astype(BF16), ((0, 0), (0, 0), (0, P_WIDTH - P_SMALL - LANES)))
    dp = jnp.concatenate([dqkv_raw, dgz, dxbc_raw, dgate, dhg, dsz, dsmall], axis=-1).reshape(t, P_WIDTH)
    dh1 = _mm(dp, k["win_t"], F32, f"mm_in_dx{l}").reshape(bsz, s, d)
    g["w_in"] = _mm(tr(sv["h1"]), dp, BF16, f"mm_in_dw{l}")
    dx, dn1w, dsh1, dsc1 = _norm_mod_bwd(sv["x"], k["n1w"], sh1, sc1, dh1, dx1, f"norm1_bwd{l}")
    g["norm1_w"] = dn1w[0]
    dmod = jnp.concatenate([dsh1, dsc1, dg1, dsh2, dsc2, dg2], axis=-1)[:, 0]
    return dx, g, dlb, dmod


def _local_step(x, mod, wf, wr, target):
    lower = _lb_fwd(wr["hgrn_lb_param"])
    ks = [_layer_consts(l, wf, wr, lower) for l in range(DEPTH)]
    saved = []
    h = x
    for l in range(DEPTH):
        h, sv = _layer_fwd(l, h, mod[l], ks[l])
        saved.append(sv)
    loss8, dh, dfnw = _final_loss(h, wr["final_norm_w"][None], target)
    per_layer, dlbs, dmods = [None] * DEPTH, [None] * DEPTH, [None] * DEPTH
    for l in reversed(range(DEPTH)):
        dh, per_layer[l], dlbs[l], dmods[l] = _layer_bwd(l, dh, ks[l], saved[l])
    grads = {n: [per_layer[l][n] for l in range(DEPTH)] for n in per_layer[0]}
    grads = {n: g if n in UNALIGNED else jnp.stack(g) for n, g in grads.items()}
    grads["hgrn_lb_param"] = _lb_bwd(wr["hgrn_lb_param"], jnp.concatenate(dlbs, axis=0))
    grads["final_norm_w"] = dfnw[0]
    return loss8[0, 0], dh, grads, jnp.stack(dmods)


def kernel(x, c, w_ada, b_ada, norm1_w, w_in, gdn_conv_w, gdn_a_log, gdn_dt_bias, gdn_norm_w, hgrn_lb_param, hgrn_norm_w, ssd_conv_w, ssd_conv_b, ssd_a_log, ssd_dt_bias, ssd_d, ssd_norm_w, w_br_a, w_br_b, w_br_c, w_out, norm2_w, ffn_w_up, ffn_conv_w, ffn_conv_b, ffn_w_down, final_norm_w, loss_target, m_w_ada, m_b_ada, m_norm1_w, m_w_in, m_gdn_conv_w, m_gdn_a_log, m_gdn_dt_bias, m_gdn_norm_w, m_hgrn_lb_param, m_hgrn_norm_w, m_ssd_conv_w, m_ssd_conv_b, m_ssd_a_log, m_ssd_dt_bias, m_ssd_d, m_ssd_norm_w, m_w_br_a, m_w_br_b, m_w_br_c, m_w_out, m_norm2_w, m_ffn_w_up, m_ffn_conv_w, m_ffn_conv_b, m_ffn_w_down, m_final_norm_w, v_w_ada, v_b_ada, v_norm1_w, v_w_in, v_gdn_conv_w, v_gdn_a_log, v_gdn_dt_bias, v_gdn_norm_w, v_hgrn_lb_param, v_hgrn_norm_w, v_ssd_conv_w, v_ssd_conv_b, v_ssd_a_log, v_ssd_dt_bias, v_ssd_d, v_ssd_norm_w, v_w_br_a, v_w_br_b, v_w_br_c, v_w_out, v_norm2_w, v_ffn_w_up, v_ffn_conv_w, v_ffn_conv_b, v_ffn_w_down, v_final_norm_w):
    given = dict(locals())
    w = {n: given[n] for n in WEIGHTS}
    m = {n: given["m_" + n] for n in WEIGHTS}
    v = {n: given["v_" + n] for n in WEIGHTS}
    me = _my_index()
    bsz = c.shape[0]
    ncol = 6 * D_MODEL // N_DEV

    shards = [w[n].astype(BF16) if n in MATMUL_WEIGHTS else w[n] for n, _, _ in SPLIT] + [c]
    gathered = _gather_two_level(shards, "gather_weights")
    wf = {n: _join_cols(g, n == "w_in", f"join_{n}") if n in UNALIGNED else _join_blocks(g, shape, axis)
          for (n, shape, axis), g in zip(SPLIT, gathered)}
    c_all = gathered[-1].reshape(N_DEV * bsz, D_MODEL)

    b_cols = lax.dynamic_slice_in_dim(b_ada, me * ncol, ncol, axis=1)[:, None]
    mod_cols = _ada_fwd(c_all, w_ada, b_cols)
    send = mod_cols.reshape(DEPTH, N_DEV, bsz, ncol).transpose(1, 0, 2, 3)
    got = _all_to_all([send], "scatter_mod")[0]
    mod = got.transpose(1, 2, 0, 3).reshape(DEPTH, bsz, 6 * D_MODEL)

    loss, dx, grads, dmod = _local_step(x, mod, wf, w, loss_target)

    send = dmod.reshape(DEPTH, bsz, N_DEV, ncol).transpose(2, 0, 1, 3)
    got_dmod = _all_to_all([send], "scatter_dmod")[0]
    dmod_all = got_dmod.transpose(1, 0, 2, 3).reshape(DEPTH, N_DEV * bsz, ncol)
    g_w_ada, g_b_cols = _ada_bwd(c_all.T, dmod_all)

    core = lax.axis_index("c")
    by_core = []
    for n, shape, axis in SPLIT:
        if n in UNALIGNED:
            by_core.append(_split_cols(grads[n], n == "w_in", shape[axis] // N_DEV, f"split_{n}"))
            continue
        parts = _split_blocks(grads[n], shape, axis).astype(BF16)
        parts = parts.reshape((N_DEV // 2, 2, -1, parts.shape[-1]))
        by_core.append(jnp.swapaxes(parts, 0, 1))
    from_sibling = _sibling_swap(by_core, "swap_grads")
    sums = [_pair_sum(lax.dynamic_index_in_dim(mine, core, 0, keepdims=False), theirs, f"pair_sum_{n}")
            for (n, _, _), mine, theirs in zip(SPLIT, by_core, from_sibling)]
    got = _chip_scatter(sums, "scatter_grads")
    grads["b_ada"] = lax.dynamic_update_slice_in_dim(jnp.zeros_like(b_ada), g_b_cols[:, 0], me * ncol, axis=1)

    out = {}
    slots = [(n, g8) for (n, _, _), g8 in zip(SPLIT, got)] + [("w_ada", g_w_ada[None])]
    for n, gs in slots:
        out[n] = _sum_adamw(gs.reshape((gs.shape[0],) + w[n].shape), w[n], m[n], v[n], f"adamw_{n}")
    r8 = _gather_two_level([_pack_repl(grads)], "gather_small_grads")[0]
    res = _sum_adamw(r8[:, None], _pack_repl(w)[None], _pack_repl(m)[None], _pack_repl(v)[None], "adamw_repl")
    repl_out = [_unpack_repl(o[0]) for o in res]
    pick = lambda i, n: out[n][i] if n in out else repl_out[i][n]
    loss = lax.psum(loss, ("x", "y", "c"))
    return (loss, dx, *[pick(i, n) for i in range(4) for n in WEIGHTS])
```

```python
import functools
import math

import jax
import jax.numpy as jnp
from jax import lax
from jax.experimental import pallas as pl
from jax.experimental.pallas import tpu as pltpu

F32, BF16 = jnp.float32, jnp.bfloat16
HI = lax.Precision.HIGHEST
MESH_ID = pl.DeviceIdType.MESH

N_DEV = 8
EPS = 1e-6
D_MODEL = 1024
DEPTH = 2
GDN_HEADS, GDN_DK, GDN_CHUNK = 4, 128, 64
HGRN_HEADS, HGRN_DK, HGRN_CHUNK, HGRN_BLOCK = 4, 128, 16, 128
SSD_HEADS, SSD_HEAD_DIM, SSD_GROUPS, SSD_STATE, SSD_CHUNK = 8, 64, 2, 128, 64
SSD_INNER = SSD_HEADS * SSD_HEAD_DIM
FFN_HIDDEN = 2816
LANES = 128
P_QKV, P_GZ, P_XBC, P_GATE, P_HQ, P_HF, P_HI, P_HG, P_SZ, P_SMALL, P_WIDTH = (
    0, 1536, 2048, 3072, 6144, 6656, 7168, 7680, 8192, 8704, 9216)
SM_A, SM_B, SM_DT = 0, 4, 8
W_IN_SPLITS = (1536, 4, 4, 512, 512, 512, 512, 512, 512, 1024, 8, 3072)

ADAM_LR, ADAM_B1, ADAM_B2, ADAM_EPS, ADAM_WD, ADAM_STEP = 0.001, 0.9, 0.999, 1e-08, 0.01, 10

V7X_VMEM_LIMIT = 56 * 1024 * 1024
PACK_W = 1024


def _call(body, name, grid, in_specs, out_specs, out_shape, scratch=()):
    return pl.pallas_call(
        body, name=name, grid=grid, in_specs=in_specs, out_specs=out_specs, out_shape=out_shape,
        scratch_shapes=list(scratch),
        compiler_params=pltpu.CompilerParams(
            dimension_semantics=("arbitrary",) * len(grid), vmem_limit_bytes=V7X_VMEM_LIMIT),
    )


def _pick(n, cands):
    for c in cands:
        if n % c == 0:
            return c
    raise ValueError(f"no tile for {n} among {cands}")


def _row_tile(s, cap):
    t = cap
    while s % t:
        t //= 2
    return t


def _sds(shape, dtype):
    return jax.ShapeDtypeStruct(shape, dtype)


def _dot(a, b):
    return lax.dot_general(a, b, (((1,), (0,)), ((), ())), precision=HI, preferred_element_type=F32)


NN, NT, TN = (((1,), (0,)), ((), ())), (((1,), (1,)), ((), ())), (((0,), (0,)), ((), ()))


def _mxu(a, b, dims):
    return lax.dot_general(a.astype(BF16), b.astype(BF16), dims, preferred_element_type=F32)


@jax.custom_vjp
def _bdot(a, b):
    return _mxu(a, b, NN)


@jax.custom_vjp
def _bdot_nt(a, b):
    return _mxu(a, b, NT)


@jax.custom_vjp
def _bdot_tn(a, b):
    return _mxu(a, b, TN)


_bdot.defvjp(lambda a, b: (_mxu(a, b, NN), (a, b)), lambda r, d: (_mxu(d, r[1], NT), _mxu(r[0], d, TN)))
_bdot_nt.defvjp(lambda a, b: (_mxu(a, b, NT), (a, b)), lambda r, d: (_mxu(d, r[1], NN), _mxu(d, r[0], TN)))
_bdot_tn.defvjp(lambda a, b: (_mxu(a, b, TN), (a, b)), lambda r, d: (_mxu(r[1], d, NT), _mxu(r[0], d, NN)))


def _split(x):
    hi = x.astype(BF16)
    return hi, (x - hi.astype(F32)).astype(BF16)


def _mxu3(a, b, dims):
    ah, al = _split(a)
    bh, bl = _split(b)
    return _mxu(ah, bh, dims) + (_mxu(ah, bl, dims) + _mxu(al, bh, dims))


@jax.custom_vjp
def _dot3(a, b):
    return _mxu3(a, b, NN)


_dot3.defvjp(lambda a, b: (_mxu3(a, b, NN), (a, b)), lambda r, d: (_mxu3(d, r[1], NT), _mxu3(r[0], d, TN)))


def _pieces(x):
    x1 = x.astype(BF16)
    r1 = x - x1.astype(F32)
    x2 = r1.astype(BF16)
    return x1, x2, (r1 - x2.astype(F32)).astype(BF16)


def _mask_mxu(mask, x, dims):
    x1, x2, x3 = _pieces(x)
    return _mxu(mask, x1, dims) + (_mxu(mask, x2, dims) + _mxu(mask, x3, dims))


def _spread_mxu(x, mask, dims):
    x1, x2, x3 = _pieces(x)
    return _mxu(x1, mask, dims) + (_mxu(x2, mask, dims) + _mxu(x3, mask, dims))


@jax.custom_vjp
def _mask_dot(mask, x):
    return _mask_mxu(mask, x, NN)


@jax.custom_vjp
def _spread_dot(x, mask):
    return _spread_mxu(x, mask, NN)


_mask_dot.defvjp(lambda m, x: (_mask_mxu(m, x, NN), m), lambda m, d: (jnp.zeros_like(m), _mask_mxu(m, d, TN)))
_spread_dot.defvjp(lambda x, m: (_spread_mxu(x, m, NN), m), lambda m, d: (_spread_mxu(d, m, NT), jnp.zeros_like(m)))


def _iota(shape, axis):
    return lax.broadcasted_iota(jnp.int32, shape, axis)


def _silu(x):
    return x * jax.nn.sigmoid(x)


def _softplus(x):
    return jnp.maximum(x, 0.0) + jnp.log1p(jnp.exp(-jnp.abs(x)))


def _rms(x, w):
    return x * lax.rsqrt(jnp.mean(x * x, axis=-1, keepdims=True) + EPS) * w


def _lane_col(x, lane):
    m = (_iota(x.shape, 1) == lane).astype(F32)
    return jnp.sum(x * m, axis=1, keepdims=True)


def _col_to_row(c):
    n = c.shape[0]
    eye = (_iota((n, n), 0) == _iota((n, n), 1)).astype(F32)
    return jnp.sum(c * eye, axis=0, keepdims=True)


def _tril(n, strict=False):
    r, c = _iota((n, n), 0), _iota((n, n), 1)
    return (r > c) if strict else (r >= c)


def _mm(a, b, out_dtype, name):
    m, k = a.shape
    n = b.shape[1]
    tm = _pick(m, (1024, 1408, 512, 256, 128, 64, 32, 16, 8))
    tn = _pick(n, (1024, 1408, 768, 512, 384, 256, 128))
    tk = k if k <= 3072 else _pick(k, (1024, 768, 512, 384, 256, 128))
    nk = k // tk

    def body_one(a_ref, b_ref, o_ref):
        o_ref[...] = _bdot(a_ref[...], b_ref[...]).astype(out_dtype)

    def body(a_ref, b_ref, o_ref, acc_ref):
        kk = pl.program_id(2)

        @pl.when(kk == 0)
        def _():
            acc_ref[...] = jnp.zeros_like(acc_ref)

        acc_ref[...] += _bdot(a_ref[...], b_ref[...])

        @pl.when(kk == nk - 1)
        def _():
            o_ref[...] = acc_ref[...].astype(out_dtype)

    return _call(
        body_one if nk == 1 else body, name, (m // tm, n // tn, nk),
        [pl.BlockSpec((tm, tk), lambda i, j, kk: (i, kk)), pl.BlockSpec((tk, tn), lambda i, j, kk: (kk, j))],
        pl.BlockSpec((tm, tn), lambda i, j, kk: (i, j)), _sds((m, n), out_dtype),
        scratch=[] if nk == 1 else [pltpu.VMEM((tm, tn), F32)],
    )(a, b)


def _ada_fwd(c_all, w, b):
    depth, _, n = w.shape
    rows = c_all.shape[0]

    def body(c_ref, w_ref, b_ref, o_ref):
        o_ref[...] = _dot(_silu(c_ref[...]), w_ref[...]) + b_ref[...]

    return _call(
        body, "ada_fwd", (depth,),
        [pl.BlockSpec((rows, D_MODEL), lambda l: (0, 0)), pl.BlockSpec((None, D_MODEL, n), lambda l: (l, 0, 0)),
         pl.BlockSpec((None, 1, n), lambda l: (l, 0, 0))],
        pl.BlockSpec((None, rows, n), lambda l: (l, 0, 0)), _sds((depth, rows, n), F32),
    )(c_all, w, b)


def _ada_bwd(c_all_t, dmod):
    depth, rows, n = dmod.shape

    def body(ct_ref, dm_ref, dw_ref, db_ref):
        dm = dm_ref[...]
        dw_ref[...] = _dot(_silu(ct_ref[...]), dm)
        db_ref[...] = jnp.sum(dm, axis=0, keepdims=True)

    return _call(
        body, "ada_bwd", (depth,),
        [pl.BlockSpec((D_MODEL, rows), lambda l: (0, 0)), pl.BlockSpec((None, rows, n), lambda l: (l, 0, 0))],
        [pl.BlockSpec((None, D_MODEL, n), lambda l: (l, 0, 0)), pl.BlockSpec((None, 1, n), lambda l: (l, 0, 0))],
        [_sds((depth, D_MODEL, n), F32), _sds((depth, 1, n), F32)],
    )(c_all_t, dmod)


def _lb_fn(p):
    rows = [p[l:l + 1] for l in range(DEPTH)]
    mx = functools.reduce(jnp.maximum, rows)
    es = [jnp.exp(r - mx) for r in rows]
    tot = functools.reduce(lambda a, b: a + b, es)
    sm = [e / tot for e in es]
    out, run = [], None
    for l in range(DEPTH):
        run = sm[l] if run is None else run + sm[l]
        out.append(run - sm[0])
    return jnp.concatenate(out, axis=0)


def _lb_fwd(p):
    def body(p_ref, o_ref):
        o_ref[...] = _lb_fn(p_ref[...])

    full = pl.BlockSpec(p.shape, lambda i: (0, 0))
    return _call(body, "lb_fwd", (1,), [full], full, _sds(p.shape, F32))(p)


def _lb_bwd(p, d_lower):
    def body(p_ref, d_ref, o_ref):
        _, vjp = jax.vjp(_lb_fn, p_ref[...])
        o_ref[...] = vjp(d_ref[...])[0]

    full = pl.BlockSpec(p.shape, lambda i: (0, 0))
    return _call(body, "lb_bwd", (1,), [full, full], full, _sds(p.shape, F32))(p, d_lower)


def _norm_mod_fn(x, w, shift, scale):
    return _rms(x, w) * (1.0 + scale) + shift


def _norm_mod_fwd(x, w, shift, scale, name):
    bsz, s, d = x.shape
    ts = _row_tile(s, 512)

    def body(x_ref, w_ref, sh_ref, sc_ref, o_ref):
        o_ref[...] = _norm_mod_fn(x_ref[...], w_ref[...], sh_ref[...], sc_ref[...]).astype(BF16)

    row = pl.BlockSpec((None, ts, d), lambda b, i: (b, i, 0))
    per_b = pl.BlockSpec((None, 1, d), lambda b, i: (b, 0, 0))
    return _call(body, name, (bsz, s // ts), [row, pl.BlockSpec((1, d), lambda b, i: (0, 0)), per_b, per_b],
                 row, _sds(x.shape, BF16))(x, w, shift, scale)


def _norm_mod_bwd(x, w, shift, scale, dh, carry, name):
    bsz, s, d = x.shape
    ts = _row_tile(s, 512)

    def body(x_ref, w_ref, sh_ref, sc_ref, dh_ref, c_ref, dx_ref, dw_ref, dsh_ref, dsc_ref):
        b, i = pl.program_id(0), pl.program_id(1)
        _, vjp = jax.vjp(_norm_mod_fn, x_ref[...], w_ref[...], sh_ref[...], sc_ref[...])
        dx, dw, dsh, dsc = vjp(dh_ref[...])
        dx_ref[...] = dx + c_ref[...]

        @pl.when((b == 0) & (i == 0))
        def _():
            dw_ref[...] = jnp.zeros_like(dw_ref)

        @pl.when(i == 0)
        def _():
            dsh_ref[...] = jnp.zeros_like(dsh_ref)
            dsc_ref[...] = jnp.zeros_like(dsc_ref)

        dw_ref[...] += dw
        dsh_ref[...] += dsh
        dsc_ref[...] += dsc

    row = pl.BlockSpec((None, ts, d), lambda b, i: (b, i, 0))
    per_b = pl.BlockSpec((None, 1, d), lambda b, i: (b, 0, 0))
    wspec = pl.BlockSpec((1, d), lambda b, i: (0, 0))
    return _call(body, name, (bsz, s // ts), [row, wspec, per_b, per_b, row, row],
                 [row, wspec, per_b, per_b],
                 [_sds(x.shape, F32), _sds((1, d), F32), _sds((bsz, 1, d), F32), _sds((bsz, 1, d), F32)],
                 )(x, w, shift, scale, dh, carry)


def _resid_fwd(x, y, gate, name):
    bsz, s, d = x.shape
    ts = _row_tile(s, 1024)

    def body(x_ref, y_ref, g_ref, o_ref):
        o_ref[...] = x_ref[...] + g_ref[...] * y_ref[...]

    row = pl.BlockSpec((None, ts, d), lambda b, i: (b, i, 0))
    per_b = pl.BlockSpec((None, 1, d), lambda b, i: (b, 0, 0))
    return _call(body, name, (bsz, s // ts), [row, row, per_b], row, _sds(x.shape, F32))(x, y, gate)


def _gate_bwd(dx, y, gate, name):
    bsz, s, d = dx.shape
    ts = _row_tile(s, 1024)

    def body(dx_ref, y_ref, g_ref, dy_ref, dg_ref):
        dxv = dx_ref[...]
        dy_ref[...] = (dxv * g_ref[...]).astype(BF16)

        @pl.when(pl.program_id(1) == 0)
        def _():
            dg_ref[...] = jnp.zeros_like(dg_ref)

        dg_ref[...] += jnp.sum(dxv * y_ref[...], axis=0, keepdims=True)

    row = pl.BlockSpec((None, ts, d), lambda b, i: (b, i, 0))
    per_b = pl.BlockSpec((None, 1, d), lambda b, i: (b, 0, 0))
    return _call(body, name, (bsz, s // ts), [row, row, per_b], [row, per_b],
                 [_sds(dx.shape, BF16), _sds((bsz, 1, d), F32)])(dx, y, gate)


HALO = 8


def _conv_pre(xx, w_ref, b_ref, kw, rows):
    acc = w_ref[kw - 1:kw, :] * xx[HALO:HALO + rows]
    for k in range(kw - 1):
        acc = acc + w_ref[k:k + 1, :] * pltpu.roll(xx, kw - 1 - k, 0)[HALO:HALO + rows]
    return acc + b_ref[...]


def _conv_fwd(x, col0, width, w, b, name):
    bsz, s, _ = x.shape
    kw = w.shape[0]
    ts = _row_tile(s, 1024)
    tc = _pick(width, (512, 256, 128))
    assert col0 % tc == 0
    c0 = col0 // tc
    hb = ts // HALO

    def body(x_ref, xp_ref, w_ref, b_ref, o_ref):
        i = pl.program_id(1)
        xp = jnp.where(i > 0, xp_ref[...], 0.0)
        xx = jnp.concatenate([xp, x_ref[...]], axis=0)
        pre = _conv_pre(xx, w_ref, b_ref, kw, ts)
        o_ref[...] = _silu(pre)

    return _call(
        body, name, (bsz, s // ts, width // tc),
        [pl.BlockSpec((None, ts, tc), lambda bb, i, j: (bb, i, c0 + j)),
         pl.BlockSpec((None, HALO, tc), lambda bb, i, j: (bb, jnp.maximum(i * hb - 1, 0), c0 + j)),
         pl.BlockSpec((kw, tc), lambda bb, i, j: (0, j)), pl.BlockSpec((1, tc), lambda bb, i, j: (0, j))],
        pl.BlockSpec((None, ts, tc), lambda bb, i, j: (bb, i, j)), _sds((bsz, s, width), F32),
    )(x, x, w, b)


def _conv_bwd(dy, x, col0, width, w, b, name):
    bsz, s, _ = x.shape
    kw = w.shape[0]
    ts = _row_tile(s, 1024)
    tc = _pick(width, (512, 256, 128))
    c0 = col0 // tc
    hb = ts // HALO
    nt = s // ts
    last_h = s // HALO - 1

    def body(x_ref, xp_ref, xn_ref, dy_ref, dyn_ref, w_ref, b_ref, dx_ref, dw_ref, db_ref):
        bb, i = pl.program_id(1), pl.program_id(2)
        xp = jnp.where(i > 0, xp_ref[...], 0.0)
        xx = jnp.concatenate([xp, x_ref[...], xn_ref[...]], axis=0)
        dyy = jnp.concatenate([dy_ref[...], jnp.where(i < nt - 1, dyn_ref[...], 0.0)], axis=0)
        n = ts + HALO
        pre = _conv_pre(xx, w_ref, b_ref, kw, n)
        sg = jax.nn.sigmoid(pre)
        dpre = dyy * (sg * (1.0 + pre * (1.0 - sg)))
        dx = w_ref[kw - 1:kw, :] * dpre[:ts]
        for k in range(kw - 1):
            dx = dx + w_ref[k:k + 1, :] * pltpu.roll(dpre, n - (kw - 1 - k), 0)[:ts]
        dx_ref[...] = dx.astype(BF16)

        @pl.when((bb == 0) & (i == 0))
        def _():
            dw_ref[...] = jnp.zeros_like(dw_ref)
            db_ref[...] = jnp.zeros_like(db_ref)

        dt = dpre[:ts]
        db_ref[...] += jnp.sum(dt, axis=0, keepdims=True)
        dw_ref[kw - 1:kw, :] += jnp.sum(dt * xx[HALO:HALO + ts], axis=0, keepdims=True)
        for k in range(kw - 1):
            xs = pltpu.roll(xx, kw - 1 - k, 0)[HALO:HALO + ts]
            dw_ref[k:k + 1, :] += jnp.sum(dt * xs, axis=0, keepdims=True)

    xspec = lambda f: pl.BlockSpec((None, HALO, tc), f)
    return _call(
        body, name, (width // tc, bsz, nt),
        [pl.BlockSpec((None, ts, tc), lambda j, bb, i: (bb, i, c0 + j)),
         xspec(lambda j, bb, i: (bb, jnp.maximum(i * hb - 1, 0), c0 + j)),
         xspec(lambda j, bb, i: (bb, jnp.minimum((i + 1) * hb, last_h), c0 + j)),
         pl.BlockSpec((None, ts, tc), lambda j, bb, i: (bb, i, j)),
         xspec(lambda j, bb, i: (bb, jnp.minimum((i + 1) * hb, last_h), j)),
         pl.BlockSpec((kw, tc), lambda j, bb, i: (0, j)), pl.BlockSpec((1, tc), lambda j, bb, i: (0, j))],
        [pl.BlockSpec((None, ts, tc), lambda j, bb, i: (bb, i, j)),
         pl.BlockSpec((kw, tc), lambda j, bb, i: (0, j)), pl.BlockSpec((1, tc), lambda j, bb, i: (0, j))],
        [_sds((bsz, s, width), BF16), _sds((kw, width), F32), _sds((1, width), F32)],
    )(x, x, x, dy, dy, w, b)


def _conv_glu_fwd(x, w, b, name):
    bsz, s, f2 = x.shape
    f = f2 // 2
    kw = w.shape[0]
    ts = _row_tile(s, 2048)
    tc = _pick(f, (256, 128))
    nf = f // tc
    hb = ts // HALO

    def body(xg_ref, xgp_ref, xv_ref, xvp_ref, wg_ref, wv_ref, bg_ref, bv_ref, o_ref):
        i = pl.program_id(1)
        halves = []
        for x_ref, xp_ref, w_ref, b_ref in ((xg_ref, xgp_ref, wg_ref, bg_ref), (xv_ref, xvp_ref, wv_ref, bv_ref)):
            xx = jnp.concatenate([jnp.where(i > 0, xp_ref[...], 0.0), x_ref[...]], axis=0)
            halves.append(_conv_pre(xx, w_ref, b_ref, kw, ts))
        o_ref[...] = (_silu(halves[0]) * halves[1]).astype(BF16)

    tile = lambda off: pl.BlockSpec((None, ts, tc), lambda bb, i, j: (bb, i, off + j))
    prev = lambda off: pl.BlockSpec((None, HALO, tc), lambda bb, i, j: (bb, jnp.maximum(i * hb - 1, 0), off + j))
    wsp = lambda rows, off: pl.BlockSpec((rows, tc), lambda bb, i, j: (0, off + j))
    return _call(
        body, name, (bsz, s // ts, nf),
        [tile(0), prev(0), tile(nf), prev(nf), wsp(kw, 0), wsp(kw, nf), wsp(1, 0), wsp(1, nf)],
        pl.BlockSpec((None, ts, tc), lambda bb, i, j: (bb, i, j)), _sds((bsz, s, f), BF16),
    )(x, x, x, x, w, w, b, b)


def _conv_glu_bwd(da, x, w, b, name):
    bsz, s, f2 = x.shape
    f = f2 // 2
    kw = w.shape[0]
    ts = _row_tile(s, 2048)
    tc = _pick(f, (256, 128))
    nf = f // tc
    hb = ts // HALO
    nt = s // ts
    last_h = s // HALO - 1
    n = ts + HALO

    def body(xg_ref, xgp_ref, xgn_ref, xv_ref, xvp_ref, xvn_ref, da_ref, dan_ref,
             wg_ref, wv_ref, bg_ref, bv_ref, wx_ref, dx_ref, dw_ref, db_ref):
        j, bb, i = pl.program_id(0), pl.program_id(1), pl.program_id(2)
        day = jnp.concatenate([da_ref[...], jnp.where(i < nt - 1, dan_ref[...], 0.0)], axis=0)
        xg = jnp.concatenate([jnp.where(i > 0, xgp_ref[...], 0.0), xg_ref[...], xgn_ref[...]], axis=0)
        pre_g = _conv_pre(xg, wg_ref, bg_ref, kw, n)
        sg = jax.nn.sigmoid(pre_g)

        @pl.when((bb == 0) & (i == 0))
        def _():
            dw_ref[...] = jnp.zeros_like(dw_ref)
            db_ref[...] = jnp.zeros_like(db_ref)

        def finish(dpre, xx):
            dx = wx_ref[kw - 1:kw, :] * dpre[:ts]
            for k in range(kw - 1):
                dx = dx + wx_ref[k:k + 1, :] * pltpu.roll(dpre, n - (kw - 1 - k), 0)[:ts]
            dx_ref[...] = dx.astype(BF16)
            dt = dpre[:ts]
            db_ref[...] += jnp.sum(dt, axis=0, keepdims=True)
            dw_ref[kw - 1:kw, :] += jnp.sum(dt * xx[HALO:HALO + ts], axis=0, keepdims=True)
            for k in range(kw - 1):
                dw_ref[k:k + 1, :] += jnp.sum(dt * pltpu.roll(xx, kw - 1 - k, 0)[HALO:HALO + ts], axis=0, keepdims=True)

        @pl.when(j < nf)
        def _():
            xv = jnp.concatenate([jnp.where(i > 0, xvp_ref[...], 0.0), xv_ref[...], xvn_ref[...]], axis=0)
            pre_v = _conv_pre(xv, wv_ref, bv_ref, kw, n)
            finish(day * pre_v * (sg * (1.0 + pre_g * (1.0 - sg))), xg)

        @pl.when(j >= nf)
        def _():
            xv = jnp.concatenate([jnp.where(i > 0, xvp_ref[...], 0.0), xv_ref[...], xvn_ref[...]], axis=0)
            finish(day * (pre_g * sg), xv)

    tile = lambda off: pl.BlockSpec((None, ts, tc), lambda j, bb, i: (bb, i, off + j % nf))
    prev = lambda off: pl.BlockSpec((None, HALO, tc), lambda j, bb, i: (bb, jnp.maximum(i * hb - 1, 0), off + j % nf))
    nxt = lambda off: pl.BlockSpec((None, HALO, tc), lambda j, bb, i: (bb, jnp.minimum((i + 1) * hb, last_h), off + j % nf))
    wsp = lambda rows, off: pl.BlockSpec((rows, tc), lambda j, bb, i: (0, off + j % nf))
    own = lambda rows: pl.BlockSpec((rows, tc), lambda j, bb, i: (0, j))
    return _call(
        body, name, (2 * nf, bsz, nt),
        [tile(0), prev(0), nxt(0), tile(nf), prev(nf), nxt(nf), tile(0), nxt(0),
         wsp(kw, 0), wsp(kw, nf), wsp(1, 0), wsp(1, nf), own(kw)],
        [pl.BlockSpec((None, ts, tc), lambda j, bb, i: (bb, i, j)), own(kw), own(1)],
        [_sds((bsz, s, f2), BF16), _sds((kw, f2), F32), _sds((1, f2), F32)],
    )(x, x, x, x, x, x, da, da, w, w, b, b, w)


def _merge_fwd(p, oa, ob, oc, wa, wb, wc, name):
    bsz, s, _ = p.shape
    tm = _row_tile(s, 512)
    gblk = P_GATE // (3 * D_MODEL)

    def body(g_ref, oa_ref, ob_ref, oc_ref, wa_ref, wb_ref, wc_ref, o_ref):
        acc = None
        for i, (o_r, w_r) in enumerate(((oa_ref, wa_ref), (ob_ref, wb_ref), (oc_ref, wc_ref))):
            y = _bdot(o_r[...], w_r[...])
            t = jax.nn.sigmoid(g_ref[:, i * D_MODEL:(i + 1) * D_MODEL]) * y
            acc = t if acc is None else acc + t
        o_ref[...] = acc.astype(BF16)

    orow = pl.BlockSpec((None, tm, 512), lambda b, i: (b, i, 0))
    wfull = pl.BlockSpec((512, D_MODEL), lambda b, i: (0, 0))
    return _call(
        body, name, (bsz, s // tm),
        [pl.BlockSpec((None, tm, 3 * D_MODEL), lambda b, i: (b, i, gblk)), orow, orow, orow, wfull, wfull, wfull],
        pl.BlockSpec((None, tm, D_MODEL), lambda b, i: (b, i, 0)), _sds((bsz, s, D_MODEL), BF16),
    )(p, oa, ob, oc, wa, wb, wc)


def _merge_bwd(dm, p, oa, ob, oc, wa, wb, wc, wat, wbt, wct, name):
    bsz, s, _ = p.shape
    tm = _row_tile(s, 512)
    gblk = P_GATE // (3 * D_MODEL)

    def body(dm_ref, g_ref, oa_ref, ob_ref, oc_ref, wa_ref, wb_ref, wc_ref, wat_ref, wbt_ref, wct_ref,
             dg_ref, doa_ref, dob_ref, doc_ref, dya_ref, dyb_ref, dyc_ref):
        dmv = dm_ref[...]
        trip = ((oa_ref, wa_ref, wat_ref, doa_ref, dya_ref), (ob_ref, wb_ref, wbt_ref, dob_ref, dyb_ref),
                (oc_ref, wc_ref, wct_ref, doc_ref, dyc_ref))
        for i, (o_r, w_r, wt_r, do_r, dy_r) in enumerate(trip):
            y = _bdot(o_r[...], w_r[...])
            sg = jax.nn.sigmoid(g_ref[:, i * D_MODEL:(i + 1) * D_MODEL])
            dg_ref[:, i * D_MODEL:(i + 1) * D_MODEL] = (dmv * y * sg * (1.0 - sg)).astype(BF16)
            dy = (dmv * sg).astype(BF16)
            dy_r[...] = dy
            do_r[...] = _bdot(dy, wt_r[...])

    orow = pl.BlockSpec((None, tm, 512), lambda b, i: (b, i, 0))
    drow = pl.BlockSpec((None, tm, D_MODEL), lambda b, i: (b, i, 0))
    grow = pl.BlockSpec((None, tm, 3 * D_MODEL), lambda b, i: (b, i, 0))
    wfull = pl.BlockSpec((512, D_MODEL), lambda b, i: (0, 0))
    wtfull = pl.BlockSpec((D_MODEL, 512), lambda b, i: (0, 0))
    return _call(
        body, name, (bsz, s // tm),
        [drow, pl.BlockSpec((None, tm, 3 * D_MODEL), lambda b, i: (b, i, gblk)), orow, orow, orow,
         wfull, wfull, wfull, wtfull, wtfull, wtfull],
        [grow, orow, orow, orow, drow, drow, drow],
        [_sds((bsz, s, 3 * D_MODEL), BF16)] + [_sds((bsz, s, 512), F32)] * 3 + [_sds((bsz, s, D_MODEL), BF16)] * 3,
    )(dm, p, oa, ob, oc, wa, wb, wc, wat, wbt, wct)


def _final_loss(x, w, target):
    bsz, s, d = x.shape
    ts = _row_tile(s, 512)

    def body(x_ref, w_ref, t_ref, loss_ref, dx_ref, dw_ref):
        first = (pl.program_id(0) == 0) & (pl.program_id(1) == 0)
        y, vjp = jax.vjp(_rms, x_ref[...], w_ref[...])
        err = y - t_ref[...]
        dx, dw = vjp(err * (1.0 / d))
        dx_ref[...] = dx

        @pl.when(first)
        def _():
            loss_ref[...] = jnp.zeros_like(loss_ref)
            dw_ref[...] = jnp.zeros_like(dw_ref)

        loss_ref[...] += 0.5 * jnp.sum(jnp.sum(err * err, axis=1, keepdims=True), axis=0, keepdims=True) * (1.0 / d)
        dw_ref[...] += dw

    row = pl.BlockSpec((None, ts, d), lambda b, i: (b, i, 0))
    wspec = pl.BlockSpec((1, d), lambda b, i: (0, 0))
    return _call(body, "final_loss", (bsz, s // ts), [row, wspec, row],
                 [pl.BlockSpec((8, LANES), lambda b, i: (0, 0)), row, wspec],
                 [_sds((8, LANES), F32), _sds(x.shape, F32), _sds((1, d), F32)])(x, w, target)


def _unit_lower_inverses(ms):
    n = ms[0].shape[0]
    eye = (_iota((n, n), 0) == _iota((n, n), 1)).astype(F32)
    ps = [-m for m in ms]
    xs = [eye + p for p in ps]
    for _ in range(int(math.log2(n)) - 1):
        ps = [_mxu3(p, p, NN) for p in ps]
        xs = [x + _mxu3(x, p, NN) for x, p in zip(xs, ps)]
    return xs


@jax.custom_vjp
def _known_inverse(m, t):
    return t


_known_inverse.defvjp(lambda m, t: (t, t), lambda t, dt: (-_mxu3(t, _mxu3(dt, t, NT), TN), jnp.zeros_like(t)))


def _gdn_chunk(states, qkv, small, z, a_row, dt_row, nw, tinvs=None):
    nb = len(qkv)
    c = qkv[0].shape[0]
    kw = GDN_HEADS * GDN_DK
    incl, strict = _tril(c), _tril(c, True)
    g_all = [-jnp.exp(a_row) * _softplus(small[b] + dt_row) for b in range(nb)]
    beta_all = [jax.nn.sigmoid(small[b]) for b in range(nb)]
    big_g_all = [_mask_dot(incl.astype(BF16), g_all[b]) for b in range(nb)]
    items = [(b, h) for b in range(nb) for h in range(GDN_HEADS)]
    ids = range(len(items))
    col = lambda b, part, h: qkv[b][:, part * kw + h * GDN_DK:part * kw + (h + 1) * GDN_DK]
    unit = lambda t: t * lax.rsqrt(jnp.sum(t * t, axis=-1, keepdims=True) + EPS)
    q = [unit(col(b, 0, h)) * (GDN_DK ** -0.5) for b, h in items]
    k = [unit(col(b, 1, h)) for b, h in items]
    v = [col(b, 2, h) for b, h in items]
    gc = [_lane_col(big_g_all[b], SM_A + h) for b, h in items]
    bc = [_lane_col(beta_all[b], SM_B + h) for b, h in items]
    g_last = [jnp.sum(_lane_col(g_all[b], SM_A + h), axis=0, keepdims=True) for b, h in items]
    decay = [jnp.where(incl, jnp.exp(jnp.where(incl, gc[i] - _col_to_row(gc[i]), 0.0)), 0.0) for i in ids]
    kb = [k[i] * bc[i] for i in ids]
    m = [jnp.where(strict, _bdot_nt(kb[i], k[i]) * decay[i], 0.0) for i in ids]
    if tinvs is None:
        tinv = _unit_lower_inverses(m)
    else:
        tinv = [_known_inverse(m[i], tinvs[i]) for i in ids]
    eg = [jnp.exp(gc[i]) for i in ids]
    u = [_dot3(tinv[i], v[i] * bc[i]) for i in ids]
    w = [_dot3(tinv[i], kb[i] * eg[i]) for i in ids]
    attn = [_bdot_nt(q[i], k[i]) * decay[i] for i in ids]
    v_new = [u[i] - _bdot(w[i], states[i]) for i in ids]
    o_st = [_bdot(q[i] * eg[i], states[i]) for i in ids]
    o = [o_st[i] + _bdot(attn[i], v_new[i]) for i in ids]
    grow = [_bdot_tn(k[i] * jnp.exp(g_last[i] - gc[i]), v_new[i]) for i in ids]
    new_states = [states[i] * jnp.exp(g_last[i]) + grow[i] for i in ids]
    outs = [_rms(o[i], nw) * _silu(z[b][:, h * GDN_DK:(h + 1) * GDN_DK]) for i, (b, h) in enumerate(items)]
    per_seq = [jnp.concatenate(outs[b * GDN_HEADS:(b + 1) * GDN_HEADS], axis=1) for b in range(nb)]
    return new_states, per_seq, tinv


def _seq_items(bsz, heads):
    return [(b, h) for b in range(bsz) for h in range(heads)]


def _gdn_fwd(qkv_act, p, a_row, dt_row, nw, name):
    bsz, s, _ = qkv_act.shape
    c = GDN_CHUNK
    nc = s // c
    items = _seq_items(bsz, GDN_HEADS)

    def body(qkv_ref, sm_ref, z_ref, a_ref, dt_ref, nw_ref, o_ref, st_ref, ti_ref, st_scr):
        @pl.when(pl.program_id(0) == 0)
        def _():
            st_scr[...] = jnp.zeros_like(st_scr)

        st_ref[...] = st_scr[...]
        seqs = range(bsz)
        new_states, o, tinvs = _gdn_chunk(
            [st_scr[b, h] for b, h in items], [qkv_ref[b] for b in seqs], [sm_ref[b] for b in seqs],
            [z_ref[b] for b in seqs], a_ref[...], dt_ref[...], nw_ref[...])
        for i, (b, h) in enumerate(items):
            st_scr[b, h] = new_states[i]
            ti_ref[b, h] = tinvs[i]
        for b in seqs:
            o_ref[b] = o[b].astype(BF16)

    row = lambda w, blk: pl.BlockSpec((bsz, c, w), lambda n, blk=blk: (0, n, blk))
    prm = pl.BlockSpec((1, LANES), lambda n: (0, 0))
    return _call(
        body, name, (nc,), [row(1536, 0), row(LANES, P_SMALL // LANES), row(512, P_GZ // 512), prm, prm, prm],
        [row(512, 0), pl.BlockSpec((bsz, None, 4, LANES, LANES), lambda n: (0, n, 0, 0, 0)),
         pl.BlockSpec((bsz, None, 4, c, c), lambda n: (0, n, 0, 0, 0))],
        [_sds((bsz, s, 512), BF16), _sds((bsz, nc, 4, LANES, LANES), F32), _sds((bsz, nc, 4, c, c), F32)],
        scratch=[pltpu.VMEM((bsz, 4, LANES, LANES), F32)],
    )(qkv_act, p, p, a_row, dt_row, nw)


def _gdn_bwd(do, qkv_act, p, a_row, dt_row, nw, st_all, ti_all, name):
    bsz, s, _ = qkv_act.shape
    c = GDN_CHUNK
    nc = s // c

    items = _seq_items(bsz, GDN_HEADS)

    def body(qkv_ref, sm_ref, z_ref, a_ref, dt_ref, nw_ref, do_ref, st_ref, ti_ref,
             dqkv_ref, dsm_ref, dz_ref, da_ref, ddt_ref, dnw_ref, ds_scr):
        @pl.when(pl.program_id(0) == 0)
        def _():
            ds_scr[...] = jnp.zeros_like(ds_scr)
            da_ref[...] = jnp.zeros_like(da_ref)
            ddt_ref[...] = jnp.zeros_like(ddt_ref)
            dnw_ref[...] = jnp.zeros_like(dnw_ref)

        seqs = range(bsz)
        tinvs = [ti_ref[b, h] for b, h in items]
        chunk = lambda *a: _gdn_chunk(*a, tinvs=tinvs)[:2]
        _, vjp = jax.vjp(chunk, [st_ref[b, h] for b, h in items], [qkv_ref[b] for b in seqs], [sm_ref[b] for b in seqs],
                         [z_ref[b] for b in seqs], a_ref[...], dt_ref[...], nw_ref[...])
        d_states, dqkv, dsm, dz, da, ddt, dnw = vjp(([ds_scr[b, h] for b, h in items], [do_ref[b] for b in seqs]))
        for i, (b, h) in enumerate(items):
            ds_scr[b, h] = d_states[i]
        for b in seqs:
            dqkv_ref[b] = dqkv[b]
            dsm_ref[b] = dsm[b]
            dz_ref[b] = dz[b].astype(BF16)
        da_ref[...] += da
        ddt_ref[...] += ddt
        dnw_ref[...] += dnw

    rrow = lambda w, blk: pl.BlockSpec((bsz, c, w), lambda n, blk=blk: (0, nc - 1 - n, blk))
    prm = pl.BlockSpec((1, LANES), lambda n: (0, 0))
    return _call(
        body, name, (nc,),
        [rrow(1536, 0), rrow(LANES, P_SMALL // LANES), rrow(512, P_GZ // 512), prm, prm, prm, rrow(512, 0),
         pl.BlockSpec((bsz, None, 4, LANES, LANES), lambda n: (0, nc - 1 - n, 0, 0, 0)),
         pl.BlockSpec((bsz, None, 4, c, c), lambda n: (0, nc - 1 - n, 0, 0, 0))],
        [rrow(1536, 0), rrow(LANES, 0), rrow(512, 0), prm, prm, prm],
        [_sds((bsz, s, 1536), F32), _sds((bsz, s, LANES), F32), _sds((bsz, s, 512), BF16)] + [_sds((1, LANES), F32)] * 3,
        scratch=[pltpu.VMEM((bsz, 4, LANES, LANES), F32)],
    )(qkv_act, p, p, a_row, dt_row, nw, do, st_all, ti_all)


def _hgrn_block(states, q_raw, f_raw, i_raw, g_raw, lb, nw):
    n = q_raw[0].shape[0]
    c = HGRN_CHUNK
    r, cc = _iota((n, n), 0), _iota((n, n), 1)
    same = (r // c) == (cc // c)
    causal = same & (r >= cc)
    ref_row = (r // c) * c + (c // 2 - 1)
    run_sum = causal.astype(F32)
    rel_sum = run_sum - (same & (ref_row >= cc)).astype(F32)
    sums = jnp.concatenate([run_sum, rel_sum, same.astype(F32)], axis=0).astype(BF16)
    seqs, chunks = range(len(q_raw)), range(n // c)
    items = _seq_items(len(q_raw), HGRN_HEADS)
    hs = lambda t, h: t[:, h * HGRN_DK:(h + 1) * HGRN_DK]
    rows = lambda t, j: t[j * c:(j + 1) * c]
    q = [_silu(q_raw[b]) for b in seqs]
    logf = [jnp.log(lb + (1.0 - lb) * jax.nn.sigmoid(f_raw[b])) for b in seqs]
    k = [(1.0 - lb) * jax.nn.sigmoid(-f_raw[b]) for b in seqs]
    all_sums = [_mask_dot(sums, logf[b]) for b in seqs]
    big_g, g_rel, g_tot = ([t[i * n:(i + 1) * n] for t in all_sums] for i in range(3))
    q_rel = [q[b] * jnp.exp(g_rel[b]) for b in seqs]
    k_rel = [k[b] * jnp.exp(-g_rel[b]) for b in seqs]
    qg = [q[b] * jnp.exp(big_g[b]) for b in seqs]
    k_end = [k[b] * jnp.exp(g_tot[b] - big_g[b]) for b in seqs]
    keep = [[jnp.exp(g_tot[b][j * c:j * c + 1]) for j in chunks] for b in seqs]
    scores = [_bdot_nt(hs(q_rel[b], h), hs(k_rel[b], h)) for b, h in items]
    o_intra = [_bdot(jnp.where(causal, scores[i], 0.0), hs(i_raw[b], h)) for i, (b, h) in enumerate(items)]
    grow = [[_bdot_tn(rows(hs(i_raw[b], h), j), rows(hs(k_end[b], h), j)) for j in chunks] for b, h in items]
    entering, new_states = [], []
    for i, (b, h) in enumerate(items):
        st, per_chunk = states[i], []
        for j in chunks:
            per_chunk.append(st)
            st = st * hs(keep[b][j], h) + grow[i][j]
        entering.append(per_chunk)
        new_states.append(st)
    o_inter = [[_bdot_nt(rows(hs(qg[b], h), j), entering[i][j]) for j in chunks] for i, (b, h) in enumerate(items)]
    outs = [_rms(o_intra[i] + jnp.concatenate(o_inter[i], axis=0), nw) * _silu(hs(g_raw[b], h))
            for i, (b, h) in enumerate(items)]
    return new_states, [jnp.concatenate(outs[b * HGRN_HEADS:(b + 1) * HGRN_HEADS], axis=1) for b in seqs]


def _hgrn_fwd(p, lb, nw, name):
    bsz, s, _ = p.shape
    n = HGRN_BLOCK
    nb = s // n

    items = _seq_items(bsz, HGRN_HEADS)

    def body(q_ref, f_ref, i_ref, g_ref, lb_ref, nw_ref, o_ref, st_ref, st_scr):
        @pl.when(pl.program_id(0) == 0)
        def _():
            st_scr[...] = jnp.zeros_like(st_scr)

        st_ref[...] = st_scr[...]
        per_seq = lambda ref: [ref[b] for b in range(bsz)]
        new_states, o = _hgrn_block([st_scr[b, h] for b, h in items], per_seq(q_ref), per_seq(f_ref), per_seq(i_ref),
                                    per_seq(g_ref), lb_ref[...], nw_ref[...])
        for i, (b, h) in enumerate(items):
            st_scr[b, h] = new_states[i]
        for b in range(bsz):
            o_ref[b] = o[b].astype(BF16)

    row = lambda blk: pl.BlockSpec((bsz, n, 512), lambda i, blk=blk: (0, i, blk))
    return _call(
        body, name, (nb,),
        [row(P_HQ // 512), row(P_HF // 512), row(P_HI // 512), row(P_HG // 512),
         pl.BlockSpec((1, 512), lambda i: (0, 0)), pl.BlockSpec((1, LANES), lambda i: (0, 0))],
        [row(0), pl.BlockSpec((bsz, None, 4, LANES, LANES), lambda i: (0, i, 0, 0, 0))],
        [_sds((bsz, s, 512), BF16), _sds((bsz, nb, 4, LANES, LANES), F32)],
        scratch=[pltpu.VMEM((bsz, 4, LANES, LANES), F32)],
    )(p, p, p, p, lb, nw)


def _hgrn_bwd(do, p, lb, nw, st_all, name):
    bsz, s, _ = p.shape
    n = HGRN_BLOCK
    nb = s // n

    items = _seq_items(bsz, HGRN_HEADS)

    def body(q_ref, f_ref, i_ref, g_ref, lb_ref, nw_ref, do_ref, st_ref, dp_ref, dlb_ref, dnw_ref, ds_scr):
        @pl.when(pl.program_id(0) == 0)
        def _():
            ds_scr[...] = jnp.zeros_like(ds_scr)
            dlb_ref[...] = jnp.zeros_like(dlb_ref)
            dnw_ref[...] = jnp.zeros_like(dnw_ref)

        per_seq = lambda ref: [ref[b] for b in range(bsz)]
        _, vjp = jax.vjp(_hgrn_block, [st_ref[b, h] for b, h in items], per_seq(q_ref), per_seq(f_ref), per_seq(i_ref),
                         per_seq(g_ref), lb_ref[...], nw_ref[...])
        d_states, dq, df, di, dg, dlb, dnw = vjp(([ds_scr[b, h] for b, h in items], per_seq(do_ref)))
        for i, (b, h) in enumerate(items):
            ds_scr[b, h] = d_states[i]
        for b in range(bsz):
            for j, t in enumerate((dq, df, di, dg)):
                dp_ref[b, :, j * 512:(j + 1) * 512] = t[b].astype(BF16)
        dlb_ref[...] += dlb
        dnw_ref[...] += dnw

    row = lambda blk: pl.BlockSpec((bsz, n, 512), lambda i, blk=blk: (0, nb - 1 - i, blk))
    return _call(
        body, name, (nb,),
        [row(P_HQ // 512), row(P_HF // 512), row(P_HI // 512), row(P_HG // 512),
         pl.BlockSpec((1, 512), lambda i: (0, 0)), pl.BlockSpec((1, LANES), lambda i: (0, 0)), row(0),
         pl.BlockSpec((bsz, None, 4, LANES, LANES), lambda i: (0, nb - 1 - i, 0, 0, 0))],
        [pl.BlockSpec((bsz, n, 2048), lambda i: (0, nb - 1 - i, 0)),
         pl.BlockSpec((1, 512), lambda i: (0, 0)), pl.BlockSpec((1, LANES), lambda i: (0, 0))],
        [_sds((bsz, s, 2048), BF16), _sds((1, 512), F32), _sds((1, LANES), F32)],
        scratch=[pltpu.VMEM((bsz, 4, LANES, LANES), F32)],
    )(p, p, p, p, lb, nw, do, st_all)


def _ssd_chunk(states, xbc, small, z, a_row, dt_row, d_row, nw):
    seqs = range(len(xbc))
    c = xbc[0].shape[0]
    incl = _tril(c)
    spread = (_iota((LANES, SSD_INNER), 0) == SM_DT + _iota((LANES, SSD_INNER), 1) // SSD_HEAD_DIM).astype(BF16)
    dt_all = [_softplus(small[b] + dt_row) for b in seqs]
    both = [_spread_dot(jnp.concatenate([dt_all[b], dt_all[b] * (-jnp.exp(a_row))], axis=0), spread) for b in seqs]
    dt_e, da_e = [t[:c] for t in both], [t[c:] for t in both]
    acs_e = [_mask_dot(incl.astype(BF16), da_e[b]) for b in seqs]
    last_e = [jnp.sum(da_e[b], axis=0, keepdims=True) for b in seqs]
    xs = [xbc[b][:, :SSD_INNER] for b in seqs]
    xdt = [xs[b] * dt_e[b] for b in seqs]
    gw = SSD_GROUPS * SSD_STATE
    lane = _iota((1, LANES), 1)
    items = _seq_items(len(xbc), 4)
    grp = [(b, g) for b in seqs for g in range(SSD_GROUPS)]
    ps = lambda t, j: t[:, j * LANES:(j + 1) * LANES]
    bg = {(b, g): xbc[b][:, SSD_INNER + g * SSD_STATE:SSD_INNER + (g + 1) * SSD_STATE] for b, g in grp}
    cg = {(b, g): xbc[b][:, SSD_INNER + gw + g * SSD_STATE:SSD_INNER + gw + (g + 1) * SSD_STATE] for b, g in grp}
    cb = {bgk: _bdot_nt(cg[bgk], bg[bgk]) for bgk in grp}

    def seg(b, j, sub):
        ac = ps(acs_e[b], j)[:, sub * SSD_HEAD_DIM:sub * SSD_HEAD_DIM + 1]
        return jnp.where(incl, jnp.exp(jnp.where(incl, ac - _col_to_row(ac), 0.0)), 0.0)

    mine = [((lane // SSD_HEAD_DIM) == sub).astype(F32) for sub in range(2)]
    y_in = [[_bdot(cb[b, j // 2] * seg(b, j, sub), ps(xdt[b], j) * mine[sub]) for sub in range(2)] for b, j in items]
    y_st = [_bdot(cg[b, j // 2], states[i]) for i, (b, j) in enumerate(items)]
    grow = [_bdot_tn(bg[b, j // 2], ps(xdt[b], j) * jnp.exp(ps(last_e[b], j) - ps(acs_e[b], j))) for b, j in items]
    new_states = [states[i] * jnp.exp(ps(last_e[b], j)) + grow[i] for i, (b, j) in enumerate(items)]
    ys = [y_in[i][0] + y_in[i][1] + y_st[i] * jnp.exp(ps(acs_e[b], j)) + ps(d_row, j) * ps(xs[b], j)
          for i, (b, j) in enumerate(items)]
    gwid = SSD_INNER // SSD_GROUPS
    outs = []
    for b in seqs:
        yz = jnp.concatenate(ys[4 * b:4 * b + 4], axis=1) * _silu(z[b])
        outs.append(jnp.concatenate(
            [_rms(yz[:, g * gwid:(g + 1) * gwid], nw[:, g * gwid:(g + 1) * gwid]) for g in range(SSD_GROUPS)], axis=1))
    return new_states, outs


def _ssd_fwd(xbc_act, p, a_row, dt_row, d_row, nw, name):
    bsz, s, _ = xbc_act.shape
    c = SSD_CHUNK
    nc = s // c

    items = _seq_items(bsz, 4)

    def body(x_ref, sm_ref, z_ref, a_ref, dt_ref, d_ref, nw_ref, o_ref, st_ref, st_scr):
        @pl.when(pl.program_id(0) == 0)
        def _():
            st_scr[...] = jnp.zeros_like(st_scr)

        st_ref[...] = st_scr[...]
        per_seq = lambda ref: [ref[b] for b in range(bsz)]
        new_states, o = _ssd_chunk([st_scr[b, j] for b, j in items], per_seq(x_ref), per_seq(sm_ref), per_seq(z_ref),
                                   a_ref[...], dt_ref[...], d_ref[...], nw_ref[...])
        for i, (b, j) in enumerate(items):
            st_scr[b, j] = new_states[i]
        for b in range(bsz):
            o_ref[b] = o[b].astype(BF16)

    row = lambda w, blk: pl.BlockSpec((bsz, c, w), lambda n, blk=blk: (0, n, blk))
    prm = pl.BlockSpec((1, LANES), lambda n: (0, 0))
    prm5 = pl.BlockSpec((1, 512), lambda n: (0, 0))
    return _call(
        body, name, (nc,),
        [row(1024, 0), row(LANES, P_SMALL // LANES), row(512, P_SZ // 512), prm, prm, prm5, prm5],
        [row(512, 0), pl.BlockSpec((bsz, None, 4, LANES, LANES), lambda n: (0, n, 0, 0, 0))],
        [_sds((bsz, s, 512), BF16), _sds((bsz, nc, 4, LANES, LANES), F32)],
        scratch=[pltpu.VMEM((bsz, 4, LANES, LANES), F32)],
    )(xbc_act, p, p, a_row, dt_row, d_row, nw)


def _ssd_bwd(do, xbc_act, p, a_row, dt_row, d_row, nw, st_all, name):
    bsz, s, _ = xbc_act.shape
    c = SSD_CHUNK
    nc = s // c

    items = _seq_items(bsz, 4)

    def body(x_ref, sm_ref, z_ref, a_ref, dt_ref, d_ref, nw_ref, do_ref, st_ref,
             dx_ref, dsm_ref, dz_ref, da_ref, ddt_ref, dd_ref, dnw_ref, ds_scr):
        @pl.when(pl.program_id(0) == 0)
        def _():
            ds_scr[...] = jnp.zeros_like(ds_scr)
            da_ref[...] = jnp.zeros_like(da_ref)
            ddt_ref[...] = jnp.zeros_like(ddt_ref)
            dd_ref[...] = jnp.zeros_like(dd_ref)
            dnw_ref[...] = jnp.zeros_like(dnw_ref)

        per_seq = lambda ref: [ref[b] for b in range(bsz)]
        _, vjp = jax.vjp(_ssd_chunk, [st_ref[b, j] for b, j in items], per_seq(x_ref), per_seq(sm_ref), per_seq(z_ref),
                         a_ref[...], dt_ref[...], d_ref[...], nw_ref[...])
        d_states, dx, dsm, dz, da, ddt, dd, dnw = vjp(([ds_scr[b, j] for b, j in items], per_seq(do_ref)))
        for i, (b, j) in enumerate(items):
            ds_scr[b, j] = d_states[i]
        for b in range(bsz):
            dx_ref[b] = dx[b]
            dsm_ref[b] = dsm[b]
            dz_ref[b] = dz[b].astype(BF16)
        da_ref[...] += da
        ddt_ref[...] += ddt
        dd_ref[...] += dd
        dnw_ref[...] += dnw

    row = lambda w, blk: pl.BlockSpec((bsz, c, w), lambda n, blk=blk: (0, nc - 1 - n, blk))
    prm = pl.BlockSpec((1, LANES), lambda n: (0, 0))
    prm5 = pl.BlockSpec((1, 512), lambda n: (0, 0))
    return _call(
        body, name, (nc,),
        [row(1024, 0), row(LANES, P_SMALL // LANES), row(512, P_SZ // 512), prm, prm, prm5, prm5, row(512, 0),
         pl.BlockSpec((bsz, None, 4, LANES, LANES), lambda n: (0, nc - 1 - n, 0, 0, 0))],
        [row(1024, 0), row(LANES, 0), row(512, 0), prm, prm, prm5, prm5],
        [_sds((bsz, s, 1024), F32), _sds((bsz, s, LANES), F32), _sds((bsz, s, 512), BF16),
         _sds((1, LANES), F32), _sds((1, LANES), F32), _sds((1, 512), F32), _sds((1, 512), F32)],
        scratch=[pltpu.VMEM((bsz, 4, LANES, LANES), F32)],
    )(xbc_act, p, p, a_row, dt_row, d_row, nw, do, st_all)


def _peer(k):
    x, y, c = lax.axis_index("x"), lax.axis_index("y"), lax.axis_index("c")
    px = 1 - x if k & 4 else x
    py = 1 - y if k & 2 else y
    pc = 1 - c if k & 1 else c
    return (px, py, pc), 4 * px + 2 * py + pc


def _my_index():
    return 4 * lax.axis_index("x") + 2 * lax.axis_index("y") + lax.axis_index("c")


def _mesh_place():
    x, y, c = lax.axis_index("x"), lax.axis_index("y"), lax.axis_index("c")
    return (x, y, c), (x, y, 1 - c), [(1 - x, y), (x, 1 - y), (1 - x, 1 - y)]


def _run_exchange(body, name, arrays, out_shape, n_sems):
    n = len(arrays)
    any_spec = pl.BlockSpec(memory_space=pl.ANY)
    return pl.pallas_call(
        body, name=name, out_shape=out_shape, in_specs=[any_spec] * n, out_specs=[any_spec] * n,
        scratch_shapes=[pltpu.SemaphoreType.DMA((n_sems, n)), pltpu.SemaphoreType.DMA((n_sems, n)),
                        pltpu.SemaphoreType.DMA((n,))],
    )(*arrays)


def _all_to_all(arrays, name):
    n = len(arrays)

    def body(*refs):
        ins, outs = refs[:n], refs[n:2 * n]
        send_sems, recv_sems, local_sems = refs[2 * n:]
        me = _my_index()

        def copy(i, k, arriving):
            peer, slot = _peer(k)
            return pltpu.make_async_remote_copy(
                src_ref=ins[i].at[slot], dst_ref=outs[i].at[slot if arriving else me], send_sem=send_sems.at[k - 1, i],
                recv_sem=recv_sems.at[k - 1, i], device_id=peer, device_id_type=MESH_ID)

        mine = [pltpu.make_async_copy(ins[i].at[me], outs[i].at[me], local_sems.at[i]) for i in range(n)]
        sends = [copy(i, k, False) for k in range(1, N_DEV) for i in range(n)]
        for cp in mine + sends:
            cp.start()
        for k in range(1, N_DEV):
            for i in range(n):
                copy(i, k, True).wait_recv()
        for cp in sends:
            cp.wait_send()
        for cp in mine:
            cp.wait()

    return _run_exchange(body, name, arrays, [_sds(a.shape, a.dtype) for a in arrays], N_DEV - 1)


def _gather_two_level(arrays, name):
    n = len(arrays)

    def body(*refs):
        ins, outs = refs[:n], refs[n:2 * n]
        send_sems, recv_sems, local_sems = refs[2 * n:]
        (x, y, c), sibling, chips = _mesh_place()
        slot = lambda px, py, pc: 4 * px + 2 * py + pc

        def copy(i, k, block, to, src=None):
            return pltpu.make_async_remote_copy(
                src_ref=outs[i].at[block] if src is None else src, dst_ref=outs[i].at[block],
                send_sem=send_sems.at[k, i], recv_sem=recv_sems.at[k, i], device_id=to, device_id_type=MESH_ID)

        me = slot(x, y, c)
        mine = [pltpu.make_async_copy(ins[i], outs[i].at[me], local_sems.at[i]) for i in range(n)]
        first = [copy(i, 0, me, sibling, src=ins[i]) for i in range(n)]
        first += [copy(i, 1 + j, me, (*chip, c), src=ins[i]) for j, chip in enumerate(chips) for i in range(n)]
        for cp in mine + first:
            cp.start()
        passed = []
        for j, chip in enumerate(chips):
            for i in range(n):
                copy(i, 1 + j, slot(*chip, c), (x, y, c)).wait_recv()
                cp = copy(i, 4 + j, slot(*chip, c), sibling)
                cp.start()
                passed.append(cp)
        for i in range(n):
            copy(i, 0, slot(x, y, 1 - c), (x, y, c)).wait_recv()
        for j, chip in enumerate(chips):
            for i in range(n):
                copy(i, 4 + j, slot(*chip, 1 - c), (x, y, c)).wait_recv()
        for cp in first + passed:
            cp.wait_send()
        for cp in mine:
            cp.wait()

    out_shape = [_sds((N_DEV,) + a.shape, a.dtype) for a in arrays]
    return _run_exchange(body, name, arrays, out_shape, 7)


def _sibling_swap(arrays, name):
    n = len(arrays)

    def body(*refs):
        ins, outs = refs[:n], refs[n:2 * n]
        send_sems, recv_sems, _ = refs[2 * n:]
        (x, y, c), sibling, _ = _mesh_place()
        copies = [pltpu.make_async_remote_copy(
            src_ref=ins[i].at[1 - c], dst_ref=outs[i], send_sem=send_sems.at[0, i], recv_sem=recv_sems.at[0, i],
            device_id=sibling, device_id_type=MESH_ID) for i in range(n)]
        for cp in copies:
            cp.start()
        for cp in copies:
            cp.wait()

    out_shape = [_sds(a.shape[1:], a.dtype) for a in arrays]
    return _run_exchange(body, name, arrays, out_shape, 1)


def _chip_scatter(arrays, name):
    n = len(arrays)

    def body(*refs):
        ins, outs = refs[:n], refs[n:2 * n]
        send_sems, recv_sems, local_sems = refs[2 * n:]
        (x, y, c), _, chips = _mesh_place()
        me = 2 * x + y
        mine = [pltpu.make_async_copy(ins[i].at[me], outs[i].at[me], local_sems.at[i]) for i in range(n)]
        sends = [pltpu.make_async_remote_copy(
            src_ref=ins[i].at[2 * chip[0] + chip[1]], dst_ref=outs[i].at[me], send_sem=send_sems.at[j, i],
            recv_sem=recv_sems.at[j, i], device_id=(*chip, c), device_id_type=MESH_ID)
            for j, chip in enumerate(chips) for i in range(n)]
        for cp in mine + sends:
            cp.start()
        for j, chip in enumerate(chips):
            for i in range(n):
                pltpu.make_async_remote_copy(
                    src_ref=ins[i].at[me], dst_ref=outs[i].at[2 * chip[0] + chip[1]], send_sem=send_sems.at[j, i],
                    recv_sem=recv_sems.at[j, i], device_id=(*chip, c), device_id_type=MESH_ID).wait_recv()
        for cp in sends:
            cp.wait_send()
        for cp in mine:
            cp.wait()

    out_shape = [_sds(a.shape, a.dtype) for a in arrays]
    return _run_exchange(body, name, arrays, out_shape, 3)


def _pair_sum(a, b, name):
    lead, rows, width = a.shape
    tr = _pick(rows, (256, 128, 64, 32, 16, 8)) if rows % 8 == 0 else rows

    def body(a_ref, b_ref, o_ref):
        o_ref[...] = (a_ref[...].astype(F32) + b_ref[...].astype(F32)).astype(o_ref.dtype)

    blk = pl.BlockSpec((None, tr, width), lambda l, i: (l, i, 0))
    return _call(body, name, (lead, rows // tr), [blk, blk], blk, _sds(a.shape, a.dtype))(a, b)


def _sum_adamw(gs, w, m, v, name):
    lead, rows, width = w.shape
    slots = gs.shape[0]
    tr = _pick(rows, (128, 64, 32, 16, 8)) if rows % 8 == 0 else rows

    def body(g_ref, w_ref, m_ref, v_ref, go_ref, d_ref, mo_ref, vo_ref):
        g = g_ref[0].astype(F32)
        for i in range(1, slots):
            g = g + g_ref[i].astype(F32)
        m2 = ADAM_B1 * m_ref[...] + (1.0 - ADAM_B1) * g
        v2 = ADAM_B2 * v_ref[...] + (1.0 - ADAM_B2) * (g * g)
        m_hat = m2 / (1.0 - ADAM_B1 ** ADAM_STEP)
        v_hat = v2 / (1.0 - ADAM_B2 ** ADAM_STEP)
        go_ref[...] = g
        d_ref[...] = -ADAM_LR * (m_hat / (jnp.sqrt(v_hat) + ADAM_EPS) + ADAM_WD * w_ref[...])
        mo_ref[...] = m2
        vo_ref[...] = v2

    blk = pl.BlockSpec((None, tr, width), lambda l, i: (l, i, 0))
    return _call(body, name, (lead, rows // tr),
                 [pl.BlockSpec((slots, None, tr, width), lambda l, i: (0, l, i, 0)), blk, blk, blk],
                 [blk] * 4, [_sds(w.shape, F32)] * 4)(gs, w, m, v)


MATMUL_WEIGHTS = ("w_in", "w_br_a", "w_br_b", "w_br_c", "w_out", "ffn_w_up", "ffn_w_down")
UNALIGNED = ("w_in", "ffn_w_up")
SPLIT = (
    ("w_in", (DEPTH, D_MODEL, 8720), 2),
    ("gdn_conv_w", (DEPTH, 4, 1536), 2), ("ssd_conv_w", (DEPTH, 4, 1024), 2),
    ("w_br_a", (DEPTH, 512, D_MODEL), 2), ("w_br_b", (DEPTH, 512, D_MODEL), 2), ("w_br_c", (DEPTH, 512, D_MODEL), 2),
    ("w_out", (DEPTH, D_MODEL, D_MODEL), 1), ("ffn_w_up", (DEPTH, D_MODEL, 2 * FFN_HIDDEN), 2),
    ("ffn_conv_w", (DEPTH, 3, 2 * FFN_HIDDEN), 2), ("ffn_w_down", (DEPTH, FFN_HIDDEN, D_MODEL), 1),
)
REPL = (
    ("b_ada", (DEPTH, 6 * D_MODEL)), ("norm1_w", (DEPTH, D_MODEL)), ("gdn_a_log", (DEPTH, 4)),
    ("gdn_dt_bias", (DEPTH, 4)), ("gdn_norm_w", (DEPTH, 128)), ("hgrn_lb_param", (DEPTH, 512)),
    ("hgrn_norm_w", (DEPTH, 128)), ("ssd_conv_b", (DEPTH, 1024)), ("ssd_a_log", (DEPTH, 8)),
    ("ssd_dt_bias", (DEPTH, 8)), ("ssd_d", (DEPTH, 8)), ("ssd_norm_w", (DEPTH, 512)), ("norm2_w", (DEPTH, D_MODEL)),
    ("ffn_conv_b", (DEPTH, 2 * FFN_HIDDEN)), ("final_norm_w", (D_MODEL,)),
)
WEIGHTS = ("w_ada", "b_ada", "norm1_w", "w_in", "gdn_conv_w", "gdn_a_log", "gdn_dt_bias", "gdn_norm_w",
           "hgrn_lb_param", "hgrn_norm_w", "ssd_conv_w", "ssd_conv_b", "ssd_a_log", "ssd_dt_bias", "ssd_d",
           "ssd_norm_w", "w_br_a", "w_br_b", "w_br_c", "w_out", "norm2_w", "ffn_w_up", "ffn_conv_w", "ffn_conv_b",
           "ffn_w_down", "final_norm_w")


def _block_shape(shape, axis):
    return tuple(d // N_DEV if i == axis else d for i, d in enumerate(shape))


def _join_blocks(gathered, shape, axis):
    return jnp.moveaxis(gathered, 0, axis).reshape(shape)


def _split_blocks(full, shape, axis):
    bs = _block_shape(shape, axis)
    t = full.reshape(shape[:axis] + (N_DEV, bs[axis]) + shape[axis + 1:])
    return jnp.moveaxis(t, axis, 0)


def _pack_repl(vals):
    parts = []
    for n, shape in REPL:
        size = math.prod(shape)
        parts.append(jnp.pad(vals[n].reshape(-1), (0, -(-size // PACK_W) * PACK_W - size)))
    cat = jnp.concatenate(parts)
    rows = -(-cat.shape[0] // (8 * PACK_W)) * 8
    return jnp.pad(cat, (0, rows * PACK_W - cat.shape[0])).reshape(rows, PACK_W)


def _unpack_repl(packed):
    flat, out, off = packed.reshape(-1), {}, 0
    for n, shape in REPL:
        size = math.prod(shape)
        out[n] = flat[off:off + size].reshape(shape)
        off += -(-size // PACK_W) * PACK_W
    return out


def _lane_row(vec, lane0):
    return jnp.pad(vec, (lane0, LANES - lane0 - vec.shape[0]))[None]


def _arrange_w_in(w):
    offs = [0]
    for sz in W_IN_SPLITS:
        offs.append(offs[-1] + sz)
    qkv, a, b, gz, hq, hf, hi, hg, sz_, xbc, dt, gate = [w[:, offs[i]:offs[i + 1]] for i in range(12)]
    pad = jnp.zeros((w.shape[0], P_WIDTH - P_SMALL - 16), w.dtype)
    return jnp.concatenate([qkv, gz, xbc, gate, hq, hf, hi, hg, sz_, a, b, dt, pad], axis=1)


def _restore_w_in(wp):
    cut = lambda o, n: wp[:, o:o + n]
    return jnp.concatenate([
        cut(P_QKV, 1536), cut(P_SMALL + SM_A, 4), cut(P_SMALL + SM_B, 4), cut(P_GZ, 512), cut(P_HQ, 512),
        cut(P_HF, 512), cut(P_HI, 512), cut(P_HG, 512), cut(P_SZ, 512), cut(P_XBC, 1024), cut(P_SMALL + SM_DT, 8),
        cut(P_GATE, 3072)], axis=1)


def _join_cols(gathered, arrange, name):
    _, depth, rows, cols = gathered.shape
    tr = _pick(rows, (256, 128, 64, 32, 16, 8))
    width = P_WIDTH if arrange else N_DEV * cols

    def body(g_ref, o_ref):
        row = jnp.concatenate([g_ref[d] for d in range(N_DEV)], axis=1)
        o_ref[...] = _arrange_w_in(row) if arrange else row

    return _call(body, name, (depth, rows // tr),
                 [pl.BlockSpec((N_DEV, None, tr, cols), lambda l, i: (0, l, i, 0))],
                 pl.BlockSpec((None, tr, width), lambda l, i: (l, i, 0)), _sds((depth, rows, width), gathered.dtype),
                 )(gathered)


def _split_cols(per_layer, restore, cols, name):
    depth = len(per_layer)
    rows = per_layer[0].shape[0]
    tr = _pick(rows, (256, 128, 64, 32, 16, 8))
    nt = rows // tr

    def body(*refs):
        o_ref = refs[depth]
        for l in range(depth):
            @pl.when(pl.program_id(0) == l)
            def _(l=l):
                row = _restore_w_in(refs[l][...]) if restore else refs[l][...]
                for d in range(N_DEV):
                    o_ref[d % 2, d // 2] = row[:, d * cols:(d + 1) * cols]

    return _call(body, name, (depth, nt),
                 [pl.BlockSpec((tr, a.shape[1]), lambda l, i: (i, 0)) for a in per_layer],
                 pl.BlockSpec((2, N_DEV // 2, tr, cols), lambda l, i: (0, 0, l * nt + i, 0)),
                 _sds((2, N_DEV // 2, depth * rows, cols), per_layer[0].dtype))(*per_layer)


def _layer_consts(l, wf, wr, lower):
    t = lambda a: a.T
    k = {}
    k["n1w"], k["n2w"] = wr["norm1_w"][l][None], wr["norm2_w"][l][None]
    k["win"], k["win_t"] = wf["w_in"][l], t(wf["w_in"][l])
    for n in ("w_br_a", "w_br_b", "w_br_c", "w_out", "ffn_w_up", "ffn_w_down"):
        k[n], k[n + "_t"] = wf[n][l], t(wf[n][l])
    k["gdn_conv_w"], k["gdn_conv_b"] = wf["gdn_conv_w"][l], jnp.zeros((1, 1536), F32)
    k["ssd_conv_w"], k["ssd_conv_b"] = wf["ssd_conv_w"][l], wr["ssd_conv_b"][l][None]
    k["ffn_conv_w"], k["ffn_conv_b"] = wf["ffn_conv_w"][l], wr["ffn_conv_b"][l][None]
    k["gdn_a"], k["gdn_dt"] = _lane_row(wr["gdn_a_log"][l], SM_A), _lane_row(wr["gdn_dt_bias"][l], SM_A)
    k["gdn_nw"], k["hgrn_nw"] = wr["gdn_norm_w"][l][None], wr["hgrn_norm_w"][l][None]
    k["ssd_a"], k["ssd_dt"] = _lane_row(wr["ssd_a_log"][l], SM_DT), _lane_row(wr["ssd_dt_bias"][l], SM_DT)
    k["ssd_d"] = jnp.repeat(wr["ssd_d"][l], SSD_HEAD_DIM)[None]
    k["ssd_nw"] = wr["ssd_norm_w"][l][None]
    k["lb"] = lower[l:l + 1]
    return k


def _layer_fwd(l, x, mod, k):
    bsz, s, d = x.shape
    t = bsz * s
    sv = {"x": x}
    sv["mod"] = [mod[:, None, i * d:(i + 1) * d] for i in range(6)]
    sh1, sc1, g1, sh2, sc2, g2 = sv["mod"]
    h1 = _norm_mod_fwd(x, k["n1w"], sh1, sc1, f"norm1_fwd{l}")
    p = _mm(h1.reshape(t, d), k["win"], F32, f"mm_in{l}").reshape(bsz, s, P_WIDTH)
    qkv_act = _conv_fwd(p, P_QKV, 1536, k["gdn_conv_w"], k["gdn_conv_b"], f"gdn_conv_fwd{l}")
    oa, st_a, ti_a = _gdn_fwd(qkv_act, p, k["gdn_a"], k["gdn_dt"], k["gdn_nw"], f"gdn_fwd{l}")
    ob, st_b = _hgrn_fwd(p, k["lb"], k["hgrn_nw"], f"hgrn_fwd{l}")
    xbc_act = _conv_fwd(p, P_XBC, 1024, k["ssd_conv_w"], k["ssd_conv_b"], f"ssd_conv_fwd{l}")
    oc, st_c = _ssd_fwd(xbc_act, p, k["ssd_a"], k["ssd_dt"], k["ssd_d"], k["ssd_nw"], f"ssd_fwd{l}")
    merged = _merge_fwd(p, oa, ob, oc, k["w_br_a"], k["w_br_b"], k["w_br_c"], f"merge_fwd{l}")
    mix = _mm(merged.reshape(t, d), k["w_out"], F32, f"mm_out{l}").reshape(bsz, s, d)
    x1 = _resid_fwd(x, mix, g1, f"resid1_fwd{l}")
    h2 = _norm_mod_fwd(x1, k["n2w"], sh2, sc2, f"norm2_fwd{l}")
    u_pre = _mm(h2.reshape(t, d), k["ffn_w_up"], F32, f"mm_up{l}").reshape(bsz, s, 2 * FFN_HIDDEN)
    a = _conv_glu_fwd(u_pre, k["ffn_conv_w"], k["ffn_conv_b"], f"ffn_conv_glu_fwd{l}")
    ffn = _mm(a.reshape(t, FFN_HIDDEN), k["ffn_w_down"], F32, f"mm_down{l}").reshape(bsz, s, d)
    x2 = _resid_fwd(x1, ffn, g2, f"resid2_fwd{l}")
    sv.update(h1=h1, p=p, qkv_act=qkv_act, oa=oa, st_a=st_a, ti_a=ti_a, ob=ob, st_b=st_b, xbc_act=xbc_act, oc=oc, st_c=st_c,
              merged=merged, mix=mix, x1=x1, h2=h2, u_pre=u_pre, a=a, ffn=ffn)
    return x2, sv


def _layer_bwd(l, dx2, k, sv):
    bsz, s, d = dx2.shape
    t = bsz * s
    f2 = 2 * FFN_HIDDEN
    sh1, sc1, g1, sh2, sc2, g2 = sv["mod"]
    tr = lambda a: a.reshape(t, -1).T
    g = {}
    dffn, dg2 = _gate_bwd(dx2, sv["ffn"], g2, f"gate2_bwd{l}")
    dffn2 = dffn.reshape(t, d)
    da = _mm(dffn2, k["ffn_w_down_t"], F32, f"mm_down_dx{l}").reshape(bsz, s, FFN_HIDDEN)
    g["ffn_w_down"] = _mm(tr(sv["a"]), dffn2, BF16, f"mm_down_dw{l}")
    du_pre, g["ffn_conv_w"], dfcb = _conv_glu_bwd(da, sv["u_pre"], k["ffn_conv_w"], k["ffn_conv_b"], f"ffn_conv_glu_bwd{l}")
    g["ffn_conv_b"] = dfcb[0]
    du2 = du_pre.reshape(t, f2)
    dh2 = _mm(du2, k["ffn_w_up_t"], F32, f"mm_up_dx{l}").reshape(bsz, s, d)
    g["ffn_w_up"] = _mm(tr(sv["h2"]), du2, BF16, f"mm_up_dw{l}")
    dx1, dn2w, dsh2, dsc2 = _norm_mod_bwd(sv["x1"], k["n2w"], sh2, sc2, dh2, dx2, f"norm2_bwd{l}")
    g["norm2_w"] = dn2w[0]
    dmix, dg1 = _gate_bwd(dx1, sv["mix"], g1, f"gate1_bwd{l}")
    dmix2 = dmix.reshape(t, d)
    dmerged = _mm(dmix2, k["w_out_t"], F32, f"mm_out_dx{l}").reshape(bsz, s, d)
    g["w_out"] = _mm(tr(sv["merged"]), dmix2, BF16, f"mm_out_dw{l}")
    p = sv["p"]
    dgate, doa, dob, doc, dya, dyb, dyc = _merge_bwd(
        dmerged, p, sv["oa"], sv["ob"], sv["oc"], k["w_br_a"], k["w_br_b"], k["w_br_c"],
        k["w_br_a_t"], k["w_br_b_t"], k["w_br_c_t"], f"merge_bwd{l}")
    g["w_br_a"] = _mm(tr(sv["oa"]), dya.reshape(t, d), BF16, f"mm_bra_dw{l}")
    g["w_br_b"] = _mm(tr(sv["ob"]), dyb.reshape(t, d), BF16, f"mm_brb_dw{l}")
    g["w_br_c"] = _mm(tr(sv["oc"]), dyc.reshape(t, d), BF16, f"mm_brc_dw{l}")
    dxbc_act, dsm_c, dsz, da_c, ddt_c, dd_c, dnw_c = _ssd_bwd(
        doc, sv["xbc_act"], p, k["ssd_a"], k["ssd_dt"], k["ssd_d"], k["ssd_nw"], sv["st_c"], f"ssd_bwd{l}")
    dxbc_raw, g["ssd_conv_w"], dscb = _conv_bwd(dxbc_act, p, P_XBC, 1024, k["ssd_conv_w"], k["ssd_conv_b"], f"ssd_conv_bwd{l}")
    g["ssd_conv_b"] = dscb[0]
    g["ssd_a_log"], g["ssd_dt_bias"] = da_c[0, SM_DT:SM_DT + 8], ddt_c[0, SM_DT:SM_DT + 8]
    g["ssd_d"] = dd_c.reshape(SSD_HEADS, SSD_HEAD_DIM).sum(axis=1)
    g["ssd_norm_w"] = dnw_c[0]
    dhg, dlb, dnw_b = _hgrn_bwd(dob, p, k["lb"], k["hgrn_nw"], sv["st_b"], f"hgrn_bwd{l}")
    g["hgrn_norm_w"] = dnw_b[0]
    dqkv_act, dsm_a, dgz, da_a, ddt_a, dnw_a = _gdn_bwd(
        doa, sv["qkv_act"], p, k["gdn_a"], k["gdn_dt"], k["gdn_nw"], sv["st_a"], sv["ti_a"], f"gdn_bwd{l}")
    dqkv_raw, g["gdn_conv_w"], _ = _conv_bwd(dqkv_act, p, P_QKV, 1536, k["gdn_conv_w"], k["gdn_conv_b"], f"gdn_conv_bwd{l}")
    g["gdn_a_log"], g["gdn_dt_bias"], g["gdn_norm_w"] = da_a[0, :4], ddt_a[0, :4], dnw_a[0]
    dsmall = jnp.pad((dsm_a + dsm_c).astype(BF16), ((0, 0), (0, 0), (0, P_WIDTH - P_SMALL - LANES)))
    dp = jnp.concatenate([dqkv_raw, dgz, dxbc_raw, dgate, dhg, dsz, dsmall], axis=-1).reshape(t, P_WIDTH)
    dh1 = _mm(dp, k["win_t"], F32, f"mm_in_dx{l}").reshape(bsz, s, d)
    g["w_in"] = _mm(tr(sv["h1"]), dp, BF16, f"mm_in_dw{l}")
    dx, dn1w, dsh1, dsc1 = _norm_mod_bwd(sv["x"], k["n1w"], sh1, sc1, dh1, dx1, f"norm1_bwd{l}")
    g["norm1_w"] = dn1w[0]
    dmod = jnp.concatenate([dsh1, dsc1, dg1, dsh2, dsc2, dg2], axis=-1)[:, 0]
    return dx, g, dlb, dmod


def _local_step(x, mod, wf, wr, target):
    lower = _lb_fwd(wr["hgrn_lb_param"])
    ks = [_layer_consts(l, wf, wr, lower) for l in range(DEPTH)]
    saved = []
    h = x
    for l in range(DEPTH):
        h, sv = _layer_fwd(l, h, mod[l], ks[l])
        saved.append(sv)
    loss8, dh, dfnw = _final_loss(h, wr["final_norm_w"][None], target)
    per_layer, dlbs, dmods = [None] * DEPTH, [None] * DEPTH, [None] * DEPTH
    for l in reversed(range(DEPTH)):
        dh, per_layer[l], dlbs[l], dmods[l] = _layer_bwd(l, dh, ks[l], saved[l])
    grads = {n: [per_layer[l][n] for l in range(DEPTH)] for n in per_layer[0]}
    grads = {n: g if n in UNALIGNED else jnp.stack(g) for n, g in grads.items()}
    grads["hgrn_lb_param"] = _lb_bwd(wr["hgrn_lb_param"], jnp.concatenate(dlbs, axis=0))
    grads["final_norm_w"] = dfnw[0]
    return loss8[0, 0], dh, grads, jnp.stack(dmods)


def kernel(x, c, w_ada, b_ada, norm1_w, w_in, gdn_conv_w, gdn_a_log, gdn_dt_bias, gdn_norm_w, hgrn_lb_param, hgrn_norm_w, ssd_conv_w, ssd_conv_b, ssd_a_log, ssd_dt_bias, ssd_d, ssd_norm_w, w_br_a, w_br_b, w_br_c, w_out, norm2_w, ffn_w_up, ffn_conv_w, ffn_conv_b, ffn_w_down, final_norm_w, loss_target, m_w_ada, m_b_ada, m_norm1_w, m_w_in, m_gdn_conv_w, m_gdn_a_log, m_gdn_dt_bias, m_gdn_norm_w, m_hgrn_lb_param, m_hgrn_norm_w, m_ssd_conv_w, m_ssd_conv_b, m_ssd_a_log, m_ssd_dt_bias, m_ssd_d, m_ssd_norm_w, m_w_br_a, m_w_br_b, m_w_br_c, m_w_out, m_norm2_w, m_ffn_w_up, m_ffn_conv_w, m_ffn_conv_b, m_ffn_w_down, m_final_norm_w, v_w_ada, v_b_ada, v_norm1_w, v_w_in, v_gdn_conv_w, v_gdn_a_log, v_gdn_dt_bias, v_gdn_norm_w, v_hgrn_lb_param, v_hgrn_norm_w, v_ssd_conv_w, v_ssd_conv_b, v_ssd_a_log, v_ssd_dt_bias, v_ssd_d, v_ssd_norm_w, v_w_br_a, v_w_br_b, v_w_br_c, v_w_out, v_norm2_w, v_ffn_w_up, v_ffn_conv_w, v_ffn_conv_b, v_ffn_w_down, v_final_norm_w):
    given = dict(locals())
    w = {n: given[n] for n in WEIGHTS}
    m = {n: given["m_" + n] for n in WEIGHTS}
    v = {n: given["v_" + n] for n in WEIGHTS}
    me = _my_index()
    bsz = c.shape[0]
    ncol = 6 * D_MODEL // N_DEV

    shards = [w[n].astype(BF16) if n in MATMUL_WEIGHTS else w[n] for n, _, _ in SPLIT] + [c]
    gathered = _gather_two_level(shards, "gather_weights")
    wf = {n: _join_cols(g, n == "w_in", f"join_{n}") if n in UNALIGNED else _join_blocks(g, shape, axis)
          for (n, shape, axis), g in zip(SPLIT, gathered)}
    c_all = gathered[-1].reshape(N_DEV * bsz, D_MODEL)

    b_cols = lax.dynamic_slice_in_dim(b_ada, me * ncol, ncol, axis=1)[:, None]
    mod_cols = _ada_fwd(c_all, w_ada, b_cols)
    send = mod_cols.reshape(DEPTH, N_DEV, bsz, ncol).transpose(1, 0, 2, 3)
    got = _all_to_all([send], "scatter_mod")[0]
    mod = got.transpose(1, 2, 0, 3).reshape(DEPTH, bsz, 6 * D_MODEL)

    loss, dx, grads, dmod = _local_step(x, mod, wf, w, loss_target)

    send = dmod.reshape(DEPTH, bsz, N_DEV, ncol).transpose(2, 0, 1, 3)
    got_dmod = _all_to_all([send], "scatter_dmod")[0]
    dmod_all = got_dmod.transpose(1, 0, 2, 3).reshape(DEPTH, N_DEV * bsz, ncol)
    g_w_ada, g_b_cols = _ada_bwd(c_all.T, dmod_all)

    core = lax.axis_index("c")
    by_core = []
    for n, shape, axis in SPLIT:
        if n in UNALIGNED:
            by_core.append(_split_cols(grads[n], n == "w_in", shape[axis] // N_DEV, f"split_{n}"))
            continue
        parts = _split_blocks(grads[n], shape, axis).astype(BF16)
        parts = parts.reshape((N_DEV // 2, 2, -1, parts.shape[-1]))
        by_core.append(jnp.swapaxes(parts, 0, 1))
    from_sibling = _sibling_swap(by_core, "swap_grads")
    sums = [_pair_sum(lax.dynamic_index_in_dim(mine, core, 0, keepdims=False), theirs, f"pair_sum_{n}")
            for (n, _, _), mine, theirs in zip(SPLIT, by_core, from_sibling)]
    got = _chip_scatter(sums, "scatter_grads")
    grads["b_ada"] = lax.dynamic_update_slice_in_dim(jnp.zeros_like(b_ada), g_b_cols[:, 0], me * ncol, axis=1)

    out = {}
    slots = [(n, g8) for (n, _, _), g8 in zip(SPLIT, got)] + [("w_ada", g_w_ada[None])]
    for n, gs in slots:
        out[n] = _sum_adamw(gs.reshape((gs.shape[0],) + w[n].shape), w[n], m[n], v[n], f"adamw_{n}")
    r8 = _gather_two_level([_pack_repl(grads)], "gather_small_grads")[0]
    res = _sum_adamw(r8[:, None], _pack_repl(w)[None], _pack_repl(m)[None], _pack_repl(v)[None], "adamw_repl")
    repl_out = [_unpack_repl(o[0]) for o in res]
    pick = lambda i, n: out[n][i] if n in out else repl_out[i][n]
    loss = lax.psum(loss, ("x", "y", "c"))
    return (loss, dx, *[pick(i, n) for i in range(4) for n in WEIGHTS])
```

```python
import functools
import math

import jax
import jax.numpy as jnp
from jax import lax
from jax.experimental import pallas as pl
from jax.experimental.pallas import tpu as pltpu

F32, BF16 = jnp.float32, jnp.bfloat16
HI = lax.Precision.HIGHEST
MESH_ID = pl.DeviceIdType.MESH

N_DEV = 8
EPS = 1e-6
D_MODEL = 1024
DEPTH = 2
GDN_HEADS, GDN_DK, GDN_CHUNK = 4, 128, 64
HGRN_HEADS, HGRN_DK, HGRN_CHUNK, HGRN_BLOCK = 4, 128, 16, 128
SSD_HEADS, SSD_HEAD_DIM, SSD_GROUPS, SSD_STATE, SSD_CHUNK = 8, 64, 2, 128, 64
SSD_INNER = SSD_HEADS * SSD_HEAD_DIM
FFN_HIDDEN = 2816
LANES = 128
P_QKV, P_GZ, P_XBC, P_GATE, P_HQ, P_HF, P_HI, P_HG, P_SZ, P_SMALL, P_WIDTH = (
    0, 1536, 2048, 3072, 6144, 6656, 7168, 7680, 8192, 8704, 9216)
SM_A, SM_B, SM_DT = 0, 4, 8
W_IN_SPLITS = (1536, 4, 4, 512, 512, 512, 512, 512, 512, 1024, 8, 3072)

ADAM_LR, ADAM_B1, ADAM_B2, ADAM_EPS, ADAM_WD, ADAM_STEP = 0.001, 0.9, 0.999, 1e-08, 0.01, 10

V7X_VMEM_LIMIT = 56 * 1024 * 1024
PACK_W = 1024


def _call(body, name, grid, in_specs, out_specs, out_shape, scratch=(), aliases=None):
    return pl.pallas_call(
        body, name=name, grid=grid, in_specs=in_specs, out_specs=out_specs, out_shape=out_shape,
        scratch_shapes=list(scratch), input_output_aliases=aliases or {},
        compiler_params=pltpu.CompilerParams(
            dimension_semantics=("arbitrary",) * len(grid), vmem_limit_bytes=V7X_VMEM_LIMIT),
    )


IN_PLACE = pl.BlockSpec(memory_space=pl.ANY)


def _pick(n, cands):
    for c in cands:
        if n % c == 0:
            return c
    raise ValueError(f"no tile for {n} among {cands}")


def _row_tile(s, cap):
    t = cap
    while s % t:
        t //= 2
    return t


def _sds(shape, dtype):
    return jax.ShapeDtypeStruct(shape, dtype)


def _dot(a, b):
    return lax.dot_general(a, b, (((1,), (0,)), ((), ())), precision=HI, preferred_element_type=F32)


NN, NT, TN = (((1,), (0,)), ((), ())), (((1,), (1,)), ((), ())), (((0,), (0,)), ((), ()))


def _mxu(a, b, dims):
    return lax.dot_general(a.astype(BF16), b.astype(BF16), dims, preferred_element_type=F32)


@jax.custom_vjp
def _bdot(a, b):
    return _mxu(a, b, NN)


@jax.custom_vjp
def _bdot_nt(a, b):
    return _mxu(a, b, NT)


@jax.custom_vjp
def _bdot_tn(a, b):
    return _mxu(a, b, TN)


_bdot.defvjp(lambda a, b: (_mxu(a, b, NN), (a, b)), lambda r, d: (_mxu(d, r[1], NT), _mxu(r[0], d, TN)))
_bdot_nt.defvjp(lambda a, b: (_mxu(a, b, NT), (a, b)), lambda r, d: (_mxu(d, r[1], NN), _mxu(d, r[0], TN)))
_bdot_tn.defvjp(lambda a, b: (_mxu(a, b, TN), (a, b)), lambda r, d: (_mxu(r[1], d, NT), _mxu(r[0], d, NN)))


def _split(x):
    hi = x.astype(BF16)
    return hi, (x - hi.astype(F32)).astype(BF16)


def _mxu3(a, b, dims):
    ah, al = _split(a)
    bh, bl = _split(b)
    return _mxu(ah, bh, dims) + (_mxu(ah, bl, dims) + _mxu(al, bh, dims))


@jax.custom_vjp
def _dot3(a, b):
    return _mxu3(a, b, NN)


_dot3.defvjp(lambda a, b: (_mxu3(a, b, NN), (a, b)), lambda r, d: (_mxu3(d, r[1], NT), _mxu3(r[0], d, TN)))


def _pieces(x):
    x1 = x.astype(BF16)
    r1 = x - x1.astype(F32)
    x2 = r1.astype(BF16)
    return x1, x2, (r1 - x2.astype(F32)).astype(BF16)


def _mask_mxu(mask, x, dims):
    x1, x2, x3 = _pieces(x)
    return _mxu(mask, x1, dims) + (_mxu(mask, x2, dims) + _mxu(mask, x3, dims))


def _spread_mxu(x, mask, dims):
    x1, x2, x3 = _pieces(x)
    return _mxu(x1, mask, dims) + (_mxu(x2, mask, dims) + _mxu(x3, mask, dims))


@jax.custom_vjp
def _mask_dot(mask, x):
    return _mask_mxu(mask, x, NN)


@jax.custom_vjp
def _spread_dot(x, mask):
    return _spread_mxu(x, mask, NN)


_mask_dot.defvjp(lambda m, x: (_mask_mxu(m, x, NN), m), lambda m, d: (jnp.zeros_like(m), _mask_mxu(m, d, TN)))
_spread_dot.defvjp(lambda x, m: (_spread_mxu(x, m, NN), m), lambda m, d: (_spread_mxu(d, m, NT), jnp.zeros_like(m)))


def _iota(shape, axis):
    return lax.broadcasted_iota(jnp.int32, shape, axis)


def _silu(x):
    return x * jax.nn.sigmoid(x)


def _softplus(x):
    return jnp.maximum(x, 0.0) + jnp.log1p(jnp.exp(-jnp.abs(x)))


def _rms(x, w):
    return x * lax.rsqrt(jnp.mean(x * x, axis=-1, keepdims=True) + EPS) * w


def _lane_col(x, lane):
    m = (_iota(x.shape, 1) == lane).astype(F32)
    return jnp.sum(x * m, axis=1, keepdims=True)


def _col_to_row(c):
    n = c.shape[0]
    eye = (_iota((n, n), 0) == _iota((n, n), 1)).astype(F32)
    return jnp.sum(c * eye, axis=0, keepdims=True)


def _tril(n, strict=False):
    r, c = _iota((n, n), 0), _iota((n, n), 1)
    return (r > c) if strict else (r >= c)


def _mm(a, b, out_dtype, name):
    m, k = a.shape
    n = b.shape[1]
    tm = _pick(m, (1024, 1408, 512, 256, 128, 64, 32, 16, 8))
    tn = _pick(n, (1024, 1408, 768, 512, 384, 256, 128))
    tk = k if k <= 3072 else _pick(k, (1024, 768, 512, 384, 256, 128))
    nk = k // tk

    def body_one(a_ref, b_ref, o_ref):
        o_ref[...] = _bdot(a_ref[...], b_ref[...]).astype(out_dtype)

    def body(a_ref, b_ref, o_ref, acc_ref):
        kk = pl.program_id(2)

        @pl.when(kk == 0)
        def _():
            acc_ref[...] = jnp.zeros_like(acc_ref)

        acc_ref[...] += _bdot(a_ref[...], b_ref[...])

        @pl.when(kk == nk - 1)
        def _():
            o_ref[...] = acc_ref[...].astype(out_dtype)

    return _call(
        body_one if nk == 1 else body, name, (m // tm, n // tn, nk),
        [pl.BlockSpec((tm, tk), lambda i, j, kk: (i, kk)), pl.BlockSpec((tk, tn), lambda i, j, kk: (kk, j))],
        pl.BlockSpec((tm, tn), lambda i, j, kk: (i, j)), _sds((m, n), out_dtype),
        scratch=[] if nk == 1 else [pltpu.VMEM((tm, tn), F32)],
    )(a, b)


def _ada_fwd(c_all, w, b):
    depth, _, n = w.shape
    rows = c_all.shape[0]

    def body(c_ref, w_ref, b_ref, o_ref):
        o_ref[...] = _dot(_silu(c_ref[...]), w_ref[...]) + b_ref[...]

    return _call(
        body, "ada_fwd", (depth,),
        [pl.BlockSpec((rows, D_MODEL), lambda l: (0, 0)), pl.BlockSpec((None, D_MODEL, n), lambda l: (l, 0, 0)),
         pl.BlockSpec((None, 1, n), lambda l: (l, 0, 0))],
        pl.BlockSpec((None, rows, n), lambda l: (l, 0, 0)), _sds((depth, rows, n), F32),
    )(c_all, w, b)


def _ada_bwd(c_all_t, dmod):
    depth, rows, n = dmod.shape

    def body(ct_ref, dm_ref, dw_ref, db_ref):
        dm = dm_ref[...]
        dw_ref[...] = _dot(_silu(ct_ref[...]), dm)
        db_ref[...] = jnp.sum(dm, axis=0, keepdims=True)

    return _call(
        body, "ada_bwd", (depth,),
        [pl.BlockSpec((D_MODEL, rows), lambda l: (0, 0)), pl.BlockSpec((None, rows, n), lambda l: (l, 0, 0))],
        [pl.BlockSpec((None, D_MODEL, n), lambda l: (l, 0, 0)), pl.BlockSpec((None, 1, n), lambda l: (l, 0, 0))],
        [_sds((depth, D_MODEL, n), F32), _sds((depth, 1, n), F32)],
    )(c_all_t, dmod)


def _lb_fn(p):
    rows = [p[l:l + 1] for l in range(DEPTH)]
    mx = functools.reduce(jnp.maximum, rows)
    es = [jnp.exp(r - mx) for r in rows]
    tot = functools.reduce(lambda a, b: a + b, es)
    sm = [e / tot for e in es]
    out, run = [], None
    for l in range(DEPTH):
        run = sm[l] if run is None else run + sm[l]
        out.append(run - sm[0])
    return jnp.concatenate(out, axis=0)


def _lb_fwd(p):
    def body(p_ref, o_ref):
        o_ref[...] = _lb_fn(p_ref[...])

    full = pl.BlockSpec(p.shape, lambda i: (0, 0))
    return _call(body, "lb_fwd", (1,), [full], full, _sds(p.shape, F32))(p)


def _lb_bwd(p, d_lower):
    def body(p_ref, d_ref, o_ref):
        _, vjp = jax.vjp(_lb_fn, p_ref[...])
        o_ref[...] = vjp(d_ref[...])[0]

    full = pl.BlockSpec(p.shape, lambda i: (0, 0))
    return _call(body, "lb_bwd", (1,), [full, full], full, _sds(p.shape, F32))(p, d_lower)


def _norm_mod_fn(x, w, shift, scale):
    return _rms(x, w) * (1.0 + scale) + shift


def _norm_mod_fwd(x, w, shift, scale, name):
    bsz, s, d = x.shape
    ts = _row_tile(s, 512)

    def body(x_ref, w_ref, sh_ref, sc_ref, o_ref):
        o_ref[...] = _norm_mod_fn(x_ref[...], w_ref[...], sh_ref[...], sc_ref[...]).astype(BF16)

    row = pl.BlockSpec((None, ts, d), lambda b, i: (b, i, 0))
    per_b = pl.BlockSpec((None, 1, d), lambda b, i: (b, 0, 0))
    return _call(body, name, (bsz, s // ts), [row, pl.BlockSpec((1, d), lambda b, i: (0, 0)), per_b, per_b],
                 row, _sds(x.shape, BF16))(x, w, shift, scale)


def _norm_mod_bwd(x, w, shift, scale, dh, carry, name):
    bsz, s, d = x.shape
    ts = _row_tile(s, 512)

    def body(x_ref, w_ref, sh_ref, sc_ref, dh_ref, c_ref, dx_ref, dw_ref, dsh_ref, dsc_ref):
        b, i = pl.program_id(0), pl.program_id(1)
        _, vjp = jax.vjp(_norm_mod_fn, x_ref[...], w_ref[...], sh_ref[...], sc_ref[...])
        dx, dw, dsh, dsc = vjp(dh_ref[...])
        dx_ref[...] = dx + c_ref[...]

        @pl.when((b == 0) & (i == 0))
        def _():
            dw_ref[...] = jnp.zeros_like(dw_ref)

        @pl.when(i == 0)
        def _():
            dsh_ref[...] = jnp.zeros_like(dsh_ref)
            dsc_ref[...] = jnp.zeros_like(dsc_ref)

        dw_ref[...] += dw
        dsh_ref[...] += dsh
        dsc_ref[...] += dsc

    row = pl.BlockSpec((None, ts, d), lambda b, i: (b, i, 0))
    per_b = pl.BlockSpec((None, 1, d), lambda b, i: (b, 0, 0))
    wspec = pl.BlockSpec((1, d), lambda b, i: (0, 0))
    return _call(body, name, (bsz, s // ts), [row, wspec, per_b, per_b, row, row],
                 [row, wspec, per_b, per_b],
                 [_sds(x.shape, F32), _sds((1, d), F32), _sds((bsz, 1, d), F32), _sds((bsz, 1, d), F32)],
                 )(x, w, shift, scale, dh, carry)


def _resid_fwd(x, y, gate, name):
    bsz, s, d = x.shape
    ts = _row_tile(s, 1024)

    def body(x_ref, y_ref, g_ref, o_ref):
        o_ref[...] = x_ref[...] + g_ref[...] * y_ref[...]

    row = pl.BlockSpec((None, ts, d), lambda b, i: (b, i, 0))
    per_b = pl.BlockSpec((None, 1, d), lambda b, i: (b, 0, 0))
    return _call(body, name, (bsz, s // ts), [row, row, per_b], row, _sds(x.shape, F32))(x, y, gate)


def _gate_bwd(dx, y, gate, name):
    bsz, s, d = dx.shape
    ts = _row_tile(s, 1024)

    def body(dx_ref, y_ref, g_ref, dy_ref, dg_ref):
        dxv = dx_ref[...]
        dy_ref[...] = (dxv * g_ref[...]).astype(BF16)

        @pl.when(pl.program_id(1) == 0)
        def _():
            dg_ref[...] = jnp.zeros_like(dg_ref)

        dg_ref[...] += jnp.sum(dxv * y_ref[...], axis=0, keepdims=True)

    row = pl.BlockSpec((None, ts, d), lambda b, i: (b, i, 0))
    per_b = pl.BlockSpec((None, 1, d), lambda b, i: (b, 0, 0))
    return _call(body, name, (bsz, s // ts), [row, row, per_b], [row, per_b],
                 [_sds(dx.shape, BF16), _sds((bsz, 1, d), F32)])(dx, y, gate)


HALO = 8


def _conv_pre(xx, w_ref, b_ref, kw, rows):
    acc = w_ref[kw - 1:kw, :] * xx[HALO:HALO + rows]
    for k in range(kw - 1):
        acc = acc + w_ref[k:k + 1, :] * pltpu.roll(xx, kw - 1 - k, 0)[HALO:HALO + rows]
    return acc + b_ref[...]


def _conv_fwd(x, col0, width, w, b, name):
    bsz, s, _ = x.shape
    kw = w.shape[0]
    ts = _row_tile(s, 1024)
    tc = _pick(width, (512, 256, 128))
    assert col0 % tc == 0
    c0 = col0 // tc
    hb = ts // HALO

    def body(x_ref, xp_ref, w_ref, b_ref, o_ref):
        i = pl.program_id(1)
        xp = jnp.where(i > 0, xp_ref[...], 0.0)
        xx = jnp.concatenate([xp, x_ref[...]], axis=0)
        pre = _conv_pre(xx, w_ref, b_ref, kw, ts)
        o_ref[...] = _silu(pre)

    return _call(
        body, name, (bsz, s // ts, width // tc),
        [pl.BlockSpec((None, ts, tc), lambda bb, i, j: (bb, i, c0 + j)),
         pl.BlockSpec((None, HALO, tc), lambda bb, i, j: (bb, jnp.maximum(i * hb - 1, 0), c0 + j)),
         pl.BlockSpec((kw, tc), lambda bb, i, j: (0, j)), pl.BlockSpec((1, tc), lambda bb, i, j: (0, j))],
        pl.BlockSpec((None, ts, tc), lambda bb, i, j: (bb, i, j)), _sds((bsz, s, width), F32),
    )(x, x, w, b)


def _conv_bwd(dy, x, col0, width, w, b, dp, name):
    bsz, s, _ = x.shape
    kw = w.shape[0]
    ts = _row_tile(s, 1024)
    tc = _pick(width, (512, 256, 128))
    c0 = col0 // tc
    hb = ts // HALO
    nt = s // ts
    last_h = s // HALO - 1

    def body(x_ref, xp_ref, xn_ref, dy_ref, dyn_ref, w_ref, b_ref, _, dx_ref, dw_ref, db_ref):
        bb, i = pl.program_id(1), pl.program_id(2)
        xp = jnp.where(i > 0, xp_ref[...], 0.0)
        xx = jnp.concatenate([xp, x_ref[...], xn_ref[...]], axis=0)
        dyy = jnp.concatenate([dy_ref[...], jnp.where(i < nt - 1, dyn_ref[...], 0.0)], axis=0)
        n = ts + HALO
        pre = _conv_pre(xx, w_ref, b_ref, kw, n)
        sg = jax.nn.sigmoid(pre)
        dpre = dyy * (sg * (1.0 + pre * (1.0 - sg)))
        dx = w_ref[kw - 1:kw, :] * dpre[:ts]
        for k in range(kw - 1):
            dx = dx + w_ref[k:k + 1, :] * pltpu.roll(dpre, n - (kw - 1 - k), 0)[:ts]
        dx_ref[...] = dx.astype(BF16)

        @pl.when((bb == 0) & (i == 0))
        def _():
            dw_ref[...] = jnp.zeros_like(dw_ref)
            db_ref[...] = jnp.zeros_like(db_ref)

        dt = dpre[:ts]
        db_ref[...] += jnp.sum(dt, axis=0, keepdims=True)
        dw_ref[kw - 1:kw, :] += jnp.sum(dt * xx[HALO:HALO + ts], axis=0, keepdims=True)
        for k in range(kw - 1):
            xs = pltpu.roll(xx, kw - 1 - k, 0)[HALO:HALO + ts]
            dw_ref[k:k + 1, :] += jnp.sum(dt * xs, axis=0, keepdims=True)

    xspec = lambda f: pl.BlockSpec((None, HALO, tc), f)
    return _call(
        body, name, (width // tc, bsz, nt),
        [pl.BlockSpec((None, ts, tc), lambda j, bb, i: (bb, i, c0 + j)),
         xspec(lambda j, bb, i: (bb, jnp.maximum(i * hb - 1, 0), c0 + j)),
         xspec(lambda j, bb, i: (bb, jnp.minimum((i + 1) * hb, last_h), c0 + j)),
         pl.BlockSpec((None, ts, tc), lambda j, bb, i: (bb, i, j)),
         xspec(lambda j, bb, i: (bb, jnp.minimum((i + 1) * hb, last_h), j)),
         pl.BlockSpec((kw, tc), lambda j, bb, i: (0, j)), pl.BlockSpec((1, tc), lambda j, bb, i: (0, j)), IN_PLACE],
        [pl.BlockSpec((None, ts, tc), lambda j, bb, i: (bb, i, c0 + j)),
         pl.BlockSpec((kw, tc), lambda j, bb, i: (0, j)), pl.BlockSpec((1, tc), lambda j, bb, i: (0, j))],
        [_sds(dp.shape, BF16), _sds((kw, width), F32), _sds((1, width), F32)], aliases={7: 0},
    )(x, x, x, dy, dy, w, b, dp)


def _conv_glu_fwd(x, w, b, name):
    bsz, s, f2 = x.shape
    f = f2 // 2
    kw = w.shape[0]
    ts = _row_tile(s, 2048)
    tc = _pick(f, (256, 128))
    nf = f // tc
    hb = ts // HALO

    def body(xg_ref, xgp_ref, xv_ref, xvp_ref, wg_ref, wv_ref, bg_ref, bv_ref, o_ref):
        i = pl.program_id(1)
        halves = []
        for x_ref, xp_ref, w_ref, b_ref in ((xg_ref, xgp_ref, wg_ref, bg_ref), (xv_ref, xvp_ref, wv_ref, bv_ref)):
            xx = jnp.concatenate([jnp.where(i > 0, xp_ref[...], 0.0), x_ref[...]], axis=0)
            halves.append(_conv_pre(xx, w_ref, b_ref, kw, ts))
        o_ref[...] = (_silu(halves[0]) * halves[1]).astype(BF16)

    tile = lambda off: pl.BlockSpec((None, ts, tc), lambda bb, i, j: (bb, i, off + j))
    prev = lambda off: pl.BlockSpec((None, HALO, tc), lambda bb, i, j: (bb, jnp.maximum(i * hb - 1, 0), off + j))
    wsp = lambda rows, off: pl.BlockSpec((rows, tc), lambda bb, i, j: (0, off + j))
    return _call(
        body, name, (bsz, s // ts, nf),
        [tile(0), prev(0), tile(nf), prev(nf), wsp(kw, 0), wsp(kw, nf), wsp(1, 0), wsp(1, nf)],
        pl.BlockSpec((None, ts, tc), lambda bb, i, j: (bb, i, j)), _sds((bsz, s, f), BF16),
    )(x, x, x, x, w, w, b, b)


def _conv_glu_bwd(da, x, w, b, name):
    bsz, s, f2 = x.shape
    f = f2 // 2
    kw = w.shape[0]
    ts = _row_tile(s, 2048)
    tc = _pick(f, (256, 128))
    nf = f // tc
    hb = ts // HALO
    nt = s // ts
    last_h = s // HALO - 1
    n = ts + HALO

    def body(xg_ref, xgp_ref, xgn_ref, xv_ref, xvp_ref, xvn_ref, da_ref, dan_ref,
             wg_ref, wv_ref, bg_ref, bv_ref, wx_ref, dx_ref, dw_ref, db_ref):
        j, bb, i = pl.program_id(0), pl.program_id(1), pl.program_id(2)
        day = jnp.concatenate([da_ref[...], jnp.where(i < nt - 1, dan_ref[...], 0.0)], axis=0)
        xg = jnp.concatenate([jnp.where(i > 0, xgp_ref[...], 0.0), xg_ref[...], xgn_ref[...]], axis=0)
        pre_g = _conv_pre(xg, wg_ref, bg_ref, kw, n)
        sg = jax.nn.sigmoid(pre_g)

        @pl.when((bb == 0) & (i == 0))
        def _():
            dw_ref[...] = jnp.zeros_like(dw_ref)
            db_ref[...] = jnp.zeros_like(db_ref)

        def finish(dpre, xx):
            dx = wx_ref[kw - 1:kw, :] * dpre[:ts]
            for k in range(kw - 1):
                dx = dx + wx_ref[k:k + 1, :] * pltpu.roll(dpre, n - (kw - 1 - k), 0)[:ts]
            dx_ref[...] = dx.astype(BF16)
            dt = dpre[:ts]
            db_ref[...] += jnp.sum(dt, axis=0, keepdims=True)
            dw_ref[kw - 1:kw, :] += jnp.sum(dt * xx[HALO:HALO + ts], axis=0, keepdims=True)
            for k in range(kw - 1):
                dw_ref[k:k + 1, :] += jnp.sum(dt * pltpu.roll(xx, kw - 1 - k, 0)[HALO:HALO + ts], axis=0, keepdims=True)

        @pl.when(j < nf)
        def _():
            xv = jnp.concatenate([jnp.where(i > 0, xvp_ref[...], 0.0), xv_ref[...], xvn_ref[...]], axis=0)
            pre_v = _conv_pre(xv, wv_ref, bv_ref, kw, n)
            finish(day * pre_v * (sg * (1.0 + pre_g * (1.0 - sg))), xg)

        @pl.when(j >= nf)
        def _():
            xv = jnp.concatenate([jnp.where(i > 0, xvp_ref[...], 0.0), xv_ref[...], xvn_ref[...]], axis=0)
            finish(day * (pre_g * sg), xv)

    tile = lambda off: pl.BlockSpec((None, ts, tc), lambda j, bb, i: (bb, i, off + j % nf))
    prev = lambda off: pl.BlockSpec((None, HALO, tc), lambda j, bb, i: (bb, jnp.maximum(i * hb - 1, 0), off + j % nf))
    nxt = lambda off: pl.BlockSpec((None, HALO, tc), lambda j, bb, i: (bb, jnp.minimum((i + 1) * hb, last_h), off + j % nf))
    wsp = lambda rows, off: pl.BlockSpec((rows, tc), lambda j, bb, i: (0, off + j % nf))
    own = lambda rows: pl.BlockSpec((rows, tc), lambda j, bb, i: (0, j))
    return _call(
        body, name, (2 * nf, bsz, nt),
        [tile(0), prev(0), nxt(0), tile(nf), prev(nf), nxt(nf), tile(0), nxt(0),
         wsp(kw, 0), wsp(kw, nf), wsp(1, 0), wsp(1, nf), own(kw)],
        [pl.BlockSpec((None, ts, tc), lambda j, bb, i: (bb, i, j)), own(kw), own(1)],
        [_sds((bsz, s, f2), BF16), _sds((kw, f2), F32), _sds((1, f2), F32)],
    )(x, x, x, x, x, x, da, da, w, w, b, b, w)


def _merge_fwd(p, oa, ob, oc, wa, wb, wc, name):
    bsz, s, _ = p.shape
    tm = _row_tile(s, 512)
    gblk = P_GATE // (3 * D_MODEL)

    def body(g_ref, oa_ref, ob_ref, oc_ref, wa_ref, wb_ref, wc_ref, o_ref):
        acc = None
        for i, (o_r, w_r) in enumerate(((oa_ref, wa_ref), (ob_ref, wb_ref), (oc_ref, wc_ref))):
            y = _bdot(o_r[...], w_r[...])
            t = jax.nn.sigmoid(g_ref[:, i * D_MODEL:(i + 1) * D_MODEL]) * y
            acc = t if acc is None else acc + t
        o_ref[...] = acc.astype(BF16)

    orow = pl.BlockSpec((None, tm, 512), lambda b, i: (b, i, 0))
    wfull = pl.BlockSpec((512, D_MODEL), lambda b, i: (0, 0))
    return _call(
        body, name, (bsz, s // tm),
        [pl.BlockSpec((None, tm, 3 * D_MODEL), lambda b, i: (b, i, gblk)), orow, orow, orow, wfull, wfull, wfull],
        pl.BlockSpec((None, tm, D_MODEL), lambda b, i: (b, i, 0)), _sds((bsz, s, D_MODEL), BF16),
    )(p, oa, ob, oc, wa, wb, wc)


def _merge_bwd(dm, p, oa, ob, oc, wa, wb, wc, wat, wbt, wct, name):
    bsz, s, _ = p.shape
    tm = _row_tile(s, 512)
    gblk = P_GATE // (3 * D_MODEL)

    def body(dm_ref, g_ref, oa_ref, ob_ref, oc_ref, wa_ref, wb_ref, wc_ref, wat_ref, wbt_ref, wct_ref,
             dg_ref, doa_ref, dob_ref, doc_ref, dya_ref, dyb_ref, dyc_ref):
        dmv = dm_ref[...]
        trip = ((oa_ref, wa_ref, wat_ref, doa_ref, dya_ref), (ob_ref, wb_ref, wbt_ref, dob_ref, dyb_ref),
                (oc_ref, wc_ref, wct_ref, doc_ref, dyc_ref))
        for i, (o_r, w_r, wt_r, do_r, dy_r) in enumerate(trip):
            y = _bdot(o_r[...], w_r[...])
            sg = jax.nn.sigmoid(g_ref[:, i * D_MODEL:(i + 1) * D_MODEL])
            dg_ref[:, i * D_MODEL:(i + 1) * D_MODEL] = (dmv * y * sg * (1.0 - sg)).astype(BF16)
            dy = (dmv * sg).astype(BF16)
            dy_r[...] = dy
            do_r[...] = _bdot(dy, wt_r[...])

    orow = pl.BlockSpec((None, tm, 512), lambda b, i: (b, i, 0))
    drow = pl.BlockSpec((None, tm, D_MODEL), lambda b, i: (b, i, 0))
    grow = pl.BlockSpec((None, tm, 3 * D_MODEL), lambda b, i: (b, i, gblk))
    wfull = pl.BlockSpec((512, D_MODEL), lambda b, i: (0, 0))
    wtfull = pl.BlockSpec((D_MODEL, 512), lambda b, i: (0, 0))
    return _call(
        body, name, (bsz, s // tm),
        [drow, grow, orow, orow, orow, wfull, wfull, wfull, wtfull, wtfull, wtfull],
        [grow, orow, orow, orow, drow, drow, drow],
        [_sds(p.shape, BF16)] + [_sds((bsz, s, 512), F32)] * 3 + [_sds((bsz, s, D_MODEL), BF16)] * 3,
    )(dm, p, oa, ob, oc, wa, wb, wc, wat, wbt, wct)


def _final_loss(x, w, target):
    bsz, s, d = x.shape
    ts = _row_tile(s, 512)

    def body(x_ref, w_ref, t_ref, loss_ref, dx_ref, dw_ref):
        first = (pl.program_id(0) == 0) & (pl.program_id(1) == 0)
        y, vjp = jax.vjp(_rms, x_ref[...], w_ref[...])
        err = y - t_ref[...]
        dx, dw = vjp(err * (1.0 / d))
        dx_ref[...] = dx

        @pl.when(first)
        def _():
            loss_ref[...] = jnp.zeros_like(loss_ref)
            dw_ref[...] = jnp.zeros_like(dw_ref)

        loss_ref[...] += 0.5 * jnp.sum(jnp.sum(err * err, axis=1, keepdims=True), axis=0, keepdims=True) * (1.0 / d)
        dw_ref[...] += dw

    row = pl.BlockSpec((None, ts, d), lambda b, i: (b, i, 0))
    wspec = pl.BlockSpec((1, d), lambda b, i: (0, 0))
    return _call(body, "final_loss", (bsz, s // ts), [row, wspec, row],
                 [pl.BlockSpec((8, LANES), lambda b, i: (0, 0)), row, wspec],
                 [_sds((8, LANES), F32), _sds(x.shape, F32), _sds((1, d), F32)])(x, w, target)


def _unit_lower_inverses(ms):
    n = ms[0].shape[0]
    eye = (_iota((n, n), 0) == _iota((n, n), 1)).astype(F32)
    ps = [-m for m in ms]
    xs = [eye + p for p in ps]
    for _ in range(int(math.log2(n)) - 1):
        ps = [_mxu3(p, p, NN) for p in ps]
        xs = [x + _mxu3(x, p, NN) for x, p in zip(xs, ps)]
    return xs


@jax.custom_vjp
def _known_inverse(m, t):
    return t


_known_inverse.defvjp(lambda m, t: (t, t), lambda t, dt: (-_mxu3(t, _mxu3(dt, t, NT), TN), jnp.zeros_like(t)))


def _gdn_chunk(states, qkv, small, z, a_row, dt_row, nw, tinvs=None):
    nb = len(qkv)
    c = qkv[0].shape[0]
    kw = GDN_HEADS * GDN_DK
    incl, strict = _tril(c), _tril(c, True)
    g_all = [-jnp.exp(a_row) * _softplus(small[b] + dt_row) for b in range(nb)]
    beta_all = [jax.nn.sigmoid(small[b]) for b in range(nb)]
    big_g_all = [_mask_dot(incl.astype(BF16), g_all[b]) for b in range(nb)]
    items = [(b, h) for b in range(nb) for h in range(GDN_HEADS)]
    ids = range(len(items))
    col = lambda b, part, h: qkv[b][:, part * kw + h * GDN_DK:part * kw + (h + 1) * GDN_DK]
    unit = lambda t: t * lax.rsqrt(jnp.sum(t * t, axis=-1, keepdims=True) + EPS)
    q = [unit(col(b, 0, h)) * (GDN_DK ** -0.5) for b, h in items]
    k = [unit(col(b, 1, h)) for b, h in items]
    v = [col(b, 2, h) for b, h in items]
    gc = [_lane_col(big_g_all[b], SM_A + h) for b, h in items]
    bc = [_lane_col(beta_all[b], SM_B + h) for b, h in items]
    g_last = [jnp.sum(_lane_col(g_all[b], SM_A + h), axis=0, keepdims=True) for b, h in items]
    decay = [jnp.where(incl, jnp.exp(jnp.where(incl, gc[i] - _col_to_row(gc[i]), 0.0)), 0.0) for i in ids]
    kb = [k[i] * bc[i] for i in ids]
    m = [jnp.where(strict, _bdot_nt(kb[i], k[i]) * decay[i], 0.0) for i in ids]
    if tinvs is None:
        tinv = _unit_lower_inverses(m)
    else:
        tinv = [_known_inverse(m[i], tinvs[i]) for i in ids]
    eg = [jnp.exp(gc[i]) for i in ids]
    u = [_dot3(tinv[i], v[i] * bc[i]) for i in ids]
    w = [_dot3(tinv[i], kb[i] * eg[i]) for i in ids]
    attn = [_bdot_nt(q[i], k[i]) * decay[i] for i in ids]
    v_new = [u[i] - _bdot(w[i], states[i]) for i in ids]
    o_st = [_bdot(q[i] * eg[i], states[i]) for i in ids]
    o = [o_st[i] + _bdot(attn[i], v_new[i]) for i in ids]
    grow = [_bdot_tn(k[i] * jnp.exp(g_last[i] - gc[i]), v_new[i]) for i in ids]
    new_states = [states[i] * jnp.exp(g_last[i]) + grow[i] for i in ids]
    outs = [_rms(o[i], nw) * _silu(z[b][:, h * GDN_DK:(h + 1) * GDN_DK]) for i, (b, h) in enumerate(items)]
    per_seq = [jnp.concatenate(outs[b * GDN_HEADS:(b + 1) * GDN_HEADS], axis=1) for b in range(nb)]
    return new_states, per_seq, tinv


def _seq_items(bsz, heads):
    return [(b, h) for b in range(bsz) for h in range(heads)]


def _gdn_fwd(qkv_act, p, a_row, dt_row, nw, name):
    bsz, s, _ = qkv_act.shape
    c = GDN_CHUNK
    nc = s // c
    items = _seq_items(bsz, GDN_HEADS)

    def body(qkv_ref, sm_ref, z_ref, a_ref, dt_ref, nw_ref, o_ref, st_ref, ti_ref, st_scr):
        @pl.when(pl.program_id(0) == 0)
        def _():
            st_scr[...] = jnp.zeros_like(st_scr)

        st_ref[...] = st_scr[...]
        seqs = range(bsz)
        new_states, o, tinvs = _gdn_chunk(
            [st_scr[b, h] for b, h in items], [qkv_ref[b] for b in seqs], [sm_ref[b] for b in seqs],
            [z_ref[b] for b in seqs], a_ref[...], dt_ref[...], nw_ref[...])
        for i, (b, h) in enumerate(items):
            st_scr[b, h] = new_states[i]
            ti_ref[b, h] = tinvs[i]
        for b in seqs:
            o_ref[b] = o[b].astype(BF16)

    row = lambda w, blk: pl.BlockSpec((bsz, c, w), lambda n, blk=blk: (0, n, blk))
    prm = pl.BlockSpec((1, LANES), lambda n: (0, 0))
    return _call(
        body, name, (nc,), [row(1536, 0), row(LANES, P_SMALL // LANES), row(512, P_GZ // 512), prm, prm, prm],
        [row(512, 0), pl.BlockSpec((bsz, None, 4, LANES, LANES), lambda n: (0, n, 0, 0, 0)),
         pl.BlockSpec((bsz, None, 4, c, c), lambda n: (0, n, 0, 0, 0))],
        [_sds((bsz, s, 512), BF16), _sds((bsz, nc, 4, LANES, LANES), F32), _sds((bsz, nc, 4, c, c), F32)],
        scratch=[pltpu.VMEM((bsz, 4, LANES, LANES), F32)],
    )(qkv_act, p, p, a_row, dt_row, nw)


def _gdn_bwd(do, qkv_act, p, a_row, dt_row, nw, st_all, ti_all, dp, name):
    bsz, s, _ = qkv_act.shape
    c = GDN_CHUNK
    nc = s // c

    items = _seq_items(bsz, GDN_HEADS)

    def body(qkv_ref, sm_ref, z_ref, a_ref, dt_ref, nw_ref, do_ref, st_ref, ti_ref, _,
             dqkv_ref, dsm_ref, dz_ref, da_ref, ddt_ref, dnw_ref, ds_scr):
        @pl.when(pl.program_id(0) == 0)
        def _():
            ds_scr[...] = jnp.zeros_like(ds_scr)
            da_ref[...] = jnp.zeros_like(da_ref)
            ddt_ref[...] = jnp.zeros_like(ddt_ref)
            dnw_ref[...] = jnp.zeros_like(dnw_ref)

        seqs = range(bsz)
        tinvs = [ti_ref[b, h] for b, h in items]
        chunk = lambda *a: _gdn_chunk(*a, tinvs=tinvs)[:2]
        _, vjp = jax.vjp(chunk, [st_ref[b, h] for b, h in items], [qkv_ref[b] for b in seqs], [sm_ref[b] for b in seqs],
                         [z_ref[b] for b in seqs], a_ref[...], dt_ref[...], nw_ref[...])
        d_states, dqkv, dsm, dz, da, ddt, dnw = vjp(([ds_scr[b, h] for b, h in items], [do_ref[b] for b in seqs]))
        for i, (b, h) in enumerate(items):
            ds_scr[b, h] = d_states[i]
        for b in seqs:
            dqkv_ref[b] = dqkv[b]
            dsm_ref[b] = dsm[b]
            dz_ref[b] = dz[b].astype(BF16)
        da_ref[...] += da
        ddt_ref[...] += ddt
        dnw_ref[...] += dnw

    rrow = lambda w, blk: pl.BlockSpec((bsz, c, w), lambda n, blk=blk: (0, nc - 1 - n, blk))
    prm = pl.BlockSpec((1, LANES), lambda n: (0, 0))
    return _call(
        body, name, (nc,),
        [rrow(1536, 0), rrow(LANES, P_SMALL // LANES), rrow(512, P_GZ // 512), prm, prm, prm, rrow(512, 0),
         pl.BlockSpec((bsz, None, 4, LANES, LANES), lambda n: (0, nc - 1 - n, 0, 0, 0)),
         pl.BlockSpec((bsz, None, 4, c, c), lambda n: (0, nc - 1 - n, 0, 0, 0)), IN_PLACE],
        [rrow(1536, 0), rrow(LANES, 0), rrow(512, P_GZ // 512), prm, prm, prm],
        [_sds((bsz, s, 1536), F32), _sds((bsz, s, LANES), F32), _sds(dp.shape, BF16)] + [_sds((1, LANES), F32)] * 3,
        scratch=[pltpu.VMEM((bsz, 4, LANES, LANES), F32)], aliases={9: 2},
    )(qkv_act, p, p, a_row, dt_row, nw, do, st_all, ti_all, dp)


def _hgrn_block(states, q_raw, f_raw, i_raw, g_raw, lb, nw):
    n = q_raw[0].shape[0]
    c = HGRN_CHUNK
    r, cc = _iota((n, n), 0), _iota((n, n), 1)
    same = (r // c) == (cc // c)
    causal = same & (r >= cc)
    ref_row = (r // c) * c + (c // 2 - 1)
    run_sum = causal.astype(F32)
    rel_sum = run_sum - (same & (ref_row >= cc)).astype(F32)
    sums = jnp.concatenate([run_sum, rel_sum, same.astype(F32)], axis=0).astype(BF16)
    seqs, chunks = range(len(q_raw)), range(n // c)
    items = _seq_items(len(q_raw), HGRN_HEADS)
    hs = lambda t, h: t[:, h * HGRN_DK:(h + 1) * HGRN_DK]
    rows = lambda t, j: t[j * c:(j + 1) * c]
    q = [_silu(q_raw[b]) for b in seqs]
    logf = [jnp.log(lb + (1.0 - lb) * jax.nn.sigmoid(f_raw[b])) for b in seqs]
    k = [(1.0 - lb) * jax.nn.sigmoid(-f_raw[b]) for b in seqs]
    all_sums = [_mask_dot(sums, logf[b]) for b in seqs]
    big_g, g_rel, g_tot = ([t[i * n:(i + 1) * n] for t in all_sums] for i in range(3))
    q_rel = [q[b] * jnp.exp(g_rel[b]) for b in seqs]
    k_rel = [k[b] * jnp.exp(-g_rel[b]) for b in seqs]
    qg = [q[b] * jnp.exp(big_g[b]) for b in seqs]
    k_end = [k[b] * jnp.exp(g_tot[b] - big_g[b]) for b in seqs]
    keep = [[jnp.exp(g_tot[b][j * c:j * c + 1]) for j in chunks] for b in seqs]
    scores = [_bdot_nt(hs(q_rel[b], h), hs(k_rel[b], h)) for b, h in items]
    o_intra = [_bdot(jnp.where(causal, scores[i], 0.0), hs(i_raw[b], h)) for i, (b, h) in enumerate(items)]
    grow = [[_bdot_tn(rows(hs(i_raw[b], h), j), rows(hs(k_end[b], h), j)) for j in chunks] for b, h in items]
    entering, new_states = [], []
    for i, (b, h) in enumerate(items):
        st, per_chunk = states[i], []
        for j in chunks:
            per_chunk.append(st)
            st = st * hs(keep[b][j], h) + grow[i][j]
        entering.append(per_chunk)
        new_states.append(st)
    o_inter = [[_bdot_nt(rows(hs(qg[b], h), j), entering[i][j]) for j in chunks] for i, (b, h) in enumerate(items)]
    outs = [_rms(o_intra[i] + jnp.concatenate(o_inter[i], axis=0), nw) * _silu(hs(g_raw[b], h))
            for i, (b, h) in enumerate(items)]
    return new_states, [jnp.concatenate(outs[b * HGRN_HEADS:(b + 1) * HGRN_HEADS], axis=1) for b in seqs]


def _hgrn_fwd(p, lb, nw, name):
    bsz, s, _ = p.shape
    n = HGRN_BLOCK
    nb = s // n

    items = _seq_items(bsz, HGRN_HEADS)

    def body(q_ref, f_ref, i_ref, g_ref, lb_ref, nw_ref, o_ref, st_ref, st_scr):
        @pl.when(pl.program_id(0) == 0)
        def _():
            st_scr[...] = jnp.zeros_like(st_scr)

        st_ref[...] = st_scr[...]
        per_seq = lambda ref: [ref[b] for b in range(bsz)]
        new_states, o = _hgrn_block([st_scr[b, h] for b, h in items], per_seq(q_ref), per_seq(f_ref), per_seq(i_ref),
                                    per_seq(g_ref), lb_ref[...], nw_ref[...])
        for i, (b, h) in enumerate(items):
            st_scr[b, h] = new_states[i]
        for b in range(bsz):
            o_ref[b] = o[b].astype(BF16)

    row = lambda blk: pl.BlockSpec((bsz, n, 512), lambda i, blk=blk: (0, i, blk))
    return _call(
        body, name, (nb,),
        [row(P_HQ // 512), row(P_HF // 512), row(P_HI // 512), row(P_HG // 512),
         pl.BlockSpec((1, 512), lambda i: (0, 0)), pl.BlockSpec((1, LANES), lambda i: (0, 0))],
        [row(0), pl.BlockSpec((bsz, None, 4, LANES, LANES), lambda i: (0, i, 0, 0, 0))],
        [_sds((bsz, s, 512), BF16), _sds((bsz, nb, 4, LANES, LANES), F32)],
        scratch=[pltpu.VMEM((bsz, 4, LANES, LANES), F32)],
    )(p, p, p, p, lb, nw)


def _hgrn_bwd(do, p, lb, nw, st_all, dp, name):
    bsz, s, _ = p.shape
    n = HGRN_BLOCK
    nb = s // n

    items = _seq_items(bsz, HGRN_HEADS)

    def body(q_ref, f_ref, i_ref, g_ref, lb_ref, nw_ref, do_ref, st_ref, _, dp_ref, dlb_ref, dnw_ref, ds_scr):
        @pl.when(pl.program_id(0) == 0)
        def _():
            ds_scr[...] = jnp.zeros_like(ds_scr)
            dlb_ref[...] = jnp.zeros_like(dlb_ref)
            dnw_ref[...] = jnp.zeros_like(dnw_ref)

        per_seq = lambda ref: [ref[b] for b in range(bsz)]
        _, vjp = jax.vjp(_hgrn_block, [st_ref[b, h] for b, h in items], per_seq(q_ref), per_seq(f_ref), per_seq(i_ref),
                         per_seq(g_ref), lb_ref[...], nw_ref[...])
        d_states, dq, df, di, dg, dlb, dnw = vjp(([ds_scr[b, h] for b, h in items], per_seq(do_ref)))
        for i, (b, h) in enumerate(items):
            ds_scr[b, h] = d_states[i]
        for b in range(bsz):
            for j, t in enumerate((dq, df, di, dg)):
                dp_ref[b, :, j * 512:(j + 1) * 512] = t[b].astype(BF16)
        dlb_ref[...] += dlb
        dnw_ref[...] += dnw

    row = lambda blk: pl.BlockSpec((bsz, n, 512), lambda i, blk=blk: (0, nb - 1 - i, blk))
    return _call(
        body, name, (nb,),
        [row(P_HQ // 512), row(P_HF // 512), row(P_HI // 512), row(P_HG // 512),
         pl.BlockSpec((1, 512), lambda i: (0, 0)), pl.BlockSpec((1, LANES), lambda i: (0, 0)), row(0),
         pl.BlockSpec((bsz, None, 4, LANES, LANES), lambda i: (0, nb - 1 - i, 0, 0, 0)), IN_PLACE],
        [pl.BlockSpec((bsz, n, 2048), lambda i: (0, nb - 1 - i, P_HQ // 2048)),
         pl.BlockSpec((1, 512), lambda i: (0, 0)), pl.BlockSpec((1, LANES), lambda i: (0, 0))],
        [_sds(dp.shape, BF16), _sds((1, 512), F32), _sds((1, LANES), F32)],
        scratch=[pltpu.VMEM((bsz, 4, LANES, LANES), F32)], aliases={8: 0},
    )(p, p, p, p, lb, nw, do, st_all, dp)


def _ssd_chunk(states, xbc, small, z, a_row, dt_row, d_row, nw):
    seqs = range(len(xbc))
    c = xbc[0].shape[0]
    incl = _tril(c)
    spread = (_iota((LANES, SSD_INNER), 0) == SM_DT + _iota((LANES, SSD_INNER), 1) // SSD_HEAD_DIM).astype(BF16)
    dt_all = [_softplus(small[b] + dt_row) for b in seqs]
    both = [_spread_dot(jnp.concatenate([dt_all[b], dt_all[b] * (-jnp.exp(a_row))], axis=0), spread) for b in seqs]
    dt_e, da_e = [t[:c] for t in both], [t[c:] for t in both]
    acs_e = [_mask_dot(incl.astype(BF16), da_e[b]) for b in seqs]
    last_e = [jnp.sum(da_e[b], axis=0, keepdims=True) for b in seqs]
    xs = [xbc[b][:, :SSD_INNER] for b in seqs]
    xdt = [xs[b] * dt_e[b] for b in seqs]
    gw = SSD_GROUPS * SSD_STATE
    lane = _iota((1, LANES), 1)
    items = _seq_items(len(xbc), 4)
    grp = [(b, g) for b in seqs for g in range(SSD_GROUPS)]
    ps = lambda t, j: t[:, j * LANES:(j + 1) * LANES]
    bg = {(b, g): xbc[b][:, SSD_INNER + g * SSD_STATE:SSD_INNER + (g + 1) * SSD_STATE] for b, g in grp}
    cg = {(b, g): xbc[b][:, SSD_INNER + gw + g * SSD_STATE:SSD_INNER + gw + (g + 1) * SSD_STATE] for b, g in grp}
    cb = {bgk: _bdot_nt(cg[bgk], bg[bgk]) for bgk in grp}

    def seg(b, j, sub):
        ac = ps(acs_e[b], j)[:, sub * SSD_HEAD_DIM:sub * SSD_HEAD_DIM + 1]
        return jnp.where(incl, jnp.exp(jnp.where(incl, ac - _col_to_row(ac), 0.0)), 0.0)

    mine = [((lane // SSD_HEAD_DIM) == sub).astype(F32) for sub in range(2)]
    y_in = [[_bdot(cb[b, j // 2] * seg(b, j, sub), ps(xdt[b], j) * mine[sub]) for sub in range(2)] for b, j in items]
    y_st = [_bdot(cg[b, j // 2], states[i]) for i, (b, j) in enumerate(items)]
    grow = [_bdot_tn(bg[b, j // 2], ps(xdt[b], j) * jnp.exp(ps(last_e[b], j) - ps(acs_e[b], j))) for b, j in items]
    new_states = [states[i] * jnp.exp(ps(last_e[b], j)) + grow[i] for i, (b, j) in enumerate(items)]
    ys = [y_in[i][0] + y_in[i][1] + y_st[i] * jnp.exp(ps(acs_e[b], j)) + ps(d_row, j) * ps(xs[b], j)
          for i, (b, j) in enumerate(items)]
    gwid = SSD_INNER // SSD_GROUPS
    outs = []
    for b in seqs:
        yz = jnp.concatenate(ys[4 * b:4 * b + 4], axis=1) * _silu(z[b])
        outs.append(jnp.concatenate(
            [_rms(yz[:, g * gwid:(g + 1) * gwid], nw[:, g * gwid:(g + 1) * gwid]) for g in range(SSD_GROUPS)], axis=1))
    return new_states, outs


def _ssd_fwd(xbc_act, p, a_row, dt_row, d_row, nw, name):
    bsz, s, _ = xbc_act.shape
    c = SSD_CHUNK
    nc = s // c

    items = _seq_items(bsz, 4)

    def body(x_ref, sm_ref, z_ref, a_ref, dt_ref, d_ref, nw_ref, o_ref, st_ref, st_scr):
        @pl.when(pl.program_id(0) == 0)
        def _():
            st_scr[...] = jnp.zeros_like(st_scr)

        st_ref[...] = st_scr[...]
        per_seq = lambda ref: [ref[b] for b in range(bsz)]
        new_states, o = _ssd_chunk([st_scr[b, j] for b, j in items], per_seq(x_ref), per_seq(sm_ref), per_seq(z_ref),
                                   a_ref[...], dt_ref[...], d_ref[...], nw_ref[...])
        for i, (b, j) in enumerate(items):
            st_scr[b, j] = new_states[i]
        for b in range(bsz):
            o_ref[b] = o[b].astype(BF16)

    row = lambda w, blk: pl.BlockSpec((bsz, c, w), lambda n, blk=blk: (0, n, blk))
    prm = pl.BlockSpec((1, LANES), lambda n: (0, 0))
    prm5 = pl.BlockSpec((1, 512), lambda n: (0, 0))
    return _call(
        body, name, (nc,),
        [row(1024, 0), row(LANES, P_SMALL // LANES), row(512, P_SZ // 512), prm, prm, prm5, prm5],
        [row(512, 0), pl.BlockSpec((bsz, None, 4, LANES, LANES), lambda n: (0, n, 0, 0, 0))],
        [_sds((bsz, s, 512), BF16), _sds((bsz, nc, 4, LANES, LANES), F32)],
        scratch=[pltpu.VMEM((bsz, 4, LANES, LANES), F32)],
    )(xbc_act, p, p, a_row, dt_row, d_row, nw)


def _ssd_bwd(do, xbc_act, p, a_row, dt_row, d_row, nw, st_all, dp, name):
    bsz, s, _ = xbc_act.shape
    c = SSD_CHUNK
    nc = s // c

    items = _seq_items(bsz, 4)

    def body(x_ref, sm_ref, z_ref, a_ref, dt_ref, d_ref, nw_ref, do_ref, st_ref, _,
             dx_ref, dsm_ref, dz_ref, da_ref, ddt_ref, dd_ref, dnw_ref, ds_scr):
        @pl.when(pl.program_id(0) == 0)
        def _():
            ds_scr[...] = jnp.zeros_like(ds_scr)
            da_ref[...] = jnp.zeros_like(da_ref)
            ddt_ref[...] = jnp.zeros_like(ddt_ref)
            dd_ref[...] = jnp.zeros_like(dd_ref)
            dnw_ref[...] = jnp.zeros_like(dnw_ref)

        per_seq = lambda ref: [ref[b] for b in range(bsz)]
        _, vjp = jax.vjp(_ssd_chunk, [st_ref[b, j] for b, j in items], per_seq(x_ref), per_seq(sm_ref), per_seq(z_ref),
                         a_ref[...], dt_ref[...], d_ref[...], nw_ref[...])
        d_states, dx, dsm, dz, da, ddt, dd, dnw = vjp(([ds_scr[b, j] for b, j in items], per_seq(do_ref)))
        for i, (b, j) in enumerate(items):
            ds_scr[b, j] = d_states[i]
        for b in range(bsz):
            dx_ref[b] = dx[b]
            dsm_ref[b] = dsm[b]
            dz_ref[b] = dz[b].astype(BF16)
        da_ref[...] += da
        ddt_ref[...] += ddt
        dd_ref[...] += dd
        dnw_ref[...] += dnw

    row = lambda w, blk: pl.BlockSpec((bsz, c, w), lambda n, blk=blk: (0, nc - 1 - n, blk))
    prm = pl.BlockSpec((1, LANES), lambda n: (0, 0))
    prm5 = pl.BlockSpec((1, 512), lambda n: (0, 0))
    return _call(
        body, name, (nc,),
        [row(1024, 0), row(LANES, P_SMALL // LANES), row(512, P_SZ // 512), prm, prm, prm5, prm5, row(512, 0),
         pl.BlockSpec((bsz, None, 4, LANES, LANES), lambda n: (0, nc - 1 - n, 0, 0, 0)), IN_PLACE],
        [row(1024, 0), row(LANES, 0), row(512, P_SZ // 512), prm, prm, prm5, prm5],
        [_sds((bsz, s, 1024), F32), _sds((bsz, s, LANES), F32), _sds(dp.shape, BF16),
         _sds((1, LANES), F32), _sds((1, LANES), F32), _sds((1, 512), F32), _sds((1, 512), F32)],
        scratch=[pltpu.VMEM((bsz, 4, LANES, LANES), F32)], aliases={9: 2},
    )(xbc_act, p, p, a_row, dt_row, d_row, nw, do, st_all, dp)


def _small_cols(dsm_a, dsm_c, dp, name):
    bsz, s, _ = dsm_a.shape
    ts = _row_tile(s, 1024)
    width = P_WIDTH - P_SMALL

    def body(a_ref, c_ref, _, o_ref):
        o_ref[:, :LANES] = (a_ref[...] + c_ref[...]).astype(BF16)
        o_ref[:, LANES:] = jnp.zeros((ts, width - LANES), BF16)

    row = pl.BlockSpec((None, ts, LANES), lambda b, i: (b, i, 0))
    return _call(body, name, (bsz, s // ts), [row, row, IN_PLACE],
                 pl.BlockSpec((None, ts, width), lambda b, i: (b, i, P_SMALL // width)), _sds(dp.shape, BF16),
                 aliases={2: 0})(dsm_a, dsm_c, dp)


def _peer(k):
    x, y, c = lax.axis_index("x"), lax.axis_index("y"), lax.axis_index("c")
    px = 1 - x if k & 4 else x
    py = 1 - y if k & 2 else y
    pc = 1 - c if k & 1 else c
    return (px, py, pc), 4 * px + 2 * py + pc


def _my_index():
    return 4 * lax.axis_index("x") + 2 * lax.axis_index("y") + lax.axis_index("c")


def _mesh_place():
    x, y, c = lax.axis_index("x"), lax.axis_index("y"), lax.axis_index("c")
    return (x, y, c), (x, y, 1 - c), [(1 - x, y), (x, 1 - y), (1 - x, 1 - y)]


def _run_exchange(body, name, arrays, out_shape, n_sems):
    n = len(arrays)
    any_spec = pl.BlockSpec(memory_space=pl.ANY)
    return pl.pallas_call(
        body, name=name, out_shape=out_shape, in_specs=[any_spec] * n, out_specs=[any_spec] * n,
        scratch_shapes=[pltpu.SemaphoreType.DMA((n_sems, n)), pltpu.SemaphoreType.DMA((n_sems, n)),
                        pltpu.SemaphoreType.DMA((n,))],
    )(*arrays)


def _all_to_all(arrays, name):
    n = len(arrays)

    def body(*refs):
        ins, outs = refs[:n], refs[n:2 * n]
        send_sems, recv_sems, local_sems = refs[2 * n:]
        me = _my_index()

        def copy(i, k, arriving):
            peer, slot = _peer(k)
            return pltpu.make_async_remote_copy(
                src_ref=ins[i].at[slot], dst_ref=outs[i].at[slot if arriving else me], send_sem=send_sems.at[k - 1, i],
                recv_sem=recv_sems.at[k - 1, i], device_id=peer, device_id_type=MESH_ID)

        mine = [pltpu.make_async_copy(ins[i].at[me], outs[i].at[me], local_sems.at[i]) for i in range(n)]
        sends = [copy(i, k, False) for k in range(1, N_DEV) for i in range(n)]
        for cp in mine + sends:
            cp.start()
        for k in range(1, N_DEV):
            for i in range(n):
                copy(i, k, True).wait_recv()
        for cp in sends:
            cp.wait_send()
        for cp in mine:
            cp.wait()

    return _run_exchange(body, name, arrays, [_sds(a.shape, a.dtype) for a in arrays], N_DEV - 1)


def _gather_two_level(arrays, name):
    n = len(arrays)

    def body(*refs):
        ins, outs = refs[:n], refs[n:2 * n]
        send_sems, recv_sems, local_sems = refs[2 * n:]
        (x, y, c), sibling, chips = _mesh_place()
        slot = lambda px, py, pc: 4 * px + 2 * py + pc

        def copy(i, k, block, to, src=None):
            return pltpu.make_async_remote_copy(
                src_ref=outs[i].at[block] if src is None else src, dst_ref=outs[i].at[block],
                send_sem=send_sems.at[k, i], recv_sem=recv_sems.at[k, i], device_id=to, device_id_type=MESH_ID)

        me = slot(x, y, c)
        mine = [pltpu.make_async_copy(ins[i], outs[i].at[me], local_sems.at[i]) for i in range(n)]
        first = [copy(i, 0, me, sibling, src=ins[i]) for i in range(n)]
        first += [copy(i, 1 + j, me, (*chip, c), src=ins[i]) for j, chip in enumerate(chips) for i in range(n)]
        for cp in mine + first:
            cp.start()
        passed = []
        for j, chip in enumerate(chips):
            for i in range(n):
                copy(i, 1 + j, slot(*chip, c), (x, y, c)).wait_recv()
                cp = copy(i, 4 + j, slot(*chip, c), sibling)
                cp.start()
                passed.append(cp)
        for i in range(n):
            copy(i, 0, slot(x, y, 1 - c), (x, y, c)).wait_recv()
        for j, chip in enumerate(chips):
            for i in range(n):
                copy(i, 4 + j, slot(*chip, 1 - c), (x, y, c)).wait_recv()
        for cp in first + passed:
            cp.wait_send()
        for cp in mine:
            cp.wait()

    out_shape = [_sds((N_DEV,) + a.shape, a.dtype) for a in arrays]
    return _run_exchange(body, name, arrays, out_shape, 7)


def _sibling_swap(arrays, name):
    n = len(arrays)

    def body(*refs):
        ins, outs = refs[:n], refs[n:2 * n]
        send_sems, recv_sems, _ = refs[2 * n:]
        (x, y, c), sibling, _ = _mesh_place()
        copies = [pltpu.make_async_remote_copy(
            src_ref=ins[i].at[1 - c], dst_ref=outs[i], send_sem=send_sems.at[0, i], recv_sem=recv_sems.at[0, i],
            device_id=sibling, device_id_type=MESH_ID) for i in range(n)]
        for cp in copies:
            cp.start()
        for cp in copies:
            cp.wait()

    out_shape = [_sds(a.shape[1:], a.dtype) for a in arrays]
    return _run_exchange(body, name, arrays, out_shape, 1)


def _chip_scatter(arrays, name):
    n = len(arrays)

    def body(*refs):
        ins, outs = refs[:n], refs[n:2 * n]
        send_sems, recv_sems, local_sems = refs[2 * n:]
        (x, y, c), _, chips = _mesh_place()
        me = 2 * x + y
        mine = [pltpu.make_async_copy(ins[i].at[me], outs[i].at[me], local_sems.at[i]) for i in range(n)]
        sends = [pltpu.make_async_remote_copy(
            src_ref=ins[i].at[2 * chip[0] + chip[1]], dst_ref=outs[i].at[me], send_sem=send_sems.at[j, i],
            recv_sem=recv_sems.at[j, i], device_id=(*chip, c), device_id_type=MESH_ID)
            for j, chip in enumerate(chips) for i in range(n)]
        for cp in mine + sends:
            cp.start()
        for j, chip in enumerate(chips):
            for i in range(n):
                pltpu.make_async_remote_copy(
                    src_ref=ins[i].at[me], dst_ref=outs[i].at[2 * chip[0] + chip[1]], send_sem=send_sems.at[j, i],
                    recv_sem=recv_sems.at[j, i], device_id=(*chip, c), device_id_type=MESH_ID).wait_recv()
        for cp in sends:
            cp.wait_send()
        for cp in mine:
            cp.wait()

    out_shape = [_sds(a.shape, a.dtype) for a in arrays]
    return _run_exchange(body, name, arrays, out_shape, 3)


def _pair_sum(a, b, name):
    lead, rows, width = a.shape
    tr = _pick(rows, (256, 128, 64, 32, 16, 8)) if rows % 8 == 0 else rows

    def body(a_ref, b_ref, o_ref):
        o_ref[...] = (a_ref[...].astype(F32) + b_ref[...].astype(F32)).astype(o_ref.dtype)

    blk = pl.BlockSpec((None, tr, width), lambda l, i: (l, i, 0))
    return _call(body, name, (lead, rows // tr), [blk, blk], blk, _sds(a.shape, a.dtype))(a, b)


def _sum_adamw(gs, w, m, v, name):
    lead, rows, width = w.shape
    slots = gs.shape[0]
    tr = _pick(rows, (128, 64, 32, 16, 8)) if rows % 8 == 0 else rows

    def body(g_ref, w_ref, m_ref, v_ref, go_ref, d_ref, mo_ref, vo_ref):
        g = g_ref[0].astype(F32)
        for i in range(1, slots):
            g = g + g_ref[i].astype(F32)
        m2 = ADAM_B1 * m_ref[...] + (1.0 - ADAM_B1) * g
        v2 = ADAM_B2 * v_ref[...] + (1.0 - ADAM_B2) * (g * g)
        m_hat = m2 / (1.0 - ADAM_B1 ** ADAM_STEP)
        v_hat = v2 / (1.0 - ADAM_B2 ** ADAM_STEP)
        go_ref[...] = g
        d_ref[...] = -ADAM_LR * (m_hat / (jnp.sqrt(v_hat) + ADAM_EPS) + ADAM_WD * w_ref[...])
        mo_ref[...] = m2
        vo_ref[...] = v2

    blk = pl.BlockSpec((None, tr, width), lambda l, i: (l, i, 0))
    return _call(body, name, (lead, rows // tr),
                 [pl.BlockSpec((slots, None, tr, width), lambda l, i: (0, l, i, 0)), blk, blk, blk],
                 [blk] * 4, [_sds(w.shape, F32)] * 4)(gs, w, m, v)


MATMUL_WEIGHTS = ("w_in", "w_br_a", "w_br_b", "w_br_c", "w_out", "ffn_w_up", "ffn_w_down")
UNALIGNED = ("w_in", "ffn_w_up")
SPLIT = (
    ("w_in", (DEPTH, D_MODEL, 8720), 2),
    ("gdn_conv_w", (DEPTH, 4, 1536), 2), ("ssd_conv_w", (DEPTH, 4, 1024), 2),
    ("w_br_a", (DEPTH, 512, D_MODEL), 2), ("w_br_b", (DEPTH, 512, D_MODEL), 2), ("w_br_c", (DEPTH, 512, D_MODEL), 2),
    ("w_out", (DEPTH, D_MODEL, D_MODEL), 1), ("ffn_w_up", (DEPTH, D_MODEL, 2 * FFN_HIDDEN), 2),
    ("ffn_conv_w", (DEPTH, 3, 2 * FFN_HIDDEN), 2), ("ffn_w_down", (DEPTH, FFN_HIDDEN, D_MODEL), 1),
)
REPL = (
    ("b_ada", (DEPTH, 6 * D_MODEL)), ("norm1_w", (DEPTH, D_MODEL)), ("gdn_a_log", (DEPTH, 4)),
    ("gdn_dt_bias", (DEPTH, 4)), ("gdn_norm_w", (DEPTH, 128)), ("hgrn_lb_param", (DEPTH, 512)),
    ("hgrn_norm_w", (DEPTH, 128)), ("ssd_conv_b", (DEPTH, 1024)), ("ssd_a_log", (DEPTH, 8)),
    ("ssd_dt_bias", (DEPTH, 8)), ("ssd_d", (DEPTH, 8)), ("ssd_norm_w", (DEPTH, 512)), ("norm2_w", (DEPTH, D_MODEL)),
    ("ffn_conv_b", (DEPTH, 2 * FFN_HIDDEN)), ("final_norm_w", (D_MODEL,)),
)
WEIGHTS = ("w_ada", "b_ada", "norm1_w", "w_in", "gdn_conv_w", "gdn_a_log", "gdn_dt_bias", "gdn_norm_w",
           "hgrn_lb_param", "hgrn_norm_w", "ssd_conv_w", "ssd_conv_b", "ssd_a_log", "ssd_dt_bias", "ssd_d",
           "ssd_norm_w", "w_br_a", "w_br_b", "w_br_c", "w_out", "norm2_w", "ffn_w_up", "ffn_conv_w", "ffn_conv_b",
           "ffn_w_down", "final_norm_w")


def _block_shape(shape, axis):
    return tuple(d // N_DEV if i == axis else d for i, d in enumerate(shape))


def _join_blocks(gathered, shape, axis):
    return jnp.moveaxis(gathered, 0, axis).reshape(shape)


def _split_blocks(full, shape, axis):
    bs = _block_shape(shape, axis)
    t = full.reshape(shape[:axis] + (N_DEV, bs[axis]) + shape[axis + 1:])
    return jnp.moveaxis(t, axis, 0)


def _pack_repl(vals):
    parts = []
    for n, shape in REPL:
        size = math.prod(shape)
        parts.append(jnp.pad(vals[n].reshape(-1), (0, -(-size // PACK_W) * PACK_W - size)))
    cat = jnp.concatenate(parts)
    rows = -(-cat.shape[0] // (8 * PACK_W)) * 8
    return jnp.pad(cat, (0, rows * PACK_W - cat.shape[0])).reshape(rows, PACK_W)


def _unpack_repl(packed):
    flat, out, off = packed.reshape(-1), {}, 0
    for n, shape in REPL:
        size = math.prod(shape)
        out[n] = flat[off:off + size].reshape(shape)
        off += -(-size // PACK_W) * PACK_W
    return out


def _lane_row(vec, lane0):
    return jnp.pad(vec, (lane0, LANES - lane0 - vec.shape[0]))[None]


def _arrange_w_in(w):
    offs = [0]
    for sz in W_IN_SPLITS:
        offs.append(offs[-1] + sz)
    qkv, a, b, gz, hq, hf, hi, hg, sz_, xbc, dt, gate = [w[:, offs[i]:offs[i + 1]] for i in range(12)]
    pad = jnp.zeros((w.shape[0], P_WIDTH - P_SMALL - 16), w.dtype)
    return jnp.concatenate([qkv, gz, xbc, gate, hq, hf, hi, hg, sz_, a, b, dt, pad], axis=1)


def _restore_w_in(wp):
    cut = lambda o, n: wp[:, o:o + n]
    return jnp.concatenate([
        cut(P_QKV, 1536), cut(P_SMALL + SM_A, 4), cut(P_SMALL + SM_B, 4), cut(P_GZ, 512), cut(P_HQ, 512),
        cut(P_HF, 512), cut(P_HI, 512), cut(P_HG, 512), cut(P_SZ, 512), cut(P_XBC, 1024), cut(P_SMALL + SM_DT, 8),
        cut(P_GATE, 3072)], axis=1)


def _join_cols(gathered, arrange, name):
    _, depth, rows, cols = gathered.shape
    tr = _pick(rows, (256, 128, 64, 32, 16, 8))
    width = P_WIDTH if arrange else N_DEV * cols

    def body(g_ref, o_ref):
        row = jnp.concatenate([g_ref[d] for d in range(N_DEV)], axis=1)
        o_ref[...] = _arrange_w_in(row) if arrange else row

    return _call(body, name, (depth, rows // tr),
                 [pl.BlockSpec((N_DEV, None, tr, cols), lambda l, i: (0, l, i, 0))],
                 pl.BlockSpec((None, tr, width), lambda l, i: (l, i, 0)), _sds((depth, rows, width), gathered.dtype),
                 )(gathered)


def _split_cols(per_layer, restore, cols, name):
    depth = len(per_layer)
    rows = per_layer[0].shape[0]
    tr = _pick(rows, (256, 128, 64, 32, 16, 8))
    nt = rows // tr

    def body(*refs):
        o_ref = refs[depth]
        for l in range(depth):
            @pl.when(pl.program_id(0) == l)
            def _(l=l):
                row = _restore_w_in(refs[l][...]) if restore else refs[l][...]
                for d in range(N_DEV):
                    o_ref[d % 2, d // 2] = row[:, d * cols:(d + 1) * cols]

    return _call(body, name, (depth, nt),
                 [pl.BlockSpec((tr, a.shape[1]), lambda l, i: (i, 0)) for a in per_layer],
                 pl.BlockSpec((2, N_DEV // 2, tr, cols), lambda l, i: (0, 0, l * nt + i, 0)),
                 _sds((2, N_DEV // 2, depth * rows, cols), per_layer[0].dtype))(*per_layer)


def _layer_consts(l, wf, wr, lower):
    t = lambda a: a.T
    k = {}
    k["n1w"], k["n2w"] = wr["norm1_w"][l][None], wr["norm2_w"][l][None]
    k["win"], k["win_t"] = wf["w_in"][l], t(wf["w_in"][l])
    for n in ("w_br_a", "w_br_b", "w_br_c", "w_out", "ffn_w_up", "ffn_w_down"):
        k[n], k[n + "_t"] = wf[n][l], t(wf[n][l])
    k["gdn_conv_w"], k["gdn_conv_b"] = wf["gdn_conv_w"][l], jnp.zeros((1, 1536), F32)
    k["ssd_conv_w"], k["ssd_conv_b"] = wf["ssd_conv_w"][l], wr["ssd_conv_b"][l][None]
    k["ffn_conv_w"], k["ffn_conv_b"] = wf["ffn_conv_w"][l], wr["ffn_conv_b"][l][None]
    k["gdn_a"], k["gdn_dt"] = _lane_row(wr["gdn_a_log"][l], SM_A), _lane_row(wr["gdn_dt_bias"][l], SM_A)
    k["gdn_nw"], k["hgrn_nw"] = wr["gdn_norm_w"][l][None], wr["hgrn_norm_w"][l][None]
    k["ssd_a"], k["ssd_dt"] = _lane_row(wr["ssd_a_log"][l], SM_DT), _lane_row(wr["ssd_dt_bias"][l], SM_DT)
    k["ssd_d"] = jnp.repeat(wr["ssd_d"][l], SSD_HEAD_DIM)[None]
    k["ssd_nw"] = wr["ssd_norm_w"][l][None]
    k["lb"] = lower[l:l + 1]
    return k


def _layer_fwd(l, x, mod, k):
    bsz, s, d = x.shape
    t = bsz * s
    sv = {"x": x}
    sv["mod"] = [mod[:, None, i * d:(i + 1) * d] for i in range(6)]
    sh1, sc1, g1, sh2, sc2, g2 = sv["mod"]
    h1 = _norm_mod_fwd(x, k["n1w"], sh1, sc1, f"norm1_fwd{l}")
    p = _mm(h1.reshape(t, d), k["win"], F32, f"mm_in{l}").reshape(bsz, s, P_WIDTH)
    qkv_act = _conv_fwd(p, P_QKV, 1536, k["gdn_conv_w"], k["gdn_conv_b"], f"gdn_conv_fwd{l}")
    oa, st_a, ti_a = _gdn_fwd(qkv_act, p, k["gdn_a"], k["gdn_dt"], k["gdn_nw"], f"gdn_fwd{l}")
    ob, st_b = _hgrn_fwd(p, k["lb"], k["hgrn_nw"], f"hgrn_fwd{l}")
    xbc_act = _conv_fwd(p, P_XBC, 1024, k["ssd_conv_w"], k["ssd_conv_b"], f"ssd_conv_fwd{l}")
    oc, st_c = _ssd_fwd(xbc_act, p, k["ssd_a"], k["ssd_dt"], k["ssd_d"], k["ssd_nw"], f"ssd_fwd{l}")
    merged = _merge_fwd(p, oa, ob, oc, k["w_br_a"], k["w_br_b"], k["w_br_c"], f"merge_fwd{l}")
    mix = _mm(merged.reshape(t, d), k["w_out"], F32, f"mm_out{l}").reshape(bsz, s, d)
    x1 = _resid_fwd(x, mix, g1, f"resid1_fwd{l}")
    h2 = _norm_mod_fwd(x1, k["n2w"], sh2, sc2, f"norm2_fwd{l}")
    u_pre = _mm(h2.reshape(t, d), k["ffn_w_up"], F32, f"mm_up{l}").reshape(bsz, s, 2 * FFN_HIDDEN)
    a = _conv_glu_fwd(u_pre, k["ffn_conv_w"], k["ffn_conv_b"], f"ffn_conv_glu_fwd{l}")
    ffn = _mm(a.reshape(t, FFN_HIDDEN), k["ffn_w_down"], F32, f"mm_down{l}").reshape(bsz, s, d)
    x2 = _resid_fwd(x1, ffn, g2, f"resid2_fwd{l}")
    sv.update(h1=h1, p=p, qkv_act=qkv_act, oa=oa, st_a=st_a, ti_a=ti_a, ob=ob, st_b=st_b, xbc_act=xbc_act, oc=oc, st_c=st_c,
              merged=merged, mix=mix, x1=x1, h2=h2, u_pre=u_pre, a=a, ffn=ffn)
    return x2, sv


def _layer_bwd(l, dx2, k, sv):
    bsz, s, d = dx2.shape
    t = bsz * s
    f2 = 2 * FFN_HIDDEN
    sh1, sc1, g1, sh2, sc2, g2 = sv["mod"]
    tr = lambda a: a.reshape(t, -1).T
    g = {}
    dffn, dg2 = _gate_bwd(dx2, sv["ffn"], g2, f"gate2_bwd{l}")
    dffn2 = dffn.reshape(t, d)
    da = _mm(dffn2, k["ffn_w_down_t"], F32, f"mm_down_dx{l}").reshape(bsz, s, FFN_HIDDEN)
    g["ffn_w_down"] = _mm(tr(sv["a"]), dffn2, BF16, f"mm_down_dw{l}")
    du_pre, g["ffn_conv_w"], dfcb = _conv_glu_bwd(da, sv["u_pre"], k["ffn_conv_w"], k["ffn_conv_b"], f"ffn_conv_glu_bwd{l}")
    g["ffn_conv_b"] = dfcb[0]
    du2 = du_pre.reshape(t, f2)
    dh2 = _mm(du2, k["ffn_w_up_t"], F32, f"mm_up_dx{l}").reshape(bsz, s, d)
    g["ffn_w_up"] = _mm(tr(sv["h2"]), du2, BF16, f"mm_up_dw{l}")
    dx1, dn2w, dsh2, dsc2 = _norm_mod_bwd(sv["x1"], k["n2w"], sh2, sc2, dh2, dx2, f"norm2_bwd{l}")
    g["norm2_w"] = dn2w[0]
    dmix, dg1 = _gate_bwd(dx1, sv["mix"], g1, f"gate1_bwd{l}")
    dmix2 = dmix.reshape(t, d)
    dmerged = _mm(dmix2, k["w_out_t"], F32, f"mm_out_dx{l}").reshape(bsz, s, d)
    g["w_out"] = _mm(tr(sv["merged"]), dmix2, BF16, f"mm_out_dw{l}")
    p = sv["p"]
    dp, doa, dob, doc, dya, dyb, dyc = _merge_bwd(
        dmerged, p, sv["oa"], sv["ob"], sv["oc"], k["w_br_a"], k["w_br_b"], k["w_br_c"],
        k["w_br_a_t"], k["w_br_b_t"], k["w_br_c_t"], f"merge_bwd{l}")
    g["w_br_a"] = _mm(tr(sv["oa"]), dya.reshape(t, d), BF16, f"mm_bra_dw{l}")
    g["w_br_b"] = _mm(tr(sv["ob"]), dyb.reshape(t, d), BF16, f"mm_brb_dw{l}")
    g["w_br_c"] = _mm(tr(sv["oc"]), dyc.reshape(t, d), BF16, f"mm_brc_dw{l}")
    dxbc_act, dsm_c, dp, da_c, ddt_c, dd_c, dnw_c = _ssd_bwd(
        doc, sv["xbc_act"], p, k["ssd_a"], k["ssd_dt"], k["ssd_d"], k["ssd_nw"], sv["st_c"], dp, f"ssd_bwd{l}")
    dp, g["ssd_conv_w"], dscb = _conv_bwd(dxbc_act, p, P_XBC, 1024, k["ssd_conv_w"], k["ssd_conv_b"], dp, f"ssd_conv_bwd{l}")
    g["ssd_conv_b"] = dscb[0]
    g["ssd_a_log"], g["ssd_dt_bias"] = da_c[0, SM_DT:SM_DT + 8], ddt_c[0, SM_DT:SM_DT + 8]
    g["ssd_d"] = dd_c.reshape(SSD_HEADS, SSD_HEAD_DIM).sum(axis=1)
    g["ssd_norm_w"] = dnw_c[0]
    dp, dlb, dnw_b = _hgrn_bwd(dob, p, k["lb"], k["hgrn_nw"], sv["st_b"], dp, f"hgrn_bwd{l}")
    g["hgrn_norm_w"] = dnw_b[0]
    dqkv_act, dsm_a, dp, da_a, ddt_a, dnw_a = _gdn_bwd(
        doa, sv["qkv_act"], p, k["gdn_a"], k["gdn_dt"], k["gdn_nw"], sv["st_a"], sv["ti_a"], dp, f"gdn_bwd{l}")
    dp, g["gdn_conv_w"], _ = _conv_bwd(dqkv_act, p, P_QKV, 1536, k["gdn_conv_w"], k["gdn_conv_b"], dp, f"gdn_conv_bwd{l}")
    g["gdn_a_log"], g["gdn_dt_bias"], g["gdn_norm_w"] = da_a[0, :4], ddt_a[0, :4], dnw_a[0]
    dp = _small_cols(dsm_a, dsm_c, dp, f"small_cols{l}").reshape(t, P_WIDTH)
    dh1 = _mm(dp, k["win_t"], F32, f"mm_in_dx{l}").reshape(bsz, s, d)
    g["w_in"] = _mm(tr(sv["h1"]), dp, BF16, f"mm_in_dw{l}")
    dx, dn1w, dsh1, dsc1 = _norm_mod_bwd(sv["x"], k["n1w"], sh1, sc1, dh1, dx1, f"norm1_bwd{l}")
    g["norm1_w"] = dn1w[0]
    dmod = jnp.concatenate([dsh1, dsc1, dg1, dsh2, dsc2, dg2], axis=-1)[:, 0]
    return dx, g, dlb, dmod


def _local_step(x, mod, wf, wr, target):
    lower = _lb_fwd(wr["hgrn_lb_param"])
    ks = [_layer_consts(l, wf, wr, lower) for l in range(DEPTH)]
    saved = []
    h = x
    for l in range(DEPTH):
        h, sv = _layer_fwd(l, h, mod[l], ks[l])
        saved.append(sv)
    loss8, dh, dfnw = _final_loss(h, wr["final_norm_w"][None], target)
    per_layer, dlbs, dmods = [None] * DEPTH, [None] * DEPTH, [None] * DEPTH
    for l in reversed(range(DEPTH)):
        dh, per_layer[l], dlbs[l], dmods[l] = _layer_bwd(l, dh, ks[l], saved[l])
    grads = {n: [per_layer[l][n] for l in range(DEPTH)] for n in per_layer[0]}
    grads = {n: g if n in UNALIGNED else jnp.stack(g) for n, g in grads.items()}
    grads["hgrn_lb_param"] = _lb_bwd(wr["hgrn_lb_param"], jnp.concatenate(dlbs, axis=0))
    grads["final_norm_w"] = dfnw[0]
    return loss8[0, 0], dh, grads, jnp.stack(dmods)


def kernel(x, c, w_ada, b_ada, norm1_w, w_in, gdn_conv_w, gdn_a_log, gdn_dt_bias, gdn_norm_w, hgrn_lb_param, hgrn_norm_w, ssd_conv_w, ssd_conv_b, ssd_a_log, ssd_dt_bias, ssd_d, ssd_norm_w, w_br_a, w_br_b, w_br_c, w_out, norm2_w, ffn_w_up, ffn_conv_w, ffn_conv_b, ffn_w_down, final_norm_w, loss_target, m_w_ada, m_b_ada, m_norm1_w, m_w_in, m_gdn_conv_w, m_gdn_a_log, m_gdn_dt_bias, m_gdn_norm_w, m_hgrn_lb_param, m_hgrn_norm_w, m_ssd_conv_w, m_ssd_conv_b, m_ssd_a_log, m_ssd_dt_bias, m_ssd_d, m_ssd_norm_w, m_w_br_a, m_w_br_b, m_w_br_c, m_w_out, m_norm2_w, m_ffn_w_up, m_ffn_conv_w, m_ffn_conv_b, m_ffn_w_down, m_final_norm_w, v_w_ada, v_b_ada, v_norm1_w, v_w_in, v_gdn_conv_w, v_gdn_a_log, v_gdn_dt_bias, v_gdn_norm_w, v_hgrn_lb_param, v_hgrn_norm_w, v_ssd_conv_w, v_ssd_conv_b, v_ssd_a_log, v_ssd_dt_bias, v_ssd_d, v_ssd_norm_w, v_w_br_a, v_w_br_b, v_w_br_c, v_w_out, v_norm2_w, v_ffn_w_up, v_ffn_conv_w, v_ffn_conv_b, v_ffn_w_down, v_final_norm_w):
    given = dict(locals())
    w = {n: given[n] for n in WEIGHTS}
    m = {n: given["m_" + n] for n in WEIGHTS}
    v = {n: given["v_" + n] for n in WEIGHTS}
    me = _my_index()
    bsz = c.shape[0]
    ncol = 6 * D_MODEL // N_DEV

    shards = [w[n].astype(BF16) if n in MATMUL_WEIGHTS else w[n] for n, _, _ in SPLIT] + [c]
    gathered = _gather_two_level(shards, "gather_weights")
    wf = {n: _join_cols(g, n == "w_in", f"join_{n}") if n in UNALIGNED else _join_blocks(g, shape, axis)
          for (n, shape, axis), g in zip(SPLIT, gathered)}
    c_all = gathered[-1].reshape(N_DEV * bsz, D_MODEL)

    b_cols = lax.dynamic_slice_in_dim(b_ada, me * ncol, ncol, axis=1)[:, None]
    mod_cols = _ada_fwd(c_all, w_ada, b_cols)
    send = mod_cols.reshape(DEPTH, N_DEV, bsz, ncol).transpose(1, 0, 2, 3)
    got = _all_to_all([send], "scatter_mod")[0]
    mod = got.transpose(1, 2, 0, 3).reshape(DEPTH, bsz, 6 * D_MODEL)

    loss, dx, grads, dmod = _local_step(x, mod, wf, w, loss_target)

    send = dmod.reshape(DEPTH, bsz, N_DEV, ncol).transpose(2, 0, 1, 3)
    got_dmod = _all_to_all([send], "scatter_dmod")[0]
    dmod_all = got_dmod.transpose(1, 0, 2, 3).reshape(DEPTH, N_DEV * bsz, ncol)
    g_w_ada, g_b_cols = _ada_bwd(c_all.T, dmod_all)

    core = lax.axis_index("c")
    by_core = []
    for n, shape, axis in SPLIT:
        if n in UNALIGNED:
            by_core.append(_split_cols(grads[n], n == "w_in", shape[axis] // N_DEV, f"split_{n}"))
            continue
        parts = _split_blocks(grads[n], shape, axis).astype(BF16)
        parts = parts.reshape((N_DEV // 2, 2, -1, parts.shape[-1]))
        by_core.append(jnp.swapaxes(parts, 0, 1))
    from_sibling = _sibling_swap(by_core, "swap_grads")
    sums = [_pair_sum(lax.dynamic_index_in_dim(mine, core, 0, keepdims=False), theirs, f"pair_sum_{n}")
            for (n, _, _), mine, theirs in zip(SPLIT, by_core, from_sibling)]
    got = _chip_scatter(sums, "scatter_grads")
    grads["b_ada"] = lax.dynamic_update_slice_in_dim(jnp.zeros_like(b_ada), g_b_cols[:, 0], me * ncol, axis=1)

    out = {}
    slots = [(n, g8) for (n, _, _), g8 in zip(SPLIT, got)] + [("w_ada", g_w_ada[None])]
    for n, gs in slots:
        out[n] = _sum_adamw(gs.reshape((gs.shape[0],) + w[n].shape), w[n], m[n], v[n], f"adamw_{n}")
    r8 = _gather_two_level([_pack_repl(grads)], "gather_small_grads")[0]
    res = _sum_adamw(r8[:, None], _pack_repl(w)[None], _pack_repl(m)[None], _pack_repl(v)[None], "adamw_repl")
    repl_out = [_unpack_repl(o[0]) for o in res]
    pick = lambda i, n: out[n][i] if n in out else repl_out[i][n]
    loss = lax.psum(loss, ("x", "y", "c"))
    return (loss, dx, *[pick(i, n) for i in range(4) for n in WEIGHTS])
```

```python
import functools
import math

import jax
import jax.numpy as jnp
from jax import lax
from jax.experimental import pallas as pl
from jax.experimental.pallas import tpu as pltpu

F32, BF16 = jnp.float32, jnp.bfloat16
HI = lax.Precision.HIGHEST
MESH_ID = pl.DeviceIdType.MESH

N_DEV = 8
EPS = 1e-6
D_MODEL = 1024
DEPTH = 2
GDN_HEADS, GDN_DK, GDN_CHUNK = 4, 128, 64
HGRN_HEADS, HGRN_DK, HGRN_CHUNK, HGRN_BLOCK = 4, 128, 16, 128
SSD_HEADS, SSD_HEAD_DIM, SSD_GROUPS, SSD_STATE, SSD_CHUNK = 8, 64, 2, 128, 64
SSD_INNER = SSD_HEADS * SSD_HEAD_DIM
FFN_HIDDEN = 2816
LANES = 128
P_QKV, P_GZ, P_XBC, P_GATE, P_HQ, P_HF, P_HI, P_HG, P_SZ, P_SMALL, P_WIDTH = (
    0, 1536, 2048, 3072, 6144, 6656, 7168, 7680, 8192, 8704, 9216)
SM_A, SM_B, SM_DT = 0, 4, 8
W_IN_SPLITS = (1536, 4, 4, 512, 512, 512, 512, 512, 512, 1024, 8, 3072)

ADAM_LR, ADAM_B1, ADAM_B2, ADAM_EPS, ADAM_WD, ADAM_STEP = 0.001, 0.9, 0.999, 1e-08, 0.01, 10

V7X_VMEM_LIMIT = 56 * 1024 * 1024
MM_OPERAND_VMEM = 34 * 1024 * 1024
PACK_W = 1024


def _call(body, name, grid, in_specs, out_specs, out_shape, scratch=(), aliases=None):
    return pl.pallas_call(
        body, name=name, grid=grid, in_specs=in_specs, out_specs=out_specs, out_shape=out_shape,
        scratch_shapes=list(scratch), input_output_aliases=aliases or {},
        compiler_params=pltpu.CompilerParams(
            dimension_semantics=("arbitrary",) * len(grid), vmem_limit_bytes=V7X_VMEM_LIMIT),
    )


IN_PLACE = pl.BlockSpec(memory_space=pl.ANY)


def _pick(n, cands):
    for c in cands:
        if n % c == 0:
            return c
    raise ValueError(f"no tile for {n} among {cands}")


def _row_tile(s, cap):
    t = cap
    while s % t:
        t //= 2
    return t


def _sds(shape, dtype):
    return jax.ShapeDtypeStruct(shape, dtype)


def _dot(a, b):
    return lax.dot_general(a, b, (((1,), (0,)), ((), ())), precision=HI, preferred_element_type=F32)


NN, NT, TN = (((1,), (0,)), ((), ())), (((1,), (1,)), ((), ())), (((0,), (0,)), ((), ()))


def _mxu(a, b, dims):
    return lax.dot_general(a.astype(BF16), b.astype(BF16), dims, preferred_element_type=F32)


@jax.custom_vjp
def _bdot(a, b):
    return _mxu(a, b, NN)


@jax.custom_vjp
def _bdot_nt(a, b):
    return _mxu(a, b, NT)


@jax.custom_vjp
def _bdot_tn(a, b):
    return _mxu(a, b, TN)


_bdot.defvjp(lambda a, b: (_mxu(a, b, NN), (a, b)), lambda r, d: (_mxu(d, r[1], NT), _mxu(r[0], d, TN)))
_bdot_nt.defvjp(lambda a, b: (_mxu(a, b, NT), (a, b)), lambda r, d: (_mxu(d, r[1], NN), _mxu(d, r[0], TN)))
_bdot_tn.defvjp(lambda a, b: (_mxu(a, b, TN), (a, b)), lambda r, d: (_mxu(r[1], d, NT), _mxu(r[0], d, NN)))


def _split(x):
    hi = x.astype(BF16)
    return hi, (x - hi.astype(F32)).astype(BF16)


def _mxu3(a, b, dims):
    ah, al = _split(a)
    bh, bl = _split(b)
    return _mxu(ah, bh, dims) + (_mxu(ah, bl, dims) + _mxu(al, bh, dims))


@jax.custom_vjp
def _dot3(a, b):
    return _mxu3(a, b, NN)


_dot3.defvjp(lambda a, b: (_mxu3(a, b, NN), (a, b)), lambda r, d: (_mxu3(d, r[1], NT), _mxu3(r[0], d, TN)))


def _pieces(x):
    x1 = x.astype(BF16)
    r1 = x - x1.astype(F32)
    x2 = r1.astype(BF16)
    return x1, x2, (r1 - x2.astype(F32)).astype(BF16)


def _mask_mxu(mask, x, dims):
    x1, x2, x3 = _pieces(x)
    return _mxu(mask, x1, dims) + (_mxu(mask, x2, dims) + _mxu(mask, x3, dims))


def _spread_mxu(x, mask, dims):
    x1, x2, x3 = _pieces(x)
    return _mxu(x1, mask, dims) + (_mxu(x2, mask, dims) + _mxu(x3, mask, dims))


@jax.custom_vjp
def _mask_dot(mask, x):
    return _mask_mxu(mask, x, NN)


@jax.custom_vjp
def _spread_dot(x, mask):
    return _spread_mxu(x, mask, NN)


_mask_dot.defvjp(lambda m, x: (_mask_mxu(m, x, NN), m), lambda m, d: (jnp.zeros_like(m), _mask_mxu(m, d, TN)))
_spread_dot.defvjp(lambda x, m: (_spread_mxu(x, m, NN), m), lambda m, d: (_spread_mxu(d, m, NT), jnp.zeros_like(m)))


def _iota(shape, axis):
    return lax.broadcasted_iota(jnp.int32, shape, axis)


def _silu(x):
    return x * jax.nn.sigmoid(x)


def _softplus(x):
    return jnp.maximum(x, 0.0) + jnp.log1p(jnp.exp(-jnp.abs(x)))


def _rms(x, w):
    return x * lax.rsqrt(jnp.mean(x * x, axis=-1, keepdims=True) + EPS) * w


def _lane_col(x, lane):
    m = (_iota(x.shape, 1) == lane).astype(F32)
    return jnp.sum(x * m, axis=1, keepdims=True)


def _col_to_row(c):
    n = c.shape[0]
    eye = (_iota((n, n), 0) == _iota((n, n), 1)).astype(F32)
    return jnp.sum(c * eye, axis=0, keepdims=True)


def _tril(n, strict=False):
    r, c = _iota((n, n), 0), _iota((n, n), 1)
    return (r > c) if strict else (r >= c)


def _mm(a, b, out_dtype, name):
    m, k = a.shape
    n = b.shape[1]
    tm = _pick(m, (1024, 1408, 512, 256, 128, 64, 32, 16, 8))
    tn = _pick(n, (1024, 1408, 768, 512, 384, 256, 128))
    fits = lambda c: k % c == 0 and (tm + tn) * c * 2 * 2 <= MM_OPERAND_VMEM
    tk = next(c for c in (k, 4096, 3072, 2816, 2048, 1024, 768, 512, 384, 256, 128) if fits(c))
    nk = k // tk

    def body_one(a_ref, b_ref, o_ref):
        o_ref[...] = _bdot(a_ref[...], b_ref[...]).astype(out_dtype)

    def body(a_ref, b_ref, o_ref, acc_ref):
        kk = pl.program_id(2)

        @pl.when(kk == 0)
        def _():
            acc_ref[...] = jnp.zeros_like(acc_ref)

        acc_ref[...] += _bdot(a_ref[...], b_ref[...])

        @pl.when(kk == nk - 1)
        def _():
            o_ref[...] = acc_ref[...].astype(out_dtype)

    return _call(
        body_one if nk == 1 else body, name, (m // tm, n // tn, nk),
        [pl.BlockSpec((tm, tk), lambda i, j, kk: (i, kk)), pl.BlockSpec((tk, tn), lambda i, j, kk: (kk, j))],
        pl.BlockSpec((tm, tn), lambda i, j, kk: (i, j)), _sds((m, n), out_dtype),
        scratch=[] if nk == 1 else [pltpu.VMEM((tm, tn), F32)],
    )(a, b)


def _ada_fwd(c_all, w, b):
    depth, _, n = w.shape
    rows = c_all.shape[0]

    def body(c_ref, w_ref, b_ref, o_ref):
        o_ref[...] = _dot(_silu(c_ref[...]), w_ref[...]) + b_ref[...]

    return _call(
        body, "ada_fwd", (depth,),
        [pl.BlockSpec((rows, D_MODEL), lambda l: (0, 0)), pl.BlockSpec((None, D_MODEL, n), lambda l: (l, 0, 0)),
         pl.BlockSpec((None, 1, n), lambda l: (l, 0, 0))],
        pl.BlockSpec((None, rows, n), lambda l: (l, 0, 0)), _sds((depth, rows, n), F32),
    )(c_all, w, b)


def _ada_bwd(c_all_t, dmod):
    depth, rows, n = dmod.shape

    def body(ct_ref, dm_ref, dw_ref, db_ref):
        dm = dm_ref[...]
        dw_ref[...] = _dot(_silu(ct_ref[...]), dm)
        db_ref[...] = jnp.sum(dm, axis=0, keepdims=True)

    return _call(
        body, "ada_bwd", (depth,),
        [pl.BlockSpec((D_MODEL, rows), lambda l: (0, 0)), pl.BlockSpec((None, rows, n), lambda l: (l, 0, 0))],
        [pl.BlockSpec((None, D_MODEL, n), lambda l: (l, 0, 0)), pl.BlockSpec((None, 1, n), lambda l: (l, 0, 0))],
        [_sds((depth, D_MODEL, n), F32), _sds((depth, 1, n), F32)],
    )(c_all_t, dmod)


def _lb_fn(p):
    rows = [p[l:l + 1] for l in range(DEPTH)]
    mx = functools.reduce(jnp.maximum, rows)
    es = [jnp.exp(r - mx) for r in rows]
    tot = functools.reduce(lambda a, b: a + b, es)
    sm = [e / tot for e in es]
    out, run = [], None
    for l in range(DEPTH):
        run = sm[l] if run is None else run + sm[l]
        out.append(run - sm[0])
    return jnp.concatenate(out, axis=0)


def _lb_fwd(p):
    def body(p_ref, o_ref):
        o_ref[...] = _lb_fn(p_ref[...])

    full = pl.BlockSpec(p.shape, lambda i: (0, 0))
    return _call(body, "lb_fwd", (1,), [full], full, _sds(p.shape, F32))(p)


def _lb_bwd(p, d_lower):
    def body(p_ref, d_ref, o_ref):
        _, vjp = jax.vjp(_lb_fn, p_ref[...])
        o_ref[...] = vjp(d_ref[...])[0]

    full = pl.BlockSpec(p.shape, lambda i: (0, 0))
    return _call(body, "lb_bwd", (1,), [full, full], full, _sds(p.shape, F32))(p, d_lower)


def _norm_mod_fn(x, w, shift, scale):
    return _rms(x, w) * (1.0 + scale) + shift


def _norm_mod_fwd(x, w, shift, scale, name):
    bsz, s, d = x.shape
    ts = _row_tile(s, 512)

    def body(x_ref, w_ref, sh_ref, sc_ref, o_ref):
        o_ref[...] = _norm_mod_fn(x_ref[...], w_ref[...], sh_ref[...], sc_ref[...]).astype(BF16)

    row = pl.BlockSpec((None, ts, d), lambda b, i: (b, i, 0))
    per_b = pl.BlockSpec((None, 1, d), lambda b, i: (b, 0, 0))
    return _call(body, name, (bsz, s // ts), [row, pl.BlockSpec((1, d), lambda b, i: (0, 0)), per_b, per_b],
                 row, _sds(x.shape, BF16))(x, w, shift, scale)


def _norm_mod_bwd(x, w, shift, scale, dh, carry, name):
    bsz, s, d = x.shape
    ts = _row_tile(s, 512)

    def body(x_ref, w_ref, sh_ref, sc_ref, dh_ref, c_ref, dx_ref, dw_ref, dsh_ref, dsc_ref):
        b, i = pl.program_id(0), pl.program_id(1)
        _, vjp = jax.vjp(_norm_mod_fn, x_ref[...], w_ref[...], sh_ref[...], sc_ref[...])
        dx, dw, dsh, dsc = vjp(dh_ref[...])
        dx_ref[...] = dx + c_ref[...]

        @pl.when((b == 0) & (i == 0))
        def _():
            dw_ref[...] = jnp.zeros_like(dw_ref)

        @pl.when(i == 0)
        def _():
            dsh_ref[...] = jnp.zeros_like(dsh_ref)
            dsc_ref[...] = jnp.zeros_like(dsc_ref)

        dw_ref[...] += dw
        dsh_ref[...] += dsh
        dsc_ref[...] += dsc

    row = pl.BlockSpec((None, ts, d), lambda b, i: (b, i, 0))
    per_b = pl.BlockSpec((None, 1, d), lambda b, i: (b, 0, 0))
    wspec = pl.BlockSpec((1, d), lambda b, i: (0, 0))
    return _call(body, name, (bsz, s // ts), [row, wspec, per_b, per_b, row, row],
                 [row, wspec, per_b, per_b],
                 [_sds(x.shape, F32), _sds((1, d), F32), _sds((bsz, 1, d), F32), _sds((bsz, 1, d), F32)],
                 )(x, w, shift, scale, dh, carry)


def _resid_fwd(x, y, gate, name):
    bsz, s, d = x.shape
    ts = _row_tile(s, 1024)

    def body(x_ref, y_ref, g_ref, o_ref):
        o_ref[...] = x_ref[...] + g_ref[...] * y_ref[...]

    row = pl.BlockSpec((None, ts, d), lambda b, i: (b, i, 0))
    per_b = pl.BlockSpec((None, 1, d), lambda b, i: (b, 0, 0))
    return _call(body, name, (bsz, s // ts), [row, row, per_b], row, _sds(x.shape, F32))(x, y, gate)


def _gate_bwd(dx, y, gate, name):
    bsz, s, d = dx.shape
    ts = _row_tile(s, 1024)

    def body(dx_ref, y_ref, g_ref, dy_ref, dg_ref):
        dxv = dx_ref[...]
        dy_ref[...] = (dxv * g_ref[...]).astype(BF16)

        @pl.when(pl.program_id(1) == 0)
        def _():
            dg_ref[...] = jnp.zeros_like(dg_ref)

        dg_ref[...] += jnp.sum(dxv * y_ref[...], axis=0, keepdims=True)

    row = pl.BlockSpec((None, ts, d), lambda b, i: (b, i, 0))
    per_b = pl.BlockSpec((None, 1, d), lambda b, i: (b, 0, 0))
    return _call(body, name, (bsz, s // ts), [row, row, per_b], [row, per_b],
                 [_sds(dx.shape, BF16), _sds((bsz, 1, d), F32)])(dx, y, gate)


HALO = 8


def _conv_pre(xx, w_ref, b_ref, kw, rows):
    acc = w_ref[kw - 1:kw, :] * xx[HALO:HALO + rows]
    for k in range(kw - 1):
        acc = acc + w_ref[k:k + 1, :] * pltpu.roll(xx, kw - 1 - k, 0)[HALO:HALO + rows]
    return acc + b_ref[...]


def _conv_fwd(x, col0, width, w, b, name):
    bsz, s, _ = x.shape
    kw = w.shape[0]
    ts = _row_tile(s, 1024)
    tc = _pick(width, (512, 256, 128))
    assert col0 % tc == 0
    c0 = col0 // tc
    hb = ts // HALO

    def body(x_ref, xp_ref, w_ref, b_ref, o_ref):
        i = pl.program_id(1)
        xp = jnp.where(i > 0, xp_ref[...], 0.0)
        xx = jnp.concatenate([xp, x_ref[...]], axis=0)
        pre = _conv_pre(xx, w_ref, b_ref, kw, ts)
        o_ref[...] = _silu(pre)

    return _call(
        body, name, (bsz, s // ts, width // tc),
        [pl.BlockSpec((None, ts, tc), lambda bb, i, j: (bb, i, c0 + j)),
         pl.BlockSpec((None, HALO, tc), lambda bb, i, j: (bb, jnp.maximum(i * hb - 1, 0), c0 + j)),
         pl.BlockSpec((kw, tc), lambda bb, i, j: (0, j)), pl.BlockSpec((1, tc), lambda bb, i, j: (0, j))],
        pl.BlockSpec((None, ts, tc), lambda bb, i, j: (bb, i, j)), _sds((bsz, s, width), F32),
    )(x, x, w, b)


def _conv_bwd(dy, x, col0, width, w, b, dp, name):
    bsz, s, _ = x.shape
    kw = w.shape[0]
    ts = _row_tile(s, 1024)
    tc = _pick(width, (512, 256, 128))
    c0 = col0 // tc
    hb = ts // HALO
    nt = s // ts
    last_h = s // HALO - 1

    def body(x_ref, xp_ref, xn_ref, dy_ref, dyn_ref, w_ref, b_ref, _, dx_ref, dw_ref, db_ref):
        bb, i = pl.program_id(1), pl.program_id(2)
        xp = jnp.where(i > 0, xp_ref[...], 0.0)
        xx = jnp.concatenate([xp, x_ref[...], xn_ref[...]], axis=0)
        dyy = jnp.concatenate([dy_ref[...], jnp.where(i < nt - 1, dyn_ref[...], 0.0)], axis=0)
        n = ts + HALO
        pre = _conv_pre(xx, w_ref, b_ref, kw, n)
        sg = jax.nn.sigmoid(pre)
        dpre = dyy * (sg * (1.0 + pre * (1.0 - sg)))
        dx = w_ref[kw - 1:kw, :] * dpre[:ts]
        for k in range(kw - 1):
            dx = dx + w_ref[k:k + 1, :] * pltpu.roll(dpre, n - (kw - 1 - k), 0)[:ts]
        dx_ref[...] = dx.astype(BF16)

        @pl.when((bb == 0) & (i == 0))
        def _():
            dw_ref[...] = jnp.zeros_like(dw_ref)
            db_ref[...] = jnp.zeros_like(db_ref)

        dt = dpre[:ts]
        db_ref[...] += jnp.sum(dt, axis=0, keepdims=True)
        dw_ref[kw - 1:kw, :] += jnp.sum(dt * xx[HALO:HALO + ts], axis=0, keepdims=True)
        for k in range(kw - 1):
            xs = pltpu.roll(xx, kw - 1 - k, 0)[HALO:HALO + ts]
            dw_ref[k:k + 1, :] += jnp.sum(dt * xs, axis=0, keepdims=True)

    xspec = lambda f: pl.BlockSpec((None, HALO, tc), f)
    return _call(
        body, name, (width // tc, bsz, nt),
        [pl.BlockSpec((None, ts, tc), lambda j, bb, i: (bb, i, c0 + j)),
         xspec(lambda j, bb, i: (bb, jnp.maximum(i * hb - 1, 0), c0 + j)),
         xspec(lambda j, bb, i: (bb, jnp.minimum((i + 1) * hb, last_h), c0 + j)),
         pl.BlockSpec((None, ts, tc), lambda j, bb, i: (bb, i, j)),
         xspec(lambda j, bb, i: (bb, jnp.minimum((i + 1) * hb, last_h), j)),
         pl.BlockSpec((kw, tc), lambda j, bb, i: (0, j)), pl.BlockSpec((1, tc), lambda j, bb, i: (0, j)), IN_PLACE],
        [pl.BlockSpec((None, ts, tc), lambda j, bb, i: (bb, i, c0 + j)),
         pl.BlockSpec((kw, tc), lambda j, bb, i: (0, j)), pl.BlockSpec((1, tc), lambda j, bb, i: (0, j))],
        [_sds(dp.shape, BF16), _sds((kw, width), F32), _sds((1, width), F32)], aliases={7: 0},
    )(x, x, x, dy, dy, w, b, dp)


def _conv_glu_fwd(x, w, b, name):
    bsz, s, f2 = x.shape
    f = f2 // 2
    kw = w.shape[0]
    ts = _row_tile(s, 2048)
    tc = _pick(f, (256, 128))
    nf = f // tc
    hb = ts // HALO

    def body(xg_ref, xgp_ref, xv_ref, xvp_ref, wg_ref, wv_ref, bg_ref, bv_ref, o_ref):
        i = pl.program_id(1)
        halves = []
        for x_ref, xp_ref, w_ref, b_ref in ((xg_ref, xgp_ref, wg_ref, bg_ref), (xv_ref, xvp_ref, wv_ref, bv_ref)):
            xx = jnp.concatenate([jnp.where(i > 0, xp_ref[...], 0.0), x_ref[...]], axis=0)
            halves.append(_conv_pre(xx, w_ref, b_ref, kw, ts))
        o_ref[...] = (_silu(halves[0]) * halves[1]).astype(BF16)

    tile = lambda off: pl.BlockSpec((None, ts, tc), lambda bb, i, j: (bb, i, off + j))
    prev = lambda off: pl.BlockSpec((None, HALO, tc), lambda bb, i, j: (bb, jnp.maximum(i * hb - 1, 0), off + j))
    wsp = lambda rows, off: pl.BlockSpec((rows, tc), lambda bb, i, j: (0, off + j))
    return _call(
        body, name, (bsz, s // ts, nf),
        [tile(0), prev(0), tile(nf), prev(nf), wsp(kw, 0), wsp(kw, nf), wsp(1, 0), wsp(1, nf)],
        pl.BlockSpec((None, ts, tc), lambda bb, i, j: (bb, i, j)), _sds((bsz, s, f), BF16),
    )(x, x, x, x, w, w, b, b)


def _conv_glu_bwd(da, x, w, b, name):
    bsz, s, f2 = x.shape
    f = f2 // 2
    kw = w.shape[0]
    ts = _row_tile(s, 2048)
    tc = _pick(f, (256, 128))
    nf = f // tc
    hb = ts // HALO
    nt = s // ts
    last_h = s // HALO - 1
    n = ts + HALO

    def body(xg_ref, xgp_ref, xgn_ref, xv_ref, xvp_ref, xvn_ref, da_ref, dan_ref,
             wg_ref, wv_ref, bg_ref, bv_ref, wx_ref, dx_ref, dw_ref, db_ref):
        j, bb, i = pl.program_id(0), pl.program_id(1), pl.program_id(2)
        day = jnp.concatenate([da_ref[...], jnp.where(i < nt - 1, dan_ref[...], 0.0)], axis=0)
        xg = jnp.concatenate([jnp.where(i > 0, xgp_ref[...], 0.0), xg_ref[...], xgn_ref[...]], axis=0)
        pre_g = _conv_pre(xg, wg_ref, bg_ref, kw, n)
        sg = jax.nn.sigmoid(pre_g)

        @pl.when((bb == 0) & (i == 0))
        def _():
            dw_ref[...] = jnp.zeros_like(dw_ref)
            db_ref[...] = jnp.zeros_like(db_ref)

        def finish(dpre, xx):
            dx = wx_ref[kw - 1:kw, :] * dpre[:ts]
            for k in range(kw - 1):
                dx = dx + wx_ref[k:k + 1, :] * pltpu.roll(dpre, n - (kw - 1 - k), 0)[:ts]
            dx_ref[...] = dx.astype(BF16)
            dt = dpre[:ts]
            db_ref[...] += jnp.sum(dt, axis=0, keepdims=True)
            dw_ref[kw - 1:kw, :] += jnp.sum(dt * xx[HALO:HALO + ts], axis=0, keepdims=True)
            for k in range(kw - 1):
                dw_ref[k:k + 1, :] += jnp.sum(dt * pltpu.roll(xx, kw - 1 - k, 0)[HALO:HALO + ts], axis=0, keepdims=True)

        @pl.when(j < nf)
        def _():
            xv = jnp.concatenate([jnp.where(i > 0, xvp_ref[...], 0.0), xv_ref[...], xvn_ref[...]], axis=0)
            pre_v = _conv_pre(xv, wv_ref, bv_ref, kw, n)
            finish(day * pre_v * (sg * (1.0 + pre_g * (1.0 - sg))), xg)

        @pl.when(j >= nf)
        def _():
            xv = jnp.concatenate([jnp.where(i > 0, xvp_ref[...], 0.0), xv_ref[...], xvn_ref[...]], axis=0)
            finish(day * (pre_g * sg), xv)

    tile = lambda off: pl.BlockSpec((None, ts, tc), lambda j, bb, i: (bb, i, off + j % nf))
    prev = lambda off: pl.BlockSpec((None, HALO, tc), lambda j, bb, i: (bb, jnp.maximum(i * hb - 1, 0), off + j % nf))
    nxt = lambda off: pl.BlockSpec((None, HALO, tc), lambda j, bb, i: (bb, jnp.minimum((i + 1) * hb, last_h), off + j % nf))
    wsp = lambda rows, off: pl.BlockSpec((rows, tc), lambda j, bb, i: (0, off + j % nf))
    own = lambda rows: pl.BlockSpec((rows, tc), lambda j, bb, i: (0, j))
    return _call(
        body, name, (2 * nf, bsz, nt),
        [tile(0), prev(0), nxt(0), tile(nf), prev(nf), nxt(nf), tile(0), nxt(0),
         wsp(kw, 0), wsp(kw, nf), wsp(1, 0), wsp(1, nf), own(kw)],
        [pl.BlockSpec((None, ts, tc), lambda j, bb, i: (bb, i, j)), own(kw), own(1)],
        [_sds((bsz, s, f2), BF16), _sds((kw, f2), F32), _sds((1, f2), F32)],
    )(x, x, x, x, x, x, da, da, w, w, b, b, w)


def _merge_fwd(p, oa, ob, oc, wa, wb, wc, name):
    bsz, s, _ = p.shape
    tm = _row_tile(s, 512)
    gblk = P_GATE // (3 * D_MODEL)

    def body(g_ref, oa_ref, ob_ref, oc_ref, wa_ref, wb_ref, wc_ref, o_ref):
        acc = None
        for i, (o_r, w_r) in enumerate(((oa_ref, wa_ref), (ob_ref, wb_ref), (oc_ref, wc_ref))):
            y = _bdot(o_r[...], w_r[...])
            t = jax.nn.sigmoid(g_ref[:, i * D_MODEL:(i + 1) * D_MODEL]) * y
            acc = t if acc is None else acc + t
        o_ref[...] = acc.astype(BF16)

    orow = pl.BlockSpec((None, tm, 512), lambda b, i: (b, i, 0))
    wfull = pl.BlockSpec((512, D_MODEL), lambda b, i: (0, 0))
    return _call(
        body, name, (bsz, s // tm),
        [pl.BlockSpec((None, tm, 3 * D_MODEL), lambda b, i: (b, i, gblk)), orow, orow, orow, wfull, wfull, wfull],
        pl.BlockSpec((None, tm, D_MODEL), lambda b, i: (b, i, 0)), _sds((bsz, s, D_MODEL), BF16),
    )(p, oa, ob, oc, wa, wb, wc)


def _merge_bwd(dm, p, oa, ob, oc, wa, wb, wc, wat, wbt, wct, name):
    bsz, s, _ = p.shape
    tm = _row_tile(s, 512)
    gblk = P_GATE // (3 * D_MODEL)

    def body(dm_ref, g_ref, oa_ref, ob_ref, oc_ref, wa_ref, wb_ref, wc_ref, wat_ref, wbt_ref, wct_ref,
             dg_ref, doa_ref, dob_ref, doc_ref, dya_ref, dyb_ref, dyc_ref):
        dmv = dm_ref[...]
        trip = ((oa_ref, wa_ref, wat_ref, doa_ref, dya_ref), (ob_ref, wb_ref, wbt_ref, dob_ref, dyb_ref),
                (oc_ref, wc_ref, wct_ref, doc_ref, dyc_ref))
        for i, (o_r, w_r, wt_r, do_r, dy_r) in enumerate(trip):
            y = _bdot(o_r[...], w_r[...])
            sg = jax.nn.sigmoid(g_ref[:, i * D_MODEL:(i + 1) * D_MODEL])
            dg_ref[:, i * D_MODEL:(i + 1) * D_MODEL] = (dmv * y * sg * (1.0 - sg)).astype(BF16)
            dy = (dmv * sg).astype(BF16)
            dy_r[...] = dy
            do_r[...] = _bdot(dy, wt_r[...])

    orow = pl.BlockSpec((None, tm, 512), lambda b, i: (b, i, 0))
    drow = pl.BlockSpec((None, tm, D_MODEL), lambda b, i: (b, i, 0))
    grow = pl.BlockSpec((None, tm, 3 * D_MODEL), lambda b, i: (b, i, gblk))
    wfull = pl.BlockSpec((512, D_MODEL), lambda b, i: (0, 0))
    wtfull = pl.BlockSpec((D_MODEL, 512), lambda b, i: (0, 0))
    return _call(
        body, name, (bsz, s // tm),
        [drow, grow, orow, orow, orow, wfull, wfull, wfull, wtfull, wtfull, wtfull],
        [grow, orow, orow, orow, drow, drow, drow],
        [_sds(p.shape, BF16)] + [_sds((bsz, s, 512), F32)] * 3 + [_sds((bsz, s, D_MODEL), BF16)] * 3,
    )(dm, p, oa, ob, oc, wa, wb, wc, wat, wbt, wct)


def _final_loss(x, w, target):
    bsz, s, d = x.shape
    ts = _row_tile(s, 512)

    def body(x_ref, w_ref, t_ref, loss_ref, dx_ref, dw_ref):
        first = (pl.program_id(0) == 0) & (pl.program_id(1) == 0)
        y, vjp = jax.vjp(_rms, x_ref[...], w_ref[...])
        err = y - t_ref[...]
        dx, dw = vjp(err * (1.0 / d))
        dx_ref[...] = dx

        @pl.when(first)
        def _():
            loss_ref[...] = jnp.zeros_like(loss_ref)
            dw_ref[...] = jnp.zeros_like(dw_ref)

        loss_ref[...] += 0.5 * jnp.sum(jnp.sum(err * err, axis=1, keepdims=True), axis=0, keepdims=True) * (1.0 / d)
        dw_ref[...] += dw

    row = pl.BlockSpec((None, ts, d), lambda b, i: (b, i, 0))
    wspec = pl.BlockSpec((1, d), lambda b, i: (0, 0))
    return _call(body, "final_loss", (bsz, s // ts), [row, wspec, row],
                 [pl.BlockSpec((8, LANES), lambda b, i: (0, 0)), row, wspec],
                 [_sds((8, LANES), F32), _sds(x.shape, F32), _sds((1, d), F32)])(x, w, target)


def _unit_lower_inverses(ms):
    n = ms[0].shape[0]
    eye = (_iota((n, n), 0) == _iota((n, n), 1)).astype(F32)
    ps = [-m for m in ms]
    xs = [eye + p for p in ps]
    for _ in range(int(math.log2(n)) - 1):
        ps = [_mxu3(p, p, NN) for p in ps]
        xs = [x + _mxu3(x, p, NN) for x, p in zip(xs, ps)]
    return xs


@jax.custom_vjp
def _known_inverse(m, t):
    return t


_known_inverse.defvjp(lambda m, t: (t, t), lambda t, dt: (-_mxu3(t, _mxu3(dt, t, NT), TN), jnp.zeros_like(t)))


def _gdn_chunk(states, qkv, small, z, a_row, dt_row, nw, tinvs=None):
    nb = len(qkv)
    c = qkv[0].shape[0]
    kw = GDN_HEADS * GDN_DK
    incl, strict = _tril(c), _tril(c, True)
    g_all = [-jnp.exp(a_row) * _softplus(small[b] + dt_row) for b in range(nb)]
    beta_all = [jax.nn.sigmoid(small[b]) for b in range(nb)]
    big_g_all = [_mask_dot(incl.astype(BF16), g_all[b]) for b in range(nb)]
    items = [(b, h) for b in range(nb) for h in range(GDN_HEADS)]
    ids = range(len(items))
    col = lambda b, part, h: qkv[b][:, part * kw + h * GDN_DK:part * kw + (h + 1) * GDN_DK]
    unit = lambda t: t * lax.rsqrt(jnp.sum(t * t, axis=-1, keepdims=True) + EPS)
    q = [unit(col(b, 0, h)) * (GDN_DK ** -0.5) for b, h in items]
    k = [unit(col(b, 1, h)) for b, h in items]
    v = [col(b, 2, h) for b, h in items]
    gc = [_lane_col(big_g_all[b], SM_A + h) for b, h in items]
    bc = [_lane_col(beta_all[b], SM_B + h) for b, h in items]
    g_last = [jnp.sum(_lane_col(g_all[b], SM_A + h), axis=0, keepdims=True) for b, h in items]
    decay = [jnp.where(incl, jnp.exp(jnp.where(incl, gc[i] - _col_to_row(gc[i]), 0.0)), 0.0) for i in ids]
    kb = [k[i] * bc[i] for i in ids]
    m = [jnp.where(strict, _bdot_nt(kb[i], k[i]) * decay[i], 0.0) for i in ids]
    if tinvs is None:
        tinv = _unit_lower_inverses(m)
    else:
        tinv = [_known_inverse(m[i], tinvs[i]) for i in ids]
    eg = [jnp.exp(gc[i]) for i in ids]
    u = [_dot3(tinv[i], v[i] * bc[i]) for i in ids]
    w = [_dot3(tinv[i], kb[i] * eg[i]) for i in ids]
    attn = [_bdot_nt(q[i], k[i]) * decay[i] for i in ids]
    v_new = [u[i] - _bdot(w[i], states[i]) for i in ids]
    o_st = [_bdot(q[i] * eg[i], states[i]) for i in ids]
    o = [o_st[i] + _bdot(attn[i], v_new[i]) for i in ids]
    grow = [_bdot_tn(k[i] * jnp.exp(g_last[i] - gc[i]), v_new[i]) for i in ids]
    new_states = [states[i] * jnp.exp(g_last[i]) + grow[i] for i in ids]
    outs = [_rms(o[i], nw) * _silu(z[b][:, h * GDN_DK:(h + 1) * GDN_DK]) for i, (b, h) in enumerate(items)]
    per_seq = [jnp.concatenate(outs[b * GDN_HEADS:(b + 1) * GDN_HEADS], axis=1) for b in range(nb)]
    return new_states, per_seq, tinv


def _seq_items(bsz, heads):
    return [(b, h) for b in range(bsz) for h in range(heads)]


def _gdn_fwd(qkv_act, p, a_row, dt_row, nw, name):
    bsz, s, _ = qkv_act.shape
    c = GDN_CHUNK
    nc = s // c
    items = _seq_items(bsz, GDN_HEADS)

    def body(qkv_ref, sm_ref, z_ref, a_ref, dt_ref, nw_ref, o_ref, st_ref, ti_ref, st_scr):
        @pl.when(pl.program_id(0) == 0)
        def _():
            st_scr[...] = jnp.zeros_like(st_scr)

        st_ref[...] = st_scr[...]
        seqs = range(bsz)
        new_states, o, tinvs = _gdn_chunk(
            [st_scr[b, h] for b, h in items], [qkv_ref[b] for b in seqs], [sm_ref[b] for b in seqs],
            [z_ref[b] for b in seqs], a_ref[...], dt_ref[...], nw_ref[...])
        for i, (b, h) in enumerate(items):
            st_scr[b, h] = new_states[i]
            ti_ref[b, h] = tinvs[i]
        for b in seqs:
            o_ref[b] = o[b].astype(BF16)

    row = lambda w, blk: pl.BlockSpec((bsz, c, w), lambda n, blk=blk: (0, n, blk))
    prm = pl.BlockSpec((1, LANES), lambda n: (0, 0))
    return _call(
        body, name, (nc,), [row(1536, 0), row(LANES, P_SMALL // LANES), row(512, P_GZ // 512), prm, prm, prm],
        [row(512, 0), pl.BlockSpec((bsz, None, 4, LANES, LANES), lambda n: (0, n, 0, 0, 0)),
         pl.BlockSpec((bsz, None, 4, c, c), lambda n: (0, n, 0, 0, 0))],
        [_sds((bsz, s, 512), BF16), _sds((bsz, nc, 4, LANES, LANES), F32), _sds((bsz, nc, 4, c, c), F32)],
        scratch=[pltpu.VMEM((bsz, 4, LANES, LANES), F32)],
    )(qkv_act, p, p, a_row, dt_row, nw)


def _gdn_bwd(do, qkv_act, p, a_row, dt_row, nw, st_all, ti_all, dp, name):
    bsz, s, _ = qkv_act.shape
    c = GDN_CHUNK
    nc = s // c

    items = _seq_items(bsz, GDN_HEADS)

    def body(qkv_ref, sm_ref, z_ref, a_ref, dt_ref, nw_ref, do_ref, st_ref, ti_ref, _,
             dqkv_ref, dsm_ref, dz_ref, da_ref, ddt_ref, dnw_ref, ds_scr):
        @pl.when(pl.program_id(0) == 0)
        def _():
            ds_scr[...] = jnp.zeros_like(ds_scr)
            da_ref[...] = jnp.zeros_like(da_ref)
            ddt_ref[...] = jnp.zeros_like(ddt_ref)
            dnw_ref[...] = jnp.zeros_like(dnw_ref)

        seqs = range(bsz)
        tinvs = [ti_ref[b, h] for b, h in items]
        chunk = lambda *a: _gdn_chunk(*a, tinvs=tinvs)[:2]
        _, vjp = jax.vjp(chunk, [st_ref[b, h] for b, h in items], [qkv_ref[b] for b in seqs], [sm_ref[b] for b in seqs],
                         [z_ref[b] for b in seqs], a_ref[...], dt_ref[...], nw_ref[...])
        d_states, dqkv, dsm, dz, da, ddt, dnw = vjp(([ds_scr[b, h] for b, h in items], [do_ref[b] for b in seqs]))
        for i, (b, h) in enumerate(items):
            ds_scr[b, h] = d_states[i]
        for b in seqs:
            dqkv_ref[b] = dqkv[b]
            dsm_ref[b] = dsm[b]
            dz_ref[b] = dz[b].astype(BF16)
        da_ref[...] += da
        ddt_ref[...] += ddt
        dnw_ref[...] += dnw

    rrow = lambda w, blk: pl.BlockSpec((bsz, c, w), lambda n, blk=blk: (0, nc - 1 - n, blk))
    prm = pl.BlockSpec((1, LANES), lambda n: (0, 0))
    return _call(
        body, name, (nc,),
        [rrow(1536, 0), rrow(LANES, P_SMALL // LANES), rrow(512, P_GZ // 512), prm, prm, prm, rrow(512, 0),
         pl.BlockSpec((bsz, None, 4, LANES, LANES), lambda n: (0, nc - 1 - n, 0, 0, 0)),
         pl.BlockSpec((bsz, None, 4, c, c), lambda n: (0, nc - 1 - n, 0, 0, 0)), IN_PLACE],
        [rrow(1536, 0), rrow(LANES, 0), rrow(512, P_GZ // 512), prm, prm, prm],
        [_sds((bsz, s, 1536), F32), _sds((bsz, s, LANES), F32), _sds(dp.shape, BF16)] + [_sds((1, LANES), F32)] * 3,
        scratch=[pltpu.VMEM((bsz, 4, LANES, LANES), F32)], aliases={9: 2},
    )(qkv_act, p, p, a_row, dt_row, nw, do, st_all, ti_all, dp)


def _hgrn_block(states, q_raw, f_raw, i_raw, g_raw, lb, nw):
    n = q_raw[0].shape[0]
    c = HGRN_CHUNK
    r, cc = _iota((n, n), 0), _iota((n, n), 1)
    same = (r // c) == (cc // c)
    causal = same & (r >= cc)
    ref_row = (r // c) * c + (c // 2 - 1)
    run_sum = causal.astype(F32)
    rel_sum = run_sum - (same & (ref_row >= cc)).astype(F32)
    sums = jnp.concatenate([run_sum, rel_sum, same.astype(F32)], axis=0).astype(BF16)
    seqs, chunks = range(len(q_raw)), range(n // c)
    items = _seq_items(len(q_raw), HGRN_HEADS)
    hs = lambda t, h: t[:, h * HGRN_DK:(h + 1) * HGRN_DK]
    rows = lambda t, j: t[j * c:(j + 1) * c]
    q = [_silu(q_raw[b]) for b in seqs]
    logf = [jnp.log(lb + (1.0 - lb) * jax.nn.sigmoid(f_raw[b])) for b in seqs]
    k = [(1.0 - lb) * jax.nn.sigmoid(-f_raw[b]) for b in seqs]
    all_sums = [_mask_dot(sums, logf[b]) for b in seqs]
    big_g, g_rel, g_tot = ([t[i * n:(i + 1) * n] for t in all_sums] for i in range(3))
    q_rel = [q[b] * jnp.exp(g_rel[b]) for b in seqs]
    k_rel = [k[b] * jnp.exp(-g_rel[b]) for b in seqs]
    qg = [q[b] * jnp.exp(big_g[b]) for b in seqs]
    k_end = [k[b] * jnp.exp(g_tot[b] - big_g[b]) for b in seqs]
    keep = [[jnp.exp(g_tot[b][j * c:j * c + 1]) for j in chunks] for b in seqs]
    scores = [_bdot_nt(hs(q_rel[b], h), hs(k_rel[b], h)) for b, h in items]
    o_intra = [_bdot(jnp.where(causal, scores[i], 0.0), hs(i_raw[b], h)) for i, (b, h) in enumerate(items)]
    grow = [[_bdot_tn(rows(hs(i_raw[b], h), j), rows(hs(k_end[b], h), j)) for j in chunks] for b, h in items]
    entering, new_states = [], []
    for i, (b, h) in enumerate(items):
        st, per_chunk = states[i], []
        for j in chunks:
            per_chunk.append(st)
            st = st * hs(keep[b][j], h) + grow[i][j]
        entering.append(per_chunk)
        new_states.append(st)
    o_inter = [[_bdot_nt(rows(hs(qg[b], h), j), entering[i][j]) for j in chunks] for i, (b, h) in enumerate(items)]
    outs = [_rms(o_intra[i] + jnp.concatenate(o_inter[i], axis=0), nw) * _silu(hs(g_raw[b], h))
            for i, (b, h) in enumerate(items)]
    return new_states, [jnp.concatenate(outs[b * HGRN_HEADS:(b + 1) * HGRN_HEADS], axis=1) for b in seqs]


def _hgrn_fwd(p, lb, nw, name):
    bsz, s, _ = p.shape
    n = HGRN_BLOCK
    nb = s // n

    items = _seq_items(bsz, HGRN_HEADS)

    def body(q_ref, f_ref, i_ref, g_ref, lb_ref, nw_ref, o_ref, st_ref, st_scr):
        @pl.when(pl.program_id(0) == 0)
        def _():
            st_scr[...] = jnp.zeros_like(st_scr)

        st_ref[...] = st_scr[...]
        per_seq = lambda ref: [ref[b] for b in range(bsz)]
        new_states, o = _hgrn_block([st_scr[b, h] for b, h in items], per_seq(q_ref), per_seq(f_ref), per_seq(i_ref),
                                    per_seq(g_ref), lb_ref[...], nw_ref[...])
        for i, (b, h) in enumerate(items):
            st_scr[b, h] = new_states[i]
        for b in range(bsz):
            o_ref[b] = o[b].astype(BF16)

    row = lambda blk: pl.BlockSpec((bsz, n, 512), lambda i, blk=blk: (0, i, blk))
    return _call(
        body, name, (nb,),
        [row(P_HQ // 512), row(P_HF // 512), row(P_HI // 512), row(P_HG // 512),
         pl.BlockSpec((1, 512), lambda i: (0, 0)), pl.BlockSpec((1, LANES), lambda i: (0, 0))],
        [row(0), pl.BlockSpec((bsz, None, 4, LANES, LANES), lambda i: (0, i, 0, 0, 0))],
        [_sds((bsz, s, 512), BF16), _sds((bsz, nb, 4, LANES, LANES), F32)],
        scratch=[pltpu.VMEM((bsz, 4, LANES, LANES), F32)],
    )(p, p, p, p, lb, nw)


def _hgrn_bwd(do, p, lb, nw, st_all, dp, name):
    bsz, s, _ = p.shape
    n = HGRN_BLOCK
    nb = s // n

    items = _seq_items(bsz, HGRN_HEADS)

    def body(q_ref, f_ref, i_ref, g_ref, lb_ref, nw_ref, do_ref, st_ref, _, dp_ref, dlb_ref, dnw_ref, ds_scr):
        @pl.when(pl.program_id(0) == 0)
        def _():
            ds_scr[...] = jnp.zeros_like(ds_scr)
            dlb_ref[...] = jnp.zeros_like(dlb_ref)
            dnw_ref[...] = jnp.zeros_like(dnw_ref)

        per_seq = lambda ref: [ref[b] for b in range(bsz)]
        _, vjp = jax.vjp(_hgrn_block, [st_ref[b, h] for b, h in items], per_seq(q_ref), per_seq(f_ref), per_seq(i_ref),
                         per_seq(g_ref), lb_ref[...], nw_ref[...])
        d_states, dq, df, di, dg, dlb, dnw = vjp(([ds_scr[b, h] for b, h in items], per_seq(do_ref)))
        for i, (b, h) in enumerate(items):
            ds_scr[b, h] = d_states[i]
        for b in range(bsz):
            for j, t in enumerate((dq, df, di, dg)):
                dp_ref[b, :, j * 512:(j + 1) * 512] = t[b].astype(BF16)
        dlb_ref[...] += dlb
        dnw_ref[...] += dnw

    row = lambda blk: pl.BlockSpec((bsz, n, 512), lambda i, blk=blk: (0, nb - 1 - i, blk))
    return _call(
        body, name, (nb,),
        [row(P_HQ // 512), row(P_HF // 512), row(P_HI // 512), row(P_HG // 512),
         pl.BlockSpec((1, 512), lambda i: (0, 0)), pl.BlockSpec((1, LANES), lambda i: (0, 0)), row(0),
         pl.BlockSpec((bsz, None, 4, LANES, LANES), lambda i: (0, nb - 1 - i, 0, 0, 0)), IN_PLACE],
        [pl.BlockSpec((bsz, n, 2048), lambda i: (0, nb - 1 - i, P_HQ // 2048)),
         pl.BlockSpec((1, 512), lambda i: (0, 0)), pl.BlockSpec((1, LANES), lambda i: (0, 0))],
        [_sds(dp.shape, BF16), _sds((1, 512), F32), _sds((1, LANES), F32)],
        scratch=[pltpu.VMEM((bsz, 4, LANES, LANES), F32)], aliases={8: 0},
    )(p, p, p, p, lb, nw, do, st_all, dp)


def _ssd_chunk(states, xbc, small, z, a_row, dt_row, d_row, nw):
    seqs = range(len(xbc))
    c = xbc[0].shape[0]
    incl = _tril(c)
    spread = (_iota((LANES, SSD_INNER), 0) == SM_DT + _iota((LANES, SSD_INNER), 1) // SSD_HEAD_DIM).astype(BF16)
    dt_all = [_softplus(small[b] + dt_row) for b in seqs]
    both = [_spread_dot(jnp.concatenate([dt_all[b], dt_all[b] * (-jnp.exp(a_row))], axis=0), spread) for b in seqs]
    dt_e, da_e = [t[:c] for t in both], [t[c:] for t in both]
    acs_e = [_mask_dot(incl.astype(BF16), da_e[b]) for b in seqs]
    last_e = [jnp.sum(da_e[b], axis=0, keepdims=True) for b in seqs]
    xs = [xbc[b][:, :SSD_INNER] for b in seqs]
    xdt = [xs[b] * dt_e[b] for b in seqs]
    gw = SSD_GROUPS * SSD_STATE
    lane = _iota((1, LANES), 1)
    items = _seq_items(len(xbc), 4)
    grp = [(b, g) for b in seqs for g in range(SSD_GROUPS)]
    ps = lambda t, j: t[:, j * LANES:(j + 1) * LANES]
    bg = {(b, g): xbc[b][:, SSD_INNER + g * SSD_STATE:SSD_INNER + (g + 1) * SSD_STATE] for b, g in grp}
    cg = {(b, g): xbc[b][:, SSD_INNER + gw + g * SSD_STATE:SSD_INNER + gw + (g + 1) * SSD_STATE] for b, g in grp}
    cb = {bgk: _bdot_nt(cg[bgk], bg[bgk]) for bgk in grp}

    def seg(b, j, sub):
        ac = ps(acs_e[b], j)[:, sub * SSD_HEAD_DIM:sub * SSD_HEAD_DIM + 1]
        return jnp.where(incl, jnp.exp(jnp.where(incl, ac - _col_to_row(ac), 0.0)), 0.0)

    mine = [((lane // SSD_HEAD_DIM) == sub).astype(F32) for sub in range(2)]
    y_in = [[_bdot(cb[b, j // 2] * seg(b, j, sub), ps(xdt[b], j) * mine[sub]) for sub in range(2)] for b, j in items]
    y_st = [_bdot(cg[b, j // 2], states[i]) for i, (b, j) in enumerate(items)]
    grow = [_bdot_tn(bg[b, j // 2], ps(xdt[b], j) * jnp.exp(ps(last_e[b], j) - ps(acs_e[b], j))) for b, j in items]
    new_states = [states[i] * jnp.exp(ps(last_e[b], j)) + grow[i] for i, (b, j) in enumerate(items)]
    ys = [y_in[i][0] + y_in[i][1] + y_st[i] * jnp.exp(ps(acs_e[b], j)) + ps(d_row, j) * ps(xs[b], j)
          for i, (b, j) in enumerate(items)]
    gwid = SSD_INNER // SSD_GROUPS
    outs = []
    for b in seqs:
        yz = jnp.concatenate(ys[4 * b:4 * b + 4], axis=1) * _silu(z[b])
        outs.append(jnp.concatenate(
            [_rms(yz[:, g * gwid:(g + 1) * gwid], nw[:, g * gwid:(g + 1) * gwid]) for g in range(SSD_GROUPS)], axis=1))
    return new_states, outs


def _ssd_fwd(xbc_act, p, a_row, dt_row, d_row, nw, name):
    bsz, s, _ = xbc_act.shape
    c = SSD_CHUNK
    nc = s // c

    items = _seq_items(bsz, 4)

    def body(x_ref, sm_ref, z_ref, a_ref, dt_ref, d_ref, nw_ref, o_ref, st_ref, st_scr):
        @pl.when(pl.program_id(0) == 0)
        def _():
            st_scr[...] = jnp.zeros_like(st_scr)

        st_ref[...] = st_scr[...]
        per_seq = lambda ref: [ref[b] for b in range(bsz)]
        new_states, o = _ssd_chunk([st_scr[b, j] for b, j in items], per_seq(x_ref), per_seq(sm_ref), per_seq(z_ref),
                                   a_ref[...], dt_ref[...], d_ref[...], nw_ref[...])
        for i, (b, j) in enumerate(items):
            st_scr[b, j] = new_states[i]
        for b in range(bsz):
            o_ref[b] = o[b].astype(BF16)

    row = lambda w, blk: pl.BlockSpec((bsz, c, w), lambda n, blk=blk: (0, n, blk))
    prm = pl.BlockSpec((1, LANES), lambda n: (0, 0))
    prm5 = pl.BlockSpec((1, 512), lambda n: (0, 0))
    return _call(
        body, name, (nc,),
        [row(1024, 0), row(LANES, P_SMALL // LANES), row(512, P_SZ // 512), prm, prm, prm5, prm5],
        [row(512, 0), pl.BlockSpec((bsz, None, 4, LANES, LANES), lambda n: (0, n, 0, 0, 0))],
        [_sds((bsz, s, 512), BF16), _sds((bsz, nc, 4, LANES, LANES), F32)],
        scratch=[pltpu.VMEM((bsz, 4, LANES, LANES), F32)],
    )(xbc_act, p, p, a_row, dt_row, d_row, nw)


def _ssd_bwd(do, xbc_act, p, a_row, dt_row, d_row, nw, st_all, dp, name):
    bsz, s, _ = xbc_act.shape
    c = SSD_CHUNK
    nc = s // c

    items = _seq_items(bsz, 4)

    def body(x_ref, sm_ref, z_ref, a_ref, dt_ref, d_ref, nw_ref, do_ref, st_ref, _,
             dx_ref, dsm_ref, dz_ref, da_ref, ddt_ref, dd_ref, dnw_ref, ds_scr):
        @pl.when(pl.program_id(0) == 0)
        def _():
            ds_scr[...] = jnp.zeros_like(ds_scr)
            da_ref[...] = jnp.zeros_like(da_ref)
            ddt_ref[...] = jnp.zeros_like(ddt_ref)
            dd_ref[...] = jnp.zeros_like(dd_ref)
            dnw_ref[...] = jnp.zeros_like(dnw_ref)

        per_seq = lambda ref: [ref[b] for b in range(bsz)]
        _, vjp = jax.vjp(_ssd_chunk, [st_ref[b, j] for b, j in items], per_seq(x_ref), per_seq(sm_ref), per_seq(z_ref),
                         a_ref[...], dt_ref[...], d_ref[...], nw_ref[...])
        d_states, dx, dsm, dz, da, ddt, dd, dnw = vjp(([ds_scr[b, j] for b, j in items], per_seq(do_ref)))
        for i, (b, j) in enumerate(items):
            ds_scr[b, j] = d_states[i]
        for b in range(bsz):
            dx_ref[b] = dx[b]
            dsm_ref[b] = dsm[b]
            dz_ref[b] = dz[b].astype(BF16)
        da_ref[...] += da
        ddt_ref[...] += ddt
        dd_ref[...] += dd
        dnw_ref[...] += dnw

    row = lambda w, blk: pl.BlockSpec((bsz, c, w), lambda n, blk=blk: (0, nc - 1 - n, blk))
    prm = pl.BlockSpec((1, LANES), lambda n: (0, 0))
    prm5 = pl.BlockSpec((1, 512), lambda n: (0, 0))
    return _call(
        body, name, (nc,),
        [row(1024, 0), row(LANES, P_SMALL // LANES), row(512, P_SZ // 512), prm, prm, prm5, prm5, row(512, 0),
         pl.BlockSpec((bsz, None, 4, LANES, LANES), lambda n: (0, nc - 1 - n, 0, 0, 0)), IN_PLACE],
        [row(1024, 0), row(LANES, 0), row(512, P_SZ // 512), prm, prm, prm5, prm5],
        [_sds((bsz, s, 1024), F32), _sds((bsz, s, LANES), F32), _sds(dp.shape, BF16),
         _sds((1, LANES), F32), _sds((1, LANES), F32), _sds((1, 512), F32), _sds((1, 512), F32)],
        scratch=[pltpu.VMEM((bsz, 4, LANES, LANES), F32)], aliases={9: 2},
    )(xbc_act, p, p, a_row, dt_row, d_row, nw, do, st_all, dp)


def _small_cols(dsm_a, dsm_c, dp, name):
    bsz, s, _ = dsm_a.shape
    ts = _row_tile(s, 1024)
    width = P_WIDTH - P_SMALL

    def body(a_ref, c_ref, _, o_ref):
        o_ref[:, :LANES] = (a_ref[...] + c_ref[...]).astype(BF16)
        o_ref[:, LANES:] = jnp.zeros((ts, width - LANES), BF16)

    row = pl.BlockSpec((None, ts, LANES), lambda b, i: (b, i, 0))
    return _call(body, name, (bsz, s // ts), [row, row, IN_PLACE],
                 pl.BlockSpec((None, ts, width), lambda b, i: (b, i, P_SMALL // width)), _sds(dp.shape, BF16),
                 aliases={2: 0})(dsm_a, dsm_c, dp)


def _peer(k):
    x, y, c = lax.axis_index("x"), lax.axis_index("y"), lax.axis_index("c")
    px = 1 - x if k & 4 else x
    py = 1 - y if k & 2 else y
    pc = 1 - c if k & 1 else c
    return (px, py, pc), 4 * px + 2 * py + pc


def _my_index():
    return 4 * lax.axis_index("x") + 2 * lax.axis_index("y") + lax.axis_index("c")


def _mesh_place():
    x, y, c = lax.axis_index("x"), lax.axis_index("y"), lax.axis_index("c")
    return (x, y, c), (x, y, 1 - c), [(1 - x, y), (x, 1 - y), (1 - x, 1 - y)]


def _run_exchange(body, name, arrays, out_shape, n_sems):
    n = len(arrays)
    any_spec = pl.BlockSpec(memory_space=pl.ANY)
    return pl.pallas_call(
        body, name=name, out_shape=out_shape, in_specs=[any_spec] * n, out_specs=[any_spec] * n,
        scratch_shapes=[pltpu.SemaphoreType.DMA((n_sems, n)), pltpu.SemaphoreType.DMA((n_sems, n)),
                        pltpu.SemaphoreType.DMA((n,))],
    )(*arrays)


def _all_to_all(arrays, name):
    n = len(arrays)

    def body(*refs):
        ins, outs = refs[:n], refs[n:2 * n]
        send_sems, recv_sems, local_sems = refs[2 * n:]
        me = _my_index()

        def copy(i, k, arriving):
            peer, slot = _peer(k)
            return pltpu.make_async_remote_copy(
                src_ref=ins[i].at[slot], dst_ref=outs[i].at[slot if arriving else me], send_sem=send_sems.at[k - 1, i],
                recv_sem=recv_sems.at[k - 1, i], device_id=peer, device_id_type=MESH_ID)

        mine = [pltpu.make_async_copy(ins[i].at[me], outs[i].at[me], local_sems.at[i]) for i in range(n)]
        sends = [copy(i, k, False) for k in range(1, N_DEV) for i in range(n)]
        for cp in mine + sends:
            cp.start()
        for k in range(1, N_DEV):
            for i in range(n):
                copy(i, k, True).wait_recv()
        for cp in sends:
            cp.wait_send()
        for cp in mine:
            cp.wait()

    return _run_exchange(body, name, arrays, [_sds(a.shape, a.dtype) for a in arrays], N_DEV - 1)


def _gather_two_level(arrays, name):
    n = len(arrays)

    def body(*refs):
        ins, outs = refs[:n], refs[n:2 * n]
        send_sems, recv_sems, local_sems = refs[2 * n:]
        (x, y, c), sibling, chips = _mesh_place()
        slot = lambda px, py, pc: 4 * px + 2 * py + pc

        def copy(i, k, block, to, src=None):
            return pltpu.make_async_remote_copy(
                src_ref=outs[i].at[block] if src is None else src, dst_ref=outs[i].at[block],
                send_sem=send_sems.at[k, i], recv_sem=recv_sems.at[k, i], device_id=to, device_id_type=MESH_ID)

        me = slot(x, y, c)
        mine = [pltpu.make_async_copy(ins[i], outs[i].at[me], local_sems.at[i]) for i in range(n)]
        first = [copy(i, 0, me, sibling, src=ins[i]) for i in range(n)]
        first += [copy(i, 1 + j, me, (*chip, c), src=ins[i]) for j, chip in enumerate(chips) for i in range(n)]
        for cp in mine + first:
            cp.start()
        passed = []
        for j, chip in enumerate(chips):
            for i in range(n):
                copy(i, 1 + j, slot(*chip, c), (x, y, c)).wait_recv()
                cp = copy(i, 4 + j, slot(*chip, c), sibling)
                cp.start()
                passed.append(cp)
        for i in range(n):
            copy(i, 0, slot(x, y, 1 - c), (x, y, c)).wait_recv()
        for j, chip in enumerate(chips):
            for i in range(n):
                copy(i, 4 + j, slot(*chip, 1 - c), (x, y, c)).wait_recv()
        for cp in first + passed:
            cp.wait_send()
        for cp in mine:
            cp.wait()

    out_shape = [_sds((N_DEV,) + a.shape, a.dtype) for a in arrays]
    return _run_exchange(body, name, arrays, out_shape, 7)


def _sibling_swap(arrays, name):
    n = len(arrays)

    def body(*refs):
        ins, outs = refs[:n], refs[n:2 * n]
        send_sems, recv_sems, _ = refs[2 * n:]
        (x, y, c), sibling, _ = _mesh_place()
        copies = [pltpu.make_async_remote_copy(
            src_ref=ins[i].at[1 - c], dst_ref=outs[i], send_sem=send_sems.at[0, i], recv_sem=recv_sems.at[0, i],
            device_id=sibling, device_id_type=MESH_ID) for i in range(n)]
        for cp in copies:
            cp.start()
        for cp in copies:
            cp.wait()

    out_shape = [_sds(a.shape[1:], a.dtype) for a in arrays]
    return _run_exchange(body, name, arrays, out_shape, 1)


def _chip_scatter(arrays, name):
    n = len(arrays)

    def body(*refs):
        ins, outs = refs[:n], refs[n:2 * n]
        send_sems, recv_sems, local_sems = refs[2 * n:]
        (x, y, c), _, chips = _mesh_place()
        me = 2 * x + y
        mine = [pltpu.make_async_copy(ins[i].at[me], outs[i].at[me], local_sems.at[i]) for i in range(n)]
        sends = [pltpu.make_async_remote_copy(
            src_ref=ins[i].at[2 * chip[0] + chip[1]], dst_ref=outs[i].at[me], send_sem=send_sems.at[j, i],
            recv_sem=recv_sems.at[j, i], device_id=(*chip, c), device_id_type=MESH_ID)
            for j, chip in enumerate(chips) for i in range(n)]
        for cp in mine + sends:
            cp.start()
        for j, chip in enumerate(chips):
            for i in range(n):
                pltpu.make_async_remote_copy(
                    src_ref=ins[i].at[me], dst_ref=outs[i].at[2 * chip[0] + chip[1]], send_sem=send_sems.at[j, i],
                    recv_sem=recv_sems.at[j, i], device_id=(*chip, c), device_id_type=MESH_ID).wait_recv()
        for cp in sends:
            cp.wait_send()
        for cp in mine:
            cp.wait()

    out_shape = [_sds(a.shape, a.dtype) for a in arrays]
    return _run_exchange(body, name, arrays, out_shape, 3)


def _pair_sum(a, b, name):
    lead, rows, width = a.shape
    tr = _pick(rows, (256, 128, 64, 32, 16, 8)) if rows % 8 == 0 else rows

    def body(a_ref, b_ref, o_ref):
        o_ref[...] = (a_ref[...].astype(F32) + b_ref[...].astype(F32)).astype(o_ref.dtype)

    blk = pl.BlockSpec((None, tr, width), lambda l, i: (l, i, 0))
    return _call(body, name, (lead, rows // tr), [blk, blk], blk, _sds(a.shape, a.dtype))(a, b)


def _sum_adamw(gs, w, m, v, name):
    lead, rows, width = w.shape
    slots = gs.shape[0]
    tr = _pick(rows, (128, 64, 32, 16, 8)) if rows % 8 == 0 else rows

    def body(g_ref, w_ref, m_ref, v_ref, go_ref, d_ref, mo_ref, vo_ref):
        g = g_ref[0].astype(F32)
        for i in range(1, slots):
            g = g + g_ref[i].astype(F32)
        m2 = ADAM_B1 * m_ref[...] + (1.0 - ADAM_B1) * g
        v2 = ADAM_B2 * v_ref[...] + (1.0 - ADAM_B2) * (g * g)
        m_hat = m2 / (1.0 - ADAM_B1 ** ADAM_STEP)
        v_hat = v2 / (1.0 - ADAM_B2 ** ADAM_STEP)
        go_ref[...] = g
        d_ref[...] = -ADAM_LR * (m_hat / (jnp.sqrt(v_hat) + ADAM_EPS) + ADAM_WD * w_ref[...])
        mo_ref[...] = m2
        vo_ref[...] = v2

    blk = pl.BlockSpec((None, tr, width), lambda l, i: (l, i, 0))
    return _call(body, name, (lead, rows // tr),
                 [pl.BlockSpec((slots, None, tr, width), lambda l, i: (0, l, i, 0)), blk, blk, blk],
                 [blk] * 4, [_sds(w.shape, F32)] * 4)(gs, w, m, v)


MATMUL_WEIGHTS = ("w_in", "w_br_a", "w_br_b", "w_br_c", "w_out", "ffn_w_up", "ffn_w_down")
UNALIGNED = ("w_in", "ffn_w_up")
SPLIT = (
    ("w_in", (DEPTH, D_MODEL, 8720), 2),
    ("gdn_conv_w", (DEPTH, 4, 1536), 2), ("ssd_conv_w", (DEPTH, 4, 1024), 2),
    ("w_br_a", (DEPTH, 512, D_MODEL), 2), ("w_br_b", (DEPTH, 512, D_MODEL), 2), ("w_br_c", (DEPTH, 512, D_MODEL), 2),
    ("w_out", (DEPTH, D_MODEL, D_MODEL), 1), ("ffn_w_up", (DEPTH, D_MODEL, 2 * FFN_HIDDEN), 2),
    ("ffn_conv_w", (DEPTH, 3, 2 * FFN_HIDDEN), 2), ("ffn_w_down", (DEPTH, FFN_HIDDEN, D_MODEL), 1),
)
REPL = (
    ("b_ada", (DEPTH, 6 * D_MODEL)), ("norm1_w", (DEPTH, D_MODEL)), ("gdn_a_log", (DEPTH, 4)),
    ("gdn_dt_bias", (DEPTH, 4)), ("gdn_norm_w", (DEPTH, 128)), ("hgrn_lb_param", (DEPTH, 512)),
    ("hgrn_norm_w", (DEPTH, 128)), ("ssd_conv_b", (DEPTH, 1024)), ("ssd_a_log", (DEPTH, 8)),
    ("ssd_dt_bias", (DEPTH, 8)), ("ssd_d", (DEPTH, 8)), ("ssd_norm_w", (DEPTH, 512)), ("norm2_w", (DEPTH, D_MODEL)),
    ("ffn_conv_b", (DEPTH, 2 * FFN_HIDDEN)), ("final_norm_w", (D_MODEL,)),
)
WEIGHTS = ("w_ada", "b_ada", "norm1_w", "w_in", "gdn_conv_w", "gdn_a_log", "gdn_dt_bias", "gdn_norm_w",
           "hgrn_lb_param", "hgrn_norm_w", "ssd_conv_w", "ssd_conv_b", "ssd_a_log", "ssd_dt_bias", "ssd_d",
           "ssd_norm_w", "w_br_a", "w_br_b", "w_br_c", "w_out", "norm2_w", "ffn_w_up", "ffn_conv_w", "ffn_conv_b",
           "ffn_w_down", "final_norm_w")


def _block_shape(shape, axis):
    return tuple(d // N_DEV if i == axis else d for i, d in enumerate(shape))


def _join_blocks(gathered, shape, axis):
    return jnp.moveaxis(gathered, 0, axis).reshape(shape)


def _split_blocks(full, shape, axis):
    bs = _block_shape(shape, axis)
    t = full.reshape(shape[:axis] + (N_DEV, bs[axis]) + shape[axis + 1:])
    return jnp.moveaxis(t, axis, 0)


def _pack_repl(vals):
    parts = []
    for n, shape in REPL:
        size = math.prod(shape)
        parts.append(jnp.pad(vals[n].reshape(-1), (0, -(-size // PACK_W) * PACK_W - size)))
    cat = jnp.concatenate(parts)
    rows = -(-cat.shape[0] // (8 * PACK_W)) * 8
    return jnp.pad(cat, (0, rows * PACK_W - cat.shape[0])).reshape(rows, PACK_W)


def _unpack_repl(packed):
    flat, out, off = packed.reshape(-1), {}, 0
    for n, shape in REPL:
        size = math.prod(shape)
        out[n] = flat[off:off + size].reshape(shape)
        off += -(-size // PACK_W) * PACK_W
    return out


def _lane_row(vec, lane0):
    return jnp.pad(vec, (lane0, LANES - lane0 - vec.shape[0]))[None]


def _arrange_w_in(w):
    offs = [0]
    for sz in W_IN_SPLITS:
        offs.append(offs[-1] + sz)
    qkv, a, b, gz, hq, hf, hi, hg, sz_, xbc, dt, gate = [w[:, offs[i]:offs[i + 1]] for i in range(12)]
    pad = jnp.zeros((w.shape[0], P_WIDTH - P_SMALL - 16), w.dtype)
    return jnp.concatenate([qkv, gz, xbc, gate, hq, hf, hi, hg, sz_, a, b, dt, pad], axis=1)


def _restore_w_in(wp):
    cut = lambda o, n: wp[:, o:o + n]
    return jnp.concatenate([
        cut(P_QKV, 1536), cut(P_SMALL + SM_A, 4), cut(P_SMALL + SM_B, 4), cut(P_GZ, 512), cut(P_HQ, 512),
        cut(P_HF, 512), cut(P_HI, 512), cut(P_HG, 512), cut(P_SZ, 512), cut(P_XBC, 1024), cut(P_SMALL + SM_DT, 8),
        cut(P_GATE, 3072)], axis=1)


def _join_cols(gathered, arrange, name):
    _, depth, rows, cols = gathered.shape
    tr = _pick(rows, (256, 128, 64, 32, 16, 8))
    width = P_WIDTH if arrange else N_DEV * cols

    def body(g_ref, o_ref):
        row = jnp.concatenate([g_ref[d] for d in range(N_DEV)], axis=1)
        o_ref[...] = _arrange_w_in(row) if arrange else row

    return _call(body, name, (depth, rows // tr),
                 [pl.BlockSpec((N_DEV, None, tr, cols), lambda l, i: (0, l, i, 0))],
                 pl.BlockSpec((None, tr, width), lambda l, i: (l, i, 0)), _sds((depth, rows, width), gathered.dtype),
                 )(gathered)


def _split_cols(per_layer, restore, cols, name):
    depth = len(per_layer)
    rows = per_layer[0].shape[0]
    tr = _pick(rows, (256, 128, 64, 32, 16, 8))
    nt = rows // tr

    def body(*refs):
        o_ref = refs[depth]
        for l in range(depth):
            @pl.when(pl.program_id(0) == l)
            def _(l=l):
                row = _restore_w_in(refs[l][...]) if restore else refs[l][...]
                for d in range(N_DEV):
                    o_ref[d % 2, d // 2] = row[:, d * cols:(d + 1) * cols]

    return _call(body, name, (depth, nt),
                 [pl.BlockSpec((tr, a.shape[1]), lambda l, i: (i, 0)) for a in per_layer],
                 pl.BlockSpec((2, N_DEV // 2, tr, cols), lambda l, i: (0, 0, l * nt + i, 0)),
                 _sds((2, N_DEV // 2, depth * rows, cols), per_layer[0].dtype))(*per_layer)


def _layer_consts(l, wf, wr, lower):
    t = lambda a: a.T
    k = {}
    k["n1w"], k["n2w"] = wr["norm1_w"][l][None], wr["norm2_w"][l][None]
    k["win"], k["win_t"] = wf["w_in"][l], t(wf["w_in"][l])
    for n in ("w_br_a", "w_br_b", "w_br_c", "w_out", "ffn_w_up", "ffn_w_down"):
        k[n], k[n + "_t"] = wf[n][l], t(wf[n][l])
    k["gdn_conv_w"], k["gdn_conv_b"] = wf["gdn_conv_w"][l], jnp.zeros((1, 1536), F32)
    k["ssd_conv_w"], k["ssd_conv_b"] = wf["ssd_conv_w"][l], wr["ssd_conv_b"][l][None]
    k["ffn_conv_w"], k["ffn_conv_b"] = wf["ffn_conv_w"][l], wr["ffn_conv_b"][l][None]
    k["gdn_a"], k["gdn_dt"] = _lane_row(wr["gdn_a_log"][l], SM_A), _lane_row(wr["gdn_dt_bias"][l], SM_A)
    k["gdn_nw"], k["hgrn_nw"] = wr["gdn_norm_w"][l][None], wr["hgrn_norm_w"][l][None]
    k["ssd_a"], k["ssd_dt"] = _lane_row(wr["ssd_a_log"][l], SM_DT), _lane_row(wr["ssd_dt_bias"][l], SM_DT)
    k["ssd_d"] = jnp.repeat(wr["ssd_d"][l], SSD_HEAD_DIM)[None]
    k["ssd_nw"] = wr["ssd_norm_w"][l][None]
    k["lb"] = lower[l:l + 1]
    return k


def _layer_fwd(l, x, mod, k):
    bsz, s, d = x.shape
    t = bsz * s
    sv = {"x": x}
    sv["mod"] = [mod[:, None, i * d:(i + 1) * d] for i in range(6)]
    sh1, sc1, g1, sh2, sc2, g2 = sv["mod"]
    h1 = _norm_mod_fwd(x, k["n1w"], sh1, sc1, f"norm1_fwd{l}")
    p = _mm(h1.reshape(t, d), k["win"], F32, f"mm_in{l}").reshape(bsz, s, P_WIDTH)
    qkv_act = _conv_fwd(p, P_QKV, 1536, k["gdn_conv_w"], k["gdn_conv_b"], f"gdn_conv_fwd{l}")
    oa, st_a, ti_a = _gdn_fwd(qkv_act, p, k["gdn_a"], k["gdn_dt"], k["gdn_nw"], f"gdn_fwd{l}")
    ob, st_b = _hgrn_fwd(p, k["lb"], k["hgrn_nw"], f"hgrn_fwd{l}")
    xbc_act = _conv_fwd(p, P_XBC, 1024, k["ssd_conv_w"], k["ssd_conv_b"], f"ssd_conv_fwd{l}")
    oc, st_c = _ssd_fwd(xbc_act, p, k["ssd_a"], k["ssd_dt"], k["ssd_d"], k["ssd_nw"], f"ssd_fwd{l}")
    merged = _merge_fwd(p, oa, ob, oc, k["w_br_a"], k["w_br_b"], k["w_br_c"], f"merge_fwd{l}")
    mix = _mm(merged.reshape(t, d), k["w_out"], F32, f"mm_out{l}").reshape(bsz, s, d)
    x1 = _resid_fwd(x, mix, g1, f"resid1_fwd{l}")
    h2 = _norm_mod_fwd(x1, k["n2w"], sh2, sc2, f"norm2_fwd{l}")
    u_pre = _mm(h2.reshape(t, d), k["ffn_w_up"], F32, f"mm_up{l}").reshape(bsz, s, 2 * FFN_HIDDEN)
    a = _conv_glu_fwd(u_pre, k["ffn_conv_w"], k["ffn_conv_b"], f"ffn_conv_glu_fwd{l}")
    ffn = _mm(a.reshape(t, FFN_HIDDEN), k["ffn_w_down"], F32, f"mm_down{l}").reshape(bsz, s, d)
    x2 = _resid_fwd(x1, ffn, g2, f"resid2_fwd{l}")
    sv.update(h1=h1, p=p, qkv_act=qkv_act, oa=oa, st_a=st_a, ti_a=ti_a, ob=ob, st_b=st_b, xbc_act=xbc_act, oc=oc, st_c=st_c,
              merged=merged, mix=mix, x1=x1, h2=h2, u_pre=u_pre, a=a, ffn=ffn)
    return x2, sv


def _layer_bwd(l, dx2, k, sv):
    bsz, s, d = dx2.shape
    t = bsz * s
    f2 = 2 * FFN_HIDDEN
    sh1, sc1, g1, sh2, sc2, g2 = sv["mod"]
    tr = lambda a: a.reshape(t, -1).T
    g = {}
    dffn, dg2 = _gate_bwd(dx2, sv["ffn"], g2, f"gate2_bwd{l}")
    dffn2 = dffn.reshape(t, d)
    da = _mm(dffn2, k["ffn_w_down_t"], F32, f"mm_down_dx{l}").reshape(bsz, s, FFN_HIDDEN)
    g["ffn_w_down"] = _mm(tr(sv["a"]), dffn2, BF16, f"mm_down_dw{l}")
    du_pre, g["ffn_conv_w"], dfcb = _conv_glu_bwd(da, sv["u_pre"], k["ffn_conv_w"], k["ffn_conv_b"], f"ffn_conv_glu_bwd{l}")
    g["ffn_conv_b"] = dfcb[0]
    du2 = du_pre.reshape(t, f2)
    dh2 = _mm(du2, k["ffn_w_up_t"], F32, f"mm_up_dx{l}").reshape(bsz, s, d)
    g["ffn_w_up"] = _mm(tr(sv["h2"]), du2, BF16, f"mm_up_dw{l}")
    dx1, dn2w, dsh2, dsc2 = _norm_mod_bwd(sv["x1"], k["n2w"], sh2, sc2, dh2, dx2, f"norm2_bwd{l}")
    g["norm2_w"] = dn2w[0]
    dmix, dg1 = _gate_bwd(dx1, sv["mix"], g1, f"gate1_bwd{l}")
    dmix2 = dmix.reshape(t, d)
    dmerged = _mm(dmix2, k["w_out_t"], F32, f"mm_out_dx{l}").reshape(bsz, s, d)
    g["w_out"] = _mm(tr(sv["merged"]), dmix2, BF16, f"mm_out_dw{l}")
    p = sv["p"]
    dp, doa, dob, doc, dya, dyb, dyc = _merge_bwd(
        dmerged, p, sv["oa"], sv["ob"], sv["oc"], k["w_br_a"], k["w_br_b"], k["w_br_c"],
        k["w_br_a_t"], k["w_br_b_t"], k["w_br_c_t"], f"merge_bwd{l}")
    g["w_br_a"] = _mm(tr(sv["oa"]), dya.reshape(t, d), BF16, f"mm_bra_dw{l}")
    g["w_br_b"] = _mm(tr(sv["ob"]), dyb.reshape(t, d), BF16, f"mm_brb_dw{l}")
    g["w_br_c"] = _mm(tr(sv["oc"]), dyc.reshape(t, d), BF16, f"mm_brc_dw{l}")
    dxbc_act, dsm_c, dp, da_c, ddt_c, dd_c, dnw_c = _ssd_bwd(
        doc, sv["xbc_act"], p, k["ssd_a"], k["ssd_dt"], k["ssd_d"], k["ssd_nw"], sv["st_c"], dp, f"ssd_bwd{l}")
    dp, g["ssd_conv_w"], dscb = _conv_bwd(dxbc_act, p, P_XBC, 1024, k["ssd_conv_w"], k["ssd_conv_b"], dp, f"ssd_conv_bwd{l}")
    g["ssd_conv_b"] = dscb[0]
    g["ssd_a_log"], g["ssd_dt_bias"] = da_c[0, SM_DT:SM_DT + 8], ddt_c[0, SM_DT:SM_DT + 8]
    g["ssd_d"] = dd_c.reshape(SSD_HEADS, SSD_HEAD_DIM).sum(axis=1)
    g["ssd_norm_w"] = dnw_c[0]
    dp, dlb, dnw_b = _hgrn_bwd(dob, p, k["lb"], k["hgrn_nw"], sv["st_b"], dp, f"hgrn_bwd{l}")
    g["hgrn_norm_w"] = dnw_b[0]
    dqkv_act, dsm_a, dp, da_a, ddt_a, dnw_a = _gdn_bwd(
        doa, sv["qkv_act"], p, k["gdn_a"], k["gdn_dt"], k["gdn_nw"], sv["st_a"], sv["ti_a"], dp, f"gdn_bwd{l}")
    dp, g["gdn_conv_w"], _ = _conv_bwd(dqkv_act, p, P_QKV, 1536, k["gdn_conv_w"], k["gdn_conv_b"], dp, f"gdn_conv_bwd{l}")
    g["gdn_a_log"], g["gdn_dt_bias"], g["gdn_norm_w"] = da_a[0, :4], ddt_a[0, :4], dnw_a[0]
    dp = _small_cols(dsm_a, dsm_c, dp, f"small_cols{l}").reshape(t, P_WIDTH)
    dh1 = _mm(dp, k["win_t"], F32, f"mm_in_dx{l}").reshape(bsz, s, d)
    g["w_in"] = _mm(tr(sv["h1"]), dp, BF16, f"mm_in_dw{l}")
    dx, dn1w, dsh1, dsc1 = _norm_mod_bwd(sv["x"], k["n1w"], sh1, sc1, dh1, dx1, f"norm1_bwd{l}")
    g["norm1_w"] = dn1w[0]
    dmod = jnp.concatenate([dsh1, dsc1, dg1, dsh2, dsc2, dg2], axis=-1)[:, 0]
    return dx, g, dlb, dmod


def _local_step(x, mod, wf, wr, target):
    lower = _lb_fwd(wr["hgrn_lb_param"])
    ks = [_layer_consts(l, wf, wr, lower) for l in range(DEPTH)]
    saved = []
    h = x
    for l in range(DEPTH):
        h, sv = _layer_fwd(l, h, mod[l], ks[l])
        saved.append(sv)
    loss8, dh, dfnw = _final_loss(h, wr["final_norm_w"][None], target)
    per_layer, dlbs, dmods = [None] * DEPTH, [None] * DEPTH, [None] * DEPTH
    for l in reversed(range(DEPTH)):
        dh, per_layer[l], dlbs[l], dmods[l] = _layer_bwd(l, dh, ks[l], saved[l])
    grads = {n: [per_layer[l][n] for l in range(DEPTH)] for n in per_layer[0]}
    grads = {n: g if n in UNALIGNED else jnp.stack(g) for n, g in grads.items()}
    grads["hgrn_lb_param"] = _lb_bwd(wr["hgrn_lb_param"], jnp.concatenate(dlbs, axis=0))
    grads["final_norm_w"] = dfnw[0]
    return loss8[0, 0], dh, grads, jnp.stack(dmods)


def kernel(x, c, w_ada, b_ada, norm1_w, w_in, gdn_conv_w, gdn_a_log, gdn_dt_bias, gdn_norm_w, hgrn_lb_param, hgrn_norm_w, ssd_conv_w, ssd_conv_b, ssd_a_log, ssd_dt_bias, ssd_d, ssd_norm_w, w_br_a, w_br_b, w_br_c, w_out, norm2_w, ffn_w_up, ffn_conv_w, ffn_conv_b, ffn_w_down, final_norm_w, loss_target, m_w_ada, m_b_ada, m_norm1_w, m_w_in, m_gdn_conv_w, m_gdn_a_log, m_gdn_dt_bias, m_gdn_norm_w, m_hgrn_lb_param, m_hgrn_norm_w, m_ssd_conv_w, m_ssd_conv_b, m_ssd_a_log, m_ssd_dt_bias, m_ssd_d, m_ssd_norm_w, m_w_br_a, m_w_br_b, m_w_br_c, m_w_out, m_norm2_w, m_ffn_w_up, m_ffn_conv_w, m_ffn_conv_b, m_ffn_w_down, m_final_norm_w, v_w_ada, v_b_ada, v_norm1_w, v_w_in, v_gdn_conv_w, v_gdn_a_log, v_gdn_dt_bias, v_gdn_norm_w, v_hgrn_lb_param, v_hgrn_norm_w, v_ssd_conv_w, v_ssd_conv_b, v_ssd_a_log, v_ssd_dt_bias, v_ssd_d, v_ssd_norm_w, v_w_br_a, v_w_br_b, v_w_br_c, v_w_out, v_norm2_w, v_ffn_w_up, v_ffn_conv_w, v_ffn_conv_b, v_ffn_w_down, v_final_norm_w):
    given = dict(locals())
    w = {n: given[n] for n in WEIGHTS}
    m = {n: given["m_" + n] for n in WEIGHTS}
    v = {n: given["v_" + n] for n in WEIGHTS}
    me = _my_index()
    bsz = c.shape[0]
    ncol = 6 * D_MODEL // N_DEV

    shards = [w[n].astype(BF16) if n in MATMUL_WEIGHTS else w[n] for n, _, _ in SPLIT] + [c]
    gathered = _gather_two_level(shards, "gather_weights")
    wf = {n: _join_cols(g, n == "w_in", f"join_{n}") if n in UNALIGNED else _join_blocks(g, shape, axis)
          for (n, shape, axis), g in zip(SPLIT, gathered)}
    c_all = gathered[-1].reshape(N_DEV * bsz, D_MODEL)

    b_cols = lax.dynamic_slice_in_dim(b_ada, me * ncol, ncol, axis=1)[:, None]
    mod_cols = _ada_fwd(c_all, w_ada, b_cols)
    send = mod_cols.reshape(DEPTH, N_DEV, bsz, ncol).transpose(1, 0, 2, 3)
    got = _all_to_all([send], "scatter_mod")[0]
    mod = got.transpose(1, 2, 0, 3).reshape(DEPTH, bsz, 6 * D_MODEL)

    loss, dx, grads, dmod = _local_step(x, mod, wf, w, loss_target)

    send = dmod.reshape(DEPTH, bsz, N_DEV, ncol).transpose(2, 0, 1, 3)
    got_dmod = _all_to_all([send], "scatter_dmod")[0]
    dmod_all = got_dmod.transpose(1, 0, 2, 3).reshape(DEPTH, N_DEV * bsz, ncol)
    g_w_ada, g_b_cols = _ada_bwd(c_all.T, dmod_all)

    core = lax.axis_index("c")
    by_core = []
    for n, shape, axis in SPLIT:
        if n in UNALIGNED:
            by_core.append(_split_cols(grads[n], n == "w_in", shape[axis] // N_DEV, f"split_{n}"))
            continue
        parts = _split_blocks(grads[n], shape, axis).astype(BF16)
        parts = parts.reshape((N_DEV // 2, 2, -1, parts.shape[-1]))
        by_core.append(jnp.swapaxes(parts, 0, 1))
    from_sibling = _sibling_swap(by_core, "swap_grads")
    sums = [_pair_sum(lax.dynamic_index_in_dim(mine, core, 0, keepdims=False), theirs, f"pair_sum_{n}")
            for (n, _, _), mine, theirs in zip(SPLIT, by_core, from_sibling)]
    got = _chip_scatter(sums, "scatter_grads")
    grads["b_ada"] = lax.dynamic_update_slice_in_dim(jnp.zeros_like(b_ada), g_b_cols[:, 0], me * ncol, axis=1)

    out = {}
    slots = [(n, g8) for (n, _, _), g8 in zip(SPLIT, got)] + [("w_ada", g_w_ada[None])]
    for n, gs in slots:
        out[n] = _sum_adamw(gs.reshape((gs.shape[0],) + w[n].shape), w[n], m[n], v[n], f"adamw_{n}")
    r8 = _gather_two_level([_pack_repl(grads)], "gather_small_grads")[0]
    res = _sum_adamw(r8[:, None], _pack_repl(w)[None], _pack_repl(m)[None], _pack_repl(v)[None], "adamw_repl")
    repl_out = [_unpack_repl(o[0]) for o in res]
    pick = lambda i, n: out[n][i] if n in out else repl_out[i][n]
    loss = lax.psum(loss, ("x", "y", "c"))
    return (loss, dx, *[pick(i, n) for i in range(4) for n in WEIGHTS])
```

```python
import functools
import math

import jax
import jax.numpy as jnp
from jax import lax
from jax.experimental import pallas as pl
from jax.experimental.pallas import tpu as pltpu

F32, BF16 = jnp.float32, jnp.bfloat16
HI = lax.Precision.HIGHEST
MESH_ID = pl.DeviceIdType.MESH

N_DEV = 8
EPS = 1e-6
D_MODEL = 1024
DEPTH = 2
GDN_HEADS, GDN_DK, GDN_CHUNK = 4, 128, 64
HGRN_HEADS, HGRN_DK, HGRN_CHUNK, HGRN_BLOCK = 4, 128, 16, 128
SSD_HEADS, SSD_HEAD_DIM, SSD_GROUPS, SSD_STATE, SSD_CHUNK = 8, 64, 2, 128, 64
SSD_INNER = SSD_HEADS * SSD_HEAD_DIM
FFN_HIDDEN = 2816
LANES = 128
P_QKV, P_GZ, P_XBC, P_GATE, P_HQ, P_HF, P_HI, P_HG, P_SZ, P_SMALL, P_WIDTH = (
    0, 1536, 2048, 3072, 6144, 6656, 7168, 7680, 8192, 8704, 9216)
SM_A, SM_B, SM_DT = 0, 4, 8
W_IN_SPLITS = (1536, 4, 4, 512, 512, 512, 512, 512, 512, 1024, 8, 3072)

ADAM_LR, ADAM_B1, ADAM_B2, ADAM_EPS, ADAM_WD, ADAM_STEP = 0.001, 0.9, 0.999, 1e-08, 0.01, 10

V7X_VMEM_LIMIT = 56 * 1024 * 1024
MM_OPERAND_VMEM = 34 * 1024 * 1024
PACK_W = 1024


def _call(body, name, grid, in_specs, out_specs, out_shape, scratch=(), aliases=None):
    return pl.pallas_call(
        body, name=name, grid=grid, in_specs=in_specs, out_specs=out_specs, out_shape=out_shape,
        scratch_shapes=list(scratch), input_output_aliases=aliases or {},
        compiler_params=pltpu.CompilerParams(
            dimension_semantics=("arbitrary",) * len(grid), vmem_limit_bytes=V7X_VMEM_LIMIT),
    )


IN_PLACE = pl.BlockSpec(memory_space=pl.ANY)


def _pick(n, cands):
    for c in cands:
        if n % c == 0:
            return c
    raise ValueError(f"no tile for {n} among {cands}")


def _row_tile(s, cap):
    t = cap
    while s % t:
        t //= 2
    return t


def _sds(shape, dtype):
    return jax.ShapeDtypeStruct(shape, dtype)


def _dot(a, b):
    return lax.dot_general(a, b, (((1,), (0,)), ((), ())), precision=HI, preferred_element_type=F32)


NN, NT, TN = (((1,), (0,)), ((), ())), (((1,), (1,)), ((), ())), (((0,), (0,)), ((), ()))


def _mxu(a, b, dims):
    return lax.dot_general(a.astype(BF16), b.astype(BF16), dims, preferred_element_type=F32)


@jax.custom_vjp
def _bdot(a, b):
    return _mxu(a, b, NN)


@jax.custom_vjp
def _bdot_nt(a, b):
    return _mxu(a, b, NT)


@jax.custom_vjp
def _bdot_tn(a, b):
    return _mxu(a, b, TN)


_bdot.defvjp(lambda a, b: (_mxu(a, b, NN), (a, b)), lambda r, d: (_mxu(d, r[1], NT), _mxu(r[0], d, TN)))
_bdot_nt.defvjp(lambda a, b: (_mxu(a, b, NT), (a, b)), lambda r, d: (_mxu(d, r[1], NN), _mxu(d, r[0], TN)))
_bdot_tn.defvjp(lambda a, b: (_mxu(a, b, TN), (a, b)), lambda r, d: (_mxu(r[1], d, NT), _mxu(r[0], d, NN)))


def _split(x):
    hi = x.astype(BF16)
    return hi, (x - hi.astype(F32)).astype(BF16)


def _mxu3(a, b, dims):
    ah, al = _split(a)
    bh, bl = _split(b)
    return _mxu(ah, bh, dims) + (_mxu(ah, bl, dims) + _mxu(al, bh, dims))


@jax.custom_vjp
def _dot3(a, b):
    return _mxu3(a, b, NN)


_dot3.defvjp(lambda a, b: (_mxu3(a, b, NN), (a, b)), lambda r, d: (_mxu3(d, r[1], NT), _mxu3(r[0], d, TN)))


def _pieces(x):
    x1 = x.astype(BF16)
    r1 = x - x1.astype(F32)
    x2 = r1.astype(BF16)
    return x1, x2, (r1 - x2.astype(F32)).astype(BF16)


def _mask_mxu(mask, x, dims):
    x1, x2, x3 = _pieces(x)
    return _mxu(mask, x1, dims) + (_mxu(mask, x2, dims) + _mxu(mask, x3, dims))


def _spread_mxu(x, mask, dims):
    x1, x2, x3 = _pieces(x)
    return _mxu(x1, mask, dims) + (_mxu(x2, mask, dims) + _mxu(x3, mask, dims))


@jax.custom_vjp
def _mask_dot(mask, x):
    return _mask_mxu(mask, x, NN)


@jax.custom_vjp
def _spread_dot(x, mask):
    return _spread_mxu(x, mask, NN)


_mask_dot.defvjp(lambda m, x: (_mask_mxu(m, x, NN), m), lambda m, d: (jnp.zeros_like(m), _mask_mxu(m, d, TN)))
_spread_dot.defvjp(lambda x, m: (_spread_mxu(x, m, NN), m), lambda m, d: (_spread_mxu(d, m, NT), jnp.zeros_like(m)))


def _iota(shape, axis):
    return lax.broadcasted_iota(jnp.int32, shape, axis)


def _silu(x):
    return x * jax.nn.sigmoid(x)


def _softplus(x):
    return jnp.maximum(x, 0.0) + jnp.log1p(jnp.exp(-jnp.abs(x)))


def _rms(x, w):
    return x * lax.rsqrt(jnp.mean(x * x, axis=-1, keepdims=True) + EPS) * w


def _lane_col(x, lane):
    m = (_iota(x.shape, 1) == lane).astype(F32)
    return jnp.sum(x * m, axis=1, keepdims=True)


def _col_to_row(c):
    n = c.shape[0]
    eye = (_iota((n, n), 0) == _iota((n, n), 1)).astype(F32)
    return jnp.sum(c * eye, axis=0, keepdims=True)


def _tril(n, strict=False):
    r, c = _iota((n, n), 0), _iota((n, n), 1)
    return (r > c) if strict else (r >= c)


def _mm(a, b, out_dtype, name):
    m, k = a.shape
    n = b.shape[1]
    tm = _pick(m, (1024, 1408, 512, 256, 128, 64, 32, 16, 8))
    tn = _pick(n, (1024, 1408, 768, 512, 384, 256, 128))
    fits = lambda c: k % c == 0 and (tm + tn) * c * 2 * 2 <= MM_OPERAND_VMEM
    tk = next(c for c in (k, 4096, 3072, 2816, 2048, 1024, 768, 512, 384, 256, 128) if fits(c))
    nk = k // tk

    def body_one(a_ref, b_ref, o_ref):
        o_ref[...] = _bdot(a_ref[...], b_ref[...]).astype(out_dtype)

    def body(a_ref, b_ref, o_ref, acc_ref):
        kk = pl.program_id(2)

        @pl.when(kk == 0)
        def _():
            acc_ref[...] = jnp.zeros_like(acc_ref)

        acc_ref[...] += _bdot(a_ref[...], b_ref[...])

        @pl.when(kk == nk - 1)
        def _():
            o_ref[...] = acc_ref[...].astype(out_dtype)

    return _call(
        body_one if nk == 1 else body, name, (m // tm, n // tn, nk),
        [pl.BlockSpec((tm, tk), lambda i, j, kk: (i, kk)), pl.BlockSpec((tk, tn), lambda i, j, kk: (kk, j))],
        pl.BlockSpec((tm, tn), lambda i, j, kk: (i, j)), _sds((m, n), out_dtype),
        scratch=[] if nk == 1 else [pltpu.VMEM((tm, tn), F32)],
    )(a, b)


def _ada_fwd(c_all, w, b):
    depth, _, n = w.shape
    rows = c_all.shape[0]

    def body(c_ref, w_ref, b_ref, o_ref):
        o_ref[...] = _dot(_silu(c_ref[...]), w_ref[...]) + b_ref[...]

    return _call(
        body, "ada_fwd", (depth,),
        [pl.BlockSpec((rows, D_MODEL), lambda l: (0, 0)), pl.BlockSpec((None, D_MODEL, n), lambda l: (l, 0, 0)),
         pl.BlockSpec((None, 1, n), lambda l: (l, 0, 0))],
        pl.BlockSpec((None, rows, n), lambda l: (l, 0, 0)), _sds((depth, rows, n), F32),
    )(c_all, w, b)


def _ada_bwd(c_all_t, dmod):
    depth, rows, n = dmod.shape

    def body(ct_ref, dm_ref, dw_ref, db_ref):
        dm = dm_ref[...]
        dw_ref[...] = _dot(_silu(ct_ref[...]), dm)
        db_ref[...] = jnp.sum(dm, axis=0, keepdims=True)

    return _call(
        body, "ada_bwd", (depth,),
        [pl.BlockSpec((D_MODEL, rows), lambda l: (0, 0)), pl.BlockSpec((None, rows, n), lambda l: (l, 0, 0))],
        [pl.BlockSpec((None, D_MODEL, n), lambda l: (l, 0, 0)), pl.BlockSpec((None, 1, n), lambda l: (l, 0, 0))],
        [_sds((depth, D_MODEL, n), F32), _sds((depth, 1, n), F32)],
    )(c_all_t, dmod)


def _lb_fn(p):
    rows = [p[l:l + 1] for l in range(DEPTH)]
    mx = functools.reduce(jnp.maximum, rows)
    es = [jnp.exp(r - mx) for r in rows]
    tot = functools.reduce(lambda a, b: a + b, es)
    sm = [e / tot for e in es]
    out, run = [], None
    for l in range(DEPTH):
        run = sm[l] if run is None else run + sm[l]
        out.append(run - sm[0])
    return jnp.concatenate(out, axis=0)


def _lb_fwd(p):
    def body(p_ref, o_ref):
        o_ref[...] = _lb_fn(p_ref[...])

    full = pl.BlockSpec(p.shape, lambda i: (0, 0))
    return _call(body, "lb_fwd", (1,), [full], full, _sds(p.shape, F32))(p)


def _lb_bwd(p, d_lower):
    def body(p_ref, d_ref, o_ref):
        _, vjp = jax.vjp(_lb_fn, p_ref[...])
        o_ref[...] = vjp(d_ref[...])[0]

    full = pl.BlockSpec(p.shape, lambda i: (0, 0))
    return _call(body, "lb_bwd", (1,), [full, full], full, _sds(p.shape, F32))(p, d_lower)


def _norm_mod_fn(x, w, shift, scale):
    return _rms(x, w) * (1.0 + scale) + shift


def _norm_mod_fwd(x, w, shift, scale, name):
    bsz, s, d = x.shape
    ts = _row_tile(s, 512)

    def body(x_ref, w_ref, sh_ref, sc_ref, o_ref):
        o_ref[...] = _norm_mod_fn(x_ref[...], w_ref[...], sh_ref[...], sc_ref[...]).astype(BF16)

    row = pl.BlockSpec((None, ts, d), lambda b, i: (b, i, 0))
    per_b = pl.BlockSpec((None, 1, d), lambda b, i: (b, 0, 0))
    return _call(body, name, (bsz, s // ts), [row, pl.BlockSpec((1, d), lambda b, i: (0, 0)), per_b, per_b],
                 row, _sds(x.shape, BF16))(x, w, shift, scale)


def _norm_mod_bwd(x, w, shift, scale, dh, carry, name):
    bsz, s, d = x.shape
    ts = _row_tile(s, 512)

    def body(x_ref, w_ref, sh_ref, sc_ref, dh_ref, c_ref, dx_ref, dw_ref, dsh_ref, dsc_ref):
        b, i = pl.program_id(0), pl.program_id(1)
        _, vjp = jax.vjp(_norm_mod_fn, x_ref[...], w_ref[...], sh_ref[...], sc_ref[...])
        dx, dw, dsh, dsc = vjp(dh_ref[...])
        dx_ref[...] = dx + c_ref[...]

        @pl.when((b == 0) & (i == 0))
        def _():
            dw_ref[...] = jnp.zeros_like(dw_ref)

        @pl.when(i == 0)
        def _():
            dsh_ref[...] = jnp.zeros_like(dsh_ref)
            dsc_ref[...] = jnp.zeros_like(dsc_ref)

        dw_ref[...] += dw
        dsh_ref[...] += dsh
        dsc_ref[...] += dsc

    row = pl.BlockSpec((None, ts, d), lambda b, i: (b, i, 0))
    per_b = pl.BlockSpec((None, 1, d), lambda b, i: (b, 0, 0))
    wspec = pl.BlockSpec((1, d), lambda b, i: (0, 0))
    return _call(body, name, (bsz, s // ts), [row, wspec, per_b, per_b, row, row],
                 [row, wspec, per_b, per_b],
                 [_sds(x.shape, F32), _sds((1, d), F32), _sds((bsz, 1, d), F32), _sds((bsz, 1, d), F32)],
                 )(x, w, shift, scale, dh, carry)


def _resid_fwd(x, y, gate, name):
    bsz, s, d = x.shape
    ts = _row_tile(s, 1024)

    def body(x_ref, y_ref, g_ref, o_ref):
        o_ref[...] = x_ref[...] + g_ref[...] * y_ref[...]

    row = pl.BlockSpec((None, ts, d), lambda b, i: (b, i, 0))
    per_b = pl.BlockSpec((None, 1, d), lambda b, i: (b, 0, 0))
    return _call(body, name, (bsz, s // ts), [row, row, per_b], row, _sds(x.shape, F32))(x, y, gate)


def _gate_bwd(dx, y, gate, name):
    bsz, s, d = dx.shape
    ts = _row_tile(s, 1024)

    def body(dx_ref, y_ref, g_ref, dy_ref, dg_ref):
        dxv = dx_ref[...]
        dy_ref[...] = (dxv * g_ref[...]).astype(BF16)

        @pl.when(pl.program_id(1) == 0)
        def _():
            dg_ref[...] = jnp.zeros_like(dg_ref)

        dg_ref[...] += jnp.sum(dxv * y_ref[...], axis=0, keepdims=True)

    row = pl.BlockSpec((None, ts, d), lambda b, i: (b, i, 0))
    per_b = pl.BlockSpec((None, 1, d), lambda b, i: (b, 0, 0))
    return _call(body, name, (bsz, s // ts), [row, row, per_b], [row, per_b],
                 [_sds(dx.shape, BF16), _sds((bsz, 1, d), F32)])(dx, y, gate)


HALO = 8


def _conv_pre(xx, w_ref, b_ref, kw, rows):
    acc = w_ref[kw - 1:kw, :] * xx[HALO:HALO + rows]
    for k in range(kw - 1):
        acc = acc + w_ref[k:k + 1, :] * pltpu.roll(xx, kw - 1 - k, 0)[HALO:HALO + rows]
    return acc + b_ref[...]


def _conv_fwd(x, col0, width, w, b, name):
    bsz, s, _ = x.shape
    kw = w.shape[0]
    ts = _row_tile(s, 1024)
    tc = _pick(width, (512, 256, 128))
    assert col0 % tc == 0
    c0 = col0 // tc
    hb = ts // HALO

    def body(x_ref, xp_ref, w_ref, b_ref, o_ref):
        i = pl.program_id(1)
        xp = jnp.where(i > 0, xp_ref[...], 0.0)
        xx = jnp.concatenate([xp, x_ref[...]], axis=0)
        pre = _conv_pre(xx, w_ref, b_ref, kw, ts)
        o_ref[...] = _silu(pre)

    return _call(
        body, name, (bsz, s // ts, width // tc),
        [pl.BlockSpec((None, ts, tc), lambda bb, i, j: (bb, i, c0 + j)),
         pl.BlockSpec((None, HALO, tc), lambda bb, i, j: (bb, jnp.maximum(i * hb - 1, 0), c0 + j)),
         pl.BlockSpec((kw, tc), lambda bb, i, j: (0, j)), pl.BlockSpec((1, tc), lambda bb, i, j: (0, j))],
        pl.BlockSpec((None, ts, tc), lambda bb, i, j: (bb, i, j)), _sds((bsz, s, width), F32),
    )(x, x, w, b)


def _conv_bwd(dy, x, col0, width, w, b, dp, name):
    bsz, s, _ = x.shape
    kw = w.shape[0]
    ts = _row_tile(s, 1024)
    tc = _pick(width, (512, 256, 128))
    c0 = col0 // tc
    hb = ts // HALO
    nt = s // ts
    last_h = s // HALO - 1

    def body(x_ref, xp_ref, xn_ref, dy_ref, dyn_ref, w_ref, b_ref, _, dx_ref, dw_ref, db_ref):
        bb, i = pl.program_id(1), pl.program_id(2)
        xp = jnp.where(i > 0, xp_ref[...], 0.0)
        xx = jnp.concatenate([xp, x_ref[...], xn_ref[...]], axis=0)
        dyy = jnp.concatenate([dy_ref[...], jnp.where(i < nt - 1, dyn_ref[...], 0.0)], axis=0)
        n = ts + HALO
        pre = _conv_pre(xx, w_ref, b_ref, kw, n)
        sg = jax.nn.sigmoid(pre)
        dpre = dyy * (sg * (1.0 + pre * (1.0 - sg)))
        dx = w_ref[kw - 1:kw, :] * dpre[:ts]
        for k in range(kw - 1):
            dx = dx + w_ref[k:k + 1, :] * pltpu.roll(dpre, n - (kw - 1 - k), 0)[:ts]
        dx_ref[...] = dx.astype(BF16)

        @pl.when((bb == 0) & (i == 0))
        def _():
            dw_ref[...] = jnp.zeros_like(dw_ref)
            db_ref[...] = jnp.zeros_like(db_ref)

        dt = dpre[:ts]
        db_ref[...] += jnp.sum(dt, axis=0, keepdims=True)
        dw_ref[kw - 1:kw, :] += jnp.sum(dt * xx[HALO:HALO + ts], axis=0, keepdims=True)
        for k in range(kw - 1):
            xs = pltpu.roll(xx, kw - 1 - k, 0)[HALO:HALO + ts]
            dw_ref[k:k + 1, :] += jnp.sum(dt * xs, axis=0, keepdims=True)

    xspec = lambda f: pl.BlockSpec((None, HALO, tc), f)
    return _call(
        body, name, (width // tc, bsz, nt),
        [pl.BlockSpec((None, ts, tc), lambda j, bb, i: (bb, i, c0 + j)),
         xspec(lambda j, bb, i: (bb, jnp.maximum(i * hb - 1, 0), c0 + j)),
         xspec(lambda j, bb, i: (bb, jnp.minimum((i + 1) * hb, last_h), c0 + j)),
         pl.BlockSpec((None, ts, tc), lambda j, bb, i: (bb, i, j)),
         xspec(lambda j, bb, i: (bb, jnp.minimum((i + 1) * hb, last_h), j)),
         pl.BlockSpec((kw, tc), lambda j, bb, i: (0, j)), pl.BlockSpec((1, tc), lambda j, bb, i: (0, j)), IN_PLACE],
        [pl.BlockSpec((None, ts, tc), lambda j, bb, i: (bb, i, c0 + j)),
         pl.BlockSpec((kw, tc), lambda j, bb, i: (0, j)), pl.BlockSpec((1, tc), lambda j, bb, i: (0, j))],
        [_sds(dp.shape, BF16), _sds((kw, width), F32), _sds((1, width), F32)], aliases={7: 0},
    )(x, x, x, dy, dy, w, b, dp)


def _conv_glu_fwd(x, w, b, name):
    bsz, s, f2 = x.shape
    f = f2 // 2
    kw = w.shape[0]
    ts = _row_tile(s, 2048)
    tc = _pick(f, (256, 128))
    nf = f // tc
    hb = ts // HALO

    def body(xg_ref, xgp_ref, xv_ref, xvp_ref, wg_ref, wv_ref, bg_ref, bv_ref, o_ref):
        i = pl.program_id(1)
        halves = []
        for x_ref, xp_ref, w_ref, b_ref in ((xg_ref, xgp_ref, wg_ref, bg_ref), (xv_ref, xvp_ref, wv_ref, bv_ref)):
            xx = jnp.concatenate([jnp.where(i > 0, xp_ref[...], 0.0), x_ref[...]], axis=0)
            halves.append(_conv_pre(xx, w_ref, b_ref, kw, ts))
        o_ref[...] = (_silu(halves[0]) * halves[1]).astype(BF16)

    tile = lambda off: pl.BlockSpec((None, ts, tc), lambda bb, i, j: (bb, i, off + j))
    prev = lambda off: pl.BlockSpec((None, HALO, tc), lambda bb, i, j: (bb, jnp.maximum(i * hb - 1, 0), off + j))
    wsp = lambda rows, off: pl.BlockSpec((rows, tc), lambda bb, i, j: (0, off + j))
    return _call(
        body, name, (bsz, s // ts, nf),
        [tile(0), prev(0), tile(nf), prev(nf), wsp(kw, 0), wsp(kw, nf), wsp(1, 0), wsp(1, nf)],
        pl.BlockSpec((None, ts, tc), lambda bb, i, j: (bb, i, j)), _sds((bsz, s, f), BF16),
    )(x, x, x, x, w, w, b, b)


def _conv_glu_bwd(da, x, w, b, name):
    bsz, s, f2 = x.shape
    f = f2 // 2
    kw = w.shape[0]
    ts = _row_tile(s, 2048)
    tc = _pick(f, (256, 128))
    nf = f // tc
    hb = ts // HALO
    nt = s // ts
    last_h = s // HALO - 1
    n = ts + HALO

    def body(xg_ref, xgp_ref, xgn_ref, xv_ref, xvp_ref, xvn_ref, da_ref, dan_ref,
             wg_ref, wv_ref, bg_ref, bv_ref, wx_ref, dx_ref, dw_ref, db_ref):
        j, bb, i = pl.program_id(0), pl.program_id(1), pl.program_id(2)
        day = jnp.concatenate([da_ref[...], jnp.where(i < nt - 1, dan_ref[...], 0.0)], axis=0)
        xg = jnp.concatenate([jnp.where(i > 0, xgp_ref[...], 0.0), xg_ref[...], xgn_ref[...]], axis=0)
        pre_g = _conv_pre(xg, wg_ref, bg_ref, kw, n)
        sg = jax.nn.sigmoid(pre_g)

        @pl.when((bb == 0) & (i == 0))
        def _():
            dw_ref[...] = jnp.zeros_like(dw_ref)
            db_ref[...] = jnp.zeros_like(db_ref)

        def finish(dpre, xx):
            dx = wx_ref[kw - 1:kw, :] * dpre[:ts]
            for k in range(kw - 1):
                dx = dx + wx_ref[k:k + 1, :] * pltpu.roll(dpre, n - (kw - 1 - k), 0)[:ts]
            dx_ref[...] = dx.astype(BF16)
            dt = dpre[:ts]
            db_ref[...] += jnp.sum(dt, axis=0, keepdims=True)
            dw_ref[kw - 1:kw, :] += jnp.sum(dt * xx[HALO:HALO + ts], axis=0, keepdims=True)
            for k in range(kw - 1):
                dw_ref[k:k + 1, :] += jnp.sum(dt * pltpu.roll(xx, kw - 1 - k, 0)[HALO:HALO + ts], axis=0, keepdims=True)

        @pl.when(j < nf)
        def _():
            xv = jnp.concatenate([jnp.where(i > 0, xvp_ref[...], 0.0), xv_ref[...], xvn_ref[...]], axis=0)
            pre_v = _conv_pre(xv, wv_ref, bv_ref, kw, n)
            finish(day * pre_v * (sg * (1.0 + pre_g * (1.0 - sg))), xg)

        @pl.when(j >= nf)
        def _():
            xv = jnp.concatenate([jnp.where(i > 0, xvp_ref[...], 0.0), xv_ref[...], xvn_ref[...]], axis=0)
            finish(day * (pre_g * sg), xv)

    tile = lambda off: pl.BlockSpec((None, ts, tc), lambda j, bb, i: (bb, i, off + j % nf))
    prev = lambda off: pl.BlockSpec((None, HALO, tc), lambda j, bb, i: (bb, jnp.maximum(i * hb - 1, 0), off + j % nf))
    nxt = lambda off: pl.BlockSpec((None, HALO, tc), lambda j, bb, i: (bb, jnp.minimum((i + 1) * hb, last_h), off + j % nf))
    wsp = lambda rows, off: pl.BlockSpec((rows, tc), lambda j, bb, i: (0, off + j % nf))
    own = lambda rows: pl.BlockSpec((rows, tc), lambda j, bb, i: (0, j))
    return _call(
        body, name, (2 * nf, bsz, nt),
        [tile(0), prev(0), nxt(0), tile(nf), prev(nf), nxt(nf), tile(0), nxt(0),
         wsp(kw, 0), wsp(kw, nf), wsp(1, 0), wsp(1, nf), own(kw)],
        [pl.BlockSpec((None, ts, tc), lambda j, bb, i: (bb, i, j)), own(kw), own(1)],
        [_sds((bsz, s, f2), BF16), _sds((kw, f2), F32), _sds((1, f2), F32)],
    )(x, x, x, x, x, x, da, da, w, w, b, b, w)


def _merge_fwd(p, oa, ob, oc, wa, wb, wc, name):
    bsz, s, _ = p.shape
    tm = _row_tile(s, 512)
    gblk = P_GATE // (3 * D_MODEL)

    def body(g_ref, oa_ref, ob_ref, oc_ref, wa_ref, wb_ref, wc_ref, o_ref):
        acc = None
        for i, (o_r, w_r) in enumerate(((oa_ref, wa_ref), (ob_ref, wb_ref), (oc_ref, wc_ref))):
            y = _bdot(o_r[...], w_r[...])
            t = jax.nn.sigmoid(g_ref[:, i * D_MODEL:(i + 1) * D_MODEL]) * y
            acc = t if acc is None else acc + t
        o_ref[...] = acc.astype(BF16)

    orow = pl.BlockSpec((None, tm, 512), lambda b, i: (b, i, 0))
    wfull = pl.BlockSpec((512, D_MODEL), lambda b, i: (0, 0))
    return _call(
        body, name, (bsz, s // tm),
        [pl.BlockSpec((None, tm, 3 * D_MODEL), lambda b, i: (b, i, gblk)), orow, orow, orow, wfull, wfull, wfull],
        pl.BlockSpec((None, tm, D_MODEL), lambda b, i: (b, i, 0)), _sds((bsz, s, D_MODEL), BF16),
    )(p, oa, ob, oc, wa, wb, wc)


def _merge_bwd(dm, p, oa, ob, oc, wa, wb, wc, wat, wbt, wct, name):
    bsz, s, _ = p.shape
    tm = _row_tile(s, 512)
    gblk = P_GATE // (3 * D_MODEL)

    def body(dm_ref, g_ref, oa_ref, ob_ref, oc_ref, wa_ref, wb_ref, wc_ref, wat_ref, wbt_ref, wct_ref,
             dg_ref, doa_ref, dob_ref, doc_ref, dya_ref, dyb_ref, dyc_ref):
        dmv = dm_ref[...]
        trip = ((oa_ref, wa_ref, wat_ref, doa_ref, dya_ref), (ob_ref, wb_ref, wbt_ref, dob_ref, dyb_ref),
                (oc_ref, wc_ref, wct_ref, doc_ref, dyc_ref))
        for i, (o_r, w_r, wt_r, do_r, dy_r) in enumerate(trip):
            y = _bdot(o_r[...], w_r[...])
            sg = jax.nn.sigmoid(g_ref[:, i * D_MODEL:(i + 1) * D_MODEL])
            dg_ref[:, i * D_MODEL:(i + 1) * D_MODEL] = (dmv * y * sg * (1.0 - sg)).astype(BF16)
            dy = (dmv * sg).astype(BF16)
            dy_r[...] = dy
            do_r[...] = _bdot(dy, wt_r[...])

    orow = pl.BlockSpec((None, tm, 512), lambda b, i: (b, i, 0))
    drow = pl.BlockSpec((None, tm, D_MODEL), lambda b, i: (b, i, 0))
    grow = pl.BlockSpec((None, tm, 3 * D_MODEL), lambda b, i: (b, i, gblk))
    wfull = pl.BlockSpec((512, D_MODEL), lambda b, i: (0, 0))
    wtfull = pl.BlockSpec((D_MODEL, 512), lambda b, i: (0, 0))
    return _call(
        body, name, (bsz, s // tm),
        [drow, grow, orow, orow, orow, wfull, wfull, wfull, wtfull, wtfull, wtfull],
        [grow, orow, orow, orow, drow, drow, drow],
        [_sds(p.shape, BF16)] + [_sds((bsz, s, 512), F32)] * 3 + [_sds((bsz, s, D_MODEL), BF16)] * 3,
    )(dm, p, oa, ob, oc, wa, wb, wc, wat, wbt, wct)


def _final_loss(x, w, target):
    bsz, s, d = x.shape
    ts = _row_tile(s, 512)

    def body(x_ref, w_ref, t_ref, loss_ref, dx_ref, dw_ref):
        first = (pl.program_id(0) == 0) & (pl.program_id(1) == 0)
        y, vjp = jax.vjp(_rms, x_ref[...], w_ref[...])
        err = y - t_ref[...]
        dx, dw = vjp(err * (1.0 / d))
        dx_ref[...] = dx

        @pl.when(first)
        def _():
            loss_ref[...] = jnp.zeros_like(loss_ref)
            dw_ref[...] = jnp.zeros_like(dw_ref)

        loss_ref[...] += 0.5 * jnp.sum(jnp.sum(err * err, axis=1, keepdims=True), axis=0, keepdims=True) * (1.0 / d)
        dw_ref[...] += dw

    row = pl.BlockSpec((None, ts, d), lambda b, i: (b, i, 0))
    wspec = pl.BlockSpec((1, d), lambda b, i: (0, 0))
    return _call(body, "final_loss", (bsz, s // ts), [row, wspec, row],
                 [pl.BlockSpec((8, LANES), lambda b, i: (0, 0)), row, wspec],
                 [_sds((8, LANES), F32), _sds(x.shape, F32), _sds((1, d), F32)])(x, w, target)


def _unit_lower_inverses(ms):
    n = ms[0].shape[0]
    r, c = _iota((n, n), 0), _iota((n, n), 1)
    same = lambda size: (r // size) == (c // size)
    xs = [(r == c).astype(F32) - jnp.where(same(2), m, 0.0) for m in ms]
    size = 4
    while size <= n:
        below = same(size) & ~same(size // 2)
        xs = [x - _mxu3(x, _mxu3(jnp.where(below, m, 0.0), x, NN), NN) for x, m in zip(xs, ms)]
        size *= 2
    return xs


@jax.custom_vjp
def _known_inverse(m, t):
    return t


_known_inverse.defvjp(lambda m, t: (t, t), lambda t, dt: (-_mxu3(t, _mxu3(dt, t, NT), TN), jnp.zeros_like(t)))


def _gdn_chunk(states, qkv, small, z, a_row, dt_row, nw, tinvs=None):
    nb = len(qkv)
    c = qkv[0].shape[0]
    kw = GDN_HEADS * GDN_DK
    incl, strict = _tril(c), _tril(c, True)
    g_all = [-jnp.exp(a_row) * _softplus(small[b] + dt_row) for b in range(nb)]
    beta_all = [jax.nn.sigmoid(small[b]) for b in range(nb)]
    big_g_all = [_mask_dot(incl.astype(BF16), g_all[b]) for b in range(nb)]
    items = [(b, h) for b in range(nb) for h in range(GDN_HEADS)]
    ids = range(len(items))
    col = lambda b, part, h: qkv[b][:, part * kw + h * GDN_DK:part * kw + (h + 1) * GDN_DK]
    unit = lambda t: t * lax.rsqrt(jnp.sum(t * t, axis=-1, keepdims=True) + EPS)
    q = [unit(col(b, 0, h)) * (GDN_DK ** -0.5) for b, h in items]
    k = [unit(col(b, 1, h)) for b, h in items]
    v = [col(b, 2, h) for b, h in items]
    gc = [_lane_col(big_g_all[b], SM_A + h) for b, h in items]
    bc = [_lane_col(beta_all[b], SM_B + h) for b, h in items]
    g_last = [jnp.sum(_lane_col(g_all[b], SM_A + h), axis=0, keepdims=True) for b, h in items]
    decay = [jnp.where(incl, jnp.exp(jnp.where(incl, gc[i] - _col_to_row(gc[i]), 0.0)), 0.0) for i in ids]
    kb = [k[i] * bc[i] for i in ids]
    m = [jnp.where(strict, _bdot_nt(kb[i], k[i]) * decay[i], 0.0) for i in ids]
    if tinvs is None:
        tinv = _unit_lower_inverses(m)
    else:
        tinv = [_known_inverse(m[i], tinvs[i]) for i in ids]
    eg = [jnp.exp(gc[i]) for i in ids]
    u = [_dot3(tinv[i], v[i] * bc[i]) for i in ids]
    w = [_dot3(tinv[i], kb[i] * eg[i]) for i in ids]
    attn = [_bdot_nt(q[i], k[i]) * decay[i] for i in ids]
    v_new = [u[i] - _bdot(w[i], states[i]) for i in ids]
    o_st = [_bdot(q[i] * eg[i], states[i]) for i in ids]
    o = [o_st[i] + _bdot(attn[i], v_new[i]) for i in ids]
    grow = [_bdot_tn(k[i] * jnp.exp(g_last[i] - gc[i]), v_new[i]) for i in ids]
    new_states = [states[i] * jnp.exp(g_last[i]) + grow[i] for i in ids]
    outs = [_rms(o[i], nw) * _silu(z[b][:, h * GDN_DK:(h + 1) * GDN_DK]) for i, (b, h) in enumerate(items)]
    per_seq = [jnp.concatenate(outs[b * GDN_HEADS:(b + 1) * GDN_HEADS], axis=1) for b in range(nb)]
    return new_states, per_seq, tinv


def _seq_items(bsz, heads):
    return [(b, h) for b in range(bsz) for h in range(heads)]


def _gdn_fwd(qkv_act, p, a_row, dt_row, nw, name):
    bsz, s, _ = qkv_act.shape
    c = GDN_CHUNK
    nc = s // c
    items = _seq_items(bsz, GDN_HEADS)

    def body(qkv_ref, sm_ref, z_ref, a_ref, dt_ref, nw_ref, o_ref, st_ref, ti_ref, st_scr):
        @pl.when(pl.program_id(0) == 0)
        def _():
            st_scr[...] = jnp.zeros_like(st_scr)

        st_ref[...] = st_scr[...]
        seqs = range(bsz)
        new_states, o, tinvs = _gdn_chunk(
            [st_scr[b, h] for b, h in items], [qkv_ref[b] for b in seqs], [sm_ref[b] for b in seqs],
            [z_ref[b] for b in seqs], a_ref[...], dt_ref[...], nw_ref[...])
        for i, (b, h) in enumerate(items):
            st_scr[b, h] = new_states[i]
            ti_ref[b, h] = tinvs[i]
        for b in seqs:
            o_ref[b] = o[b].astype(BF16)

    row = lambda w, blk: pl.BlockSpec((bsz, c, w), lambda n, blk=blk: (0, n, blk))
    prm = pl.BlockSpec((1, LANES), lambda n: (0, 0))
    return _call(
        body, name, (nc,), [row(1536, 0), row(LANES, P_SMALL // LANES), row(512, P_GZ // 512), prm, prm, prm],
        [row(512, 0), pl.BlockSpec((bsz, None, 4, LANES, LANES), lambda n: (0, n, 0, 0, 0)),
         pl.BlockSpec((bsz, None, 4, c, c), lambda n: (0, n, 0, 0, 0))],
        [_sds((bsz, s, 512), BF16), _sds((bsz, nc, 4, LANES, LANES), F32), _sds((bsz, nc, 4, c, c), F32)],
        scratch=[pltpu.VMEM((bsz, 4, LANES, LANES), F32)],
    )(qkv_act, p, p, a_row, dt_row, nw)


def _gdn_bwd(do, qkv_act, p, a_row, dt_row, nw, st_all, ti_all, dp, name):
    bsz, s, _ = qkv_act.shape
    c = GDN_CHUNK
    nc = s // c

    items = _seq_items(bsz, GDN_HEADS)

    def body(qkv_ref, sm_ref, z_ref, a_ref, dt_ref, nw_ref, do_ref, st_ref, ti_ref, _,
             dqkv_ref, dsm_ref, dz_ref, da_ref, ddt_ref, dnw_ref, ds_scr):
        @pl.when(pl.program_id(0) == 0)
        def _():
            ds_scr[...] = jnp.zeros_like(ds_scr)
            da_ref[...] = jnp.zeros_like(da_ref)
            ddt_ref[...] = jnp.zeros_like(ddt_ref)
            dnw_ref[...] = jnp.zeros_like(dnw_ref)

        seqs = range(bsz)
        tinvs = [ti_ref[b, h] for b, h in items]
        chunk = lambda *a: _gdn_chunk(*a, tinvs=tinvs)[:2]
        _, vjp = jax.vjp(chunk, [st_ref[b, h] for b, h in items], [qkv_ref[b] for b in seqs], [sm_ref[b] for b in seqs],
                         [z_ref[b] for b in seqs], a_ref[...], dt_ref[...], nw_ref[...])
        d_states, dqkv, dsm, dz, da, ddt, dnw = vjp(([ds_scr[b, h] for b, h in items], [do_ref[b] for b in seqs]))
        for i, (b, h) in enumerate(items):
            ds_scr[b, h] = d_states[i]
        for b in seqs:
            dqkv_ref[b] = dqkv[b]
            dsm_ref[b] = dsm[b]
            dz_ref[b] = dz[b].astype(BF16)
        da_ref[...] += da
        ddt_ref[...] += ddt
        dnw_ref[...] += dnw

    rrow = lambda w, blk: pl.BlockSpec((bsz, c, w), lambda n, blk=blk: (0, nc - 1 - n, blk))
    prm = pl.BlockSpec((1, LANES), lambda n: (0, 0))
    return _call(
        body, name, (nc,),
        [rrow(1536, 0), rrow(LANES, P_SMALL // LANES), rrow(512, P_GZ // 512), prm, prm, prm, rrow(512, 0),
         pl.BlockSpec((bsz, None, 4, LANES, LANES), lambda n: (0, nc - 1 - n, 0, 0, 0)),
         pl.BlockSpec((bsz, None, 4, c, c), lambda n: (0, nc - 1 - n, 0, 0, 0)), IN_PLACE],
        [rrow(1536, 0), rrow(LANES, 0), rrow(512, P_GZ // 512), prm, prm, prm],
        [_sds((bsz, s, 1536), F32), _sds((bsz, s, LANES), F32), _sds(dp.shape, BF16)] + [_sds((1, LANES), F32)] * 3,
        scratch=[pltpu.VMEM((bsz, 4, LANES, LANES), F32)], aliases={9: 2},
    )(qkv_act, p, p, a_row, dt_row, nw, do, st_all, ti_all, dp)


def _hgrn_block(states, q_raw, f_raw, i_raw, g_raw, lb, nw):
    n = q_raw[0].shape[0]
    c = HGRN_CHUNK
    r, cc = _iota((n, n), 0), _iota((n, n), 1)
    same = (r // c) == (cc // c)
    causal = same & (r >= cc)
    ref_row = (r // c) * c + (c // 2 - 1)
    run_sum = causal.astype(F32)
    rel_sum = run_sum - (same & (ref_row >= cc)).astype(F32)
    sums = jnp.concatenate([run_sum, rel_sum, same.astype(F32)], axis=0).astype(BF16)
    seqs, chunks = range(len(q_raw)), range(n // c)
    items = _seq_items(len(q_raw), HGRN_HEADS)
    hs = lambda t, h: t[:, h * HGRN_DK:(h + 1) * HGRN_DK]
    rows = lambda t, j: t[j * c:(j + 1) * c]
    q = [_silu(q_raw[b]) for b in seqs]
    logf = [jnp.log(lb + (1.0 - lb) * jax.nn.sigmoid(f_raw[b])) for b in seqs]
    k = [(1.0 - lb) * jax.nn.sigmoid(-f_raw[b]) for b in seqs]
    all_sums = [_mask_dot(sums, logf[b]) for b in seqs]
    big_g, g_rel, g_tot = ([t[i * n:(i + 1) * n] for t in all_sums] for i in range(3))
    q_rel = [q[b] * jnp.exp(g_rel[b]) for b in seqs]
    k_rel = [k[b] * jnp.exp(-g_rel[b]) for b in seqs]
    qg = [q[b] * jnp.exp(big_g[b]) for b in seqs]
    k_end = [k[b] * jnp.exp(g_tot[b] - big_g[b]) for b in seqs]
    keep = [[jnp.exp(g_tot[b][j * c:j * c + 1]) for j in chunks] for b in seqs]
    scores = [_bdot_nt(hs(q_rel[b], h), hs(k_rel[b], h)) for b, h in items]
    o_intra = [_bdot(jnp.where(causal, scores[i], 0.0), hs(i_raw[b], h)) for i, (b, h) in enumerate(items)]
    grow = [[_bdot_tn(rows(hs(i_raw[b], h), j), rows(hs(k_end[b], h), j)) for j in chunks] for b, h in items]
    entering, new_states = [], []
    for i, (b, h) in enumerate(items):
        st, per_chunk = states[i], []
        for j in chunks:
            per_chunk.append(st)
            st = st * hs(keep[b][j], h) + grow[i][j]
        entering.append(per_chunk)
        new_states.append(st)
    o_inter = [[_bdot_nt(rows(hs(qg[b], h), j), entering[i][j]) for j in chunks] for i, (b, h) in enumerate(items)]
    outs = [_rms(o_intra[i] + jnp.concatenate(o_inter[i], axis=0), nw) * _silu(hs(g_raw[b], h))
            for i, (b, h) in enumerate(items)]
    return new_states, [jnp.concatenate(outs[b * HGRN_HEADS:(b + 1) * HGRN_HEADS], axis=1) for b in seqs]


def _hgrn_fwd(p, lb, nw, name):
    bsz, s, _ = p.shape
    n = HGRN_BLOCK
    nb = s // n

    items = _seq_items(bsz, HGRN_HEADS)

    def body(q_ref, f_ref, i_ref, g_ref, lb_ref, nw_ref, o_ref, st_ref, st_scr):
        @pl.when(pl.program_id(0) == 0)
        def _():
            st_scr[...] = jnp.zeros_like(st_scr)

        st_ref[...] = st_scr[...]
        per_seq = lambda ref: [ref[b] for b in range(bsz)]
        new_states, o = _hgrn_block([st_scr[b, h] for b, h in items], per_seq(q_ref), per_seq(f_ref), per_seq(i_ref),
                                    per_seq(g_ref), lb_ref[...], nw_ref[...])
        for i, (b, h) in enumerate(items):
            st_scr[b, h] = new_states[i]
        for b in range(bsz):
            o_ref[b] = o[b].astype(BF16)

    row = lambda blk: pl.BlockSpec((bsz, n, 512), lambda i, blk=blk: (0, i, blk))
    return _call(
        body, name, (nb,),
        [row(P_HQ // 512), row(P_HF // 512), row(P_HI // 512), row(P_HG // 512),
         pl.BlockSpec((1, 512), lambda i: (0, 0)), pl.BlockSpec((1, LANES), lambda i: (0, 0))],
        [row(0), pl.BlockSpec((bsz, None, 4, LANES, LANES), lambda i: (0, i, 0, 0, 0))],
        [_sds((bsz, s, 512), BF16), _sds((bsz, nb, 4, LANES, LANES), F32)],
        scratch=[pltpu.VMEM((bsz, 4, LANES, LANES), F32)],
    )(p, p, p, p, lb, nw)


def _hgrn_bwd(do, p, lb, nw, st_all, dp, name):
    bsz, s, _ = p.shape
    n = HGRN_BLOCK
    nb = s // n

    items = _seq_items(bsz, HGRN_HEADS)

    def body(q_ref, f_ref, i_ref, g_ref, lb_ref, nw_ref, do_ref, st_ref, _, dp_ref, dlb_ref, dnw_ref, ds_scr):
        @pl.when(pl.program_id(0) == 0)
        def _():
            ds_scr[...] = jnp.zeros_like(ds_scr)
            dlb_ref[...] = jnp.zeros_like(dlb_ref)
            dnw_ref[...] = jnp.zeros_like(dnw_ref)

        per_seq = lambda ref: [ref[b] for b in range(bsz)]
        _, vjp = jax.vjp(_hgrn_block, [st_ref[b, h] for b, h in items], per_seq(q_ref), per_seq(f_ref), per_seq(i_ref),
                         per_seq(g_ref), lb_ref[...], nw_ref[...])
        d_states, dq, df, di, dg, dlb, dnw = vjp(([ds_scr[b, h] for b, h in items], per_seq(do_ref)))
        for i, (b, h) in enumerate(items):
            ds_scr[b, h] = d_states[i]
        for b in range(bsz):
            for j, t in enumerate((dq, df, di, dg)):
                dp_ref[b, :, j * 512:(j + 1) * 512] = t[b].astype(BF16)
        dlb_ref[...] += dlb
        dnw_ref[...] += dnw

    row = lambda blk: pl.BlockSpec((bsz, n, 512), lambda i, blk=blk: (0, nb - 1 - i, blk))
    return _call(
        body, name, (nb,),
        [row(P_HQ // 512), row(P_HF // 512), row(P_HI // 512), row(P_HG // 512),
         pl.BlockSpec((1, 512), lambda i: (0, 0)), pl.BlockSpec((1, LANES), lambda i: (0, 0)), row(0),
         pl.BlockSpec((bsz, None, 4, LANES, LANES), lambda i: (0, nb - 1 - i, 0, 0, 0)), IN_PLACE],
        [pl.BlockSpec((bsz, n, 2048), lambda i: (0, nb - 1 - i, P_HQ // 2048)),
         pl.BlockSpec((1, 512), lambda i: (0, 0)), pl.BlockSpec((1, LANES), lambda i: (0, 0))],
        [_sds(dp.shape, BF16), _sds((1, 512), F32), _sds((1, LANES), F32)],
        scratch=[pltpu.VMEM((bsz, 4, LANES, LANES), F32)], aliases={8: 0},
    )(p, p, p, p, lb, nw, do, st_all, dp)


def _ssd_chunk(states, xbc, small, z, a_row, dt_row, d_row, nw):
    seqs = range(len(xbc))
    c = xbc[0].shape[0]
    incl = _tril(c)
    spread = (_iota((LANES, SSD_INNER), 0) == SM_DT + _iota((LANES, SSD_INNER), 1) // SSD_HEAD_DIM).astype(BF16)
    dt_all = [_softplus(small[b] + dt_row) for b in seqs]
    both = [_spread_dot(jnp.concatenate([dt_all[b], dt_all[b] * (-jnp.exp(a_row))], axis=0), spread) for b in seqs]
    dt_e, da_e = [t[:c] for t in both], [t[c:] for t in both]
    acs_e = [_mask_dot(incl.astype(BF16), da_e[b]) for b in seqs]
    last_e = [jnp.sum(da_e[b], axis=0, keepdims=True) for b in seqs]
    xs = [xbc[b][:, :SSD_INNER] for b in seqs]
    xdt = [xs[b] * dt_e[b] for b in seqs]
    gw = SSD_GROUPS * SSD_STATE
    lane = _iota((1, LANES), 1)
    items = _seq_items(len(xbc), 4)
    grp = [(b, g) for b in seqs for g in range(SSD_GROUPS)]
    ps = lambda t, j: t[:, j * LANES:(j + 1) * LANES]
    bg = {(b, g): xbc[b][:, SSD_INNER + g * SSD_STATE:SSD_INNER + (g + 1) * SSD_STATE] for b, g in grp}
    cg = {(b, g): xbc[b][:, SSD_INNER + gw + g * SSD_STATE:SSD_INNER + gw + (g + 1) * SSD_STATE] for b, g in grp}
    cb = {bgk: _bdot_nt(cg[bgk], bg[bgk]) for bgk in grp}

    def seg(b, j, sub):
        ac = ps(acs_e[b], j)[:, sub * SSD_HEAD_DIM:sub * SSD_HEAD_DIM + 1]
        return jnp.where(incl, jnp.exp(jnp.where(incl, ac - _col_to_row(ac), 0.0)), 0.0)

    mine = [((lane // SSD_HEAD_DIM) == sub).astype(F32) for sub in range(2)]
    y_in = [[_bdot(cb[b, j // 2] * seg(b, j, sub), ps(xdt[b], j) * mine[sub]) for sub in range(2)] for b, j in items]
    y_st = [_bdot(cg[b, j // 2], states[i]) for i, (b, j) in enumerate(items)]
    grow = [_bdot_tn(bg[b, j // 2], ps(xdt[b], j) * jnp.exp(ps(last_e[b], j) - ps(acs_e[b], j))) for b, j in items]
    new_states = [states[i] * jnp.exp(ps(last_e[b], j)) + grow[i] for i, (b, j) in enumerate(items)]
    ys = [y_in[i][0] + y_in[i][1] + y_st[i] * jnp.exp(ps(acs_e[b], j)) + ps(d_row, j) * ps(xs[b], j)
          for i, (b, j) in enumerate(items)]
    gwid = SSD_INNER // SSD_GROUPS
    outs = []
    for b in seqs:
        yz = jnp.concatenate(ys[4 * b:4 * b + 4], axis=1) * _silu(z[b])
        outs.append(jnp.concatenate(
            [_rms(yz[:, g * gwid:(g + 1) * gwid], nw[:, g * gwid:(g + 1) * gwid]) for g in range(SSD_GROUPS)], axis=1))
    return new_states, outs


def _ssd_fwd(xbc_act, p, a_row, dt_row, d_row, nw, name):
    bsz, s, _ = xbc_act.shape
    c = SSD_CHUNK
    nc = s // c

    items = _seq_items(bsz, 4)

    def body(x_ref, sm_ref, z_ref, a_ref, dt_ref, d_ref, nw_ref, o_ref, st_ref, st_scr):
        @pl.when(pl.program_id(0) == 0)
        def _():
            st_scr[...] = jnp.zeros_like(st_scr)

        st_ref[...] = st_scr[...]
        per_seq = lambda ref: [ref[b] for b in range(bsz)]
        new_states, o = _ssd_chunk([st_scr[b, j] for b, j in items], per_seq(x_ref), per_seq(sm_ref), per_seq(z_ref),
                                   a_ref[...], dt_ref[...], d_ref[...], nw_ref[...])
        for i, (b, j) in enumerate(items):
            st_scr[b, j] = new_states[i]
        for b in range(bsz):
            o_ref[b] = o[b].astype(BF16)

    row = lambda w, blk: pl.BlockSpec((bsz, c, w), lambda n, blk=blk: (0, n, blk))
    prm = pl.BlockSpec((1, LANES), lambda n: (0, 0))
    prm5 = pl.BlockSpec((1, 512), lambda n: (0, 0))
    return _call(
        body, name, (nc,),
        [row(1024, 0), row(LANES, P_SMALL // LANES), row(512, P_SZ // 512), prm, prm, prm5, prm5],
        [row(512, 0), pl.BlockSpec((bsz, None, 4, LANES, LANES), lambda n: (0, n, 0, 0, 0))],
        [_sds((bsz, s, 512), BF16), _sds((bsz, nc, 4, LANES, LANES), F32)],
        scratch=[pltpu.VMEM((bsz, 4, LANES, LANES), F32)],
    )(xbc_act, p, p, a_row, dt_row, d_row, nw)


def _ssd_bwd(do, xbc_act, p, a_row, dt_row, d_row, nw, st_all, dp, name):
    bsz, s, _ = xbc_act.shape
    c = SSD_CHUNK
    nc = s // c

    items = _seq_items(bsz, 4)

    def body(x_ref, sm_ref, z_ref, a_ref, dt_ref, d_ref, nw_ref, do_ref, st_ref, _,
             dx_ref, dsm_ref, dz_ref, da_ref, ddt_ref, dd_ref, dnw_ref, ds_scr):
        @pl.when(pl.program_id(0) == 0)
        def _():
            ds_scr[...] = jnp.zeros_like(ds_scr)
            da_ref[...] = jnp.zeros_like(da_ref)
            ddt_ref[...] = jnp.zeros_like(ddt_ref)
            dd_ref[...] = jnp.zeros_like(dd_ref)
            dnw_ref[...] = jnp.zeros_like(dnw_ref)

        per_seq = lambda ref: [ref[b] for b in range(bsz)]
        _, vjp = jax.vjp(_ssd_chunk, [st_ref[b, j] for b, j in items], per_seq(x_ref), per_seq(sm_ref), per_seq(z_ref),
                         a_ref[...], dt_ref[...], d_ref[...], nw_ref[...])
        d_states, dx, dsm, dz, da, ddt, dd, dnw = vjp(([ds_scr[b, j] for b, j in items], per_seq(do_ref)))
        for i, (b, j) in enumerate(items):
            ds_scr[b, j] = d_states[i]
        for b in range(bsz):
            dx_ref[b] = dx[b]
            dsm_ref[b] = dsm[b]
            dz_ref[b] = dz[b].astype(BF16)
        da_ref[...] += da
        ddt_ref[...] += ddt
        dd_ref[...] += dd
        dnw_ref[...] += dnw

    row = lambda w, blk: pl.BlockSpec((bsz, c, w), lambda n, blk=blk: (0, nc - 1 - n, blk))
    prm = pl.BlockSpec((1, LANES), lambda n: (0, 0))
    prm5 = pl.BlockSpec((1, 512), lambda n: (0, 0))
    return _call(
        body, name, (nc,),
        [row(1024, 0), row(LANES, P_SMALL // LANES), row(512, P_SZ // 512), prm, prm, prm5, prm5, row(512, 0),
         pl.BlockSpec((bsz, None, 4, LANES, LANES), lambda n: (0, nc - 1 - n, 0, 0, 0)), IN_PLACE],
        [row(1024, 0), row(LANES, 0), row(512, P_SZ // 512), prm, prm, prm5, prm5],
        [_sds((bsz, s, 1024), F32), _sds((bsz, s, LANES), F32), _sds(dp.shape, BF16),
         _sds((1, LANES), F32), _sds((1, LANES), F32), _sds((1, 512), F32), _sds((1, 512), F32)],
        scratch=[pltpu.VMEM((bsz, 4, LANES, LANES), F32)], aliases={9: 2},
    )(xbc_act, p, p, a_row, dt_row, d_row, nw, do, st_all, dp)


def _small_cols(dsm_a, dsm_c, dp, name):
    bsz, s, _ = dsm_a.shape
    ts = _row_tile(s, 1024)
    width = P_WIDTH - P_SMALL

    def body(a_ref, c_ref, _, o_ref):
        o_ref[:, :LANES] = (a_ref[...] + c_ref[...]).astype(BF16)
        o_ref[:, LANES:] = jnp.zeros((ts, width - LANES), BF16)

    row = pl.BlockSpec((None, ts, LANES), lambda b, i: (b, i, 0))
    return _call(body, name, (bsz, s // ts), [row, row, IN_PLACE],
                 pl.BlockSpec((None, ts, width), lambda b, i: (b, i, P_SMALL // width)), _sds(dp.shape, BF16),
                 aliases={2: 0})(dsm_a, dsm_c, dp)


def _peer(k):
    x, y, c = lax.axis_index("x"), lax.axis_index("y"), lax.axis_index("c")
    px = 1 - x if k & 4 else x
    py = 1 - y if k & 2 else y
    pc = 1 - c if k & 1 else c
    return (px, py, pc), 4 * px + 2 * py + pc


def _my_index():
    return 4 * lax.axis_index("x") + 2 * lax.axis_index("y") + lax.axis_index("c")


def _mesh_place():
    x, y, c = lax.axis_index("x"), lax.axis_index("y"), lax.axis_index("c")
    return (x, y, c), (x, y, 1 - c), [(1 - x, y), (x, 1 - y), (1 - x, 1 - y)]


def _run_exchange(body, name, arrays, out_shape, n_sems):
    n = len(arrays)
    any_spec = pl.BlockSpec(memory_space=pl.ANY)
    return pl.pallas_call(
        body, name=name, out_shape=out_shape, in_specs=[any_spec] * n, out_specs=[any_spec] * n,
        scratch_shapes=[pltpu.SemaphoreType.DMA((n_sems, n)), pltpu.SemaphoreType.DMA((n_sems, n)),
                        pltpu.SemaphoreType.DMA((n,))],
    )(*arrays)


def _all_to_all(arrays, name):
    n = len(arrays)

    def body(*refs):
        ins, outs = refs[:n], refs[n:2 * n]
        send_sems, recv_sems, local_sems = refs[2 * n:]
        me = _my_index()

        def copy(i, k, arriving):
            peer, slot = _peer(k)
            return pltpu.make_async_remote_copy(
                src_ref=ins[i].at[slot], dst_ref=outs[i].at[slot if arriving else me], send_sem=send_sems.at[k - 1, i],
                recv_sem=recv_sems.at[k - 1, i], device_id=peer, device_id_type=MESH_ID)

        mine = [pltpu.make_async_copy(ins[i].at[me], outs[i].at[me], local_sems.at[i]) for i in range(n)]
        sends = [copy(i, k, False) for k in range(1, N_DEV) for i in range(n)]
        for cp in mine + sends:
            cp.start()
        for k in range(1, N_DEV):
            for i in range(n):
                copy(i, k, True).wait_recv()
        for cp in sends:
            cp.wait_send()
        for cp in mine:
            cp.wait()

    return _run_exchange(body, name, arrays, [_sds(a.shape, a.dtype) for a in arrays], N_DEV - 1)


def _gather_two_level(arrays, name):
    n = len(arrays)

    def body(*refs):
        ins, outs = refs[:n], refs[n:2 * n]
        send_sems, recv_sems, local_sems = refs[2 * n:]
        (x, y, c), sibling, chips = _mesh_place()
        slot = lambda px, py, pc: 4 * px + 2 * py + pc

        def copy(i, k, block, to, src=None):
            return pltpu.make_async_remote_copy(
                src_ref=outs[i].at[block] if src is None else src, dst_ref=outs[i].at[block],
                send_sem=send_sems.at[k, i], recv_sem=recv_sems.at[k, i], device_id=to, device_id_type=MESH_ID)

        me = slot(x, y, c)
        mine = [pltpu.make_async_copy(ins[i], outs[i].at[me], local_sems.at[i]) for i in range(n)]
        first = [copy(i, 0, me, sibling, src=ins[i]) for i in range(n)]
        first += [copy(i, 1 + j, me, (*chip, c), src=ins[i]) for j, chip in enumerate(chips) for i in range(n)]
        for cp in mine + first:
            cp.start()
        passed = []
        for j, chip in enumerate(chips):
            for i in range(n):
                copy(i, 1 + j, slot(*chip, c), (x, y, c)).wait_recv()
                cp = copy(i, 4 + j, slot(*chip, c), sibling)
                cp.start()
                passed.append(cp)
        for i in range(n):
            copy(i, 0, slot(x, y, 1 - c), (x, y, c)).wait_recv()
        for j, chip in enumerate(chips):
            for i in range(n):
                copy(i, 4 + j, slot(*chip, 1 - c), (x, y, c)).wait_recv()
        for cp in first + passed:
            cp.wait_send()
        for cp in mine:
            cp.wait()

    out_shape = [_sds((N_DEV,) + a.shape, a.dtype) for a in arrays]
    return _run_exchange(body, name, arrays, out_shape, 7)


def _sibling_swap(arrays, name):
    n = len(arrays)

    def body(*refs):
        ins, outs = refs[:n], refs[n:2 * n]
        send_sems, recv_sems, _ = refs[2 * n:]
        (x, y, c), sibling, _ = _mesh_place()
        copies = [pltpu.make_async_remote_copy(
            src_ref=ins[i].at[1 - c], dst_ref=outs[i], send_sem=send_sems.at[0, i], recv_sem=recv_sems.at[0, i],
            device_id=sibling, device_id_type=MESH_ID) for i in range(n)]
        for cp in copies:
            cp.start()
        for cp in copies:
            cp.wait()

    out_shape = [_sds(a.shape[1:], a.dtype) for a in arrays]
    return _run_exchange(body, name, arrays, out_shape, 1)


def _chip_scatter(arrays, name):
    n = len(arrays)

    def body(*refs):
        ins, outs = refs[:n], refs[n:2 * n]
        send_sems, recv_sems, local_sems = refs[2 * n:]
        (x, y, c), _, chips = _mesh_place()
        me = 2 * x + y
        mine = [pltpu.make_async_copy(ins[i].at[me], outs[i].at[me], local_sems.at[i]) for i in range(n)]
        sends = [pltpu.make_async_remote_copy(
            src_ref=ins[i].at[2 * chip[0] + chip[1]], dst_ref=outs[i].at[me], send_sem=send_sems.at[j, i],
            recv_sem=recv_sems.at[j, i], device_id=(*chip, c), device_id_type=MESH_ID)
            for j, chip in enumerate(chips) for i in range(n)]
        for cp in mine + sends:
            cp.start()
        for j, chip in enumerate(chips):
            for i in range(n):
                pltpu.make_async_remote_copy(
                    src_ref=ins[i].at[me], dst_ref=outs[i].at[2 * chip[0] + chip[1]], send_sem=send_sems.at[j, i],
                    recv_sem=recv_sems.at[j, i], device_id=(*chip, c), device_id_type=MESH_ID).wait_recv()
        for cp in sends:
            cp.wait_send()
        for cp in mine:
            cp.wait()

    out_shape = [_sds(a.shape, a.dtype) for a in arrays]
    return _run_exchange(body, name, arrays, out_shape, 3)


def _pair_sum(a, b, name):
    lead, rows, width = a.shape
    tr = _pick(rows, (256, 128, 64, 32, 16, 8)) if rows % 8 == 0 else rows

    def body(a_ref, b_ref, o_ref):
        o_ref[...] = (a_ref[...].astype(F32) + b_ref[...].astype(F32)).astype(o_ref.dtype)

    blk = pl.BlockSpec((None, tr, width), lambda l, i: (l, i, 0))
    return _call(body, name, (lead, rows // tr), [blk, blk], blk, _sds(a.shape, a.dtype))(a, b)


def _sum_adamw(gs, w, m, v, name):
    lead, rows, width = w.shape
    slots = gs.shape[0]
    tr = _pick(rows, (128, 64, 32, 16, 8)) if rows % 8 == 0 else rows

    def body(g_ref, w_ref, m_ref, v_ref, go_ref, d_ref, mo_ref, vo_ref):
        g = g_ref[0].astype(F32)
        for i in range(1, slots):
            g = g + g_ref[i].astype(F32)
        m2 = ADAM_B1 * m_ref[...] + (1.0 - ADAM_B1) * g
        v2 = ADAM_B2 * v_ref[...] + (1.0 - ADAM_B2) * (g * g)
        m_hat = m2 / (1.0 - ADAM_B1 ** ADAM_STEP)
        v_hat = v2 / (1.0 - ADAM_B2 ** ADAM_STEP)
        go_ref[...] = g
        d_ref[...] = -ADAM_LR * (m_hat / (jnp.sqrt(v_hat) + ADAM_EPS) + ADAM_WD * w_ref[...])
        mo_ref[...] = m2
        vo_ref[...] = v2

    blk = pl.BlockSpec((None, tr, width), lambda l, i: (l, i, 0))
    return _call(body, name, (lead, rows // tr),
                 [pl.BlockSpec((slots, None, tr, width), lambda l, i: (0, l, i, 0)), blk, blk, blk],
                 [blk] * 4, [_sds(w.shape, F32)] * 4)(gs, w, m, v)


MATMUL_WEIGHTS = ("w_in", "w_br_a", "w_br_b", "w_br_c", "w_out", "ffn_w_up", "ffn_w_down")
UNALIGNED = ("w_in", "ffn_w_up")
SPLIT = (
    ("w_in", (DEPTH, D_MODEL, 8720), 2),
    ("gdn_conv_w", (DEPTH, 4, 1536), 2), ("ssd_conv_w", (DEPTH, 4, 1024), 2),
    ("w_br_a", (DEPTH, 512, D_MODEL), 2), ("w_br_b", (DEPTH, 512, D_MODEL), 2), ("w_br_c", (DEPTH, 512, D_MODEL), 2),
    ("w_out", (DEPTH, D_MODEL, D_MODEL), 1), ("ffn_w_up", (DEPTH, D_MODEL, 2 * FFN_HIDDEN), 2),
    ("ffn_conv_w", (DEPTH, 3, 2 * FFN_HIDDEN), 2), ("ffn_w_down", (DEPTH, FFN_HIDDEN, D_MODEL), 1),
)
REPL = (
    ("b_ada", (DEPTH, 6 * D_MODEL)), ("norm1_w", (DEPTH, D_MODEL)), ("gdn_a_log", (DEPTH, 4)),
    ("gdn_dt_bias", (DEPTH, 4)), ("gdn_norm_w", (DEPTH, 128)), ("hgrn_lb_param", (DEPTH, 512)),
    ("hgrn_norm_w", (DEPTH, 128)), ("ssd_conv_b", (DEPTH, 1024)), ("ssd_a_log", (DEPTH, 8)),
    ("ssd_dt_bias", (DEPTH, 8)), ("ssd_d", (DEPTH, 8)), ("ssd_norm_w", (DEPTH, 512)), ("norm2_w", (DEPTH, D_MODEL)),
    ("ffn_conv_b", (DEPTH, 2 * FFN_HIDDEN)), ("final_norm_w", (D_MODEL,)),
)
WEIGHTS = ("w_ada", "b_ada", "norm1_w", "w_in", "gdn_conv_w", "gdn_a_log", "gdn_dt_bias", "gdn_norm_w",
           "hgrn_lb_param", "hgrn_norm_w", "ssd_conv_w", "ssd_conv_b", "ssd_a_log", "ssd_dt_bias", "ssd_d",
           "ssd_norm_w", "w_br_a", "w_br_b", "w_br_c", "w_out", "norm2_w", "ffn_w_up", "ffn_conv_w", "ffn_conv_b",
           "ffn_w_down", "final_norm_w")


def _block_shape(shape, axis):
    return tuple(d // N_DEV if i == axis else d for i, d in enumerate(shape))


def _join_blocks(gathered, shape, axis):
    return jnp.moveaxis(gathered, 0, axis).reshape(shape)


def _split_blocks(full, shape, axis):
    bs = _block_shape(shape, axis)
    t = full.reshape(shape[:axis] + (N_DEV, bs[axis]) + shape[axis + 1:])
    return jnp.moveaxis(t, axis, 0)


def _pack_repl(vals):
    parts = []
    for n, shape in REPL:
        size = math.prod(shape)
        parts.append(jnp.pad(vals[n].reshape(-1), (0, -(-size // PACK_W) * PACK_W - size)))
    cat = jnp.concatenate(parts)
    rows = -(-cat.shape[0] // (8 * PACK_W)) * 8
    return jnp.pad(cat, (0, rows * PACK_W - cat.shape[0])).reshape(rows, PACK_W)


def _unpack_repl(packed):
    flat, out, off = packed.reshape(-1), {}, 0
    for n, shape in REPL:
        size = math.prod(shape)
        out[n] = flat[off:off + size].reshape(shape)
        off += -(-size // PACK_W) * PACK_W
    return out


def _lane_row(vec, lane0):
    return jnp.pad(vec, (lane0, LANES - lane0 - vec.shape[0]))[None]


def _arrange_w_in(w):
    offs = [0]
    for sz in W_IN_SPLITS:
        offs.append(offs[-1] + sz)
    qkv, a, b, gz, hq, hf, hi, hg, sz_, xbc, dt, gate = [w[:, offs[i]:offs[i + 1]] for i in range(12)]
    pad = jnp.zeros((w.shape[0], P_WIDTH - P_SMALL - 16), w.dtype)
    return jnp.concatenate([qkv, gz, xbc, gate, hq, hf, hi, hg, sz_, a, b, dt, pad], axis=1)


def _restore_w_in(wp):
    cut = lambda o, n: wp[:, o:o + n]
    return jnp.concatenate([
        cut(P_QKV, 1536), cut(P_SMALL + SM_A, 4), cut(P_SMALL + SM_B, 4), cut(P_GZ, 512), cut(P_HQ, 512),
        cut(P_HF, 512), cut(P_HI, 512), cut(P_HG, 512), cut(P_SZ, 512), cut(P_XBC, 1024), cut(P_SMALL + SM_DT, 8),
        cut(P_GATE, 3072)], axis=1)


def _join_cols(gathered, arrange, name):
    _, depth, rows, cols = gathered.shape
    tr = _pick(rows, (256, 128, 64, 32, 16, 8))
    width = P_WIDTH if arrange else N_DEV * cols

    def body(g_ref, o_ref):
        row = jnp.concatenate([g_ref[d] for d in range(N_DEV)], axis=1)
        o_ref[...] = _arrange_w_in(row) if arrange else row

    return _call(body, name, (depth, rows // tr),
                 [pl.BlockSpec((N_DEV, None, tr, cols), lambda l, i: (0, l, i, 0))],
                 pl.BlockSpec((None, tr, width), lambda l, i: (l, i, 0)), _sds((depth, rows, width), gathered.dtype),
                 )(gathered)


def _split_cols(per_layer, restore, cols, name):
    depth = len(per_layer)
    rows = per_layer[0].shape[0]
    tr = _pick(rows, (256, 128, 64, 32, 16, 8))
    nt = rows // tr

    def body(*refs):
        o_ref = refs[depth]
        for l in range(depth):
            @pl.when(pl.program_id(0) == l)
            def _(l=l):
                row = _restore_w_in(refs[l][...]) if restore else refs[l][...]
                for d in range(N_DEV):
                    o_ref[d % 2, d // 2] = row[:, d * cols:(d + 1) * cols]

    return _call(body, name, (depth, nt),
                 [pl.BlockSpec((tr, a.shape[1]), lambda l, i: (i, 0)) for a in per_layer],
                 pl.BlockSpec((2, N_DEV // 2, tr, cols), lambda l, i: (0, 0, l * nt + i, 0)),
                 _sds((2, N_DEV // 2, depth * rows, cols), per_layer[0].dtype))(*per_layer)


def _layer_consts(l, wf, wr, lower):
    t = lambda a: a.T
    k = {}
    k["n1w"], k["n2w"] = wr["norm1_w"][l][None], wr["norm2_w"][l][None]
    k["win"], k["win_t"] = wf["w_in"][l], t(wf["w_in"][l])
    for n in ("w_br_a", "w_br_b", "w_br_c", "w_out", "ffn_w_up", "ffn_w_down"):
        k[n], k[n + "_t"] = wf[n][l], t(wf[n][l])
    k["gdn_conv_w"], k["gdn_conv_b"] = wf["gdn_conv_w"][l], jnp.zeros((1, 1536), F32)
    k["ssd_conv_w"], k["ssd_conv_b"] = wf["ssd_conv_w"][l], wr["ssd_conv_b"][l][None]
    k["ffn_conv_w"], k["ffn_conv_b"] = wf["ffn_conv_w"][l], wr["ffn_conv_b"][l][None]
    k["gdn_a"], k["gdn_dt"] = _lane_row(wr["gdn_a_log"][l], SM_A), _lane_row(wr["gdn_dt_bias"][l], SM_A)
    k["gdn_nw"], k["hgrn_nw"] = wr["gdn_norm_w"][l][None], wr["hgrn_norm_w"][l][None]
    k["ssd_a"], k["ssd_dt"] = _lane_row(wr["ssd_a_log"][l], SM_DT), _lane_row(wr["ssd_dt_bias"][l], SM_DT)
    k["ssd_d"] = jnp.repeat(wr["ssd_d"][l], SSD_HEAD_DIM)[None]
    k["ssd_nw"] = wr["ssd_norm_w"][l][None]
    k["lb"] = lower[l:l + 1]
    return k


def _layer_fwd(l, x, mod, k):
    bsz, s, d = x.shape
    t = bsz * s
    sv = {"x": x}
    sv["mod"] = [mod[:, None, i * d:(i + 1) * d] for i in range(6)]
    sh1, sc1, g1, sh2, sc2, g2 = sv["mod"]
    h1 = _norm_mod_fwd(x, k["n1w"], sh1, sc1, f"norm1_fwd{l}")
    p = _mm(h1.reshape(t, d), k["win"], F32, f"mm_in{l}").reshape(bsz, s, P_WIDTH)
    qkv_act = _conv_fwd(p, P_QKV, 1536, k["gdn_conv_w"], k["gdn_conv_b"], f"gdn_conv_fwd{l}")
    oa, st_a, ti_a = _gdn_fwd(qkv_act, p, k["gdn_a"], k["gdn_dt"], k["gdn_nw"], f"gdn_fwd{l}")
    ob, st_b = _hgrn_fwd(p, k["lb"], k["hgrn_nw"], f"hgrn_fwd{l}")
    xbc_act = _conv_fwd(p, P_XBC, 1024, k["ssd_conv_w"], k["ssd_conv_b"], f"ssd_conv_fwd{l}")
    oc, st_c = _ssd_fwd(xbc_act, p, k["ssd_a"], k["ssd_dt"], k["ssd_d"], k["ssd_nw"], f"ssd_fwd{l}")
    merged = _merge_fwd(p, oa, ob, oc, k["w_br_a"], k["w_br_b"], k["w_br_c"], f"merge_fwd{l}")
    mix = _mm(merged.reshape(t, d), k["w_out"], F32, f"mm_out{l}").reshape(bsz, s, d)
    x1 = _resid_fwd(x, mix, g1, f"resid1_fwd{l}")
    h2 = _norm_mod_fwd(x1, k["n2w"], sh2, sc2, f"norm2_fwd{l}")
    u_pre = _mm(h2.reshape(t, d), k["ffn_w_up"], F32, f"mm_up{l}").reshape(bsz, s, 2 * FFN_HIDDEN)
    a = _conv_glu_fwd(u_pre, k["ffn_conv_w"], k["ffn_conv_b"], f"ffn_conv_glu_fwd{l}")
    ffn = _mm(a.reshape(t, FFN_HIDDEN), k["ffn_w_down"], F32, f"mm_down{l}").reshape(bsz, s, d)
    x2 = _resid_fwd(x1, ffn, g2, f"resid2_fwd{l}")
    sv.update(h1=h1, p=p, qkv_act=qkv_act, oa=oa, st_a=st_a, ti_a=ti_a, ob=ob, st_b=st_b, xbc_act=xbc_act, oc=oc, st_c=st_c,
              merged=merged, mix=mix, x1=x1, h2=h2, u_pre=u_pre, a=a, ffn=ffn)
    return x2, sv


def _layer_bwd(l, dx2, k, sv):
    bsz, s, d = dx2.shape
    t = bsz * s
    f2 = 2 * FFN_HIDDEN
    sh1, sc1, g1, sh2, sc2, g2 = sv["mod"]
    tr = lambda a: a.reshape(t, -1).T
    g = {}
    dffn, dg2 = _gate_bwd(dx2, sv["ffn"], g2, f"gate2_bwd{l}")
    dffn2 = dffn.reshape(t, d)
    da = _mm(dffn2, k["ffn_w_down_t"], F32, f"mm_down_dx{l}").reshape(bsz, s, FFN_HIDDEN)
    g["ffn_w_down"] = _mm(tr(sv["a"]), dffn2, BF16, f"mm_down_dw{l}")
    du_pre, g["ffn_conv_w"], dfcb = _conv_glu_bwd(da, sv["u_pre"], k["ffn_conv_w"], k["ffn_conv_b"], f"ffn_conv_glu_bwd{l}")
    g["ffn_conv_b"] = dfcb[0]
    du2 = du_pre.reshape(t, f2)
    dh2 = _mm(du2, k["ffn_w_up_t"], F32, f"mm_up_dx{l}").reshape(bsz, s, d)
    g["ffn_w_up"] = _mm(tr(sv["h2"]), du2, BF16, f"mm_up_dw{l}")
    dx1, dn2w, dsh2, dsc2 = _norm_mod_bwd(sv["x1"], k["n2w"], sh2, sc2, dh2, dx2, f"norm2_bwd{l}")
    g["norm2_w"] = dn2w[0]
    dmix, dg1 = _gate_bwd(dx1, sv["mix"], g1, f"gate1_bwd{l}")
    dmix2 = dmix.reshape(t, d)
    dmerged = _mm(dmix2, k["w_out_t"], F32, f"mm_out_dx{l}").reshape(bsz, s, d)
    g["w_out"] = _mm(tr(sv["merged"]), dmix2, BF16, f"mm_out_dw{l}")
    p = sv["p"]
    dp, doa, dob, doc, dya, dyb, dyc = _merge_bwd(
        dmerged, p, sv["oa"], sv["ob"], sv["oc"], k["w_br_a"], k["w_br_b"], k["w_br_c"],
        k["w_br_a_t"], k["w_br_b_t"], k["w_br_c_t"], f"merge_bwd{l}")
    g["w_br_a"] = _mm(tr(sv["oa"]), dya.reshape(t, d), BF16, f"mm_bra_dw{l}")
    g["w_br_b"] = _mm(tr(sv["ob"]), dyb.reshape(t, d), BF16, f"mm_brb_dw{l}")
    g["w_br_c"] = _mm(tr(sv["oc"]), dyc.reshape(t, d), BF16, f"mm_brc_dw{l}")
    dxbc_act, dsm_c, dp, da_c, ddt_c, dd_c, dnw_c = _ssd_bwd(
        doc, sv["xbc_act"], p, k["ssd_a"], k["ssd_dt"], k["ssd_d"], k["ssd_nw"], sv["st_c"], dp, f"ssd_bwd{l}")
    dp, g["ssd_conv_w"], dscb = _conv_bwd(dxbc_act, p, P_XBC, 1024, k["ssd_conv_w"], k["ssd_conv_b"], dp, f"ssd_conv_bwd{l}")
    g["ssd_conv_b"] = dscb[0]
    g["ssd_a_log"], g["ssd_dt_bias"] = da_c[0, SM_DT:SM_DT + 8], ddt_c[0, SM_DT:SM_DT + 8]
    g["ssd_d"] = dd_c.reshape(SSD_HEADS, SSD_HEAD_DIM).sum(axis=1)
    g["ssd_norm_w"] = dnw_c[0]
    dp, dlb, dnw_b = _hgrn_bwd(dob, p, k["lb"], k["hgrn_nw"], sv["st_b"], dp, f"hgrn_bwd{l}")
    g["hgrn_norm_w"] = dnw_b[0]
    dqkv_act, dsm_a, dp, da_a, ddt_a, dnw_a = _gdn_bwd(
        doa, sv["qkv_act"], p, k["gdn_a"], k["gdn_dt"], k["gdn_nw"], sv["st_a"], sv["ti_a"], dp, f"gdn_bwd{l}")
    dp, g["gdn_conv_w"], _ = _conv_bwd(dqkv_act, p, P_QKV, 1536, k["gdn_conv_w"], k["gdn_conv_b"], dp, f"gdn_conv_bwd{l}")
    g["gdn_a_log"], g["gdn_dt_bias"], g["gdn_norm_w"] = da_a[0, :4], ddt_a[0, :4], dnw_a[0]
    dp = _small_cols(dsm_a, dsm_c, dp, f"small_cols{l}").reshape(t, P_WIDTH)
    dh1 = _mm(dp, k["win_t"], F32, f"mm_in_dx{l}").reshape(bsz, s, d)
    g["w_in"] = _mm(tr(sv["h1"]), dp, BF16, f"mm_in_dw{l}")
    dx, dn1w, dsh1, dsc1 = _norm_mod_bwd(sv["x"], k["n1w"], sh1, sc1, dh1, dx1, f"norm1_bwd{l}")
    g["norm1_w"] = dn1w[0]
    dmod = jnp.concatenate([dsh1, dsc1, dg1, dsh2, dsc2, dg2], axis=-1)[:, 0]
    return dx, g, dlb, dmod


def _local_step(x, mod, wf, wr, target):
    lower = _lb_fwd(wr["hgrn_lb_param"])
    ks = [_layer_consts(l, wf, wr, lower) for l in range(DEPTH)]
    saved = []
    h = x
    for l in range(DEPTH):
        h, sv = _layer_fwd(l, h, mod[l], ks[l])
        saved.append(sv)
    loss8, dh, dfnw = _final_loss(h, wr["final_norm_w"][None], target)
    per_layer, dlbs, dmods = [None] * DEPTH, [None] * DEPTH, [None] * DEPTH
    for l in reversed(range(DEPTH)):
        dh, per_layer[l], dlbs[l], dmods[l] = _layer_bwd(l, dh, ks[l], saved[l])
    grads = {n: [per_layer[l][n] for l in range(DEPTH)] for n in per_layer[0]}
    grads = {n: g if n in UNALIGNED else jnp.stack(g) for n, g in grads.items()}
    grads["hgrn_lb_param"] = _lb_bwd(wr["hgrn_lb_param"], jnp.concatenate(dlbs, axis=0))
    grads["final_norm_w"] = dfnw[0]
    return loss8[0, 0], dh, grads, jnp.stack(dmods)


def kernel(x, c, w_ada, b_ada, norm1_w, w_in, gdn_conv_w, gdn_a_log, gdn_dt_bias, gdn_norm_w, hgrn_lb_param, hgrn_norm_w, ssd_conv_w, ssd_conv_b, ssd_a_log, ssd_dt_bias, ssd_d, ssd_norm_w, w_br_a, w_br_b, w_br_c, w_out, norm2_w, ffn_w_up, ffn_conv_w, ffn_conv_b, ffn_w_down, final_norm_w, loss_target, m_w_ada, m_b_ada, m_norm1_w, m_w_in, m_gdn_conv_w, m_gdn_a_log, m_gdn_dt_bias, m_gdn_norm_w, m_hgrn_lb_param, m_hgrn_norm_w, m_ssd_conv_w, m_ssd_conv_b, m_ssd_a_log, m_ssd_dt_bias, m_ssd_d, m_ssd_norm_w, m_w_br_a, m_w_br_b, m_w_br_c, m_w_out, m_norm2_w, m_ffn_w_up, m_ffn_conv_w, m_ffn_conv_b, m_ffn_w_down, m_final_norm_w, v_w_ada, v_b_ada, v_norm1_w, v_w_in, v_gdn_conv_w, v_gdn_a_log, v_gdn_dt_bias, v_gdn_norm_w, v_hgrn_lb_param, v_hgrn_norm_w, v_ssd_conv_w, v_ssd_conv_b, v_ssd_a_log, v_ssd_dt_bias, v_ssd_d, v_ssd_norm_w, v_w_br_a, v_w_br_b, v_w_br_c, v_w_out, v_norm2_w, v_ffn_w_up, v_ffn_conv_w, v_ffn_conv_b, v_ffn_w_down, v_final_norm_w):
    given = dict(locals())
    w = {n: given[n] for n in WEIGHTS}
    m = {n: given["m_" + n] for n in WEIGHTS}
    v = {n: given["v_" + n] for n in WEIGHTS}
    me = _my_index()
    bsz = c.shape[0]
    ncol = 6 * D_MODEL // N_DEV

    shards = [w[n].astype(BF16) if n in MATMUL_WEIGHTS else w[n] for n, _, _ in SPLIT] + [c]
    gathered = _gather_two_level(shards, "gather_weights")
    wf = {n: _join_cols(g, n == "w_in", f"join_{n}") if n in UNALIGNED else _join_blocks(g, shape, axis)
          for (n, shape, axis), g in zip(SPLIT, gathered)}
    c_all = gathered[-1].reshape(N_DEV * bsz, D_MODEL)

    b_cols = lax.dynamic_slice_in_dim(b_ada, me * ncol, ncol, axis=1)[:, None]
    mod_cols = _ada_fwd(c_all, w_ada, b_cols)
    send = mod_cols.reshape(DEPTH, N_DEV, bsz, ncol).transpose(1, 0, 2, 3)
    got = _all_to_all([send], "scatter_mod")[0]
    mod = got.transpose(1, 2, 0, 3).reshape(DEPTH, bsz, 6 * D_MODEL)

    loss, dx, grads, dmod = _local_step(x, mod, wf, w, loss_target)

    send = dmod.reshape(DEPTH, bsz, N_DEV, ncol).transpose(2, 0, 1, 3)
    got_dmod = _all_to_all([send], "scatter_dmod")[0]
    dmod_all = got_dmod.transpose(1, 0, 2, 3).reshape(DEPTH, N_DEV * bsz, ncol)
    g_w_ada, g_b_cols = _ada_bwd(c_all.T, dmod_all)

    core = lax.axis_index("c")
    by_core = []
    for n, shape, axis in SPLIT:
        if n in UNALIGNED:
            by_core.append(_split_cols(grads[n], n == "w_in", shape[axis] // N_DEV, f"split_{n}"))
            continue
        parts = _split_blocks(grads[n], shape, axis).astype(BF16)
        parts = parts.reshape((N_DEV // 2, 2, -1, parts.shape[-1]))
        by_core.append(jnp.swapaxes(parts, 0, 1))
    from_sibling = _sibling_swap(by_core, "swap_grads")
    sums = [_pair_sum(lax.dynamic_index_in_dim(mine, core, 0, keepdims=False), theirs, f"pair_sum_{n}")
            for (n, _, _), mine, theirs in zip(SPLIT, by_core, from_sibling)]
    got = _chip_scatter(sums, "scatter_grads")
    grads["b_ada"] = lax.dynamic_update_slice_in_dim(jnp.zeros_like(b_ada), g_b_cols[:, 0], me * ncol, axis=1)

    out = {}
    slots = [(n, g8) for (n, _, _), g8 in zip(SPLIT, got)] + [("w_ada", g_w_ada[None])]
    for n, gs in slots:
        out[n] = _sum_adamw(gs.reshape((gs.shape[0],) + w[n].shape), w[n], m[n], v[n], f"adamw_{n}")
    r8 = _gather_two_level([_pack_repl(grads)], "gather_small_grads")[0]
    res = _sum_adamw(r8[:, None], _pack_repl(w)[None], _pack_repl(m)[None], _pack_repl(v)[None], "adamw_repl")
    repl_out = [_unpack_repl(o[0]) for o in res]
    pick = lambda i, n: out[n][i] if n in out else repl_out[i][n]
    loss = lax.psum(loss, ("x", "y", "c"))
    return (loss, dx, *[pick(i, n) for i in range(4) for n in WEIGHTS])
```

```python
import functools
import math

import jax
import jax.numpy as jnp
from jax import lax
from jax.experimental import pallas as pl
from jax.experimental.pallas import tpu as pltpu

F32, BF16 = jnp.float32, jnp.bfloat16
HI = lax.Precision.HIGHEST
MESH_ID = pl.DeviceIdType.MESH

N_DEV = 8
EPS = 1e-6
D_MODEL = 1024
DEPTH = 2
GDN_HEADS, GDN_DK, GDN_CHUNK = 4, 128, 64
HGRN_HEADS, HGRN_DK, HGRN_CHUNK, HGRN_BLOCK = 4, 128, 16, 128
SSD_HEADS, SSD_HEAD_DIM, SSD_GROUPS, SSD_STATE, SSD_CHUNK = 8, 64, 2, 128, 64
SSD_INNER = SSD_HEADS * SSD_HEAD_DIM
FFN_HIDDEN = 2816
LANES = 128
P_QKV, P_GZ, P_XBC, P_GATE, P_HQ, P_HF, P_HI, P_HG, P_SZ, P_SMALL, P_WIDTH = (
    0, 1536, 2048, 3072, 6144, 6656, 7168, 7680, 8192, 8704, 9216)
SM_A, SM_B, SM_DT = 0, 4, 8
W_IN_SPLITS = (1536, 4, 4, 512, 512, 512, 512, 512, 512, 1024, 8, 3072)

ADAM_LR, ADAM_B1, ADAM_B2, ADAM_EPS, ADAM_WD, ADAM_STEP = 0.001, 0.9, 0.999, 1e-08, 0.01, 10

V7X_VMEM_LIMIT = 56 * 1024 * 1024
MM_OPERAND_VMEM = 34 * 1024 * 1024
PACK_W = 1024


def _call(body, name, grid, in_specs, out_specs, out_shape, scratch=(), aliases=None):
    return pl.pallas_call(
        body, name=name, grid=grid, in_specs=in_specs, out_specs=out_specs, out_shape=out_shape,
        scratch_shapes=list(scratch), input_output_aliases=aliases or {},
        compiler_params=pltpu.CompilerParams(
            dimension_semantics=("arbitrary",) * len(grid), vmem_limit_bytes=V7X_VMEM_LIMIT),
    )


IN_PLACE = pl.BlockSpec(memory_space=pl.ANY)


def _pick(n, cands):
    for c in cands:
        if n % c == 0:
            return c
    raise ValueError(f"no tile for {n} among {cands}")


def _row_tile(s, cap):
    t = cap
    while s % t:
        t //= 2
    return t


def _sds(shape, dtype):
    return jax.ShapeDtypeStruct(shape, dtype)


def _dot(a, b):
    return lax.dot_general(a, b, (((1,), (0,)), ((), ())), precision=HI, preferred_element_type=F32)


NN, NT, TN = (((1,), (0,)), ((), ())), (((1,), (1,)), ((), ())), (((0,), (0,)), ((), ()))


def _mxu(a, b, dims):
    return lax.dot_general(a.astype(BF16), b.astype(BF16), dims, preferred_element_type=F32)


@jax.custom_vjp
def _bdot(a, b):
    return _mxu(a, b, NN)


@jax.custom_vjp
def _bdot_nt(a, b):
    return _mxu(a, b, NT)


@jax.custom_vjp
def _bdot_tn(a, b):
    return _mxu(a, b, TN)


_bdot.defvjp(lambda a, b: (_mxu(a, b, NN), (a, b)), lambda r, d: (_mxu(d, r[1], NT), _mxu(r[0], d, TN)))
_bdot_nt.defvjp(lambda a, b: (_mxu(a, b, NT), (a, b)), lambda r, d: (_mxu(d, r[1], NN), _mxu(d, r[0], TN)))
_bdot_tn.defvjp(lambda a, b: (_mxu(a, b, TN), (a, b)), lambda r, d: (_mxu(r[1], d, NT), _mxu(r[0], d, NN)))


def _split(x):
    hi = x.astype(BF16)
    return hi, (x - hi.astype(F32)).astype(BF16)


def _mxu3(a, b, dims):
    ah, al = _split(a)
    bh, bl = _split(b)
    return _mxu(ah, bh, dims) + (_mxu(ah, bl, dims) + _mxu(al, bh, dims))


@jax.custom_vjp
def _dot3(a, b):
    return _mxu3(a, b, NN)


_dot3.defvjp(lambda a, b: (_mxu3(a, b, NN), (a, b)), lambda r, d: (_mxu3(d, r[1], NT), _mxu3(r[0], d, TN)))


def _pieces(x):
    x1 = x.astype(BF16)
    r1 = x - x1.astype(F32)
    x2 = r1.astype(BF16)
    return x1, x2, (r1 - x2.astype(F32)).astype(BF16)


def _mask_mxu(mask, x, dims):
    x1, x2, x3 = _pieces(x)
    return _mxu(mask, x1, dims) + (_mxu(mask, x2, dims) + _mxu(mask, x3, dims))


def _spread_mxu(x, mask, dims):
    x1, x2, x3 = _pieces(x)
    return _mxu(x1, mask, dims) + (_mxu(x2, mask, dims) + _mxu(x3, mask, dims))


@jax.custom_vjp
def _mask_dot(mask, x):
    return _mask_mxu(mask, x, NN)


@jax.custom_vjp
def _spread_dot(x, mask):
    return _spread_mxu(x, mask, NN)


_mask_dot.defvjp(lambda m, x: (_mask_mxu(m, x, NN), m), lambda m, d: (jnp.zeros_like(m), _mask_mxu(m, d, TN)))
_spread_dot.defvjp(lambda x, m: (_spread_mxu(x, m, NN), m), lambda m, d: (_spread_mxu(d, m, NT), jnp.zeros_like(m)))


def _iota(shape, axis):
    return lax.broadcasted_iota(jnp.int32, shape, axis)


def _silu(x):
    return x * jax.nn.sigmoid(x)


def _softplus(x):
    return jnp.maximum(x, 0.0) + jnp.log1p(jnp.exp(-jnp.abs(x)))


def _rms(x, w):
    return x * lax.rsqrt(jnp.mean(x * x, axis=-1, keepdims=True) + EPS) * w


def _lane_col(x, lane):
    m = (_iota(x.shape, 1) == lane).astype(F32)
    return jnp.sum(x * m, axis=1, keepdims=True)


def _col_to_row(c):
    n = c.shape[0]
    eye = (_iota((n, n), 0) == _iota((n, n), 1)).astype(F32)
    return jnp.sum(c * eye, axis=0, keepdims=True)


def _tril(n, strict=False):
    r, c = _iota((n, n), 0), _iota((n, n), 1)
    return (r > c) if strict else (r >= c)


def _mm(a, b, out_dtype, name):
    m, k = a.shape
    n = b.shape[1]
    tm = _pick(m, (1024, 1408, 512, 256, 128, 64, 32, 16, 8))
    tn = _pick(n, (1024, 1408, 768, 512, 384, 256, 128))
    fits = lambda c: k % c == 0 and (tm + tn) * c * 2 * 2 <= MM_OPERAND_VMEM
    tk = next(c for c in (k, 4096, 3072, 2816, 2048, 1024, 768, 512, 384, 256, 128) if fits(c))
    nk = k // tk

    def body_one(a_ref, b_ref, o_ref):
        o_ref[...] = _bdot(a_ref[...], b_ref[...]).astype(out_dtype)

    def body(a_ref, b_ref, o_ref, acc_ref):
        kk = pl.program_id(2)

        @pl.when(kk == 0)
        def _():
            acc_ref[...] = jnp.zeros_like(acc_ref)

        acc_ref[...] += _bdot(a_ref[...], b_ref[...])

        @pl.when(kk == nk - 1)
        def _():
            o_ref[...] = acc_ref[...].astype(out_dtype)

    return _call(
        body_one if nk == 1 else body, name, (m // tm, n // tn, nk),
        [pl.BlockSpec((tm, tk), lambda i, j, kk: (i, kk)), pl.BlockSpec((tk, tn), lambda i, j, kk: (kk, j))],
        pl.BlockSpec((tm, tn), lambda i, j, kk: (i, j)), _sds((m, n), out_dtype),
        scratch=[] if nk == 1 else [pltpu.VMEM((tm, tn), F32)],
    )(a, b)


def _mm_resid(a, b, x, gate, name):
    bsz, s, d = x.shape
    k = a.shape[1]
    tm = _row_tile(s, 1024)
    tn = _pick(d, (1024, 512, 256, 128))
    assert (tm + tn) * k * 2 * 2 <= MM_OPERAND_VMEM, "one K step only"
    per_seq = s // tm

    def body(a_ref, b_ref, x_ref, g_ref, y_ref, o_ref):
        y = _bdot(a_ref[...], b_ref[...])
        y_ref[...] = y
        o_ref[...] = x_ref[...] + g_ref[...] * y

    x2, rows = x.reshape(bsz * s, d), pl.BlockSpec((tm, tn), lambda i, j: (i, j))
    y, out = _call(
        body, name, (bsz * per_seq, d // tn),
        [pl.BlockSpec((tm, k), lambda i, j: (i, 0)), pl.BlockSpec((k, tn), lambda i, j: (0, j)), rows,
         pl.BlockSpec((None, 1, tn), lambda i, j: (i // per_seq, 0, j))],
        [rows, rows], [_sds((bsz * s, d), F32)] * 2,
    )(a, b, x2, gate)
    return y.reshape(x.shape), out.reshape(x.shape)


def _ada_fwd(c_all, w, b):
    depth, _, n = w.shape
    rows = c_all.shape[0]

    def body(c_ref, w_ref, b_ref, o_ref):
        o_ref[...] = _dot(_silu(c_ref[...]), w_ref[...]) + b_ref[...]

    return _call(
        body, "ada_fwd", (depth,),
        [pl.BlockSpec((rows, D_MODEL), lambda l: (0, 0)), pl.BlockSpec((None, D_MODEL, n), lambda l: (l, 0, 0)),
         pl.BlockSpec((None, 1, n), lambda l: (l, 0, 0))],
        pl.BlockSpec((None, rows, n), lambda l: (l, 0, 0)), _sds((depth, rows, n), F32),
    )(c_all, w, b)


def _ada_bwd(c_all_t, dmod):
    depth, rows, n = dmod.shape

    def body(ct_ref, dm_ref, dw_ref, db_ref):
        dm = dm_ref[...]
        dw_ref[...] = _dot(_silu(ct_ref[...]), dm)
        db_ref[...] = jnp.sum(dm, axis=0, keepdims=True)

    return _call(
        body, "ada_bwd", (depth,),
        [pl.BlockSpec((D_MODEL, rows), lambda l: (0, 0)), pl.BlockSpec((None, rows, n), lambda l: (l, 0, 0))],
        [pl.BlockSpec((None, D_MODEL, n), lambda l: (l, 0, 0)), pl.BlockSpec((None, 1, n), lambda l: (l, 0, 0))],
        [_sds((depth, D_MODEL, n), F32), _sds((depth, 1, n), F32)],
    )(c_all_t, dmod)


def _lb_fn(p):
    rows = [p[l:l + 1] for l in range(DEPTH)]
    mx = functools.reduce(jnp.maximum, rows)
    es = [jnp.exp(r - mx) for r in rows]
    tot = functools.reduce(lambda a, b: a + b, es)
    sm = [e / tot for e in es]
    out, run = [], None
    for l in range(DEPTH):
        run = sm[l] if run is None else run + sm[l]
        out.append(run - sm[0])
    return jnp.concatenate(out, axis=0)


def _lb_fwd(p):
    def body(p_ref, o_ref):
        o_ref[...] = _lb_fn(p_ref[...])

    full = pl.BlockSpec(p.shape, lambda i: (0, 0))
    return _call(body, "lb_fwd", (1,), [full], full, _sds(p.shape, F32))(p)


def _lb_bwd(p, d_lower):
    def body(p_ref, d_ref, o_ref):
        _, vjp = jax.vjp(_lb_fn, p_ref[...])
        o_ref[...] = vjp(d_ref[...])[0]

    full = pl.BlockSpec(p.shape, lambda i: (0, 0))
    return _call(body, "lb_bwd", (1,), [full, full], full, _sds(p.shape, F32))(p, d_lower)


def _norm_mod_fn(x, w, shift, scale):
    return _rms(x, w) * (1.0 + scale) + shift


def _norm_mod_fwd(x, w, shift, scale, name):
    bsz, s, d = x.shape
    ts = _row_tile(s, 512)

    def body(x_ref, w_ref, sh_ref, sc_ref, o_ref):
        o_ref[...] = _norm_mod_fn(x_ref[...], w_ref[...], sh_ref[...], sc_ref[...]).astype(BF16)

    row = pl.BlockSpec((None, ts, d), lambda b, i: (b, i, 0))
    per_b = pl.BlockSpec((None, 1, d), lambda b, i: (b, 0, 0))
    return _call(body, name, (bsz, s // ts), [row, pl.BlockSpec((1, d), lambda b, i: (0, 0)), per_b, per_b],
                 row, _sds(x.shape, BF16))(x, w, shift, scale)


def _norm_mod_bwd(x, w, shift, scale, dh, carry, name):
    bsz, s, d = x.shape
    ts = _row_tile(s, 512)

    def body(x_ref, w_ref, sh_ref, sc_ref, dh_ref, c_ref, dx_ref, dw_ref, dsh_ref, dsc_ref):
        b, i = pl.program_id(0), pl.program_id(1)
        _, vjp = jax.vjp(_norm_mod_fn, x_ref[...], w_ref[...], sh_ref[...], sc_ref[...])
        dx, dw, dsh, dsc = vjp(dh_ref[...])
        dx_ref[...] = dx + c_ref[...]

        @pl.when((b == 0) & (i == 0))
        def _():
            dw_ref[...] = jnp.zeros_like(dw_ref)

        @pl.when(i == 0)
        def _():
            dsh_ref[...] = jnp.zeros_like(dsh_ref)
            dsc_ref[...] = jnp.zeros_like(dsc_ref)

        dw_ref[...] += dw
        dsh_ref[...] += dsh
        dsc_ref[...] += dsc

    row = pl.BlockSpec((None, ts, d), lambda b, i: (b, i, 0))
    per_b = pl.BlockSpec((None, 1, d), lambda b, i: (b, 0, 0))
    wspec = pl.BlockSpec((1, d), lambda b, i: (0, 0))
    return _call(body, name, (bsz, s // ts), [row, wspec, per_b, per_b, row, row],
                 [row, wspec, per_b, per_b],
                 [_sds(x.shape, F32), _sds((1, d), F32), _sds((bsz, 1, d), F32), _sds((bsz, 1, d), F32)],
                 )(x, w, shift, scale, dh, carry)


def _gate_bwd(dx, y, gate, name):
    bsz, s, d = dx.shape
    ts = _row_tile(s, 1024)

    def body(dx_ref, y_ref, g_ref, dy_ref, dg_ref):
        dxv = dx_ref[...]
        dy_ref[...] = (dxv * g_ref[...]).astype(BF16)

        @pl.when(pl.program_id(1) == 0)
        def _():
            dg_ref[...] = jnp.zeros_like(dg_ref)

        dg_ref[...] += jnp.sum(dxv * y_ref[...], axis=0, keepdims=True)

    row = pl.BlockSpec((None, ts, d), lambda b, i: (b, i, 0))
    per_b = pl.BlockSpec((None, 1, d), lambda b, i: (b, 0, 0))
    return _call(body, name, (bsz, s // ts), [row, row, per_b], [row, per_b],
                 [_sds(dx.shape, BF16), _sds((bsz, 1, d), F32)])(dx, y, gate)


HALO = 8


def _conv_pre(xx, w_ref, b_ref, kw, rows):
    acc = w_ref[kw - 1:kw, :] * xx[HALO:HALO + rows]
    for k in range(kw - 1):
        acc = acc + w_ref[k:k + 1, :] * pltpu.roll(xx, kw - 1 - k, 0)[HALO:HALO + rows]
    return acc + b_ref[...]


def _conv_fwd(x, col0, width, w, b, name):
    bsz, s, _ = x.shape
    kw = w.shape[0]
    ts = _row_tile(s, 1024)
    tc = _pick(width, (512, 256, 128))
    assert col0 % tc == 0
    c0 = col0 // tc
    hb = ts // HALO

    def body(x_ref, xp_ref, w_ref, b_ref, o_ref):
        i = pl.program_id(1)
        xp = jnp.where(i > 0, xp_ref[...], 0.0)
        xx = jnp.concatenate([xp, x_ref[...]], axis=0)
        pre = _conv_pre(xx, w_ref, b_ref, kw, ts)
        o_ref[...] = _silu(pre)

    return _call(
        body, name, (bsz, s // ts, width // tc),
        [pl.BlockSpec((None, ts, tc), lambda bb, i, j: (bb, i, c0 + j)),
         pl.BlockSpec((None, HALO, tc), lambda bb, i, j: (bb, jnp.maximum(i * hb - 1, 0), c0 + j)),
         pl.BlockSpec((kw, tc), lambda bb, i, j: (0, j)), pl.BlockSpec((1, tc), lambda bb, i, j: (0, j))],
        pl.BlockSpec((None, ts, tc), lambda bb, i, j: (bb, i, j)), _sds((bsz, s, width), F32),
    )(x, x, w, b)


def _conv_bwd(dy, x, col0, width, w, b, dp, name):
    bsz, s, _ = x.shape
    kw = w.shape[0]
    ts = _row_tile(s, 1024)
    tc = _pick(width, (512, 256, 128))
    c0 = col0 // tc
    hb = ts // HALO
    nt = s // ts
    last_h = s // HALO - 1

    def body(x_ref, xp_ref, xn_ref, dy_ref, dyn_ref, w_ref, b_ref, _, dx_ref, dw_ref, db_ref):
        bb, i = pl.program_id(1), pl.program_id(2)
        xp = jnp.where(i > 0, xp_ref[...], 0.0)
        xx = jnp.concatenate([xp, x_ref[...], xn_ref[...]], axis=0)
        dyy = jnp.concatenate([dy_ref[...], jnp.where(i < nt - 1, dyn_ref[...], 0.0)], axis=0)
        n = ts + HALO
        pre = _conv_pre(xx, w_ref, b_ref, kw, n)
        sg = jax.nn.sigmoid(pre)
        dpre = dyy * (sg * (1.0 + pre * (1.0 - sg)))
        dx = w_ref[kw - 1:kw, :] * dpre[:ts]
        for k in range(kw - 1):
            dx = dx + w_ref[k:k + 1, :] * pltpu.roll(dpre, n - (kw - 1 - k), 0)[:ts]
        dx_ref[...] = dx.astype(BF16)

        @pl.when((bb == 0) & (i == 0))
        def _():
            dw_ref[...] = jnp.zeros_like(dw_ref)
            db_ref[...] = jnp.zeros_like(db_ref)

        dt = dpre[:ts]
        db_ref[...] += jnp.sum(dt, axis=0, keepdims=True)
        dw_ref[kw - 1:kw, :] += jnp.sum(dt * xx[HALO:HALO + ts], axis=0, keepdims=True)
        for k in range(kw - 1):
            xs = pltpu.roll(xx, kw - 1 - k, 0)[HALO:HALO + ts]
            dw_ref[k:k + 1, :] += jnp.sum(dt * xs, axis=0, keepdims=True)

    xspec = lambda f: pl.BlockSpec((None, HALO, tc), f)
    return _call(
        body, name, (width // tc, bsz, nt),
        [pl.BlockSpec((None, ts, tc), lambda j, bb, i: (bb, i, c0 + j)),
         xspec(lambda j, bb, i: (bb, jnp.maximum(i * hb - 1, 0), c0 + j)),
         xspec(lambda j, bb, i: (bb, jnp.minimum((i + 1) * hb, last_h), c0 + j)),
         pl.BlockSpec((None, ts, tc), lambda j, bb, i: (bb, i, j)),
         xspec(lambda j, bb, i: (bb, jnp.minimum((i + 1) * hb, last_h), j)),
         pl.BlockSpec((kw, tc), lambda j, bb, i: (0, j)), pl.BlockSpec((1, tc), lambda j, bb, i: (0, j)), IN_PLACE],
        [pl.BlockSpec((None, ts, tc), lambda j, bb, i: (bb, i, c0 + j)),
         pl.BlockSpec((kw, tc), lambda j, bb, i: (0, j)), pl.BlockSpec((1, tc), lambda j, bb, i: (0, j))],
        [_sds(dp.shape, BF16), _sds((kw, width), F32), _sds((1, width), F32)], aliases={7: 0},
    )(x, x, x, dy, dy, w, b, dp)


def _conv_glu_fwd(x, w, b, name):
    bsz, s, f2 = x.shape
    f = f2 // 2
    kw = w.shape[0]
    ts = _row_tile(s, 2048)
    tc = _pick(f, (256, 128))
    nf = f // tc
    hb = ts // HALO

    def body(xg_ref, xgp_ref, xv_ref, xvp_ref, wg_ref, wv_ref, bg_ref, bv_ref, o_ref):
        i = pl.program_id(1)
        halves = []
        for x_ref, xp_ref, w_ref, b_ref in ((xg_ref, xgp_ref, wg_ref, bg_ref), (xv_ref, xvp_ref, wv_ref, bv_ref)):
            xx = jnp.concatenate([jnp.where(i > 0, xp_ref[...], 0.0), x_ref[...]], axis=0)
            halves.append(_conv_pre(xx, w_ref, b_ref, kw, ts))
        o_ref[...] = (_silu(halves[0]) * halves[1]).astype(BF16)

    tile = lambda off: pl.BlockSpec((None, ts, tc), lambda bb, i, j: (bb, i, off + j))
    prev = lambda off: pl.BlockSpec((None, HALO, tc), lambda bb, i, j: (bb, jnp.maximum(i * hb - 1, 0), off + j))
    wsp = lambda rows, off: pl.BlockSpec((rows, tc), lambda bb, i, j: (0, off + j))
    return _call(
        body, name, (bsz, s // ts, nf),
        [tile(0), prev(0), tile(nf), prev(nf), wsp(kw, 0), wsp(kw, nf), wsp(1, 0), wsp(1, nf)],
        pl.BlockSpec((None, ts, tc), lambda bb, i, j: (bb, i, j)), _sds((bsz, s, f), BF16),
    )(x, x, x, x, w, w, b, b)


def _conv_glu_bwd(da, x, w, b, name):
    bsz, s, f2 = x.shape
    f = f2 // 2
    kw = w.shape[0]
    ts = _row_tile(s, 2048)
    tc = _pick(f, (256, 128))
    nf = f // tc
    hb = ts // HALO
    nt = s // ts
    last_h = s // HALO - 1
    n = ts + HALO

    def body(xg_ref, xgp_ref, xgn_ref, xv_ref, xvp_ref, xvn_ref, da_ref, dan_ref,
             wg_ref, wv_ref, bg_ref, bv_ref, wx_ref, dx_ref, dw_ref, db_ref):
        j, bb, i = pl.program_id(0), pl.program_id(1), pl.program_id(2)
        day = jnp.concatenate([da_ref[...], jnp.where(i < nt - 1, dan_ref[...], 0.0)], axis=0)
        xg = jnp.concatenate([jnp.where(i > 0, xgp_ref[...], 0.0), xg_ref[...], xgn_ref[...]], axis=0)
        pre_g = _conv_pre(xg, wg_ref, bg_ref, kw, n)
        sg = jax.nn.sigmoid(pre_g)

        @pl.when((bb == 0) & (i == 0))
        def _():
            dw_ref[...] = jnp.zeros_like(dw_ref)
            db_ref[...] = jnp.zeros_like(db_ref)

        def finish(dpre, xx):
            dx = wx_ref[kw - 1:kw, :] * dpre[:ts]
            for k in range(kw - 1):
                dx = dx + wx_ref[k:k + 1, :] * pltpu.roll(dpre, n - (kw - 1 - k), 0)[:ts]
            dx_ref[...] = dx.astype(BF16)
            dt = dpre[:ts]
            db_ref[...] += jnp.sum(dt, axis=0, keepdims=True)
            dw_ref[kw - 1:kw, :] += jnp.sum(dt * xx[HALO:HALO + ts], axis=0, keepdims=True)
            for k in range(kw - 1):
                dw_ref[k:k + 1, :] += jnp.sum(dt * pltpu.roll(xx, kw - 1 - k, 0)[HALO:HALO + ts], axis=0, keepdims=True)

        @pl.when(j < nf)
        def _():
            xv = jnp.concatenate([jnp.where(i > 0, xvp_ref[...], 0.0), xv_ref[...], xvn_ref[...]], axis=0)
            pre_v = _conv_pre(xv, wv_ref, bv_ref, kw, n)
            finish(day * pre_v * (sg * (1.0 + pre_g * (1.0 - sg))), xg)

        @pl.when(j >= nf)
        def _():
            xv = jnp.concatenate([jnp.where(i > 0, xvp_ref[...], 0.0), xv_ref[...], xvn_ref[...]], axis=0)
            finish(day * (pre_g * sg), xv)

    tile = lambda off: pl.BlockSpec((None, ts, tc), lambda j, bb, i: (bb, i, off + j % nf))
    prev = lambda off: pl.BlockSpec((None, HALO, tc), lambda j, bb, i: (bb, jnp.maximum(i * hb - 1, 0), off + j % nf))
    nxt = lambda off: pl.BlockSpec((None, HALO, tc), lambda j, bb, i: (bb, jnp.minimum((i + 1) * hb, last_h), off + j % nf))
    wsp = lambda rows, off: pl.BlockSpec((rows, tc), lambda j, bb, i: (0, off + j % nf))
    own = lambda rows: pl.BlockSpec((rows, tc), lambda j, bb, i: (0, j))
    return _call(
        body, name, (2 * nf, bsz, nt),
        [tile(0), prev(0), nxt(0), tile(nf), prev(nf), nxt(nf), tile(0), nxt(0),
         wsp(kw, 0), wsp(kw, nf), wsp(1, 0), wsp(1, nf), own(kw)],
        [pl.BlockSpec((None, ts, tc), lambda j, bb, i: (bb, i, j)), own(kw), own(1)],
        [_sds((bsz, s, f2), BF16), _sds((kw, f2), F32), _sds((1, f2), F32)],
    )(x, x, x, x, x, x, da, da, w, w, b, b, w)


def _merge_fwd(p, oa, ob, oc, wa, wb, wc, name):
    bsz, s, _ = p.shape
    tm = _row_tile(s, 512)
    gblk = P_GATE // (3 * D_MODEL)

    def body(g_ref, oa_ref, ob_ref, oc_ref, wa_ref, wb_ref, wc_ref, o_ref):
        acc = None
        for i, (o_r, w_r) in enumerate(((oa_ref, wa_ref), (ob_ref, wb_ref), (oc_ref, wc_ref))):
            y = _bdot(o_r[...], w_r[...])
            t = jax.nn.sigmoid(g_ref[:, i * D_MODEL:(i + 1) * D_MODEL]) * y
            acc = t if acc is None else acc + t
        o_ref[...] = acc.astype(BF16)

    orow = pl.BlockSpec((None, tm, 512), lambda b, i: (b, i, 0))
    wfull = pl.BlockSpec((512, D_MODEL), lambda b, i: (0, 0))
    return _call(
        body, name, (bsz, s // tm),
        [pl.BlockSpec((None, tm, 3 * D_MODEL), lambda b, i: (b, i, gblk)), orow, orow, orow, wfull, wfull, wfull],
        pl.BlockSpec((None, tm, D_MODEL), lambda b, i: (b, i, 0)), _sds((bsz, s, D_MODEL), BF16),
    )(p, oa, ob, oc, wa, wb, wc)


def _merge_bwd(dm, p, oa, ob, oc, wa, wb, wc, wat, wbt, wct, name):
    bsz, s, _ = p.shape
    tm = _row_tile(s, 512)
    gblk = P_GATE // (3 * D_MODEL)

    def body(dm_ref, g_ref, oa_ref, ob_ref, oc_ref, wa_ref, wb_ref, wc_ref, wat_ref, wbt_ref, wct_ref,
             dg_ref, doa_ref, dob_ref, doc_ref, dya_ref, dyb_ref, dyc_ref):
        dmv = dm_ref[...]
        trip = ((oa_ref, wa_ref, wat_ref, doa_ref, dya_ref), (ob_ref, wb_ref, wbt_ref, dob_ref, dyb_ref),
                (oc_ref, wc_ref, wct_ref, doc_ref, dyc_ref))
        for i, (o_r, w_r, wt_r, do_r, dy_r) in enumerate(trip):
            y = _bdot(o_r[...], w_r[...])
            sg = jax.nn.sigmoid(g_ref[:, i * D_MODEL:(i + 1) * D_MODEL])
            dg_ref[:, i * D_MODEL:(i + 1) * D_MODEL] = (dmv * y * sg * (1.0 - sg)).astype(BF16)
            dy = (dmv * sg).astype(BF16)
            dy_r[...] = dy
            do_r[...] = _bdot(dy, wt_r[...])

    orow = pl.BlockSpec((None, tm, 512), lambda b, i: (b, i, 0))
    drow = pl.BlockSpec((None, tm, D_MODEL), lambda b, i: (b, i, 0))
    grow = pl.BlockSpec((None, tm, 3 * D_MODEL), lambda b, i: (b, i, gblk))
    wfull = pl.BlockSpec((512, D_MODEL), lambda b, i: (0, 0))
    wtfull = pl.BlockSpec((D_MODEL, 512), lambda b, i: (0, 0))
    return _call(
        body, name, (bsz, s // tm),
        [drow, grow, orow, orow, orow, wfull, wfull, wfull, wtfull, wtfull, wtfull],
        [grow, orow, orow, orow, drow, drow, drow],
        [_sds(p.shape, BF16)] + [_sds((bsz, s, 512), F32)] * 3 + [_sds((bsz, s, D_MODEL), BF16)] * 3,
    )(dm, p, oa, ob, oc, wa, wb, wc, wat, wbt, wct)


def _final_loss(x, w, target):
    bsz, s, d = x.shape
    ts = _row_tile(s, 512)

    def body(x_ref, w_ref, t_ref, loss_ref, dx_ref, dw_ref):
        first = (pl.program_id(0) == 0) & (pl.program_id(1) == 0)
        y, vjp = jax.vjp(_rms, x_ref[...], w_ref[...])
        err = y - t_ref[...]
        dx, dw = vjp(err * (1.0 / d))
        dx_ref[...] = dx

        @pl.when(first)
        def _():
            loss_ref[...] = jnp.zeros_like(loss_ref)
            dw_ref[...] = jnp.zeros_like(dw_ref)

        loss_ref[...] += 0.5 * jnp.sum(jnp.sum(err * err, axis=1, keepdims=True), axis=0, keepdims=True) * (1.0 / d)
        dw_ref[...] += dw

    row = pl.BlockSpec((None, ts, d), lambda b, i: (b, i, 0))
    wspec = pl.BlockSpec((1, d), lambda b, i: (0, 0))
    return _call(body, "final_loss", (bsz, s // ts), [row, wspec, row],
                 [pl.BlockSpec((8, LANES), lambda b, i: (0, 0)), row, wspec],
                 [_sds((8, LANES), F32), _sds(x.shape, F32), _sds((1, d), F32)])(x, w, target)


def _unit_lower_inverses(ms):
    n = ms[0].shape[0]
    r, c = _iota((n, n), 0), _iota((n, n), 1)
    same = lambda size: (r // size) == (c // size)
    xs = [(r == c).astype(F32) - jnp.where(same(2), m, 0.0) for m in ms]
    size = 4
    while size <= n:
        below = same(size) & ~same(size // 2)
        xs = [x - _mxu3(x, _mxu3(jnp.where(below, m, 0.0), x, NN), NN) for x, m in zip(xs, ms)]
        size *= 2
    return xs


@jax.custom_vjp
def _known_inverse(m, t):
    return t


_known_inverse.defvjp(lambda m, t: (t, t), lambda t, dt: (-_mxu3(t, _mxu3(dt, t, NT), TN), jnp.zeros_like(t)))


def _gdn_chunk(states, qkv, small, z, a_row, dt_row, nw, tinvs=None):
    nb = len(qkv)
    c = qkv[0].shape[0]
    kw = GDN_HEADS * GDN_DK
    incl, strict = _tril(c), _tril(c, True)
    g_all = [-jnp.exp(a_row) * _softplus(small[b] + dt_row) for b in range(nb)]
    beta_all = [jax.nn.sigmoid(small[b]) for b in range(nb)]
    big_g_all = [_mask_dot(incl.astype(BF16), g_all[b]) for b in range(nb)]
    items = [(b, h) for b in range(nb) for h in range(GDN_HEADS)]
    ids = range(len(items))
    col = lambda b, part, h: qkv[b][:, part * kw + h * GDN_DK:part * kw + (h + 1) * GDN_DK]
    unit = lambda t: t * lax.rsqrt(jnp.sum(t * t, axis=-1, keepdims=True) + EPS)
    q = [unit(col(b, 0, h)) * (GDN_DK ** -0.5) for b, h in items]
    k = [unit(col(b, 1, h)) for b, h in items]
    v = [col(b, 2, h) for b, h in items]
    gc = [_lane_col(big_g_all[b], SM_A + h) for b, h in items]
    bc = [_lane_col(beta_all[b], SM_B + h) for b, h in items]
    g_last = [jnp.sum(_lane_col(g_all[b], SM_A + h), axis=0, keepdims=True) for b, h in items]
    decay = [jnp.where(incl, jnp.exp(jnp.where(incl, gc[i] - _col_to_row(gc[i]), 0.0)), 0.0) for i in ids]
    kb = [k[i] * bc[i] for i in ids]
    m = [jnp.where(strict, _bdot_nt(kb[i], k[i]) * decay[i], 0.0) for i in ids]
    if tinvs is None:
        tinv = _unit_lower_inverses(m)
    else:
        tinv = [_known_inverse(m[i], tinvs[i]) for i in ids]
    eg = [jnp.exp(gc[i]) for i in ids]
    u = [_dot3(tinv[i], v[i] * bc[i]) for i in ids]
    w = [_dot3(tinv[i], kb[i] * eg[i]) for i in ids]
    attn = [_bdot_nt(q[i], k[i]) * decay[i] for i in ids]
    v_new = [u[i] - _bdot(w[i], states[i]) for i in ids]
    o_st = [_bdot(q[i] * eg[i], states[i]) for i in ids]
    o = [o_st[i] + _bdot(attn[i], v_new[i]) for i in ids]
    grow = [_bdot_tn(k[i] * jnp.exp(g_last[i] - gc[i]), v_new[i]) for i in ids]
    new_states = [states[i] * jnp.exp(g_last[i]) + grow[i] for i in ids]
    outs = [_rms(o[i], nw) * _silu(z[b][:, h * GDN_DK:(h + 1) * GDN_DK]) for i, (b, h) in enumerate(items)]
    per_seq = [jnp.concatenate(outs[b * GDN_HEADS:(b + 1) * GDN_HEADS], axis=1) for b in range(nb)]
    return new_states, per_seq, tinv


def _seq_items(bsz, heads):
    return [(b, h) for b in range(bsz) for h in range(heads)]


def _gdn_fwd(qkv_act, p, a_row, dt_row, nw, name):
    bsz, s, _ = qkv_act.shape
    c = GDN_CHUNK
    nc = s // c
    items = _seq_items(bsz, GDN_HEADS)

    def body(qkv_ref, sm_ref, z_ref, a_ref, dt_ref, nw_ref, o_ref, st_ref, ti_ref, st_scr):
        @pl.when(pl.program_id(0) == 0)
        def _():
            st_scr[...] = jnp.zeros_like(st_scr)

        st_ref[...] = st_scr[...]
        seqs = range(bsz)
        new_states, o, tinvs = _gdn_chunk(
            [st_scr[b, h] for b, h in items], [qkv_ref[b] for b in seqs], [sm_ref[b] for b in seqs],
            [z_ref[b] for b in seqs], a_ref[...], dt_ref[...], nw_ref[...])
        for i, (b, h) in enumerate(items):
            st_scr[b, h] = new_states[i]
            ti_ref[b, h] = tinvs[i]
        for b in seqs:
            o_ref[b] = o[b].astype(BF16)

    row = lambda w, blk: pl.BlockSpec((bsz, c, w), lambda n, blk=blk: (0, n, blk))
    prm = pl.BlockSpec((1, LANES), lambda n: (0, 0))
    return _call(
        body, name, (nc,), [row(1536, 0), row(LANES, P_SMALL // LANES), row(512, P_GZ // 512), prm, prm, prm],
        [row(512, 0), pl.BlockSpec((bsz, None, 4, LANES, LANES), lambda n: (0, n, 0, 0, 0)),
         pl.BlockSpec((bsz, None, 4, c, c), lambda n: (0, n, 0, 0, 0))],
        [_sds((bsz, s, 512), BF16), _sds((bsz, nc, 4, LANES, LANES), F32), _sds((bsz, nc, 4, c, c), F32)],
        scratch=[pltpu.VMEM((bsz, 4, LANES, LANES), F32)],
    )(qkv_act, p, p, a_row, dt_row, nw)


def _gdn_bwd(do, qkv_act, p, a_row, dt_row, nw, st_all, ti_all, dp, name):
    bsz, s, _ = qkv_act.shape
    c = GDN_CHUNK
    nc = s // c

    items = _seq_items(bsz, GDN_HEADS)

    def body(qkv_ref, sm_ref, z_ref, a_ref, dt_ref, nw_ref, do_ref, st_ref, ti_ref, _,
             dqkv_ref, dsm_ref, dz_ref, da_ref, ddt_ref, dnw_ref, ds_scr):
        @pl.when(pl.program_id(0) == 0)
        def _():
            ds_scr[...] = jnp.zeros_like(ds_scr)
            da_ref[...] = jnp.zeros_like(da_ref)
            ddt_ref[...] = jnp.zeros_like(ddt_ref)
            dnw_ref[...] = jnp.zeros_like(dnw_ref)

        seqs = range(bsz)
        tinvs = [ti_ref[b, h] for b, h in items]
        chunk = lambda *a: _gdn_chunk(*a, tinvs=tinvs)[:2]
        _, vjp = jax.vjp(chunk, [st_ref[b, h] for b, h in items], [qkv_ref[b] for b in seqs], [sm_ref[b] for b in seqs],
                         [z_ref[b] for b in seqs], a_ref[...], dt_ref[...], nw_ref[...])
        d_states, dqkv, dsm, dz, da, ddt, dnw = vjp(([ds_scr[b, h] for b, h in items], [do_ref[b] for b in seqs]))
        for i, (b, h) in enumerate(items):
            ds_scr[b, h] = d_states[i]
        for b in seqs:
            dqkv_ref[b] = dqkv[b]
            dsm_ref[b] = dsm[b]
            dz_ref[b] = dz[b].astype(BF16)
        da_ref[...] += da
        ddt_ref[...] += ddt
        dnw_ref[...] += dnw

    rrow = lambda w, blk: pl.BlockSpec((bsz, c, w), lambda n, blk=blk: (0, nc - 1 - n, blk))
    prm = pl.BlockSpec((1, LANES), lambda n: (0, 0))
    return _call(
        body, name, (nc,),
        [rrow(1536, 0), rrow(LANES, P_SMALL // LANES), rrow(512, P_GZ // 512), prm, prm, prm, rrow(512, 0),
         pl.BlockSpec((bsz, None, 4, LANES, LANES), lambda n: (0, nc - 1 - n, 0, 0, 0)),
         pl.BlockSpec((bsz, None, 4, c, c), lambda n: (0, nc - 1 - n, 0, 0, 0)), IN_PLACE],
        [rrow(1536, 0), rrow(LANES, 0), rrow(512, P_GZ // 512), prm, prm, prm],
        [_sds((bsz, s, 1536), F32), _sds((bsz, s, LANES), F32), _sds(dp.shape, BF16)] + [_sds((1, LANES), F32)] * 3,
        scratch=[pltpu.VMEM((bsz, 4, LANES, LANES), F32)], aliases={9: 2},
    )(qkv_act, p, p, a_row, dt_row, nw, do, st_all, ti_all, dp)


def _hgrn_block(states, q_raw, f_raw, i_raw, g_raw, lb, nw):
    n = q_raw[0].shape[0]
    c = HGRN_CHUNK
    r, cc = _iota((n, n), 0), _iota((n, n), 1)
    same = (r // c) == (cc // c)
    causal = same & (r >= cc)
    ref_row = (r // c) * c + (c // 2 - 1)
    run_sum = causal.astype(F32)
    rel_sum = run_sum - (same & (ref_row >= cc)).astype(F32)
    sums = jnp.concatenate([run_sum, rel_sum, same.astype(F32)], axis=0).astype(BF16)
    seqs, chunks = range(len(q_raw)), range(n // c)
    items = _seq_items(len(q_raw), HGRN_HEADS)
    hs = lambda t, h: t[:, h * HGRN_DK:(h + 1) * HGRN_DK]
    rows = lambda t, j: t[j * c:(j + 1) * c]
    q = [_silu(q_raw[b]) for b in seqs]
    logf = [jnp.log(lb + (1.0 - lb) * jax.nn.sigmoid(f_raw[b])) for b in seqs]
    k = [(1.0 - lb) * jax.nn.sigmoid(-f_raw[b]) for b in seqs]
    all_sums = [_mask_dot(sums, logf[b]) for b in seqs]
    big_g, g_rel, g_tot = ([t[i * n:(i + 1) * n] for t in all_sums] for i in range(3))
    q_rel = [q[b] * jnp.exp(g_rel[b]) for b in seqs]
    k_rel = [k[b] * jnp.exp(-g_rel[b]) for b in seqs]
    qg = [q[b] * jnp.exp(big_g[b]) for b in seqs]
    k_end = [k[b] * jnp.exp(g_tot[b] - big_g[b]) for b in seqs]
    keep = [[jnp.exp(g_tot[b][j * c:j * c + 1]) for j in chunks] for b in seqs]
    scores = [_bdot_nt(hs(q_rel[b], h), hs(k_rel[b], h)) for b, h in items]
    o_intra = [_bdot(jnp.where(causal, scores[i], 0.0), hs(i_raw[b], h)) for i, (b, h) in enumerate(items)]
    grow = [[_bdot_tn(rows(hs(i_raw[b], h), j), rows(hs(k_end[b], h), j)) for j in chunks] for b, h in items]
    entering, new_states = [], []
    for i, (b, h) in enumerate(items):
        st, per_chunk = states[i], []
        for j in chunks:
            per_chunk.append(st)
            st = st * hs(keep[b][j], h) + grow[i][j]
        entering.append(per_chunk)
        new_states.append(st)
    o_inter = [[_bdot_nt(rows(hs(qg[b], h), j), entering[i][j]) for j in chunks] for i, (b, h) in enumerate(items)]
    outs = [_rms(o_intra[i] + jnp.concatenate(o_inter[i], axis=0), nw) * _silu(hs(g_raw[b], h))
            for i, (b, h) in enumerate(items)]
    return new_states, [jnp.concatenate(outs[b * HGRN_HEADS:(b + 1) * HGRN_HEADS], axis=1) for b in seqs]


def _hgrn_fwd(p, lb, nw, name):
    bsz, s, _ = p.shape
    n = HGRN_BLOCK
    nb = s // n

    items = _seq_items(bsz, HGRN_HEADS)

    def body(q_ref, f_ref, i_ref, g_ref, lb_ref, nw_ref, o_ref, st_ref, st_scr):
        @pl.when(pl.program_id(0) == 0)
        def _():
            st_scr[...] = jnp.zeros_like(st_scr)

        st_ref[...] = st_scr[...]
        per_seq = lambda ref: [ref[b] for b in range(bsz)]
        new_states, o = _hgrn_block([st_scr[b, h] for b, h in items], per_seq(q_ref), per_seq(f_ref), per_seq(i_ref),
                                    per_seq(g_ref), lb_ref[...], nw_ref[...])
        for i, (b, h) in enumerate(items):
            st_scr[b, h] = new_states[i]
        for b in range(bsz):
            o_ref[b] = o[b].astype(BF16)

    row = lambda blk: pl.BlockSpec((bsz, n, 512), lambda i, blk=blk: (0, i, blk))
    return _call(
        body, name, (nb,),
        [row(P_HQ // 512), row(P_HF // 512), row(P_HI // 512), row(P_HG // 512),
         pl.BlockSpec((1, 512), lambda i: (0, 0)), pl.BlockSpec((1, LANES), lambda i: (0, 0))],
        [row(0), pl.BlockSpec((bsz, None, 4, LANES, LANES), lambda i: (0, i, 0, 0, 0))],
        [_sds((bsz, s, 512), BF16), _sds((bsz, nb, 4, LANES, LANES), F32)],
        scratch=[pltpu.VMEM((bsz, 4, LANES, LANES), F32)],
    )(p, p, p, p, lb, nw)


def _hgrn_bwd(do, p, lb, nw, st_all, dp, name):
    bsz, s, _ = p.shape
    n = HGRN_BLOCK
    nb = s // n

    items = _seq_items(bsz, HGRN_HEADS)

    def body(q_ref, f_ref, i_ref, g_ref, lb_ref, nw_ref, do_ref, st_ref, _, dp_ref, dlb_ref, dnw_ref, ds_scr):
        @pl.when(pl.program_id(0) == 0)
        def _():
            ds_scr[...] = jnp.zeros_like(ds_scr)
            dlb_ref[...] = jnp.zeros_like(dlb_ref)
            dnw_ref[...] = jnp.zeros_like(dnw_ref)

        per_seq = lambda ref: [ref[b] for b in range(bsz)]
        _, vjp = jax.vjp(_hgrn_block, [st_ref[b, h] for b, h in items], per_seq(q_ref), per_seq(f_ref), per_seq(i_ref),
                         per_seq(g_ref), lb_ref[...], nw_ref[...])
        d_states, dq, df, di, dg, dlb, dnw = vjp(([ds_scr[b, h] for b, h in items], per_seq(do_ref)))
        for i, (b, h) in enumerate(items):
            ds_scr[b, h] = d_states[i]
        for b in range(bsz):
            for j, t in enumerate((dq, df, di, dg)):
                dp_ref[b, :, j * 512:(j + 1) * 512] = t[b].astype(BF16)
        dlb_ref[...] += dlb
        dnw_ref[...] += dnw

    row = lambda blk: pl.BlockSpec((bsz, n, 512), lambda i, blk=blk: (0, nb - 1 - i, blk))
    return _call(
        body, name, (nb,),
        [row(P_HQ // 512), row(P_HF // 512), row(P_HI // 512), row(P_HG // 512),
         pl.BlockSpec((1, 512), lambda i: (0, 0)), pl.BlockSpec((1, LANES), lambda i: (0, 0)), row(0),
         pl.BlockSpec((bsz, None, 4, LANES, LANES), lambda i: (0, nb - 1 - i, 0, 0, 0)), IN_PLACE],
        [pl.BlockSpec((bsz, n, 2048), lambda i: (0, nb - 1 - i, P_HQ // 2048)),
         pl.BlockSpec((1, 512), lambda i: (0, 0)), pl.BlockSpec((1, LANES), lambda i: (0, 0))],
        [_sds(dp.shape, BF16), _sds((1, 512), F32), _sds((1, LANES), F32)],
        scratch=[pltpu.VMEM((bsz, 4, LANES, LANES), F32)], aliases={8: 0},
    )(p, p, p, p, lb, nw, do, st_all, dp)


def _ssd_chunk(states, xbc, small, z, a_row, dt_row, d_row, nw):
    seqs = range(len(xbc))
    c = xbc[0].shape[0]
    incl = _tril(c)
    spread = (_iota((LANES, SSD_INNER), 0) == SM_DT + _iota((LANES, SSD_INNER), 1) // SSD_HEAD_DIM).astype(BF16)
    dt_all = [_softplus(small[b] + dt_row) for b in seqs]
    both = [_spread_dot(jnp.concatenate([dt_all[b], dt_all[b] * (-jnp.exp(a_row))], axis=0), spread) for b in seqs]
    dt_e, da_e = [t[:c] for t in both], [t[c:] for t in both]
    acs_e = [_mask_dot(incl.astype(BF16), da_e[b]) for b in seqs]
    last_e = [jnp.sum(da_e[b], axis=0, keepdims=True) for b in seqs]
    xs = [xbc[b][:, :SSD_INNER] for b in seqs]
    xdt = [xs[b] * dt_e[b] for b in seqs]
    gw = SSD_GROUPS * SSD_STATE
    lane = _iota((1, LANES), 1)
    items = _seq_items(len(xbc), 4)
    grp = [(b, g) for b in seqs for g in range(SSD_GROUPS)]
    ps = lambda t, j: t[:, j * LANES:(j + 1) * LANES]
    bg = {(b, g): xbc[b][:, SSD_INNER + g * SSD_STATE:SSD_INNER + (g + 1) * SSD_STATE] for b, g in grp}
    cg = {(b, g): xbc[b][:, SSD_INNER + gw + g * SSD_STATE:SSD_INNER + gw + (g + 1) * SSD_STATE] for b, g in grp}
    cb = {bgk: _bdot_nt(cg[bgk], bg[bgk]) for bgk in grp}

    def seg(b, j, sub):
        ac = ps(acs_e[b], j)[:, sub * SSD_HEAD_DIM:sub * SSD_HEAD_DIM + 1]
        return jnp.where(incl, jnp.exp(jnp.where(incl, ac - _col_to_row(ac), 0.0)), 0.0)

    mine = [((lane // SSD_HEAD_DIM) == sub).astype(F32) for sub in range(2)]
    y_in = [[_bdot(cb[b, j // 2] * seg(b, j, sub), ps(xdt[b], j) * mine[sub]) for sub in range(2)] for b, j in items]
    y_st = [_bdot(cg[b, j // 2], states[i]) for i, (b, j) in enumerate(items)]
    grow = [_bdot_tn(bg[b, j // 2], ps(xdt[b], j) * jnp.exp(ps(last_e[b], j) - ps(acs_e[b], j))) for b, j in items]
    new_states = [states[i] * jnp.exp(ps(last_e[b], j)) + grow[i] for i, (b, j) in enumerate(items)]
    ys = [y_in[i][0] + y_in[i][1] + y_st[i] * jnp.exp(ps(acs_e[b], j)) + ps(d_row, j) * ps(xs[b], j)
          for i, (b, j) in enumerate(items)]
    gwid = SSD_INNER // SSD_GROUPS
    outs = []
    for b in seqs:
        yz = jnp.concatenate(ys[4 * b:4 * b + 4], axis=1) * _silu(z[b])
        outs.append(jnp.concatenate(
            [_rms(yz[:, g * gwid:(g + 1) * gwid], nw[:, g * gwid:(g + 1) * gwid]) for g in range(SSD_GROUPS)], axis=1))
    return new_states, outs


def _ssd_fwd(xbc_act, p, a_row, dt_row, d_row, nw, name):
    bsz, s, _ = xbc_act.shape
    c = SSD_CHUNK
    nc = s // c

    items = _seq_items(bsz, 4)

    def body(x_ref, sm_ref, z_ref, a_ref, dt_ref, d_ref, nw_ref, o_ref, st_ref, st_scr):
        @pl.when(pl.program_id(0) == 0)
        def _():
            st_scr[...] = jnp.zeros_like(st_scr)

        st_ref[...] = st_scr[...]
        per_seq = lambda ref: [ref[b] for b in range(bsz)]
        new_states, o = _ssd_chunk([st_scr[b, j] for b, j in items], per_seq(x_ref), per_seq(sm_ref), per_seq(z_ref),
                                   a_ref[...], dt_ref[...], d_ref[...], nw_ref[...])
        for i, (b, j) in enumerate(items):
            st_scr[b, j] = new_states[i]
        for b in range(bsz):
            o_ref[b] = o[b].astype(BF16)

    row = lambda w, blk: pl.BlockSpec((bsz, c, w), lambda n, blk=blk: (0, n, blk))
    prm = pl.BlockSpec((1, LANES), lambda n: (0, 0))
    prm5 = pl.BlockSpec((1, 512), lambda n: (0, 0))
    return _call(
        body, name, (nc,),
        [row(1024, 0), row(LANES, P_SMALL // LANES), row(512, P_SZ // 512), prm, prm, prm5, prm5],
        [row(512, 0), pl.BlockSpec((bsz, None, 4, LANES, LANES), lambda n: (0, n, 0, 0, 0))],
        [_sds((bsz, s, 512), BF16), _sds((bsz, nc, 4, LANES, LANES), F32)],
        scratch=[pltpu.VMEM((bsz, 4, LANES, LANES), F32)],
    )(xbc_act, p, p, a_row, dt_row, d_row, nw)


def _ssd_bwd(do, xbc_act, p, a_row, dt_row, d_row, nw, st_all, dp, name):
    bsz, s, _ = xbc_act.shape
    c = SSD_CHUNK
    nc = s // c

    items = _seq_items(bsz, 4)

    def body(x_ref, sm_ref, z_ref, a_ref, dt_ref, d_ref, nw_ref, do_ref, st_ref, _,
             dx_ref, dsm_ref, dz_ref, da_ref, ddt_ref, dd_ref, dnw_ref, ds_scr):
        @pl.when(pl.program_id(0) == 0)
        def _():
            ds_scr[...] = jnp.zeros_like(ds_scr)
            da_ref[...] = jnp.zeros_like(da_ref)
            ddt_ref[...] = jnp.zeros_like(ddt_ref)
            dd_ref[...] = jnp.zeros_like(dd_ref)
            dnw_ref[...] = jnp.zeros_like(dnw_ref)

        per_seq = lambda ref: [ref[b] for b in range(bsz)]
        _, vjp = jax.vjp(_ssd_chunk, [st_ref[b, j] for b, j in items], per_seq(x_ref), per_seq(sm_ref), per_seq(z_ref),
                         a_ref[...], dt_ref[...], d_ref[...], nw_ref[...])
        d_states, dx, dsm, dz, da, ddt, dd, dnw = vjp(([ds_scr[b, j] for b, j in items], per_seq(do_ref)))
        for i, (b, j) in enumerate(items):
            ds_scr[b, j] = d_states[i]
        for b in range(bsz):
            dx_ref[b] = dx[b]
            dsm_ref[b] = dsm[b]
            dz_ref[b] = dz[b].astype(BF16)
        da_ref[...] += da
        ddt_ref[...] += ddt
        dd_ref[...] += dd
        dnw_ref[...] += dnw

    row = lambda w, blk: pl.BlockSpec((bsz, c, w), lambda n, blk=blk: (0, nc - 1 - n, blk))
    prm = pl.BlockSpec((1, LANES), lambda n: (0, 0))
    prm5 = pl.BlockSpec((1, 512), lambda n: (0, 0))
    return _call(
        body, name, (nc,),
        [row(1024, 0), row(LANES, P_SMALL // LANES), row(512, P_SZ // 512), prm, prm, prm5, prm5, row(512, 0),
         pl.BlockSpec((bsz, None, 4, LANES, LANES), lambda n: (0, nc - 1 - n, 0, 0, 0)), IN_PLACE],
        [row(1024, 0), row(LANES, 0), row(512, P_SZ // 512), prm, prm, prm5, prm5],
        [_sds((bsz, s, 1024), F32), _sds((bsz, s, LANES), F32), _sds(dp.shape, BF16),
         _sds((1, LANES), F32), _sds((1, LANES), F32), _sds((1, 512), F32), _sds((1, 512), F32)],
        scratch=[pltpu.VMEM((bsz, 4, LANES, LANES), F32)], aliases={9: 2},
    )(xbc_act, p, p, a_row, dt_row, d_row, nw, do, st_all, dp)


def _small_cols(dsm_a, dsm_c, dp, name):
    bsz, s, _ = dsm_a.shape
    ts = _row_tile(s, 1024)
    width = P_WIDTH - P_SMALL

    def body(a_ref, c_ref, _, o_ref):
        o_ref[:, :LANES] = (a_ref[...] + c_ref[...]).astype(BF16)
        o_ref[:, LANES:] = jnp.zeros((ts, width - LANES), BF16)

    row = pl.BlockSpec((None, ts, LANES), lambda b, i: (b, i, 0))
    return _call(body, name, (bsz, s // ts), [row, row, IN_PLACE],
                 pl.BlockSpec((None, ts, width), lambda b, i: (b, i, P_SMALL // width)), _sds(dp.shape, BF16),
                 aliases={2: 0})(dsm_a, dsm_c, dp)


def _peer(k):
    x, y, c = lax.axis_index("x"), lax.axis_index("y"), lax.axis_index("c")
    px = 1 - x if k & 4 else x
    py = 1 - y if k & 2 else y
    pc = 1 - c if k & 1 else c
    return (px, py, pc), 4 * px + 2 * py + pc


def _my_index():
    return 4 * lax.axis_index("x") + 2 * lax.axis_index("y") + lax.axis_index("c")


def _mesh_place():
    x, y, c = lax.axis_index("x"), lax.axis_index("y"), lax.axis_index("c")
    return (x, y, c), (x, y, 1 - c), [(1 - x, y), (x, 1 - y), (1 - x, 1 - y)]


def _run_exchange(body, name, arrays, out_shape, n_sems):
    n = len(arrays)
    any_spec = pl.BlockSpec(memory_space=pl.ANY)
    return pl.pallas_call(
        body, name=name, out_shape=out_shape, in_specs=[any_spec] * n, out_specs=[any_spec] * n,
        scratch_shapes=[pltpu.SemaphoreType.DMA((n_sems, n)), pltpu.SemaphoreType.DMA((n_sems, n)),
                        pltpu.SemaphoreType.DMA((n,))],
    )(*arrays)


def _all_to_all(arrays, name):
    n = len(arrays)

    def body(*refs):
        ins, outs = refs[:n], refs[n:2 * n]
        send_sems, recv_sems, local_sems = refs[2 * n:]
        me = _my_index()

        def copy(i, k, arriving):
            peer, slot = _peer(k)
            return pltpu.make_async_remote_copy(
                src_ref=ins[i].at[slot], dst_ref=outs[i].at[slot if arriving else me], send_sem=send_sems.at[k - 1, i],
                recv_sem=recv_sems.at[k - 1, i], device_id=peer, device_id_type=MESH_ID)

        mine = [pltpu.make_async_copy(ins[i].at[me], outs[i].at[me], local_sems.at[i]) for i in range(n)]
        sends = [copy(i, k, False) for k in range(1, N_DEV) for i in range(n)]
        for cp in mine + sends:
            cp.start()
        for k in range(1, N_DEV):
            for i in range(n):
                copy(i, k, True).wait_recv()
        for cp in sends:
            cp.wait_send()
        for cp in mine:
            cp.wait()

    return _run_exchange(body, name, arrays, [_sds(a.shape, a.dtype) for a in arrays], N_DEV - 1)


def _gather_two_level(arrays, name):
    n = len(arrays)

    def body(*refs):
        ins, outs = refs[:n], refs[n:2 * n]
        send_sems, recv_sems, local_sems = refs[2 * n:]
        (x, y, c), sibling, chips = _mesh_place()
        slot = lambda px, py, pc: 4 * px + 2 * py + pc

        def copy(i, k, block, to, src=None):
            return pltpu.make_async_remote_copy(
                src_ref=outs[i].at[block] if src is None else src, dst_ref=outs[i].at[block],
                send_sem=send_sems.at[k, i], recv_sem=recv_sems.at[k, i], device_id=to, device_id_type=MESH_ID)

        me = slot(x, y, c)
        mine = [pltpu.make_async_copy(ins[i], outs[i].at[me], local_sems.at[i]) for i in range(n)]
        first = [copy(i, 0, me, sibling, src=ins[i]) for i in range(n)]
        first += [copy(i, 1 + j, me, (*chip, c), src=ins[i]) for j, chip in enumerate(chips) for i in range(n)]
        for cp in mine + first:
            cp.start()
        passed = []
        for j, chip in enumerate(chips):
            for i in range(n):
                copy(i, 1 + j, slot(*chip, c), (x, y, c)).wait_recv()
                cp = copy(i, 4 + j, slot(*chip, c), sibling)
                cp.start()
                passed.append(cp)
        for i in range(n):
            copy(i, 0, slot(x, y, 1 - c), (x, y, c)).wait_recv()
        for j, chip in enumerate(chips):
            for i in range(n):
                copy(i, 4 + j, slot(*chip, 1 - c), (x, y, c)).wait_recv()
        for cp in first + passed:
            cp.wait_send()
        for cp in mine:
            cp.wait()

    out_shape = [_sds((N_DEV,) + a.shape, a.dtype) for a in arrays]
    return _run_exchange(body, name, arrays, out_shape, 7)


def _sibling_swap(arrays, name):
    n = len(arrays)

    def body(*refs):
        ins, outs = refs[:n], refs[n:2 * n]
        send_sems, recv_sems, _ = refs[2 * n:]
        (x, y, c), sibling, _ = _mesh_place()
        copies = [pltpu.make_async_remote_copy(
            src_ref=ins[i].at[1 - c], dst_ref=outs[i], send_sem=send_sems.at[0, i], recv_sem=recv_sems.at[0, i],
            device_id=sibling, device_id_type=MESH_ID) for i in range(n)]
        for cp in copies:
            cp.start()
        for cp in copies:
            cp.wait()

    out_shape = [_sds(a.shape[1:], a.dtype) for a in arrays]
    return _run_exchange(body, name, arrays, out_shape, 1)


def _chip_scatter(arrays, name):
    n = len(arrays)

    def body(*refs):
        ins, outs = refs[:n], refs[n:2 * n]
        send_sems, recv_sems, local_sems = refs[2 * n:]
        (x, y, c), _, chips = _mesh_place()
        me = 2 * x + y
        mine = [pltpu.make_async_copy(ins[i].at[me], outs[i].at[me], local_sems.at[i]) for i in range(n)]
        sends = [pltpu.make_async_remote_copy(
            src_ref=ins[i].at[2 * chip[0] + chip[1]], dst_ref=outs[i].at[me], send_sem=send_sems.at[j, i],
            recv_sem=recv_sems.at[j, i], device_id=(*chip, c), device_id_type=MESH_ID)
            for j, chip in enumerate(chips) for i in range(n)]
        for cp in mine + sends:
            cp.start()
        for j, chip in enumerate(chips):
            for i in range(n):
                pltpu.make_async_remote_copy(
                    src_ref=ins[i].at[me], dst_ref=outs[i].at[2 * chip[0] + chip[1]], send_sem=send_sems.at[j, i],
                    recv_sem=recv_sems.at[j, i], device_id=(*chip, c), device_id_type=MESH_ID).wait_recv()
        for cp in sends:
            cp.wait_send()
        for cp in mine:
            cp.wait()

    out_shape = [_sds(a.shape, a.dtype) for a in arrays]
    return _run_exchange(body, name, arrays, out_shape, 3)


def _pair_sum(a, b, name):
    lead, rows, width = a.shape
    tr = _pick(rows, (256, 128, 64, 32, 16, 8)) if rows % 8 == 0 else rows

    def body(a_ref, b_ref, o_ref):
        o_ref[...] = (a_ref[...].astype(F32) + b_ref[...].astype(F32)).astype(o_ref.dtype)

    blk = pl.BlockSpec((None, tr, width), lambda l, i: (l, i, 0))
    return _call(body, name, (lead, rows // tr), [blk, blk], blk, _sds(a.shape, a.dtype))(a, b)


def _sum_adamw(gs, w, m, v, name):
    lead, rows, width = w.shape
    slots = gs.shape[0]
    tr = _pick(rows, (128, 64, 32, 16, 8)) if rows % 8 == 0 else rows

    def body(g_ref, w_ref, m_ref, v_ref, go_ref, d_ref, mo_ref, vo_ref):
        g = g_ref[0].astype(F32)
        for i in range(1, slots):
            g = g + g_ref[i].astype(F32)
        m2 = ADAM_B1 * m_ref[...] + (1.0 - ADAM_B1) * g
        v2 = ADAM_B2 * v_ref[...] + (1.0 - ADAM_B2) * (g * g)
        m_hat = m2 / (1.0 - ADAM_B1 ** ADAM_STEP)
        v_hat = v2 / (1.0 - ADAM_B2 ** ADAM_STEP)
        go_ref[...] = g
        d_ref[...] = -ADAM_LR * (m_hat / (jnp.sqrt(v_hat) + ADAM_EPS) + ADAM_WD * w_ref[...])
        mo_ref[...] = m2
        vo_ref[...] = v2

    blk = pl.BlockSpec((None, tr, width), lambda l, i: (l, i, 0))
    return _call(body, name, (lead, rows // tr),
                 [pl.BlockSpec((slots, None, tr, width), lambda l, i: (0, l, i, 0)), blk, blk, blk],
                 [blk] * 4, [_sds(w.shape, F32)] * 4)(gs, w, m, v)


MATMUL_WEIGHTS = ("w_in", "w_br_a", "w_br_b", "w_br_c", "w_out", "ffn_w_up", "ffn_w_down")
UNALIGNED = ("w_in", "ffn_w_up")
SPLIT = (
    ("w_in", (DEPTH, D_MODEL, 8720), 2),
    ("gdn_conv_w", (DEPTH, 4, 1536), 2), ("ssd_conv_w", (DEPTH, 4, 1024), 2),
    ("w_br_a", (DEPTH, 512, D_MODEL), 2), ("w_br_b", (DEPTH, 512, D_MODEL), 2), ("w_br_c", (DEPTH, 512, D_MODEL), 2),
    ("w_out", (DEPTH, D_MODEL, D_MODEL), 1), ("ffn_w_up", (DEPTH, D_MODEL, 2 * FFN_HIDDEN), 2),
    ("ffn_conv_w", (DEPTH, 3, 2 * FFN_HIDDEN), 2), ("ffn_w_down", (DEPTH, FFN_HIDDEN, D_MODEL), 1),
)
REPL = (
    ("b_ada", (DEPTH, 6 * D_MODEL)), ("norm1_w", (DEPTH, D_MODEL)), ("gdn_a_log", (DEPTH, 4)),
    ("gdn_dt_bias", (DEPTH, 4)), ("gdn_norm_w", (DEPTH, 128)), ("hgrn_lb_param", (DEPTH, 512)),
    ("hgrn_norm_w", (DEPTH, 128)), ("ssd_conv_b", (DEPTH, 1024)), ("ssd_a_log", (DEPTH, 8)),
    ("ssd_dt_bias", (DEPTH, 8)), ("ssd_d", (DEPTH, 8)), ("ssd_norm_w", (DEPTH, 512)), ("norm2_w", (DEPTH, D_MODEL)),
    ("ffn_conv_b", (DEPTH, 2 * FFN_HIDDEN)), ("final_norm_w", (D_MODEL,)),
)
WEIGHTS = ("w_ada", "b_ada", "norm1_w", "w_in", "gdn_conv_w", "gdn_a_log", "gdn_dt_bias", "gdn_norm_w",
           "hgrn_lb_param", "hgrn_norm_w", "ssd_conv_w", "ssd_conv_b", "ssd_a_log", "ssd_dt_bias", "ssd_d",
           "ssd_norm_w", "w_br_a", "w_br_b", "w_br_c", "w_out", "norm2_w", "ffn_w_up", "ffn_conv_w", "ffn_conv_b",
           "ffn_w_down", "final_norm_w")


def _block_shape(shape, axis):
    return tuple(d // N_DEV if i == axis else d for i, d in enumerate(shape))


def _join_blocks(gathered, shape, axis):
    return jnp.moveaxis(gathered, 0, axis).reshape(shape)


def _split_blocks(full, shape, axis):
    bs = _block_shape(shape, axis)
    t = full.reshape(shape[:axis] + (N_DEV, bs[axis]) + shape[axis + 1:])
    return jnp.moveaxis(t, axis, 0)


def _pack_repl(vals):
    parts = []
    for n, shape in REPL:
        size = math.prod(shape)
        parts.append(jnp.pad(vals[n].reshape(-1), (0, -(-size // PACK_W) * PACK_W - size)))
    cat = jnp.concatenate(parts)
    rows = -(-cat.shape[0] // (8 * PACK_W)) * 8
    return jnp.pad(cat, (0, rows * PACK_W - cat.shape[0])).reshape(rows, PACK_W)


def _unpack_repl(packed):
    flat, out, off = packed.reshape(-1), {}, 0
    for n, shape in REPL:
        size = math.prod(shape)
        out[n] = flat[off:off + size].reshape(shape)
        off += -(-size // PACK_W) * PACK_W
    return out


def _lane_row(vec, lane0):
    return jnp.pad(vec, (lane0, LANES - lane0 - vec.shape[0]))[None]


def _arrange_w_in(w):
    offs = [0]
    for sz in W_IN_SPLITS:
        offs.append(offs[-1] + sz)
    qkv, a, b, gz, hq, hf, hi, hg, sz_, xbc, dt, gate = [w[:, offs[i]:offs[i + 1]] for i in range(12)]
    pad = jnp.zeros((w.shape[0], P_WIDTH - P_SMALL - 16), w.dtype)
    return jnp.concatenate([qkv, gz, xbc, gate, hq, hf, hi, hg, sz_, a, b, dt, pad], axis=1)


def _restore_w_in(wp):
    cut = lambda o, n: wp[:, o:o + n]
    return jnp.concatenate([
        cut(P_QKV, 1536), cut(P_SMALL + SM_A, 4), cut(P_SMALL + SM_B, 4), cut(P_GZ, 512), cut(P_HQ, 512),
        cut(P_HF, 512), cut(P_HI, 512), cut(P_HG, 512), cut(P_SZ, 512), cut(P_XBC, 1024), cut(P_SMALL + SM_DT, 8),
        cut(P_GATE, 3072)], axis=1)


def _join_cols(gathered, arrange, name):
    _, depth, rows, cols = gathered.shape
    tr = _pick(rows, (256, 128, 64, 32, 16, 8))
    width = P_WIDTH if arrange else N_DEV * cols

    def body(g_ref, o_ref):
        row = jnp.concatenate([g_ref[d] for d in range(N_DEV)], axis=1)
        o_ref[...] = _arrange_w_in(row) if arrange else row

    return _call(body, name, (depth, rows // tr),
                 [pl.BlockSpec((N_DEV, None, tr, cols), lambda l, i: (0, l, i, 0))],
                 pl.BlockSpec((None, tr, width), lambda l, i: (l, i, 0)), _sds((depth, rows, width), gathered.dtype),
                 )(gathered)


def _split_cols(per_layer, restore, cols, name):
    depth = len(per_layer)
    rows = per_layer[0].shape[0]
    tr = _pick(rows, (256, 128, 64, 32, 16, 8))
    nt = rows // tr

    def body(*refs):
        o_ref = refs[depth]
        for l in range(depth):
            @pl.when(pl.program_id(0) == l)
            def _(l=l):
                row = _restore_w_in(refs[l][...]) if restore else refs[l][...]
                for d in range(N_DEV):
                    o_ref[d % 2, d // 2] = row[:, d * cols:(d + 1) * cols]

    return _call(body, name, (depth, nt),
                 [pl.BlockSpec((tr, a.shape[1]), lambda l, i: (i, 0)) for a in per_layer],
                 pl.BlockSpec((2, N_DEV // 2, tr, cols), lambda l, i: (0, 0, l * nt + i, 0)),
                 _sds((2, N_DEV // 2, depth * rows, cols), per_layer[0].dtype))(*per_layer)


def _layer_consts(l, wf, wr, lower):
    t = lambda a: a.T
    k = {}
    k["n1w"], k["n2w"] = wr["norm1_w"][l][None], wr["norm2_w"][l][None]
    k["win"], k["win_t"] = wf["w_in"][l], t(wf["w_in"][l])
    for n in ("w_br_a", "w_br_b", "w_br_c", "w_out", "ffn_w_up", "ffn_w_down"):
        k[n], k[n + "_t"] = wf[n][l], t(wf[n][l])
    k["gdn_conv_w"], k["gdn_conv_b"] = wf["gdn_conv_w"][l], jnp.zeros((1, 1536), F32)
    k["ssd_conv_w"], k["ssd_conv_b"] = wf["ssd_conv_w"][l], wr["ssd_conv_b"][l][None]
    k["ffn_conv_w"], k["ffn_conv_b"] = wf["ffn_conv_w"][l], wr["ffn_conv_b"][l][None]
    k["gdn_a"], k["gdn_dt"] = _lane_row(wr["gdn_a_log"][l], SM_A), _lane_row(wr["gdn_dt_bias"][l], SM_A)
    k["gdn_nw"], k["hgrn_nw"] = wr["gdn_norm_w"][l][None], wr["hgrn_norm_w"][l][None]
    k["ssd_a"], k["ssd_dt"] = _lane_row(wr["ssd_a_log"][l], SM_DT), _lane_row(wr["ssd_dt_bias"][l], SM_DT)
    k["ssd_d"] = jnp.repeat(wr["ssd_d"][l], SSD_HEAD_DIM)[None]
    k["ssd_nw"] = wr["ssd_norm_w"][l][None]
    k["lb"] = lower[l:l + 1]
    return k


def _layer_fwd(l, x, mod, k):
    bsz, s, d = x.shape
    t = bsz * s
    sv = {"x": x}
    sv["mod"] = [mod[:, None, i * d:(i + 1) * d] for i in range(6)]
    sh1, sc1, g1, sh2, sc2, g2 = sv["mod"]
    h1 = _norm_mod_fwd(x, k["n1w"], sh1, sc1, f"norm1_fwd{l}")
    p = _mm(h1.reshape(t, d), k["win"], F32, f"mm_in{l}").reshape(bsz, s, P_WIDTH)
    qkv_act = _conv_fwd(p, P_QKV, 1536, k["gdn_conv_w"], k["gdn_conv_b"], f"gdn_conv_fwd{l}")
    oa, st_a, ti_a = _gdn_fwd(qkv_act, p, k["gdn_a"], k["gdn_dt"], k["gdn_nw"], f"gdn_fwd{l}")
    ob, st_b = _hgrn_fwd(p, k["lb"], k["hgrn_nw"], f"hgrn_fwd{l}")
    xbc_act = _conv_fwd(p, P_XBC, 1024, k["ssd_conv_w"], k["ssd_conv_b"], f"ssd_conv_fwd{l}")
    oc, st_c = _ssd_fwd(xbc_act, p, k["ssd_a"], k["ssd_dt"], k["ssd_d"], k["ssd_nw"], f"ssd_fwd{l}")
    merged = _merge_fwd(p, oa, ob, oc, k["w_br_a"], k["w_br_b"], k["w_br_c"], f"merge_fwd{l}")
    mix, x1 = _mm_resid(merged.reshape(t, d), k["w_out"], x, g1, f"mm_out{l}")
    h2 = _norm_mod_fwd(x1, k["n2w"], sh2, sc2, f"norm2_fwd{l}")
    u_pre = _mm(h2.reshape(t, d), k["ffn_w_up"], F32, f"mm_up{l}").reshape(bsz, s, 2 * FFN_HIDDEN)
    a = _conv_glu_fwd(u_pre, k["ffn_conv_w"], k["ffn_conv_b"], f"ffn_conv_glu_fwd{l}")
    ffn, x2 = _mm_resid(a.reshape(t, FFN_HIDDEN), k["ffn_w_down"], x1, g2, f"mm_down{l}")
    sv.update(h1=h1, p=p, qkv_act=qkv_act, oa=oa, st_a=st_a, ti_a=ti_a, ob=ob, st_b=st_b, xbc_act=xbc_act, oc=oc, st_c=st_c,
              merged=merged, mix=mix, x1=x1, h2=h2, u_pre=u_pre, a=a, ffn=ffn)
    return x2, sv


def _layer_bwd(l, dx2, k, sv):
    bsz, s, d = dx2.shape
    t = bsz * s
    f2 = 2 * FFN_HIDDEN
    sh1, sc1, g1, sh2, sc2, g2 = sv["mod"]
    tr = lambda a: a.reshape(t, -1).T
    g = {}
    dffn, dg2 = _gate_bwd(dx2, sv["ffn"], g2, f"gate2_bwd{l}")
    dffn2 = dffn.reshape(t, d)
    da = _mm(dffn2, k["ffn_w_down_t"], F32, f"mm_down_dx{l}").reshape(bsz, s, FFN_HIDDEN)
    g["ffn_w_down"] = _mm(tr(sv["a"]), dffn2, BF16, f"mm_down_dw{l}")
    du_pre, g["ffn_conv_w"], dfcb = _conv_glu_bwd(da, sv["u_pre"], k["ffn_conv_w"], k["ffn_conv_b"], f"ffn_conv_glu_bwd{l}")
    g["ffn_conv_b"] = dfcb[0]
    du2 = du_pre.reshape(t, f2)
    dh2 = _mm(du2, k["ffn_w_up_t"], F32, f"mm_up_dx{l}").reshape(bsz, s, d)
    g["ffn_w_up"] = _mm(tr(sv["h2"]), du2, BF16, f"mm_up_dw{l}")
    dx1, dn2w, dsh2, dsc2 = _norm_mod_bwd(sv["x1"], k["n2w"], sh2, sc2, dh2, dx2, f"norm2_bwd{l}")
    g["norm2_w"] = dn2w[0]
    dmix, dg1 = _gate_bwd(dx1, sv["mix"], g1, f"gate1_bwd{l}")
    dmix2 = dmix.reshape(t, d)
    dmerged = _mm(dmix2, k["w_out_t"], F32, f"mm_out_dx{l}").reshape(bsz, s, d)
    g["w_out"] = _mm(tr(sv["merged"]), dmix2, BF16, f"mm_out_dw{l}")
    p = sv["p"]
    dp, doa, dob, doc, dya, dyb, dyc = _merge_bwd(
        dmerged, p, sv["oa"], sv["ob"], sv["oc"], k["w_br_a"], k["w_br_b"], k["w_br_c"],
        k["w_br_a_t"], k["w_br_b_t"], k["w_br_c_t"], f"merge_bwd{l}")
    g["w_br_a"] = _mm(tr(sv["oa"]), dya.reshape(t, d), BF16, f"mm_bra_dw{l}")
    g["w_br_b"] = _mm(tr(sv["ob"]), dyb.reshape(t, d), BF16, f"mm_brb_dw{l}")
    g["w_br_c"] = _mm(tr(sv["oc"]), dyc.reshape(t, d), BF16, f"mm_brc_dw{l}")
    dxbc_act, dsm_c, dp, da_c, ddt_c, dd_c, dnw_c = _ssd_bwd(
        doc, sv["xbc_act"], p, k["ssd_a"], k["ssd_dt"], k["ssd_d"], k["ssd_nw"], sv["st_c"], dp, f"ssd_bwd{l}")
    dp, g["ssd_conv_w"], dscb = _conv_bwd(dxbc_act, p, P_XBC, 1024, k["ssd_conv_w"], k["ssd_conv_b"], dp, f"ssd_conv_bwd{l}")
    g["ssd_conv_b"] = dscb[0]
    g["ssd_a_log"], g["ssd_dt_bias"] = da_c[0, SM_DT:SM_DT + 8], ddt_c[0, SM_DT:SM_DT + 8]
    g["ssd_d"] = dd_c.reshape(SSD_HEADS, SSD_HEAD_DIM).sum(axis=1)
    g["ssd_norm_w"] = dnw_c[0]
    dp, dlb, dnw_b = _hgrn_bwd(dob, p, k["lb"], k["hgrn_nw"], sv["st_b"], dp, f"hgrn_bwd{l}")
    g["hgrn_norm_w"] = dnw_b[0]
    dqkv_act, dsm_a, dp, da_a, ddt_a, dnw_a = _gdn_bwd(
        doa, sv["qkv_act"], p, k["gdn_a"], k["gdn_dt"], k["gdn_nw"], sv["st_a"], sv["ti_a"], dp, f"gdn_bwd{l}")
    dp, g["gdn_conv_w"], _ = _conv_bwd(dqkv_act, p, P_QKV, 1536, k["gdn_conv_w"], k["gdn_conv_b"], dp, f"gdn_conv_bwd{l}")
    g["gdn_a_log"], g["gdn_dt_bias"], g["gdn_norm_w"] = da_a[0, :4], ddt_a[0, :4], dnw_a[0]
    dp = _small_cols(dsm_a, dsm_c, dp, f"small_cols{l}").reshape(t, P_WIDTH)
    dh1 = _mm(dp, k["win_t"], F32, f"mm_in_dx{l}").reshape(bsz, s, d)
    g["w_in"] = _mm(tr(sv["h1"]), dp, BF16, f"mm_in_dw{l}")
    dx, dn1w, dsh1, dsc1 = _norm_mod_bwd(sv["x"], k["n1w"], sh1, sc1, dh1, dx1, f"norm1_bwd{l}")
    g["norm1_w"] = dn1w[0]
    dmod = jnp.concatenate([dsh1, dsc1, dg1, dsh2, dsc2, dg2], axis=-1)[:, 0]
    return dx, g, dlb, dmod


def _local_step(x, mod, wf, wr, target):
    lower = _lb_fwd(wr["hgrn_lb_param"])
    ks = [_layer_consts(l, wf, wr, lower) for l in range(DEPTH)]
    saved = []
    h = x
    for l in range(DEPTH):
        h, sv = _layer_fwd(l, h, mod[l], ks[l])
        saved.append(sv)
    loss8, dh, dfnw = _final_loss(h, wr["final_norm_w"][None], target)
    per_layer, dlbs, dmods = [None] * DEPTH, [None] * DEPTH, [None] * DEPTH
    for l in reversed(range(DEPTH)):
        dh, per_layer[l], dlbs[l], dmods[l] = _layer_bwd(l, dh, ks[l], saved[l])
    grads = {n: [per_layer[l][n] for l in range(DEPTH)] for n in per_layer[0]}
    grads = {n: g if n in UNALIGNED else jnp.stack(g) for n, g in grads.items()}
    grads["hgrn_lb_param"] = _lb_bwd(wr["hgrn_lb_param"], jnp.concatenate(dlbs, axis=0))
    grads["final_norm_w"] = dfnw[0]
    return loss8[0, 0], dh, grads, jnp.stack(dmods)


def kernel(x, c, w_ada, b_ada, norm1_w, w_in, gdn_conv_w, gdn_a_log, gdn_dt_bias, gdn_norm_w, hgrn_lb_param, hgrn_norm_w, ssd_conv_w, ssd_conv_b, ssd_a_log, ssd_dt_bias, ssd_d, ssd_norm_w, w_br_a, w_br_b, w_br_c, w_out, norm2_w, ffn_w_up, ffn_conv_w, ffn_conv_b, ffn_w_down, final_norm_w, loss_target, m_w_ada, m_b_ada, m_norm1_w, m_w_in, m_gdn_conv_w, m_gdn_a_log, m_gdn_dt_bias, m_gdn_norm_w, m_hgrn_lb_param, m_hgrn_norm_w, m_ssd_conv_w, m_ssd_conv_b, m_ssd_a_log, m_ssd_dt_bias, m_ssd_d, m_ssd_norm_w, m_w_br_a, m_w_br_b, m_w_br_c, m_w_out, m_norm2_w, m_ffn_w_up, m_ffn_conv_w, m_ffn_conv_b, m_ffn_w_down, m_final_norm_w, v_w_ada, v_b_ada, v_norm1_w, v_w_in, v_gdn_conv_w, v_gdn_a_log, v_gdn_dt_bias, v_gdn_norm_w, v_hgrn_lb_param, v_hgrn_norm_w, v_ssd_conv_w, v_ssd_conv_b, v_ssd_a_log, v_ssd_dt_bias, v_ssd_d, v_ssd_norm_w, v_w_br_a, v_w_br_b, v_w_br_c, v_w_out, v_norm2_w, v_ffn_w_up, v_ffn_conv_w, v_ffn_conv_b, v_ffn_w_down, v_final_norm_w):
    given = dict(locals())
    w = {n: given[n] for n in WEIGHTS}
    m = {n: given["m_" + n] for n in WEIGHTS}
    v = {n: given["v_" + n] for n in WEIGHTS}
    me = _my_index()
    bsz = c.shape[0]
    ncol = 6 * D_MODEL // N_DEV

    shards = [w[n].astype(BF16) if n in MATMUL_WEIGHTS else w[n] for n, _, _ in SPLIT] + [c]
    gathered = _gather_two_level(shards, "gather_weights")
    wf = {n: _join_cols(g, n == "w_in", f"join_{n}") if n in UNALIGNED else _join_blocks(g, shape, axis)
          for (n, shape, axis), g in zip(SPLIT, gathered)}
    c_all = gathered[-1].reshape(N_DEV * bsz, D_MODEL)

    b_cols = lax.dynamic_slice_in_dim(b_ada, me * ncol, ncol, axis=1)[:, None]
    mod_cols = _ada_fwd(c_all, w_ada, b_cols)
    send = mod_cols.reshape(DEPTH, N_DEV, bsz, ncol).transpose(1, 0, 2, 3)
    got = _all_to_all([send], "scatter_mod")[0]
    mod = got.transpose(1, 2, 0, 3).reshape(DEPTH, bsz, 6 * D_MODEL)

    loss, dx, grads, dmod = _local_step(x, mod, wf, w, loss_target)

    send = dmod.reshape(DEPTH, bsz, N_DEV, ncol).transpose(2, 0, 1, 3)
    got_dmod = _all_to_all([send], "scatter_dmod")[0]
    dmod_all = got_dmod.transpose(1, 0, 2, 3).reshape(DEPTH, N_DEV * bsz, ncol)
    g_w_ada, g_b_cols = _ada_bwd(c_all.T, dmod_all)

    core = lax.axis_index("c")
    by_core = []
    for n, shape, axis in SPLIT:
        if n in UNALIGNED:
            by_core.append(_split_cols(grads[n], n == "w_in", shape[axis] // N_DEV, f"split_{n}"))
            continue
        parts = _split_blocks(grads[n], shape, axis).astype(BF16)
        parts = parts.reshape((N_DEV // 2, 2, -1, parts.shape[-1]))
        by_core.append(jnp.swapaxes(parts, 0, 1))
    from_sibling = _sibling_swap(by_core, "swap_grads")
    sums = [_pair_sum(lax.dynamic_index_in_dim(mine, core, 0, keepdims=False), theirs, f"pair_sum_{n}")
            for (n, _, _), mine, theirs in zip(SPLIT, by_core, from_sibling)]
    got = _chip_scatter(sums, "scatter_grads")
    grads["b_ada"] = lax.dynamic_update_slice_in_dim(jnp.zeros_like(b_ada), g_b_cols[:, 0], me * ncol, axis=1)

    out = {}
    slots = [(n, g8) for (n, _, _), g8 in zip(SPLIT, got)] + [("w_ada", g_w_ada[None])]
    for n, gs in slots:
        out[n] = _sum_adamw(gs.reshape((gs.shape[0],) + w[n].shape), w[n], m[n], v[n], f"adamw_{n}")
    r8 = _gather_two_level([_pack_repl(grads)], "gather_small_grads")[0]
    res = _sum_adamw(r8[:, None], _pack_repl(w)[None], _pack_repl(m)[None], _pack_repl(v)[None], "adamw_repl")
    repl_out = [_unpack_repl(o[0]) for o in res]
    pick = lambda i, n: out[n][i] if n in out else repl_out[i][n]
    loss = lax.psum(loss, ("x", "y", "c"))
    return (loss, dx, *[pick(i, n) for i in range(4) for n in WEIGHTS])
```

```python
import functools
import math

import jax
import jax.numpy as jnp
from jax import lax
from jax.experimental import pallas as pl
from jax.experimental.pallas import tpu as pltpu

F32, BF16 = jnp.float32, jnp.bfloat16
HI = lax.Precision.HIGHEST
MESH_ID = pl.DeviceIdType.MESH

N_DEV = 8
EPS = 1e-6
D_MODEL = 1024
DEPTH = 2
GDN_HEADS, GDN_DK, GDN_CHUNK = 4, 128, 64
HGRN_HEADS, HGRN_DK, HGRN_CHUNK, HGRN_BLOCK = 4, 128, 16, 128
SSD_HEADS, SSD_HEAD_DIM, SSD_GROUPS, SSD_STATE, SSD_CHUNK = 8, 64, 2, 128, 64
SSD_INNER = SSD_HEADS * SSD_HEAD_DIM
FFN_HIDDEN = 2816
LANES = 128
P_QKV, P_GZ, P_XBC, P_GATE, P_HQ, P_HF, P_HI, P_HG, P_SZ, P_SMALL, P_WIDTH = (
    0, 1536, 2048, 3072, 6144, 6656, 7168, 7680, 8192, 8704, 9216)
SM_A, SM_B, SM_DT = 0, 4, 8
W_IN_SPLITS = (1536, 4, 4, 512, 512, 512, 512, 512, 512, 1024, 8, 3072)

ADAM_LR, ADAM_B1, ADAM_B2, ADAM_EPS, ADAM_WD, ADAM_STEP = 0.001, 0.9, 0.999, 1e-08, 0.01, 10

V7X_VMEM_LIMIT = 56 * 1024 * 1024
MM_OPERAND_VMEM = 34 * 1024 * 1024
PACK_W = 1024


def _call(body, name, grid, in_specs, out_specs, out_shape, scratch=(), aliases=None):
    return pl.pallas_call(
        body, name=name, grid=grid, in_specs=in_specs, out_specs=out_specs, out_shape=out_shape,
        scratch_shapes=list(scratch), input_output_aliases=aliases or {},
        compiler_params=pltpu.CompilerParams(
            dimension_semantics=("arbitrary",) * len(grid), vmem_limit_bytes=V7X_VMEM_LIMIT),
    )


IN_PLACE = pl.BlockSpec(memory_space=pl.ANY)


def _pick(n, cands):
    for c in cands:
        if n % c == 0:
            return c
    raise ValueError(f"no tile for {n} among {cands}")


def _row_tile(s, cap):
    t = cap
    while s % t:
        t //= 2
    return t


def _sds(shape, dtype):
    return jax.ShapeDtypeStruct(shape, dtype)


def _dot(a, b):
    return lax.dot_general(a, b, (((1,), (0,)), ((), ())), precision=HI, preferred_element_type=F32)


NN, NT, TN = (((1,), (0,)), ((), ())), (((1,), (1,)), ((), ())), (((0,), (0,)), ((), ()))


def _mxu(a, b, dims):
    return lax.dot_general(a.astype(BF16), b.astype(BF16), dims, preferred_element_type=F32)


@jax.custom_vjp
def _bdot(a, b):
    return _mxu(a, b, NN)


@jax.custom_vjp
def _bdot_nt(a, b):
    return _mxu(a, b, NT)


@jax.custom_vjp
def _bdot_tn(a, b):
    return _mxu(a, b, TN)


_bdot.defvjp(lambda a, b: (_mxu(a, b, NN), (a, b)), lambda r, d: (_mxu(d, r[1], NT), _mxu(r[0], d, TN)))
_bdot_nt.defvjp(lambda a, b: (_mxu(a, b, NT), (a, b)), lambda r, d: (_mxu(d, r[1], NN), _mxu(d, r[0], TN)))
_bdot_tn.defvjp(lambda a, b: (_mxu(a, b, TN), (a, b)), lambda r, d: (_mxu(r[1], d, NT), _mxu(r[0], d, NN)))


def _split(x):
    hi = x.astype(BF16)
    return hi, (x - hi.astype(F32)).astype(BF16)


def _mxu3(a, b, dims):
    ah, al = _split(a)
    bh, bl = _split(b)
    return _mxu(ah, bh, dims) + (_mxu(ah, bl, dims) + _mxu(al, bh, dims))


@jax.custom_vjp
def _dot3(a, b):
    return _mxu3(a, b, NN)


_dot3.defvjp(lambda a, b: (_mxu3(a, b, NN), (a, b)), lambda r, d: (_mxu3(d, r[1], NT), _mxu3(r[0], d, TN)))


def _pieces(x):
    x1 = x.astype(BF16)
    r1 = x - x1.astype(F32)
    x2 = r1.astype(BF16)
    return x1, x2, (r1 - x2.astype(F32)).astype(BF16)


def _mask_mxu(mask, x, dims):
    x1, x2, x3 = _pieces(x)
    return _mxu(mask, x1, dims) + (_mxu(mask, x2, dims) + _mxu(mask, x3, dims))


def _spread_mxu(x, mask, dims):
    x1, x2, x3 = _pieces(x)
    return _mxu(x1, mask, dims) + (_mxu(x2, mask, dims) + _mxu(x3, mask, dims))


@jax.custom_vjp
def _mask_dot(mask, x):
    return _mask_mxu(mask, x, NN)


@jax.custom_vjp
def _spread_dot(x, mask):
    return _spread_mxu(x, mask, NN)


_mask_dot.defvjp(lambda m, x: (_mask_mxu(m, x, NN), m), lambda m, d: (jnp.zeros_like(m), _mask_mxu(m, d, TN)))
_spread_dot.defvjp(lambda x, m: (_spread_mxu(x, m, NN), m), lambda m, d: (_spread_mxu(d, m, NT), jnp.zeros_like(m)))


def _iota(shape, axis):
    return lax.broadcasted_iota(jnp.int32, shape, axis)


def _silu(x):
    return x * jax.nn.sigmoid(x)


def _softplus(x):
    return jnp.maximum(x, 0.0) + jnp.log1p(jnp.exp(-jnp.abs(x)))


def _rms(x, w):
    return x * lax.rsqrt(jnp.mean(x * x, axis=-1, keepdims=True) + EPS) * w


def _lane_col(x, lane):
    m = (_iota(x.shape, 1) == lane).astype(F32)
    return jnp.sum(x * m, axis=1, keepdims=True)


def _col_to_row(c):
    n = c.shape[0]
    eye = (_iota((n, n), 0) == _iota((n, n), 1)).astype(F32)
    return jnp.sum(c * eye, axis=0, keepdims=True)


def _tril(n, strict=False):
    r, c = _iota((n, n), 0), _iota((n, n), 1)
    return (r > c) if strict else (r >= c)


def _mm(a, b, out_dtype, name):
    m, k = a.shape
    n = b.shape[1]
    tm = _pick(m, (1024, 1408, 512, 256, 128, 64, 32, 16, 8))
    tn = _pick(n, (1024, 1408, 768, 512, 384, 256, 128))
    fits = lambda c: k % c == 0 and (tm + tn) * c * 2 * 2 <= MM_OPERAND_VMEM
    tk = next(c for c in (k, 4096, 3072, 2816, 2048, 1024, 768, 512, 384, 256, 128) if fits(c))
    nk = k // tk

    def body_one(a_ref, b_ref, o_ref):
        o_ref[...] = _bdot(a_ref[...], b_ref[...]).astype(out_dtype)

    def body(a_ref, b_ref, o_ref, acc_ref):
        kk = pl.program_id(2)

        @pl.when(kk == 0)
        def _():
            acc_ref[...] = jnp.zeros_like(acc_ref)

        acc_ref[...] += _bdot(a_ref[...], b_ref[...])

        @pl.when(kk == nk - 1)
        def _():
            o_ref[...] = acc_ref[...].astype(out_dtype)

    return _call(
        body_one if nk == 1 else body, name, (m // tm, n // tn, nk),
        [pl.BlockSpec((tm, tk), lambda i, j, kk: (i, kk)), pl.BlockSpec((tk, tn), lambda i, j, kk: (kk, j))],
        pl.BlockSpec((tm, tn), lambda i, j, kk: (i, j)), _sds((m, n), out_dtype),
        scratch=[] if nk == 1 else [pltpu.VMEM((tm, tn), F32)],
    )(a, b)


def _mm_resid(a, b, x, gate, name):
    bsz, s, d = x.shape
    k = a.shape[1]
    tm = _row_tile(s, 1024)
    tn = _pick(d, (1024, 512, 256, 128))
    assert (tm + tn) * k * 2 * 2 <= MM_OPERAND_VMEM, "one K step only"
    per_seq = s // tm

    def body(a_ref, b_ref, x_ref, g_ref, y_ref, o_ref):
        y = _bdot(a_ref[...], b_ref[...])
        y_ref[...] = y
        o_ref[...] = x_ref[...] + g_ref[...] * y

    x2, rows = x.reshape(bsz * s, d), pl.BlockSpec((tm, tn), lambda i, j: (i, j))
    y, out = _call(
        body, name, (bsz * per_seq, d // tn),
        [pl.BlockSpec((tm, k), lambda i, j: (i, 0)), pl.BlockSpec((k, tn), lambda i, j: (0, j)), rows,
         pl.BlockSpec((None, 1, tn), lambda i, j: (i // per_seq, 0, j))],
        [rows, rows], [_sds((bsz * s, d), F32)] * 2,
    )(a, b, x2, gate)
    return y.reshape(x.shape), out.reshape(x.shape)


def _ada_fwd(c_all, w, b):
    depth, _, n = w.shape
    rows = c_all.shape[0]

    def body(c_ref, w_ref, b_ref, o_ref):
        o_ref[...] = _dot(_silu(c_ref[...]), w_ref[...]) + b_ref[...]

    return _call(
        body, "ada_fwd", (depth,),
        [pl.BlockSpec((rows, D_MODEL), lambda l: (0, 0)), pl.BlockSpec((None, D_MODEL, n), lambda l: (l, 0, 0)),
         pl.BlockSpec((None, 1, n), lambda l: (l, 0, 0))],
        pl.BlockSpec((None, rows, n), lambda l: (l, 0, 0)), _sds((depth, rows, n), F32),
    )(c_all, w, b)


def _ada_bwd(c_all_t, dmod):
    depth, rows, n = dmod.shape

    def body(ct_ref, dm_ref, dw_ref, db_ref):
        dm = dm_ref[...]
        dw_ref[...] = _dot(_silu(ct_ref[...]), dm)
        db_ref[...] = jnp.sum(dm, axis=0, keepdims=True)

    return _call(
        body, "ada_bwd", (depth,),
        [pl.BlockSpec((D_MODEL, rows), lambda l: (0, 0)), pl.BlockSpec((None, rows, n), lambda l: (l, 0, 0))],
        [pl.BlockSpec((None, D_MODEL, n), lambda l: (l, 0, 0)), pl.BlockSpec((None, 1, n), lambda l: (l, 0, 0))],
        [_sds((depth, D_MODEL, n), F32), _sds((depth, 1, n), F32)],
    )(c_all_t, dmod)


def _lb_fn(p):
    rows = [p[l:l + 1] for l in range(DEPTH)]
    mx = functools.reduce(jnp.maximum, rows)
    es = [jnp.exp(r - mx) for r in rows]
    tot = functools.reduce(lambda a, b: a + b, es)
    sm = [e / tot for e in es]
    out, run = [], None
    for l in range(DEPTH):
        run = sm[l] if run is None else run + sm[l]
        out.append(run - sm[0])
    return jnp.concatenate(out, axis=0)


def _lb_fwd(p):
    def body(p_ref, o_ref):
        o_ref[...] = _lb_fn(p_ref[...])

    full = pl.BlockSpec(p.shape, lambda i: (0, 0))
    return _call(body, "lb_fwd", (1,), [full], full, _sds(p.shape, F32))(p)


def _lb_bwd(p, d_lower):
    def body(p_ref, d_ref, o_ref):
        _, vjp = jax.vjp(_lb_fn, p_ref[...])
        o_ref[...] = vjp(d_ref[...])[0]

    full = pl.BlockSpec(p.shape, lambda i: (0, 0))
    return _call(body, "lb_bwd", (1,), [full, full], full, _sds(p.shape, F32))(p, d_lower)


def _norm_mod_fn(x, w, shift, scale):
    return _rms(x, w) * (1.0 + scale) + shift


def _norm_mod_fwd(x, w, shift, scale, name):
    bsz, s, d = x.shape
    ts = _row_tile(s, 512)

    def body(x_ref, w_ref, sh_ref, sc_ref, o_ref):
        o_ref[...] = _norm_mod_fn(x_ref[...], w_ref[...], sh_ref[...], sc_ref[...]).astype(BF16)

    row = pl.BlockSpec((None, ts, d), lambda b, i: (b, i, 0))
    per_b = pl.BlockSpec((None, 1, d), lambda b, i: (b, 0, 0))
    return _call(body, name, (bsz, s // ts), [row, pl.BlockSpec((1, d), lambda b, i: (0, 0)), per_b, per_b],
                 row, _sds(x.shape, BF16))(x, w, shift, scale)


def _norm_mod_bwd(x, w, shift, scale, dh, carry, name):
    bsz, s, d = x.shape
    ts = _row_tile(s, 512)

    def body(x_ref, w_ref, sh_ref, sc_ref, dh_ref, c_ref, dx_ref, dw_ref, dsh_ref, dsc_ref):
        b, i = pl.program_id(0), pl.program_id(1)
        _, vjp = jax.vjp(_norm_mod_fn, x_ref[...], w_ref[...], sh_ref[...], sc_ref[...])
        dx, dw, dsh, dsc = vjp(dh_ref[...])
        dx_ref[...] = dx + c_ref[...]

        @pl.when((b == 0) & (i == 0))
        def _():
            dw_ref[...] = jnp.zeros_like(dw_ref)

        @pl.when(i == 0)
        def _():
            dsh_ref[...] = jnp.zeros_like(dsh_ref)
            dsc_ref[...] = jnp.zeros_like(dsc_ref)

        dw_ref[...] += dw
        dsh_ref[...] += dsh
        dsc_ref[...] += dsc

    row = pl.BlockSpec((None, ts, d), lambda b, i: (b, i, 0))
    per_b = pl.BlockSpec((None, 1, d), lambda b, i: (b, 0, 0))
    wspec = pl.BlockSpec((1, d), lambda b, i: (0, 0))
    return _call(body, name, (bsz, s // ts), [row, wspec, per_b, per_b, row, row],
                 [row, wspec, per_b, per_b],
                 [_sds(x.shape, F32), _sds((1, d), F32), _sds((bsz, 1, d), F32), _sds((bsz, 1, d), F32)],
                 )(x, w, shift, scale, dh, carry)


def _gate_bwd(dx, y, gate, name):
    bsz, s, d = dx.shape
    ts = _row_tile(s, 1024)

    def body(dx_ref, y_ref, g_ref, dy_ref, dg_ref):
        dxv = dx_ref[...]
        dy_ref[...] = (dxv * g_ref[...]).astype(BF16)

        @pl.when(pl.program_id(1) == 0)
        def _():
            dg_ref[...] = jnp.zeros_like(dg_ref)

        dg_ref[...] += jnp.sum(dxv * y_ref[...], axis=0, keepdims=True)

    row = pl.BlockSpec((None, ts, d), lambda b, i: (b, i, 0))
    per_b = pl.BlockSpec((None, 1, d), lambda b, i: (b, 0, 0))
    return _call(body, name, (bsz, s // ts), [row, row, per_b], [row, per_b],
                 [_sds(dx.shape, BF16), _sds((bsz, 1, d), F32)])(dx, y, gate)


HALO = 8


def _conv_pre(xx, w_ref, b_ref, kw, rows):
    acc = w_ref[kw - 1:kw, :] * xx[HALO:HALO + rows]
    for k in range(kw - 1):
        acc = acc + w_ref[k:k + 1, :] * pltpu.roll(xx, kw - 1 - k, 0)[HALO:HALO + rows]
    return acc + b_ref[...]


def _conv_fwd(x, col0, width, w, b, name):
    bsz, s, _ = x.shape
    kw = w.shape[0]
    ts = _row_tile(s, 1024)
    tc = _pick(width, (512, 256, 128))
    assert col0 % tc == 0
    c0 = col0 // tc
    hb = ts // HALO

    def body(x_ref, xp_ref, w_ref, b_ref, o_ref):
        i = pl.program_id(1)
        xp = jnp.where(i > 0, xp_ref[...], 0.0)
        xx = jnp.concatenate([xp, x_ref[...]], axis=0)
        pre = _conv_pre(xx, w_ref, b_ref, kw, ts)
        o_ref[...] = _silu(pre)

    return _call(
        body, name, (bsz, s // ts, width // tc),
        [pl.BlockSpec((None, ts, tc), lambda bb, i, j: (bb, i, c0 + j)),
         pl.BlockSpec((None, HALO, tc), lambda bb, i, j: (bb, jnp.maximum(i * hb - 1, 0), c0 + j)),
         pl.BlockSpec((kw, tc), lambda bb, i, j: (0, j)), pl.BlockSpec((1, tc), lambda bb, i, j: (0, j))],
        pl.BlockSpec((None, ts, tc), lambda bb, i, j: (bb, i, j)), _sds((bsz, s, width), F32),
    )(x, x, w, b)


def _conv_bwd(dy, x, col0, width, w, b, dp, name):
    bsz, s, _ = x.shape
    kw = w.shape[0]
    ts = _row_tile(s, 1024)
    tc = _pick(width, (512, 256, 128))
    c0 = col0 // tc
    hb = ts // HALO
    nt = s // ts
    last_h = s // HALO - 1

    def body(x_ref, xp_ref, xn_ref, dy_ref, dyn_ref, w_ref, b_ref, _, dx_ref, dw_ref, db_ref):
        bb, i = pl.program_id(1), pl.program_id(2)
        xp = jnp.where(i > 0, xp_ref[...], 0.0)
        xx = jnp.concatenate([xp, x_ref[...], xn_ref[...]], axis=0)
        dyy = jnp.concatenate([dy_ref[...], jnp.where(i < nt - 1, dyn_ref[...], 0.0)], axis=0)
        n = ts + HALO
        pre = _conv_pre(xx, w_ref, b_ref, kw, n)
        sg = jax.nn.sigmoid(pre)
        dpre = dyy * (sg * (1.0 + pre * (1.0 - sg)))
        dx = w_ref[kw - 1:kw, :] * dpre[:ts]
        for k in range(kw - 1):
            dx = dx + w_ref[k:k + 1, :] * pltpu.roll(dpre, n - (kw - 1 - k), 0)[:ts]
        dx_ref[...] = dx.astype(BF16)

        @pl.when((bb == 0) & (i == 0))
        def _():
            dw_ref[...] = jnp.zeros_like(dw_ref)
            db_ref[...] = jnp.zeros_like(db_ref)

        dt = dpre[:ts]
        db_ref[...] += jnp.sum(dt, axis=0, keepdims=True)
        dw_ref[kw - 1:kw, :] += jnp.sum(dt * xx[HALO:HALO + ts], axis=0, keepdims=True)
        for k in range(kw - 1):
            xs = pltpu.roll(xx, kw - 1 - k, 0)[HALO:HALO + ts]
            dw_ref[k:k + 1, :] += jnp.sum(dt * xs, axis=0, keepdims=True)

    xspec = lambda f: pl.BlockSpec((None, HALO, tc), f)
    return _call(
        body, name, (width // tc, bsz, nt),
        [pl.BlockSpec((None, ts, tc), lambda j, bb, i: (bb, i, c0 + j)),
         xspec(lambda j, bb, i: (bb, jnp.maximum(i * hb - 1, 0), c0 + j)),
         xspec(lambda j, bb, i: (bb, jnp.minimum((i + 1) * hb, last_h), c0 + j)),
         pl.BlockSpec((None, ts, tc), lambda j, bb, i: (bb, i, j)),
         xspec(lambda j, bb, i: (bb, jnp.minimum((i + 1) * hb, last_h), j)),
         pl.BlockSpec((kw, tc), lambda j, bb, i: (0, j)), pl.BlockSpec((1, tc), lambda j, bb, i: (0, j)), IN_PLACE],
        [pl.BlockSpec((None, ts, tc), lambda j, bb, i: (bb, i, c0 + j)),
         pl.BlockSpec((kw, tc), lambda j, bb, i: (0, j)), pl.BlockSpec((1, tc), lambda j, bb, i: (0, j))],
        [_sds(dp.shape, BF16), _sds((kw, width), F32), _sds((1, width), F32)], aliases={7: 0},
    )(x, x, x, dy, dy, w, b, dp)


def _conv_glu_fwd(x, w, b, name):
    bsz, s, f2 = x.shape
    f = f2 // 2
    kw = w.shape[0]
    ts = _row_tile(s, 2048)
    tc = _pick(f, (256, 128))
    nf = f // tc
    hb = ts // HALO

    def body(xg_ref, xgp_ref, xv_ref, xvp_ref, wg_ref, wv_ref, bg_ref, bv_ref, o_ref):
        i = pl.program_id(1)
        halves = []
        for x_ref, xp_ref, w_ref, b_ref in ((xg_ref, xgp_ref, wg_ref, bg_ref), (xv_ref, xvp_ref, wv_ref, bv_ref)):
            xx = jnp.concatenate([jnp.where(i > 0, xp_ref[...], 0.0), x_ref[...]], axis=0)
            halves.append(_conv_pre(xx, w_ref, b_ref, kw, ts))
        o_ref[...] = (_silu(halves[0]) * halves[1]).astype(BF16)

    tile = lambda off: pl.BlockSpec((None, ts, tc), lambda bb, i, j: (bb, i, off + j))
    prev = lambda off: pl.BlockSpec((None, HALO, tc), lambda bb, i, j: (bb, jnp.maximum(i * hb - 1, 0), off + j))
    wsp = lambda rows, off: pl.BlockSpec((rows, tc), lambda bb, i, j: (0, off + j))
    return _call(
        body, name, (bsz, s // ts, nf),
        [tile(0), prev(0), tile(nf), prev(nf), wsp(kw, 0), wsp(kw, nf), wsp(1, 0), wsp(1, nf)],
        pl.BlockSpec((None, ts, tc), lambda bb, i, j: (bb, i, j)), _sds((bsz, s, f), BF16),
    )(x, x, x, x, w, w, b, b)


def _conv_glu_bwd(da, x, w, b, name):
    bsz, s, f2 = x.shape
    f = f2 // 2
    kw = w.shape[0]
    ts = _row_tile(s, 2048)
    tc = _pick(f, (256, 128))
    nf = f // tc
    hb = ts // HALO
    nt = s // ts
    last_h = s // HALO - 1
    n = ts + HALO

    def body(xg_ref, xgp_ref, xgn_ref, xv_ref, xvp_ref, xvn_ref, da_ref, dan_ref,
             wg_ref, wv_ref, bg_ref, bv_ref, wx_ref, dx_ref, dw_ref, db_ref):
        j, bb, i = pl.program_id(0), pl.program_id(1), pl.program_id(2)
        day = jnp.concatenate([da_ref[...], jnp.where(i < nt - 1, dan_ref[...], 0.0)], axis=0)
        xg = jnp.concatenate([jnp.where(i > 0, xgp_ref[...], 0.0), xg_ref[...], xgn_ref[...]], axis=0)
        pre_g = _conv_pre(xg, wg_ref, bg_ref, kw, n)
        sg = jax.nn.sigmoid(pre_g)

        @pl.when((bb == 0) & (i == 0))
        def _():
            dw_ref[...] = jnp.zeros_like(dw_ref)
            db_ref[...] = jnp.zeros_like(db_ref)

        def finish(dpre, xx):
            dx = wx_ref[kw - 1:kw, :] * dpre[:ts]
            for k in range(kw - 1):
                dx = dx + wx_ref[k:k + 1, :] * pltpu.roll(dpre, n - (kw - 1 - k), 0)[:ts]
            dx_ref[...] = dx.astype(BF16)
            dt = dpre[:ts]
            db_ref[...] += jnp.sum(dt, axis=0, keepdims=True)
            dw_ref[kw - 1:kw, :] += jnp.sum(dt * xx[HALO:HALO + ts], axis=0, keepdims=True)
            for k in range(kw - 1):
                dw_ref[k:k + 1, :] += jnp.sum(dt * pltpu.roll(xx, kw - 1 - k, 0)[HALO:HALO + ts], axis=0, keepdims=True)

        @pl.when(j < nf)
        def _():
            xv = jnp.concatenate([jnp.where(i > 0, xvp_ref[...], 0.0), xv_ref[...], xvn_ref[...]], axis=0)
            pre_v = _conv_pre(xv, wv_ref, bv_ref, kw, n)
            finish(day * pre_v * (sg * (1.0 + pre_g * (1.0 - sg))), xg)

        @pl.when(j >= nf)
        def _():
            xv = jnp.concatenate([jnp.where(i > 0, xvp_ref[...], 0.0), xv_ref[...], xvn_ref[...]], axis=0)
            finish(day * (pre_g * sg), xv)

    tile = lambda off: pl.BlockSpec((None, ts, tc), lambda j, bb, i: (bb, i, off + j % nf))
    prev = lambda off: pl.BlockSpec((None, HALO, tc), lambda j, bb, i: (bb, jnp.maximum(i * hb - 1, 0), off + j % nf))
    nxt = lambda off: pl.BlockSpec((None, HALO, tc), lambda j, bb, i: (bb, jnp.minimum((i + 1) * hb, last_h), off + j % nf))
    wsp = lambda rows, off: pl.BlockSpec((rows, tc), lambda j, bb, i: (0, off + j % nf))
    own = lambda rows: pl.BlockSpec((rows, tc), lambda j, bb, i: (0, j))
    return _call(
        body, name, (2 * nf, bsz, nt),
        [tile(0), prev(0), nxt(0), tile(nf), prev(nf), nxt(nf), tile(0), nxt(0),
         wsp(kw, 0), wsp(kw, nf), wsp(1, 0), wsp(1, nf), own(kw)],
        [pl.BlockSpec((None, ts, tc), lambda j, bb, i: (bb, i, j)), own(kw), own(1)],
        [_sds((bsz, s, f2), BF16), _sds((kw, f2), F32), _sds((1, f2), F32)],
    )(x, x, x, x, x, x, da, da, w, w, b, b, w)


def _merge_fwd(p, oa, ob, oc, wa, wb, wc, name):
    bsz, s, _ = p.shape
    tm = _row_tile(s, 512)
    gblk = P_GATE // (3 * D_MODEL)

    def body(g_ref, oa_ref, ob_ref, oc_ref, wa_ref, wb_ref, wc_ref, o_ref):
        acc = None
        for i, (o_r, w_r) in enumerate(((oa_ref, wa_ref), (ob_ref, wb_ref), (oc_ref, wc_ref))):
            y = _bdot(o_r[...], w_r[...])
            t = jax.nn.sigmoid(g_ref[:, i * D_MODEL:(i + 1) * D_MODEL]) * y
            acc = t if acc is None else acc + t
        o_ref[...] = acc.astype(BF16)

    orow = pl.BlockSpec((None, tm, 512), lambda b, i: (b, i, 0))
    wfull = pl.BlockSpec((512, D_MODEL), lambda b, i: (0, 0))
    return _call(
        body, name, (bsz, s // tm),
        [pl.BlockSpec((None, tm, 3 * D_MODEL), lambda b, i: (b, i, gblk)), orow, orow, orow, wfull, wfull, wfull],
        pl.BlockSpec((None, tm, D_MODEL), lambda b, i: (b, i, 0)), _sds((bsz, s, D_MODEL), BF16),
    )(p, oa, ob, oc, wa, wb, wc)


def _merge_bwd(dm, p, oa, ob, oc, wa, wb, wc, wat, wbt, wct, name):
    bsz, s, _ = p.shape
    tm = _row_tile(s, 512)
    gblk = P_GATE // (3 * D_MODEL)

    def body(dm_ref, g_ref, oa_ref, ob_ref, oc_ref, wa_ref, wb_ref, wc_ref, wat_ref, wbt_ref, wct_ref,
             dg_ref, doa_ref, dob_ref, doc_ref, dya_ref, dyb_ref, dyc_ref):
        dmv = dm_ref[...]
        trip = ((oa_ref, wa_ref, wat_ref, doa_ref, dya_ref), (ob_ref, wb_ref, wbt_ref, dob_ref, dyb_ref),
                (oc_ref, wc_ref, wct_ref, doc_ref, dyc_ref))
        for i, (o_r, w_r, wt_r, do_r, dy_r) in enumerate(trip):
            y = _bdot(o_r[...], w_r[...])
            sg = jax.nn.sigmoid(g_ref[:, i * D_MODEL:(i + 1) * D_MODEL])
            dg_ref[:, i * D_MODEL:(i + 1) * D_MODEL] = (dmv * y * sg * (1.0 - sg)).astype(BF16)
            dy = (dmv * sg).astype(BF16)
            dy_r[...] = dy
            do_r[...] = _bdot(dy, wt_r[...])

    orow = pl.BlockSpec((None, tm, 512), lambda b, i: (b, i, 0))
    drow = pl.BlockSpec((None, tm, D_MODEL), lambda b, i: (b, i, 0))
    grow = pl.BlockSpec((None, tm, 3 * D_MODEL), lambda b, i: (b, i, gblk))
    wfull = pl.BlockSpec((512, D_MODEL), lambda b, i: (0, 0))
    wtfull = pl.BlockSpec((D_MODEL, 512), lambda b, i: (0, 0))
    return _call(
        body, name, (bsz, s // tm),
        [drow, grow, orow, orow, orow, wfull, wfull, wfull, wtfull, wtfull, wtfull],
        [grow, orow, orow, orow, drow, drow, drow],
        [_sds(p.shape, BF16)] + [_sds((bsz, s, 512), F32)] * 3 + [_sds((bsz, s, D_MODEL), BF16)] * 3,
    )(dm, p, oa, ob, oc, wa, wb, wc, wat, wbt, wct)


def _final_loss(x, w, target):
    bsz, s, d = x.shape
    ts = _row_tile(s, 512)

    def body(x_ref, w_ref, t_ref, loss_ref, dx_ref, dw_ref):
        first = (pl.program_id(0) == 0) & (pl.program_id(1) == 0)
        y, vjp = jax.vjp(_rms, x_ref[...], w_ref[...])
        err = y - t_ref[...]
        dx, dw = vjp(err * (1.0 / d))
        dx_ref[...] = dx

        @pl.when(first)
        def _():
            loss_ref[...] = jnp.zeros_like(loss_ref)
            dw_ref[...] = jnp.zeros_like(dw_ref)

        loss_ref[...] += 0.5 * jnp.sum(jnp.sum(err * err, axis=1, keepdims=True), axis=0, keepdims=True) * (1.0 / d)
        dw_ref[...] += dw

    row = pl.BlockSpec((None, ts, d), lambda b, i: (b, i, 0))
    wspec = pl.BlockSpec((1, d), lambda b, i: (0, 0))
    return _call(body, "final_loss", (bsz, s // ts), [row, wspec, row],
                 [pl.BlockSpec((8, LANES), lambda b, i: (0, 0)), row, wspec],
                 [_sds((8, LANES), F32), _sds(x.shape, F32), _sds((1, d), F32)])(x, w, target)


def _unit_lower_inverses(ms):
    n = ms[0].shape[0]
    r, c = _iota((n, n), 0), _iota((n, n), 1)
    same = lambda size: (r // size) == (c // size)
    xs = [(r == c).astype(F32) - jnp.where(same(2), m, 0.0) for m in ms]
    size = 4
    while size <= n:
        below = same(size) & ~same(size // 2)
        xs = [x - _mxu3(x, _mxu3(jnp.where(below, m, 0.0), x, NN), NN) for x, m in zip(xs, ms)]
        size *= 2
    return xs


@jax.custom_vjp
def _known_inverse(m, t):
    return t


_known_inverse.defvjp(lambda m, t: (t, t), lambda t, dt: (-_mxu3(t, _mxu3(dt, t, NT), TN), jnp.zeros_like(t)))


def _gdn_chunk(states, qkv, small, z, a_row, dt_row, nw, tinvs=None):
    nb = len(qkv)
    c = qkv[0].shape[0]
    kw = GDN_HEADS * GDN_DK
    incl, strict = _tril(c), _tril(c, True)
    g_all = [-jnp.exp(a_row) * _softplus(small[b] + dt_row) for b in range(nb)]
    beta_all = [jax.nn.sigmoid(small[b]) for b in range(nb)]
    big_g_all = [_mask_dot(incl.astype(BF16), g_all[b]) for b in range(nb)]
    items = [(b, h) for b in range(nb) for h in range(GDN_HEADS)]
    ids = range(len(items))
    col = lambda b, part, h: qkv[b][:, part * kw + h * GDN_DK:part * kw + (h + 1) * GDN_DK]
    unit = lambda t: t * lax.rsqrt(jnp.sum(t * t, axis=-1, keepdims=True) + EPS)
    q = [unit(col(b, 0, h)) * (GDN_DK ** -0.5) for b, h in items]
    k = [unit(col(b, 1, h)) for b, h in items]
    v = [col(b, 2, h) for b, h in items]
    gc = [_lane_col(big_g_all[b], SM_A + h) for b, h in items]
    bc = [_lane_col(beta_all[b], SM_B + h) for b, h in items]
    g_last = [jnp.sum(_lane_col(g_all[b], SM_A + h), axis=0, keepdims=True) for b, h in items]
    decay = [jnp.where(incl, jnp.exp(jnp.where(incl, gc[i] - _col_to_row(gc[i]), 0.0)), 0.0) for i in ids]
    kb = [k[i] * bc[i] for i in ids]
    m = [jnp.where(strict, _bdot_nt(kb[i], k[i]) * decay[i], 0.0) for i in ids]
    if tinvs is None:
        tinv = _unit_lower_inverses(m)
    else:
        tinv = [_known_inverse(m[i], tinvs[i]) for i in ids]
    eg = [jnp.exp(gc[i]) for i in ids]
    u = [_dot3(tinv[i], v[i] * bc[i]) for i in ids]
    w = [_dot3(tinv[i], kb[i] * eg[i]) for i in ids]
    attn = [_bdot_nt(q[i], k[i]) * decay[i] for i in ids]
    v_new = [u[i] - _bdot(w[i], states[i]) for i in ids]
    o_st = [_bdot(q[i] * eg[i], states[i]) for i in ids]
    o = [o_st[i] + _bdot(attn[i], v_new[i]) for i in ids]
    grow = [_bdot_tn(k[i] * jnp.exp(g_last[i] - gc[i]), v_new[i]) for i in ids]
    new_states = [states[i] * jnp.exp(g_last[i]) + grow[i] for i in ids]
    outs = [_rms(o[i], nw) * _silu(z[b][:, h * GDN_DK:(h + 1) * GDN_DK]) for i, (b, h) in enumerate(items)]
    per_seq = [jnp.concatenate(outs[b * GDN_HEADS:(b + 1) * GDN_HEADS], axis=1) for b in range(nb)]
    return new_states, per_seq, tinv


def _seq_items(bsz, heads):
    return [(b, h) for b in range(bsz) for h in range(heads)]


def _gdn_fwd(qkv_act, p, a_row, dt_row, nw, name):
    bsz, s, _ = qkv_act.shape
    c = GDN_CHUNK
    nc = s // c
    items = _seq_items(bsz, GDN_HEADS)

    def body(qkv_ref, sm_ref, z_ref, a_ref, dt_ref, nw_ref, o_ref, st_ref, ti_ref, st_scr):
        @pl.when(pl.program_id(0) == 0)
        def _():
            st_scr[...] = jnp.zeros_like(st_scr)

        st_ref[...] = st_scr[...]
        seqs = range(bsz)
        new_states, o, tinvs = _gdn_chunk(
            [st_scr[b, h] for b, h in items], [qkv_ref[b] for b in seqs], [sm_ref[b] for b in seqs],
            [z_ref[b] for b in seqs], a_ref[...], dt_ref[...], nw_ref[...])
        for i, (b, h) in enumerate(items):
            st_scr[b, h] = new_states[i]
            ti_ref[b, h] = tinvs[i]
        for b in seqs:
            o_ref[b] = o[b].astype(BF16)

    row = lambda w, blk: pl.BlockSpec((bsz, c, w), lambda n, blk=blk: (0, n, blk))
    prm = pl.BlockSpec((1, LANES), lambda n: (0, 0))
    return _call(
        body, name, (nc,), [row(1536, 0), row(LANES, P_SMALL // LANES), row(512, P_GZ // 512), prm, prm, prm],
        [row(512, 0), pl.BlockSpec((bsz, None, 4, LANES, LANES), lambda n: (0, n, 0, 0, 0)),
         pl.BlockSpec((bsz, None, 4, c, c), lambda n: (0, n, 0, 0, 0))],
        [_sds((bsz, s, 512), BF16), _sds((bsz, nc, 4, LANES, LANES), F32), _sds((bsz, nc, 4, c, c), F32)],
        scratch=[pltpu.VMEM((bsz, 4, LANES, LANES), F32)],
    )(qkv_act, p, p, a_row, dt_row, nw)


def _gdn_bwd(do, qkv_act, p, a_row, dt_row, nw, st_all, ti_all, dp, name):
    bsz, s, _ = qkv_act.shape
    c = GDN_CHUNK
    nc = s // c

    items = _seq_items(bsz, GDN_HEADS)

    def body(qkv_ref, sm_ref, z_ref, a_ref, dt_ref, nw_ref, do_ref, st_ref, ti_ref, _,
             dqkv_ref, dsm_ref, dz_ref, da_ref, ddt_ref, dnw_ref, ds_scr):
        @pl.when(pl.program_id(0) == 0)
        def _():
            ds_scr[...] = jnp.zeros_like(ds_scr)
            da_ref[...] = jnp.zeros_like(da_ref)
            ddt_ref[...] = jnp.zeros_like(ddt_ref)
            dnw_ref[...] = jnp.zeros_like(dnw_ref)

        seqs = range(bsz)
        tinvs = [ti_ref[b, h] for b, h in items]
        chunk = lambda *a: _gdn_chunk(*a, tinvs=tinvs)[:2]
        _, vjp = jax.vjp(chunk, [st_ref[b, h] for b, h in items], [qkv_ref[b] for b in seqs], [sm_ref[b] for b in seqs],
                         [z_ref[b] for b in seqs], a_ref[...], dt_ref[...], nw_ref[...])
        d_states, dqkv, dsm, dz, da, ddt, dnw = vjp(([ds_scr[b, h] for b, h in items], [do_ref[b] for b in seqs]))
        for i, (b, h) in enumerate(items):
            ds_scr[b, h] = d_states[i]
        for b in seqs:
            dqkv_ref[b] = dqkv[b]
            dsm_ref[b] = dsm[b]
            dz_ref[b] = dz[b].astype(BF16)
        da_ref[...] += da
        ddt_ref[...] += ddt
        dnw_ref[...] += dnw

    rrow = lambda w, blk: pl.BlockSpec((bsz, c, w), lambda n, blk=blk: (0, nc - 1 - n, blk))
    prm = pl.BlockSpec((1, LANES), lambda n: (0, 0))
    return _call(
        body, name, (nc,),
        [rrow(1536, 0), rrow(LANES, P_SMALL // LANES), rrow(512, P_GZ // 512), prm, prm, prm, rrow(512, 0),
         pl.BlockSpec((bsz, None, 4, LANES, LANES), lambda n: (0, nc - 1 - n, 0, 0, 0)),
         pl.BlockSpec((bsz, None, 4, c, c), lambda n: (0, nc - 1 - n, 0, 0, 0)), IN_PLACE],
        [rrow(1536, 0), rrow(LANES, 0), rrow(512, P_GZ // 512), prm, prm, prm],
        [_sds((bsz, s, 1536), F32), _sds((bsz, s, LANES), F32), _sds(dp.shape, BF16)] + [_sds((1, LANES), F32)] * 3,
        scratch=[pltpu.VMEM((bsz, 4, LANES, LANES), F32)], aliases={9: 2},
    )(qkv_act, p, p, a_row, dt_row, nw, do, st_all, ti_all, dp)


def _hgrn_block(states, q_raw, f_raw, i_raw, g_raw, lb, nw):
    n = q_raw[0].shape[0]
    c = HGRN_CHUNK
    r, cc = _iota((n, n), 0), _iota((n, n), 1)
    same = (r // c) == (cc // c)
    causal = same & (r >= cc)
    ref_row = (r // c) * c + (c // 2 - 1)
    run_sum = causal.astype(F32)
    rel_sum = run_sum - (same & (ref_row >= cc)).astype(F32)
    sums = jnp.concatenate([run_sum, rel_sum, same.astype(F32)], axis=0).astype(BF16)
    seqs, chunks = range(len(q_raw)), range(n // c)
    items = _seq_items(len(q_raw), HGRN_HEADS)
    hs = lambda t, h: t[:, h * HGRN_DK:(h + 1) * HGRN_DK]
    rows = lambda t, j: t[j * c:(j + 1) * c]
    q = [_silu(q_raw[b]) for b in seqs]
    logf = [jnp.log(lb + (1.0 - lb) * jax.nn.sigmoid(f_raw[b])) for b in seqs]
    k = [(1.0 - lb) * jax.nn.sigmoid(-f_raw[b]) for b in seqs]
    all_sums = [_mask_dot(sums, logf[b]) for b in seqs]
    big_g, g_rel, g_tot = ([t[i * n:(i + 1) * n] for t in all_sums] for i in range(3))
    q_rel = [q[b] * jnp.exp(g_rel[b]) for b in seqs]
    k_rel = [k[b] * jnp.exp(-g_rel[b]) for b in seqs]
    qg = [q[b] * jnp.exp(big_g[b]) for b in seqs]
    k_end = [k[b] * jnp.exp(g_tot[b] - big_g[b]) for b in seqs]
    keep = [[jnp.exp(g_tot[b][j * c:j * c + 1]) for j in chunks] for b in seqs]
    scores = [_bdot_nt(hs(q_rel[b], h), hs(k_rel[b], h)) for b, h in items]
    o_intra = [_bdot(jnp.where(causal, scores[i], 0.0), hs(i_raw[b], h)) for i, (b, h) in enumerate(items)]
    grow = [[_bdot_tn(rows(hs(i_raw[b], h), j), rows(hs(k_end[b], h), j)) for j in chunks] for b, h in items]
    entering, new_states = [], []
    for i, (b, h) in enumerate(items):
        st, per_chunk = states[i], []
        for j in chunks:
            per_chunk.append(st)
            st = st * hs(keep[b][j], h) + grow[i][j]
        entering.append(per_chunk)
        new_states.append(st)
    o_inter = [[_bdot_nt(rows(hs(qg[b], h), j), entering[i][j]) for j in chunks] for i, (b, h) in enumerate(items)]
    outs = [_rms(o_intra[i] + jnp.concatenate(o_inter[i], axis=0), nw) * _silu(hs(g_raw[b], h))
            for i, (b, h) in enumerate(items)]
    return new_states, [jnp.concatenate(outs[b * HGRN_HEADS:(b + 1) * HGRN_HEADS], axis=1) for b in seqs]


def _hgrn_fwd(p, lb, nw, name):
    bsz, s, _ = p.shape
    n = HGRN_BLOCK
    nb = s // n

    items = _seq_items(bsz, HGRN_HEADS)

    def body(q_ref, f_ref, i_ref, g_ref, lb_ref, nw_ref, o_ref, st_ref, st_scr):
        @pl.when(pl.program_id(0) == 0)
        def _():
            st_scr[...] = jnp.zeros_like(st_scr)

        st_ref[...] = st_scr[...]
        per_seq = lambda ref: [ref[b] for b in range(bsz)]
        new_states, o = _hgrn_block([st_scr[b, h] for b, h in items], per_seq(q_ref), per_seq(f_ref), per_seq(i_ref),
                                    per_seq(g_ref), lb_ref[...], nw_ref[...])
        for i, (b, h) in enumerate(items):
            st_scr[b, h] = new_states[i]
        for b in range(bsz):
            o_ref[b] = o[b].astype(BF16)

    row = lambda blk: pl.BlockSpec((bsz, n, 512), lambda i, blk=blk: (0, i, blk))
    return _call(
        body, name, (nb,),
        [row(P_HQ // 512), row(P_HF // 512), row(P_HI // 512), row(P_HG // 512),
         pl.BlockSpec((1, 512), lambda i: (0, 0)), pl.BlockSpec((1, LANES), lambda i: (0, 0))],
        [row(0), pl.BlockSpec((bsz, None, 4, LANES, LANES), lambda i: (0, i, 0, 0, 0))],
        [_sds((bsz, s, 512), BF16), _sds((bsz, nb, 4, LANES, LANES), F32)],
        scratch=[pltpu.VMEM((bsz, 4, LANES, LANES), F32)],
    )(p, p, p, p, lb, nw)


def _hgrn_bwd(do, p, lb, nw, st_all, dp, name):
    bsz, s, _ = p.shape
    n = HGRN_BLOCK
    nb = s // n

    items = _seq_items(bsz, HGRN_HEADS)

    def body(q_ref, f_ref, i_ref, g_ref, lb_ref, nw_ref, do_ref, st_ref, _, dp_ref, dlb_ref, dnw_ref, ds_scr):
        @pl.when(pl.program_id(0) == 0)
        def _():
            ds_scr[...] = jnp.zeros_like(ds_scr)
            dlb_ref[...] = jnp.zeros_like(dlb_ref)
            dnw_ref[...] = jnp.zeros_like(dnw_ref)

        per_seq = lambda ref: [ref[b] for b in range(bsz)]
        _, vjp = jax.vjp(_hgrn_block, [st_ref[b, h] for b, h in items], per_seq(q_ref), per_seq(f_ref), per_seq(i_ref),
                         per_seq(g_ref), lb_ref[...], nw_ref[...])
        d_states, dq, df, di, dg, dlb, dnw = vjp(([ds_scr[b, h] for b, h in items], per_seq(do_ref)))
        for i, (b, h) in enumerate(items):
            ds_scr[b, h] = d_states[i]
        for b in range(bsz):
            for j, t in enumerate((dq, df, di, dg)):
                dp_ref[b, :, j * 512:(j + 1) * 512] = t[b].astype(BF16)
        dlb_ref[...] += dlb
        dnw_ref[...] += dnw

    row = lambda blk: pl.BlockSpec((bsz, n, 512), lambda i, blk=blk: (0, nb - 1 - i, blk))
    return _call(
        body, name, (nb,),
        [row(P_HQ // 512), row(P_HF // 512), row(P_HI // 512), row(P_HG // 512),
         pl.BlockSpec((1, 512), lambda i: (0, 0)), pl.BlockSpec((1, LANES), lambda i: (0, 0)), row(0),
         pl.BlockSpec((bsz, None, 4, LANES, LANES), lambda i: (0, nb - 1 - i, 0, 0, 0)), IN_PLACE],
        [pl.BlockSpec((bsz, n, 2048), lambda i: (0, nb - 1 - i, P_HQ // 2048)),
         pl.BlockSpec((1, 512), lambda i: (0, 0)), pl.BlockSpec((1, LANES), lambda i: (0, 0))],
        [_sds(dp.shape, BF16), _sds((1, 512), F32), _sds((1, LANES), F32)],
        scratch=[pltpu.VMEM((bsz, 4, LANES, LANES), F32)], aliases={8: 0},
    )(p, p, p, p, lb, nw, do, st_all, dp)


def _ssd_chunk(states, xbc, small, z, a_row, dt_row, d_row, nw):
    seqs = range(len(xbc))
    c = xbc[0].shape[0]
    incl = _tril(c)
    spread = (_iota((LANES, SSD_INNER), 0) == SM_DT + _iota((LANES, SSD_INNER), 1) // SSD_HEAD_DIM).astype(BF16)
    dt_all = [_softplus(small[b] + dt_row) for b in seqs]
    both = [_spread_dot(jnp.concatenate([dt_all[b], dt_all[b] * (-jnp.exp(a_row))], axis=0), spread) for b in seqs]
    dt_e, da_e = [t[:c] for t in both], [t[c:] for t in both]
    acs_e = [_mask_dot(incl.astype(BF16), da_e[b]) for b in seqs]
    last_e = [jnp.sum(da_e[b], axis=0, keepdims=True) for b in seqs]
    xs = [xbc[b][:, :SSD_INNER] for b in seqs]
    xdt = [xs[b] * dt_e[b] for b in seqs]
    gw = SSD_GROUPS * SSD_STATE
    lane = _iota((1, LANES), 1)
    items = _seq_items(len(xbc), 4)
    grp = [(b, g) for b in seqs for g in range(SSD_GROUPS)]
    ps = lambda t, j: t[:, j * LANES:(j + 1) * LANES]
    bg = {(b, g): xbc[b][:, SSD_INNER + g * SSD_STATE:SSD_INNER + (g + 1) * SSD_STATE] for b, g in grp}
    cg = {(b, g): xbc[b][:, SSD_INNER + gw + g * SSD_STATE:SSD_INNER + gw + (g + 1) * SSD_STATE] for b, g in grp}
    cb = {bgk: _bdot_nt(cg[bgk], bg[bgk]) for bgk in grp}

    def seg(b, j, sub):
        ac = ps(acs_e[b], j)[:, sub * SSD_HEAD_DIM:sub * SSD_HEAD_DIM + 1]
        return jnp.where(incl, jnp.exp(jnp.where(incl, ac - _col_to_row(ac), 0.0)), 0.0)

    mine = [((lane // SSD_HEAD_DIM) == sub).astype(F32) for sub in range(2)]
    y_in = [[_bdot(cb[b, j // 2] * seg(b, j, sub), ps(xdt[b], j) * mine[sub]) for sub in range(2)] for b, j in items]
    y_st = [_bdot(cg[b, j // 2], states[i]) for i, (b, j) in enumerate(items)]
    grow = [_bdot_tn(bg[b, j // 2], ps(xdt[b], j) * jnp.exp(ps(last_e[b], j) - ps(acs_e[b], j))) for b, j in items]
    new_states = [states[i] * jnp.exp(ps(last_e[b], j)) + grow[i] for i, (b, j) in enumerate(items)]
    ys = [y_in[i][0] + y_in[i][1] + y_st[i] * jnp.exp(ps(acs_e[b], j)) + ps(d_row, j) * ps(xs[b], j)
          for i, (b, j) in enumerate(items)]
    gwid = SSD_INNER // SSD_GROUPS
    outs = []
    for b in seqs:
        yz = jnp.concatenate(ys[4 * b:4 * b + 4], axis=1) * _silu(z[b])
        outs.append(jnp.concatenate(
            [_rms(yz[:, g * gwid:(g + 1) * gwid], nw[:, g * gwid:(g + 1) * gwid]) for g in range(SSD_GROUPS)], axis=1))
    return new_states, outs


def _ssd_fwd(xbc_act, p, a_row, dt_row, d_row, nw, name):
    bsz, s, _ = xbc_act.shape
    c = SSD_CHUNK
    nc = s // c

    items = _seq_items(bsz, 4)

    def body(x_ref, sm_ref, z_ref, a_ref, dt_ref, d_ref, nw_ref, o_ref, st_ref, st_scr):
        @pl.when(pl.program_id(0) == 0)
        def _():
            st_scr[...] = jnp.zeros_like(st_scr)

        st_ref[...] = st_scr[...]
        per_seq = lambda ref: [ref[b] for b in range(bsz)]
        new_states, o = _ssd_chunk([st_scr[b, j] for b, j in items], per_seq(x_ref), per_seq(sm_ref), per_seq(z_ref),
                                   a_ref[...], dt_ref[...], d_ref[...], nw_ref[...])
        for i, (b, j) in enumerate(items):
            st_scr[b, j] = new_states[i]
        for b in range(bsz):
            o_ref[b] = o[b].astype(BF16)

    row = lambda w, blk: pl.BlockSpec((bsz, c, w), lambda n, blk=blk: (0, n, blk))
    prm = pl.BlockSpec((1, LANES), lambda n: (0, 0))
    prm5 = pl.BlockSpec((1, 512), lambda n: (0, 0))
    return _call(
        body, name, (nc,),
        [row(1024, 0), row(LANES, P_SMALL // LANES), row(512, P_SZ // 512), prm, prm, prm5, prm5],
        [row(512, 0), pl.BlockSpec((bsz, None, 4, LANES, LANES), lambda n: (0, n, 0, 0, 0))],
        [_sds((bsz, s, 512), BF16), _sds((bsz, nc, 4, LANES, LANES), F32)],
        scratch=[pltpu.VMEM((bsz, 4, LANES, LANES), F32)],
    )(xbc_act, p, p, a_row, dt_row, d_row, nw)


def _ssd_bwd(do, xbc_act, p, a_row, dt_row, d_row, nw, st_all, dp, name):
    bsz, s, _ = xbc_act.shape
    c = SSD_CHUNK
    nc = s // c

    items = _seq_items(bsz, 4)

    def body(x_ref, sm_ref, z_ref, a_ref, dt_ref, d_ref, nw_ref, do_ref, st_ref, _,
             dx_ref, dsm_ref, dz_ref, da_ref, ddt_ref, dd_ref, dnw_ref, ds_scr):
        @pl.when(pl.program_id(0) == 0)
        def _():
            ds_scr[...] = jnp.zeros_like(ds_scr)
            da_ref[...] = jnp.zeros_like(da_ref)
            ddt_ref[...] = jnp.zeros_like(ddt_ref)
            dd_ref[...] = jnp.zeros_like(dd_ref)
            dnw_ref[...] = jnp.zeros_like(dnw_ref)

        per_seq = lambda ref: [ref[b] for b in range(bsz)]
        _, vjp = jax.vjp(_ssd_chunk, [st_ref[b, j] for b, j in items], per_seq(x_ref), per_seq(sm_ref), per_seq(z_ref),
                         a_ref[...], dt_ref[...], d_ref[...], nw_ref[...])
        d_states, dx, dsm, dz, da, ddt, dd, dnw = vjp(([ds_scr[b, j] for b, j in items], per_seq(do_ref)))
        for i, (b, j) in enumerate(items):
            ds_scr[b, j] = d_states[i]
        for b in range(bsz):
            dx_ref[b] = dx[b]
            dsm_ref[b] = dsm[b]
            dz_ref[b] = dz[b].astype(BF16)
        da_ref[...] += da
        ddt_ref[...] += ddt
        dd_ref[...] += dd
        dnw_ref[...] += dnw

    row = lambda w, blk: pl.BlockSpec((bsz, c, w), lambda n, blk=blk: (0, nc - 1 - n, blk))
    prm = pl.BlockSpec((1, LANES), lambda n: (0, 0))
    prm5 = pl.BlockSpec((1, 512), lambda n: (0, 0))
    return _call(
        body, name, (nc,),
        [row(1024, 0), row(LANES, P_SMALL // LANES), row(512, P_SZ // 512), prm, prm, prm5, prm5, row(512, 0),
         pl.BlockSpec((bsz, None, 4, LANES, LANES), lambda n: (0, nc - 1 - n, 0, 0, 0)), IN_PLACE],
        [row(1024, 0), row(LANES, 0), row(512, P_SZ // 512), prm, prm, prm5, prm5],
        [_sds((bsz, s, 1024), F32), _sds((bsz, s, LANES), F32), _sds(dp.shape, BF16),
         _sds((1, LANES), F32), _sds((1, LANES), F32), _sds((1, 512), F32), _sds((1, 512), F32)],
        scratch=[pltpu.VMEM((bsz, 4, LANES, LANES), F32)], aliases={9: 2},
    )(xbc_act, p, p, a_row, dt_row, d_row, nw, do, st_all, dp)


def _small_cols(dsm_a, dsm_c, dp, name):
    bsz, s, _ = dsm_a.shape
    ts = _row_tile(s, 1024)
    width = P_WIDTH - P_SMALL

    def body(a_ref, c_ref, _, o_ref):
        o_ref[:, :LANES] = (a_ref[...] + c_ref[...]).astype(BF16)
        o_ref[:, LANES:] = jnp.zeros((ts, width - LANES), BF16)

    row = pl.BlockSpec((None, ts, LANES), lambda b, i: (b, i, 0))
    return _call(body, name, (bsz, s // ts), [row, row, IN_PLACE],
                 pl.BlockSpec((None, ts, width), lambda b, i: (b, i, P_SMALL // width)), _sds(dp.shape, BF16),
                 aliases={2: 0})(dsm_a, dsm_c, dp)


def _peer(k):
    x, y, c = lax.axis_index("x"), lax.axis_index("y"), lax.axis_index("c")
    px = 1 - x if k & 4 else x
    py = 1 - y if k & 2 else y
    pc = 1 - c if k & 1 else c
    return (px, py, pc), 4 * px + 2 * py + pc


def _my_index():
    return 4 * lax.axis_index("x") + 2 * lax.axis_index("y") + lax.axis_index("c")


def _mesh_place():
    x, y, c = lax.axis_index("x"), lax.axis_index("y"), lax.axis_index("c")
    return (x, y, c), (x, y, 1 - c), [(1 - x, y), (x, 1 - y), (1 - x, 1 - y)]


def _run_exchange(body, name, arrays, out_shape, n_sems):
    n = len(arrays)
    any_spec = pl.BlockSpec(memory_space=pl.ANY)
    return pl.pallas_call(
        body, name=name, out_shape=out_shape, in_specs=[any_spec] * n, out_specs=[any_spec] * n,
        scratch_shapes=[pltpu.SemaphoreType.DMA((n_sems, n)), pltpu.SemaphoreType.DMA((n_sems, n)),
                        pltpu.SemaphoreType.DMA((n,))],
    )(*arrays)


def _all_to_all(arrays, name):
    n = len(arrays)

    def body(*refs):
        ins, outs = refs[:n], refs[n:2 * n]
        send_sems, recv_sems, local_sems = refs[2 * n:]
        me = _my_index()

        def copy(i, k, arriving):
            peer, slot = _peer(k)
            return pltpu.make_async_remote_copy(
                src_ref=ins[i].at[slot], dst_ref=outs[i].at[slot if arriving else me], send_sem=send_sems.at[k - 1, i],
                recv_sem=recv_sems.at[k - 1, i], device_id=peer, device_id_type=MESH_ID)

        mine = [pltpu.make_async_copy(ins[i].at[me], outs[i].at[me], local_sems.at[i]) for i in range(n)]
        sends = [copy(i, k, False) for k in range(1, N_DEV) for i in range(n)]
        for cp in mine + sends:
            cp.start()
        for k in range(1, N_DEV):
            for i in range(n):
                copy(i, k, True).wait_recv()
        for cp in sends:
            cp.wait_send()
        for cp in mine:
            cp.wait()

    return _run_exchange(body, name, arrays, [_sds(a.shape, a.dtype) for a in arrays], N_DEV - 1)


def _gather_two_level(arrays, name):
    n = len(arrays)

    def body(*refs):
        ins, outs = refs[:n], refs[n:2 * n]
        send_sems, recv_sems, local_sems = refs[2 * n:]
        (x, y, c), sibling, chips = _mesh_place()
        slot = lambda px, py, pc: 4 * px + 2 * py + pc

        def copy(i, k, block, to, src=None):
            return pltpu.make_async_remote_copy(
                src_ref=outs[i].at[block] if src is None else src, dst_ref=outs[i].at[block],
                send_sem=send_sems.at[k, i], recv_sem=recv_sems.at[k, i], device_id=to, device_id_type=MESH_ID)

        me = slot(x, y, c)
        mine = [pltpu.make_async_copy(ins[i], outs[i].at[me], local_sems.at[i]) for i in range(n)]
        first = [copy(i, 1 + j, me, (*chips[j], c), src=ins[i]) for j in (2, 1, 0) for i in range(n)]
        first += [copy(i, 0, me, sibling, src=ins[i]) for i in range(n)]
        for cp in first + mine:
            cp.start()
        passed = []
        for j, chip in enumerate(chips):
            for i in range(n):
                copy(i, 1 + j, slot(*chip, c), (x, y, c)).wait_recv()
                cp = copy(i, 4 + j, slot(*chip, c), sibling)
                cp.start()
                passed.append(cp)
        for i in range(n):
            copy(i, 0, slot(x, y, 1 - c), (x, y, c)).wait_recv()
        for j, chip in enumerate(chips):
            for i in range(n):
                copy(i, 4 + j, slot(*chip, 1 - c), (x, y, c)).wait_recv()
        for cp in first + passed:
            cp.wait_send()
        for cp in mine:
            cp.wait()

    out_shape = [_sds((N_DEV,) + a.shape, a.dtype) for a in arrays]
    return _run_exchange(body, name, arrays, out_shape, 7)


def _sibling_swap(arrays, name):
    n = len(arrays)

    def body(*refs):
        ins, outs = refs[:n], refs[n:2 * n]
        send_sems, recv_sems, _ = refs[2 * n:]
        (x, y, c), sibling, _ = _mesh_place()
        copies = [pltpu.make_async_remote_copy(
            src_ref=ins[i].at[1 - c], dst_ref=outs[i], send_sem=send_sems.at[0, i], recv_sem=recv_sems.at[0, i],
            device_id=sibling, device_id_type=MESH_ID) for i in range(n)]
        for cp in copies:
            cp.start()
        for cp in copies:
            cp.wait()

    out_shape = [_sds(a.shape[1:], a.dtype) for a in arrays]
    return _run_exchange(body, name, arrays, out_shape, 1)


def _chip_scatter(arrays, name):
    n = len(arrays)

    def body(*refs):
        ins, outs = refs[:n], refs[n:2 * n]
        send_sems, recv_sems, local_sems = refs[2 * n:]
        (x, y, c), _, chips = _mesh_place()
        me = 2 * x + y
        mine = [pltpu.make_async_copy(ins[i].at[me], outs[i].at[me], local_sems.at[i]) for i in range(n)]
        sends = [pltpu.make_async_remote_copy(
            src_ref=ins[i].at[2 * chip[0] + chip[1]], dst_ref=outs[i].at[me], send_sem=send_sems.at[j, i],
            recv_sem=recv_sems.at[j, i], device_id=(*chip, c), device_id_type=MESH_ID)
            for j, chip in enumerate(chips) for i in range(n)]
        for cp in mine + sends:
            cp.start()
        for j, chip in enumerate(chips):
            for i in range(n):
                pltpu.make_async_remote_copy(
                    src_ref=ins[i].at[me], dst_ref=outs[i].at[2 * chip[0] + chip[1]], send_sem=send_sems.at[j, i],
                    recv_sem=recv_sems.at[j, i], device_id=(*chip, c), device_id_type=MESH_ID).wait_recv()
        for cp in sends:
            cp.wait_send()
        for cp in mine:
            cp.wait()

    out_shape = [_sds(a.shape, a.dtype) for a in arrays]
    return _run_exchange(body, name, arrays, out_shape, 3)


def _pair_sum(a, b, name):
    lead, rows, width = a.shape
    tr = _pick(rows, (256, 128, 64, 32, 16, 8)) if rows % 8 == 0 else rows

    def body(a_ref, b_ref, o_ref):
        o_ref[...] = (a_ref[...].astype(F32) + b_ref[...].astype(F32)).astype(o_ref.dtype)

    blk = pl.BlockSpec((None, tr, width), lambda l, i: (l, i, 0))
    return _call(body, name, (lead, rows // tr), [blk, blk], blk, _sds(a.shape, a.dtype))(a, b)


def _sum_adamw(gs, w, m, v, name):
    lead, rows, width = w.shape
    slots = gs.shape[0]
    tr = _pick(rows, (128, 64, 32, 16, 8)) if rows % 8 == 0 else rows

    def body(g_ref, w_ref, m_ref, v_ref, go_ref, d_ref, mo_ref, vo_ref):
        g = g_ref[0].astype(F32)
        for i in range(1, slots):
            g = g + g_ref[i].astype(F32)
        m2 = ADAM_B1 * m_ref[...] + (1.0 - ADAM_B1) * g
        v2 = ADAM_B2 * v_ref[...] + (1.0 - ADAM_B2) * (g * g)
        m_hat = m2 / (1.0 - ADAM_B1 ** ADAM_STEP)
        v_hat = v2 / (1.0 - ADAM_B2 ** ADAM_STEP)
        go_ref[...] = g
        d_ref[...] = -ADAM_LR * (m_hat / (jnp.sqrt(v_hat) + ADAM_EPS) + ADAM_WD * w_ref[...])
        mo_ref[...] = m2
        vo_ref[...] = v2

    blk = pl.BlockSpec((None, tr, width), lambda l, i: (l, i, 0))
    return _call(body, name, (lead, rows // tr),
                 [pl.BlockSpec((slots, None, tr, width), lambda l, i: (0, l, i, 0)), blk, blk, blk],
                 [blk] * 4, [_sds(w.shape, F32)] * 4)(gs, w, m, v)


MATMUL_WEIGHTS = ("w_in", "w_br_a", "w_br_b", "w_br_c", "w_out", "ffn_w_up", "ffn_w_down")
UNALIGNED = ("w_in", "ffn_w_up")
SPLIT = (
    ("w_in", (DEPTH, D_MODEL, 8720), 2),
    ("gdn_conv_w", (DEPTH, 4, 1536), 2), ("ssd_conv_w", (DEPTH, 4, 1024), 2),
    ("w_br_a", (DEPTH, 512, D_MODEL), 2), ("w_br_b", (DEPTH, 512, D_MODEL), 2), ("w_br_c", (DEPTH, 512, D_MODEL), 2),
    ("w_out", (DEPTH, D_MODEL, D_MODEL), 1), ("ffn_w_up", (DEPTH, D_MODEL, 2 * FFN_HIDDEN), 2),
    ("ffn_conv_w", (DEPTH, 3, 2 * FFN_HIDDEN), 2), ("ffn_w_down", (DEPTH, FFN_HIDDEN, D_MODEL), 1),
)
REPL = (
    ("b_ada", (DEPTH, 6 * D_MODEL)), ("norm1_w", (DEPTH, D_MODEL)), ("gdn_a_log", (DEPTH, 4)),
    ("gdn_dt_bias", (DEPTH, 4)), ("gdn_norm_w", (DEPTH, 128)), ("hgrn_lb_param", (DEPTH, 512)),
    ("hgrn_norm_w", (DEPTH, 128)), ("ssd_conv_b", (DEPTH, 1024)), ("ssd_a_log", (DEPTH, 8)),
    ("ssd_dt_bias", (DEPTH, 8)), ("ssd_d", (DEPTH, 8)), ("ssd_norm_w", (DEPTH, 512)), ("norm2_w", (DEPTH, D_MODEL)),
    ("ffn_conv_b", (DEPTH, 2 * FFN_HIDDEN)), ("final_norm_w", (D_MODEL,)),
)
WEIGHTS = ("w_ada", "b_ada", "norm1_w", "w_in", "gdn_conv_w", "gdn_a_log", "gdn_dt_bias", "gdn_norm_w",
           "hgrn_lb_param", "hgrn_norm_w", "ssd_conv_w", "ssd_conv_b", "ssd_a_log", "ssd_dt_bias", "ssd_d",
           "ssd_norm_w", "w_br_a", "w_br_b", "w_br_c", "w_out", "norm2_w", "ffn_w_up", "ffn_conv_w", "ffn_conv_b",
           "ffn_w_down", "final_norm_w")


def _block_shape(shape, axis):
    return tuple(d // N_DEV if i == axis else d for i, d in enumerate(shape))


def _join_blocks(gathered, shape, axis):
    return jnp.moveaxis(gathered, 0, axis).reshape(shape)


def _split_blocks(full, shape, axis):
    bs = _block_shape(shape, axis)
    t = full.reshape(shape[:axis] + (N_DEV, bs[axis]) + shape[axis + 1:])
    return jnp.moveaxis(t, axis, 0)


def _pack_repl(vals):
    parts = []
    for n, shape in REPL:
        size = math.prod(shape)
        parts.append(jnp.pad(vals[n].reshape(-1), (0, -(-size // PACK_W) * PACK_W - size)))
    cat = jnp.concatenate(parts)
    rows = -(-cat.shape[0] // (8 * PACK_W)) * 8
    return jnp.pad(cat, (0, rows * PACK_W - cat.shape[0])).reshape(rows, PACK_W)


def _unpack_repl(packed):
    flat, out, off = packed.reshape(-1), {}, 0
    for n, shape in REPL:
        size = math.prod(shape)
        out[n] = flat[off:off + size].reshape(shape)
        off += -(-size // PACK_W) * PACK_W
    return out


def _lane_row(vec, lane0):
    return jnp.pad(vec, (lane0, LANES - lane0 - vec.shape[0]))[None]


def _arrange_w_in(w):
    offs = [0]
    for sz in W_IN_SPLITS:
        offs.append(offs[-1] + sz)
    qkv, a, b, gz, hq, hf, hi, hg, sz_, xbc, dt, gate = [w[:, offs[i]:offs[i + 1]] for i in range(12)]
    pad = jnp.zeros((w.shape[0], P_WIDTH - P_SMALL - 16), w.dtype)
    return jnp.concatenate([qkv, gz, xbc, gate, hq, hf, hi, hg, sz_, a, b, dt, pad], axis=1)


def _restore_w_in(wp):
    cut = lambda o, n: wp[:, o:o + n]
    return jnp.concatenate([
        cut(P_QKV, 1536), cut(P_SMALL + SM_A, 4), cut(P_SMALL + SM_B, 4), cut(P_GZ, 512), cut(P_HQ, 512),
        cut(P_HF, 512), cut(P_HI, 512), cut(P_HG, 512), cut(P_SZ, 512), cut(P_XBC, 1024), cut(P_SMALL + SM_DT, 8),
        cut(P_GATE, 3072)], axis=1)


def _join_cols(gathered, arrange, name):
    _, depth, rows, cols = gathered.shape
    tr = _pick(rows, (256, 128, 64, 32, 16, 8))
    width = P_WIDTH if arrange else N_DEV * cols

    def body(g_ref, o_ref):
        row = jnp.concatenate([g_ref[d] for d in range(N_DEV)], axis=1)
        o_ref[...] = _arrange_w_in(row) if arrange else row

    return _call(body, name, (depth, rows // tr),
                 [pl.BlockSpec((N_DEV, None, tr, cols), lambda l, i: (0, l, i, 0))],
                 pl.BlockSpec((None, tr, width), lambda l, i: (l, i, 0)), _sds((depth, rows, width), gathered.dtype),
                 )(gathered)


def _split_cols(per_layer, restore, cols, name):
    depth = len(per_layer)
    rows = per_layer[0].shape[0]
    tr = _pick(rows, (256, 128, 64, 32, 16, 8))
    nt = rows // tr

    def body(*refs):
        o_ref = refs[depth]
        for l in range(depth):
            @pl.when(pl.program_id(0) == l)
            def _(l=l):
                row = _restore_w_in(refs[l][...]) if restore else refs[l][...]
                for d in range(N_DEV):
                    o_ref[d % 2, d // 2] = row[:, d * cols:(d + 1) * cols]

    return _call(body, name, (depth, nt),
                 [pl.BlockSpec((tr, a.shape[1]), lambda l, i: (i, 0)) for a in per_layer],
                 pl.BlockSpec((2, N_DEV // 2, tr, cols), lambda l, i: (0, 0, l * nt + i, 0)),
                 _sds((2, N_DEV // 2, depth * rows, cols), per_layer[0].dtype))(*per_layer)


def _layer_consts(l, wf, wr, lower):
    t = lambda a: a.T
    k = {}
    k["n1w"], k["n2w"] = wr["norm1_w"][l][None], wr["norm2_w"][l][None]
    k["win"], k["win_t"] = wf["w_in"][l], t(wf["w_in"][l])
    for n in ("w_br_a", "w_br_b", "w_br_c", "w_out", "ffn_w_up", "ffn_w_down"):
        k[n], k[n + "_t"] = wf[n][l], t(wf[n][l])
    k["gdn_conv_w"], k["gdn_conv_b"] = wf["gdn_conv_w"][l], jnp.zeros((1, 1536), F32)
    k["ssd_conv_w"], k["ssd_conv_b"] = wf["ssd_conv_w"][l], wr["ssd_conv_b"][l][None]
    k["ffn_conv_w"], k["ffn_conv_b"] = wf["ffn_conv_w"][l], wr["ffn_conv_b"][l][None]
    k["gdn_a"], k["gdn_dt"] = _lane_row(wr["gdn_a_log"][l], SM_A), _lane_row(wr["gdn_dt_bias"][l], SM_A)
    k["gdn_nw"], k["hgrn_nw"] = wr["gdn_norm_w"][l][None], wr["hgrn_norm_w"][l][None]
    k["ssd_a"], k["ssd_dt"] = _lane_row(wr["ssd_a_log"][l], SM_DT), _lane_row(wr["ssd_dt_bias"][l], SM_DT)
    k["ssd_d"] = jnp.repeat(wr["ssd_d"][l], SSD_HEAD_DIM)[None]
    k["ssd_nw"] = wr["ssd_norm_w"][l][None]
    k["lb"] = lower[l:l + 1]
    return k


def _layer_fwd(l, x, mod, k):
    bsz, s, d = x.shape
    t = bsz * s
    sv = {"x": x}
    sv["mod"] = [mod[:, None, i * d:(i + 1) * d] for i in range(6)]
    sh1, sc1, g1, sh2, sc2, g2 = sv["mod"]
    h1 = _norm_mod_fwd(x, k["n1w"], sh1, sc1, f"norm1_fwd{l}")
    p = _mm(h1.reshape(t, d), k["win"], F32, f"mm_in{l}").reshape(bsz, s, P_WIDTH)
    qkv_act = _conv_fwd(p, P_QKV, 1536, k["gdn_conv_w"], k["gdn_conv_b"], f"gdn_conv_fwd{l}")
    oa, st_a, ti_a = _gdn_fwd(qkv_act, p, k["gdn_a"], k["gdn_dt"], k["gdn_nw"], f"gdn_fwd{l}")
    ob, st_b = _hgrn_fwd(p, k["lb"], k["hgrn_nw"], f"hgrn_fwd{l}")
    xbc_act = _conv_fwd(p, P_XBC, 1024, k["ssd_conv_w"], k["ssd_conv_b"], f"ssd_conv_fwd{l}")
    oc, st_c = _ssd_fwd(xbc_act, p, k["ssd_a"], k["ssd_dt"], k["ssd_d"], k["ssd_nw"], f"ssd_fwd{l}")
    merged = _merge_fwd(p, oa, ob, oc, k["w_br_a"], k["w_br_b"], k["w_br_c"], f"merge_fwd{l}")
    mix, x1 = _mm_resid(merged.reshape(t, d), k["w_out"], x, g1, f"mm_out{l}")
    h2 = _norm_mod_fwd(x1, k["n2w"], sh2, sc2, f"norm2_fwd{l}")
    u_pre = _mm(h2.reshape(t, d), k["ffn_w_up"], F32, f"mm_up{l}").reshape(bsz, s, 2 * FFN_HIDDEN)
    a = _conv_glu_fwd(u_pre, k["ffn_conv_w"], k["ffn_conv_b"], f"ffn_conv_glu_fwd{l}")
    ffn, x2 = _mm_resid(a.reshape(t, FFN_HIDDEN), k["ffn_w_down"], x1, g2, f"mm_down{l}")
    sv.update(h1=h1, p=p, qkv_act=qkv_act, oa=oa, st_a=st_a, ti_a=ti_a, ob=ob, st_b=st_b, xbc_act=xbc_act, oc=oc, st_c=st_c,
              merged=merged, mix=mix, x1=x1, h2=h2, u_pre=u_pre, a=a, ffn=ffn)
    return x2, sv


def _layer_bwd(l, dx2, k, sv):
    bsz, s, d = dx2.shape
    t = bsz * s
    f2 = 2 * FFN_HIDDEN
    sh1, sc1, g1, sh2, sc2, g2 = sv["mod"]
    tr = lambda a: a.reshape(t, -1).T
    g = {}
    dffn, dg2 = _gate_bwd(dx2, sv["ffn"], g2, f"gate2_bwd{l}")
    dffn2 = dffn.reshape(t, d)
    da = _mm(dffn2, k["ffn_w_down_t"], F32, f"mm_down_dx{l}").reshape(bsz, s, FFN_HIDDEN)
    g["ffn_w_down"] = _mm(tr(sv["a"]), dffn2, BF16, f"mm_down_dw{l}")
    du_pre, g["ffn_conv_w"], dfcb = _conv_glu_bwd(da, sv["u_pre"], k["ffn_conv_w"], k["ffn_conv_b"], f"ffn_conv_glu_bwd{l}")
    g["ffn_conv_b"] = dfcb[0]
    du2 = du_pre.reshape(t, f2)
    dh2 = _mm(du2, k["ffn_w_up_t"], F32, f"mm_up_dx{l}").reshape(bsz, s, d)
    g["ffn_w_up"] = _mm(tr(sv["h2"]), du2, BF16, f"mm_up_dw{l}")
    dx1, dn2w, dsh2, dsc2 = _norm_mod_bwd(sv["x1"], k["n2w"], sh2, sc2, dh2, dx2, f"norm2_bwd{l}")
    g["norm2_w"] = dn2w[0]
    dmix, dg1 = _gate_bwd(dx1, sv["mix"], g1, f"gate1_bwd{l}")
    dmix2 = dmix.reshape(t, d)
    dmerged = _mm(dmix2, k["w_out_t"], F32, f"mm_out_dx{l}").reshape(bsz, s, d)
    g["w_out"] = _mm(tr(sv["merged"]), dmix2, BF16, f"mm_out_dw{l}")
    p = sv["p"]
    dp, doa, dob, doc, dya, dyb, dyc = _merge_bwd(
        dmerged, p, sv["oa"], sv["ob"], sv["oc"], k["w_br_a"], k["w_br_b"], k["w_br_c"],
        k["w_br_a_t"], k["w_br_b_t"], k["w_br_c_t"], f"merge_bwd{l}")
    g["w_br_a"] = _mm(tr(sv["oa"]), dya.reshape(t, d), BF16, f"mm_bra_dw{l}")
    g["w_br_b"] = _mm(tr(sv["ob"]), dyb.reshape(t, d), BF16, f"mm_brb_dw{l}")
    g["w_br_c"] = _mm(tr(sv["oc"]), dyc.reshape(t, d), BF16, f"mm_brc_dw{l}")
    dxbc_act, dsm_c, dp, da_c, ddt_c, dd_c, dnw_c = _ssd_bwd(
        doc, sv["xbc_act"], p, k["ssd_a"], k["ssd_dt"], k["ssd_d"], k["ssd_nw"], sv["st_c"], dp, f"ssd_bwd{l}")
    dp, g["ssd_conv_w"], dscb = _conv_bwd(dxbc_act, p, P_XBC, 1024, k["ssd_conv_w"], k["ssd_conv_b"], dp, f"ssd_conv_bwd{l}")
    g["ssd_conv_b"] = dscb[0]
    g["ssd_a_log"], g["ssd_dt_bias"] = da_c[0, SM_DT:SM_DT + 8], ddt_c[0, SM_DT:SM_DT + 8]
    g["ssd_d"] = dd_c.reshape(SSD_HEADS, SSD_HEAD_DIM).sum(axis=1)
    g["ssd_norm_w"] = dnw_c[0]
    dp, dlb, dnw_b = _hgrn_bwd(dob, p, k["lb"], k["hgrn_nw"], sv["st_b"], dp, f"hgrn_bwd{l}")
    g["hgrn_norm_w"] = dnw_b[0]
    dqkv_act, dsm_a, dp, da_a, ddt_a, dnw_a = _gdn_bwd(
        doa, sv["qkv_act"], p, k["gdn_a"], k["gdn_dt"], k["gdn_nw"], sv["st_a"], sv["ti_a"], dp, f"gdn_bwd{l}")
    dp, g["gdn_conv_w"], _ = _conv_bwd(dqkv_act, p, P_QKV, 1536, k["gdn_conv_w"], k["gdn_conv_b"], dp, f"gdn_conv_bwd{l}")
    g["gdn_a_log"], g["gdn_dt_bias"], g["gdn_norm_w"] = da_a[0, :4], ddt_a[0, :4], dnw_a[0]
    dp = _small_cols(dsm_a, dsm_c, dp, f"small_cols{l}").reshape(t, P_WIDTH)
    dh1 = _mm(dp, k["win_t"], F32, f"mm_in_dx{l}").reshape(bsz, s, d)
    g["w_in"] = _mm(tr(sv["h1"]), dp, BF16, f"mm_in_dw{l}")
    dx, dn1w, dsh1, dsc1 = _norm_mod_bwd(sv["x"], k["n1w"], sh1, sc1, dh1, dx1, f"norm1_bwd{l}")
    g["norm1_w"] = dn1w[0]
    dmod = jnp.concatenate([dsh1, dsc1, dg1, dsh2, dsc2, dg2], axis=-1)[:, 0]
    return dx, g, dlb, dmod


def _local_step(x, mod, wf, wr, target):
    lower = _lb_fwd(wr["hgrn_lb_param"])
    ks = [_layer_consts(l, wf, wr, lower) for l in range(DEPTH)]
    saved = []
    h = x
    for l in range(DEPTH):
        h, sv = _layer_fwd(l, h, mod[l], ks[l])
        saved.append(sv)
    loss8, dh, dfnw = _final_loss(h, wr["final_norm_w"][None], target)
    per_layer, dlbs, dmods = [None] * DEPTH, [None] * DEPTH, [None] * DEPTH
    for l in reversed(range(DEPTH)):
        dh, per_layer[l], dlbs[l], dmods[l] = _layer_bwd(l, dh, ks[l], saved[l])
    grads = {n: [per_layer[l][n] for l in range(DEPTH)] for n in per_layer[0]}
    grads = {n: g if n in UNALIGNED else jnp.stack(g) for n, g in grads.items()}
    grads["hgrn_lb_param"] = _lb_bwd(wr["hgrn_lb_param"], jnp.concatenate(dlbs, axis=0))
    grads["final_norm_w"] = dfnw[0]
    return loss8[0, 0], dh, grads, jnp.stack(dmods)


def kernel(x, c, w_ada, b_ada, norm1_w, w_in, gdn_conv_w, gdn_a_log, gdn_dt_bias, gdn_norm_w, hgrn_lb_param, hgrn_norm_w, ssd_conv_w, ssd_conv_b, ssd_a_log, ssd_dt_bias, ssd_d, ssd_norm_w, w_br_a, w_br_b, w_br_c, w_out, norm2_w, ffn_w_up, ffn_conv_w, ffn_conv_b, ffn_w_down, final_norm_w, loss_target, m_w_ada, m_b_ada, m_norm1_w, m_w_in, m_gdn_conv_w, m_gdn_a_log, m_gdn_dt_bias, m_gdn_norm_w, m_hgrn_lb_param, m_hgrn_norm_w, m_ssd_conv_w, m_ssd_conv_b, m_ssd_a_log, m_ssd_dt_bias, m_ssd_d, m_ssd_norm_w, m_w_br_a, m_w_br_b, m_w_br_c, m_w_out, m_norm2_w, m_ffn_w_up, m_ffn_conv_w, m_ffn_conv_b, m_ffn_w_down, m_final_norm_w, v_w_ada, v_b_ada, v_norm1_w, v_w_in, v_gdn_conv_w, v_gdn_a_log, v_gdn_dt_bias, v_gdn_norm_w, v_hgrn_lb_param, v_hgrn_norm_w, v_ssd_conv_w, v_ssd_conv_b, v_ssd_a_log, v_ssd_dt_bias, v_ssd_d, v_ssd_norm_w, v_w_br_a, v_w_br_b, v_w_br_c, v_w_out, v_norm2_w, v_ffn_w_up, v_ffn_conv_w, v_ffn_conv_b, v_ffn_w_down, v_final_norm_w):
    given = dict(locals())
    w = {n: given[n] for n in WEIGHTS}
    m = {n: given["m_" + n] for n in WEIGHTS}
    v = {n: given["v_" + n] for n in WEIGHTS}
    me = _my_index()
    bsz = c.shape[0]
    ncol = 6 * D_MODEL // N_DEV

    shards = [w[n].astype(BF16) if n in MATMUL_WEIGHTS else w[n] for n, _, _ in SPLIT] + [c]
    gathered = _gather_two_level(shards, "gather_weights")
    wf = {n: _join_cols(g, n == "w_in", f"join_{n}") if n in UNALIGNED else _join_blocks(g, shape, axis)
          for (n, shape, axis), g in zip(SPLIT, gathered)}
    c_all = gathered[-1].reshape(N_DEV * bsz, D_MODEL)

    b_cols = lax.dynamic_slice_in_dim(b_ada, me * ncol, ncol, axis=1)[:, None]
    mod_cols = _ada_fwd(c_all, w_ada, b_cols)
    send = mod_cols.reshape(DEPTH, N_DEV, bsz, ncol).transpose(1, 0, 2, 3)
    got = _all_to_all([send], "scatter_mod")[0]
    mod = got.transpose(1, 2, 0, 3).reshape(DEPTH, bsz, 6 * D_MODEL)

    loss, dx, grads, dmod = _local_step(x, mod, wf, w, loss_target)

    send = dmod.reshape(DEPTH, bsz, N_DEV, ncol).transpose(2, 0, 1, 3)
    got_dmod = _all_to_all([send], "scatter_dmod")[0]
    dmod_all = got_dmod.transpose(1, 0, 2, 3).reshape(DEPTH, N_DEV * bsz, ncol)
    g_w_ada, g_b_cols = _ada_bwd(c_all.T, dmod_all)

    core = lax.axis_index("c")
    by_core = []
    for n, shape, axis in SPLIT:
        if n in UNALIGNED:
            by_core.append(_split_cols(grads[n], n == "w_in", shape[axis] // N_DEV, f"split_{n}"))
            continue
        parts = _split_blocks(grads[n], shape, axis).astype(BF16)
        parts = parts.reshape((N_DEV // 2, 2, -1, parts.shape[-1]))
        by_core.append(jnp.swapaxes(parts, 0, 1))
    from_sibling = _sibling_swap(by_core, "swap_grads")
    sums = [_pair_sum(lax.dynamic_index_in_dim(mine, core, 0, keepdims=False), theirs, f"pair_sum_{n}")
            for (n, _, _), mine, theirs in zip(SPLIT, by_core, from_sibling)]
    got = _chip_scatter(sums, "scatter_grads")
    grads["b_ada"] = lax.dynamic_update_slice_in_dim(jnp.zeros_like(b_ada), g_b_cols[:, 0], me * ncol, axis=1)

    out = {}
    slots = [(n, g8) for (n, _, _), g8 in zip(SPLIT, got)] + [("w_ada", g_w_ada[None])]
    for n, gs in slots:
        out[n] = _sum_adamw(gs.reshape((gs.shape[0],) + w[n].shape), w[n], m[n], v[n], f"adamw_{n}")
    r8 = _gather_two_level([_pack_repl(grads)], "gather_small_grads")[0]
    res = _sum_adamw(r8[:, None], _pack_repl(w)[None], _pack_repl(m)[None], _pack_repl(v)[None], "adamw_repl")
    repl_out = [_unpack_repl(o[0]) for o in res]
    pick = lambda i, n: out[n][i] if n in out else repl_out[i][n]
    loss = lax.psum(loss, ("x", "y", "c"))
    return (loss, dx, *[pick(i, n) for i in range(4) for n in WEIGHTS])
```
